```python
import math
import jax, jax.numpy as jnp
from jax import lax
import numpy as np

D_MODEL = 1024
BATCH = 8
SEQ = 4096
DEPTH = 4

ROPE_THETA = 10000.0
QBLK = 128
MLA_HEADS = 8
MLA_NOPE = 64
MLA_ROPE = 32
MLA_V = 64
MLA_Q_RANK = 256
MLA_KV_RANK = 128
FOX_HEADS = 8
FOX_DIM = 64
SWA_HEADS = 16
SWA_KV_HEADS = 2
SWA_DIM = 64
WINDOW = 128

RMS_EPS = 1e-6
LN_EPS = 1e-5
ALPHA = (2 * DEPTH) ** 0.25
BETA = (8 * DEPTH) ** -0.25

EVEN_WIDTH = MLA_HEADS * MLA_V + FOX_HEADS * FOX_DIM
ODD_WIDTH = SWA_HEADS * SWA_DIM
EVEN_SIZES = (MLA_Q_RANK, MLA_KV_RANK, MLA_ROPE, FOX_HEADS * FOX_DIM,
              FOX_HEADS * FOX_DIM, FOX_HEADS * FOX_DIM, FOX_HEADS, EVEN_WIDTH)
EVEN_IN = sum(EVEN_SIZES)
EVEN_V_START = MLA_Q_RANK + MLA_KV_RANK + MLA_ROPE + 2 * FOX_HEADS * FOX_DIM
ODD_SIZES = (SWA_HEADS * SWA_DIM, SWA_KV_HEADS * SWA_DIM, SWA_KV_HEADS * SWA_DIM, ODD_WIDTH)
ODD_IN = sum(ODD_SIZES)
ODD_V_START = SWA_HEADS * SWA_DIM + SWA_KV_HEADS * SWA_DIM
N_EVEN = (DEPTH + 1) // 2
N_ODD = DEPTH // 2

kernel_name = "hybrid_mla_fox_swa_deepnorm"


def _split(h, sizes):
    cuts = [int(c) for c in np.cumsum(sizes)[:-1]]
    return jnp.split(h, cuts, axis=-1)


def _heads(t, n_heads):
    b, s, _ = t.shape
    return t.reshape(b, s, n_heads, -1).transpose(0, 2, 1, 3)


def _merge(t):
    b, h, s, d = t.shape
    return t.transpose(0, 2, 1, 3).reshape(b, s, h * d)


def rms_norm(t, g):
    tf = t.astype(jnp.float32)
    tf = tf * lax.rsqrt(jnp.mean(tf * tf, axis=-1, keepdims=True) + RMS_EPS)
    return (tf * g.astype(jnp.float32)).astype(t.dtype)


def layer_norm(t, g, b):
    tf = t.astype(jnp.float32)
    mu = jnp.mean(tf, axis=-1, keepdims=True)
    var = jnp.mean(jnp.square(tf - mu), axis=-1, keepdims=True)
    y = (tf - mu) * lax.rsqrt(var + LN_EPS) * g.astype(jnp.float32) + b.astype(jnp.float32)
    return y.astype(t.dtype)


def rope(t, pos):
    d = t.shape[-1]
    inv = ROPE_THETA ** (-jnp.arange(0, d, 2, dtype=jnp.float32) / d)
    ang = pos.astype(jnp.float32)[:, None] * inv[None, :]
    cos, sin = jnp.cos(ang), jnp.sin(ang)
    t1, t2 = jnp.split(t.astype(jnp.float32), 2, axis=-1)
    return jnp.concatenate([t1 * cos - t2 * sin, t2 * cos + t1 * sin], axis=-1).astype(t.dtype)


def causal_block_attention(q, k, v, scale, cum_logf=None):
    b, h, s, dk = q.shape
    dv = v.shape[-1]
    nb = s // QBLK
    qb = q.reshape(b, h, nb, QBLK, dk).transpose(2, 0, 1, 3, 4)
    kpos = jnp.arange(s)
    idx = jnp.arange(nb)

    def block(args):
        if cum_logf is None:
            q_i, i = args
        else:
            q_i, c_i, i = args
        sc = jnp.einsum('bhqd,bhkd->bhqk', q_i, k,
                        preferred_element_type=jnp.float32) * scale
        if cum_logf is not None:
            sc = sc + c_i[..., :, None] - cum_logf[..., None, :]
        qpos = i * QBLK + jnp.arange(QBLK)
        mask = kpos[None, :] <= qpos[:, None]
        sc = jnp.where(mask, sc, -jnp.inf)
        p = jax.nn.softmax(sc, axis=-1)
        return jnp.einsum('bhqk,bhkd->bhqd', p.astype(v.dtype), v)

    if cum_logf is None:
        xs = (qb, idx)
    else:
        cb = cum_logf.reshape(b, h, nb, QBLK).transpose(2, 0, 1, 3)
        xs = (qb, cb, idx)
    out = lax.map(block, xs)
    return out.transpose(1, 2, 0, 3, 4).reshape(b, h, s, dv)


def sliding_window_sink_attention(q, k, v, sinks):
    b, h, s, d = q.shape
    hkv = k.shape[1]
    g = h // hkv
    nb = s // QBLK
    scale = d ** -0.5
    qb = q.reshape(b, hkv, g, nb, QBLK, d).transpose(3, 0, 1, 2, 4, 5)
    pad = ((0, 0), (0, 0), (QBLK, 0), (0, 0))
    kp = jnp.pad(k, pad)
    vp = jnp.pad(v, pad)
    sink = sinks.astype(jnp.float32).reshape(1, hkv, g, 1, 1)

    def block(args):
        q_i, i = args
        start = i * QBLK
        k_i = lax.dynamic_slice_in_dim(kp, start, 2 * QBLK, axis=2)
        v_i = lax.dynamic_slice_in_dim(vp, start, 2 * QBLK, axis=2)
        sc = jnp.einsum('bkgqd,bkjd->bkgqj', q_i, k_i,
                        preferred_element_type=jnp.float32) * scale
        qpos = start + jnp.arange(QBLK)
        kpos = start - QBLK + jnp.arange(2 * QBLK)
        diff = qpos[:, None] - kpos[None, :]
        mask = (diff >= 0) & (diff < WINDOW) & (kpos[None, :] >= 0)
        sc = jnp.where(mask, sc, -jnp.inf)
        logits = jnp.concatenate(
            [sc, jnp.broadcast_to(sink, sc.shape[:-1] + (1,))], axis=-1)
        p = jax.nn.softmax(logits, axis=-1)[..., :-1]
        return jnp.einsum('bkgqj,bkjd->bkgqd', p.astype(v.dtype), v_i)

    out = lax.map(block, (qb, jnp.arange(nb)))
    return out.transpose(1, 2, 3, 0, 4, 5).reshape(b, h, s, d)


def even_mixer(x, w_in, q_norm, w_uq, kv_norm, w_ukv, b_f, w_out, pos):
    b, s, _ = x.shape
    h = x @ w_in
    cq, ckv, k_pe, fq, fk, fv, f_logit, gate = _split(h, EVEN_SIZES)
    q = (rms_norm(cq, q_norm) @ w_uq).reshape(b, s, MLA_HEADS, MLA_NOPE + MLA_ROPE)
    q = q.transpose(0, 2, 1, 3)
    q_nope, q_pe = q[..., :MLA_NOPE], rope(q[..., MLA_NOPE:], pos)
    kv = (rms_norm(ckv, kv_norm) @ w_ukv).reshape(b, s, MLA_HEADS, MLA_NOPE + MLA_V)
    kv = kv.transpose(0, 2, 1, 3)
    k_nope, v_mla = kv[..., :MLA_NOPE], kv[..., MLA_NOPE:]
    k_pe = rope(k_pe[:, None], pos)
    q_mla = jnp.concatenate([q_nope, q_pe], axis=-1)
    k_mla = jnp.concatenate(
        [k_nope, jnp.broadcast_to(k_pe, (b, MLA_HEADS, s, MLA_ROPE))], axis=-1)
    o_mla = causal_block_attention(q_mla, k_mla, v_mla, (MLA_NOPE + MLA_ROPE) ** -0.5)
    log_f = jax.nn.log_sigmoid((f_logit + b_f).astype(jnp.float32))
    cum = lax.cumsum(log_f, axis=1).transpose(0, 2, 1)
    o_fox = causal_block_attention(_heads(fq, FOX_HEADS), _heads(fk, FOX_HEADS),
                                   _heads(fv, FOX_HEADS), FOX_DIM ** -0.5, cum)
    o = jnp.concatenate([_merge(o_mla), _merge(o_fox)], axis=-1)
    return (o * jax.nn.silu(gate)) @ w_out


def odd_mixer(x, w_in, sinks, w_out, pos):
    h = x @ w_in
    q, k, v, gate = _split(h, ODD_SIZES)
    q = rope(_heads(q, SWA_HEADS), pos)
    k = rope(_heads(k, SWA_KV_HEADS), pos)
    v = _heads(v, SWA_KV_HEADS)
    o = _merge(sliding_window_sink_attention(q, k, v, sinks))
    return (o * jax.nn.silu(gate)) @ w_out


def _fwd_setup_inputs(seed: int = 0) -> dict:
    key = jax.random.key(seed)
    ks = jax.random.split(key, 16)
    nrm = jax.random.normal
    even_in_scale = jnp.ones((EVEN_IN,), jnp.float32).at[
        EVEN_V_START:EVEN_V_START + FOX_HEADS * FOX_DIM].set(BETA)
    ukv_scale = jnp.tile(jnp.concatenate([jnp.ones((MLA_NOPE,), jnp.float32),
                                          jnp.full((MLA_V,), BETA, jnp.float32)]), MLA_HEADS)
    odd_in_scale = jnp.ones((ODD_IN,), jnp.float32).at[
        ODD_V_START:ODD_V_START + SWA_KV_HEADS * SWA_DIM].set(BETA)
    return {
        "x": nrm(ks[0], (BATCH, SEQ, D_MODEL), jnp.float32),
        "even_w_in": nrm(ks[1], (N_EVEN, D_MODEL, EVEN_IN), jnp.float32) * D_MODEL ** -0.5 * even_in_scale,
        "even_q_norm": 1.0 + 0.02 * nrm(ks[2], (N_EVEN, MLA_Q_RANK), jnp.float32),
        "even_w_uq": nrm(ks[3], (N_EVEN, MLA_Q_RANK, MLA_HEADS * (MLA_NOPE + MLA_ROPE)), jnp.float32) * MLA_Q_RANK ** -0.5,
        "even_kv_norm": 1.0 + 0.02 * nrm(ks[4], (N_EVEN, MLA_KV_RANK), jnp.float32),
        "even_w_ukv": nrm(ks[5], (N_EVEN, MLA_KV_RANK, MLA_HEADS * (MLA_NOPE + MLA_V)), jnp.float32) * MLA_KV_RANK ** -0.5 * ukv_scale,
        "even_b_f": jax.random.uniform(ks[6], (N_EVEN, FOX_HEADS), jnp.float32, 1.0, 6.0),
        "even_w_out": nrm(ks[7], (N_EVEN, EVEN_WIDTH, D_MODEL), jnp.float32) * EVEN_WIDTH ** -0.5 * BETA,
        "even_ln_g": 1.0 + 0.02 * nrm(ks[8], (N_EVEN, D_MODEL), jnp.float32),
        "even_ln_b": 0.02 * nrm(ks[9], (N_EVEN, D_MODEL), jnp.float32),
        "odd_w_in": nrm(ks[10], (N_ODD, D_MODEL, ODD_IN), jnp.float32) * D_MODEL ** -0.5 * odd_in_scale,
        "odd_sinks": 0.5 * nrm(ks[11], (N_ODD, SWA_HEADS), jnp.float32),
        "odd_w_out": nrm(ks[12], (N_ODD, ODD_WIDTH, D_MODEL), jnp.float32) * ODD_WIDTH ** -0.5 * BETA,
        "odd_ln_g": 1.0 + 0.02 * nrm(ks[13], (N_ODD, D_MODEL), jnp.float32),
        "odd_ln_b": 0.02 * nrm(ks[14], (N_ODD, D_MODEL), jnp.float32),
    }


def _fwd_reference(x, even_w_in, even_q_norm, even_w_uq, even_kv_norm, even_w_ukv, even_b_f,
              even_w_out, even_ln_g, even_ln_b, odd_w_in, odd_sinks, odd_w_out,
              odd_ln_g, odd_ln_b):
    pos = jnp.arange(x.shape[1])
    for layer in range(DEPTH):
        j = layer // 2
        if layer % 2 == 0:
            y = even_mixer(x, even_w_in[j], even_q_norm[j], even_w_uq[j], even_kv_norm[j],
                           even_w_ukv[j], even_b_f[j], even_w_out[j], pos)
            x = layer_norm(ALPHA * x + y, even_ln_g[j], even_ln_b[j])
        else:
            y = odd_mixer(x, odd_w_in[j], odd_sinks[j], odd_w_out[j], pos)
            x = layer_norm(ALPHA * x + y, odd_ln_g[j], odd_ln_b[j])
    return x


import jax as _jax
import jax.numpy as _jnp

TWIN_FORMAT = 'train_step'
FWD_PARAMS = ['x', 'even_w_in', 'even_q_norm', 'even_w_uq', 'even_kv_norm', 'even_w_ukv', 'even_b_f', 'even_w_out', 'even_ln_g', 'even_ln_b', 'odd_w_in', 'odd_sinks', 'odd_w_out', 'odd_ln_g', 'odd_ln_b']
TWIN_WEIGHTS = ['even_w_in', 'even_q_norm', 'even_w_uq', 'even_kv_norm', 'even_w_ukv', 'even_b_f', 'even_w_out', 'even_ln_g', 'even_ln_b', 'odd_w_in', 'odd_sinks', 'odd_w_out', 'odd_ln_g', 'odd_ln_b']
TWIN_DIFF_INPUT = 'x'
TWIN_INPUTS = ['x', 'even_w_in', 'even_q_norm', 'even_w_uq', 'even_kv_norm', 'even_w_ukv', 'even_b_f', 'even_w_out', 'even_ln_g', 'even_ln_b', 'odd_w_in', 'odd_sinks', 'odd_w_out', 'odd_ln_g', 'odd_ln_b', 'loss_target', 'm_even_w_in', 'm_even_q_norm', 'm_even_w_uq', 'm_even_kv_norm', 'm_even_w_ukv', 'm_even_b_f', 'm_even_w_out', 'm_even_ln_g', 'm_even_ln_b', 'm_odd_w_in', 'm_odd_sinks', 'm_odd_w_out', 'm_odd_ln_g', 'm_odd_ln_b', 'v_even_w_in', 'v_even_q_norm', 'v_even_w_uq', 'v_even_kv_norm', 'v_even_w_ukv', 'v_even_b_f', 'v_even_w_out', 'v_even_ln_g', 'v_even_ln_b', 'v_odd_w_in', 'v_odd_sinks', 'v_odd_w_out', 'v_odd_ln_g', 'v_odd_ln_b']
TWIN_OUTPUTS = ['loss', 'grad_x', 'grad_even_w_in', 'grad_even_q_norm', 'grad_even_w_uq', 'grad_even_kv_norm', 'grad_even_w_ukv', 'grad_even_b_f', 'grad_even_w_out', 'grad_even_ln_g', 'grad_even_ln_b', 'grad_odd_w_in', 'grad_odd_sinks', 'grad_odd_w_out', 'grad_odd_ln_g', 'grad_odd_ln_b', 'delta_even_w_in', 'delta_even_q_norm', 'delta_even_w_uq', 'delta_even_kv_norm', 'delta_even_w_ukv', 'delta_even_b_f', 'delta_even_w_out', 'delta_even_ln_g', 'delta_even_ln_b', 'delta_odd_w_in', 'delta_odd_sinks', 'delta_odd_w_out', 'delta_odd_ln_g', 'delta_odd_ln_b', 'new_m_even_w_in', 'new_m_even_q_norm', 'new_m_even_w_uq', 'new_m_even_kv_norm', 'new_m_even_w_ukv', 'new_m_even_b_f', 'new_m_even_w_out', 'new_m_even_ln_g', 'new_m_even_ln_b', 'new_m_odd_w_in', 'new_m_odd_sinks', 'new_m_odd_w_out', 'new_m_odd_ln_g', 'new_m_odd_ln_b', 'new_v_even_w_in', 'new_v_even_q_norm', 'new_v_even_w_uq', 'new_v_even_kv_norm', 'new_v_even_w_ukv', 'new_v_even_b_f', 'new_v_even_w_out', 'new_v_even_ln_g', 'new_v_even_ln_b', 'new_v_odd_w_in', 'new_v_odd_sinks', 'new_v_odd_w_out', 'new_v_odd_ln_g', 'new_v_odd_ln_b']
TWIN_LEAF_KINDS = {'loss': 'loss', 'grad_x': 'grad_x', 'grad_even_w_in': 'grad_w', 'grad_even_q_norm': 'grad_w', 'grad_even_w_uq': 'grad_w', 'grad_even_kv_norm': 'grad_w', 'grad_even_w_ukv': 'grad_w', 'grad_even_b_f': 'grad_w', 'grad_even_w_out': 'grad_w', 'grad_even_ln_g': 'grad_w', 'grad_even_ln_b': 'grad_w', 'grad_odd_w_in': 'grad_w', 'grad_odd_sinks': 'grad_w', 'grad_odd_w_out': 'grad_w', 'grad_odd_ln_g': 'grad_w', 'grad_odd_ln_b': 'grad_w', 'delta_even_w_in': 'delta_w', 'delta_even_q_norm': 'delta_w', 'delta_even_w_uq': 'delta_w', 'delta_even_kv_norm': 'delta_w', 'delta_even_w_ukv': 'delta_w', 'delta_even_b_f': 'delta_w', 'delta_even_w_out': 'delta_w', 'delta_even_ln_g': 'delta_w', 'delta_even_ln_b': 'delta_w', 'delta_odd_w_in': 'delta_w', 'delta_odd_sinks': 'delta_w', 'delta_odd_w_out': 'delta_w', 'delta_odd_ln_g': 'delta_w', 'delta_odd_ln_b': 'delta_w', 'new_m_even_w_in': 'new_m', 'new_m_even_q_norm': 'new_m', 'new_m_even_w_uq': 'new_m', 'new_m_even_kv_norm': 'new_m', 'new_m_even_w_ukv': 'new_m', 'new_m_even_b_f': 'new_m', 'new_m_even_w_out': 'new_m', 'new_m_even_ln_g': 'new_m', 'new_m_even_ln_b': 'new_m', 'new_m_odd_w_in': 'new_m', 'new_m_odd_sinks': 'new_m', 'new_m_odd_w_out': 'new_m', 'new_m_odd_ln_g': 'new_m', 'new_m_odd_ln_b': 'new_m', 'new_v_even_w_in': 'new_v', 'new_v_even_q_norm': 'new_v', 'new_v_even_w_uq': 'new_v', 'new_v_even_kv_norm': 'new_v', 'new_v_even_w_ukv': 'new_v', 'new_v_even_b_f': 'new_v', 'new_v_even_w_out': 'new_v', 'new_v_even_ln_g': 'new_v', 'new_v_even_ln_b': 'new_v', 'new_v_odd_w_in': 'new_v', 'new_v_odd_sinks': 'new_v', 'new_v_odd_w_out': 'new_v', 'new_v_odd_ln_g': 'new_v', 'new_v_odd_ln_b': 'new_v'}


def _forward(args):
    return _fwd_reference(*[args[k] for k in FWD_PARAMS])


def _output_shape():
    def fwd():
        inp = _fwd_setup_inputs(0)
        return _fwd_reference(*[inp[k] for k in FWD_PARAMS])
    out = _jax.eval_shape(fwd)
    return out.shape, out.dtype

N_MICROBATCH = 1
ADAM_LR = 0.001
ADAM_B1 = 0.9
ADAM_B2 = 0.999
ADAM_EPS = 1e-08
ADAM_WD = 0.01
ADAM_STEP = 10
PER_EXAMPLE_BATCH_AXIS = {'x': 0, 'loss_target': 0}
SHARED_INPUTS = []
_WEIGHT_DTYPES = {'even_w_in': _jnp.float32, 'even_q_norm': _jnp.float32, 'even_w_uq': _jnp.float32, 'even_kv_norm': _jnp.float32, 'even_w_ukv': _jnp.float32, 'even_b_f': _jnp.float32, 'even_w_out': _jnp.float32, 'even_ln_g': _jnp.float32, 'even_ln_b': _jnp.float32, 'odd_w_in': _jnp.float32, 'odd_sinks': _jnp.float32, 'odd_w_out': _jnp.float32, 'odd_ln_g': _jnp.float32, 'odd_ln_b': _jnp.float32}
MOMENT_SCALE = {'even_w_in': 4.376536e-03, 'even_q_norm': 2.877972e-03, 'even_w_uq': 1.701980e-03, 'even_kv_norm': 6.605346e-03, 'even_w_ukv': 4.365892e-03, 'even_b_f': 3.718849e-02, 'even_w_out': 7.060874e-03, 'even_ln_g': 1.217493e+00, 'even_ln_b': 6.005375e-01, 'odd_w_in': 4.442344e-03, 'odd_sinks': 1.876295e-03, 'odd_w_out': 5.474852e-03, 'odd_ln_g': 2.270102e+01, 'odd_ln_b': 9.214027e-01}


def _to_microbatches(a, axis):
    t = _jnp.moveaxis(a, axis, 0)
    t = t.reshape((N_MICROBATCH, t.shape[0] // N_MICROBATCH) + t.shape[1:])
    return _jnp.moveaxis(t, 1, axis + 1)


def setup_inputs(seed: int = 0) -> dict:
    inp = _fwd_setup_inputs(seed)
    key = _jax.random.fold_in(_jax.random.key(seed), 7919)
    shape, _ = _output_shape()
    out = dict(inp)
    out["loss_target"] = _jax.random.normal(_jax.random.fold_in(key, 0), shape, _jnp.float32)
    for i, name in enumerate(TWIN_WEIGHTS):
        w = inp[name].astype(_jnp.float32)
        if MOMENT_SCALE is None:
            s = _jnp.sqrt(_jnp.mean(_jnp.square(w)) + 1e-30)
        else:
            s = MOMENT_SCALE[name]
        km, kv = _jax.random.split(_jax.random.fold_in(key, i + 1))
        out[name] = w
        out["m_" + name] = s * _jax.random.normal(km, w.shape, _jnp.float32)
        out["v_" + name] = (s * s) * _jax.random.uniform(kv, w.shape, _jnp.float32, 0.5, 1.5)
    if N_MICROBATCH > 1:
        for name, axis in PER_EXAMPLE_BATCH_AXIS.items():
            out[name] = _to_microbatches(out[name], axis)
    return {'x': out['x'], 'even_w_in': out['even_w_in'], 'even_q_norm': out['even_q_norm'], 'even_w_uq': out['even_w_uq'], 'even_kv_norm': out['even_kv_norm'], 'even_w_ukv': out['even_w_ukv'], 'even_b_f': out['even_b_f'], 'even_w_out': out['even_w_out'], 'even_ln_g': out['even_ln_g'], 'even_ln_b': out['even_ln_b'], 'odd_w_in': out['odd_w_in'], 'odd_sinks': out['odd_sinks'], 'odd_w_out': out['odd_w_out'], 'odd_ln_g': out['odd_ln_g'], 'odd_ln_b': out['odd_ln_b'], 'loss_target': out['loss_target'], 'm_even_w_in': out['m_even_w_in'], 'm_even_q_norm': out['m_even_q_norm'], 'm_even_w_uq': out['m_even_w_uq'], 'm_even_kv_norm': out['m_even_kv_norm'], 'm_even_w_ukv': out['m_even_w_ukv'], 'm_even_b_f': out['m_even_b_f'], 'm_even_w_out': out['m_even_w_out'], 'm_even_ln_g': out['m_even_ln_g'], 'm_even_ln_b': out['m_even_ln_b'], 'm_odd_w_in': out['m_odd_w_in'], 'm_odd_sinks': out['m_odd_sinks'], 'm_odd_w_out': out['m_odd_w_out'], 'm_odd_ln_g': out['m_odd_ln_g'], 'm_odd_ln_b': out['m_odd_ln_b'], 'v_even_w_in': out['v_even_w_in'], 'v_even_q_norm': out['v_even_q_norm'], 'v_even_w_uq': out['v_even_w_uq'], 'v_even_kv_norm': out['v_even_kv_norm'], 'v_even_w_ukv': out['v_even_w_ukv'], 'v_even_b_f': out['v_even_b_f'], 'v_even_w_out': out['v_even_w_out'], 'v_even_ln_g': out['v_even_ln_g'], 'v_even_ln_b': out['v_even_ln_b'], 'v_odd_w_in': out['v_odd_w_in'], 'v_odd_sinks': out['v_odd_sinks'], 'v_odd_w_out': out['v_odd_w_out'], 'v_odd_ln_g': out['v_odd_ln_g'], 'v_odd_ln_b': out['v_odd_ln_b']}


def _loss(weights, diff, rest, loss_target):
    with _jax.named_scope("forward"):
        args = {**rest, TWIN_DIFF_INPUT: diff, **{k: w.astype(_WEIGHT_DTYPES[k]) for k, w in weights.items()}}
        y = _forward(args)
    with _jax.named_scope("loss_head"):
        err = _jnp.square(y.astype(_jnp.float32) - loss_target)
        return 0.5 * _jnp.sum(_jnp.mean(err, axis=-1)) if err.ndim else 0.5 * err


def _adamw(w, g, m, v):
    m = ADAM_B1 * m + (1.0 - ADAM_B1) * g
    v = ADAM_B2 * v + (1.0 - ADAM_B2) * _jnp.square(g)
    m_hat = m / (1.0 - ADAM_B1 ** ADAM_STEP)
    v_hat = v / (1.0 - ADAM_B2 ** ADAM_STEP)
    delta = -ADAM_LR * (m_hat / (_jnp.sqrt(v_hat) + ADAM_EPS) + ADAM_WD * w)
    return delta, m, v


def reference(x, even_w_in, even_q_norm, even_w_uq, even_kv_norm, even_w_ukv, even_b_f, even_w_out, even_ln_g, even_ln_b, odd_w_in, odd_sinks, odd_w_out, odd_ln_g, odd_ln_b, loss_target, m_even_w_in, m_even_q_norm, m_even_w_uq, m_even_kv_norm, m_even_w_ukv, m_even_b_f, m_even_w_out, m_even_ln_g, m_even_ln_b, m_odd_w_in, m_odd_sinks, m_odd_w_out, m_odd_ln_g, m_odd_ln_b, v_even_w_in, v_even_q_norm, v_even_w_uq, v_even_kv_norm, v_even_w_ukv, v_even_b_f, v_even_w_out, v_even_ln_g, v_even_ln_b, v_odd_w_in, v_odd_sinks, v_odd_w_out, v_odd_ln_g, v_odd_ln_b):
    given = dict(x=x, even_w_in=even_w_in, even_q_norm=even_q_norm, even_w_uq=even_w_uq, even_kv_norm=even_kv_norm, even_w_ukv=even_w_ukv, even_b_f=even_b_f, even_w_out=even_w_out, even_ln_g=even_ln_g, even_ln_b=even_ln_b, odd_w_in=odd_w_in, odd_sinks=odd_sinks, odd_w_out=odd_w_out, odd_ln_g=odd_ln_g, odd_ln_b=odd_ln_b, loss_target=loss_target, m_even_w_in=m_even_w_in, m_even_q_norm=m_even_q_norm, m_even_w_uq=m_even_w_uq, m_even_kv_norm=m_even_kv_norm, m_even_w_ukv=m_even_w_ukv, m_even_b_f=m_even_b_f, m_even_w_out=m_even_w_out, m_even_ln_g=m_even_ln_g, m_even_ln_b=m_even_ln_b, m_odd_w_in=m_odd_w_in, m_odd_sinks=m_odd_sinks, m_odd_w_out=m_odd_w_out, m_odd_ln_g=m_odd_ln_g, m_odd_ln_b=m_odd_ln_b, v_even_w_in=v_even_w_in, v_even_q_norm=v_even_q_norm, v_even_w_uq=v_even_w_uq, v_even_kv_norm=v_even_kv_norm, v_even_w_ukv=v_even_w_ukv, v_even_b_f=v_even_b_f, v_even_w_out=v_even_w_out, v_even_ln_g=v_even_ln_g, v_even_ln_b=v_even_ln_b, v_odd_w_in=v_odd_w_in, v_odd_sinks=v_odd_sinks, v_odd_w_out=v_odd_w_out, v_odd_ln_g=v_odd_ln_g, v_odd_ln_b=v_odd_ln_b)
    weights = {n: given[n] for n in TWIN_WEIGHTS}
    shared = {n: given[n] for n in SHARED_INPUTS}
    per_example = {n: given[n] for n in ['x']}
    grad_fn = _jax.value_and_grad(_loss, argnums=(0, 1))

    def one_microbatch(ex, loss_target):
        ex = dict(ex)
        diff = ex.pop(TWIN_DIFF_INPUT)
        return grad_fn(weights, diff, {**shared, **ex}, loss_target)

    if N_MICROBATCH == 1:
        loss, (grad_w, grad_x) = one_microbatch(per_example, given["loss_target"])
    else:
        def body(carry, xs):
            loss_sum, grad_sum = carry
            l_k, (gw_k, gx_k) = one_microbatch(xs[0], xs[1])
            with _jax.named_scope("update"):
                return (loss_sum + l_k, _jax.tree.map(_jnp.add, grad_sum, gw_k)), gx_k

        init = (_jnp.zeros((), _jnp.float32), _jax.tree.map(_jnp.zeros_like, weights))
        (loss, grad_w), grad_x = _jax.lax.scan(body, init, (per_example, given["loss_target"]))
    with _jax.named_scope("update"):
        delta_w, new_m, new_v = {}, {}, {}
        for n in TWIN_WEIGHTS:
            delta_w[n], new_m[n], new_v[n] = _adamw(weights[n], grad_w[n], given["m_" + n], given["v_" + n])
    return (loss, grad_x, *[grad_w[n] for n in TWIN_WEIGHTS], *[delta_w[n] for n in TWIN_WEIGHTS],
            *[new_m[n] for n in TWIN_WEIGHTS], *[new_v[n] for n in TWIN_WEIGHTS])
```

```python
import functools

import jax
import jax.numpy as jnp
import numpy as np
from jax import lax
from jax.experimental import pallas as pl
from jax.experimental.pallas import tpu as pltpu

F32 = jnp.float32
BF16 = jnp.bfloat16
MESH = pl.DeviceIdType.MESH

D_MODEL = 1024
DEPTH = 4
ROPE_THETA = 10000.0
MLA_HEADS, MLA_NOPE, MLA_ROPE, MLA_V = 8, 64, 32, 64
MLA_Q_RANK, MLA_KV_RANK = 256, 128
FOX_HEADS, FOX_DIM = 8, 64
SWA_HEADS, SWA_KV_HEADS, SWA_DIM, WINDOW = 16, 2, 64, 128
SWA_GROUP = SWA_HEADS // SWA_KV_HEADS
RMS_EPS = 1e-6
LN_EPS = 1e-5
ALPHA = (2 * DEPTH) ** 0.25
EVEN_IN = 2984
ODD_IN = 2304
ADAM_LR, ADAM_B1, ADAM_B2, ADAM_EPS, ADAM_WD, ADAM_STEP = 0.001, 0.9, 0.999, 1e-08, 0.01, 10

LANES = 128
HEAD_PAD = 128
N_CHIPS = 4
N_DEV = 8
E_GATE, E_FQ, E_FK, E_FV, E_SMALL = 0, 1024, 1536, 2048, 2560
E_PAD_IN = 3072
KPE_LANE = 64
FL_LANE = 96
O_GATE, O_Q, O_K, O_V = 0, 1024, 2048, 2176

_ARB = "arbitrary"
_PAR = "parallel"


def _cparams(sem):
    return pltpu.CompilerParams(dimension_semantics=sem)


def _pick(n, cands):
    for c in cands:
        if n % c == 0:
            return c
    return n


def _mm(a, b, *, out_dtype, name, res=None, res_scale=1.0):
    m, k = a.shape
    _, n = b.shape
    tm = _pick(m, (512, 256, 128))
    tn = _pick(n, (1024, 768, 512, 384, 256, 128))
    tk = _pick(k, (1024, 512, 256, 128))
    nk = k // tk

    def body(*refs):
        if res is None:
            a_ref, b_ref, o_ref, acc_ref = refs
        else:
            a_ref, b_ref, r_ref, o_ref, acc_ref = refs
        kk = pl.program_id(2)

        @pl.when(kk == 0)
        def _():
            acc_ref[...] = jnp.zeros_like(acc_ref)

        acc_ref[...] += jnp.dot(a_ref[...].astype(BF16), b_ref[...].astype(BF16), preferred_element_type=F32)

        @pl.when(kk == nk - 1)
        def _():
            r = acc_ref[...]
            if res is not None:
                r = r + res_scale * r_ref[...]
            o_ref[...] = r.astype(o_ref.dtype)

    in_specs = [pl.BlockSpec((tm, tk), lambda i, j, kk: (i, kk)), pl.BlockSpec((tk, tn), lambda i, j, kk: (kk, j))]
    args = [a, b]
    if res is not None:
        in_specs.append(pl.BlockSpec((tm, tn), lambda i, j, kk: (i, j)))
        args.append(res)
    return pl.pallas_call(
        body, name=name, grid=(m // tm, n // tn, nk), in_specs=in_specs,
        out_specs=pl.BlockSpec((tm, tn), lambda i, j, kk: (i, j)),
        out_shape=jax.ShapeDtypeStruct((m, n), out_dtype),
        scratch_shapes=[pltpu.VMEM((tm, tn), F32)],
        compiler_params=_cparams((_PAR, _PAR, _ARB)),
    )(*args)


def _rope_tile(t, cos, sin, half):
    w = t.shape[-1]
    lane = lax.broadcasted_iota(jnp.int32, t.shape, 1)
    first = (lane % (2 * half)) < half
    sw = jnp.where(first, pltpu.roll(t, w - half, 1), pltpu.roll(t, half, 1))
    return t * cos + sw * sin


def _sigmoid(x):
    return 1.0 / (1.0 + jnp.exp(-x))


def _lane_mask(shape, lo, hi):
    lane = lax.broadcasted_iota(jnp.int32, shape, 1)
    return (lane >= lo) & (lane < hi)


def _rms(x, g):
    r = lax.rsqrt(jnp.mean(x * x, axis=-1, keepdims=True) + RMS_EPS)
    u = x * r
    return u, r, u * g


def _rms_bwd(dy, u, r, g):
    dyg = dy * g
    dx = r * (dyg - u * jnp.mean(dyg * u, axis=-1, keepdims=True))
    return dx, jnp.sum(dy * u, axis=0, keepdims=True)


def _even_mid_fwd(h, qg, kg, bf_tile, w_uq, w_ukv, cos, sin, *, name):
    s = h.shape[0]
    tb = _pick(s, (512, 256, 128))
    nq = MLA_HEADS * HEAD_PAD

    def body(h_ref, qg_ref, kg_ref, bf_ref, wuq_ref, wukv_ref, cos_ref, sin_ref,
             q_ref, k_ref, v_ref, qn_ref, kvn_ref, lf_ref):
        hb = h_ref[...]
        cq, ckv, misc = hb[:, :MLA_Q_RANK], hb[:, MLA_Q_RANK:MLA_Q_RANK + MLA_KV_RANK], hb[:, MLA_Q_RANK + MLA_KV_RANK:]
        cs, sn = cos_ref[...], sin_ref[...]
        _, _, qn = _rms(cq, qg_ref[...])
        qn = qn.astype(BF16)
        qn_ref[...] = qn
        q = jnp.dot(qn, wuq_ref[...], preferred_element_type=F32)
        _, _, kvn = _rms(ckv, kg_ref[...])
        kvn = kvn.astype(BF16)
        kvn_ref[...] = kvn
        kv = jnp.dot(kvn, wukv_ref[...], preferred_element_type=F32)
        kpe = jnp.where(_lane_mask(misc.shape, KPE_LANE, KPE_LANE + MLA_ROPE), _rope_tile(misc, cs, sn, MLA_ROPE // 2), 0.0)
        for hd in range(MLA_HEADS):
            sl = slice(hd * HEAD_PAD, (hd + 1) * HEAD_PAD)
            q_ref[:, sl] = _rope_tile(q[:, sl], cs, sn, MLA_ROPE // 2).astype(BF16)
            k_ref[:, sl] = (kv[:, sl] + kpe).astype(BF16)
        v_ref[...] = kv[:, nq:].astype(BF16)
        xf = misc + bf_ref[...]
        logf = jnp.minimum(xf, 0.0) - jnp.log(1.0 + jnp.exp(-jnp.abs(xf)))
        lf_ref[...] = jnp.where(_lane_mask(misc.shape, FL_LANE, FL_LANE + FOX_HEADS), logf, 0.0)

    full = lambda a: pl.BlockSpec(a.shape, lambda i: (0,) * a.ndim)
    rows = lambda w, c=0: pl.BlockSpec((tb, w), lambda i: (i, c))
    return pl.pallas_call(
        body, name=name, grid=(s // tb,),
        in_specs=[rows(512, E_SMALL // 512), full(qg), full(kg), full(bf_tile), full(w_uq), full(w_ukv), rows(LANES), rows(LANES)],
        out_specs=[rows(nq), rows(nq), rows(MLA_HEADS * MLA_V), rows(MLA_Q_RANK), rows(MLA_KV_RANK), rows(LANES)],
        out_shape=[jax.ShapeDtypeStruct((s, nq), BF16), jax.ShapeDtypeStruct((s, nq), BF16),
                   jax.ShapeDtypeStruct((s, MLA_HEADS * MLA_V), BF16), jax.ShapeDtypeStruct((s, MLA_Q_RANK), BF16),
                   jax.ShapeDtypeStruct((s, MLA_KV_RANK), BF16), jax.ShapeDtypeStruct((s, LANES), F32)],
        compiler_params=_cparams((_PAR,)),
    )(h, qg, kg, bf_tile, w_uq, w_ukv, cos, sin)


def _even_mid_bwd(h, dq, dk, dv, dlogf, qg, kg, bf_tile, w_uq_t, w_ukv_t, cos, sin, *, name):
    s = h.shape[0]
    tb = _pick(s, (256, 128))
    nq = MLA_HEADS * HEAD_PAD

    def body(h_ref, dq_ref, dk_ref, dv_ref, dlf_ref, qg_ref, kg_ref, bf_ref, wuqt_ref, wukvt_ref, cos_ref, sin_ref,
             dh_ref, dqp_ref, dqg_ref, dkg_ref, dbf_ref):
        @pl.when(pl.program_id(0) == 0)
        def _():
            dqg_ref[...] = jnp.zeros_like(dqg_ref)
            dkg_ref[...] = jnp.zeros_like(dkg_ref)
            dbf_ref[...] = jnp.zeros_like(dbf_ref)

        hb = h_ref[...]
        cq, ckv, misc = hb[:, :MLA_Q_RANK], hb[:, MLA_Q_RANK:MLA_Q_RANK + MLA_KV_RANK], hb[:, MLA_Q_RANK + MLA_KV_RANK:]
        cs, sn = cos_ref[...], -sin_ref[...]
        dkpe = jnp.zeros(misc.shape, F32)
        for hd in range(MLA_HEADS):
            sl = slice(hd * HEAD_PAD, (hd + 1) * HEAD_PAD)
            dqp_ref[:, sl] = _rope_tile(dq_ref[:, sl], cs, sn, MLA_ROPE // 2).astype(BF16)
            dkpe = dkpe + dk_ref[:, sl]
        dqn = jnp.dot(dqp_ref[...], wuqt_ref[...], preferred_element_type=F32)
        uq, rq, _ = _rms(cq, qg_ref[...])
        dcq, dqg = _rms_bwd(dqn, uq, rq, qg_ref[...])
        dqg_ref[...] += dqg
        dkv = jnp.concatenate([dk_ref[...].astype(BF16), dv_ref[...]], axis=1)
        dkvn = jnp.dot(dkv, wukvt_ref[...], preferred_element_type=F32)
        uk, rk, _ = _rms(ckv, kg_ref[...])
        dckv, dkg = _rms_bwd(dkvn, uk, rk, kg_ref[...])
        dkg_ref[...] += dkg
        dmisc = jnp.where(_lane_mask(misc.shape, KPE_LANE, KPE_LANE + MLA_ROPE), _rope_tile(dkpe, cs, sn, MLA_ROPE // 2), 0.0)
        dfl = jnp.where(_lane_mask(misc.shape, FL_LANE, FL_LANE + FOX_HEADS), dlf_ref[...] * _sigmoid(-(misc + bf_ref[...])), 0.0)
        dbf_ref[...] += jnp.sum(dfl, axis=0, keepdims=True)
        dh_ref[:, :MLA_Q_RANK] = dcq.astype(BF16)
        dh_ref[:, MLA_Q_RANK:MLA_Q_RANK + MLA_KV_RANK] = dckv.astype(BF16)
        dh_ref[:, MLA_Q_RANK + MLA_KV_RANK:] = (dmisc + dfl).astype(BF16)

    full = lambda a: pl.BlockSpec(a.shape, lambda i: (0,) * a.ndim)
    rows = lambda w, c=0: pl.BlockSpec((tb, w), lambda i: (i, c))
    return pl.pallas_call(
        body, name=name, grid=(s // tb,),
        in_specs=[rows(512, E_SMALL // 512), rows(nq), rows(nq), rows(MLA_HEADS * MLA_V), rows(LANES), full(qg), full(kg),
                  full(bf_tile), full(w_uq_t), full(w_ukv_t), rows(LANES), rows(LANES)],
        out_specs=[rows(512), rows(nq), full(qg), full(kg), full(bf_tile)],
        out_shape=[jax.ShapeDtypeStruct((s, 512), BF16), jax.ShapeDtypeStruct((s, nq), BF16),
                   jax.ShapeDtypeStruct(qg.shape, F32), jax.ShapeDtypeStruct(kg.shape, F32),
                   jax.ShapeDtypeStruct(bf_tile.shape, F32)],
        compiler_params=_cparams((_ARB,)),
    )(h, dq, dk, dv, dlogf, qg, kg, bf_tile, w_uq_t, w_ukv_t, cos, sin)


def _cumsum(x, *, reverse, name):
    s = x.shape[0]
    tb = _pick(s, (512, 256, 128))
    nb = s // tb

    def body(x_ref, o_ref, carry_ref):
        @pl.when(pl.program_id(0) == 0)
        def _():
            carry_ref[...] = jnp.zeros_like(carry_ref)

        xv = x_ref[...]
        r = lax.broadcasted_iota(jnp.int32, (tb, tb), 0)
        c = lax.broadcasted_iota(jnp.int32, (tb, tb), 1)
        tri = jnp.where((c >= r) if reverse else (c <= r), 1.0, 0.0).astype(BF16)
        hi = xv.astype(BF16)
        r1 = xv - hi.astype(F32)
        mid = r1.astype(BF16)
        lo = (r1 - mid.astype(F32)).astype(BF16)
        cs = (jnp.dot(tri, hi, preferred_element_type=F32) + jnp.dot(tri, mid, preferred_element_type=F32)
              + jnp.dot(tri, lo, preferred_element_type=F32)) + carry_ref[...]
        o_ref[...] = cs
        carry_ref[...] = cs[0:1, :] if reverse else cs[tb - 1:tb, :]

    imap = (lambda i: (nb - 1 - i, 0)) if reverse else (lambda i: (i, 0))
    return pl.pallas_call(
        body, name=name, grid=(nb,), in_specs=[pl.BlockSpec((tb, LANES), imap)],
        out_specs=pl.BlockSpec((tb, LANES), imap), out_shape=jax.ShapeDtypeStruct(x.shape, F32),
        scratch_shapes=[pltpu.VMEM((1, LANES), F32)], compiler_params=_cparams((_ARB,)),
    )(x)


_NT = (((1,), (1,)), ((), ()))
_TN = (((0,), (0,)), ((), ()))


def _causal_scores(q, k, ccol, crow, qi, ki, t, scale):
    sc = lax.dot_general(q, k, _NT, preferred_element_type=F32) * scale
    if ccol is not None:
        sc = sc + ccol - crow
    row = lax.broadcasted_iota(jnp.int32, (t, t), 0) + qi * t
    col = lax.broadcasted_iota(jnp.int32, (t, t), 1) + ki * t
    return jnp.where(col <= row, sc, -jnp.inf)


def _attn_fwd(q, k, v, ccol, crow, *, scale, name):
    nh, s, dk = q.shape
    dv = v.shape[-1]
    t = _pick(s, (512, 256, 128))
    nb = s // t
    bias = ccol is not None

    def body(*refs):
        if bias:
            q_ref, k_ref, v_ref, cc_ref, cr_ref, o_ref, lse_ref, m_s, l_s, acc_s = refs
        else:
            q_ref, k_ref, v_ref, o_ref, lse_ref, m_s, l_s, acc_s = refs
        qi, ki = pl.program_id(1), pl.program_id(2)

        @pl.when(ki == 0)
        def _():
            m_s[...] = jnp.full_like(m_s, -jnp.inf)
            l_s[...] = jnp.zeros_like(l_s)
            acc_s[...] = jnp.zeros_like(acc_s)

        @pl.when(ki <= qi)
        def _():
            sc = _causal_scores(q_ref[0], k_ref[0], cc_ref[0] if bias else None, cr_ref[0] if bias else None, qi, ki, t, scale)
            m_new = jnp.maximum(m_s[...], jnp.max(sc, axis=1, keepdims=True))
            a = jnp.exp(m_s[...] - m_new)
            p = jnp.exp(sc - m_new)
            l_s[...] = a * l_s[...] + jnp.sum(p, axis=1, keepdims=True)
            acc_s[...] = a * acc_s[...] + jnp.dot(p.astype(BF16), v_ref[0], preferred_element_type=F32)
            m_s[...] = m_new

        @pl.when(ki == qi)
        def _():
            o_ref[0] = acc_s[...] / l_s[...]
            lse_ref[0] = m_s[...] + jnp.log(l_s[...])

    qmap = lambda h, i, j: (h, i, 0)
    kmap = lambda h, i, j: (h, jnp.minimum(i, j), 0)
    in_specs = [pl.BlockSpec((1, t, dk), qmap), pl.BlockSpec((1, t, dk), kmap), pl.BlockSpec((1, t, dv), kmap)]
    args = [q, k, v]
    if bias:
        in_specs += [pl.BlockSpec((1, t, 1), qmap), pl.BlockSpec((1, 1, t), lambda h, i, j: (h, 0, jnp.minimum(i, j)))]
        args += [ccol, crow]
    return pl.pallas_call(
        body, name=name, grid=(nh, nb, nb), in_specs=in_specs,
        out_specs=[pl.BlockSpec((1, t, dv), qmap), pl.BlockSpec((1, t, 1), qmap)],
        out_shape=[jax.ShapeDtypeStruct((nh, s, dv), F32), jax.ShapeDtypeStruct((nh, s, 1), F32)],
        scratch_shapes=[pltpu.VMEM((t, 1), F32), pltpu.VMEM((t, 1), F32), pltpu.VMEM((t, dv), F32)],
        compiler_params=_cparams((_PAR, _PAR, _ARB)),
    )(*args)


def _attn_bwd(q, k, v, o, do, lse, ccol, crow, *, scale, name):
    nh, s, dk = q.shape
    dv = v.shape[-1]
    t = _pick(s, (512, 256, 128))
    nb = s // t
    bias = ccol is not None

    def body(*refs):
        if bias:
            (q_ref, k_ref, v_ref, o_ref, do_ref, lse_ref, cc_ref, cr_ref,
             dq_ref, dk_ref, dv_ref, dc_ref, dcc_ref, dk_s, dv_s, dc_s) = refs
        else:
            q_ref, k_ref, v_ref, o_ref, do_ref, lse_ref, dq_ref, dk_ref, dv_ref, dk_s, dv_s = refs
        ki, qi = pl.program_id(1), pl.program_id(2)

        @pl.when((ki == 0) & (qi == 0))
        def _():
            dq_ref[...] = jnp.zeros_like(dq_ref)
            if bias:
                dcc_ref[...] = jnp.zeros_like(dcc_ref)

        @pl.when(qi == ki)
        def _():
            dk_s[...] = jnp.zeros_like(dk_s)
            dv_s[...] = jnp.zeros_like(dv_s)
            if bias:
                dc_s[...] = jnp.zeros_like(dc_s)

        @pl.when(qi >= ki)
        def _():
            qb, kb, dob = q_ref[0], k_ref[0], do_ref[0]
            sc = _causal_scores(qb, kb, cc_ref[0] if bias else None, cr_ref[0] if bias else None, qi, ki, t, scale)
            p = jnp.exp(sc - lse_ref[0])
            delta = jnp.sum(dob.astype(F32) * o_ref[0], axis=1, keepdims=True)
            dp = lax.dot_general(dob, v_ref[0], _NT, preferred_element_type=F32)
            ds = p * (dp - delta)
            dv_s[...] += lax.dot_general(p.astype(BF16), dob, _TN, preferred_element_type=F32)
            dsb = (ds * scale).astype(BF16)
            dk_s[...] += lax.dot_general(dsb, qb, _TN, preferred_element_type=F32)
            rows = pl.ds(pl.multiple_of(qi * t, t), t)
            dq_ref[0, rows, :] += jnp.dot(dsb, kb, preferred_element_type=F32)
            if bias:
                dc_s[...] -= jnp.sum(ds, axis=0, keepdims=True)
                dcc_ref[0, rows, :] += jnp.sum(ds, axis=1, keepdims=True)

        @pl.when(qi == nb - 1)
        def _():
            dk_ref[0] = dk_s[...]
            dv_ref[0] = dv_s[...]
            if bias:
                dc_ref[0] = dc_s[...]

    qmap = lambda h, j, i: (h, jnp.maximum(i, j), 0)
    kmap = lambda h, j, i: (h, j, 0)
    in_specs = [pl.BlockSpec((1, t, dk), qmap), pl.BlockSpec((1, t, dk), kmap), pl.BlockSpec((1, t, dv), kmap),
                pl.BlockSpec((1, t, dv), qmap), pl.BlockSpec((1, t, dv), qmap), pl.BlockSpec((1, t, 1), qmap)]
    args = [q, k, v, o, do, lse]
    out_specs = [pl.BlockSpec((1, s, dk), lambda h, j, i: (h, 0, 0)), pl.BlockSpec((1, t, dk), kmap), pl.BlockSpec((1, t, dv), kmap)]
    out_shape = [jax.ShapeDtypeStruct((nh, s, dk), F32), jax.ShapeDtypeStruct((nh, s, dk), F32), jax.ShapeDtypeStruct((nh, s, dv), F32)]
    scratch = [pltpu.VMEM((t, dk), F32), pltpu.VMEM((t, dv), F32)]
    if bias:
        in_specs += [pl.BlockSpec((1, t, 1), qmap), pl.BlockSpec((1, 1, t), lambda h, j, i: (h, 0, j))]
        args += [ccol, crow]
        out_specs += [pl.BlockSpec((1, 1, t), lambda h, j, i: (h, 0, j)), pl.BlockSpec((1, s, 1), lambda h, j, i: (h, 0, 0))]
        out_shape += [jax.ShapeDtypeStruct((nh, 1, s), F32), jax.ShapeDtypeStruct((nh, s, 1), F32)]
        scratch.append(pltpu.VMEM((1, t), F32))
    return pl.pallas_call(
        body, name=name, grid=(nh, nb, nb), in_specs=in_specs, out_specs=out_specs, out_shape=out_shape,
        scratch_shapes=scratch, compiler_params=_cparams((_PAR, _ARB, _ARB)),
    )(*args)


def _swa_scores(qb, kw, start, t, scale):
    sc = lax.dot_general(qb, kw, _NT, preferred_element_type=F32) * scale
    row = lax.broadcasted_iota(jnp.int32, (t, t + WINDOW), 0)
    col = lax.broadcasted_iota(jnp.int32, (t, t + WINDOW), 1)
    diff = row - col + WINDOW
    valid = (diff >= 0) & (diff < WINDOW) & (col + start >= WINDOW)
    return jnp.where(valid, sc, -jnp.inf)


def _swa_fwd(q, kp, vp, sink, *, name):
    nh, s, d = q.shape
    t = _pick(s, (256, 128))
    scale = d ** -0.5

    def body(q_ref, k_ref, v_ref, sk_ref, o_ref, lse_ref):
        start = pl.multiple_of(pl.program_id(1) * t, t)
        kw = k_ref[0, pl.ds(start, t + WINDOW), :]
        vw = v_ref[0, pl.ds(start, t + WINDOW), :]
        sc = _swa_scores(q_ref[0], kw, start, t, scale)
        snk = sk_ref[0][:, 0:1]
        m = jnp.maximum(jnp.max(sc, axis=1, keepdims=True), snk)
        e = jnp.exp(sc - m)
        l = jnp.sum(e, axis=1, keepdims=True) + jnp.exp(snk - m)
        o_ref[0] = jnp.dot((e / l).astype(BF16), vw, preferred_element_type=F32)
        lse_ref[0] = m + jnp.log(l)

    qmap = lambda h, i: (h, i, 0)
    kvspec = pl.BlockSpec((1, s + WINDOW, d), lambda h, i: (h // SWA_GROUP, 0, 0))
    return pl.pallas_call(
        body, name=name, grid=(nh, s // t),
        in_specs=[pl.BlockSpec((1, t, d), qmap), kvspec, kvspec, pl.BlockSpec((1, 1, LANES), lambda h, i: (h, 0, 0))],
        out_specs=[pl.BlockSpec((1, t, d), qmap), pl.BlockSpec((1, t, 1), qmap)],
        out_shape=[jax.ShapeDtypeStruct((nh, s, d), F32), jax.ShapeDtypeStruct((nh, s, 1), F32)],
        compiler_params=_cparams((_PAR, _PAR)),
    )(q, kp, vp, sink)


def _swa_bwd(q, kp, vp, sink, o, do, lse, *, name):
    nh, s, d = q.shape
    nkv = kp.shape[0]
    t = _pick(s, (256, 128))
    scale = d ** -0.5

    def body(q_ref, k_ref, v_ref, sk_ref, o_ref, do_ref, lse_ref, dq_ref, dk_ref, dv_ref, dsk_ref):
        hh, i = pl.program_id(0), pl.program_id(1)
        start = pl.multiple_of(i * t, t)

        @pl.when((hh % SWA_GROUP == 0) & (i == 0))
        def _():
            dk_ref[...] = jnp.zeros_like(dk_ref)
            dv_ref[...] = jnp.zeros_like(dv_ref)

        @pl.when(i == 0)
        def _():
            dsk_ref[...] = jnp.zeros_like(dsk_ref)

        win = pl.ds(start, t + WINDOW)
        qb, dob = q_ref[0], do_ref[0]
        kw, vw = k_ref[0, win, :], v_ref[0, win, :]
        sc = _swa_scores(qb, kw, start, t, scale)
        lse_b = lse_ref[0]
        p = jnp.exp(sc - lse_b)
        delta = jnp.sum(dob.astype(F32) * o_ref[0], axis=1, keepdims=True)
        dp = lax.dot_general(dob, vw, _NT, preferred_element_type=F32)
        ds = p * (dp - delta)
        dsb = (ds * scale).astype(BF16)
        dq_ref[0] = jnp.dot(dsb, kw, preferred_element_type=F32)
        dk_ref[0, win, :] += lax.dot_general(dsb, qb, _TN, preferred_element_type=F32)
        dv_ref[0, win, :] += lax.dot_general(p.astype(BF16), dob, _TN, preferred_element_type=F32)
        psink = jnp.exp(sk_ref[0][:, 0:1] - lse_b)
        dsk_ref[0] += jnp.broadcast_to(-jnp.sum(psink * delta, axis=0, keepdims=True), (1, LANES))

    qmap = lambda h, i: (h, i, 0)
    kvspec = pl.BlockSpec((1, s + WINDOW, d), lambda h, i: (h // SWA_GROUP, 0, 0))
    skspec = pl.BlockSpec((1, 1, LANES), lambda h, i: (h, 0, 0))
    return pl.pallas_call(
        body, name=name, grid=(nh, s // t),
        in_specs=[pl.BlockSpec((1, t, d), qmap), kvspec, kvspec, skspec, pl.BlockSpec((1, t, d), qmap),
                  pl.BlockSpec((1, t, d), qmap), pl.BlockSpec((1, t, 1), qmap)],
        out_specs=[pl.BlockSpec((1, t, d), qmap), kvspec, kvspec, skspec],
        out_shape=[jax.ShapeDtypeStruct((nh, s, d), F32), jax.ShapeDtypeStruct((nkv, s + WINDOW, d), F32),
                   jax.ShapeDtypeStruct((nkv, s + WINDOW, d), F32), jax.ShapeDtypeStruct((nh, 1, LANES), F32)],
        compiler_params=_cparams((_ARB, _ARB)),
    )(q, kp, vp, sink, o, do, lse)


def _odd_rope(h, cos, sin, *, name):
    s = h.shape[0]
    tb = _pick(s, (512, 256, 128))
    nq = SWA_HEADS * SWA_DIM

    def body(q_ref, kv_ref, cos_ref, sin_ref, qo_ref, ko_ref, vo_ref):
        cs, sn = cos_ref[...], sin_ref[...]
        for j in range(nq // LANES):
            sl = slice(j * LANES, (j + 1) * LANES)
            qo_ref[:, sl] = _rope_tile(q_ref[:, sl], cs, sn, SWA_DIM // 2).astype(BF16)
        ko_ref[...] = _rope_tile(kv_ref[:, :LANES], cs, sn, SWA_DIM // 2).astype(BF16)
        vo_ref[...] = kv_ref[:, LANES:].astype(BF16)

    rows = lambda w, c=0: pl.BlockSpec((tb, w), lambda i: (i, c))
    return pl.pallas_call(
        body, name=name, grid=(s // tb,),
        in_specs=[rows(nq, O_Q // nq), rows(2 * LANES, O_K // (2 * LANES)), rows(LANES), rows(LANES)],
        out_specs=[rows(nq), rows(LANES), rows(LANES)],
        out_shape=[jax.ShapeDtypeStruct((s, nq), BF16), jax.ShapeDtypeStruct((s, LANES), BF16), jax.ShapeDtypeStruct((s, LANES), BF16)],
        compiler_params=_cparams((_PAR,)),
    )(h, h, cos, sin)


def _odd_rope_bwd(dq, dk, cos, sin, *, name):
    s = dq.shape[0]
    tb = _pick(s, (512, 256, 128))
    nq = SWA_HEADS * SWA_DIM

    def body(dq_ref, dk_ref, cos_ref, sin_ref, qo_ref, ko_ref):
        cs, sn = cos_ref[...], -sin_ref[...]
        for j in range(nq // LANES):
            sl = slice(j * LANES, (j + 1) * LANES)
            qo_ref[:, sl] = _rope_tile(dq_ref[:, sl], cs, sn, SWA_DIM // 2).astype(BF16)
        ko_ref[...] = _rope_tile(dk_ref[...], cs, sn, SWA_DIM // 2).astype(BF16)

    rows = lambda w: pl.BlockSpec((tb, w), lambda i: (i, 0))
    return pl.pallas_call(
        body, name=name, grid=(s // tb,), in_specs=[rows(nq), rows(LANES), rows(LANES), rows(LANES)],
        out_specs=[rows(nq), rows(LANES)],
        out_shape=[jax.ShapeDtypeStruct((s, nq), BF16), jax.ShapeDtypeStruct((s, LANES), BF16)],
        compiler_params=_cparams((_PAR,)),
    )(dq, dk, cos, sin)


def _out_fwd(o, h, x, w_out, g, b, *, name):
    s = x.shape[0]
    tb = _pick(s, (256, 128))

    def body(o_ref, gate_ref, x_ref, w_ref, g_ref, b_ref, xn_ref, xb_ref, z_ref, xh_ref, rs_ref):
        gate = gate_ref[...]
        z = (o_ref[...] * (gate * _sigmoid(gate))).astype(BF16)
        z_ref[...] = z
        r = ALPHA * x_ref[...] + jnp.dot(z, w_ref[...], preferred_element_type=F32)
        mu = jnp.mean(r, axis=-1, keepdims=True)
        rc = r - mu
        rstd = lax.rsqrt(jnp.mean(rc * rc, axis=-1, keepdims=True) + LN_EPS)
        xh = rc * rstd
        xn = xh * g_ref[...] + b_ref[...]
        xh_ref[...] = xh
        rs_ref[...] = rstd
        xn_ref[...] = xn
        xb_ref[...] = xn.astype(BF16)

    rows = lambda w: pl.BlockSpec((tb, w), lambda i: (i, 0))
    full = lambda a: pl.BlockSpec(a.shape, lambda i: (0,) * a.ndim)
    return pl.pallas_call(
        body, name=name, grid=(s // tb,),
        in_specs=[rows(D_MODEL), rows(D_MODEL), rows(D_MODEL), full(w_out), full(g), full(b)],
        out_specs=[rows(D_MODEL), rows(D_MODEL), rows(D_MODEL), rows(D_MODEL), rows(1)],
        out_shape=[jax.ShapeDtypeStruct((s, D_MODEL), F32), jax.ShapeDtypeStruct((s, D_MODEL), BF16),
                   jax.ShapeDtypeStruct((s, D_MODEL), BF16), jax.ShapeDtypeStruct((s, D_MODEL), F32),
                   jax.ShapeDtypeStruct((s, 1), F32)],
        compiler_params=_cparams((_PAR,)),
    )(o, h, x, w_out, g, b)


def _out_bwd(dxn, xh, rstd, g, w_out_t, o, h, *, name):
    s = dxn.shape[0]
    tb = _pick(s, (256, 128))

    def body(dxn_ref, xh_ref, rs_ref, g_ref, wt_ref, o_ref, gate_ref, dr_ref, dy_ref, do_ref, dgate_ref, dg_ref, db_ref):
        @pl.when(pl.program_id(0) == 0)
        def _():
            dg_ref[...] = jnp.zeros_like(dg_ref)
            db_ref[...] = jnp.zeros_like(db_ref)

        dxn_b, xh_b = dxn_ref[...], xh_ref[...]
        dg_ref[...] += jnp.sum(dxn_b * xh_b, axis=0, keepdims=True)
        db_ref[...] += jnp.sum(dxn_b, axis=0, keepdims=True)
        dxh = dxn_b * g_ref[...]
        dr = rs_ref[...] * (dxh - jnp.mean(dxh, axis=-1, keepdims=True) - xh_b * jnp.mean(dxh * xh_b, axis=-1, keepdims=True))
        dr_ref[...] = dr
        dy = dr.astype(BF16)
        dy_ref[...] = dy
        dz = jnp.dot(dy, wt_ref[...], preferred_element_type=F32)
        gate = gate_ref[...]
        sg = _sigmoid(gate)
        do_ref[...] = (dz * (gate * sg)).astype(BF16)
        dgate_ref[...] = (dz * o_ref[...] * (sg * (1.0 + gate * (1.0 - sg)))).astype(BF16)

    rows = lambda w: pl.BlockSpec((tb, w), lambda i: (i, 0))
    full = lambda a: pl.BlockSpec(a.shape, lambda i: (0,) * a.ndim)
    return pl.pallas_call(
        body, name=name, grid=(s // tb,),
        in_specs=[rows(D_MODEL), rows(D_MODEL), rows(1), full(g), full(w_out_t), rows(D_MODEL), rows(D_MODEL)],
        out_specs=[rows(D_MODEL), rows(D_MODEL), rows(D_MODEL), rows(D_MODEL), full(g), full(g)],
        out_shape=[jax.ShapeDtypeStruct((s, D_MODEL), F32), jax.ShapeDtypeStruct((s, D_MODEL), BF16),
                   jax.ShapeDtypeStruct((s, D_MODEL), BF16), jax.ShapeDtypeStruct((s, D_MODEL), BF16),
                   jax.ShapeDtypeStruct(g.shape, F32), jax.ShapeDtypeStruct(g.shape, F32)],
        compiler_params=_cparams((_ARB,)),
    )(dxn, xh, rstd, g, w_out_t, o, h)


def _loss_grad(y, target, *, name):
    s, d = y.shape
    tb = _pick(s, (512, 256, 128))

    def body(y_ref, t_ref, dy_ref, l_ref):
        @pl.when(pl.program_id(0) == 0)
        def _():
            l_ref[...] = jnp.zeros_like(l_ref)

        err = y_ref[...] - t_ref[...]
        dy_ref[...] = err * (1.0 / d)
        per_tok = jnp.mean(err * err, axis=-1, keepdims=True)
        l_ref[...] += 0.5 * jnp.sum(per_tok, axis=0, keepdims=True)

    rows = pl.BlockSpec((tb, d), lambda i: (i, 0))
    return pl.pallas_call(
        body, name=name, grid=(s // tb,), in_specs=[rows, rows],
        out_specs=[rows, pl.BlockSpec((8, LANES), lambda i: (0, 0))],
        out_shape=[jax.ShapeDtypeStruct((s, d), F32), jax.ShapeDtypeStruct((8, LANES), F32)],
        compiler_params=_cparams((_ARB,)),
    )(y, target)


def _adamw(w, g, m, v, *, name):
    shape = w.shape
    w3, g3, m3, v3 = (a.reshape((1,) * (3 - a.ndim) + a.shape) for a in (w, g, m, v))
    a0, a1, a2 = w3.shape
    tb = _pick(a1, (256, 128)) if a1 % 8 == 0 else a1
    c1 = 1.0 - ADAM_B1 ** ADAM_STEP
    c2 = 1.0 - ADAM_B2 ** ADAM_STEP

    def body(w_ref, g_ref, m_ref, v_ref, d_ref, mo_ref, vo_ref):
        gg = g_ref[...]
        mn = ADAM_B1 * m_ref[...] + (1.0 - ADAM_B1) * gg
        vn = ADAM_B2 * v_ref[...] + (1.0 - ADAM_B2) * (gg * gg)
        mo_ref[...] = mn
        vo_ref[...] = vn
        d_ref[...] = -ADAM_LR * ((mn / c1) / (jnp.sqrt(vn / c2) + ADAM_EPS) + ADAM_WD * w_ref[...])

    spec = pl.BlockSpec((1, tb, a2), lambda i, j: (i, j, 0))
    outs = pl.pallas_call(
        body, name=name, grid=(a0, a1 // tb), in_specs=[spec] * 4, out_specs=[spec] * 3,
        out_shape=[jax.ShapeDtypeStruct(w3.shape, F32)] * 3, compiler_params=_cparams((_PAR, _PAR)),
    )(w3, g3, m3, v3)
    return tuple(a.reshape(shape) for a in outs)


def _place():
    x, y, c = lax.axis_index("x"), lax.axis_index("y"), lax.axis_index("c")
    return x, y, c, [(1 - x, y), (x, 1 - y), (1 - x, 1 - y)]


_ANY = pl.BlockSpec(memory_space=pl.ANY)


def _gather_chip_shards(buf, *, name):
    r, cdim = buf.shape
    rh = r // 2

    def body(src, out, send_sems, recv_sems, lsem):
        x, y, c, chips = _place()
        me = 2 * x + y
        sib = (x, y, 1 - c)

        def half(chip, hc):
            return out.at[chip, pl.ds(hc * rh, rh), :]

        def copy(kk, s_ref, d_ref, to):
            return pltpu.make_async_remote_copy(src_ref=s_ref, dst_ref=d_ref, send_sem=send_sems.at[kk],
                                                recv_sem=recv_sems.at[kk], device_id=to, device_id_type=MESH)

        mine = pltpu.make_async_copy(src, out.at[me], lsem)
        mine.start()
        first = [copy(j, src.at[pl.ds(c * rh, rh), :], half(me, c), (cx, cy, c)) for j, (cx, cy) in enumerate(chips)]
        for cp in first:
            cp.start()
        passed = []
        for j, (cx, cy) in enumerate(chips):
            landed = half(2 * cx + cy, c)
            copy(j, landed, landed, (cx, cy, c)).wait_recv()
            fw = copy(3 + j, landed, landed, sib)
            fw.start()
            passed.append(fw)
        for j, (cx, cy) in enumerate(chips):
            theirs = half(2 * cx + cy, 1 - c)
            copy(3 + j, theirs, theirs, sib).wait_recv()
        for cp in first + passed:
            cp.wait_send()
        mine.wait()

    return pl.pallas_call(
        body, name=name, in_specs=[_ANY], out_specs=_ANY, out_shape=jax.ShapeDtypeStruct((N_CHIPS, r, cdim), buf.dtype),
        scratch_shapes=[pltpu.SemaphoreType.DMA((6,)), pltpu.SemaphoreType.DMA((6,)), pltpu.SemaphoreType.DMA(())],
    )(buf)


def _swap_halves(g, *, name):
    _, _, rh, cdim = g.shape

    def body(src, out, send_sem, recv_sem):
        x, y, c, _ = _place()
        cp = pltpu.make_async_remote_copy(src_ref=src.at[:, 1 - c], dst_ref=out, send_sem=send_sem, recv_sem=recv_sem,
                                          device_id=(x, y, 1 - c), device_id_type=MESH)
        cp.start()
        cp.wait()

    return pl.pallas_call(
        body, name=name, in_specs=[_ANY], out_specs=_ANY, out_shape=jax.ShapeDtypeStruct((N_CHIPS, rh, cdim), g.dtype),
        scratch_shapes=[pltpu.SemaphoreType.DMA(()), pltpu.SemaphoreType.DMA(())],
    )(g)


def _scatter_to_chips(a, *, name):
    _, rh, cdim = a.shape

    def body(src, out, send_sems, recv_sems):
        x, y, c, chips = _place()
        cps = [pltpu.make_async_remote_copy(src_ref=src.at[2 * cx + cy], dst_ref=out.at[j], send_sem=send_sems.at[j],
                                            recv_sem=recv_sems.at[j], device_id=(cx, cy, c), device_id_type=MESH)
               for j, (cx, cy) in enumerate(chips)]
        for cp in cps:
            cp.start()
        for cp in cps:
            cp.wait()

    return pl.pallas_call(
        body, name=name, in_specs=[_ANY], out_specs=_ANY, out_shape=jax.ShapeDtypeStruct((3, rh, cdim), a.dtype),
        scratch_shapes=[pltpu.SemaphoreType.DMA((3,)), pltpu.SemaphoreType.DMA((3,))],
    )(a)


def _join_halves(f, *, name):
    rh, cdim = f.shape

    def body(src, out, send_sem, recv_sem, lsem):
        x, y, c, _ = _place()
        mine = pltpu.make_async_copy(src, out.at[c], lsem)
        mine.start()
        cp = pltpu.make_async_remote_copy(src_ref=src, dst_ref=out.at[c], send_sem=send_sem, recv_sem=recv_sem,
                                          device_id=(x, y, 1 - c), device_id_type=MESH)
        cp.start()
        cp.wait()
        mine.wait()

    return pl.pallas_call(
        body, name=name, in_specs=[_ANY], out_specs=_ANY, out_shape=jax.ShapeDtypeStruct((2, rh, cdim), f.dtype),
        scratch_shapes=[pltpu.SemaphoreType.DMA(()), pltpu.SemaphoreType.DMA(()), pltpu.SemaphoreType.DMA(())],
    )(f)


def _add_sibling(g, recv, cidx, *, name):
    _, _, rh, cdim = g.shape
    tb = _pick(rh, (384, 256, 128, 64, 32, 16, 8))

    def body(c_ref, g_ref, r_ref, o_ref):
        o_ref[...] = g_ref[0] + r_ref[...]

    return pl.pallas_call(
        body, name=name,
        grid_spec=pltpu.PrefetchScalarGridSpec(
            num_scalar_prefetch=1, grid=(N_CHIPS, rh // tb),
            in_specs=[pl.BlockSpec((1, 1, tb, cdim), lambda k, i, c_ref: (k, c_ref[0], i, 0)),
                      pl.BlockSpec((1, tb, cdim), lambda k, i, c_ref: (k, i, 0))],
            out_specs=pl.BlockSpec((1, tb, cdim), lambda k, i, c_ref: (k, i, 0))),
        out_shape=jax.ShapeDtypeStruct((N_CHIPS, rh, cdim), F32), compiler_params=_cparams((_PAR, _PAR)),
    )(cidx, g, recv)


def _add_chips(a, recv, chip_idx, *, name):
    _, rh, cdim = a.shape
    tb = _pick(rh, (384, 256, 128, 64, 32, 16, 8))

    def body(k_ref, a_ref, r0_ref, r1_ref, r2_ref, o_ref):
        o_ref[...] = ((a_ref[0] + r0_ref[0]) + r1_ref[0]) + r2_ref[0]

    slot = lambda j: pl.BlockSpec((1, tb, cdim), lambda i, k_ref: (j, i, 0))
    return pl.pallas_call(
        body, name=name,
        grid_spec=pltpu.PrefetchScalarGridSpec(
            num_scalar_prefetch=1, grid=(rh // tb,),
            in_specs=[pl.BlockSpec((1, tb, cdim), lambda i, k_ref: (k_ref[0], i, 0)), slot(0), slot(1), slot(2)],
            out_specs=pl.BlockSpec((tb, cdim), lambda i, k_ref: (i, 0))),
        out_shape=jax.ShapeDtypeStruct((rh, cdim), F32), compiler_params=_cparams((_PAR,)),
    )(chip_idx, a, recv, recv, recv)


def _all_reduce_small(v, *, name):
    r, cdim = v.shape

    def body(v_ref, o_ref, buf, send_sems, recv_sems):
        x, y, c, _ = _place()
        me = 4 * x + 2 * y + c
        buf[me] = v_ref[...]
        cps = []
        for p in range(1, N_DEV):
            to = (1 - x if p & 4 else x, 1 - y if p & 2 else y, 1 - c if p & 1 else c)
            cp = pltpu.make_async_remote_copy(src_ref=v_ref, dst_ref=buf.at[me], send_sem=send_sems.at[p - 1],
                                              recv_sem=recv_sems.at[p - 1], device_id=to, device_id_type=MESH)
            cp.start()
            cps.append(cp)
        for p in range(1, N_DEV):
            frm = (4 * x + 2 * y + c) ^ p
            pltpu.make_async_remote_copy(src_ref=v_ref, dst_ref=buf.at[frm], send_sem=send_sems.at[p - 1],
                                         recv_sem=recv_sems.at[p - 1], device_id=(x, y, c), device_id_type=MESH).wait_recv()
        for cp in cps:
            cp.wait_send()
        tot = buf[0]
        for i in range(1, N_DEV):
            tot = tot + buf[i]
        o_ref[...] = tot

    vm = pl.BlockSpec(memory_space=pltpu.VMEM)
    return pl.pallas_call(
        body, name=name, in_specs=[vm], out_specs=vm, out_shape=jax.ShapeDtypeStruct((r, cdim), F32),
        scratch_shapes=[pltpu.VMEM((N_DEV, r, cdim), F32), pltpu.SemaphoreType.DMA((N_DEV - 1,)), pltpu.SemaphoreType.DMA((N_DEV - 1,))],
    )(v)


_SHARDED = ("even_w_in", "even_w_uq", "even_w_ukv", "even_w_out", "odd_w_in", "odd_w_out")
_COL_SHARDED = ("even_w_in", "even_w_uq", "even_w_ukv", "odd_w_in")
_FULL_SHAPES = {"even_w_in": (2, D_MODEL, EVEN_IN), "even_w_uq": (2, MLA_Q_RANK, 768), "even_w_ukv": (2, MLA_KV_RANK, 1024),
                "even_w_out": (2, D_MODEL, D_MODEL), "odd_w_in": (2, D_MODEL, ODD_IN), "odd_w_out": (2, D_MODEL, D_MODEL)}
PACK_COLS = 1024


def _unshard(name, stacked):
    if name in _COL_SHARDED:
        n, a, b, cc = stacked.shape
        return stacked.transpose(1, 2, 0, 3).reshape(a, b, n * cc)
    n, a, b, cc = stacked.shape
    return stacked.transpose(1, 0, 2, 3).reshape(a, n * b, cc)


def _shard_all(name, full):
    a, b, cc = full.shape
    if name in _COL_SHARDED:
        return full.reshape(a, b, N_CHIPS, cc // N_CHIPS).transpose(2, 0, 1, 3).reshape(N_CHIPS, -1)
    return full.reshape(a, N_CHIPS, b // N_CHIPS, cc).transpose(1, 0, 2, 3).reshape(N_CHIPS, -1)


def _rope_tables(s):
    pos = jnp.arange(s, dtype=F32)

    def ang(d):
        inv = ROPE_THETA ** (-jnp.arange(0, d, 2, dtype=F32) / d)
        a = pos[:, None] * inv[None, :]
        return jnp.cos(a), jnp.sin(a)

    c16, s16 = ang(MLA_ROPE)
    one, zero = jnp.ones((s, KPE_LANE), F32), jnp.zeros((s, KPE_LANE), F32)
    cos_m = jnp.concatenate([one, c16, c16, one[:, :32]], axis=1)
    sin_m = jnp.concatenate([zero, -s16, s16, zero[:, :32]], axis=1)
    c32, s32 = ang(SWA_DIM)
    cos_s = jnp.concatenate([c32, c32, c32, c32], axis=1)
    sin_s = jnp.concatenate([-s32, s32, -s32, s32], axis=1)
    return cos_m, sin_m, cos_s, sin_s


def _heads(t, nh):
    s = t.shape[0]
    return t.reshape(s, nh, -1).transpose(1, 0, 2)


def _merge(t):
    nh, s, d = t.shape
    return t.transpose(1, 0, 2).reshape(s, nh * d)


def _even_weights(w_in, w_uq, w_ukv):
    zeros = lambda n: jnp.zeros((D_MODEL, n), w_in.dtype)
    wcq, wckv, wkpe = w_in[:, 0:256], w_in[:, 256:384], w_in[:, 384:416]
    wfq, wfk, wfv = w_in[:, 416:928], w_in[:, 928:1440], w_in[:, 1440:1952]
    wfl, wg = w_in[:, 1952:1960], w_in[:, 1960:2984]
    misc = jnp.concatenate([zeros(KPE_LANE), wkpe, wfl, zeros(LANES - FL_LANE - FOX_HEADS)], axis=1)
    w_in_p = jnp.concatenate([wg, wfq, wfk, wfv, wcq, wckv, misc], axis=1)
    uq = w_uq.reshape(MLA_Q_RANK, MLA_HEADS, MLA_NOPE + MLA_ROPE)
    uq_p = jnp.pad(uq, ((0, 0), (0, 0), (0, HEAD_PAD - MLA_NOPE - MLA_ROPE))).reshape(MLA_Q_RANK, MLA_HEADS * HEAD_PAD)
    ukv = w_ukv.reshape(MLA_KV_RANK, MLA_HEADS, MLA_NOPE + MLA_V)
    uk_p = jnp.pad(ukv[..., :MLA_NOPE], ((0, 0), (0, 0), (0, HEAD_PAD - MLA_NOPE))).reshape(MLA_KV_RANK, MLA_HEADS * HEAD_PAD)
    uv = ukv[..., MLA_NOPE:].reshape(MLA_KV_RANK, MLA_HEADS * MLA_V)
    ukv_p = jnp.concatenate([uk_p, uv], axis=1)
    return w_in_p, w_in_p.T, uq_p, uq_p.T, ukv_p, ukv_p.T


def _even_weight_grads(dw_in_p, duq_p, dukv_p):
    g = dw_in_p
    gate, fq, fk, fv = g[:, E_GATE:E_FQ], g[:, E_FQ:E_FK], g[:, E_FK:E_FV], g[:, E_FV:E_SMALL]
    cq, ckv, misc = g[:, E_SMALL:E_SMALL + 256], g[:, E_SMALL + 256:E_SMALL + 384], g[:, E_SMALL + 384:]
    dw_in = jnp.concatenate([cq, ckv, misc[:, KPE_LANE:KPE_LANE + MLA_ROPE], fq, fk, fv, misc[:, FL_LANE:FL_LANE + FOX_HEADS], gate], axis=1)
    duq = duq_p.reshape(MLA_Q_RANK, MLA_HEADS, HEAD_PAD)[..., :MLA_NOPE + MLA_ROPE].reshape(MLA_Q_RANK, -1)
    nq = MLA_HEADS * HEAD_PAD
    dk = dukv_p[:, :nq].reshape(MLA_KV_RANK, MLA_HEADS, HEAD_PAD)[..., :MLA_NOPE]
    dvv = dukv_p[:, nq:].reshape(MLA_KV_RANK, MLA_HEADS, MLA_V)
    dukv = jnp.concatenate([dk, dvv], axis=-1).reshape(MLA_KV_RANK, -1)
    return dw_in, duq, dukv


def _odd_weights(w_in):
    q, k, v, gate = w_in[:, 0:1024], w_in[:, 1024:1152], w_in[:, 1152:1280], w_in[:, 1280:2304]
    w_p = jnp.concatenate([gate, q, k, v], axis=1)
    return w_p, w_p.T


def _odd_weight_grads(g):
    return jnp.concatenate([g[:, O_Q:O_K], g[:, O_K:O_V], g[:, O_V:], g[:, O_GATE:O_Q]], axis=1)


def kernel(x, even_w_in, even_q_norm, even_w_uq, even_kv_norm, even_w_ukv, even_b_f, even_w_out, even_ln_g, even_ln_b, odd_w_in, odd_sinks, odd_w_out, odd_ln_g, odd_ln_b, loss_target, m_even_w_in, m_even_q_norm, m_even_w_uq, m_even_kv_norm, m_even_w_ukv, m_even_b_f, m_even_w_out, m_even_ln_g, m_even_ln_b, m_odd_w_in, m_odd_sinks, m_odd_w_out, m_odd_ln_g, m_odd_ln_b, v_even_w_in, v_even_q_norm, v_even_w_uq, v_even_kv_norm, v_even_w_ukv, v_even_b_f, v_even_w_out, v_even_ln_g, v_even_ln_b, v_odd_w_in, v_odd_sinks, v_odd_w_out, v_odd_ln_g, v_odd_ln_b):
    weights = dict(even_w_in=even_w_in, even_q_norm=even_q_norm, even_w_uq=even_w_uq, even_kv_norm=even_kv_norm,
                   even_w_ukv=even_w_ukv, even_b_f=even_b_f, even_w_out=even_w_out, even_ln_g=even_ln_g, even_ln_b=even_ln_b,
                   odd_w_in=odd_w_in, odd_sinks=odd_sinks, odd_w_out=odd_w_out, odd_ln_g=odd_ln_g, odd_ln_b=odd_ln_b)
    mom_m = dict(even_w_in=m_even_w_in, even_q_norm=m_even_q_norm, even_w_uq=m_even_w_uq, even_kv_norm=m_even_kv_norm,
                 even_w_ukv=m_even_w_ukv, even_b_f=m_even_b_f, even_w_out=m_even_w_out, even_ln_g=m_even_ln_g, even_ln_b=m_even_ln_b,
                 odd_w_in=m_odd_w_in, odd_sinks=m_odd_sinks, odd_w_out=m_odd_w_out, odd_ln_g=m_odd_ln_g, odd_ln_b=m_odd_ln_b)
    mom_v = dict(even_w_in=v_even_w_in, even_q_norm=v_even_q_norm, even_w_uq=v_even_w_uq, even_kv_norm=v_even_kv_norm,
                 even_w_ukv=v_even_w_ukv, even_b_f=v_even_b_f, even_w_out=v_even_w_out, even_ln_g=v_even_ln_g, even_ln_b=v_even_ln_b,
                 odd_w_in=v_odd_w_in, odd_sinks=v_odd_sinks, odd_w_out=v_odd_w_out, odd_ln_g=v_odd_ln_g, odd_ln_b=v_odd_ln_b)
    names = list(weights)
    xl = x[0]
    tgt = loss_target[0]
    s = xl.shape[0]
    ax, ay, ac = lax.axis_index("x"), lax.axis_index("y"), lax.axis_index("c")
    chip = 2 * ax + ay

    flat = [weights[n].astype(BF16).reshape(-1) for n in _SHARDED]
    ln_odd = jnp.concatenate([odd_ln_g.reshape(-1), odd_ln_b.reshape(-1)])
    flat.append(lax.bitcast_convert_type(ln_odd, BF16).reshape(-1))
    packed = jnp.concatenate(flat)
    n_real = packed.shape[0]
    rows = -(-n_real // (PACK_COLS * 32)) * 32
    packed = jnp.pad(packed, (0, rows * PACK_COLS - n_real)).reshape(rows, PACK_COLS)
    gathered = _gather_chip_shards(packed, name="gather_weights").reshape(N_CHIPS, -1)
    full, off = {}, 0
    for n in _SHARDED:
        size = int(np.prod(weights[n].shape))
        full[n] = _unshard(n, gathered[:, off:off + size].reshape((N_CHIPS,) + weights[n].shape))
        off += size
    words = 4 // jnp.dtype(BF16).itemsize
    ln_bits = gathered[:, off:off + words * ln_odd.shape[0]].reshape(N_CHIPS, ln_odd.shape[0], words)
    ln_all = lax.bitcast_convert_type(ln_bits, F32).reshape(N_CHIPS, 2, 2, D_MODEL // N_CHIPS)
    odd_g_full = ln_all[:, 0].transpose(1, 0, 2).reshape(2, D_MODEL)
    odd_b_full = ln_all[:, 1].transpose(1, 0, 2).reshape(2, D_MODEL)

    cos_m, sin_m, cos_s, sin_s = _rope_tables(s)
    bf_tiles = jnp.pad(even_b_f, ((0, 0), (FL_LANE, LANES - FL_LANE - FOX_HEADS)))
    sink_tiles = jnp.broadcast_to(odd_sinks[:, :, None, None], (2, SWA_HEADS, 1, LANES))

    saved = []
    x_f, x_b = xl, xl.astype(BF16)
    for layer in range(DEPTH):
        j = layer // 2
        ln = f"L{layer}"
        if layer % 2 == 0:
            w_in_p, w_in_t, uq_p, uq_t, ukv_p, ukv_t = _even_weights(full["even_w_in"][j], full["even_w_uq"][j], full["even_w_ukv"][j])
            w_out, w_out_t = full["even_w_out"][j], full["even_w_out"][j].T
            qg, kg, bft = even_q_norm[j][None], even_kv_norm[j][None], bf_tiles[j][None]
            h = _mm(x_b, w_in_p, out_dtype=F32, name=ln + "_in")
            q, k, v, qn, kvn, logf = _even_mid_fwd(h, qg, kg, bft, uq_p, ukv_p, cos_m, sin_m, name=ln + "_mid")
            cum = _cumsum(logf, reverse=False, name=ln + "_cum")[:, FL_LANE:FL_LANE + FOX_HEADS].T
            ccol, crow = cum[:, :, None], cum[:, None, :]
            qh, kh, vh = _heads(q, MLA_HEADS), _heads(k, MLA_HEADS), _heads(v, MLA_HEADS)
            o_mla, lse_mla = _attn_fwd(qh, kh, vh, None, None, scale=(MLA_NOPE + MLA_ROPE) ** -0.5, name=ln + "_mla")
            fq, fk, fv = (_heads(h[:, a:a + 512].astype(BF16), FOX_HEADS) for a in (E_FQ, E_FK, E_FV))
            o_fox, lse_fox = _attn_fwd(fq, fk, fv, ccol, crow, scale=FOX_DIM ** -0.5, name=ln + "_fox")
            o = jnp.concatenate([_merge(o_mla), _merge(o_fox)], axis=1)
            g_ln, b_ln = even_ln_g[j][None], even_ln_b[j][None]
            x_n, x_nb, z, xh, rstd = _out_fwd(o, h, x_f, w_out, g_ln, b_ln, name=ln + "_out")
            saved.append(dict(h=h, x_b=x_b, qn=qn, kvn=kvn, qh=qh, kh=kh, vh=vh, fq=fq, fk=fk, fv=fv, ccol=ccol, crow=crow,
                              o_mla=o_mla, o_fox=o_fox, lse_mla=lse_mla, lse_fox=lse_fox, o=o, z=z, xh=xh, rstd=rstd,
                              w_in_t=w_in_t, uq_t=uq_t, ukv_t=ukv_t, w_out_t=w_out_t, qg=qg, kg=kg, bft=bft, g_ln=g_ln))
        else:
            w_in_p, w_in_t = _odd_weights(full["odd_w_in"][j])
            w_out, w_out_t = full["odd_w_out"][j], full["odd_w_out"][j].T
            h = _mm(x_b, w_in_p, out_dtype=F32, name=ln + "_in")
            q, k, v = _odd_rope(h, cos_s, sin_s, name=ln + "_rope")
            qh = _heads(q, SWA_HEADS)
            kp = jnp.pad(_heads(k, SWA_KV_HEADS), ((0, 0), (WINDOW, 0), (0, 0)))
            vp = jnp.pad(_heads(v, SWA_KV_HEADS), ((0, 0), (WINDOW, 0), (0, 0)))
            o_h, lse = _swa_fwd(qh, kp, vp, sink_tiles[j], name=ln + "_swa")
            o = _merge(o_h)
            g_ln, b_ln = odd_g_full[j][None], odd_b_full[j][None]
            x_n, x_nb, z, xh, rstd = _out_fwd(o, h, x_f, w_out, g_ln, b_ln, name=ln + "_out")
            saved.append(dict(h=h, x_b=x_b, qh=qh, kp=kp, vp=vp, o_h=o_h, lse=lse, o=o, z=z, xh=xh, rstd=rstd,
                              w_in_t=w_in_t, w_out_t=w_out_t, g_ln=g_ln))
        x_f, x_b = x_n, x_nb

    dxn, loss_tile = _loss_grad(x_f, tgt, name="loss")

    grads = {n: [None, None] for n in names}
    for layer in reversed(range(DEPTH)):
        j = layer // 2
        ln = f"L{layer}"
        sv = saved[layer]
        dr, dy, do, dgate, dg_ln, db_ln = _out_bwd(dxn, sv["xh"], sv["rstd"], sv["g_ln"], sv["w_out_t"], sv["o"], sv["h"], name=ln + "_outb")
        dw_out = _mm(sv["z"].T, dy, out_dtype=F32, name=ln + "_dwout")
        if layer % 2 == 0:
            do_mla, do_fox = _heads(do[:, :512], MLA_HEADS), _heads(do[:, 512:], FOX_HEADS)
            dqh, dkh, dvh = _attn_bwd(sv["qh"], sv["kh"], sv["vh"], sv["o_mla"], do_mla, sv["lse_mla"], None, None,
                                      scale=(MLA_NOPE + MLA_ROPE) ** -0.5, name=ln + "_mlab")
            dfq, dfk, dfv, dcrow, dccol = _attn_bwd(sv["fq"], sv["fk"], sv["fv"], sv["o_fox"], do_fox, sv["lse_fox"], sv["ccol"], sv["crow"],
                                             scale=FOX_DIM ** -0.5, name=ln + "_foxb")
            dcum = jnp.pad((dcrow[:, 0, :] + dccol[:, :, 0]).T, ((0, 0), (FL_LANE, LANES - FL_LANE - FOX_HEADS)))
            dlogf = _cumsum(dcum, reverse=True, name=ln + "_cumb")
            dh_small, dq_pre, dqg, dkg, dbf = _even_mid_bwd(
                sv["h"], _merge(dqh), _merge(dkh), _merge(dvh).astype(BF16), dlogf, sv["qg"], sv["kg"], sv["bft"],
                sv["uq_t"], sv["ukv_t"], cos_m, sin_m, name=ln + "_midb")
            duq_p = _mm(sv["qn"].T, dq_pre, out_dtype=F32, name=ln + "_dwuq")
            dkv_cat = jnp.concatenate([_merge(dkh).astype(BF16), _merge(dvh).astype(BF16)], axis=1)
            dukv_p = _mm(sv["kvn"].T, dkv_cat, out_dtype=F32, name=ln + "_dwukv")
            dh = jnp.concatenate([dgate, _merge(dfq).astype(BF16), _merge(dfk).astype(BF16), _merge(dfv).astype(BF16), dh_small], axis=1)
            dw_in_p = _mm(sv["x_b"].T, dh, out_dtype=F32, name=ln + "_dwin")
            dw_in, duq, dukv = _even_weight_grads(dw_in_p, duq_p, dukv_p)
            for n, val in (("even_w_in", dw_in), ("even_w_uq", duq), ("even_w_ukv", dukv), ("even_w_out", dw_out),
                           ("even_q_norm", dqg[0]), ("even_kv_norm", dkg[0]), ("even_b_f", dbf[0, FL_LANE:FL_LANE + FOX_HEADS]),
                           ("even_ln_g", dg_ln[0]), ("even_ln_b", db_ln[0])):
                grads[n][j] = val
        else:
            dq_h, dkp, dvp, dsink = _swa_bwd(sv["qh"], sv["kp"], sv["vp"], sink_tiles[j], sv["o_h"], _heads(do, SWA_HEADS), sv["lse"], name=ln + "_swab")
            dq_r, dk_r = _odd_rope_bwd(_merge(dq_h), _merge(dkp[:, WINDOW:]), cos_s, sin_s, name=ln + "_ropeb")
            dh = jnp.concatenate([dgate, dq_r, dk_r, _merge(dvp[:, WINDOW:]).astype(BF16)], axis=1)
            dw_in_p = _mm(sv["x_b"].T, dh, out_dtype=F32, name=ln + "_dwin")
            for n, val in (("odd_w_in", _odd_weight_grads(dw_in_p)), ("odd_w_out", dw_out), ("odd_sinks", dsink[:, 0, 0]),
                           ("odd_ln_g", dg_ln[0]), ("odd_ln_b", db_ln[0])):
                grads[n][j] = val
        dxn = _mm(dh, sv["w_in_t"], out_dtype=F32, name=ln + "_dx", res=dr, res_scale=ALPHA)
    grad_x = dxn[None]
    grads = {n: jnp.stack(v) for n, v in grads.items()}

    gp = jnp.concatenate([_shard_all(n, grads[n]) for n in _SHARDED], axis=1)
    g_real = gp.shape[1]
    g_rows = -(-g_real // (PACK_COLS * 16)) * 16
    gp = jnp.pad(gp, ((0, 0), (0, g_rows * PACK_COLS - g_real))).reshape(N_CHIPS, 2, g_rows // 2, PACK_COLS)
    from_sib = _swap_halves(gp, name="grad_swap")
    chip_sum = _add_sibling(gp, from_sib, ac.astype(jnp.int32).reshape(1), name="grad_add_sibling")
    from_chips = _scatter_to_chips(chip_sum, name="grad_scatter")
    mine = _add_chips(chip_sum, from_chips, chip.astype(jnp.int32).reshape(1), name="grad_add_chips")
    red = _join_halves(mine, name="grad_join").reshape(-1)
    gshard, off = {}, 0
    for n in _SHARDED:
        size = int(np.prod(weights[n].shape))
        gshard[n] = red[off:off + size].reshape(weights[n].shape)
        off += size

    small = [n for n in names if n not in _SHARDED]
    sv_flat = jnp.concatenate([grads[n].reshape(-1) for n in small] + [loss_tile[0, :1]])
    sv_real = sv_flat.shape[0]
    sv_rows = -(-sv_real // (PACK_COLS * 8)) * 8
    sv_sum = _all_reduce_small(jnp.pad(sv_flat, (0, sv_rows * PACK_COLS - sv_real)).reshape(sv_rows, PACK_COLS), name="small_all_reduce").reshape(-1)
    off = 0
    for n in small:
        size = int(np.prod(grads[n].shape))
        gfull = sv_sum[off:off + size].reshape(grads[n].shape)
        off += size
        if n in ("odd_ln_g", "odd_ln_b"):
            gfull = lax.dynamic_slice_in_dim(gfull, chip * (D_MODEL // N_CHIPS), D_MODEL // N_CHIPS, axis=1)
        gshard[n] = gfull
    loss = sv_sum[off]

    deltas, new_m, new_v = {}, {}, {}
    for n in names:
        deltas[n], new_m[n], new_v[n] = _adamw(weights[n], gshard[n], mom_m[n], mom_v[n], name="adamw_" + n)
    return (loss, grad_x, *[gshard[n] for n in names], *[deltas[n] for n in names],
            *[new_m[n] for n in names], *[new_v[n] for n in names])
```

```python
import functools

import jax
import jax.numpy as jnp
import numpy as np
from jax import lax
from jax.experimental import pallas as pl
from jax.experimental.pallas import tpu as pltpu

F32 = jnp.float32
BF16 = jnp.bfloat16
MESH = pl.DeviceIdType.MESH

D_MODEL = 1024
DEPTH = 4
ROPE_THETA = 10000.0
MLA_HEADS, MLA_NOPE, MLA_ROPE, MLA_V = 8, 64, 32, 64
MLA_Q_RANK, MLA_KV_RANK = 256, 128
FOX_HEADS, FOX_DIM = 8, 64
SWA_HEADS, SWA_KV_HEADS, SWA_DIM, WINDOW = 16, 2, 64, 128
SWA_GROUP = SWA_HEADS // SWA_KV_HEADS
RMS_EPS = 1e-6
LN_EPS = 1e-5
ALPHA = (2 * DEPTH) ** 0.25
EVEN_IN = 2984
ODD_IN = 2304
ADAM_LR, ADAM_B1, ADAM_B2, ADAM_EPS, ADAM_WD, ADAM_STEP = 0.001, 0.9, 0.999, 1e-08, 0.01, 10

LANES = 128
HEAD_PAD = 128
N_CHIPS = 4
N_DEV = 8
E_GATE, E_FQ, E_FK, E_FV, E_SMALL = 0, 1024, 1536, 2048, 2560
E_PAD_IN = 3072
KPE_LANE = 64
FL_LANE = 96
O_GATE, O_Q, O_K, O_V = 0, 1024, 2048, 2176

_ARB = "arbitrary"
_PAR = "parallel"


def _cparams(sem):
    return pltpu.CompilerParams(dimension_semantics=sem)


def _pick(n, cands):
    for c in cands:
        if n % c == 0:
            return c
    return n


def _mm(a, b, *, out_dtype, name, res=None, res_scale=1.0):
    m, k = a.shape
    _, n = b.shape
    tm = _pick(m, (512, 256, 128))
    tn = _pick(n, (1024, 768, 512, 384, 256, 128))
    tk = _pick(k, (1024, 512, 256, 128))
    nk = k // tk

    def body(*refs):
        if res is None:
            a_ref, b_ref, o_ref, acc_ref = refs
        else:
            a_ref, b_ref, r_ref, o_ref, acc_ref = refs
        kk = pl.program_id(2)

        @pl.when(kk == 0)
        def _():
            acc_ref[...] = jnp.zeros_like(acc_ref)

        acc_ref[...] += jnp.dot(a_ref[...].astype(BF16), b_ref[...].astype(BF16), preferred_element_type=F32)

        @pl.when(kk == nk - 1)
        def _():
            r = acc_ref[...]
            if res is not None:
                r = r + res_scale * r_ref[...]
            o_ref[...] = r.astype(o_ref.dtype)

    in_specs = [pl.BlockSpec((tm, tk), lambda i, j, kk: (i, kk)), pl.BlockSpec((tk, tn), lambda i, j, kk: (kk, j))]
    args = [a, b]
    if res is not None:
        in_specs.append(pl.BlockSpec((tm, tn), lambda i, j, kk: (i, j)))
        args.append(res)
    return pl.pallas_call(
        body, name=name, grid=(m // tm, n // tn, nk), in_specs=in_specs,
        out_specs=pl.BlockSpec((tm, tn), lambda i, j, kk: (i, j)),
        out_shape=jax.ShapeDtypeStruct((m, n), out_dtype),
        scratch_shapes=[pltpu.VMEM((tm, tn), F32)],
        compiler_params=_cparams((_PAR, _PAR, _ARB)),
    )(*args)


def _rope_tile(t, cos, sin, half):
    w = t.shape[-1]
    lane = lax.broadcasted_iota(jnp.int32, t.shape, 1)
    first = (lane % (2 * half)) < half
    sw = jnp.where(first, pltpu.roll(t, w - half, 1), pltpu.roll(t, half, 1))
    return t * cos + sw * sin


def _sigmoid(x):
    return 1.0 / (1.0 + jnp.exp(-x))


def _lane_mask(shape, lo, hi):
    lane = lax.broadcasted_iota(jnp.int32, shape, 1)
    return (lane >= lo) & (lane < hi)


def _rms(x, g):
    r = lax.rsqrt(jnp.mean(x * x, axis=-1, keepdims=True) + RMS_EPS)
    u = x * r
    return u, r, u * g


def _rms_bwd(dy, u, r, g):
    dyg = dy * g
    dx = r * (dyg - u * jnp.mean(dyg * u, axis=-1, keepdims=True))
    return dx, jnp.sum(dy * u, axis=0, keepdims=True)


def _even_mid_fwd(h, qg, kg, bf_tile, w_uq, w_ukv, cos, sin, *, name):
    s = h.shape[0]
    tb = _pick(s, (512, 256, 128))
    nq = MLA_HEADS * HEAD_PAD

    def body(h_ref, qg_ref, kg_ref, bf_ref, wuq_ref, wukv_ref, cos_ref, sin_ref,
             q_ref, k_ref, v_ref, qn_ref, kvn_ref, lf_ref):
        hb = h_ref[...]
        cq, ckv, misc = hb[:, :MLA_Q_RANK], hb[:, MLA_Q_RANK:MLA_Q_RANK + MLA_KV_RANK], hb[:, MLA_Q_RANK + MLA_KV_RANK:]
        cs, sn = cos_ref[...], sin_ref[...]
        _, _, qn = _rms(cq, qg_ref[...])
        qn = qn.astype(BF16)
        qn_ref[...] = qn
        q = jnp.dot(qn, wuq_ref[...], preferred_element_type=F32)
        _, _, kvn = _rms(ckv, kg_ref[...])
        kvn = kvn.astype(BF16)
        kvn_ref[...] = kvn
        kv = jnp.dot(kvn, wukv_ref[...], preferred_element_type=F32)
        kpe = jnp.where(_lane_mask(misc.shape, KPE_LANE, KPE_LANE + MLA_ROPE), _rope_tile(misc, cs, sn, MLA_ROPE // 2), 0.0)
        for hd in range(MLA_HEADS):
            sl = slice(hd * HEAD_PAD, (hd + 1) * HEAD_PAD)
            q_ref[:, sl] = _rope_tile(q[:, sl], cs, sn, MLA_ROPE // 2).astype(BF16)
            k_ref[:, sl] = (kv[:, sl] + kpe).astype(BF16)
        v_ref[...] = kv[:, nq:].astype(BF16)
        xf = misc + bf_ref[...]
        logf = jnp.minimum(xf, 0.0) - jnp.log(1.0 + jnp.exp(-jnp.abs(xf)))
        lf_ref[...] = jnp.where(_lane_mask(misc.shape, FL_LANE, FL_LANE + FOX_HEADS), logf, 0.0)

    full = lambda a: pl.BlockSpec(a.shape, lambda i: (0,) * a.ndim)
    rows = lambda w, c=0: pl.BlockSpec((tb, w), lambda i: (i, c))
    return pl.pallas_call(
        body, name=name, grid=(s // tb,),
        in_specs=[rows(512, E_SMALL // 512), full(qg), full(kg), full(bf_tile), full(w_uq), full(w_ukv), rows(LANES), rows(LANES)],
        out_specs=[rows(nq), rows(nq), rows(MLA_HEADS * MLA_V), rows(MLA_Q_RANK), rows(MLA_KV_RANK), rows(LANES)],
        out_shape=[jax.ShapeDtypeStruct((s, nq), BF16), jax.ShapeDtypeStruct((s, nq), BF16),
                   jax.ShapeDtypeStruct((s, MLA_HEADS * MLA_V), BF16), jax.ShapeDtypeStruct((s, MLA_Q_RANK), BF16),
                   jax.ShapeDtypeStruct((s, MLA_KV_RANK), BF16), jax.ShapeDtypeStruct((s, LANES), F32)],
        compiler_params=_cparams((_PAR,)),
    )(h, qg, kg, bf_tile, w_uq, w_ukv, cos, sin)


def _even_mid_bwd(h, dq, dk, dv, dlogf, qg, kg, bf_tile, w_uq_t, w_ukv_t, cos, sin, *, name):
    s = h.shape[0]
    tb = _pick(s, (256, 128))
    nq = MLA_HEADS * HEAD_PAD

    def body(h_ref, dq_ref, dk_ref, dv_ref, dlf_ref, qg_ref, kg_ref, bf_ref, wuqt_ref, wukvt_ref, cos_ref, sin_ref,
             dh_ref, dqp_ref, dqg_ref, dkg_ref, dbf_ref):
        @pl.when(pl.program_id(0) == 0)
        def _():
            dqg_ref[...] = jnp.zeros_like(dqg_ref)
            dkg_ref[...] = jnp.zeros_like(dkg_ref)
            dbf_ref[...] = jnp.zeros_like(dbf_ref)

        hb = h_ref[...]
        cq, ckv, misc = hb[:, :MLA_Q_RANK], hb[:, MLA_Q_RANK:MLA_Q_RANK + MLA_KV_RANK], hb[:, MLA_Q_RANK + MLA_KV_RANK:]
        cs, sn = cos_ref[...], -sin_ref[...]
        dkpe = jnp.zeros(misc.shape, F32)
        for hd in range(MLA_HEADS):
            sl = slice(hd * HEAD_PAD, (hd + 1) * HEAD_PAD)
            dqp_ref[:, sl] = _rope_tile(dq_ref[:, sl], cs, sn, MLA_ROPE // 2).astype(BF16)
            dkpe = dkpe + dk_ref[:, sl]
        dqn = jnp.dot(dqp_ref[...], wuqt_ref[...], preferred_element_type=F32)
        uq, rq, _ = _rms(cq, qg_ref[...])
        dcq, dqg = _rms_bwd(dqn, uq, rq, qg_ref[...])
        dqg_ref[...] += dqg
        dkv = jnp.concatenate([dk_ref[...].astype(BF16), dv_ref[...]], axis=1)
        dkvn = jnp.dot(dkv, wukvt_ref[...], preferred_element_type=F32)
        uk, rk, _ = _rms(ckv, kg_ref[...])
        dckv, dkg = _rms_bwd(dkvn, uk, rk, kg_ref[...])
        dkg_ref[...] += dkg
        dmisc = jnp.where(_lane_mask(misc.shape, KPE_LANE, KPE_LANE + MLA_ROPE), _rope_tile(dkpe, cs, sn, MLA_ROPE // 2), 0.0)
        dfl = jnp.where(_lane_mask(misc.shape, FL_LANE, FL_LANE + FOX_HEADS), dlf_ref[...] * _sigmoid(-(misc + bf_ref[...])), 0.0)
        dbf_ref[...] += jnp.sum(dfl, axis=0, keepdims=True)
        dh_ref[:, :MLA_Q_RANK] = dcq.astype(BF16)
        dh_ref[:, MLA_Q_RANK:MLA_Q_RANK + MLA_KV_RANK] = dckv.astype(BF16)
        dh_ref[:, MLA_Q_RANK + MLA_KV_RANK:] = (dmisc + dfl).astype(BF16)

    full = lambda a: pl.BlockSpec(a.shape, lambda i: (0,) * a.ndim)
    rows = lambda w, c=0: pl.BlockSpec((tb, w), lambda i: (i, c))
    return pl.pallas_call(
        body, name=name, grid=(s // tb,),
        in_specs=[rows(512, E_SMALL // 512), rows(nq), rows(nq), rows(MLA_HEADS * MLA_V), rows(LANES), full(qg), full(kg),
                  full(bf_tile), full(w_uq_t), full(w_ukv_t), rows(LANES), rows(LANES)],
        out_specs=[rows(512), rows(nq), full(qg), full(kg), full(bf_tile)],
        out_shape=[jax.ShapeDtypeStruct((s, 512), BF16), jax.ShapeDtypeStruct((s, nq), BF16),
                   jax.ShapeDtypeStruct(qg.shape, F32), jax.ShapeDtypeStruct(kg.shape, F32),
                   jax.ShapeDtypeStruct(bf_tile.shape, F32)],
        compiler_params=_cparams((_ARB,)),
    )(h, dq, dk, dv, dlogf, qg, kg, bf_tile, w_uq_t, w_ukv_t, cos, sin)


def _cumsum(x, *, reverse, name):
    s = x.shape[0]
    tb = _pick(s, (512, 256, 128))
    nb = s // tb

    def body(x_ref, o_ref, carry_ref):
        @pl.when(pl.program_id(0) == 0)
        def _():
            carry_ref[...] = jnp.zeros_like(carry_ref)

        xv = x_ref[...]
        r = lax.broadcasted_iota(jnp.int32, (tb, tb), 0)
        c = lax.broadcasted_iota(jnp.int32, (tb, tb), 1)
        tri = jnp.where((c >= r) if reverse else (c <= r), 1.0, 0.0).astype(BF16)
        hi = xv.astype(BF16)
        r1 = xv - hi.astype(F32)
        mid = r1.astype(BF16)
        lo = (r1 - mid.astype(F32)).astype(BF16)
        cs = (jnp.dot(tri, hi, preferred_element_type=F32) + jnp.dot(tri, mid, preferred_element_type=F32)
              + jnp.dot(tri, lo, preferred_element_type=F32)) + carry_ref[...]
        o_ref[...] = cs
        carry_ref[...] = cs[0:1, :] if reverse else cs[tb - 1:tb, :]

    imap = (lambda i: (nb - 1 - i, 0)) if reverse else (lambda i: (i, 0))
    return pl.pallas_call(
        body, name=name, grid=(nb,), in_specs=[pl.BlockSpec((tb, LANES), imap)],
        out_specs=pl.BlockSpec((tb, LANES), imap), out_shape=jax.ShapeDtypeStruct(x.shape, F32),
        scratch_shapes=[pltpu.VMEM((1, LANES), F32)], compiler_params=_cparams((_ARB,)),
    )(x)


_NT = (((1,), (1,)), ((), ()))
_TN = (((0,), (0,)), ((), ()))


def _causal_scores(q, k, ccol, crow, qi, ki, t, scale):
    sc = lax.dot_general(q, k, _NT, preferred_element_type=F32) * scale
    if ccol is not None:
        sc = sc + ccol - crow
    row = lax.broadcasted_iota(jnp.int32, (t, t), 0) + qi * t
    col = lax.broadcasted_iota(jnp.int32, (t, t), 1) + ki * t
    return jnp.where(col <= row, sc, -jnp.inf)


def _attn_fwd(q, k, v, ccol, crow, *, scale, name):
    nh, s, dk = q.shape
    dv = v.shape[-1]
    t = _pick(s, (512, 256, 128))
    nb = s // t
    bias = ccol is not None

    def body(*refs):
        if bias:
            q_ref, k_ref, v_ref, cc_ref, cr_ref, o_ref, lse_ref = refs
        else:
            q_ref, k_ref, v_ref, o_ref, lse_ref = refs
        qi = pl.program_id(1)
        qb = q_ref[0]

        def scores(j):
            rows = pl.ds(pl.multiple_of(j * t, t), t)
            sc = lax.dot_general(qb, k_ref[0, rows, :], _NT, preferred_element_type=F32) * scale
            return sc - cr_ref[0, j] if bias else sc

        def update(sc, j, m, l, acc):
            rows = pl.ds(pl.multiple_of(j * t, t), t)
            m_new = jnp.maximum(m, jnp.max(sc, axis=1, keepdims=True))
            a = jnp.exp(m - m_new)
            p = jnp.exp(sc - m_new)
            l = a * l + jnp.sum(p, axis=1, keepdims=True)
            acc = a * acc + jnp.dot(p.astype(BF16), v_ref[0, rows, :], preferred_element_type=F32)
            return m_new, l, acc

        def step(j, carry):
            sc, m, l, acc = carry
            nxt = scores(j + 1)
            return (nxt,) + update(sc, j, m, l, acc)

        init = (scores(0), jnp.full((t, 1), -jnp.inf, F32), jnp.zeros((t, 1), F32), jnp.zeros((t, dv), F32))
        sc, m, l, acc = lax.fori_loop(0, qi, step, init)
        row = lax.broadcasted_iota(jnp.int32, (t, t), 0)
        col = lax.broadcasted_iota(jnp.int32, (t, t), 1)
        m, l, acc = update(jnp.where(col <= row, sc, -jnp.inf), qi, m, l, acc)
        o_ref[0] = acc / l
        lse = m + jnp.log(l)
        lse_ref[0] = lse + cc_ref[0] if bias else lse

    qmap = lambda h, i: (h, i, 0)
    hmap = lambda h, i: (h, 0, 0)
    in_specs = [pl.BlockSpec((1, t, dk), qmap), pl.BlockSpec((1, s, dk), hmap), pl.BlockSpec((1, s, dv), hmap)]
    args = [q, k, v]
    if bias:
        in_specs += [pl.BlockSpec((1, t, 1), qmap), pl.BlockSpec((1, nb, 1, t), lambda h, i: (h, 0, 0, 0))]
        args += [ccol, crow.reshape(nh, nb, 1, t)]
    return pl.pallas_call(
        body, name=name, grid=(nh, nb), in_specs=in_specs,
        out_specs=[pl.BlockSpec((1, t, dv), qmap), pl.BlockSpec((1, t, 1), qmap)],
        out_shape=[jax.ShapeDtypeStruct((nh, s, dv), F32), jax.ShapeDtypeStruct((nh, s, 1), F32)],
        compiler_params=_cparams((_PAR, _PAR)),
    )(*args)


def _attn_bwd(q, k, v, o, do, lse, ccol, crow, *, scale, name):
    nh, s, dk = q.shape
    dv = v.shape[-1]
    t = _pick(s, (512, 256, 128))
    nb = s // t
    bias = ccol is not None

    def body(*refs):
        if bias:
            (q_ref, k_ref, v_ref, o_ref, do_ref, lse_ref, cc_ref, cr_ref,
             dq_ref, dk_ref, dv_ref, dc_ref, dcc_ref, dk_s, dv_s, dc_s) = refs
        else:
            q_ref, k_ref, v_ref, o_ref, do_ref, lse_ref, dq_ref, dk_ref, dv_ref, dk_s, dv_s = refs
        ki, qi = pl.program_id(1), pl.program_id(2)

        @pl.when((ki == 0) & (qi == 0))
        def _():
            dq_ref[...] = jnp.zeros_like(dq_ref)
            if bias:
                dcc_ref[...] = jnp.zeros_like(dcc_ref)

        @pl.when(qi == ki)
        def _():
            dk_s[...] = jnp.zeros_like(dk_s)
            dv_s[...] = jnp.zeros_like(dv_s)
            if bias:
                dc_s[...] = jnp.zeros_like(dc_s)

        @pl.when(qi >= ki)
        def _():
            qb, kb, dob = q_ref[0], k_ref[0], do_ref[0]
            sc = _causal_scores(qb, kb, cc_ref[0] if bias else None, cr_ref[0] if bias else None, qi, ki, t, scale)
            p = jnp.exp(sc - lse_ref[0])
            delta = jnp.sum(dob.astype(F32) * o_ref[0], axis=1, keepdims=True)
            dp = lax.dot_general(dob, v_ref[0], _NT, preferred_element_type=F32)
            ds = p * (dp - delta)
            dv_s[...] += lax.dot_general(p.astype(BF16), dob, _TN, preferred_element_type=F32)
            dsb = (ds * scale).astype(BF16)
            dk_s[...] += lax.dot_general(dsb, qb, _TN, preferred_element_type=F32)
            rows = pl.ds(pl.multiple_of(qi * t, t), t)
            dq_ref[0, rows, :] += jnp.dot(dsb, kb, preferred_element_type=F32)
            if bias:
                dc_s[...] -= jnp.sum(ds, axis=0, keepdims=True)
                dcc_ref[0, rows, :] += jnp.sum(ds, axis=1, keepdims=True)

        @pl.when(qi == nb - 1)
        def _():
            dk_ref[0] = dk_s[...]
            dv_ref[0] = dv_s[...]
            if bias:
                dc_ref[0] = dc_s[...]

    qmap = lambda h, j, i: (h, jnp.maximum(i, j), 0)
    kmap = lambda h, j, i: (h, j, 0)
    in_specs = [pl.BlockSpec((1, t, dk), qmap), pl.BlockSpec((1, t, dk), kmap), pl.BlockSpec((1, t, dv), kmap),
                pl.BlockSpec((1, t, dv), qmap), pl.BlockSpec((1, t, dv), qmap), pl.BlockSpec((1, t, 1), qmap)]
    args = [q, k, v, o, do, lse]
    out_specs = [pl.BlockSpec((1, s, dk), lambda h, j, i: (h, 0, 0)), pl.BlockSpec((1, t, dk), kmap), pl.BlockSpec((1, t, dv), kmap)]
    out_shape = [jax.ShapeDtypeStruct((nh, s, dk), F32), jax.ShapeDtypeStruct((nh, s, dk), F32), jax.ShapeDtypeStruct((nh, s, dv), F32)]
    scratch = [pltpu.VMEM((t, dk), F32), pltpu.VMEM((t, dv), F32)]
    if bias:
        in_specs += [pl.BlockSpec((1, t, 1), qmap), pl.BlockSpec((1, 1, t), lambda h, j, i: (h, 0, j))]
        args += [ccol, crow]
        out_specs += [pl.BlockSpec((1, 1, t), lambda h, j, i: (h, 0, j)), pl.BlockSpec((1, s, 1), lambda h, j, i: (h, 0, 0))]
        out_shape += [jax.ShapeDtypeStruct((nh, 1, s), F32), jax.ShapeDtypeStruct((nh, s, 1), F32)]
        scratch.append(pltpu.VMEM((1, t), F32))
    return pl.pallas_call(
        body, name=name, grid=(nh, nb, nb), in_specs=in_specs, out_specs=out_specs, out_shape=out_shape,
        scratch_shapes=scratch, compiler_params=_cparams((_PAR, _ARB, _ARB)),
    )(*args)


def _swa_scores(qb, kw, start, t, scale):
    sc = lax.dot_general(qb, kw, _NT, preferred_element_type=F32) * scale
    row = lax.broadcasted_iota(jnp.int32, (t, t + WINDOW), 0)
    col = lax.broadcasted_iota(jnp.int32, (t, t + WINDOW), 1)
    diff = row - col + WINDOW
    valid = (diff >= 0) & (diff < WINDOW) & (col + start >= WINDOW)
    return jnp.where(valid, sc, -jnp.inf)


def _swa_fwd(q, kp, vp, sink, *, name):
    nh, s, d = q.shape
    t = _pick(s, (256, 128))
    scale = d ** -0.5

    def body(q_ref, k_ref, v_ref, sk_ref, o_ref, lse_ref):
        start = pl.multiple_of(pl.program_id(1) * t, t)
        kw = k_ref[0, pl.ds(start, t + WINDOW), :]
        vw = v_ref[0, pl.ds(start, t + WINDOW), :]
        sc = _swa_scores(q_ref[0], kw, start, t, scale)
        snk = sk_ref[0][:, 0:1]
        m = jnp.maximum(jnp.max(sc, axis=1, keepdims=True), snk)
        e = jnp.exp(sc - m)
        l = jnp.sum(e, axis=1, keepdims=True) + jnp.exp(snk - m)
        o_ref[0] = jnp.dot((e / l).astype(BF16), vw, preferred_element_type=F32)
        lse_ref[0] = m + jnp.log(l)

    qmap = lambda h, i: (h, i, 0)
    kvspec = pl.BlockSpec((1, s + WINDOW, d), lambda h, i: (h // SWA_GROUP, 0, 0))
    return pl.pallas_call(
        body, name=name, grid=(nh, s // t),
        in_specs=[pl.BlockSpec((1, t, d), qmap), kvspec, kvspec, pl.BlockSpec((1, 1, LANES), lambda h, i: (h, 0, 0))],
        out_specs=[pl.BlockSpec((1, t, d), qmap), pl.BlockSpec((1, t, 1), qmap)],
        out_shape=[jax.ShapeDtypeStruct((nh, s, d), F32), jax.ShapeDtypeStruct((nh, s, 1), F32)],
        compiler_params=_cparams((_PAR, _PAR)),
    )(q, kp, vp, sink)


def _swa_bwd(q, kp, vp, sink, o, do, lse, *, name):
    nh, s, d = q.shape
    nkv = kp.shape[0]
    t = _pick(s, (256, 128))
    scale = d ** -0.5

    def body(q_ref, k_ref, v_ref, sk_ref, o_ref, do_ref, lse_ref, dq_ref, dk_ref, dv_ref, dsk_ref):
        hh, i = pl.program_id(0), pl.program_id(1)
        start = pl.multiple_of(i * t, t)

        @pl.when((hh % SWA_GROUP == 0) & (i == 0))
        def _():
            dk_ref[...] = jnp.zeros_like(dk_ref)
            dv_ref[...] = jnp.zeros_like(dv_ref)

        @pl.when(i == 0)
        def _():
            dsk_ref[...] = jnp.zeros_like(dsk_ref)

        win = pl.ds(start, t + WINDOW)
        qb, dob = q_ref[0], do_ref[0]
        kw, vw = k_ref[0, win, :], v_ref[0, win, :]
        sc = _swa_scores(qb, kw, start, t, scale)
        lse_b = lse_ref[0]
        p = jnp.exp(sc - lse_b)
        delta = jnp.sum(dob.astype(F32) * o_ref[0], axis=1, keepdims=True)
        dp = lax.dot_general(dob, vw, _NT, preferred_element_type=F32)
        ds = p * (dp - delta)
        dsb = (ds * scale).astype(BF16)
        dq_ref[0] = jnp.dot(dsb, kw, preferred_element_type=F32)
        dk_ref[0, win, :] += lax.dot_general(dsb, qb, _TN, preferred_element_type=F32)
        dv_ref[0, win, :] += lax.dot_general(p.astype(BF16), dob, _TN, preferred_element_type=F32)
        psink = jnp.exp(sk_ref[0][:, 0:1] - lse_b)
        dsk_ref[0] += jnp.broadcast_to(-jnp.sum(psink * delta, axis=0, keepdims=True), (1, LANES))

    qmap = lambda h, i: (h, i, 0)
    kvspec = pl.BlockSpec((1, s + WINDOW, d), lambda h, i: (h // SWA_GROUP, 0, 0))
    skspec = pl.BlockSpec((1, 1, LANES), lambda h, i: (h, 0, 0))
    return pl.pallas_call(
        body, name=name, grid=(nh, s // t),
        in_specs=[pl.BlockSpec((1, t, d), qmap), kvspec, kvspec, skspec, pl.BlockSpec((1, t, d), qmap),
                  pl.BlockSpec((1, t, d), qmap), pl.BlockSpec((1, t, 1), qmap)],
        out_specs=[pl.BlockSpec((1, t, d), qmap), kvspec, kvspec, skspec],
        out_shape=[jax.ShapeDtypeStruct((nh, s, d), F32), jax.ShapeDtypeStruct((nkv, s + WINDOW, d), F32),
                   jax.ShapeDtypeStruct((nkv, s + WINDOW, d), F32), jax.ShapeDtypeStruct((nh, 1, LANES), F32)],
        compiler_params=_cparams((_ARB, _ARB)),
    )(q, kp, vp, sink, o, do, lse)


def _odd_rope(h, cos, sin, *, name):
    s = h.shape[0]
    tb = _pick(s, (512, 256, 128))
    nq = SWA_HEADS * SWA_DIM

    def body(q_ref, kv_ref, cos_ref, sin_ref, qo_ref, ko_ref, vo_ref):
        cs, sn = cos_ref[...], sin_ref[...]
        for j in range(nq // LANES):
            sl = slice(j * LANES, (j + 1) * LANES)
            qo_ref[:, sl] = _rope_tile(q_ref[:, sl], cs, sn, SWA_DIM // 2).astype(BF16)
        ko_ref[...] = _rope_tile(kv_ref[:, :LANES], cs, sn, SWA_DIM // 2).astype(BF16)
        vo_ref[...] = kv_ref[:, LANES:].astype(BF16)

    rows = lambda w, c=0: pl.BlockSpec((tb, w), lambda i: (i, c))
    return pl.pallas_call(
        body, name=name, grid=(s // tb,),
        in_specs=[rows(nq, O_Q // nq), rows(2 * LANES, O_K // (2 * LANES)), rows(LANES), rows(LANES)],
        out_specs=[rows(nq), rows(LANES), rows(LANES)],
        out_shape=[jax.ShapeDtypeStruct((s, nq), BF16), jax.ShapeDtypeStruct((s, LANES), BF16), jax.ShapeDtypeStruct((s, LANES), BF16)],
        compiler_params=_cparams((_PAR,)),
    )(h, h, cos, sin)


def _odd_rope_bwd(dq, dk, cos, sin, *, name):
    s = dq.shape[0]
    tb = _pick(s, (512, 256, 128))
    nq = SWA_HEADS * SWA_DIM

    def body(dq_ref, dk_ref, cos_ref, sin_ref, qo_ref, ko_ref):
        cs, sn = cos_ref[...], -sin_ref[...]
        for j in range(nq // LANES):
            sl = slice(j * LANES, (j + 1) * LANES)
            qo_ref[:, sl] = _rope_tile(dq_ref[:, sl], cs, sn, SWA_DIM // 2).astype(BF16)
        ko_ref[...] = _rope_tile(dk_ref[...], cs, sn, SWA_DIM // 2).astype(BF16)

    rows = lambda w: pl.BlockSpec((tb, w), lambda i: (i, 0))
    return pl.pallas_call(
        body, name=name, grid=(s // tb,), in_specs=[rows(nq), rows(LANES), rows(LANES), rows(LANES)],
        out_specs=[rows(nq), rows(LANES)],
        out_shape=[jax.ShapeDtypeStruct((s, nq), BF16), jax.ShapeDtypeStruct((s, LANES), BF16)],
        compiler_params=_cparams((_PAR,)),
    )(dq, dk, cos, sin)


def _out_fwd(o, h, x, w_out, g, b, *, name):
    s = x.shape[0]
    tb = _pick(s, (256, 128))

    def body(o_ref, gate_ref, x_ref, w_ref, g_ref, b_ref, xn_ref, xb_ref, z_ref, xh_ref, rs_ref):
        gate = gate_ref[...]
        z = (o_ref[...] * (gate * _sigmoid(gate))).astype(BF16)
        z_ref[...] = z
        r = ALPHA * x_ref[...] + jnp.dot(z, w_ref[...], preferred_element_type=F32)
        mu = jnp.mean(r, axis=-1, keepdims=True)
        rc = r - mu
        rstd = lax.rsqrt(jnp.mean(rc * rc, axis=-1, keepdims=True) + LN_EPS)
        xh = rc * rstd
        xn = xh * g_ref[...] + b_ref[...]
        xh_ref[...] = xh
        rs_ref[...] = rstd
        xn_ref[...] = xn
        xb_ref[...] = xn.astype(BF16)

    rows = lambda w: pl.BlockSpec((tb, w), lambda i: (i, 0))
    full = lambda a: pl.BlockSpec(a.shape, lambda i: (0,) * a.ndim)
    return pl.pallas_call(
        body, name=name, grid=(s // tb,),
        in_specs=[rows(D_MODEL), rows(D_MODEL), rows(D_MODEL), full(w_out), full(g), full(b)],
        out_specs=[rows(D_MODEL), rows(D_MODEL), rows(D_MODEL), rows(D_MODEL), rows(1)],
        out_shape=[jax.ShapeDtypeStruct((s, D_MODEL), F32), jax.ShapeDtypeStruct((s, D_MODEL), BF16),
                   jax.ShapeDtypeStruct((s, D_MODEL), BF16), jax.ShapeDtypeStruct((s, D_MODEL), F32),
                   jax.ShapeDtypeStruct((s, 1), F32)],
        compiler_params=_cparams((_PAR,)),
    )(o, h, x, w_out, g, b)


def _out_bwd(dxn, xh, rstd, g, w_out_t, o, h, *, name):
    s = dxn.shape[0]
    tb = _pick(s, (256, 128))

    def body(dxn_ref, xh_ref, rs_ref, g_ref, wt_ref, o_ref, gate_ref, dr_ref, dy_ref, do_ref, dgate_ref, dg_ref, db_ref):
        @pl.when(pl.program_id(0) == 0)
        def _():
            dg_ref[...] = jnp.zeros_like(dg_ref)
            db_ref[...] = jnp.zeros_like(db_ref)

        dxn_b, xh_b = dxn_ref[...], xh_ref[...]
        dg_ref[...] += jnp.sum(dxn_b * xh_b, axis=0, keepdims=True)
        db_ref[...] += jnp.sum(dxn_b, axis=0, keepdims=True)
        dxh = dxn_b * g_ref[...]
        dr = rs_ref[...] * (dxh - jnp.mean(dxh, axis=-1, keepdims=True) - xh_b * jnp.mean(dxh * xh_b, axis=-1, keepdims=True))
        dr_ref[...] = dr
        dy = dr.astype(BF16)
        dy_ref[...] = dy
        dz = jnp.dot(dy, wt_ref[...], preferred_element_type=F32)
        gate = gate_ref[...]
        sg = _sigmoid(gate)
        do_ref[...] = (dz * (gate * sg)).astype(BF16)
        dgate_ref[...] = (dz * o_ref[...] * (sg * (1.0 + gate * (1.0 - sg)))).astype(BF16)

    rows = lambda w: pl.BlockSpec((tb, w), lambda i: (i, 0))
    full = lambda a: pl.BlockSpec(a.shape, lambda i: (0,) * a.ndim)
    return pl.pallas_call(
        body, name=name, grid=(s // tb,),
        in_specs=[rows(D_MODEL), rows(D_MODEL), rows(1), full(g), full(w_out_t), rows(D_MODEL), rows(D_MODEL)],
        out_specs=[rows(D_MODEL), rows(D_MODEL), rows(D_MODEL), rows(D_MODEL), full(g), full(g)],
        out_shape=[jax.ShapeDtypeStruct((s, D_MODEL), F32), jax.ShapeDtypeStruct((s, D_MODEL), BF16),
                   jax.ShapeDtypeStruct((s, D_MODEL), BF16), jax.ShapeDtypeStruct((s, D_MODEL), BF16),
                   jax.ShapeDtypeStruct(g.shape, F32), jax.ShapeDtypeStruct(g.shape, F32)],
        compiler_params=_cparams((_ARB,)),
    )(dxn, xh, rstd, g, w_out_t, o, h)


def _loss_grad(y, target, *, name):
    s, d = y.shape
    tb = _pick(s, (512, 256, 128))

    def body(y_ref, t_ref, dy_ref, l_ref):
        @pl.when(pl.program_id(0) == 0)
        def _():
            l_ref[...] = jnp.zeros_like(l_ref)

        err = y_ref[...] - t_ref[...]
        dy_ref[...] = err * (1.0 / d)
        per_tok = jnp.mean(err * err, axis=-1, keepdims=True)
        l_ref[...] += 0.5 * jnp.sum(per_tok, axis=0, keepdims=True)

    rows = pl.BlockSpec((tb, d), lambda i: (i, 0))
    return pl.pallas_call(
        body, name=name, grid=(s // tb,), in_specs=[rows, rows],
        out_specs=[rows, pl.BlockSpec((8, LANES), lambda i: (0, 0))],
        out_shape=[jax.ShapeDtypeStruct((s, d), F32), jax.ShapeDtypeStruct((8, LANES), F32)],
        compiler_params=_cparams((_ARB,)),
    )(y, target)


def _adamw(w, g, m, v, *, name):
    shape = w.shape
    w3, g3, m3, v3 = (a.reshape((1,) * (3 - a.ndim) + a.shape) for a in (w, g, m, v))
    a0, a1, a2 = w3.shape
    tb = _pick(a1, (256, 128)) if a1 % 8 == 0 else a1
    c1 = 1.0 - ADAM_B1 ** ADAM_STEP
    c2 = 1.0 - ADAM_B2 ** ADAM_STEP

    def body(w_ref, g_ref, m_ref, v_ref, d_ref, mo_ref, vo_ref):
        gg = g_ref[...]
        mn = ADAM_B1 * m_ref[...] + (1.0 - ADAM_B1) * gg
        vn = ADAM_B2 * v_ref[...] + (1.0 - ADAM_B2) * (gg * gg)
        mo_ref[...] = mn
        vo_ref[...] = vn
        d_ref[...] = -ADAM_LR * ((mn / c1) / (jnp.sqrt(vn / c2) + ADAM_EPS) + ADAM_WD * w_ref[...])

    spec = pl.BlockSpec((1, tb, a2), lambda i, j: (i, j, 0))
    outs = pl.pallas_call(
        body, name=name, grid=(a0, a1 // tb), in_specs=[spec] * 4, out_specs=[spec] * 3,
        out_shape=[jax.ShapeDtypeStruct(w3.shape, F32)] * 3, compiler_params=_cparams((_PAR, _PAR)),
    )(w3, g3, m3, v3)
    return tuple(a.reshape(shape) for a in outs)


def _place():
    x, y, c = lax.axis_index("x"), lax.axis_index("y"), lax.axis_index("c")
    return x, y, c, [(1 - x, y), (x, 1 - y), (1 - x, 1 - y)]


_ANY = pl.BlockSpec(memory_space=pl.ANY)


def _sem_pairs(n):
    return [pltpu.SemaphoreType.DMA((n,)), pltpu.SemaphoreType.DMA((n,))]


def _gather_chip_shards(tensors, *, name):
    nt = len(tensors)

    def body(*refs):
        srcs, outs = refs[:nt], refs[nt:2 * nt]
        send_sems, recv_sems, lsems = refs[2 * nt:]
        x, y, c, chips = _place()
        me = 2 * x + y
        sib = (x, y, 1 - c)

        def half(t, chip, hc):
            rh = tensors[t].shape[1] // 2
            return outs[t].at[chip, :, pl.ds(hc * rh, rh), :]

        def copy(t, kk, s_ref, d_ref, to):
            return pltpu.make_async_remote_copy(src_ref=s_ref, dst_ref=d_ref, send_sem=send_sems.at[6 * t + kk],
                                                recv_sem=recv_sems.at[6 * t + kk], device_id=to, device_id_type=MESH)

        mine = [pltpu.make_async_copy(srcs[t], outs[t].at[me], lsems.at[t]) for t in range(nt)]
        for cp in mine:
            cp.start()
        sent = []
        for j, (cx, cy) in enumerate(chips):
            for t in range(nt):
                rh = tensors[t].shape[1] // 2
                cp = copy(t, j, srcs[t].at[:, pl.ds(c * rh, rh), :], half(t, me, c), (cx, cy, c))
                cp.start()
                sent.append(cp)
        for j, (cx, cy) in enumerate(chips):
            for t in range(nt):
                landed = half(t, 2 * cx + cy, c)
                copy(t, j, landed, landed, (cx, cy, c)).wait_recv()
                fw = copy(t, 3 + j, landed, landed, sib)
                fw.start()
                sent.append(fw)
        for j, (cx, cy) in enumerate(chips):
            for t in range(nt):
                theirs = half(t, 2 * cx + cy, 1 - c)
                copy(t, 3 + j, theirs, theirs, sib).wait_recv()
        for cp in sent:
            cp.wait_send()
        for cp in mine:
            cp.wait()

    return pl.pallas_call(
        body, name=name, in_specs=[_ANY] * nt, out_specs=[_ANY] * nt,
        out_shape=[jax.ShapeDtypeStruct((N_CHIPS,) + a.shape, a.dtype) for a in tensors],
        scratch_shapes=_sem_pairs(6 * nt) + [pltpu.SemaphoreType.DMA((nt,))],
    )(*tensors)


def _swap_halves(gs, *, name):
    nt = len(gs)

    def body(*refs):
        srcs, outs = refs[:nt], refs[nt:2 * nt]
        send_sems, recv_sems = refs[2 * nt:]
        x, y, c, _ = _place()
        cps = [pltpu.make_async_remote_copy(src_ref=srcs[t].at[:, 1 - c], dst_ref=outs[t], send_sem=send_sems.at[t],
                                            recv_sem=recv_sems.at[t], device_id=(x, y, 1 - c), device_id_type=MESH)
               for t in range(nt)]
        for cp in cps:
            cp.start()
        for cp in cps:
            cp.wait()

    return pl.pallas_call(
        body, name=name, in_specs=[_ANY] * nt, out_specs=[_ANY] * nt,
        out_shape=[jax.ShapeDtypeStruct((N_CHIPS,) + g.shape[2:], g.dtype) for g in gs], scratch_shapes=_sem_pairs(nt),
    )(*gs)


def _scatter_to_chips(blocks, *, name):
    nt = len(blocks)

    def body(*refs):
        srcs, outs = refs[:nt], refs[nt:2 * nt]
        send_sems, recv_sems = refs[2 * nt:]
        x, y, c, chips = _place()
        cps = [pltpu.make_async_remote_copy(src_ref=srcs[t].at[2 * cx + cy], dst_ref=outs[t].at[j],
                                            send_sem=send_sems.at[3 * t + j], recv_sem=recv_sems.at[3 * t + j],
                                            device_id=(cx, cy, c), device_id_type=MESH)
               for j, (cx, cy) in enumerate(chips) for t in range(nt)]
        for cp in cps:
            cp.start()
        for cp in cps:
            cp.wait()

    return pl.pallas_call(
        body, name=name, in_specs=[_ANY] * nt, out_specs=[_ANY] * nt,
        out_shape=[jax.ShapeDtypeStruct((3,) + a.shape[1:], a.dtype) for a in blocks], scratch_shapes=_sem_pairs(3 * nt),
    )(*blocks)


def _join_halves(fs, *, name):
    nt = len(fs)

    def body(*refs):
        srcs, outs = refs[:nt], refs[nt:2 * nt]
        send_sems, recv_sems, lsems = refs[2 * nt:]
        x, y, c, _ = _place()
        mine, cps = [], []
        for t in range(nt):
            rh = fs[t].shape[1]
            dst = outs[t].at[:, pl.ds(c * rh, rh), :]
            mine.append(pltpu.make_async_copy(srcs[t], dst, lsems.at[t]))
            cps.append(pltpu.make_async_remote_copy(src_ref=srcs[t], dst_ref=dst, send_sem=send_sems.at[t],
                                                    recv_sem=recv_sems.at[t], device_id=(x, y, 1 - c), device_id_type=MESH))
        for cp in mine + cps:
            cp.start()
        for cp in cps:
            cp.wait()
        for cp in mine:
            cp.wait()

    return pl.pallas_call(
        body, name=name, in_specs=[_ANY] * nt, out_specs=[_ANY] * nt,
        out_shape=[jax.ShapeDtypeStruct((f.shape[0], 2 * f.shape[1], f.shape[2]), f.dtype) for f in fs],
        scratch_shapes=_sem_pairs(nt) + [pltpu.SemaphoreType.DMA((nt,))],
    )(*fs)


def _add_sibling(g, recv, cidx, chip_idx, *, name):
    _, _, na, rh, cdim = g.shape
    tb = _pick(rh, (256, 128, 64, 32, 16))

    def body(c_ref, k_ref, g_ref, r_ref, s_ref, o_ref):
        tot = g_ref[0, 0] + r_ref[0]
        s_ref[0] = tot.astype(BF16)

        @pl.when(pl.program_id(2) == k_ref[0])
        def _():
            o_ref[...] = tot

    return pl.pallas_call(
        body, name=name,
        grid_spec=pltpu.PrefetchScalarGridSpec(
            num_scalar_prefetch=2, grid=(na, rh // tb, N_CHIPS),
            in_specs=[pl.BlockSpec((1, 1, 1, tb, cdim), lambda a, i, k, c_ref, k_ref: (k, c_ref[0], a, i, 0)),
                      pl.BlockSpec((1, 1, tb, cdim), lambda a, i, k, c_ref, k_ref: (k, a, i, 0))],
            out_specs=[pl.BlockSpec((1, 1, tb, cdim), lambda a, i, k, c_ref, k_ref: (k, a, i, 0)),
                       pl.BlockSpec((1, tb, cdim), lambda a, i, k, c_ref, k_ref: (a, i, 0))]),
        out_shape=[jax.ShapeDtypeStruct((N_CHIPS, na, rh, cdim), BF16), jax.ShapeDtypeStruct((na, rh, cdim), F32)],
        compiler_params=_cparams((_PAR, _PAR, _ARB)),
    )(cidx, chip_idx, g, recv)


def _add_chips(own, recv, *, name):
    na, rh, cdim = own.shape
    tb = _pick(rh, (256, 128, 64, 32, 16))

    def body(a_ref, r0_ref, r1_ref, r2_ref, o_ref):
        o_ref[...] = ((a_ref[...] + r0_ref[0].astype(F32)) + r1_ref[0].astype(F32)) + r2_ref[0].astype(F32)

    slot = lambda j: pl.BlockSpec((1, 1, tb, cdim), lambda a, i: (j, a, i, 0))
    spec = pl.BlockSpec((1, tb, cdim), lambda a, i: (a, i, 0))
    return pl.pallas_call(
        body, name=name, grid=(na, rh // tb), in_specs=[spec, slot(0), slot(1), slot(2)], out_specs=spec,
        out_shape=jax.ShapeDtypeStruct((na, rh, cdim), F32), compiler_params=_cparams((_PAR, _PAR)),
    )(own, recv, recv, recv)


def _all_reduce_small(v, *, name):
    r, cdim = v.shape

    def body(v_ref, o_ref, buf, send_sems, recv_sems):
        x, y, c, _ = _place()
        me = 4 * x + 2 * y + c
        buf[me] = v_ref[...]
        cps = []
        for p in range(1, N_DEV):
            to = (1 - x if p & 4 else x, 1 - y if p & 2 else y, 1 - c if p & 1 else c)
            cp = pltpu.make_async_remote_copy(src_ref=v_ref, dst_ref=buf.at[me], send_sem=send_sems.at[p - 1],
                                              recv_sem=recv_sems.at[p - 1], device_id=to, device_id_type=MESH)
            cp.start()
            cps.append(cp)
        for p in range(1, N_DEV):
            frm = (4 * x + 2 * y + c) ^ p
            pltpu.make_async_remote_copy(src_ref=v_ref, dst_ref=buf.at[frm], send_sem=send_sems.at[p - 1],
                                         recv_sem=recv_sems.at[p - 1], device_id=(x, y, c), device_id_type=MESH).wait_recv()
        for cp in cps:
            cp.wait_send()
        tot = buf[0]
        for i in range(1, N_DEV):
            tot = tot + buf[i]
        o_ref[...] = tot

    vm = pl.BlockSpec(memory_space=pltpu.VMEM)
    return pl.pallas_call(
        body, name=name, in_specs=[vm], out_specs=vm, out_shape=jax.ShapeDtypeStruct((r, cdim), F32),
        scratch_shapes=[pltpu.VMEM((N_DEV, r, cdim), F32), pltpu.SemaphoreType.DMA((N_DEV - 1,)), pltpu.SemaphoreType.DMA((N_DEV - 1,))],
    )(v)


_SHARDED = ("even_w_in", "even_w_uq", "even_w_ukv", "even_w_out", "odd_w_in", "odd_w_out")
_COL_SHARDED = ("even_w_in", "even_w_uq", "even_w_ukv", "odd_w_in")
_FULL_SHAPES = {"even_w_in": (2, D_MODEL, EVEN_IN), "even_w_uq": (2, MLA_Q_RANK, 768), "even_w_ukv": (2, MLA_KV_RANK, 1024),
                "even_w_out": (2, D_MODEL, D_MODEL), "odd_w_in": (2, D_MODEL, ODD_IN), "odd_w_out": (2, D_MODEL, D_MODEL)}
PACK_COLS = 1024


def _unshard(name, stacked):
    if name in _COL_SHARDED:
        n, a, b, cc = stacked.shape
        return stacked.transpose(1, 2, 0, 3).reshape(a, b, n * cc)
    n, a, b, cc = stacked.shape
    return stacked.transpose(1, 0, 2, 3).reshape(a, n * b, cc)


def _chip_halves(name, full):
    a, b, cc = full.shape
    if name in _COL_SHARDED:
        return full.reshape(a, 2, b // 2, N_CHIPS, cc // N_CHIPS).transpose(3, 1, 0, 2, 4)
    return full.reshape(a, N_CHIPS, 2, b // (2 * N_CHIPS), cc).transpose(1, 2, 0, 3, 4)


def _rope_tables(s):
    pos = jnp.arange(s, dtype=F32)

    def ang(d):
        inv = ROPE_THETA ** (-jnp.arange(0, d, 2, dtype=F32) / d)
        a = pos[:, None] * inv[None, :]
        return jnp.cos(a), jnp.sin(a)

    c16, s16 = ang(MLA_ROPE)
    one, zero = jnp.ones((s, KPE_LANE), F32), jnp.zeros((s, KPE_LANE), F32)
    cos_m = jnp.concatenate([one, c16, c16, one[:, :32]], axis=1)
    sin_m = jnp.concatenate([zero, -s16, s16, zero[:, :32]], axis=1)
    c32, s32 = ang(SWA_DIM)
    cos_s = jnp.concatenate([c32, c32, c32, c32], axis=1)
    sin_s = jnp.concatenate([-s32, s32, -s32, s32], axis=1)
    return cos_m, sin_m, cos_s, sin_s


def _heads(t, nh):
    s = t.shape[0]
    return t.reshape(s, nh, -1).transpose(1, 0, 2)


def _merge(t):
    nh, s, d = t.shape
    return t.transpose(1, 0, 2).reshape(s, nh * d)


def _even_weights(w_in, w_uq, w_ukv):
    zeros = lambda n: jnp.zeros((D_MODEL, n), w_in.dtype)
    wcq, wckv, wkpe = w_in[:, 0:256], w_in[:, 256:384], w_in[:, 384:416]
    wfq, wfk, wfv = w_in[:, 416:928], w_in[:, 928:1440], w_in[:, 1440:1952]
    wfl, wg = w_in[:, 1952:1960], w_in[:, 1960:2984]
    misc = jnp.concatenate([zeros(KPE_LANE), wkpe, wfl, zeros(LANES - FL_LANE - FOX_HEADS)], axis=1)
    w_in_p = jnp.concatenate([wg, wfq, wfk, wfv, wcq, wckv, misc], axis=1)
    uq = w_uq.reshape(MLA_Q_RANK, MLA_HEADS, MLA_NOPE + MLA_ROPE)
    uq_p = jnp.pad(uq, ((0, 0), (0, 0), (0, HEAD_PAD - MLA_NOPE - MLA_ROPE))).reshape(MLA_Q_RANK, MLA_HEADS * HEAD_PAD)
    ukv = w_ukv.reshape(MLA_KV_RANK, MLA_HEADS, MLA_NOPE + MLA_V)
    uk_p = jnp.pad(ukv[..., :MLA_NOPE], ((0, 0), (0, 0), (0, HEAD_PAD - MLA_NOPE))).reshape(MLA_KV_RANK, MLA_HEADS * HEAD_PAD)
    uv = ukv[..., MLA_NOPE:].reshape(MLA_KV_RANK, MLA_HEADS * MLA_V)
    ukv_p = jnp.concatenate([uk_p, uv], axis=1)
    return w_in_p, w_in_p.T, uq_p, uq_p.T, ukv_p, ukv_p.T


def _even_weight_grads(dw_in_p, duq_p, dukv_p):
    g = dw_in_p
    gate, fq, fk, fv = g[:, E_GATE:E_FQ], g[:, E_FQ:E_FK], g[:, E_FK:E_FV], g[:, E_FV:E_SMALL]
    cq, ckv, misc = g[:, E_SMALL:E_SMALL + 256], g[:, E_SMALL + 256:E_SMALL + 384], g[:, E_SMALL + 384:]
    dw_in = jnp.concatenate([cq, ckv, misc[:, KPE_LANE:KPE_LANE + MLA_ROPE], fq, fk, fv, misc[:, FL_LANE:FL_LANE + FOX_HEADS], gate], axis=1)
    duq = duq_p.reshape(MLA_Q_RANK, MLA_HEADS, HEAD_PAD)[..., :MLA_NOPE + MLA_ROPE].reshape(MLA_Q_RANK, -1)
    nq = MLA_HEADS * HEAD_PAD
    dk = dukv_p[:, :nq].reshape(MLA_KV_RANK, MLA_HEADS, HEAD_PAD)[..., :MLA_NOPE]
    dvv = dukv_p[:, nq:].reshape(MLA_KV_RANK, MLA_HEADS, MLA_V)
    dukv = jnp.concatenate([dk, dvv], axis=-1).reshape(MLA_KV_RANK, -1)
    return dw_in, duq, dukv


def _odd_weights(w_in):
    q, k, v, gate = w_in[:, 0:1024], w_in[:, 1024:1152], w_in[:, 1152:1280], w_in[:, 1280:2304]
    w_p = jnp.concatenate([gate, q, k, v], axis=1)
    return w_p, w_p.T


def _odd_weight_grads(g):
    return jnp.concatenate([g[:, O_Q:O_K], g[:, O_K:O_V], g[:, O_V:], g[:, O_GATE:O_Q]], axis=1)


def kernel(x, even_w_in, even_q_norm, even_w_uq, even_kv_norm, even_w_ukv, even_b_f, even_w_out, even_ln_g, even_ln_b, odd_w_in, odd_sinks, odd_w_out, odd_ln_g, odd_ln_b, loss_target, m_even_w_in, m_even_q_norm, m_even_w_uq, m_even_kv_norm, m_even_w_ukv, m_even_b_f, m_even_w_out, m_even_ln_g, m_even_ln_b, m_odd_w_in, m_odd_sinks, m_odd_w_out, m_odd_ln_g, m_odd_ln_b, v_even_w_in, v_even_q_norm, v_even_w_uq, v_even_kv_norm, v_even_w_ukv, v_even_b_f, v_even_w_out, v_even_ln_g, v_even_ln_b, v_odd_w_in, v_odd_sinks, v_odd_w_out, v_odd_ln_g, v_odd_ln_b):
    weights = dict(even_w_in=even_w_in, even_q_norm=even_q_norm, even_w_uq=even_w_uq, even_kv_norm=even_kv_norm,
                   even_w_ukv=even_w_ukv, even_b_f=even_b_f, even_w_out=even_w_out, even_ln_g=even_ln_g, even_ln_b=even_ln_b,
                   odd_w_in=odd_w_in, odd_sinks=odd_sinks, odd_w_out=odd_w_out, odd_ln_g=odd_ln_g, odd_ln_b=odd_ln_b)
    mom_m = dict(even_w_in=m_even_w_in, even_q_norm=m_even_q_norm, even_w_uq=m_even_w_uq, even_kv_norm=m_even_kv_norm,
                 even_w_ukv=m_even_w_ukv, even_b_f=m_even_b_f, even_w_out=m_even_w_out, even_ln_g=m_even_ln_g, even_ln_b=m_even_ln_b,
                 odd_w_in=m_odd_w_in, odd_sinks=m_odd_sinks, odd_w_out=m_odd_w_out, odd_ln_g=m_odd_ln_g, odd_ln_b=m_odd_ln_b)
    mom_v = dict(even_w_in=v_even_w_in, even_q_norm=v_even_q_norm, even_w_uq=v_even_w_uq, even_kv_norm=v_even_kv_norm,
                 even_w_ukv=v_even_w_ukv, even_b_f=v_even_b_f, even_w_out=v_even_w_out, even_ln_g=v_even_ln_g, even_ln_b=v_even_ln_b,
                 odd_w_in=v_odd_w_in, odd_sinks=v_odd_sinks, odd_w_out=v_odd_w_out, odd_ln_g=v_odd_ln_g, odd_ln_b=v_odd_ln_b)
    names = list(weights)
    xl = x[0]
    tgt = loss_target[0]
    s = xl.shape[0]
    ax, ay, ac = lax.axis_index("x"), lax.axis_index("y"), lax.axis_index("c")
    chip = 2 * ax + ay

    ln_odd = jnp.pad(jnp.concatenate([odd_ln_g, odd_ln_b]), ((0, 12), (0, 0)))[None]
    gathered = _gather_chip_shards([weights[n].astype(BF16) for n in _SHARDED] + [ln_odd], name="gather_weights")
    full = {n: _unshard(n, gathered[i]) for i, n in enumerate(_SHARDED)}
    ln_all = gathered[-1][:, 0]
    odd_g_full = ln_all[:, 0:2].transpose(1, 0, 2).reshape(2, D_MODEL)
    odd_b_full = ln_all[:, 2:4].transpose(1, 0, 2).reshape(2, D_MODEL)

    cos_m, sin_m, cos_s, sin_s = _rope_tables(s)
    bf_tiles = jnp.pad(even_b_f, ((0, 0), (FL_LANE, LANES - FL_LANE - FOX_HEADS)))
    sink_tiles = jnp.broadcast_to(odd_sinks[:, :, None, None], (2, SWA_HEADS, 1, LANES))

    saved = []
    x_f, x_b = xl, xl.astype(BF16)
    for layer in range(DEPTH):
        j = layer // 2
        ln = f"L{layer}"
        if layer % 2 == 0:
            w_in_p, w_in_t, uq_p, uq_t, ukv_p, ukv_t = _even_weights(full["even_w_in"][j], full["even_w_uq"][j], full["even_w_ukv"][j])
            w_out, w_out_t = full["even_w_out"][j], full["even_w_out"][j].T
            qg, kg, bft = even_q_norm[j][None], even_kv_norm[j][None], bf_tiles[j][None]
            h = _mm(x_b, w_in_p, out_dtype=F32, name=ln + "_in")
            q, k, v, qn, kvn, logf = _even_mid_fwd(h, qg, kg, bft, uq_p, ukv_p, cos_m, sin_m, name=ln + "_mid")
            cum = _cumsum(logf, reverse=False, name=ln + "_cum")[:, FL_LANE:FL_LANE + FOX_HEADS].T
            ccol, crow = cum[:, :, None], cum[:, None, :]
            qh, kh, vh = _heads(q, MLA_HEADS), _heads(k, MLA_HEADS), _heads(v, MLA_HEADS)
            o_mla, lse_mla = _attn_fwd(qh, kh, vh, None, None, scale=(MLA_NOPE + MLA_ROPE) ** -0.5, name=ln + "_mla")
            fq, fk, fv = (_heads(h[:, a:a + 512].astype(BF16), FOX_HEADS) for a in (E_FQ, E_FK, E_FV))
            o_fox, lse_fox = _attn_fwd(fq, fk, fv, ccol, crow, scale=FOX_DIM ** -0.5, name=ln + "_fox")
            o = jnp.concatenate([_merge(o_mla), _merge(o_fox)], axis=1)
            g_ln, b_ln = even_ln_g[j][None], even_ln_b[j][None]
            x_n, x_nb, z, xh, rstd = _out_fwd(o, h, x_f, w_out, g_ln, b_ln, name=ln + "_out")
            saved.append(dict(h=h, x_b=x_b, qn=qn, kvn=kvn, qh=qh, kh=kh, vh=vh, fq=fq, fk=fk, fv=fv, ccol=ccol, crow=crow,
                              o_mla=o_mla, o_fox=o_fox, lse_mla=lse_mla, lse_fox=lse_fox, o=o, z=z, xh=xh, rstd=rstd,
                              w_in_t=w_in_t, uq_t=uq_t, ukv_t=ukv_t, w_out_t=w_out_t, qg=qg, kg=kg, bft=bft, g_ln=g_ln))
        else:
            w_in_p, w_in_t = _odd_weights(full["odd_w_in"][j])
            w_out, w_out_t = full["odd_w_out"][j], full["odd_w_out"][j].T
            h = _mm(x_b, w_in_p, out_dtype=F32, name=ln + "_in")
            q, k, v = _odd_rope(h, cos_s, sin_s, name=ln + "_rope")
            qh = _heads(q, SWA_HEADS)
            kp = jnp.pad(_heads(k, SWA_KV_HEADS), ((0, 0), (WINDOW, 0), (0, 0)))
            vp = jnp.pad(_heads(v, SWA_KV_HEADS), ((0, 0), (WINDOW, 0), (0, 0)))
            o_h, lse = _swa_fwd(qh, kp, vp, sink_tiles[j], name=ln + "_swa")
            o = _merge(o_h)
            g_ln, b_ln = odd_g_full[j][None], odd_b_full[j][None]
            x_n, x_nb, z, xh, rstd = _out_fwd(o, h, x_f, w_out, g_ln, b_ln, name=ln + "_out")
            saved.append(dict(h=h, x_b=x_b, qh=qh, kp=kp, vp=vp, o_h=o_h, lse=lse, o=o, z=z, xh=xh, rstd=rstd,
                              w_in_t=w_in_t, w_out_t=w_out_t, g_ln=g_ln))
        x_f, x_b = x_n, x_nb

    dxn, loss_tile = _loss_grad(x_f, tgt, name="loss")

    grads = {n: [None, None] for n in names}
    for layer in reversed(range(DEPTH)):
        j = layer // 2
        ln = f"L{layer}"
        sv = saved[layer]
        dr, dy, do, dgate, dg_ln, db_ln = _out_bwd(dxn, sv["xh"], sv["rstd"], sv["g_ln"], sv["w_out_t"], sv["o"], sv["h"], name=ln + "_outb")
        dw_out = _mm(sv["z"].T, dy, out_dtype=F32, name=ln + "_dwout")
        if layer % 2 == 0:
            do_mla, do_fox = _heads(do[:, :512], MLA_HEADS), _heads(do[:, 512:], FOX_HEADS)
            dqh, dkh, dvh = _attn_bwd(sv["qh"], sv["kh"], sv["vh"], sv["o_mla"], do_mla, sv["lse_mla"], None, None,
                                      scale=(MLA_NOPE + MLA_ROPE) ** -0.5, name=ln + "_mlab")
            dfq, dfk, dfv, dcrow, dccol = _attn_bwd(sv["fq"], sv["fk"], sv["fv"], sv["o_fox"], do_fox, sv["lse_fox"], sv["ccol"], sv["crow"],
                                             scale=FOX_DIM ** -0.5, name=ln + "_foxb")
            dcum = jnp.pad((dcrow[:, 0, :] + dccol[:, :, 0]).T, ((0, 0), (FL_LANE, LANES - FL_LANE - FOX_HEADS)))
            dlogf = _cumsum(dcum, reverse=True, name=ln + "_cumb")
            dh_small, dq_pre, dqg, dkg, dbf = _even_mid_bwd(
                sv["h"], _merge(dqh), _merge(dkh), _merge(dvh).astype(BF16), dlogf, sv["qg"], sv["kg"], sv["bft"],
                sv["uq_t"], sv["ukv_t"], cos_m, sin_m, name=ln + "_midb")
            duq_p = _mm(sv["qn"].T, dq_pre, out_dtype=F32, name=ln + "_dwuq")
            dkv_cat = jnp.concatenate([_merge(dkh).astype(BF16), _merge(dvh).astype(BF16)], axis=1)
            dukv_p = _mm(sv["kvn"].T, dkv_cat, out_dtype=F32, name=ln + "_dwukv")
            dh = jnp.concatenate([dgate, _merge(dfq).astype(BF16), _merge(dfk).astype(BF16), _merge(dfv).astype(BF16), dh_small], axis=1)
            dw_in_p = _mm(sv["x_b"].T, dh, out_dtype=F32, name=ln + "_dwin")
            dw_in, duq, dukv = _even_weight_grads(dw_in_p, duq_p, dukv_p)
            for n, val in (("even_w_in", dw_in), ("even_w_uq", duq), ("even_w_ukv", dukv), ("even_w_out", dw_out),
                           ("even_q_norm", dqg[0]), ("even_kv_norm", dkg[0]), ("even_b_f", dbf[0, FL_LANE:FL_LANE + FOX_HEADS]),
                           ("even_ln_g", dg_ln[0]), ("even_ln_b", db_ln[0])):
                grads[n][j] = val
        else:
            dq_h, dkp, dvp, dsink = _swa_bwd(sv["qh"], sv["kp"], sv["vp"], sink_tiles[j], sv["o_h"], _heads(do, SWA_HEADS), sv["lse"], name=ln + "_swab")
            dq_r, dk_r = _odd_rope_bwd(_merge(dq_h), _merge(dkp[:, WINDOW:]), cos_s, sin_s, name=ln + "_ropeb")
            dh = jnp.concatenate([dgate, dq_r, dk_r, _merge(dvp[:, WINDOW:]).astype(BF16)], axis=1)
            dw_in_p = _mm(sv["x_b"].T, dh, out_dtype=F32, name=ln + "_dwin")
            for n, val in (("odd_w_in", _odd_weight_grads(dw_in_p)), ("odd_w_out", dw_out), ("odd_sinks", dsink[:, 0, 0]),
                           ("odd_ln_g", dg_ln[0]), ("odd_ln_b", db_ln[0])):
                grads[n][j] = val
        dxn = _mm(dh, sv["w_in_t"], out_dtype=F32, name=ln + "_dx", res=dr, res_scale=ALPHA)
    grad_x = dxn[None]
    grads = {n: jnp.stack(v) for n, v in grads.items()}

    gp = [_chip_halves(n, grads[n]) for n in _SHARDED]
    from_sib = _swap_halves(gp, name="grad_swap")
    c_idx, chip_idx = ac.astype(jnp.int32).reshape(1), chip.astype(jnp.int32).reshape(1)
    pair = [_add_sibling(g, r, c_idx, chip_idx, name="grad_add_sibling_" + n) for n, g, r in zip(_SHARDED, gp, from_sib)]
    from_chips = _scatter_to_chips([p[0] for p in pair], name="grad_scatter")
    mine = [_add_chips(p[1], r, name="grad_add_chips_" + n) for n, p, r in zip(_SHARDED, pair, from_chips)]
    gshard = dict(zip(_SHARDED, _join_halves(mine, name="grad_join")))

    small = [n for n in names if n not in _SHARDED]
    sv_flat = jnp.concatenate([grads[n].reshape(-1) for n in small] + [loss_tile[0, :1]])
    sv_real = sv_flat.shape[0]
    sv_rows = -(-sv_real // (PACK_COLS * 8)) * 8
    sv_sum = _all_reduce_small(jnp.pad(sv_flat, (0, sv_rows * PACK_COLS - sv_real)).reshape(sv_rows, PACK_COLS), name="small_all_reduce").reshape(-1)
    off = 0
    for n in small:
        size = int(np.prod(grads[n].shape))
        gfull = sv_sum[off:off + size].reshape(grads[n].shape)
        off += size
        if n in ("odd_ln_g", "odd_ln_b"):
            gfull = lax.dynamic_slice_in_dim(gfull, chip * (D_MODEL // N_CHIPS), D_MODEL // N_CHIPS, axis=1)
        gshard[n] = gfull
    loss = sv_sum[off]

    deltas, new_m, new_v = {}, {}, {}
    for n in names:
        deltas[n], new_m[n], new_v[n] = _adamw(weights[n], gshard[n], mom_m[n], mom_v[n], name="adamw_" + n)
    return (loss, grad_x, *[gshard[n] for n in names], *[deltas[n] for n in names],
            *[new_m[n] for n in names], *[new_v[n] for n in names])
```

```python
import jax
import jax.numpy as jnp
import numpy as np
from jax import lax
from jax.experimental import pallas as pl
from jax.experimental.pallas import tpu as pltpu

F32 = jnp.float32
BF16 = jnp.bfloat16
MESH = pl.DeviceIdType.MESH

D_MODEL = 1024
DEPTH = 4
ROPE_THETA = 10000.0
MLA_HEADS, MLA_NOPE, MLA_ROPE, MLA_V = 8, 64, 32, 64
MLA_Q_RANK, MLA_KV_RANK = 256, 128
FOX_HEADS, FOX_DIM = 8, 64
SWA_HEADS, SWA_KV_HEADS, SWA_DIM, WINDOW = 16, 2, 64, 128
RMS_EPS = 1e-6
LN_EPS = 1e-5
ALPHA = (2 * DEPTH) ** 0.25
EVEN_IN = 2984
ODD_IN = 2304
ADAM_LR, ADAM_B1, ADAM_B2, ADAM_EPS, ADAM_WD, ADAM_STEP = 0.001, 0.9, 0.999, 1e-08, 0.01, 10

LANES = 128
HALF = LANES // 2
HEAD_PAD = 128
N_CHIPS = 4
N_DEV = 8
E_GATE, E_FQ, E_FK, E_FV, E_SMALL = 0, 1024, 1536, 2048, 2560
E_PAD_IN = 3072
KPE_LANE = 64
FL_LANE = 96
O_GATE, O_Q, O_K, O_V = 0, 1024, 2048, 2176

_ARB = "arbitrary"
_PAR = "parallel"


def _cparams(sem):
    return pltpu.CompilerParams(dimension_semantics=sem)


def _pick(n, cands):
    for c in cands:
        if n % c == 0:
            return c
    return n


def _mm(a, b, *, out_dtype, name, res=None, res_scale=1.0, also_bf16=False):
    m, k = a.shape
    _, n = b.shape
    tm = _pick(m, (512, 256, 128))
    tn = _pick(n, (1024, 768, 512, 384, 256, 128))
    tk = _pick(k, (1024, 512, 256, 128))
    nk = k // tk

    def body(*refs):
        refs = list(refs)
        a_ref, b_ref = refs[:2]
        r_ref = refs[2] if res is not None else None
        acc_ref = refs[-1]
        outs = refs[3 if res is not None else 2:-1]
        kk = pl.program_id(2)

        @pl.when(kk == 0)
        def _():
            acc_ref[...] = jnp.zeros_like(acc_ref)

        acc_ref[...] += jnp.dot(a_ref[...].astype(BF16), b_ref[...].astype(BF16), preferred_element_type=F32)

        @pl.when(kk == nk - 1)
        def _():
            r = acc_ref[...]
            if res is not None:
                r = r + res_scale * r_ref[...]
            for o_ref in outs:
                o_ref[...] = r.astype(o_ref.dtype)

    in_specs = [pl.BlockSpec((tm, tk), lambda i, j, kk: (i, kk)), pl.BlockSpec((tk, tn), lambda i, j, kk: (kk, j))]
    args = [a, b]
    if res is not None:
        in_specs.append(pl.BlockSpec((tm, tn), lambda i, j, kk: (i, j)))
        args.append(res)
    ospec = pl.BlockSpec((tm, tn), lambda i, j, kk: (i, j))
    dtypes = [out_dtype, BF16] if also_bf16 else [out_dtype]
    out = pl.pallas_call(
        body, name=name, grid=(m // tm, n // tn, nk), in_specs=in_specs,
        out_specs=[ospec] * len(dtypes), out_shape=[jax.ShapeDtypeStruct((m, n), d) for d in dtypes],
        scratch_shapes=[pltpu.VMEM((tm, tn), F32)],
        compiler_params=_cparams((_PAR, _PAR, _ARB)),
    )(*args)
    return out if also_bf16 else out[0]


def _rope_tile(t, cos, sin, half):
    w = t.shape[-1]
    lane = lax.broadcasted_iota(jnp.int32, t.shape, 1)
    first = (lane % (2 * half)) < half
    sw = jnp.where(first, pltpu.roll(t, w - half, 1), pltpu.roll(t, half, 1))
    return t * cos + sw * sin


def _sigmoid(x):
    return 1.0 / (1.0 + jnp.exp(-x))


def _lane_mask(shape, lo, hi):
    lane = lax.broadcasted_iota(jnp.int32, shape, 1)
    return (lane >= lo) & (lane < hi)


def _rms(x, g):
    r = lax.rsqrt(jnp.mean(x * x, axis=-1, keepdims=True) + RMS_EPS)
    u = x * r
    return u, r, u * g


def _rms_bwd(dy, u, r, g):
    dyg = dy * g
    dx = r * (dyg - u * jnp.mean(dyg * u, axis=-1, keepdims=True))
    return dx, jnp.sum(dy * u, axis=0, keepdims=True)


def _even_mid_fwd(h, qg, kg, bf_tile, w_uq, w_ukv, cos, sin, *, name):
    s = h.shape[0]
    tb = _pick(s, (512, 256, 128))
    nq = MLA_HEADS * HEAD_PAD

    def body(h_ref, qg_ref, kg_ref, bf_ref, wuq_ref, wukv_ref, cos_ref, sin_ref,
             q_ref, k_ref, v_ref, qn_ref, kvn_ref, lf_ref):
        hb = h_ref[...]
        cq, ckv, misc = hb[:, :MLA_Q_RANK], hb[:, MLA_Q_RANK:MLA_Q_RANK + MLA_KV_RANK], hb[:, MLA_Q_RANK + MLA_KV_RANK:]
        cs, sn = cos_ref[...], sin_ref[...]
        _, _, qn = _rms(cq, qg_ref[...])
        qn = qn.astype(BF16)
        qn_ref[...] = qn
        q = jnp.dot(qn, wuq_ref[...], preferred_element_type=F32)
        _, _, kvn = _rms(ckv, kg_ref[...])
        kvn = kvn.astype(BF16)
        kvn_ref[...] = kvn
        kv = jnp.dot(kvn, wukv_ref[...], preferred_element_type=F32)
        kpe = jnp.where(_lane_mask(misc.shape, KPE_LANE, KPE_LANE + MLA_ROPE), _rope_tile(misc, cs, sn, MLA_ROPE // 2), 0.0)
        for hd in range(MLA_HEADS):
            sl = slice(hd * HEAD_PAD, (hd + 1) * HEAD_PAD)
            q_ref[:, sl] = _rope_tile(q[:, sl], cs, sn, MLA_ROPE // 2).astype(BF16)
            k_ref[:, sl] = (kv[:, sl] + kpe).astype(BF16)
        v_ref[...] = kv[:, nq:].astype(BF16)
        xf = misc + bf_ref[...]
        logf = jnp.minimum(xf, 0.0) - jnp.log(1.0 + jnp.exp(-jnp.abs(xf)))
        lf_ref[...] = jnp.where(_lane_mask(misc.shape, FL_LANE, FL_LANE + FOX_HEADS), logf, 0.0)

    full = lambda a: pl.BlockSpec(a.shape, lambda i: (0,) * a.ndim)
    rows = lambda w, c=0: pl.BlockSpec((tb, w), lambda i: (i, c))
    return pl.pallas_call(
        body, name=name, grid=(s // tb,),
        in_specs=[rows(512, E_SMALL // 512), full(qg), full(kg), full(bf_tile), full(w_uq), full(w_ukv), rows(LANES), rows(LANES)],
        out_specs=[rows(nq), rows(nq), rows(MLA_HEADS * MLA_V), rows(MLA_Q_RANK), rows(MLA_KV_RANK), rows(LANES)],
        out_shape=[jax.ShapeDtypeStruct((s, nq), BF16), jax.ShapeDtypeStruct((s, nq), BF16),
                   jax.ShapeDtypeStruct((s, MLA_HEADS * MLA_V), BF16), jax.ShapeDtypeStruct((s, MLA_Q_RANK), BF16),
                   jax.ShapeDtypeStruct((s, MLA_KV_RANK), BF16), jax.ShapeDtypeStruct((s, LANES), F32)],
        compiler_params=_cparams((_PAR,)),
    )(h, qg, kg, bf_tile, w_uq, w_ukv, cos, sin)


def _even_mid_bwd(h, dq, dk, dv, dlogf, qg, kg, bf_tile, w_uq_t, w_ukv_t, cos, sin, *, name):
    s = h.shape[0]
    tb = _pick(s, (256, 128))
    nq = MLA_HEADS * HEAD_PAD

    def body(h_ref, dq_ref, dk_ref, dv_ref, dlf_ref, qg_ref, kg_ref, bf_ref, wuqt_ref, wukvt_ref, cos_ref, sin_ref,
             dh_ref, dqp_ref, dqg_ref, dkg_ref, dbf_ref):
        @pl.when(pl.program_id(0) == 0)
        def _():
            dqg_ref[...] = jnp.zeros_like(dqg_ref)
            dkg_ref[...] = jnp.zeros_like(dkg_ref)
            dbf_ref[...] = jnp.zeros_like(dbf_ref)

        hb = h_ref[...]
        cq, ckv, misc = hb[:, :MLA_Q_RANK], hb[:, MLA_Q_RANK:MLA_Q_RANK + MLA_KV_RANK], hb[:, MLA_Q_RANK + MLA_KV_RANK:]
        cs, sn = cos_ref[...], -sin_ref[...]
        dkpe = jnp.zeros(misc.shape, F32)
        for hd in range(MLA_HEADS):
            sl = slice(hd * HEAD_PAD, (hd + 1) * HEAD_PAD)
            dqp_ref[:, sl] = _rope_tile(dq_ref[:, sl], cs, sn, MLA_ROPE // 2).astype(BF16)
            dkpe = dkpe + dk_ref[:, sl]
        dqn = jnp.dot(dqp_ref[...], wuqt_ref[...], preferred_element_type=F32)
        uq, rq, _ = _rms(cq, qg_ref[...])
        dcq, dqg = _rms_bwd(dqn, uq, rq, qg_ref[...])
        dqg_ref[...] += dqg
        dkv = jnp.concatenate([dk_ref[...].astype(BF16), dv_ref[...]], axis=1)
        dkvn = jnp.dot(dkv, wukvt_ref[...], preferred_element_type=F32)
        uk, rk, _ = _rms(ckv, kg_ref[...])
        dckv, dkg = _rms_bwd(dkvn, uk, rk, kg_ref[...])
        dkg_ref[...] += dkg
        dmisc = jnp.where(_lane_mask(misc.shape, KPE_LANE, KPE_LANE + MLA_ROPE), _rope_tile(dkpe, cs, sn, MLA_ROPE // 2), 0.0)
        dfl = jnp.where(_lane_mask(misc.shape, FL_LANE, FL_LANE + FOX_HEADS), dlf_ref[...] * _sigmoid(-(misc + bf_ref[...])), 0.0)
        dbf_ref[...] += jnp.sum(dfl, axis=0, keepdims=True)
        dh_ref[:, :MLA_Q_RANK] = dcq.astype(BF16)
        dh_ref[:, MLA_Q_RANK:MLA_Q_RANK + MLA_KV_RANK] = dckv.astype(BF16)
        dh_ref[:, MLA_Q_RANK + MLA_KV_RANK:] = (dmisc + dfl).astype(BF16)

    full = lambda a: pl.BlockSpec(a.shape, lambda i: (0,) * a.ndim)
    rows = lambda w, c=0: pl.BlockSpec((tb, w), lambda i: (i, c))
    return pl.pallas_call(
        body, name=name, grid=(s // tb,),
        in_specs=[rows(512, E_SMALL // 512), rows(nq), rows(nq), rows(MLA_HEADS * MLA_V), rows(LANES), full(qg), full(kg),
                  full(bf_tile), full(w_uq_t), full(w_ukv_t), rows(LANES), rows(LANES)],
        out_specs=[rows(512), rows(nq), full(qg), full(kg), full(bf_tile)],
        out_shape=[jax.ShapeDtypeStruct((s, 512), BF16), jax.ShapeDtypeStruct((s, nq), BF16),
                   jax.ShapeDtypeStruct(qg.shape, F32), jax.ShapeDtypeStruct(kg.shape, F32),
                   jax.ShapeDtypeStruct(bf_tile.shape, F32)],
        compiler_params=_cparams((_ARB,)),
    )(h, dq, dk, dv, dlogf, qg, kg, bf_tile, w_uq_t, w_ukv_t, cos, sin)


def _cumsum(x, *, reverse, name):
    s = x.shape[0]
    tb = _pick(s, (512, 256, 128))
    nb = s // tb

    def body(x_ref, o_ref, carry_ref):
        @pl.when(pl.program_id(0) == 0)
        def _():
            carry_ref[...] = jnp.zeros_like(carry_ref)

        xv = x_ref[...]
        r = lax.broadcasted_iota(jnp.int32, (tb, tb), 0)
        c = lax.broadcasted_iota(jnp.int32, (tb, tb), 1)
        tri = jnp.where((c >= r) if reverse else (c <= r), 1.0, 0.0).astype(BF16)
        hi = xv.astype(BF16)
        r1 = xv - hi.astype(F32)
        mid = r1.astype(BF16)
        lo = (r1 - mid.astype(F32)).astype(BF16)
        cs = (jnp.dot(tri, hi, preferred_element_type=F32) + jnp.dot(tri, mid, preferred_element_type=F32)
              + jnp.dot(tri, lo, preferred_element_type=F32)) + carry_ref[...]
        o_ref[...] = cs
        carry_ref[...] = cs[0:1, :] if reverse else cs[tb - 1:tb, :]

    imap = (lambda i: (nb - 1 - i, 0)) if reverse else (lambda i: (i, 0))
    return pl.pallas_call(
        body, name=name, grid=(nb,), in_specs=[pl.BlockSpec((tb, LANES), imap)],
        out_specs=pl.BlockSpec((tb, LANES), imap), out_shape=jax.ShapeDtypeStruct(x.shape, F32),
        scratch_shapes=[pltpu.VMEM((1, LANES), F32)], compiler_params=_cparams((_ARB,)),
    )(x)


_NT = (((1,), (1,)), ((), ()))
_TN = (((0,), (0,)), ((), ()))


def _head_sel(x, h, dk):
    if dk == LANES:
        return x[:, h * LANES:(h + 1) * LANES]
    return jnp.where(_lane_mask(x.shape, h * HALF, (h + 1) * HALF), x, jnp.zeros_like(x))


def _head_other(x, h, dk):
    return x[:, h * LANES:(h + 1) * LANES] if dk == LANES else x


def _pair(parts, dk=HALF):
    if dk == LANES:
        return jnp.concatenate(parts, axis=1)
    return jnp.where(_lane_mask(parts[0].shape, 0, HALF), parts[0], parts[1])


def _attn_fwd(q, k, v, ccol, crow, *, dk, npairs, scale, name):
    (qa, qo), (ka, ko), (va, vo) = q, k, v
    s = qa.shape[0]
    wq = 2 * dk
    t = _pick(s, (512, 256, 128))
    nb = s // t
    bias = ccol is not None

    def body(*refs):
        if bias:
            q_ref, k_ref, v_ref, cc_ref, cr_ref, o_ref, lse_ref = refs
        else:
            q_ref, k_ref, v_ref, o_ref, lse_ref = refs
        qi = pl.program_id(1)
        qb = q_ref[...]
        qs = [_head_sel(qb, h, dk) for h in range(2)]

        def scores(j):
            rows = pl.ds(pl.multiple_of(j * t, t), t)
            kb = k_ref[rows, :]
            out = []
            for h in range(2):
                sc = lax.dot_general(qs[h], _head_other(kb, h, dk), _NT, preferred_element_type=F32) * scale
                out.append(sc - cr_ref[h, j] if bias else sc)
            return tuple(out)

        def update(scs, j, state):
            rows = pl.ds(pl.multiple_of(j * t, t), t)
            vb = v_ref[rows, :]
            new = []
            for h in range(2):
                m, l, acc = state[3 * h:3 * h + 3]
                sc = scs[h]
                m_new = jnp.maximum(m, jnp.max(sc, axis=1, keepdims=True))
                a = jnp.exp(m - m_new)
                p = jnp.exp(sc - m_new)
                l = a * l + jnp.sum(p, axis=1, keepdims=True)
                acc = a * acc + jnp.dot(p.astype(BF16), vb, preferred_element_type=F32)
                new += [m_new, l, acc]
            return tuple(new)

        def step(j, carry):
            nxt = scores(j + 1)
            return nxt + update(carry[:2], j, carry[2:])

        one = (jnp.full((t, 1), -jnp.inf, F32), jnp.zeros((t, 1), F32), jnp.zeros((t, LANES), F32))
        carry = lax.fori_loop(0, qi, step, scores(0) + one + one)
        row = lax.broadcasted_iota(jnp.int32, (t, t), 0)
        col = lax.broadcasted_iota(jnp.int32, (t, t), 1)
        diag = tuple(jnp.where(col <= row, sc, -jnp.inf) for sc in carry[:2])
        state = update(diag, qi, carry[2:])
        outs = []
        for h in range(2):
            m, l, acc = state[3 * h:3 * h + 3]
            outs.append(acc / l)
            lse = m + jnp.log(l)
            lse_ref[h] = lse + cc_ref[h] if bias else lse
        o_ref[...] = _pair(outs)

    in_specs = [pl.BlockSpec((t, wq), lambda p, i: (i, qo // wq + p)), pl.BlockSpec((s, wq), lambda p, i: (0, ko // wq + p)),
                pl.BlockSpec((s, LANES), lambda p, i: (0, vo // LANES + p))]
    args = [qa, ka, va]
    if bias:
        in_specs += [pl.BlockSpec((2, t, 1), lambda p, i: (p, i, 0)), pl.BlockSpec((2, nb, 1, t), lambda p, i: (p, 0, 0, 0))]
        args += [ccol, crow.reshape(2 * npairs, nb, 1, t)]
    return pl.pallas_call(
        body, name=name, grid=(npairs, nb), in_specs=in_specs,
        out_specs=[pl.BlockSpec((t, LANES), lambda p, i: (i, p)), pl.BlockSpec((2, t, 1), lambda p, i: (p, i, 0))],
        out_shape=[jax.ShapeDtypeStruct((s, npairs * LANES), F32), jax.ShapeDtypeStruct((2 * npairs, s, 1), F32)],
        compiler_params=_cparams((_PAR, _PAR)),
    )(*args)


def _attn_bwd(q, k, v, o, do, lse, ccol, crow, *, dk, npairs, scale, dq_dtype, name):
    (qa, qo), (ka, ko), (va, vo), (oa, oo), (da, do_o) = q, k, v, o, do
    s = qa.shape[0]
    wq = 2 * dk
    t = _pick(s, (512, 256, 128))
    nb = s // t
    bias = ccol is not None

    def body(*refs):
        if bias:
            (q_ref, k_ref, v_ref, o_ref, do_ref, lse_ref, cc_ref, cr_ref,
             dq_ref, dk_ref, dv_ref, dc_ref, dcc_ref, dq_s, dk_s, dv_s, dc_s) = refs
        else:
            q_ref, k_ref, v_ref, o_ref, do_ref, lse_ref, dq_ref, dk_ref, dv_ref, dq_s, dk_s, dv_s = refs
        ki, qi = pl.program_id(1), pl.program_id(2)

        @pl.when((ki == 0) & (qi == 0))
        def _():
            dq_s[...] = jnp.zeros_like(dq_s)
            if bias:
                dcc_ref[...] = jnp.zeros_like(dcc_ref)

        @pl.when(qi == ki)
        def _():
            dk_s[...] = jnp.zeros_like(dk_s)
            dv_s[...] = jnp.zeros_like(dv_s)
            if bias:
                dc_s[...] = jnp.zeros_like(dc_s)

        @pl.when(qi >= ki)
        def _():
            qb, kb, vb, dob, ob = q_ref[...], k_ref[...], v_ref[...], do_ref[...], o_ref[...]
            rows = pl.ds(pl.multiple_of(qi * t, t), t)
            row = lax.broadcasted_iota(jnp.int32, (t, t), 0) + qi * t
            col = lax.broadcasted_iota(jnp.int32, (t, t), 1) + ki * t
            dq_parts, dk_parts, dv_parts = [], [], []
            for h in range(2):
                qh, kh = _head_sel(qb, h, dk), _head_other(kb, h, dk)
                sc = lax.dot_general(qh, kh, _NT, preferred_element_type=F32) * scale
                if bias:
                    sc = sc + cc_ref[h] - cr_ref[h]
                p = jnp.exp(jnp.where(col <= row, sc, -jnp.inf) - lse_ref[h])
                doh = _head_sel(dob, h, HALF)
                delta = jnp.sum(doh.astype(F32) * ob, axis=1, keepdims=True)
                dp = lax.dot_general(doh, vb, _NT, preferred_element_type=F32)
                ds = p * (dp - delta)
                dsb = (ds * scale).astype(BF16)
                dv_parts.append(lax.dot_general(p.astype(BF16), dob, _TN, preferred_element_type=F32))
                dk_parts.append(lax.dot_general(dsb, _head_other(qb, h, dk), _TN, preferred_element_type=F32))
                dq_parts.append(jnp.dot(dsb, kh, preferred_element_type=F32))
                if bias:
                    dc_s[h] -= jnp.sum(ds, axis=0, keepdims=True)
                    dcc_ref[h, rows, :] += jnp.sum(ds, axis=1, keepdims=True)
            dv_s[...] += _pair(dv_parts)
            dk_s[...] += _pair(dk_parts, dk)
            dq_s[rows, :] += _pair(dq_parts, dk)

        @pl.when(qi == nb - 1)
        def _():
            dk_ref[...] = dk_s[...].astype(dk_ref.dtype)
            dv_ref[...] = dv_s[...].astype(dv_ref.dtype)
            if bias:
                dc_ref[...] = dc_s[...]

        @pl.when((ki == nb - 1) & (qi == nb - 1))
        def _():
            dq_ref[...] = dq_s[...].astype(dq_ref.dtype)

    qrow = lambda p, j, i: jnp.maximum(i, j)
    in_specs = [pl.BlockSpec((t, wq), lambda p, j, i: (qrow(p, j, i), qo // wq + p)),
                pl.BlockSpec((t, wq), lambda p, j, i: (j, ko // wq + p)),
                pl.BlockSpec((t, LANES), lambda p, j, i: (j, vo // LANES + p)),
                pl.BlockSpec((t, LANES), lambda p, j, i: (qrow(p, j, i), oo // LANES + p)),
                pl.BlockSpec((t, LANES), lambda p, j, i: (qrow(p, j, i), do_o // LANES + p)),
                pl.BlockSpec((2, t, 1), lambda p, j, i: (p, qrow(p, j, i), 0))]
    args = [qa, ka, va, oa, da, lse]
    out_specs = [pl.BlockSpec((s, wq), lambda p, j, i: (0, p)), pl.BlockSpec((t, wq), lambda p, j, i: (j, p)),
                 pl.BlockSpec((t, LANES), lambda p, j, i: (j, p))]
    out_shape = [jax.ShapeDtypeStruct((s, npairs * wq), dq_dtype), jax.ShapeDtypeStruct((s, npairs * wq), dq_dtype),
                 jax.ShapeDtypeStruct((s, npairs * LANES), BF16)]
    scratch = [pltpu.VMEM((s, wq), F32), pltpu.VMEM((t, wq), F32), pltpu.VMEM((t, LANES), F32)]
    if bias:
        in_specs += [pl.BlockSpec((2, t, 1), lambda p, j, i: (p, qrow(p, j, i), 0)), pl.BlockSpec((2, 1, t), lambda p, j, i: (p, 0, j))]
        args += [ccol, crow]
        out_specs += [pl.BlockSpec((2, 1, t), lambda p, j, i: (p, 0, j)), pl.BlockSpec((2, s, 1), lambda p, j, i: (p, 0, 0))]
        out_shape += [jax.ShapeDtypeStruct((2 * npairs, 1, s), F32), jax.ShapeDtypeStruct((2 * npairs, s, 1), F32)]
        scratch.append(pltpu.VMEM((2, 1, t), F32))
    return pl.pallas_call(
        body, name=name, grid=(npairs, nb, nb), in_specs=in_specs, out_specs=out_specs, out_shape=out_shape,
        scratch_shapes=scratch, compiler_params=_cparams((_PAR, _ARB, _ARB)),
    )(*args)


def _swa_scores(qh, kw, start, t, scale):
    sc = lax.dot_general(qh, kw, _NT, preferred_element_type=F32) * scale
    row = lax.broadcasted_iota(jnp.int32, (t, t + WINDOW), 0)
    col = lax.broadcasted_iota(jnp.int32, (t, t + WINDOW), 1)
    diff = row - col + WINDOW
    valid = (diff >= 0) & (diff < WINDOW) & (col + start >= WINDOW)
    return jnp.where(valid, sc, -jnp.inf)


def _swa_fwd(q, kp, vp, sink, *, name):
    s = q.shape[0]
    npairs = SWA_HEADS // 2
    t = _pick(s, (256, 128))
    scale = SWA_DIM ** -0.5

    def body(q_ref, k_ref, v_ref, sk_ref, o_ref, lse_ref):
        start = pl.multiple_of(pl.program_id(1) * t, t)
        kw = k_ref[pl.ds(start, t + WINDOW), :]
        vw = v_ref[pl.ds(start, t + WINDOW), :]
        qb = q_ref[...]
        outs = []
        for h in range(2):
            sc = _swa_scores(_head_sel(qb, h, HALF), kw, start, t, scale)
            snk = sk_ref[h][:, 0:1]
            m = jnp.maximum(jnp.max(sc, axis=1, keepdims=True), snk)
            e = jnp.exp(sc - m)
            l = jnp.sum(e, axis=1, keepdims=True) + jnp.exp(snk - m)
            outs.append(jnp.dot((e / l).astype(BF16), vw, preferred_element_type=F32))
            lse_ref[h] = m + jnp.log(l)
        o_ref[...] = _pair(outs)

    kvspec = pl.BlockSpec((s + WINDOW, LANES), lambda p, i: (0, 0))
    return pl.pallas_call(
        body, name=name, grid=(npairs, s // t),
        in_specs=[pl.BlockSpec((t, LANES), lambda p, i: (i, p)), kvspec, kvspec, pl.BlockSpec((2, 1, LANES), lambda p, i: (p, 0, 0))],
        out_specs=[pl.BlockSpec((t, LANES), lambda p, i: (i, p)), pl.BlockSpec((2, t, 1), lambda p, i: (p, i, 0))],
        out_shape=[jax.ShapeDtypeStruct((s, npairs * LANES), F32), jax.ShapeDtypeStruct((SWA_HEADS, s, 1), F32)],
        compiler_params=_cparams((_PAR, _PAR)),
    )(q, kp, vp, sink)


def _swa_bwd(q, kp, vp, sink, o, do, lse, *, name):
    s = q.shape[0]
    npairs = SWA_HEADS // 2
    t = _pick(s, (256, 128))
    scale = SWA_DIM ** -0.5

    def body(q_ref, k_ref, v_ref, sk_ref, o_ref, do_ref, lse_ref, dq_ref, dk_ref, dv_ref, dsk_ref):
        pp, i = pl.program_id(0), pl.program_id(1)
        start = pl.multiple_of(i * t, t)

        @pl.when((pp == 0) & (i == 0))
        def _():
            dk_ref[...] = jnp.zeros_like(dk_ref)
            dv_ref[...] = jnp.zeros_like(dv_ref)

        @pl.when(i == 0)
        def _():
            dsk_ref[...] = jnp.zeros_like(dsk_ref)

        win = pl.ds(start, t + WINDOW)
        qb, dob, ob = q_ref[...], do_ref[...], o_ref[...]
        kw, vw = k_ref[win, :], v_ref[win, :]
        dq_parts, dk_parts, dv_parts = [], [], []
        for h in range(2):
            sc = _swa_scores(_head_sel(qb, h, HALF), kw, start, t, scale)
            lse_b = lse_ref[h]
            p = jnp.exp(sc - lse_b)
            doh = _head_sel(dob, h, HALF)
            delta = jnp.sum(doh.astype(F32) * ob, axis=1, keepdims=True)
            dp = lax.dot_general(doh, vw, _NT, preferred_element_type=F32)
            ds = p * (dp - delta)
            dsb = (ds * scale).astype(BF16)
            dq_parts.append(jnp.dot(dsb, kw, preferred_element_type=F32))
            dk_parts.append(lax.dot_general(dsb, qb, _TN, preferred_element_type=F32))
            dv_parts.append(lax.dot_general(p.astype(BF16), dob, _TN, preferred_element_type=F32))
            psink = jnp.exp(sk_ref[h][:, 0:1] - lse_b)
            dsk_ref[h] += jnp.broadcast_to(-jnp.sum(psink * delta, axis=0, keepdims=True), (1, LANES))
        dq_ref[...] = _pair(dq_parts)
        dk_ref[win, :] += _pair(dk_parts)
        dv_ref[win, :] += _pair(dv_parts)

    tile = pl.BlockSpec((t, LANES), lambda p, i: (i, p))
    kvspec = pl.BlockSpec((s + WINDOW, LANES), lambda p, i: (0, 0))
    skspec = pl.BlockSpec((2, 1, LANES), lambda p, i: (p, 0, 0))
    return pl.pallas_call(
        body, name=name, grid=(npairs, s // t),
        in_specs=[tile, kvspec, kvspec, skspec, tile, tile, pl.BlockSpec((2, t, 1), lambda p, i: (p, i, 0))],
        out_specs=[tile, kvspec, kvspec, skspec],
        out_shape=[jax.ShapeDtypeStruct((s, npairs * LANES), F32), jax.ShapeDtypeStruct((s + WINDOW, LANES), F32),
                   jax.ShapeDtypeStruct((s + WINDOW, LANES), F32), jax.ShapeDtypeStruct((SWA_HEADS, 1, LANES), F32)],
        compiler_params=_cparams((_ARB, _ARB)),
    )(q, kp, vp, sink, o, do, lse)


def _odd_rope(h, cos, sin, *, name):
    s = h.shape[0]
    tb = _pick(s, (512, 256, 128))
    nq = SWA_HEADS * SWA_DIM

    def body(q_ref, kv_ref, cos_ref, sin_ref, qo_ref, ko_ref, vo_ref):
        cs, sn = cos_ref[...], sin_ref[...]
        for j in range(nq // LANES):
            sl = slice(j * LANES, (j + 1) * LANES)
            qo_ref[:, sl] = _rope_tile(q_ref[:, sl], cs, sn, SWA_DIM // 2).astype(BF16)
        ko_ref[...] = _rope_tile(kv_ref[:, :LANES], cs, sn, SWA_DIM // 2).astype(BF16)
        vo_ref[...] = kv_ref[:, LANES:].astype(BF16)

    rows = lambda w, c=0: pl.BlockSpec((tb, w), lambda i: (i, c))
    return pl.pallas_call(
        body, name=name, grid=(s // tb,),
        in_specs=[rows(nq, O_Q // nq), rows(2 * LANES, O_K // (2 * LANES)), rows(LANES), rows(LANES)],
        out_specs=[rows(nq), rows(LANES), rows(LANES)],
        out_shape=[jax.ShapeDtypeStruct((s, nq), BF16), jax.ShapeDtypeStruct((s, LANES), BF16), jax.ShapeDtypeStruct((s, LANES), BF16)],
        compiler_params=_cparams((_PAR,)),
    )(h, h, cos, sin)


def _odd_rope_bwd(dq, dk, cos, sin, *, name):
    s = dq.shape[0]
    tb = _pick(s, (512, 256, 128))
    nq = SWA_HEADS * SWA_DIM

    def body(dq_ref, dk_ref, cos_ref, sin_ref, qo_ref, ko_ref):
        cs, sn = cos_ref[...], -sin_ref[...]
        for j in range(nq // LANES):
            sl = slice(j * LANES, (j + 1) * LANES)
            qo_ref[:, sl] = _rope_tile(dq_ref[:, sl], cs, sn, SWA_DIM // 2).astype(BF16)
        ko_ref[...] = _rope_tile(dk_ref[...], cs, sn, SWA_DIM // 2).astype(BF16)

    rows = lambda w: pl.BlockSpec((tb, w), lambda i: (i, 0))
    return pl.pallas_call(
        body, name=name, grid=(s // tb,), in_specs=[rows(nq), rows(LANES), rows(LANES), rows(LANES)],
        out_specs=[rows(nq), rows(LANES)],
        out_shape=[jax.ShapeDtypeStruct((s, nq), BF16), jax.ShapeDtypeStruct((s, LANES), BF16)],
        compiler_params=_cparams((_PAR,)),
    )(dq, dk, cos, sin)


def _out_fwd(o_parts, h, x, w_out, g, b, *, name):
    s = x.shape[0]
    tb = _pick(s, (256, 128))
    n_o = len(o_parts)

    def body(*refs):
        o_refs = refs[:n_o]
        gate_ref, x_ref, w_ref, g_ref, b_ref, xn_ref, xb_ref, z_ref, xh_ref, rs_ref = refs[n_o:]
        gate = gate_ref[...]
        o = o_refs[0][...] if n_o == 1 else jnp.concatenate([r[...] for r in o_refs], axis=1)
        z = (o * (gate * _sigmoid(gate))).astype(BF16)
        z_ref[...] = z
        r = ALPHA * x_ref[...] + jnp.dot(z, w_ref[...], preferred_element_type=F32)
        mu = jnp.mean(r, axis=-1, keepdims=True)
        rc = r - mu
        rstd = lax.rsqrt(jnp.mean(rc * rc, axis=-1, keepdims=True) + LN_EPS)
        xh = rc * rstd
        xn = xh * g_ref[...] + b_ref[...]
        xh_ref[...] = xh
        rs_ref[...] = rstd
        xn_ref[...] = xn
        xb_ref[...] = xn.astype(BF16)

    rows = lambda w: pl.BlockSpec((tb, w), lambda i: (i, 0))
    full = lambda a: pl.BlockSpec(a.shape, lambda i: (0,) * a.ndim)
    return pl.pallas_call(
        body, name=name, grid=(s // tb,),
        in_specs=[rows(a.shape[1]) for a in o_parts] + [rows(D_MODEL), rows(D_MODEL), full(w_out), full(g), full(b)],
        out_specs=[rows(D_MODEL), rows(D_MODEL), rows(D_MODEL), rows(D_MODEL), rows(1)],
        out_shape=[jax.ShapeDtypeStruct((s, D_MODEL), F32), jax.ShapeDtypeStruct((s, D_MODEL), BF16),
                   jax.ShapeDtypeStruct((s, D_MODEL), BF16), jax.ShapeDtypeStruct((s, D_MODEL), F32),
                   jax.ShapeDtypeStruct((s, 1), F32)],
        compiler_params=_cparams((_PAR,)),
    )(*o_parts, h, x, w_out, g, b)


def _out_bwd(dxn, xh, rstd, g, w_out_t, o_parts, h, *, name):
    s = dxn.shape[0]
    tb = _pick(s, (256, 128))
    n_o = len(o_parts)

    def body(*refs):
        dxn_ref, xh_ref, rs_ref, g_ref, wt_ref = refs[:5]
        o_refs = refs[5:5 + n_o]
        gate_ref, dr_ref, dy_ref, do_ref, dgate_ref, dg_ref, db_ref = refs[5 + n_o:]

        @pl.when(pl.program_id(0) == 0)
        def _():
            dg_ref[...] = jnp.zeros_like(dg_ref)
            db_ref[...] = jnp.zeros_like(db_ref)

        dxn_b, xh_b = dxn_ref[...], xh_ref[...]
        dg_ref[...] += jnp.sum(dxn_b * xh_b, axis=0, keepdims=True)
        db_ref[...] += jnp.sum(dxn_b, axis=0, keepdims=True)
        dxh = dxn_b * g_ref[...]
        dr = rs_ref[...] * (dxh - jnp.mean(dxh, axis=-1, keepdims=True) - xh_b * jnp.mean(dxh * xh_b, axis=-1, keepdims=True))
        dr_ref[...] = dr
        dy = dr.astype(BF16)
        dy_ref[...] = dy
        dz = jnp.dot(dy, wt_ref[...], preferred_element_type=F32)
        gate = gate_ref[...]
        sg = _sigmoid(gate)
        o = o_refs[0][...] if n_o == 1 else jnp.concatenate([r[...] for r in o_refs], axis=1)
        do_ref[...] = (dz * (gate * sg)).astype(BF16)
        dgate_ref[...] = (dz * o * (sg * (1.0 + gate * (1.0 - sg)))).astype(BF16)

    rows = lambda w: pl.BlockSpec((tb, w), lambda i: (i, 0))
    full = lambda a: pl.BlockSpec(a.shape, lambda i: (0,) * a.ndim)
    return pl.pallas_call(
        body, name=name, grid=(s // tb,),
        in_specs=[rows(D_MODEL), rows(D_MODEL), rows(1), full(g), full(w_out_t)] + [rows(a.shape[1]) for a in o_parts] + [rows(D_MODEL)],
        out_specs=[rows(D_MODEL), rows(D_MODEL), rows(D_MODEL), rows(D_MODEL), full(g), full(g)],
        out_shape=[jax.ShapeDtypeStruct((s, D_MODEL), F32), jax.ShapeDtypeStruct((s, D_MODEL), BF16),
                   jax.ShapeDtypeStruct((s, D_MODEL), BF16), jax.ShapeDtypeStruct((s, D_MODEL), BF16),
                   jax.ShapeDtypeStruct(g.shape, F32), jax.ShapeDtypeStruct(g.shape, F32)],
        compiler_params=_cparams((_ARB,)),
    )(dxn, xh, rstd, g, w_out_t, *o_parts, h)


def _loss_grad(y, target, *, name):
    s, d = y.shape
    tb = _pick(s, (512, 256, 128))

    def body(y_ref, t_ref, dy_ref, l_ref):
        @pl.when(pl.program_id(0) == 0)
        def _():
            l_ref[...] = jnp.zeros_like(l_ref)

        err = y_ref[...] - t_ref[...]
        dy_ref[...] = err * (1.0 / d)
        per_tok = jnp.mean(err * err, axis=-1, keepdims=True)
        l_ref[...] += 0.5 * jnp.sum(per_tok, axis=0, keepdims=True)

    rows = pl.BlockSpec((tb, d), lambda i: (i, 0))
    return pl.pallas_call(
        body, name=name, grid=(s // tb,), in_specs=[rows, rows],
        out_specs=[rows, pl.BlockSpec((8, LANES), lambda i: (0, 0))],
        out_shape=[jax.ShapeDtypeStruct((s, d), F32), jax.ShapeDtypeStruct((8, LANES), F32)],
        compiler_params=_cparams((_ARB,)),
    )(y, target)


def _adamw(w, g, m, v, *, name):
    shape = w.shape
    w3, g3, m3, v3 = (a.reshape((1,) * (3 - a.ndim) + a.shape) for a in (w, g, m, v))
    a0, a1, a2 = w3.shape
    tb = _pick(a1, (256, 128)) if a1 % 8 == 0 else a1
    c1 = 1.0 - ADAM_B1 ** ADAM_STEP
    c2 = 1.0 - ADAM_B2 ** ADAM_STEP

    def body(w_ref, g_ref, m_ref, v_ref, d_ref, mo_ref, vo_ref):
        gg = g_ref[...]
        mn = ADAM_B1 * m_ref[...] + (1.0 - ADAM_B1) * gg
        vn = ADAM_B2 * v_ref[...] + (1.0 - ADAM_B2) * (gg * gg)
        mo_ref[...] = mn
        vo_ref[...] = vn
        d_ref[...] = -ADAM_LR * ((mn / c1) / (jnp.sqrt(vn / c2) + ADAM_EPS) + ADAM_WD * w_ref[...])

    spec = pl.BlockSpec((1, tb, a2), lambda i, j: (i, j, 0))
    outs = pl.pallas_call(
        body, name=name, grid=(a0, a1 // tb), in_specs=[spec] * 4, out_specs=[spec] * 3,
        out_shape=[jax.ShapeDtypeStruct(w3.shape, F32)] * 3, compiler_params=_cparams((_PAR, _PAR)),
    )(w3, g3, m3, v3)
    return tuple(a.reshape(shape) for a in outs)


def _place():
    x, y, c = lax.axis_index("x"), lax.axis_index("y"), lax.axis_index("c")
    return x, y, c, [(1 - x, y), (x, 1 - y), (1 - x, 1 - y)]


_ANY = pl.BlockSpec(memory_space=pl.ANY)


def _sem_pairs(n):
    return [pltpu.SemaphoreType.DMA((n,)), pltpu.SemaphoreType.DMA((n,))]


def _gather_chip_shards(tensors, *, name):
    nt = len(tensors)
    per = 7

    def body(*refs):
        srcs, outs = refs[:nt], refs[nt:2 * nt]
        send_sems, recv_sems = refs[2 * nt:]
        x, y, c, chips = _place()
        me = 2 * x + y
        sib = (x, y, 1 - c)

        def half(t, chip, hc):
            rh = tensors[t].shape[1] // 2
            return outs[t].at[chip, :, pl.ds(hc * rh, rh), :]

        def copy(t, kk, s_ref, d_ref, to):
            return pltpu.make_async_remote_copy(src_ref=s_ref, dst_ref=d_ref, send_sem=send_sems.at[per * t + kk],
                                                recv_sem=recv_sems.at[per * t + kk], device_id=to, device_id_type=MESH)

        sent = []
        for j, (cx, cy) in enumerate(chips):
            for t in range(nt):
                rh = tensors[t].shape[1] // 2
                cp = copy(t, j, srcs[t].at[:, pl.ds(c * rh, rh), :], half(t, me, c), (cx, cy, c))
                cp.start()
                sent.append(cp)
        own = [copy(t, 6, srcs[t], outs[t].at[me], sib) for t in range(nt)]
        for cp in own:
            cp.start()
        for j, (cx, cy) in enumerate(chips):
            for t in range(nt):
                landed = half(t, 2 * cx + cy, c)
                copy(t, j, landed, landed, (cx, cy, c)).wait_recv()
                fw = copy(t, 3 + j, landed, landed, sib)
                fw.start()
                sent.append(fw)
        for j, (cx, cy) in enumerate(chips):
            for t in range(nt):
                theirs = half(t, 2 * cx + cy, 1 - c)
                copy(t, 3 + j, theirs, theirs, sib).wait_recv()
        for cp in own:
            cp.wait()
        for cp in sent:
            cp.wait_send()

    return pl.pallas_call(
        body, name=name, in_specs=[_ANY] * nt, out_specs=[_ANY] * nt,
        out_shape=[jax.ShapeDtypeStruct((N_CHIPS,) + a.shape, a.dtype) for a in tensors],
        scratch_shapes=_sem_pairs(per * nt),
    )(*tensors)


def _swap_halves(gs, *, name):
    nt = len(gs)

    def body(*refs):
        srcs, outs = refs[:nt], refs[nt:2 * nt]
        send_sems, recv_sems = refs[2 * nt:]
        x, y, c, _ = _place()
        cps = [pltpu.make_async_remote_copy(src_ref=srcs[t].at[:, 1 - c], dst_ref=outs[t], send_sem=send_sems.at[t],
                                            recv_sem=recv_sems.at[t], device_id=(x, y, 1 - c), device_id_type=MESH)
               for t in range(nt)]
        for cp in cps:
            cp.start()
        for cp in cps:
            cp.wait()

    return pl.pallas_call(
        body, name=name, in_specs=[_ANY] * nt, out_specs=[_ANY] * nt,
        out_shape=[jax.ShapeDtypeStruct((N_CHIPS,) + g.shape[2:], g.dtype) for g in gs], scratch_shapes=_sem_pairs(nt),
    )(*gs)


def _scatter_to_chips(blocks, *, name):
    nt = len(blocks)

    def body(*refs):
        srcs, outs = refs[:nt], refs[nt:2 * nt]
        send_sems, recv_sems = refs[2 * nt:]
        x, y, c, chips = _place()
        cps = [pltpu.make_async_remote_copy(src_ref=srcs[t].at[2 * cx + cy], dst_ref=outs[t].at[j],
                                            send_sem=send_sems.at[3 * t + j], recv_sem=recv_sems.at[3 * t + j],
                                            device_id=(cx, cy, c), device_id_type=MESH)
               for j, (cx, cy) in enumerate(chips) for t in range(nt)]
        for cp in cps:
            cp.start()
        for cp in cps:
            cp.wait()

    return pl.pallas_call(
        body, name=name, in_specs=[_ANY] * nt, out_specs=[_ANY] * nt,
        out_shape=[jax.ShapeDtypeStruct((3,) + a.shape[1:], a.dtype) for a in blocks], scratch_shapes=_sem_pairs(3 * nt),
    )(*blocks)


def _join_halves(bufs, *, name):
    nt = len(bufs)

    def body(*refs):
        srcs, outs = refs[:nt], refs[nt:2 * nt]
        send_sems, recv_sems = refs[2 * nt:]
        x, y, c, _ = _place()
        cps = []
        for t in range(nt):
            rh = bufs[t].shape[1] // 2
            rows = pl.ds(c * rh, rh)
            cps.append(pltpu.make_async_remote_copy(src_ref=srcs[t].at[:, rows, :], dst_ref=outs[t].at[:, rows, :],
                                                    send_sem=send_sems.at[t], recv_sem=recv_sems.at[t],
                                                    device_id=(x, y, 1 - c), device_id_type=MESH))
        for cp in cps:
            cp.start()
        for cp in cps:
            cp.wait()

    return pl.pallas_call(
        body, name=name, in_specs=[_ANY] * nt, out_specs=[_ANY] * nt,
        out_shape=[jax.ShapeDtypeStruct(b.shape, b.dtype) for b in bufs], scratch_shapes=_sem_pairs(nt),
        input_output_aliases={t: t for t in range(nt)},
    )(*bufs)


def _add_sibling(g, recv, cidx, chip_idx, *, name):
    _, _, na, rh, cdim = g.shape
    tb = _pick(rh, (256, 128, 64, 32, 16))

    def body(c_ref, k_ref, g_ref, r_ref, s_ref, o_ref):
        tot = g_ref[0, 0] + r_ref[0]
        s_ref[0] = tot.astype(BF16)

        @pl.when(pl.program_id(2) == k_ref[0])
        def _():
            o_ref[...] = tot

    return pl.pallas_call(
        body, name=name,
        grid_spec=pltpu.PrefetchScalarGridSpec(
            num_scalar_prefetch=2, grid=(na, rh // tb, N_CHIPS),
            in_specs=[pl.BlockSpec((1, 1, 1, tb, cdim), lambda a, i, k, c_ref, k_ref: (k, c_ref[0], a, i, 0)),
                      pl.BlockSpec((1, 1, tb, cdim), lambda a, i, k, c_ref, k_ref: (k, a, i, 0))],
            out_specs=[pl.BlockSpec((1, 1, tb, cdim), lambda a, i, k, c_ref, k_ref: (k, a, i, 0)),
                       pl.BlockSpec((1, tb, cdim), lambda a, i, k, c_ref, k_ref: (a, i, 0))]),
        out_shape=[jax.ShapeDtypeStruct((N_CHIPS, na, rh, cdim), BF16), jax.ShapeDtypeStruct((na, rh, cdim), F32)],
        compiler_params=_cparams((_PAR, _PAR, _ARB)),
    )(cidx, chip_idx, g, recv)


def _add_chips(own, recv, cidx, *, name):
    na, rh, cdim = own.shape
    tb = _pick(rh, (256, 128, 64, 32, 16))
    nblk = rh // tb

    def body(c_ref, a_ref, r0_ref, r1_ref, r2_ref, o_ref):
        o_ref[...] = ((a_ref[...] + r0_ref[0].astype(F32)) + r1_ref[0].astype(F32)) + r2_ref[0].astype(F32)

    slot = lambda j: pl.BlockSpec((1, 1, tb, cdim), lambda a, i, c_ref: (j, a, i, 0))
    return pl.pallas_call(
        body, name=name,
        grid_spec=pltpu.PrefetchScalarGridSpec(
            num_scalar_prefetch=1, grid=(na, nblk),
            in_specs=[pl.BlockSpec((1, tb, cdim), lambda a, i, c_ref: (a, i, 0)), slot(0), slot(1), slot(2)],
            out_specs=pl.BlockSpec((1, tb, cdim), lambda a, i, c_ref: (a, c_ref[0] * nblk + i, 0))),
        out_shape=jax.ShapeDtypeStruct((na, 2 * rh, cdim), F32), compiler_params=_cparams((_PAR, _PAR)),
    )(cidx, own, recv, recv, recv)


def _all_reduce_small(v, *, name):
    r, cdim = v.shape

    def body(v_ref, o_ref, buf, send_sems, recv_sems):
        x, y, c, _ = _place()
        me = 4 * x + 2 * y + c
        buf[me] = v_ref[...]
        cps = []
        for p in range(1, N_DEV):
            to = (1 - x if p & 4 else x, 1 - y if p & 2 else y, 1 - c if p & 1 else c)
            cp = pltpu.make_async_remote_copy(src_ref=v_ref, dst_ref=buf.at[me], send_sem=send_sems.at[p - 1],
                                              recv_sem=recv_sems.at[p - 1], device_id=to, device_id_type=MESH)
            cp.start()
            cps.append(cp)
        for p in range(1, N_DEV):
            frm = (4 * x + 2 * y + c) ^ p
            pltpu.make_async_remote_copy(src_ref=v_ref, dst_ref=buf.at[frm], send_sem=send_sems.at[p - 1],
                                         recv_sem=recv_sems.at[p - 1], device_id=(x, y, c), device_id_type=MESH).wait_recv()
        for cp in cps:
            cp.wait_send()
        tot = buf[0]
        for i in range(1, N_DEV):
            tot = tot + buf[i]
        o_ref[...] = tot

    vm = pl.BlockSpec(memory_space=pltpu.VMEM)
    return pl.pallas_call(
        body, name=name, in_specs=[vm], out_specs=vm, out_shape=jax.ShapeDtypeStruct((r, cdim), F32),
        scratch_shapes=[pltpu.VMEM((N_DEV, r, cdim), F32), pltpu.SemaphoreType.DMA((N_DEV - 1,)), pltpu.SemaphoreType.DMA((N_DEV - 1,))],
    )(v)


_SHARDED = ("even_w_in", "even_w_uq", "even_w_ukv", "even_w_out", "odd_w_in", "odd_w_out")
_COL_SHARDED = ("even_w_in", "even_w_uq", "even_w_ukv", "odd_w_in")
PACK_COLS = 1024


def _unshard(name, stacked):
    n, a, b, cc = stacked.shape
    if name in _COL_SHARDED:
        return stacked.transpose(1, 2, 0, 3).reshape(a, b, n * cc)
    return stacked.transpose(1, 0, 2, 3).reshape(a, n * b, cc)


def _chip_halves(name, full):
    a, b, cc = full.shape
    if name in _COL_SHARDED:
        return full.reshape(a, 2, b // 2, N_CHIPS, cc // N_CHIPS).transpose(3, 1, 0, 2, 4)
    return full.reshape(a, N_CHIPS, 2, b // (2 * N_CHIPS), cc).transpose(1, 2, 0, 3, 4)


def _rope_tables(s):
    pos = jnp.arange(s, dtype=F32)

    def ang(d):
        inv = ROPE_THETA ** (-jnp.arange(0, d, 2, dtype=F32) / d)
        a = pos[:, None] * inv[None, :]
        return jnp.cos(a), jnp.sin(a)

    c16, s16 = ang(MLA_ROPE)
    one, zero = jnp.ones((s, KPE_LANE), F32), jnp.zeros((s, KPE_LANE), F32)
    cos_m = jnp.concatenate([one, c16, c16, one[:, :32]], axis=1)
    sin_m = jnp.concatenate([zero, -s16, s16, zero[:, :32]], axis=1)
    c32, s32 = ang(SWA_DIM)
    cos_s = jnp.concatenate([c32, c32, c32, c32], axis=1)
    sin_s = jnp.concatenate([-s32, s32, -s32, s32], axis=1)
    return cos_m, sin_m, cos_s, sin_s


def _even_weights(w_in, w_uq, w_ukv):
    zeros = lambda n: jnp.zeros((D_MODEL, n), w_in.dtype)
    wcq, wckv, wkpe = w_in[:, 0:256], w_in[:, 256:384], w_in[:, 384:416]
    wfq, wfk, wfv = w_in[:, 416:928], w_in[:, 928:1440], w_in[:, 1440:1952]
    wfl, wg = w_in[:, 1952:1960], w_in[:, 1960:2984]
    misc = jnp.concatenate([zeros(KPE_LANE), wkpe, wfl, zeros(LANES - FL_LANE - FOX_HEADS)], axis=1)
    w_in_p = jnp.concatenate([wg, wfq, wfk, wfv, wcq, wckv, misc], axis=1)
    uq = w_uq.reshape(MLA_Q_RANK, MLA_HEADS, MLA_NOPE + MLA_ROPE)
    uq_p = jnp.pad(uq, ((0, 0), (0, 0), (0, HEAD_PAD - MLA_NOPE - MLA_ROPE))).reshape(MLA_Q_RANK, MLA_HEADS * HEAD_PAD)
    ukv = w_ukv.reshape(MLA_KV_RANK, MLA_HEADS, MLA_NOPE + MLA_V)
    uk_p = jnp.pad(ukv[..., :MLA_NOPE], ((0, 0), (0, 0), (0, HEAD_PAD - MLA_NOPE))).reshape(MLA_KV_RANK, MLA_HEADS * HEAD_PAD)
    uv = ukv[..., MLA_NOPE:].reshape(MLA_KV_RANK, MLA_HEADS * MLA_V)
    ukv_p = jnp.concatenate([uk_p, uv], axis=1)
    return w_in_p, w_in_p.T, uq_p, uq_p.T, ukv_p, ukv_p.T


def _even_weight_grads(dw_in_p, duq_p, dukv_p):
    g = dw_in_p
    gate, fq, fk, fv = g[:, E_GATE:E_FQ], g[:, E_FQ:E_FK], g[:, E_FK:E_FV], g[:, E_FV:E_SMALL]
    cq, ckv, misc = g[:, E_SMALL:E_SMALL + 256], g[:, E_SMALL + 256:E_SMALL + 384], g[:, E_SMALL + 384:]
    dw_in = jnp.concatenate([cq, ckv, misc[:, KPE_LANE:KPE_LANE + MLA_ROPE], fq, fk, fv, misc[:, FL_LANE:FL_LANE + FOX_HEADS], gate], axis=1)
    duq = duq_p.reshape(MLA_Q_RANK, MLA_HEADS, HEAD_PAD)[..., :MLA_NOPE + MLA_ROPE].reshape(MLA_Q_RANK, -1)
    nq = MLA_HEADS * HEAD_PAD
    dk = dukv_p[:, :nq].reshape(MLA_KV_RANK, MLA_HEADS, HEAD_PAD)[..., :MLA_NOPE]
    dvv = dukv_p[:, nq:].reshape(MLA_KV_RANK, MLA_HEADS, MLA_V)
    dukv = jnp.concatenate([dk, dvv], axis=-1).reshape(MLA_KV_RANK, -1)
    return dw_in, duq, dukv


def _interleave(w, forward):
    a, b = (SWA_KV_HEADS, SWA_HEADS // SWA_KV_HEADS) if forward else (SWA_HEADS // SWA_KV_HEADS, SWA_KV_HEADS)
    return w.reshape(w.shape[0], a, b, -1).transpose(0, 2, 1, 3).reshape(w.shape[0], -1)


def _odd_weights(w_in, w_out):
    q, k, v, gate = w_in[:, 0:1024], w_in[:, 1024:1152], w_in[:, 1152:1280], w_in[:, 1280:2304]
    w_p = jnp.concatenate([_interleave(gate, True), _interleave(q, True), k, v], axis=1)
    w_out_p = _interleave(w_out.T, True).T
    return w_p, w_p.T, w_out_p, w_out_p.T


def _odd_weight_grads(g, dw_out_p):
    dw_in = jnp.concatenate([_interleave(g[:, O_Q:O_K], False), g[:, O_K:O_V], g[:, O_V:], _interleave(g[:, O_GATE:O_Q], False)], axis=1)
    return dw_in, _interleave(dw_out_p.T, False).T


def kernel(x, even_w_in, even_q_norm, even_w_uq, even_kv_norm, even_w_ukv, even_b_f, even_w_out, even_ln_g, even_ln_b, odd_w_in, odd_sinks, odd_w_out, odd_ln_g, odd_ln_b, loss_target, m_even_w_in, m_even_q_norm, m_even_w_uq, m_even_kv_norm, m_even_w_ukv, m_even_b_f, m_even_w_out, m_even_ln_g, m_even_ln_b, m_odd_w_in, m_odd_sinks, m_odd_w_out, m_odd_ln_g, m_odd_ln_b, v_even_w_in, v_even_q_norm, v_even_w_uq, v_even_kv_norm, v_even_w_ukv, v_even_b_f, v_even_w_out, v_even_ln_g, v_even_ln_b, v_odd_w_in, v_odd_sinks, v_odd_w_out, v_odd_ln_g, v_odd_ln_b):
    weights = dict(even_w_in=even_w_in, even_q_norm=even_q_norm, even_w_uq=even_w_uq, even_kv_norm=even_kv_norm,
                   even_w_ukv=even_w_ukv, even_b_f=even_b_f, even_w_out=even_w_out, even_ln_g=even_ln_g, even_ln_b=even_ln_b,
                   odd_w_in=odd_w_in, odd_sinks=odd_sinks, odd_w_out=odd_w_out, odd_ln_g=odd_ln_g, odd_ln_b=odd_ln_b)
    mom_m = dict(even_w_in=m_even_w_in, even_q_norm=m_even_q_norm, even_w_uq=m_even_w_uq, even_kv_norm=m_even_kv_norm,
                 even_w_ukv=m_even_w_ukv, even_b_f=m_even_b_f, even_w_out=m_even_w_out, even_ln_g=m_even_ln_g, even_ln_b=m_even_ln_b,
                 odd_w_in=m_odd_w_in, odd_sinks=m_odd_sinks, odd_w_out=m_odd_w_out, odd_ln_g=m_odd_ln_g, odd_ln_b=m_odd_ln_b)
    mom_v = dict(even_w_in=v_even_w_in, even_q_norm=v_even_q_norm, even_w_uq=v_even_w_uq, even_kv_norm=v_even_kv_norm,
                 even_w_ukv=v_even_w_ukv, even_b_f=v_even_b_f, even_w_out=v_even_w_out, even_ln_g=v_even_ln_g, even_ln_b=v_even_ln_b,
                 odd_w_in=v_odd_w_in, odd_sinks=v_odd_sinks, odd_w_out=v_odd_w_out, odd_ln_g=v_odd_ln_g, odd_ln_b=v_odd_ln_b)
    names = list(weights)
    xl = x[0]
    tgt = loss_target[0]
    s = xl.shape[0]
    ax, ay, ac = lax.axis_index("x"), lax.axis_index("y"), lax.axis_index("c")
    chip = 2 * ax + ay
    c_idx = ac.astype(jnp.int32).reshape(1)
    chip_idx = chip.astype(jnp.int32).reshape(1)

    ln_odd = jnp.pad(jnp.concatenate([odd_ln_g, odd_ln_b]), ((0, 12), (0, 0)))[None]
    gathered = _gather_chip_shards([weights[n].astype(BF16) for n in _SHARDED] + [ln_odd], name="gather_weights")
    full = {n: _unshard(n, gathered[i]) for i, n in enumerate(_SHARDED)}
    ln_all = gathered[-1][:, 0]
    odd_g_full = ln_all[:, 0:2].transpose(1, 0, 2).reshape(2, D_MODEL)
    odd_b_full = ln_all[:, 2:4].transpose(1, 0, 2).reshape(2, D_MODEL)

    cos_m, sin_m, cos_s, sin_s = _rope_tables(s)
    bf_tiles = jnp.pad(even_b_f, ((0, 0), (FL_LANE, LANES - FL_LANE - FOX_HEADS)))
    sink_tiles = jnp.broadcast_to(_interleave(odd_sinks, True)[:, :, None, None], (2, SWA_HEADS, 1, LANES))
    mla_scale = (MLA_NOPE + MLA_ROPE) ** -0.5
    fox_scale = FOX_DIM ** -0.5
    mla_pairs, fox_pairs = MLA_HEADS // 2, FOX_HEADS // 2

    saved = []
    x_f, x_b = xl, xl.astype(BF16)
    for layer in range(DEPTH):
        j = layer // 2
        ln = f"L{layer}"
        if layer % 2 == 0:
            w_in_p, w_in_t, uq_p, uq_t, ukv_p, ukv_t = _even_weights(full["even_w_in"][j], full["even_w_uq"][j], full["even_w_ukv"][j])
            w_out, w_out_t = full["even_w_out"][j], full["even_w_out"][j].T
            qg, kg, bft = even_q_norm[j][None], even_kv_norm[j][None], bf_tiles[j][None]
            h, hb = _mm(x_b, w_in_p, out_dtype=F32, name=ln + "_in", also_bf16=True)
            q, k, v, qn, kvn, logf = _even_mid_fwd(h, qg, kg, bft, uq_p, ukv_p, cos_m, sin_m, name=ln + "_mid")
            cum = _cumsum(logf, reverse=False, name=ln + "_cum")[:, FL_LANE:FL_LANE + FOX_HEADS].T
            ccol, crow = cum[:, :, None], cum[:, None, :]
            o_mla, lse_mla = _attn_fwd((q, 0), (k, 0), (v, 0), None, None, dk=HEAD_PAD, npairs=mla_pairs, scale=mla_scale, name=ln + "_mla")
            o_fox, lse_fox = _attn_fwd((hb, E_FQ), (hb, E_FK), (hb, E_FV), ccol, crow, dk=FOX_DIM, npairs=fox_pairs,
                                       scale=fox_scale, name=ln + "_fox")
            o_parts = [o_mla, o_fox]
            g_ln, b_ln = even_ln_g[j][None], even_ln_b[j][None]
            x_n, x_nb, z, xh, rstd = _out_fwd(o_parts, h, x_f, w_out, g_ln, b_ln, name=ln + "_out")
            saved.append(dict(h=h, hb=hb, x_b=x_b, qn=qn, kvn=kvn, q=q, k=k, v=v, ccol=ccol, crow=crow,
                              o_mla=o_mla, o_fox=o_fox, lse_mla=lse_mla, lse_fox=lse_fox, o_parts=o_parts, z=z, xh=xh, rstd=rstd,
                              w_in_t=w_in_t, uq_t=uq_t, ukv_t=ukv_t, w_out_t=w_out_t, qg=qg, kg=kg, bft=bft, g_ln=g_ln))
        else:
            w_in_p, w_in_t, w_out, w_out_t = _odd_weights(full["odd_w_in"][j], full["odd_w_out"][j])
            h = _mm(x_b, w_in_p, out_dtype=F32, name=ln + "_in")
            q, k, v = _odd_rope(h, cos_s, sin_s, name=ln + "_rope")
            kp = jnp.pad(k, ((WINDOW, 0), (0, 0)))
            vp = jnp.pad(v, ((WINDOW, 0), (0, 0)))
            o, lse = _swa_fwd(q, kp, vp, sink_tiles[j], name=ln + "_swa")
            g_ln, b_ln = odd_g_full[j][None], odd_b_full[j][None]
            x_n, x_nb, z, xh, rstd = _out_fwd([o], h, x_f, w_out, g_ln, b_ln, name=ln + "_out")
            saved.append(dict(h=h, x_b=x_b, q=q, kp=kp, vp=vp, lse=lse, o=o, o_parts=[o], z=z, xh=xh, rstd=rstd,
                              w_in_t=w_in_t, w_out_t=w_out_t, g_ln=g_ln))
        x_f, x_b = x_n, x_nb

    dxn, loss_tile = _loss_grad(x_f, tgt, name="loss")

    grads = {n: [None, None] for n in names}
    for layer in reversed(range(DEPTH)):
        j = layer // 2
        ln = f"L{layer}"
        sv = saved[layer]
        dr, dy, do, dgate, dg_ln, db_ln = _out_bwd(dxn, sv["xh"], sv["rstd"], sv["g_ln"], sv["w_out_t"], sv["o_parts"], sv["h"], name=ln + "_outb")
        dw_out = _mm(sv["z"].T, dy, out_dtype=F32, name=ln + "_dwout")
        if layer % 2 == 0:
            hb = sv["hb"]
            dq, dk, dv = _attn_bwd((sv["q"], 0), (sv["k"], 0), (sv["v"], 0), (sv["o_mla"], 0), (do, 0), sv["lse_mla"], None, None,
                                   dk=HEAD_PAD, npairs=mla_pairs, scale=mla_scale, dq_dtype=F32, name=ln + "_mlab")
            dfq, dfk, dfv, dcrow, dccol = _attn_bwd((hb, E_FQ), (hb, E_FK), (hb, E_FV), (sv["o_fox"], 0), (do, MLA_HEADS * MLA_V),
                                                    sv["lse_fox"], sv["ccol"], sv["crow"], dk=FOX_DIM, npairs=fox_pairs,
                                                    scale=fox_scale, dq_dtype=BF16, name=ln + "_foxb")
            dcum = jnp.pad((dcrow[:, 0, :] + dccol[:, :, 0]).T, ((0, 0), (FL_LANE, LANES - FL_LANE - FOX_HEADS)))
            dlogf = _cumsum(dcum, reverse=True, name=ln + "_cumb")
            dh_small, dq_pre, dqg, dkg, dbf = _even_mid_bwd(
                sv["h"], dq, dk, dv, dlogf, sv["qg"], sv["kg"], sv["bft"], sv["uq_t"], sv["ukv_t"], cos_m, sin_m, name=ln + "_midb")
            duq_p = _mm(sv["qn"].T, dq_pre, out_dtype=F32, name=ln + "_dwuq")
            dkv_cat = jnp.concatenate([dk.astype(BF16), dv], axis=1)
            dukv_p = _mm(sv["kvn"].T, dkv_cat, out_dtype=F32, name=ln + "_dwukv")
            dh = jnp.concatenate([dgate, dfq, dfk, dfv, dh_small], axis=1)
            dw_in_p = _mm(sv["x_b"].T, dh, out_dtype=F32, name=ln + "_dwin")
            dw_in, duq, dukv = _even_weight_grads(dw_in_p, duq_p, dukv_p)
            for n, val in (("even_w_in", dw_in), ("even_w_uq", duq), ("even_w_ukv", dukv), ("even_w_out", dw_out),
                           ("even_q_norm", dqg[0]), ("even_kv_norm", dkg[0]), ("even_b_f", dbf[0, FL_LANE:FL_LANE + FOX_HEADS]),
                           ("even_ln_g", dg_ln[0]), ("even_ln_b", db_ln[0])):
                grads[n][j] = val
        else:
            dq, dkp, dvp, dsink = _swa_bwd(sv["q"], sv["kp"], sv["vp"], sink_tiles[j], sv["o"], do, sv["lse"], name=ln + "_swab")
            dq_r, dk_r = _odd_rope_bwd(dq, dkp[WINDOW:], cos_s, sin_s, name=ln + "_ropeb")
            dh = jnp.concatenate([dgate, dq_r, dk_r, dvp[WINDOW:].astype(BF16)], axis=1)
            dw_in_p = _mm(sv["x_b"].T, dh, out_dtype=F32, name=ln + "_dwin")
            dw_in, dw_out = _odd_weight_grads(dw_in_p, dw_out)
            for n, val in (("odd_w_in", dw_in), ("odd_w_out", dw_out), ("odd_sinks", _interleave(dsink[None, :, 0, 0], False)[0]),
                           ("odd_ln_g", dg_ln[0]), ("odd_ln_b", db_ln[0])):
                grads[n][j] = val
        dxn = _mm(dh, sv["w_in_t"], out_dtype=F32, name=ln + "_dx", res=dr, res_scale=ALPHA)
    grad_x = dxn[None]
    grads = {n: jnp.stack(v) for n, v in grads.items()}

    gp = [_chip_halves(n, grads[n]) for n in _SHARDED]
    from_sib = _swap_halves(gp, name="grad_swap")
    pair = [_add_sibling(g, r, c_idx, chip_idx, name="grad_add_sibling_" + n) for n, g, r in zip(_SHARDED, gp, from_sib)]
    from_chips = _scatter_to_chips([p[0] for p in pair], name="grad_scatter")
    mine = [_add_chips(p[1], r, c_idx, name="grad_add_chips_" + n) for n, p, r in zip(_SHARDED, pair, from_chips)]
    gshard = dict(zip(_SHARDED, _join_halves(mine, name="grad_join")))

    small = [n for n in names if n not in _SHARDED]
    sv_flat = jnp.concatenate([grads[n].reshape(-1) for n in small] + [loss_tile[0, :1]])
    sv_real = sv_flat.shape[0]
    sv_rows = -(-sv_real // (PACK_COLS * 8)) * 8
    sv_sum = _all_reduce_small(jnp.pad(sv_flat, (0, sv_rows * PACK_COLS - sv_real)).reshape(sv_rows, PACK_COLS), name="small_all_reduce").reshape(-1)
    off = 0
    for n in small:
        size = int(np.prod(grads[n].shape))
        gfull = sv_sum[off:off + size].reshape(grads[n].shape)
        off += size
        if n in ("odd_ln_g", "odd_ln_b"):
            gfull = lax.dynamic_slice_in_dim(gfull, chip * (D_MODEL // N_CHIPS), D_MODEL // N_CHIPS, axis=1)
        gshard[n] = gfull
    loss = sv_sum[off]

    deltas, new_m, new_v = {}, {}, {}
    for n in names:
        deltas[n], new_m[n], new_v[n] = _adamw(weights[n], gshard[n], mom_m[n], mom_v[n], name="adamw_" + n)
    return (loss, grad_x, *[gshard[n] for n in names], *[deltas[n] for n in names],
            *[new_m[n] for n in names], *[new_v[n] for n in names])
```

```python
import jax
import jax.numpy as jnp
import numpy as np
from jax import lax
from jax.experimental import pallas as pl
from jax.experimental.pallas import tpu as pltpu

F32 = jnp.float32
BF16 = jnp.bfloat16
MESH = pl.DeviceIdType.MESH

D_MODEL = 1024
DEPTH = 4
ROPE_THETA = 10000.0
MLA_HEADS, MLA_NOPE, MLA_ROPE, MLA_V = 8, 64, 32, 64
MLA_Q_RANK, MLA_KV_RANK = 256, 128
FOX_HEADS, FOX_DIM = 8, 64
SWA_HEADS, SWA_KV_HEADS, SWA_DIM, WINDOW = 16, 2, 64, 128
RMS_EPS = 1e-6
LN_EPS = 1e-5
ALPHA = (2 * DEPTH) ** 0.25
EVEN_IN = 2984
ODD_IN = 2304
ADAM_LR, ADAM_B1, ADAM_B2, ADAM_EPS, ADAM_WD, ADAM_STEP = 0.001, 0.9, 0.999, 1e-08, 0.01, 10

LANES = 128
HALF = LANES // 2
HEAD_PAD = 128
N_CHIPS = 4
N_DEV = 8
E_GATE, E_FQ, E_FK, E_FV, E_SMALL = 0, 1024, 1536, 2048, 2560
E_PAD_IN = 3072
KPE_LANE = 64
FL_LANE = 96
O_GATE, O_Q, O_K, O_V = 0, 1024, 2048, 2176

_ARB = "arbitrary"
_PAR = "parallel"


def _cparams(sem):
    return pltpu.CompilerParams(dimension_semantics=sem)


def _pick(n, cands):
    for c in cands:
        if n % c == 0:
            return c
    return n


def _mm(a, b, *, out_dtype, name, res=None, res_scale=1.0, also_bf16=False):
    m, k = a.shape
    _, n = b.shape
    tm = _pick(m, (512, 256, 128))
    tn = _pick(n, (1024, 768, 512, 384, 256, 128))
    tk = _pick(k, (1024, 768, 512, 256, 128))
    nk = k // tk

    def body(*refs):
        refs = list(refs)
        a_ref, b_ref = refs[:2]
        r_ref = refs[2] if res is not None else None
        acc_ref = refs[-1]
        outs = refs[3 if res is not None else 2:-1]
        kk = pl.program_id(2)

        @pl.when(kk == 0)
        def _():
            acc_ref[...] = jnp.zeros_like(acc_ref)

        acc_ref[...] += jnp.dot(a_ref[...].astype(BF16), b_ref[...].astype(BF16), preferred_element_type=F32)

        @pl.when(kk == nk - 1)
        def _():
            r = acc_ref[...]
            if res is not None:
                r = r + res_scale * r_ref[...]
            for o_ref in outs:
                o_ref[...] = r.astype(o_ref.dtype)

    in_specs = [pl.BlockSpec((tm, tk), lambda i, j, kk: (i, kk)), pl.BlockSpec((tk, tn), lambda i, j, kk: (kk, j))]
    args = [a, b]
    if res is not None:
        in_specs.append(pl.BlockSpec((tm, tn), lambda i, j, kk: (i, j)))
        args.append(res)
    ospec = pl.BlockSpec((tm, tn), lambda i, j, kk: (i, j))
    dtypes = [out_dtype, BF16] if also_bf16 else [out_dtype]
    out = pl.pallas_call(
        body, name=name, grid=(m // tm, n // tn, nk), in_specs=in_specs,
        out_specs=[ospec] * len(dtypes), out_shape=[jax.ShapeDtypeStruct((m, n), d) for d in dtypes],
        scratch_shapes=[pltpu.VMEM((tm, tn), F32)],
        compiler_params=_cparams((_PAR, _PAR, _ARB)),
    )(*args)
    return out if also_bf16 else out[0]


def _rope_tile(t, cos, sin, half):
    w = t.shape[-1]
    lane = lax.broadcasted_iota(jnp.int32, t.shape, 1)
    first = (lane % (2 * half)) < half
    sw = jnp.where(first, pltpu.roll(t, w - half, 1), pltpu.roll(t, half, 1))
    return t * cos + sw * sin


def _sigmoid(x):
    return 1.0 / (1.0 + jnp.exp(-x))


def _lane_mask(shape, lo, hi):
    lane = lax.broadcasted_iota(jnp.int32, shape, 1)
    return (lane >= lo) & (lane < hi)


def _rms(x, g):
    r = lax.rsqrt(jnp.mean(x * x, axis=-1, keepdims=True) + RMS_EPS)
    u = x * r
    return u, r, u * g


def _rms_bwd(dy, u, r, g):
    dyg = dy * g
    dx = r * (dyg - u * jnp.mean(dyg * u, axis=-1, keepdims=True))
    return dx, jnp.sum(dy * u, axis=0, keepdims=True)


def _even_mid_fwd(h, qg, kg, bf_tile, w_uq, w_ukv, cos, sin, *, name):
    s = h.shape[0]
    tb = _pick(s, (512, 256, 128))
    nq = MLA_HEADS * HEAD_PAD

    def body(h_ref, qg_ref, kg_ref, bf_ref, wuq_ref, wukv_ref, cos_ref, sin_ref,
             q_ref, k_ref, v_ref, qn_ref, kvn_ref, lf_ref):
        hb = h_ref[...]
        cq, ckv, misc = hb[:, :MLA_Q_RANK], hb[:, MLA_Q_RANK:MLA_Q_RANK + MLA_KV_RANK], hb[:, MLA_Q_RANK + MLA_KV_RANK:]
        cs, sn = cos_ref[...], sin_ref[...]
        _, _, qn = _rms(cq, qg_ref[...])
        qn = qn.astype(BF16)
        qn_ref[...] = qn
        q = jnp.dot(qn, wuq_ref[...], preferred_element_type=F32)
        _, _, kvn = _rms(ckv, kg_ref[...])
        kvn = kvn.astype(BF16)
        kvn_ref[...] = kvn
        kv = jnp.dot(kvn, wukv_ref[...], preferred_element_type=F32)
        kpe = jnp.where(_lane_mask(misc.shape, KPE_LANE, KPE_LANE + MLA_ROPE), _rope_tile(misc, cs, sn, MLA_ROPE // 2), 0.0)
        for hd in range(MLA_HEADS):
            sl = slice(hd * HEAD_PAD, (hd + 1) * HEAD_PAD)
            q_ref[:, sl] = _rope_tile(q[:, sl], cs, sn, MLA_ROPE // 2).astype(BF16)
            k_ref[:, sl] = (kv[:, sl] + kpe).astype(BF16)
        v_ref[...] = kv[:, nq:].astype(BF16)
        xf = misc + bf_ref[...]
        logf = jnp.minimum(xf, 0.0) - jnp.log(1.0 + jnp.exp(-jnp.abs(xf)))
        lf_ref[...] = jnp.where(_lane_mask(misc.shape, FL_LANE, FL_LANE + FOX_HEADS), logf, 0.0)

    full = lambda a: pl.BlockSpec(a.shape, lambda i: (0,) * a.ndim)
    rows = lambda w, c=0: pl.BlockSpec((tb, w), lambda i: (i, c))
    return pl.pallas_call(
        body, name=name, grid=(s // tb,),
        in_specs=[rows(512, E_SMALL // 512), full(qg), full(kg), full(bf_tile), full(w_uq), full(w_ukv), rows(LANES), rows(LANES)],
        out_specs=[rows(nq), rows(nq), rows(MLA_HEADS * MLA_V), rows(MLA_Q_RANK), rows(MLA_KV_RANK), rows(LANES)],
        out_shape=[jax.ShapeDtypeStruct((s, nq), BF16), jax.ShapeDtypeStruct((s, nq), BF16),
                   jax.ShapeDtypeStruct((s, MLA_HEADS * MLA_V), BF16), jax.ShapeDtypeStruct((s, MLA_Q_RANK), BF16),
                   jax.ShapeDtypeStruct((s, MLA_KV_RANK), BF16), jax.ShapeDtypeStruct((s, LANES), F32)],
        compiler_params=_cparams((_PAR,)),
    )(h, qg, kg, bf_tile, w_uq, w_ukv, cos, sin)


def _even_mid_bwd(h, dq, dk, dv, dlogf, qg, kg, bf_tile, w_uq_t, w_ukv_t, cos, sin, *, name):
    s = h.shape[0]
    tb = _pick(s, (256, 128))
    nq = MLA_HEADS * HEAD_PAD

    def body(h_ref, dq_ref, dk_ref, dv_ref, dlf_ref, qg_ref, kg_ref, bf_ref, wuqt_ref, wukvt_ref, cos_ref, sin_ref,
             dh_ref, dqp_ref, dqg_ref, dkg_ref, dbf_ref):
        @pl.when(pl.program_id(0) == 0)
        def _():
            dqg_ref[...] = jnp.zeros_like(dqg_ref)
            dkg_ref[...] = jnp.zeros_like(dkg_ref)
            dbf_ref[...] = jnp.zeros_like(dbf_ref)

        hb = h_ref[...]
        cq, ckv, misc = hb[:, :MLA_Q_RANK], hb[:, MLA_Q_RANK:MLA_Q_RANK + MLA_KV_RANK], hb[:, MLA_Q_RANK + MLA_KV_RANK:]
        cs, sn = cos_ref[...], -sin_ref[...]
        dkpe = jnp.zeros(misc.shape, F32)
        for hd in range(MLA_HEADS):
            sl = slice(hd * HEAD_PAD, (hd + 1) * HEAD_PAD)
            dqp_ref[:, sl] = _rope_tile(dq_ref[:, sl], cs, sn, MLA_ROPE // 2).astype(BF16)
            dkpe = dkpe + dk_ref[:, sl]
        dqn = jnp.dot(dqp_ref[...], wuqt_ref[...], preferred_element_type=F32)
        uq, rq, _ = _rms(cq, qg_ref[...])
        dcq, dqg = _rms_bwd(dqn, uq, rq, qg_ref[...])
        dqg_ref[...] += dqg
        dkv = jnp.concatenate([dk_ref[...].astype(BF16), dv_ref[...]], axis=1)
        dkvn = jnp.dot(dkv, wukvt_ref[...], preferred_element_type=F32)
        uk, rk, _ = _rms(ckv, kg_ref[...])
        dckv, dkg = _rms_bwd(dkvn, uk, rk, kg_ref[...])
        dkg_ref[...] += dkg
        dmisc = jnp.where(_lane_mask(misc.shape, KPE_LANE, KPE_LANE + MLA_ROPE), _rope_tile(dkpe, cs, sn, MLA_ROPE // 2), 0.0)
        dfl = jnp.where(_lane_mask(misc.shape, FL_LANE, FL_LANE + FOX_HEADS), dlf_ref[...] * _sigmoid(-(misc + bf_ref[...])), 0.0)
        dbf_ref[...] += jnp.sum(dfl, axis=0, keepdims=True)
        dh_ref[:, :MLA_Q_RANK] = dcq.astype(BF16)
        dh_ref[:, MLA_Q_RANK:MLA_Q_RANK + MLA_KV_RANK] = dckv.astype(BF16)
        dh_ref[:, MLA_Q_RANK + MLA_KV_RANK:] = (dmisc + dfl).astype(BF16)

    full = lambda a: pl.BlockSpec(a.shape, lambda i: (0,) * a.ndim)
    rows = lambda w, c=0: pl.BlockSpec((tb, w), lambda i: (i, c))
    return pl.pallas_call(
        body, name=name, grid=(s // tb,),
        in_specs=[rows(512, E_SMALL // 512), rows(nq), rows(nq), rows(MLA_HEADS * MLA_V), rows(LANES), full(qg), full(kg),
                  full(bf_tile), full(w_uq_t), full(w_ukv_t), rows(LANES), rows(LANES)],
        out_specs=[rows(512), rows(nq), full(qg), full(kg), full(bf_tile)],
        out_shape=[jax.ShapeDtypeStruct((s, 512), BF16), jax.ShapeDtypeStruct((s, nq), BF16),
                   jax.ShapeDtypeStruct(qg.shape, F32), jax.ShapeDtypeStruct(kg.shape, F32),
                   jax.ShapeDtypeStruct(bf_tile.shape, F32)],
        compiler_params=_cparams((_ARB,)),
    )(h, dq, dk, dv, dlogf, qg, kg, bf_tile, w_uq_t, w_ukv_t, cos, sin)


def _cumsum(x, *, reverse, name):
    s = x.shape[0]
    tb = _pick(s, (512, 256, 128))
    nb = s // tb

    def body(x_ref, o_ref, carry_ref):
        @pl.when(pl.program_id(0) == 0)
        def _():
            carry_ref[...] = jnp.zeros_like(carry_ref)

        xv = x_ref[...]
        r = lax.broadcasted_iota(jnp.int32, (tb, tb), 0)
        c = lax.broadcasted_iota(jnp.int32, (tb, tb), 1)
        tri = jnp.where((c >= r) if reverse else (c <= r), 1.0, 0.0).astype(BF16)
        hi = xv.astype(BF16)
        r1 = xv - hi.astype(F32)
        mid = r1.astype(BF16)
        lo = (r1 - mid.astype(F32)).astype(BF16)
        cs = (jnp.dot(tri, hi, preferred_element_type=F32) + jnp.dot(tri, mid, preferred_element_type=F32)
              + jnp.dot(tri, lo, preferred_element_type=F32)) + carry_ref[...]
        o_ref[...] = cs
        carry_ref[...] = cs[0:1, :] if reverse else cs[tb - 1:tb, :]

    imap = (lambda i: (nb - 1 - i, 0)) if reverse else (lambda i: (i, 0))
    return pl.pallas_call(
        body, name=name, grid=(nb,), in_specs=[pl.BlockSpec((tb, LANES), imap)],
        out_specs=pl.BlockSpec((tb, LANES), imap), out_shape=jax.ShapeDtypeStruct(x.shape, F32),
        scratch_shapes=[pltpu.VMEM((1, LANES), F32)], compiler_params=_cparams((_ARB,)),
    )(x)


_NT = (((1,), (1,)), ((), ()))
_TN = (((0,), (0,)), ((), ()))


def _head_sel(x, h, dk):
    if dk == LANES:
        return x[:, h * LANES:(h + 1) * LANES]
    return jnp.where(_lane_mask(x.shape, h * HALF, (h + 1) * HALF), x, jnp.zeros_like(x))


def _head_other(x, h, dk):
    return x[:, h * LANES:(h + 1) * LANES] if dk == LANES else x


def _pair(parts, dk=HALF):
    if dk == LANES:
        return jnp.concatenate(parts, axis=1)
    return jnp.where(_lane_mask(parts[0].shape, 0, HALF), parts[0], parts[1])


def _attn_fwd(q, k, v, ccol, crow, *, dk, npairs, scale, name):
    (qa, qo), (ka, ko), (va, vo) = q, k, v
    s = qa.shape[0]
    wq = 2 * dk
    t = _pick(s, (512, 256, 128))
    nb = s // t
    bias = ccol is not None

    def body(*refs):
        if bias:
            q_ref, k_ref, v_ref, cc_ref, cr_ref, o_ref, lse_ref = refs
        else:
            q_ref, k_ref, v_ref, o_ref, lse_ref = refs
        qi = pl.program_id(1)
        qb = q_ref[...]
        qs = [_head_sel(qb, h, dk) for h in range(2)]

        def scores(j):
            rows = pl.ds(pl.multiple_of(j * t, t), t)
            kb = k_ref[rows, :]
            out = []
            for h in range(2):
                sc = lax.dot_general(qs[h], _head_other(kb, h, dk), _NT, preferred_element_type=F32) * scale
                out.append(sc - cr_ref[h, j] if bias else sc)
            return tuple(out)

        def update(scs, j, state):
            rows = pl.ds(pl.multiple_of(j * t, t), t)
            vb = v_ref[rows, :]
            soft = []
            for h in range(2):
                m, l, _ = state[3 * h:3 * h + 3]
                sc = scs[h]
                m_new = jnp.maximum(m, jnp.max(sc, axis=1, keepdims=True))
                a = jnp.exp(m - m_new)
                p = jnp.exp(sc - m_new)
                soft.append((m_new, a * l + jnp.sum(p, axis=1, keepdims=True), a, p.astype(BF16)))
            new = []
            for h in range(2):
                m_new, l, a, p = soft[h]
                new += [m_new, l, a * state[3 * h + 2] + jnp.dot(p, vb, preferred_element_type=F32)]
            return tuple(new)

        def step(j, carry):
            nxt = scores(j + 1)
            return nxt + update(carry[:2], j, carry[2:])

        one = (jnp.full((t, 1), -jnp.inf, F32), jnp.zeros((t, 1), F32), jnp.zeros((t, LANES), F32))
        carry = lax.fori_loop(0, qi, step, scores(0) + one + one)
        row = lax.broadcasted_iota(jnp.int32, (t, t), 0)
        col = lax.broadcasted_iota(jnp.int32, (t, t), 1)
        diag = tuple(jnp.where(col <= row, sc, -jnp.inf) for sc in carry[:2])
        state = update(diag, qi, carry[2:])
        outs = []
        for h in range(2):
            m, l, acc = state[3 * h:3 * h + 3]
            outs.append(acc / l)
            lse = m + jnp.log(l)
            lse_ref[h] = lse + cc_ref[h] if bias else lse
        o_ref[...] = _pair(outs)

    in_specs = [pl.BlockSpec((t, wq), lambda p, i: (i, qo // wq + p)), pl.BlockSpec((s, wq), lambda p, i: (0, ko // wq + p)),
                pl.BlockSpec((s, LANES), lambda p, i: (0, vo // LANES + p))]
    args = [qa, ka, va]
    if bias:
        in_specs += [pl.BlockSpec((2, t, 1), lambda p, i: (p, i, 0)), pl.BlockSpec((2, nb, 1, t), lambda p, i: (p, 0, 0, 0))]
        args += [ccol, crow.reshape(2 * npairs, nb, 1, t)]
    return pl.pallas_call(
        body, name=name, grid=(npairs, nb), in_specs=in_specs,
        out_specs=[pl.BlockSpec((t, LANES), lambda p, i: (i, p)), pl.BlockSpec((2, t, 1), lambda p, i: (p, i, 0))],
        out_shape=[jax.ShapeDtypeStruct((s, npairs * LANES), F32), jax.ShapeDtypeStruct((2 * npairs, s, 1), F32)],
        compiler_params=_cparams((_PAR, _PAR)),
    )(*args)


def _attn_bwd(q, k, v, o, do, lse, ccol, crow, *, dk, npairs, scale, dq_dtype, name):
    (qa, qo), (ka, ko), (va, vo), (oa, oo), (da, do_o) = q, k, v, o, do
    s = qa.shape[0]
    wq = 2 * dk
    t = _pick(s, (512, 256, 128))
    nb = s // t
    bias = ccol is not None

    def body(*refs):
        if bias:
            (q_ref, k_ref, v_ref, o_ref, do_ref, lse_ref, cc_ref, cr_ref,
             dq_ref, dk_ref, dv_ref, dc_ref, dcc_ref, dq_s, dk_s, dv_s, dc_s) = refs
        else:
            q_ref, k_ref, v_ref, o_ref, do_ref, lse_ref, dq_ref, dk_ref, dv_ref, dq_s, dk_s, dv_s = refs
        ki, qi = pl.program_id(1), pl.program_id(2)

        @pl.when((ki == 0) & (qi == 0))
        def _():
            dq_s[...] = jnp.zeros_like(dq_s)
            if bias:
                dcc_ref[...] = jnp.zeros_like(dcc_ref)

        @pl.when(qi == ki)
        def _():
            dk_s[...] = jnp.zeros_like(dk_s)
            dv_s[...] = jnp.zeros_like(dv_s)
            if bias:
                dc_s[...] = jnp.zeros_like(dc_s)

        @pl.when(qi >= ki)
        def _():
            qb, kb, vb, dob, ob = q_ref[...], k_ref[...], v_ref[...], do_ref[...], o_ref[...]
            rows = pl.ds(pl.multiple_of(qi * t, t), t)
            row = lax.broadcasted_iota(jnp.int32, (t, t), 0) + qi * t
            col = lax.broadcasted_iota(jnp.int32, (t, t), 1) + ki * t
            fed = []
            for h in range(2):
                kh, doh = _head_other(kb, h, dk), _head_sel(dob, h, HALF)
                sc = lax.dot_general(_head_sel(qb, h, dk), kh, _NT, preferred_element_type=F32)
                dp = lax.dot_general(doh, vb, _NT, preferred_element_type=F32)
                fed.append((kh, doh, sc, dp))
            mid = []
            for h in range(2):
                kh, doh, sc, dp = fed[h]
                sc = sc * scale
                if bias:
                    sc = sc + cc_ref[h] - cr_ref[h]
                p = jnp.exp(jnp.where(col <= row, sc, -jnp.inf) - lse_ref[h])
                delta = jnp.sum(doh.astype(F32) * ob, axis=1, keepdims=True)
                ds = p * (dp - delta)
                if bias:
                    dc_s[h] -= jnp.sum(ds, axis=0, keepdims=True)
                    dcc_ref[h, rows, :] += jnp.sum(ds, axis=1, keepdims=True)
                mid.append((kh, p.astype(BF16), (ds * scale).astype(BF16)))
            dq_parts, dk_parts, dv_parts = [], [], []
            for h in range(2):
                kh, pb, dsb = mid[h]
                dv_parts.append(lax.dot_general(pb, dob, _TN, preferred_element_type=F32))
                dk_parts.append(lax.dot_general(dsb, _head_other(qb, h, dk), _TN, preferred_element_type=F32))
                dq_parts.append(jnp.dot(dsb, kh, preferred_element_type=F32))
            dv_s[...] += _pair(dv_parts)
            dk_s[...] += _pair(dk_parts, dk)
            dq_s[rows, :] += _pair(dq_parts, dk)

        @pl.when(qi == nb - 1)
        def _():
            dk_ref[...] = dk_s[...].astype(dk_ref.dtype)
            dv_ref[...] = dv_s[...].astype(dv_ref.dtype)
            if bias:
                dc_ref[...] = dc_s[...]

        @pl.when((ki == nb - 1) & (qi == nb - 1))
        def _():
            dq_ref[...] = dq_s[...].astype(dq_ref.dtype)

    qrow = lambda p, j, i: jnp.maximum(i, j)
    in_specs = [pl.BlockSpec((t, wq), lambda p, j, i: (qrow(p, j, i), qo // wq + p)),
                pl.BlockSpec((t, wq), lambda p, j, i: (j, ko // wq + p)),
                pl.BlockSpec((t, LANES), lambda p, j, i: (j, vo // LANES + p)),
                pl.BlockSpec((t, LANES), lambda p, j, i: (qrow(p, j, i), oo // LANES + p)),
                pl.BlockSpec((t, LANES), lambda p, j, i: (qrow(p, j, i), do_o // LANES + p)),
                pl.BlockSpec((2, t, 1), lambda p, j, i: (p, qrow(p, j, i), 0))]
    args = [qa, ka, va, oa, da, lse]
    out_specs = [pl.BlockSpec((s, wq), lambda p, j, i: (0, p)), pl.BlockSpec((t, wq), lambda p, j, i: (j, p)),
                 pl.BlockSpec((t, LANES), lambda p, j, i: (j, p))]
    out_shape = [jax.ShapeDtypeStruct((s, npairs * wq), dq_dtype), jax.ShapeDtypeStruct((s, npairs * wq), dq_dtype),
                 jax.ShapeDtypeStruct((s, npairs * LANES), BF16)]
    scratch = [pltpu.VMEM((s, wq), F32), pltpu.VMEM((t, wq), F32), pltpu.VMEM((t, LANES), F32)]
    if bias:
        in_specs += [pl.BlockSpec((2, t, 1), lambda p, j, i: (p, qrow(p, j, i), 0)), pl.BlockSpec((2, 1, t), lambda p, j, i: (p, 0, j))]
        args += [ccol, crow]
        out_specs += [pl.BlockSpec((2, 1, t), lambda p, j, i: (p, 0, j)), pl.BlockSpec((2, s, 1), lambda p, j, i: (p, 0, 0))]
        out_shape += [jax.ShapeDtypeStruct((2 * npairs, 1, s), F32), jax.ShapeDtypeStruct((2 * npairs, s, 1), F32)]
        scratch.append(pltpu.VMEM((2, 1, t), F32))
    return pl.pallas_call(
        body, name=name, grid=(npairs, nb, nb), in_specs=in_specs, out_specs=out_specs, out_shape=out_shape,
        scratch_shapes=scratch, compiler_params=_cparams((_PAR, _ARB, _ARB)),
    )(*args)


def _sub_mask(shape, lo, hi):
    sub = lax.broadcasted_iota(jnp.int32, shape, 0)
    return (sub >= lo) & (sub < hi)


def _pair_rows(parts):
    return jnp.where(_sub_mask(parts[0].shape, 0, HALF), parts[0], parts[1])


def _swa_valid(start, t):
    krow = lax.broadcasted_iota(jnp.int32, (t + WINDOW, t), 0)
    qcol = lax.broadcasted_iota(jnp.int32, (t + WINDOW, t), 1)
    diff = qcol - krow + WINDOW
    return (diff >= 0) & (diff < WINDOW) & (krow + start >= WINDOW)


def _swa_tiles(s):
    t = _pick(s, (256, 128))
    return t, _pick(s // t, (4, 2, 1))


def _swa_fwd(q, kp, vp, sink, *, name):
    s = q.shape[0]
    npairs = SWA_HEADS // 2
    t, nt = _swa_tiles(s)
    scale = SWA_DIM ** -0.5

    def body(q_ref, k_ref, v_ref, sk_ref, o_ref, lse_ref):
        sts, vws = [], []
        for u in range(nt):
            start = pl.multiple_of((pl.program_id(1) * nt + u) * t, t)
            kw = k_ref[pl.ds(start, t + WINDOW), :]
            vws.append(v_ref[pl.ds(start, t + WINDOW), :])
            qb = q_ref[u * t:(u + 1) * t, :]
            valid = _swa_valid(start, t)
            for h in range(2):
                st = lax.dot_general(kw, _head_sel(qb, h, HALF), _NT, preferred_element_type=F32) * scale
                sts.append(jnp.where(valid, st, -jnp.inf))
        pts = []
        for u in range(nt):
            for h in range(2):
                st = sts[2 * u + h]
                snk = sk_ref[h][:, 0:1]
                m = jnp.maximum(jnp.max(st, axis=0, keepdims=True), snk)
                e = jnp.exp(st - m)
                l = jnp.sum(e, axis=0, keepdims=True) + jnp.exp(snk - m)
                pts.append((e * (1.0 / l)).astype(BF16))
                lse_ref[h, :, u * t:(u + 1) * t] = m + jnp.log(l)
        for u in range(nt):
            outs = [lax.dot_general(vws[u], pts[2 * u + h], _TN, preferred_element_type=F32) for h in range(2)]
            o_ref[:, u * t:(u + 1) * t] = _pair_rows(outs)

    kvspec = pl.BlockSpec((s + WINDOW, LANES), lambda p, i: (0, 0))
    return pl.pallas_call(
        body, name=name, grid=(npairs, s // (t * nt)),
        in_specs=[pl.BlockSpec((t * nt, LANES), lambda p, i: (i, p)), kvspec, kvspec, pl.BlockSpec((2, 1, LANES), lambda p, i: (p, 0, 0))],
        out_specs=[pl.BlockSpec((LANES, t * nt), lambda p, i: (p, i)), pl.BlockSpec((2, 1, t * nt), lambda p, i: (p, 0, i))],
        out_shape=[jax.ShapeDtypeStruct((npairs * LANES, s), F32), jax.ShapeDtypeStruct((SWA_HEADS, 1, s), F32)],
        compiler_params=_cparams((_PAR, _PAR)),
    )(q, kp, vp, sink)


def _swa_bwd(q, kp, vp, kpt, sink, ot, do, dot, lse, *, name):
    s = q.shape[0]
    npairs = SWA_HEADS // 2
    t, nt = _swa_tiles(s)
    scale = SWA_DIM ** -0.5

    def body(q_ref, k_ref, v_ref, kt_ref, sk_ref, ot_ref, do_ref, dot_ref, lse_ref, dq_ref, dk_ref, dv_ref, dsk_ref):
        pp, i = pl.program_id(0), pl.program_id(1)

        @pl.when((pp == 0) & (i == 0))
        def _():
            dk_ref[...] = jnp.zeros_like(dk_ref)
            dv_ref[...] = jnp.zeros_like(dv_ref)

        @pl.when(i == 0)
        def _():
            dsk_ref[...] = jnp.zeros_like(dsk_ref)

        fed = []
        for u in range(nt):
            cols = slice(u * t, (u + 1) * t)
            start = pl.multiple_of((i * nt + u) * t, t)
            win = pl.ds(start, t + WINDOW)
            qb, dob = q_ref[cols, :], do_ref[cols, :]
            kw, vw = k_ref[win, :], v_ref[win, :]
            valid = _swa_valid(start, t)
            for h in range(2):
                st = lax.dot_general(kw, _head_sel(qb, h, HALF), _NT, preferred_element_type=F32)
                dpt = lax.dot_general(vw, _head_sel(dob, h, HALF), _NT, preferred_element_type=F32)
                fed.append((cols, win, qb, dob, valid, st, dpt))
        mid = []
        for u in range(nt):
            cols = fed[2 * u][0]
            prod = dot_ref[:, cols].astype(F32) * ot_ref[:, cols]
            for h in range(2):
                valid, st, dpt = fed[2 * u + h][4:]
                lse_b = lse_ref[h, :, cols]
                pt = jnp.exp(jnp.where(valid, st * scale, -jnp.inf) - lse_b)
                delta = jnp.sum(jnp.where(_sub_mask(prod.shape, h * HALF, (h + 1) * HALF), prod, 0.0), axis=0, keepdims=True)
                dst = pt * (dpt - delta)
                psink = jnp.exp(sk_ref[h][:, 0:1] - lse_b)
                dsk_ref[h] += jnp.broadcast_to(-jnp.sum(psink * delta, axis=1, keepdims=True), (1, LANES))
                mid.append((pt.astype(BF16), (dst * scale).astype(BF16)))
        for u in range(nt):
            cols, win, qb, dob = fed[2 * u][:4]
            ktw = kt_ref[:, win]
            dq_parts = [jnp.dot(ktw, mid[2 * u + h][1], preferred_element_type=F32) for h in range(2)]
            dk_parts = [jnp.dot(mid[2 * u + h][1], qb, preferred_element_type=F32) for h in range(2)]
            dv_parts = [jnp.dot(mid[2 * u + h][0], dob, preferred_element_type=F32) for h in range(2)]
            dq_ref[:, cols] = _pair_rows(dq_parts)
            dk_ref[win, :] += _pair(dk_parts)
            dv_ref[win, :] += _pair(dv_parts)

    tile = pl.BlockSpec((t * nt, LANES), lambda p, i: (i, p))
    ttile = pl.BlockSpec((LANES, t * nt), lambda p, i: (p, i))
    kvspec = pl.BlockSpec((s + WINDOW, LANES), lambda p, i: (0, 0))
    ktspec = pl.BlockSpec((LANES, s + WINDOW), lambda p, i: (0, 0))
    skspec = pl.BlockSpec((2, 1, LANES), lambda p, i: (p, 0, 0))
    return pl.pallas_call(
        body, name=name, grid=(npairs, s // (t * nt)),
        in_specs=[tile, kvspec, kvspec, ktspec, skspec, ttile, tile, ttile, pl.BlockSpec((2, 1, t * nt), lambda p, i: (p, 0, i))],
        out_specs=[ttile, kvspec, kvspec, skspec],
        out_shape=[jax.ShapeDtypeStruct((npairs * LANES, s), F32), jax.ShapeDtypeStruct((s + WINDOW, LANES), F32),
                   jax.ShapeDtypeStruct((s + WINDOW, LANES), F32), jax.ShapeDtypeStruct((SWA_HEADS, 1, LANES), F32)],
        compiler_params=_cparams((_ARB, _ARB)),
    )(q, kp, vp, kpt, sink, ot, do, dot, lse)


def _odd_rope(h, cos, sin, *, name):
    s = h.shape[0]
    tb = _pick(s, (512, 256, 128))
    nq = SWA_HEADS * SWA_DIM

    def body(q_ref, kv_ref, cos_ref, sin_ref, qo_ref, ko_ref, vo_ref):
        cs, sn = cos_ref[...], sin_ref[...]
        for j in range(nq // LANES):
            sl = slice(j * LANES, (j + 1) * LANES)
            qo_ref[:, sl] = _rope_tile(q_ref[:, sl], cs, sn, SWA_DIM // 2).astype(BF16)
        ko_ref[...] = _rope_tile(kv_ref[:, :LANES], cs, sn, SWA_DIM // 2).astype(BF16)
        vo_ref[...] = kv_ref[:, LANES:].astype(BF16)

    rows = lambda w, c=0: pl.BlockSpec((tb, w), lambda i: (i, c))
    return pl.pallas_call(
        body, name=name, grid=(s // tb,),
        in_specs=[rows(nq, O_Q // nq), rows(2 * LANES, O_K // (2 * LANES)), rows(LANES), rows(LANES)],
        out_specs=[rows(nq), rows(LANES), rows(LANES)],
        out_shape=[jax.ShapeDtypeStruct((s, nq), BF16), jax.ShapeDtypeStruct((s, LANES), BF16), jax.ShapeDtypeStruct((s, LANES), BF16)],
        compiler_params=_cparams((_PAR,)),
    )(h, h, cos, sin)


def _odd_rope_bwd(dq, dk, cos, sin, *, name):
    s = dq.shape[0]
    tb = _pick(s, (512, 256, 128))
    nq = SWA_HEADS * SWA_DIM

    def body(dq_ref, dk_ref, cos_ref, sin_ref, qo_ref, ko_ref):
        cs, sn = cos_ref[...], -sin_ref[...]
        for j in range(nq // LANES):
            sl = slice(j * LANES, (j + 1) * LANES)
            qo_ref[:, sl] = _rope_tile(dq_ref[:, sl], cs, sn, SWA_DIM // 2).astype(BF16)
        ko_ref[...] = _rope_tile(dk_ref[...], cs, sn, SWA_DIM // 2).astype(BF16)

    rows = lambda w: pl.BlockSpec((tb, w), lambda i: (i, 0))
    return pl.pallas_call(
        body, name=name, grid=(s // tb,), in_specs=[rows(nq), rows(LANES), rows(LANES), rows(LANES)],
        out_specs=[rows(nq), rows(LANES)],
        out_shape=[jax.ShapeDtypeStruct((s, nq), BF16), jax.ShapeDtypeStruct((s, LANES), BF16)],
        compiler_params=_cparams((_PAR,)),
    )(dq, dk, cos, sin)


def _out_fwd(o_parts, h, x, w_out, g, b, *, name):
    s = x.shape[0]
    tb = _pick(s, (256, 128))
    n_o = len(o_parts)

    def body(*refs):
        o_refs = refs[:n_o]
        gate_ref, x_ref, w_ref, g_ref, b_ref, xn_ref, xb_ref, z_ref, xh_ref, rs_ref = refs[n_o:]
        gate = gate_ref[...]
        o = o_refs[0][...] if n_o == 1 else jnp.concatenate([r[...] for r in o_refs], axis=1)
        z = (o * (gate * _sigmoid(gate))).astype(BF16)
        z_ref[...] = z
        r = ALPHA * x_ref[...] + jnp.dot(z, w_ref[...], preferred_element_type=F32)
        mu = jnp.mean(r, axis=-1, keepdims=True)
        rc = r - mu
        rstd = lax.rsqrt(jnp.mean(rc * rc, axis=-1, keepdims=True) + LN_EPS)
        xh = rc * rstd
        xn = xh * g_ref[...] + b_ref[...]
        xh_ref[...] = xh
        rs_ref[...] = rstd
        xn_ref[...] = xn
        xb_ref[...] = xn.astype(BF16)

    rows = lambda w: pl.BlockSpec((tb, w), lambda i: (i, 0))
    full = lambda a: pl.BlockSpec(a.shape, lambda i: (0,) * a.ndim)
    return pl.pallas_call(
        body, name=name, grid=(s // tb,),
        in_specs=[rows(a.shape[1]) for a in o_parts] + [rows(D_MODEL), rows(D_MODEL), full(w_out), full(g), full(b)],
        out_specs=[rows(D_MODEL), rows(D_MODEL), rows(D_MODEL), rows(D_MODEL), rows(1)],
        out_shape=[jax.ShapeDtypeStruct((s, D_MODEL), F32), jax.ShapeDtypeStruct((s, D_MODEL), BF16),
                   jax.ShapeDtypeStruct((s, D_MODEL), BF16), jax.ShapeDtypeStruct((s, D_MODEL), F32),
                   jax.ShapeDtypeStruct((s, 1), F32)],
        compiler_params=_cparams((_PAR,)),
    )(*o_parts, h, x, w_out, g, b)


def _out_bwd(dxn, xh, rstd, g, w_out_t, o_parts, h, *, name):
    s = dxn.shape[0]
    tb = _pick(s, (256, 128))
    n_o = len(o_parts)

    def body(*refs):
        dxn_ref, xh_ref, rs_ref, g_ref, wt_ref = refs[:5]
        o_refs = refs[5:5 + n_o]
        gate_ref, dr_ref, dy_ref, do_ref, dgate_ref, dg_ref, db_ref = refs[5 + n_o:]

        @pl.when(pl.program_id(0) == 0)
        def _():
            dg_ref[...] = jnp.zeros_like(dg_ref)
            db_ref[...] = jnp.zeros_like(db_ref)

        dxn_b, xh_b = dxn_ref[...], xh_ref[...]
        dg_ref[...] += jnp.sum(dxn_b * xh_b, axis=0, keepdims=True)
        db_ref[...] += jnp.sum(dxn_b, axis=0, keepdims=True)
        dxh = dxn_b * g_ref[...]
        dr = rs_ref[...] * (dxh - jnp.mean(dxh, axis=-1, keepdims=True) - xh_b * jnp.mean(dxh * xh_b, axis=-1, keepdims=True))
        dr_ref[...] = dr
        dy = dr.astype(BF16)
        dy_ref[...] = dy
        dz = jnp.dot(dy, wt_ref[...], preferred_element_type=F32)
        gate = gate_ref[...]
        sg = _sigmoid(gate)
        o = o_refs[0][...] if n_o == 1 else jnp.concatenate([r[...] for r in o_refs], axis=1)
        do_ref[...] = (dz * (gate * sg)).astype(BF16)
        dgate_ref[...] = (dz * o * (sg * (1.0 + gate * (1.0 - sg)))).astype(BF16)

    rows = lambda w: pl.BlockSpec((tb, w), lambda i: (i, 0))
    full = lambda a: pl.BlockSpec(a.shape, lambda i: (0,) * a.ndim)
    return pl.pallas_call(
        body, name=name, grid=(s // tb,),
        in_specs=[rows(D_MODEL), rows(D_MODEL), rows(1), full(g), full(w_out_t)] + [rows(a.shape[1]) for a in o_parts] + [rows(D_MODEL)],
        out_specs=[rows(D_MODEL), rows(D_MODEL), rows(D_MODEL), rows(D_MODEL), full(g), full(g)],
        out_shape=[jax.ShapeDtypeStruct((s, D_MODEL), F32), jax.ShapeDtypeStruct((s, D_MODEL), BF16),
                   jax.ShapeDtypeStruct((s, D_MODEL), BF16), jax.ShapeDtypeStruct((s, D_MODEL), BF16),
                   jax.ShapeDtypeStruct(g.shape, F32), jax.ShapeDtypeStruct(g.shape, F32)],
        compiler_params=_cparams((_ARB,)),
    )(dxn, xh, rstd, g, w_out_t, *o_parts, h)


def _loss_grad(y, target, *, name):
    s, d = y.shape
    tb = _pick(s, (512, 256, 128))

    def body(y_ref, t_ref, dy_ref, l_ref):
        @pl.when(pl.program_id(0) == 0)
        def _():
            l_ref[...] = jnp.zeros_like(l_ref)

        err = y_ref[...] - t_ref[...]
        dy_ref[...] = err * (1.0 / d)
        per_tok = jnp.mean(err * err, axis=-1, keepdims=True)
        l_ref[...] += 0.5 * jnp.sum(per_tok, axis=0, keepdims=True)

    rows = pl.BlockSpec((tb, d), lambda i: (i, 0))
    return pl.pallas_call(
        body, name=name, grid=(s // tb,), in_specs=[rows, rows],
        out_specs=[rows, pl.BlockSpec((8, LANES), lambda i: (0, 0))],
        out_shape=[jax.ShapeDtypeStruct((s, d), F32), jax.ShapeDtypeStruct((8, LANES), F32)],
        compiler_params=_cparams((_ARB,)),
    )(y, target)


def _adamw(w, g, m, v, *, name):
    shape = w.shape
    w3, g3, m3, v3 = (a.reshape((1,) * (3 - a.ndim) + a.shape) for a in (w, g, m, v))
    a0, a1, a2 = w3.shape
    tb = _pick(a1, (256, 128)) if a1 % 8 == 0 else a1
    c1 = 1.0 - ADAM_B1 ** ADAM_STEP
    c2 = 1.0 - ADAM_B2 ** ADAM_STEP

    def body(w_ref, g_ref, m_ref, v_ref, d_ref, mo_ref, vo_ref):
        gg = g_ref[...]
        mn = ADAM_B1 * m_ref[...] + (1.0 - ADAM_B1) * gg
        vn = ADAM_B2 * v_ref[...] + (1.0 - ADAM_B2) * (gg * gg)
        mo_ref[...] = mn
        vo_ref[...] = vn
        d_ref[...] = -ADAM_LR * ((mn / c1) / (jnp.sqrt(vn / c2) + ADAM_EPS) + ADAM_WD * w_ref[...])

    spec = pl.BlockSpec((1, tb, a2), lambda i, j: (i, j, 0))
    outs = pl.pallas_call(
        body, name=name, grid=(a0, a1 // tb), in_specs=[spec] * 4, out_specs=[spec] * 3,
        out_shape=[jax.ShapeDtypeStruct(w3.shape, F32)] * 3, compiler_params=_cparams((_PAR, _PAR)),
    )(w3, g3, m3, v3)
    return tuple(a.reshape(shape) for a in outs)


def _place():
    x, y, c = lax.axis_index("x"), lax.axis_index("y"), lax.axis_index("c")
    return x, y, c, [(1 - x, y), (x, 1 - y), (1 - x, 1 - y)]


_ANY = pl.BlockSpec(memory_space=pl.ANY)


def _sem_pairs(n):
    return [pltpu.SemaphoreType.DMA((n,)), pltpu.SemaphoreType.DMA((n,))]


def _gather_chip_shards(tensors, *, name):
    nt = len(tensors)
    per = 7

    def body(*refs):
        srcs, outs = refs[:nt], refs[nt:2 * nt]
        send_sems, recv_sems = refs[2 * nt:]
        x, y, c, chips = _place()
        me = 2 * x + y
        sib = (x, y, 1 - c)

        def half(t, chip, hc):
            rh = tensors[t].shape[1] // 2
            return outs[t].at[chip, :, pl.ds(hc * rh, rh), :]

        def copy(t, kk, s_ref, d_ref, to):
            return pltpu.make_async_remote_copy(src_ref=s_ref, dst_ref=d_ref, send_sem=send_sems.at[per * t + kk],
                                                recv_sem=recv_sems.at[per * t + kk], device_id=to, device_id_type=MESH)

        sent = []
        for j, (cx, cy) in enumerate(chips):
            for t in range(nt):
                rh = tensors[t].shape[1] // 2
                cp = copy(t, j, srcs[t].at[:, pl.ds(c * rh, rh), :], half(t, me, c), (cx, cy, c))
                cp.start()
                sent.append(cp)
        own = [copy(t, 6, srcs[t], outs[t].at[me], sib) for t in range(nt)]
        for cp in own:
            cp.start()
        for j, (cx, cy) in enumerate(chips):
            for t in range(nt):
                landed = half(t, 2 * cx + cy, c)
                copy(t, j, landed, landed, (cx, cy, c)).wait_recv()
                fw = copy(t, 3 + j, landed, landed, sib)
                fw.start()
                sent.append(fw)
        for j, (cx, cy) in enumerate(chips):
            for t in range(nt):
                theirs = half(t, 2 * cx + cy, 1 - c)
                copy(t, 3 + j, theirs, theirs, sib).wait_recv()
        for cp in own:
            cp.wait()
        for cp in sent:
            cp.wait_send()

    return pl.pallas_call(
        body, name=name, in_specs=[_ANY] * nt, out_specs=[_ANY] * nt,
        out_shape=[jax.ShapeDtypeStruct((N_CHIPS,) + a.shape, a.dtype) for a in tensors],
        scratch_shapes=_sem_pairs(per * nt),
    )(*tensors)


def _swap_halves(gs, *, name):
    nt = len(gs)

    def body(*refs):
        srcs, outs = refs[:nt], refs[nt:2 * nt]
        send_sems, recv_sems = refs[2 * nt:]
        x, y, c, _ = _place()
        cps = [pltpu.make_async_remote_copy(src_ref=srcs[t].at[:, 1 - c], dst_ref=outs[t], send_sem=send_sems.at[t],
                                            recv_sem=recv_sems.at[t], device_id=(x, y, 1 - c), device_id_type=MESH)
               for t in range(nt)]
        for cp in cps:
            cp.start()
        for cp in cps:
            cp.wait()

    return pl.pallas_call(
        body, name=name, in_specs=[_ANY] * nt, out_specs=[_ANY] * nt,
        out_shape=[jax.ShapeDtypeStruct((N_CHIPS,) + g.shape[2:], g.dtype) for g in gs], scratch_shapes=_sem_pairs(nt),
    )(*gs)


def _scatter_to_chips(blocks, *, name):
    nt = len(blocks)

    def body(*refs):
        srcs, outs = refs[:nt], refs[nt:2 * nt]
        send_sems, recv_sems = refs[2 * nt:]
        x, y, c, chips = _place()
        cps = [pltpu.make_async_remote_copy(src_ref=srcs[t].at[2 * cx + cy], dst_ref=outs[t].at[j],
                                            send_sem=send_sems.at[3 * t + j], recv_sem=recv_sems.at[3 * t + j],
                                            device_id=(cx, cy, c), device_id_type=MESH)
               for j, (cx, cy) in enumerate(chips) for t in range(nt)]
        for cp in cps:
            cp.start()
        for cp in cps:
            cp.wait()

    return pl.pallas_call(
        body, name=name, in_specs=[_ANY] * nt, out_specs=[_ANY] * nt,
        out_shape=[jax.ShapeDtypeStruct((3,) + a.shape[1:], a.dtype) for a in blocks], scratch_shapes=_sem_pairs(3 * nt),
    )(*blocks)


def _join_halves(bufs, *, name):
    nt = len(bufs)

    def body(*refs):
        srcs, outs = refs[:nt], refs[nt:2 * nt]
        send_sems, recv_sems = refs[2 * nt:]
        x, y, c, _ = _place()
        cps = []
        for t in range(nt):
            rh = bufs[t].shape[1] // 2
            rows = pl.ds(c * rh, rh)
            cps.append(pltpu.make_async_remote_copy(src_ref=srcs[t].at[:, rows, :], dst_ref=outs[t].at[:, rows, :],
                                                    send_sem=send_sems.at[t], recv_sem=recv_sems.at[t],
                                                    device_id=(x, y, 1 - c), device_id_type=MESH))
        for cp in cps:
            cp.start()
        for cp in cps:
            cp.wait()

    return pl.pallas_call(
        body, name=name, in_specs=[_ANY] * nt, out_specs=[_ANY] * nt,
        out_shape=[jax.ShapeDtypeStruct(b.shape, b.dtype) for b in bufs], scratch_shapes=_sem_pairs(nt),
        input_output_aliases={t: t for t in range(nt)},
    )(*bufs)


def _add_sibling(g, recv, cidx, chip_idx, *, name):
    _, _, na, rh, cdim = g.shape
    tb = _pick(rh, (256, 128, 64, 32, 16))

    def body(c_ref, k_ref, g_ref, r_ref, s_ref, o_ref):
        tot = g_ref[0, 0] + r_ref[0]
        s_ref[0] = tot.astype(BF16)

        @pl.when(pl.program_id(2) == k_ref[0])
        def _():
            o_ref[...] = tot

    return pl.pallas_call(
        body, name=name,
        grid_spec=pltpu.PrefetchScalarGridSpec(
            num_scalar_prefetch=2, grid=(na, rh // tb, N_CHIPS),
            in_specs=[pl.BlockSpec((1, 1, 1, tb, cdim), lambda a, i, k, c_ref, k_ref: (k, c_ref[0], a, i, 0)),
                      pl.BlockSpec((1, 1, tb, cdim), lambda a, i, k, c_ref, k_ref: (k, a, i, 0))],
            out_specs=[pl.BlockSpec((1, 1, tb, cdim), lambda a, i, k, c_ref, k_ref: (k, a, i, 0)),
                       pl.BlockSpec((1, tb, cdim), lambda a, i, k, c_ref, k_ref: (a, i, 0))]),
        out_shape=[jax.ShapeDtypeStruct((N_CHIPS, na, rh, cdim), BF16), jax.ShapeDtypeStruct((na, rh, cdim), F32)],
        compiler_params=_cparams((_PAR, _PAR, _ARB)),
    )(cidx, chip_idx, g, recv)


def _add_chips(own, recv, cidx, *, name):
    na, rh, cdim = own.shape
    tb = _pick(rh, (256, 128, 64, 32, 16))
    nblk = rh // tb

    def body(c_ref, a_ref, r0_ref, r1_ref, r2_ref, o_ref):
        o_ref[...] = ((a_ref[...] + r0_ref[0].astype(F32)) + r1_ref[0].astype(F32)) + r2_ref[0].astype(F32)

    slot = lambda j: pl.BlockSpec((1, 1, tb, cdim), lambda a, i, c_ref: (j, a, i, 0))
    return pl.pallas_call(
        body, name=name,
        grid_spec=pltpu.PrefetchScalarGridSpec(
            num_scalar_prefetch=1, grid=(na, nblk),
            in_specs=[pl.BlockSpec((1, tb, cdim), lambda a, i, c_ref: (a, i, 0)), slot(0), slot(1), slot(2)],
            out_specs=pl.BlockSpec((1, tb, cdim), lambda a, i, c_ref: (a, c_ref[0] * nblk + i, 0))),
        out_shape=jax.ShapeDtypeStruct((na, 2 * rh, cdim), F32), compiler_params=_cparams((_PAR, _PAR)),
    )(cidx, own, recv, recv, recv)


def _all_reduce_small(v, *, name):
    r, cdim = v.shape

    def body(v_ref, o_ref, buf, send_sems, recv_sems):
        x, y, c, _ = _place()
        me = 4 * x + 2 * y + c
        buf[me] = v_ref[...]
        cps = []
        for p in range(1, N_DEV):
            to = (1 - x if p & 4 else x, 1 - y if p & 2 else y, 1 - c if p & 1 else c)
            cp = pltpu.make_async_remote_copy(src_ref=v_ref, dst_ref=buf.at[me], send_sem=send_sems.at[p - 1],
                                              recv_sem=recv_sems.at[p - 1], device_id=to, device_id_type=MESH)
            cp.start()
            cps.append(cp)
        for p in range(1, N_DEV):
            frm = (4 * x + 2 * y + c) ^ p
            pltpu.make_async_remote_copy(src_ref=v_ref, dst_ref=buf.at[frm], send_sem=send_sems.at[p - 1],
                                         recv_sem=recv_sems.at[p - 1], device_id=(x, y, c), device_id_type=MESH).wait_recv()
        for cp in cps:
            cp.wait_send()
        tot = buf[0]
        for i in range(1, N_DEV):
            tot = tot + buf[i]
        o_ref[...] = tot

    vm = pl.BlockSpec(memory_space=pltpu.VMEM)
    return pl.pallas_call(
        body, name=name, in_specs=[vm], out_specs=vm, out_shape=jax.ShapeDtypeStruct((r, cdim), F32),
        scratch_shapes=[pltpu.VMEM((N_DEV, r, cdim), F32), pltpu.SemaphoreType.DMA((N_DEV - 1,)), pltpu.SemaphoreType.DMA((N_DEV - 1,))],
    )(v)


_SHARDED = ("even_w_in", "even_w_uq", "even_w_ukv", "even_w_out", "odd_w_in", "odd_w_out")
_COL_SHARDED = ("even_w_in", "even_w_uq", "even_w_ukv", "odd_w_in")
PACK_COLS = 1024


def _unshard(name, stacked):
    n, a, b, cc = stacked.shape
    if name in _COL_SHARDED:
        return stacked.transpose(1, 2, 0, 3).reshape(a, b, n * cc)
    return stacked.transpose(1, 0, 2, 3).reshape(a, n * b, cc)


def _chip_halves(name, full):
    a, b, cc = full.shape
    if name in _COL_SHARDED:
        return full.reshape(a, 2, b // 2, N_CHIPS, cc // N_CHIPS).transpose(3, 1, 0, 2, 4)
    return full.reshape(a, N_CHIPS, 2, b // (2 * N_CHIPS), cc).transpose(1, 2, 0, 3, 4)


def _rope_tables(s):
    pos = jnp.arange(s, dtype=F32)

    def ang(d):
        inv = ROPE_THETA ** (-jnp.arange(0, d, 2, dtype=F32) / d)
        a = pos[:, None] * inv[None, :]
        return jnp.cos(a), jnp.sin(a)

    c16, s16 = ang(MLA_ROPE)
    one, zero = jnp.ones((s, KPE_LANE), F32), jnp.zeros((s, KPE_LANE), F32)
    cos_m = jnp.concatenate([one, c16, c16, one[:, :32]], axis=1)
    sin_m = jnp.concatenate([zero, -s16, s16, zero[:, :32]], axis=1)
    c32, s32 = ang(SWA_DIM)
    cos_s = jnp.concatenate([c32, c32, c32, c32], axis=1)
    sin_s = jnp.concatenate([-s32, s32, -s32, s32], axis=1)
    return cos_m, sin_m, cos_s, sin_s


def _even_weights(w_in, w_uq, w_ukv):
    zeros = lambda n: jnp.zeros((D_MODEL, n), w_in.dtype)
    wcq, wckv, wkpe = w_in[:, 0:256], w_in[:, 256:384], w_in[:, 384:416]
    wfq, wfk, wfv = w_in[:, 416:928], w_in[:, 928:1440], w_in[:, 1440:1952]
    wfl, wg = w_in[:, 1952:1960], w_in[:, 1960:2984]
    misc = jnp.concatenate([zeros(KPE_LANE), wkpe, wfl, zeros(LANES - FL_LANE - FOX_HEADS)], axis=1)
    w_in_p = jnp.concatenate([wg, wfq, wfk, wfv, wcq, wckv, misc], axis=1)
    uq = w_uq.reshape(MLA_Q_RANK, MLA_HEADS, MLA_NOPE + MLA_ROPE)
    uq_p = jnp.pad(uq, ((0, 0), (0, 0), (0, HEAD_PAD - MLA_NOPE - MLA_ROPE))).reshape(MLA_Q_RANK, MLA_HEADS * HEAD_PAD)
    ukv = w_ukv.reshape(MLA_KV_RANK, MLA_HEADS, MLA_NOPE + MLA_V)
    uk_p = jnp.pad(ukv[..., :MLA_NOPE], ((0, 0), (0, 0), (0, HEAD_PAD - MLA_NOPE))).reshape(MLA_KV_RANK, MLA_HEADS * HEAD_PAD)
    uv = ukv[..., MLA_NOPE:].reshape(MLA_KV_RANK, MLA_HEADS * MLA_V)
    ukv_p = jnp.concatenate([uk_p, uv], axis=1)
    return w_in_p, w_in_p.T, uq_p, uq_p.T, ukv_p, ukv_p.T


def _even_weight_grads(dw_in_p, duq_p, dukv_p):
    g = dw_in_p
    gate, fq, fk, fv = g[:, E_GATE:E_FQ], g[:, E_FQ:E_FK], g[:, E_FK:E_FV], g[:, E_FV:E_SMALL]
    cq, ckv, misc = g[:, E_SMALL:E_SMALL + 256], g[:, E_SMALL + 256:E_SMALL + 384], g[:, E_SMALL + 384:]
    dw_in = jnp.concatenate([cq, ckv, misc[:, KPE_LANE:KPE_LANE + MLA_ROPE], fq, fk, fv, misc[:, FL_LANE:FL_LANE + FOX_HEADS], gate], axis=1)
    duq = duq_p.reshape(MLA_Q_RANK, MLA_HEADS, HEAD_PAD)[..., :MLA_NOPE + MLA_ROPE].reshape(MLA_Q_RANK, -1)
    nq = MLA_HEADS * HEAD_PAD
    dk = dukv_p[:, :nq].reshape(MLA_KV_RANK, MLA_HEADS, HEAD_PAD)[..., :MLA_NOPE]
    dvv = dukv_p[:, nq:].reshape(MLA_KV_RANK, MLA_HEADS, MLA_V)
    dukv = jnp.concatenate([dk, dvv], axis=-1).reshape(MLA_KV_RANK, -1)
    return dw_in, duq, dukv


def _interleave(w, forward):
    a, b = (SWA_KV_HEADS, SWA_HEADS // SWA_KV_HEADS) if forward else (SWA_HEADS // SWA_KV_HEADS, SWA_KV_HEADS)
    return w.reshape(w.shape[0], a, b, -1).transpose(0, 2, 1, 3).reshape(w.shape[0], -1)


def _odd_weights(w_in, w_out):
    q, k, v, gate = w_in[:, 0:1024], w_in[:, 1024:1152], w_in[:, 1152:1280], w_in[:, 1280:2304]
    w_p = jnp.concatenate([_interleave(gate, True), _interleave(q, True), k, v], axis=1)
    w_out_p = _interleave(w_out.T, True).T
    return w_p, w_p.T, w_out_p, w_out_p.T


def _odd_weight_grads(g, dw_out_p):
    dw_in = jnp.concatenate([_interleave(g[:, O_Q:O_K], False), g[:, O_K:O_V], g[:, O_V:], _interleave(g[:, O_GATE:O_Q], False)], axis=1)
    return dw_in, _interleave(dw_out_p.T, False).T


def kernel(x, even_w_in, even_q_norm, even_w_uq, even_kv_norm, even_w_ukv, even_b_f, even_w_out, even_ln_g, even_ln_b, odd_w_in, odd_sinks, odd_w_out, odd_ln_g, odd_ln_b, loss_target, m_even_w_in, m_even_q_norm, m_even_w_uq, m_even_kv_norm, m_even_w_ukv, m_even_b_f, m_even_w_out, m_even_ln_g, m_even_ln_b, m_odd_w_in, m_odd_sinks, m_odd_w_out, m_odd_ln_g, m_odd_ln_b, v_even_w_in, v_even_q_norm, v_even_w_uq, v_even_kv_norm, v_even_w_ukv, v_even_b_f, v_even_w_out, v_even_ln_g, v_even_ln_b, v_odd_w_in, v_odd_sinks, v_odd_w_out, v_odd_ln_g, v_odd_ln_b):
    weights = dict(even_w_in=even_w_in, even_q_norm=even_q_norm, even_w_uq=even_w_uq, even_kv_norm=even_kv_norm,
                   even_w_ukv=even_w_ukv, even_b_f=even_b_f, even_w_out=even_w_out, even_ln_g=even_ln_g, even_ln_b=even_ln_b,
                   odd_w_in=odd_w_in, odd_sinks=odd_sinks, odd_w_out=odd_w_out, odd_ln_g=odd_ln_g, odd_ln_b=odd_ln_b)
    mom_m = dict(even_w_in=m_even_w_in, even_q_norm=m_even_q_norm, even_w_uq=m_even_w_uq, even_kv_norm=m_even_kv_norm,
                 even_w_ukv=m_even_w_ukv, even_b_f=m_even_b_f, even_w_out=m_even_w_out, even_ln_g=m_even_ln_g, even_ln_b=m_even_ln_b,
                 odd_w_in=m_odd_w_in, odd_sinks=m_odd_sinks, odd_w_out=m_odd_w_out, odd_ln_g=m_odd_ln_g, odd_ln_b=m_odd_ln_b)
    mom_v = dict(even_w_in=v_even_w_in, even_q_norm=v_even_q_norm, even_w_uq=v_even_w_uq, even_kv_norm=v_even_kv_norm,
                 even_w_ukv=v_even_w_ukv, even_b_f=v_even_b_f, even_w_out=v_even_w_out, even_ln_g=v_even_ln_g, even_ln_b=v_even_ln_b,
                 odd_w_in=v_odd_w_in, odd_sinks=v_odd_sinks, odd_w_out=v_odd_w_out, odd_ln_g=v_odd_ln_g, odd_ln_b=v_odd_ln_b)
    names = list(weights)
    xl = x[0]
    tgt = loss_target[0]
    s = xl.shape[0]
    ax, ay, ac = lax.axis_index("x"), lax.axis_index("y"), lax.axis_index("c")
    chip = 2 * ax + ay
    c_idx = ac.astype(jnp.int32).reshape(1)
    chip_idx = chip.astype(jnp.int32).reshape(1)

    ln_odd = jnp.pad(jnp.concatenate([odd_ln_g, odd_ln_b]), ((0, 12), (0, 0)))[None]
    gathered = _gather_chip_shards([weights[n].astype(BF16) for n in _SHARDED] + [ln_odd], name="gather_weights")
    full = {n: _unshard(n, gathered[i]) for i, n in enumerate(_SHARDED)}
    ln_all = gathered[-1][:, 0]
    odd_g_full = ln_all[:, 0:2].transpose(1, 0, 2).reshape(2, D_MODEL)
    odd_b_full = ln_all[:, 2:4].transpose(1, 0, 2).reshape(2, D_MODEL)

    cos_m, sin_m, cos_s, sin_s = _rope_tables(s)
    bf_tiles = jnp.pad(even_b_f, ((0, 0), (FL_LANE, LANES - FL_LANE - FOX_HEADS)))
    sink_tiles = jnp.broadcast_to(_interleave(odd_sinks, True)[:, :, None, None], (2, SWA_HEADS, 1, LANES))
    mla_scale = (MLA_NOPE + MLA_ROPE) ** -0.5
    fox_scale = FOX_DIM ** -0.5
    mla_pairs, fox_pairs = MLA_HEADS // 2, FOX_HEADS // 2

    saved = []
    x_f, x_b = xl, xl.astype(BF16)
    for layer in range(DEPTH):
        j = layer // 2
        ln = f"L{layer}"
        if layer % 2 == 0:
            w_in_p, w_in_t, uq_p, uq_t, ukv_p, ukv_t = _even_weights(full["even_w_in"][j], full["even_w_uq"][j], full["even_w_ukv"][j])
            w_out, w_out_t = full["even_w_out"][j], full["even_w_out"][j].T
            qg, kg, bft = even_q_norm[j][None], even_kv_norm[j][None], bf_tiles[j][None]
            h, hb = _mm(x_b, w_in_p, out_dtype=F32, name=ln + "_in", also_bf16=True)
            q, k, v, qn, kvn, logf = _even_mid_fwd(h, qg, kg, bft, uq_p, ukv_p, cos_m, sin_m, name=ln + "_mid")
            cum = _cumsum(logf, reverse=False, name=ln + "_cum")[:, FL_LANE:FL_LANE + FOX_HEADS].T
            ccol, crow = cum[:, :, None], cum[:, None, :]
            o_mla, lse_mla = _attn_fwd((q, 0), (k, 0), (v, 0), None, None, dk=HEAD_PAD, npairs=mla_pairs, scale=mla_scale, name=ln + "_mla")
            o_fox, lse_fox = _attn_fwd((hb, E_FQ), (hb, E_FK), (hb, E_FV), ccol, crow, dk=FOX_DIM, npairs=fox_pairs,
                                       scale=fox_scale, name=ln + "_fox")
            o_parts = [o_mla, o_fox]
            g_ln, b_ln = even_ln_g[j][None], even_ln_b[j][None]
            x_n, x_nb, z, xh, rstd = _out_fwd(o_parts, h, x_f, w_out, g_ln, b_ln, name=ln + "_out")
            saved.append(dict(h=h, hb=hb, x_b=x_b, qn=qn, kvn=kvn, q=q, k=k, v=v, ccol=ccol, crow=crow,
                              o_mla=o_mla, o_fox=o_fox, lse_mla=lse_mla, lse_fox=lse_fox, o_parts=o_parts, z=z, xh=xh, rstd=rstd,
                              w_in_t=w_in_t, uq_t=uq_t, ukv_t=ukv_t, w_out_t=w_out_t, qg=qg, kg=kg, bft=bft, g_ln=g_ln))
        else:
            w_in_p, w_in_t, w_out, w_out_t = _odd_weights(full["odd_w_in"][j], full["odd_w_out"][j])
            h = _mm(x_b, w_in_p, out_dtype=F32, name=ln + "_in")
            q, k, v = _odd_rope(h, cos_s, sin_s, name=ln + "_rope")
            kp = jnp.pad(k, ((WINDOW, 0), (0, 0)))
            vp = jnp.pad(v, ((WINDOW, 0), (0, 0)))
            o_t, lse = _swa_fwd(q, kp, vp, sink_tiles[j], name=ln + "_swa")
            o = o_t.T
            g_ln, b_ln = odd_g_full[j][None], odd_b_full[j][None]
            x_n, x_nb, z, xh, rstd = _out_fwd([o], h, x_f, w_out, g_ln, b_ln, name=ln + "_out")
            saved.append(dict(h=h, x_b=x_b, q=q, kp=kp, vp=vp, lse=lse, o_t=o_t, o_parts=[o], z=z, xh=xh, rstd=rstd,
                              w_in_t=w_in_t, w_out_t=w_out_t, g_ln=g_ln))
        x_f, x_b = x_n, x_nb

    dxn, loss_tile = _loss_grad(x_f, tgt, name="loss")

    grads = {n: [None, None] for n in names}
    for layer in reversed(range(DEPTH)):
        j = layer // 2
        ln = f"L{layer}"
        sv = saved[layer]
        dr, dy, do, dgate, dg_ln, db_ln = _out_bwd(dxn, sv["xh"], sv["rstd"], sv["g_ln"], sv["w_out_t"], sv["o_parts"], sv["h"], name=ln + "_outb")
        dw_out = _mm(sv["z"].T, dy, out_dtype=F32, name=ln + "_dwout")
        if layer % 2 == 0:
            hb = sv["hb"]
            dq, dk, dv = _attn_bwd((sv["q"], 0), (sv["k"], 0), (sv["v"], 0), (sv["o_mla"], 0), (do, 0), sv["lse_mla"], None, None,
                                   dk=HEAD_PAD, npairs=mla_pairs, scale=mla_scale, dq_dtype=F32, name=ln + "_mlab")
            dfq, dfk, dfv, dcrow, dccol = _attn_bwd((hb, E_FQ), (hb, E_FK), (hb, E_FV), (sv["o_fox"], 0), (do, MLA_HEADS * MLA_V),
                                                    sv["lse_fox"], sv["ccol"], sv["crow"], dk=FOX_DIM, npairs=fox_pairs,
                                                    scale=fox_scale, dq_dtype=BF16, name=ln + "_foxb")
            dcum = jnp.pad((dcrow[:, 0, :] + dccol[:, :, 0]).T, ((0, 0), (FL_LANE, LANES - FL_LANE - FOX_HEADS)))
            dlogf = _cumsum(dcum, reverse=True, name=ln + "_cumb")
            dh_small, dq_pre, dqg, dkg, dbf = _even_mid_bwd(
                sv["h"], dq, dk, dv, dlogf, sv["qg"], sv["kg"], sv["bft"], sv["uq_t"], sv["ukv_t"], cos_m, sin_m, name=ln + "_midb")
            duq_p = _mm(sv["qn"].T, dq_pre, out_dtype=F32, name=ln + "_dwuq")
            dkv_cat = jnp.concatenate([dk.astype(BF16), dv], axis=1)
            dukv_p = _mm(sv["kvn"].T, dkv_cat, out_dtype=F32, name=ln + "_dwukv")
            dh = jnp.concatenate([dgate, dfq, dfk, dfv, dh_small], axis=1)
            dw_in_p = _mm(sv["x_b"].T, dh, out_dtype=F32, name=ln + "_dwin")
            dw_in, duq, dukv = _even_weight_grads(dw_in_p, duq_p, dukv_p)
            for n, val in (("even_w_in", dw_in), ("even_w_uq", duq), ("even_w_ukv", dukv), ("even_w_out", dw_out),
                           ("even_q_norm", dqg[0]), ("even_kv_norm", dkg[0]), ("even_b_f", dbf[0, FL_LANE:FL_LANE + FOX_HEADS]),
                           ("even_ln_g", dg_ln[0]), ("even_ln_b", db_ln[0])):
                grads[n][j] = val
        else:
            dq_t, dkp, dvp, dsink = _swa_bwd(sv["q"], sv["kp"], sv["vp"], sv["kp"].T, sink_tiles[j], sv["o_t"], do, do.T, sv["lse"],
                                             name=ln + "_swab")
            dq_r, dk_r = _odd_rope_bwd(dq_t.T, dkp[WINDOW:], cos_s, sin_s, name=ln + "_ropeb")
            dh = jnp.concatenate([dgate, dq_r, dk_r, dvp[WINDOW:].astype(BF16)], axis=1)
            dw_in_p = _mm(sv["x_b"].T, dh, out_dtype=F32, name=ln + "_dwin")
            dw_in, dw_out = _odd_weight_grads(dw_in_p, dw_out)
            for n, val in (("odd_w_in", dw_in), ("odd_w_out", dw_out), ("odd_sinks", _interleave(dsink[None, :, 0, 0], False)[0]),
                           ("odd_ln_g", dg_ln[0]), ("odd_ln_b", db_ln[0])):
                grads[n][j] = val
        dxn = _mm(dh, sv["w_in_t"], out_dtype=F32, name=ln + "_dx", res=dr, res_scale=ALPHA)
    grad_x = dxn[None]
    grads = {n: jnp.stack(v) for n, v in grads.items()}

    gp = [_chip_halves(n, grads[n]) for n in _SHARDED]
    from_sib = _swap_halves(gp, name="grad_swap")
    pair = [_add_sibling(g, r, c_idx, chip_idx, name="grad_add_sibling_" + n) for n, g, r in zip(_SHARDED, gp, from_sib)]
    from_chips = _scatter_to_chips([p[0] for p in pair], name="grad_scatter")
    mine = [_add_chips(p[1], r, c_idx, name="grad_add_chips_" + n) for n, p, r in zip(_SHARDED, pair, from_chips)]
    gshard = dict(zip(_SHARDED, _join_halves(mine, name="grad_join")))

    small = [n for n in names if n not in _SHARDED]
    sv_flat = jnp.concatenate([grads[n].reshape(-1) for n in small] + [loss_tile[0, :1]])
    sv_real = sv_flat.shape[0]
    sv_rows = -(-sv_real // (PACK_COLS * 8)) * 8
    sv_sum = _all_reduce_small(jnp.pad(sv_flat, (0, sv_rows * PACK_COLS - sv_real)).reshape(sv_rows, PACK_COLS), name="small_all_reduce").reshape(-1)
    off = 0
    for n in small:
        size = int(np.prod(grads[n].shape))
        gfull = sv_sum[off:off + size].reshape(grads[n].shape)
        off += size
        if n in ("odd_ln_g", "odd_ln_b"):
            gfull = lax.dynamic_slice_in_dim(gfull, chip * (D_MODEL // N_CHIPS), D_MODEL // N_CHIPS, axis=1)
        gshard[n] = gfull
    loss = sv_sum[off]

    deltas, new_m, new_v = {}, {}, {}
    for n in names:
        deltas[n], new_m[n], new_v[n] = _adamw(weights[n], gshard[n], mom_m[n], mom_v[n], name="adamw_" + n)
    return (loss, grad_x, *[gshard[n] for n in names], *[deltas[n] for n in names],
            *[new_m[n] for n in names], *[new_v[n] for n in names])
```

```python
import jax
import jax.numpy as jnp
import numpy as np
from jax import lax
from jax.experimental import pallas as pl
from jax.experimental.pallas import tpu as pltpu

F32 = jnp.float32
BF16 = jnp.bfloat16
MESH = pl.DeviceIdType.MESH

D_MODEL = 1024
DEPTH = 4
ROPE_THETA = 10000.0
MLA_HEADS, MLA_NOPE, MLA_ROPE, MLA_V = 8, 64, 32, 64
MLA_Q_RANK, MLA_KV_RANK = 256, 128
FOX_HEADS, FOX_DIM = 8, 64
SWA_HEADS, SWA_KV_HEADS, SWA_DIM, WINDOW = 16, 2, 64, 128
RMS_EPS = 1e-6
LN_EPS = 1e-5
ALPHA = (2 * DEPTH) ** 0.25
EVEN_IN = 2984
ODD_IN = 2304
ADAM_LR, ADAM_B1, ADAM_B2, ADAM_EPS, ADAM_WD, ADAM_STEP = 0.001, 0.9, 0.999, 1e-08, 0.01, 10

LANES = 128
HALF = LANES // 2
HEAD_PAD = 128
N_CHIPS = 4
N_DEV = 8
E_GATE, E_FQ, E_FK, E_FV, E_SMALL = 0, 1024, 1536, 2048, 2560
E_PAD_IN = 3072
KPE_LANE = 64
FL_LANE = 96
O_GATE, O_Q, O_K, O_V = 0, 1024, 2048, 2176

_ARB = "arbitrary"
_PAR = "parallel"


def _cparams(sem):
    return pltpu.CompilerParams(dimension_semantics=sem)


def _pick(n, cands):
    for c in cands:
        if n % c == 0:
            return c
    return n


RESIDENT_B_BYTES = 8 << 20


def _mm_resident(a, b, *, out_dtype, name, res, res_scale, also_bf16):
    m, k = a.shape
    _, n = b.shape
    tm = _pick(m, (256, 128))
    cn = _pick(n, (512, 384, 256, 128))
    dtypes = [out_dtype, BF16] if also_bf16 else [out_dtype]

    def body(*refs):
        a_ref, b_ref = refs[:2]
        r_ref = refs[2] if res is not None else None
        outs = refs[3 if res is not None else 2:]
        av = a_ref[...].astype(BF16)
        for c0 in range(0, n, cn):
            r = jnp.dot(av, b_ref[:, c0:c0 + cn].astype(BF16), preferred_element_type=F32)
            if res is not None:
                r = r + res_scale * r_ref[:, c0:c0 + cn]
            for o_ref in outs:
                o_ref[:, c0:c0 + cn] = r.astype(o_ref.dtype)

    rows = lambda w: pl.BlockSpec((tm, w), lambda i: (i, 0))
    in_specs = [rows(k), pl.BlockSpec((k, n), lambda i: (0, 0))] + ([rows(n)] if res is not None else [])
    out = pl.pallas_call(
        body, name=name, grid=(m // tm,), in_specs=in_specs, out_specs=[rows(n)] * len(dtypes),
        out_shape=[jax.ShapeDtypeStruct((m, n), d) for d in dtypes], compiler_params=_cparams((_PAR,)),
    )(*([a, b] + ([res] if res is not None else [])))
    return out if also_bf16 else out[0]


def _mm(a, b, *, out_dtype, name, res=None, res_scale=1.0, also_bf16=False):
    m, k = a.shape
    _, n = b.shape
    if b.size * b.dtype.itemsize <= RESIDENT_B_BYTES:
        return _mm_resident(a, b, out_dtype=out_dtype, name=name, res=res, res_scale=res_scale, also_bf16=also_bf16)
    tm = _pick(m, (512, 256, 128))
    tn = _pick(n, (1024, 768, 512, 384, 256, 128))
    tk = _pick(k, (1024, 768, 512, 256, 128))
    nk = k // tk

    def body(*refs):
        refs = list(refs)
        a_ref, b_ref = refs[:2]
        r_ref = refs[2] if res is not None else None
        acc_ref = refs[-1]
        outs = refs[3 if res is not None else 2:-1]
        kk = pl.program_id(2)

        @pl.when(kk == 0)
        def _():
            acc_ref[...] = jnp.zeros_like(acc_ref)

        acc_ref[...] += jnp.dot(a_ref[...].astype(BF16), b_ref[...].astype(BF16), preferred_element_type=F32)

        @pl.when(kk == nk - 1)
        def _():
            r = acc_ref[...]
            if res is not None:
                r = r + res_scale * r_ref[...]
            for o_ref in outs:
                o_ref[...] = r.astype(o_ref.dtype)

    in_specs = [pl.BlockSpec((tm, tk), lambda i, j, kk: (i, kk)), pl.BlockSpec((tk, tn), lambda i, j, kk: (kk, j))]
    args = [a, b]
    if res is not None:
        in_specs.append(pl.BlockSpec((tm, tn), lambda i, j, kk: (i, j)))
        args.append(res)
    ospec = pl.BlockSpec((tm, tn), lambda i, j, kk: (i, j))
    dtypes = [out_dtype, BF16] if also_bf16 else [out_dtype]
    out = pl.pallas_call(
        body, name=name, grid=(m // tm, n // tn, nk), in_specs=in_specs,
        out_specs=[ospec] * len(dtypes), out_shape=[jax.ShapeDtypeStruct((m, n), d) for d in dtypes],
        scratch_shapes=[pltpu.VMEM((tm, tn), F32)],
        compiler_params=_cparams((_PAR, _PAR, _ARB)),
    )(*args)
    return out if also_bf16 else out[0]


def _rope_tile(t, cos, sin, half):
    w = t.shape[-1]
    lane = lax.broadcasted_iota(jnp.int32, t.shape, 1)
    first = (lane % (2 * half)) < half
    sw = jnp.where(first, pltpu.roll(t, w - half, 1), pltpu.roll(t, half, 1))
    return t * cos + sw * sin


def _sigmoid(x):
    return 1.0 / (1.0 + jnp.exp(-x))


def _lane_mask(shape, lo, hi):
    lane = lax.broadcasted_iota(jnp.int32, shape, 1)
    return (lane >= lo) & (lane < hi)


def _rms(x, g):
    r = lax.rsqrt(jnp.mean(x * x, axis=-1, keepdims=True) + RMS_EPS)
    u = x * r
    return u, r, u * g


def _rms_bwd(dy, u, r, g):
    dyg = dy * g
    dx = r * (dyg - u * jnp.mean(dyg * u, axis=-1, keepdims=True))
    return dx, jnp.sum(dy * u, axis=0, keepdims=True)


def _even_mid_fwd(h, qg, kg, bf_tile, w_uq, w_ukv, cos, sin, *, name):
    s = h.shape[0]
    tb = _pick(s, (512, 256, 128))
    nq = MLA_HEADS * HEAD_PAD

    def body(h_ref, qg_ref, kg_ref, bf_ref, wuq_ref, wukv_ref, cos_ref, sin_ref,
             q_ref, k_ref, v_ref, qn_ref, kvn_ref, lf_ref):
        hb = h_ref[...]
        cq, ckv, misc = hb[:, :MLA_Q_RANK], hb[:, MLA_Q_RANK:MLA_Q_RANK + MLA_KV_RANK], hb[:, MLA_Q_RANK + MLA_KV_RANK:]
        cs, sn = cos_ref[...], sin_ref[...]
        _, _, qn = _rms(cq, qg_ref[...])
        qn = qn.astype(BF16)
        qn_ref[...] = qn
        q = jnp.dot(qn, wuq_ref[...], preferred_element_type=F32)
        _, _, kvn = _rms(ckv, kg_ref[...])
        kvn = kvn.astype(BF16)
        kvn_ref[...] = kvn
        kv = jnp.dot(kvn, wukv_ref[...], preferred_element_type=F32)
        kpe = jnp.where(_lane_mask(misc.shape, KPE_LANE, KPE_LANE + MLA_ROPE), _rope_tile(misc, cs, sn, MLA_ROPE // 2), 0.0)
        for hd in range(MLA_HEADS):
            sl = slice(hd * HEAD_PAD, (hd + 1) * HEAD_PAD)
            q_ref[:, sl] = _rope_tile(q[:, sl], cs, sn, MLA_ROPE // 2).astype(BF16)
            k_ref[:, sl] = (kv[:, sl] + kpe).astype(BF16)
        v_ref[...] = kv[:, nq:].astype(BF16)
        xf = misc + bf_ref[...]
        logf = jnp.minimum(xf, 0.0) - jnp.log(1.0 + jnp.exp(-jnp.abs(xf)))
        lf_ref[...] = jnp.where(_lane_mask(misc.shape, FL_LANE, FL_LANE + FOX_HEADS), logf, 0.0)

    full = lambda a: pl.BlockSpec(a.shape, lambda i: (0,) * a.ndim)
    rows = lambda w, c=0: pl.BlockSpec((tb, w), lambda i: (i, c))
    return pl.pallas_call(
        body, name=name, grid=(s // tb,),
        in_specs=[rows(512, E_SMALL // 512), full(qg), full(kg), full(bf_tile), full(w_uq), full(w_ukv), rows(LANES), rows(LANES)],
        out_specs=[rows(nq), rows(nq), rows(MLA_HEADS * MLA_V), rows(MLA_Q_RANK), rows(MLA_KV_RANK), rows(LANES)],
        out_shape=[jax.ShapeDtypeStruct((s, nq), BF16), jax.ShapeDtypeStruct((s, nq), BF16),
                   jax.ShapeDtypeStruct((s, MLA_HEADS * MLA_V), BF16), jax.ShapeDtypeStruct((s, MLA_Q_RANK), BF16),
                   jax.ShapeDtypeStruct((s, MLA_KV_RANK), BF16), jax.ShapeDtypeStruct((s, LANES), F32)],
        compiler_params=_cparams((_PAR,)),
    )(h, qg, kg, bf_tile, w_uq, w_ukv, cos, sin)


def _even_mid_bwd(h, dq, dk, dv, dlogf, qg, kg, bf_tile, w_uq_t, w_ukv_t, cos, sin, *, name):
    s = h.shape[0]
    tb = _pick(s, (256, 128))
    nq = MLA_HEADS * HEAD_PAD

    def body(h_ref, dq_ref, dk_ref, dv_ref, dlf_ref, qg_ref, kg_ref, bf_ref, wuqt_ref, wukvt_ref, cos_ref, sin_ref,
             dh_ref, dqp_ref, dqg_ref, dkg_ref, dbf_ref):
        @pl.when(pl.program_id(0) == 0)
        def _():
            dqg_ref[...] = jnp.zeros_like(dqg_ref)
            dkg_ref[...] = jnp.zeros_like(dkg_ref)
            dbf_ref[...] = jnp.zeros_like(dbf_ref)

        hb = h_ref[...]
        cq, ckv, misc = hb[:, :MLA_Q_RANK], hb[:, MLA_Q_RANK:MLA_Q_RANK + MLA_KV_RANK], hb[:, MLA_Q_RANK + MLA_KV_RANK:]
        cs, sn = cos_ref[...], -sin_ref[...]
        dkpe = jnp.zeros(misc.shape, F32)
        for hd in range(MLA_HEADS):
            sl = slice(hd * HEAD_PAD, (hd + 1) * HEAD_PAD)
            dqp_ref[:, sl] = _rope_tile(dq_ref[:, sl], cs, sn, MLA_ROPE // 2).astype(BF16)
            dkpe = dkpe + dk_ref[:, sl]
        dqn = jnp.dot(dqp_ref[...], wuqt_ref[...], preferred_element_type=F32)
        uq, rq, _ = _rms(cq, qg_ref[...])
        dcq, dqg = _rms_bwd(dqn, uq, rq, qg_ref[...])
        dqg_ref[...] += dqg
        dkv = jnp.concatenate([dk_ref[...].astype(BF16), dv_ref[...]], axis=1)
        dkvn = jnp.dot(dkv, wukvt_ref[...], preferred_element_type=F32)
        uk, rk, _ = _rms(ckv, kg_ref[...])
        dckv, dkg = _rms_bwd(dkvn, uk, rk, kg_ref[...])
        dkg_ref[...] += dkg
        dmisc = jnp.where(_lane_mask(misc.shape, KPE_LANE, KPE_LANE + MLA_ROPE), _rope_tile(dkpe, cs, sn, MLA_ROPE // 2), 0.0)
        dfl = jnp.where(_lane_mask(misc.shape, FL_LANE, FL_LANE + FOX_HEADS), dlf_ref[...] * _sigmoid(-(misc + bf_ref[...])), 0.0)
        dbf_ref[...] += jnp.sum(dfl, axis=0, keepdims=True)
        dh_ref[:, :MLA_Q_RANK] = dcq.astype(BF16)
        dh_ref[:, MLA_Q_RANK:MLA_Q_RANK + MLA_KV_RANK] = dckv.astype(BF16)
        dh_ref[:, MLA_Q_RANK + MLA_KV_RANK:] = (dmisc + dfl).astype(BF16)

    full = lambda a: pl.BlockSpec(a.shape, lambda i: (0,) * a.ndim)
    rows = lambda w, c=0: pl.BlockSpec((tb, w), lambda i: (i, c))
    return pl.pallas_call(
        body, name=name, grid=(s // tb,),
        in_specs=[rows(512, E_SMALL // 512), rows(nq), rows(nq), rows(MLA_HEADS * MLA_V), rows(LANES), full(qg), full(kg),
                  full(bf_tile), full(w_uq_t), full(w_ukv_t), rows(LANES), rows(LANES)],
        out_specs=[rows(512), rows(nq), full(qg), full(kg), full(bf_tile)],
        out_shape=[jax.ShapeDtypeStruct((s, 512), BF16), jax.ShapeDtypeStruct((s, nq), BF16),
                   jax.ShapeDtypeStruct(qg.shape, F32), jax.ShapeDtypeStruct(kg.shape, F32),
                   jax.ShapeDtypeStruct(bf_tile.shape, F32)],
        compiler_params=_cparams((_ARB,)),
    )(h, dq, dk, dv, dlogf, qg, kg, bf_tile, w_uq_t, w_ukv_t, cos, sin)


def _cumsum(x, *, reverse, name):
    s = x.shape[0]
    tb = _pick(s, (512, 256, 128))
    nb = s // tb

    def body(x_ref, o_ref, carry_ref):
        @pl.when(pl.program_id(0) == 0)
        def _():
            carry_ref[...] = jnp.zeros_like(carry_ref)

        xv = x_ref[...]
        r = lax.broadcasted_iota(jnp.int32, (tb, tb), 0)
        c = lax.broadcasted_iota(jnp.int32, (tb, tb), 1)
        tri = jnp.where((c >= r) if reverse else (c <= r), 1.0, 0.0).astype(BF16)
        hi = xv.astype(BF16)
        r1 = xv - hi.astype(F32)
        mid = r1.astype(BF16)
        lo = (r1 - mid.astype(F32)).astype(BF16)
        cs = (jnp.dot(tri, hi, preferred_element_type=F32) + jnp.dot(tri, mid, preferred_element_type=F32)
              + jnp.dot(tri, lo, preferred_element_type=F32)) + carry_ref[...]
        o_ref[...] = cs
        carry_ref[...] = cs[0:1, :] if reverse else cs[tb - 1:tb, :]

    imap = (lambda i: (nb - 1 - i, 0)) if reverse else (lambda i: (i, 0))
    return pl.pallas_call(
        body, name=name, grid=(nb,), in_specs=[pl.BlockSpec((tb, LANES), imap)],
        out_specs=pl.BlockSpec((tb, LANES), imap), out_shape=jax.ShapeDtypeStruct(x.shape, F32),
        scratch_shapes=[pltpu.VMEM((1, LANES), F32)], compiler_params=_cparams((_ARB,)),
    )(x)


_NT = (((1,), (1,)), ((), ()))
_TN = (((0,), (0,)), ((), ()))


def _head_sel(x, h, dk):
    if dk == LANES:
        return x[:, h * LANES:(h + 1) * LANES]
    return jnp.where(_lane_mask(x.shape, h * HALF, (h + 1) * HALF), x, jnp.zeros_like(x))


def _head_other(x, h, dk):
    return x[:, h * LANES:(h + 1) * LANES] if dk == LANES else x


def _pair(parts, dk=HALF):
    if dk == LANES:
        return jnp.concatenate(parts, axis=1)
    return jnp.where(_lane_mask(parts[0].shape, 0, HALF), parts[0], parts[1])


def _attn_fwd(q, k, v, ccol, crow, *, dk, npairs, scale, name):
    (qa, qo), (ka, ko), (va, vo) = q, k, v
    s = qa.shape[0]
    wq = 2 * dk
    t = _pick(s, (512, 256, 128))
    nb = s // t
    bias = ccol is not None

    def body(*refs):
        if bias:
            q_ref, k_ref, v_ref, cc_ref, cr_ref, o_ref, lse_ref = refs
        else:
            q_ref, k_ref, v_ref, o_ref, lse_ref = refs
        qi = pl.program_id(1)
        qb = q_ref[...]
        qs = [_head_sel(qb, h, dk) for h in range(2)]

        def scores(j):
            rows = pl.ds(pl.multiple_of(j * t, t), t)
            kb = k_ref[rows, :]
            out = []
            for h in range(2):
                sc = lax.dot_general(qs[h], _head_other(kb, h, dk), _NT, preferred_element_type=F32) * scale
                out.append(sc - cr_ref[h, j] if bias else sc)
            return tuple(out)

        def update(scs, j, state):
            rows = pl.ds(pl.multiple_of(j * t, t), t)
            vb = v_ref[rows, :]
            soft = []
            for h in range(2):
                m, l, _ = state[3 * h:3 * h + 3]
                sc = scs[h]
                m_new = jnp.maximum(m, jnp.max(sc, axis=1, keepdims=True))
                a = jnp.exp(m - m_new)
                p = jnp.exp(sc - m_new)
                soft.append((m_new, a * l + jnp.sum(p, axis=1, keepdims=True), a, p.astype(BF16)))
            new = []
            for h in range(2):
                m_new, l, a, p = soft[h]
                new += [m_new, l, a * state[3 * h + 2] + jnp.dot(p, vb, preferred_element_type=F32)]
            return tuple(new)

        def step(j, carry):
            nxt = scores(j + 1)
            return nxt + update(carry[:2], j, carry[2:])

        one = (jnp.full((t, 1), -jnp.inf, F32), jnp.zeros((t, 1), F32), jnp.zeros((t, LANES), F32))
        carry = lax.fori_loop(0, qi, step, scores(0) + one + one)
        row = lax.broadcasted_iota(jnp.int32, (t, t), 0)
        col = lax.broadcasted_iota(jnp.int32, (t, t), 1)
        diag = tuple(jnp.where(col <= row, sc, -jnp.inf) for sc in carry[:2])
        state = update(diag, qi, carry[2:])
        outs = []
        for h in range(2):
            m, l, acc = state[3 * h:3 * h + 3]
            outs.append(acc / l)
            lse = m + jnp.log(l)
            lse_ref[h] = lse + cc_ref[h] if bias else lse
        o_ref[...] = _pair(outs)

    in_specs = [pl.BlockSpec((t, wq), lambda p, i: (i, qo // wq + p)), pl.BlockSpec((s, wq), lambda p, i: (0, ko // wq + p)),
                pl.BlockSpec((s, LANES), lambda p, i: (0, vo // LANES + p))]
    args = [qa, ka, va]
    if bias:
        in_specs += [pl.BlockSpec((2, t, 1), lambda p, i: (p, i, 0)), pl.BlockSpec((2, nb, 1, t), lambda p, i: (p, 0, 0, 0))]
        args += [ccol, crow.reshape(2 * npairs, nb, 1, t)]
    return pl.pallas_call(
        body, name=name, grid=(npairs, nb), in_specs=in_specs,
        out_specs=[pl.BlockSpec((t, LANES), lambda p, i: (i, p)), pl.BlockSpec((2, t, 1), lambda p, i: (p, i, 0))],
        out_shape=[jax.ShapeDtypeStruct((s, npairs * LANES), F32), jax.ShapeDtypeStruct((2 * npairs, s, 1), F32)],
        compiler_params=_cparams((_PAR, _PAR)),
    )(*args)


def _attn_bwd(q, k, v, o, do, lse, ccol, crow, *, dk, npairs, scale, dq_dtype, name):
    (qa, qo), (ka, ko), (va, vo), (oa, oo), (da, do_o) = q, k, v, o, do
    s = qa.shape[0]
    wq = 2 * dk
    t = _pick(s, (512, 256, 128))
    nb = s // t
    bias = ccol is not None

    def body(*refs):
        if bias:
            (q_ref, k_ref, v_ref, o_ref, do_ref, lse_ref, cc_ref, cr_ref,
             dq_ref, dk_ref, dv_ref, dc_ref, dcc_ref, dq_s, dk_s, dv_s, dc_s) = refs
        else:
            q_ref, k_ref, v_ref, o_ref, do_ref, lse_ref, dq_ref, dk_ref, dv_ref, dq_s, dk_s, dv_s = refs
        ki, qi = pl.program_id(1), pl.program_id(2)

        @pl.when((ki == 0) & (qi == 0))
        def _():
            dq_s[...] = jnp.zeros_like(dq_s)
            if bias:
                dcc_ref[...] = jnp.zeros_like(dcc_ref)

        @pl.when(qi == ki)
        def _():
            dk_s[...] = jnp.zeros_like(dk_s)
            dv_s[...] = jnp.zeros_like(dv_s)
            if bias:
                dc_s[...] = jnp.zeros_like(dc_s)

        @pl.when(qi >= ki)
        def _():
            qb, kb, vb, dob, ob = q_ref[...], k_ref[...], v_ref[...], do_ref[...], o_ref[...]
            rows = pl.ds(pl.multiple_of(qi * t, t), t)
            row = lax.broadcasted_iota(jnp.int32, (t, t), 0) + qi * t
            col = lax.broadcasted_iota(jnp.int32, (t, t), 1) + ki * t
            fed = []
            for h in range(2):
                kh, doh = _head_other(kb, h, dk), _head_sel(dob, h, HALF)
                sc = lax.dot_general(_head_sel(qb, h, dk), kh, _NT, preferred_element_type=F32)
                dp = lax.dot_general(doh, vb, _NT, preferred_element_type=F32)
                fed.append((kh, doh, sc, dp))
            mid = []
            for h in range(2):
                kh, doh, sc, dp = fed[h]
                sc = sc * scale
                if bias:
                    sc = sc + cc_ref[h] - cr_ref[h]
                p = jnp.exp(jnp.where(col <= row, sc, -jnp.inf) - lse_ref[h])
                delta = jnp.sum(doh.astype(F32) * ob, axis=1, keepdims=True)
                ds = p * (dp - delta)
                if bias:
                    dc_s[h] -= jnp.sum(ds, axis=0, keepdims=True)
                    dcc_ref[h, rows, :] += jnp.sum(ds, axis=1, keepdims=True)
                mid.append((kh, p.astype(BF16), (ds * scale).astype(BF16)))
            dq_parts, dk_parts, dv_parts = [], [], []
            for h in range(2):
                kh, pb, dsb = mid[h]
                dv_parts.append(lax.dot_general(pb, dob, _TN, preferred_element_type=F32))
                dk_parts.append(lax.dot_general(dsb, _head_other(qb, h, dk), _TN, preferred_element_type=F32))
                dq_parts.append(jnp.dot(dsb, kh, preferred_element_type=F32))
            dv_s[...] += _pair(dv_parts)
            dk_s[...] += _pair(dk_parts, dk)
            dq_s[rows, :] += _pair(dq_parts, dk)

        @pl.when(qi == nb - 1)
        def _():
            dk_ref[...] = dk_s[...].astype(dk_ref.dtype)
            dv_ref[...] = dv_s[...].astype(dv_ref.dtype)
            if bias:
                dc_ref[...] = dc_s[...]

        @pl.when((ki == nb - 1) & (qi == nb - 1))
        def _():
            dq_ref[...] = dq_s[...].astype(dq_ref.dtype)

    qrow = lambda p, j, i: jnp.maximum(i, j)
    in_specs = [pl.BlockSpec((t, wq), lambda p, j, i: (qrow(p, j, i), qo // wq + p)),
                pl.BlockSpec((t, wq), lambda p, j, i: (j, ko // wq + p)),
                pl.BlockSpec((t, LANES), lambda p, j, i: (j, vo // LANES + p)),
                pl.BlockSpec((t, LANES), lambda p, j, i: (qrow(p, j, i), oo // LANES + p)),
                pl.BlockSpec((t, LANES), lambda p, j, i: (qrow(p, j, i), do_o // LANES + p)),
                pl.BlockSpec((2, t, 1), lambda p, j, i: (p, qrow(p, j, i), 0))]
    args = [qa, ka, va, oa, da, lse]
    out_specs = [pl.BlockSpec((s, wq), lambda p, j, i: (0, p)), pl.BlockSpec((t, wq), lambda p, j, i: (j, p)),
                 pl.BlockSpec((t, LANES), lambda p, j, i: (j, p))]
    out_shape = [jax.ShapeDtypeStruct((s, npairs * wq), dq_dtype), jax.ShapeDtypeStruct((s, npairs * wq), dq_dtype),
                 jax.ShapeDtypeStruct((s, npairs * LANES), BF16)]
    scratch = [pltpu.VMEM((s, wq), F32), pltpu.VMEM((t, wq), F32), pltpu.VMEM((t, LANES), F32)]
    if bias:
        in_specs += [pl.BlockSpec((2, t, 1), lambda p, j, i: (p, qrow(p, j, i), 0)), pl.BlockSpec((2, 1, t), lambda p, j, i: (p, 0, j))]
        args += [ccol, crow]
        out_specs += [pl.BlockSpec((2, 1, t), lambda p, j, i: (p, 0, j)), pl.BlockSpec((2, s, 1), lambda p, j, i: (p, 0, 0))]
        out_shape += [jax.ShapeDtypeStruct((2 * npairs, 1, s), F32), jax.ShapeDtypeStruct((2 * npairs, s, 1), F32)]
        scratch.append(pltpu.VMEM((2, 1, t), F32))
    return pl.pallas_call(
        body, name=name, grid=(npairs, nb, nb), in_specs=in_specs, out_specs=out_specs, out_shape=out_shape,
        scratch_shapes=scratch, compiler_params=_cparams((_PAR, _ARB, _ARB)),
    )(*args)


def _sub_mask(shape, lo, hi):
    sub = lax.broadcasted_iota(jnp.int32, shape, 0)
    return (sub >= lo) & (sub < hi)


def _pair_rows(parts):
    return jnp.where(_sub_mask(parts[0].shape, 0, HALF), parts[0], parts[1])


def _swa_valid(start, t):
    krow = lax.broadcasted_iota(jnp.int32, (t + WINDOW, t), 0)
    qcol = lax.broadcasted_iota(jnp.int32, (t + WINDOW, t), 1)
    diff = qcol - krow + WINDOW
    return (diff >= 0) & (diff < WINDOW) & (krow + start >= WINDOW)


def _swa_tiles(s):
    t = _pick(s, (256, 128))
    return t, _pick(s // t, (4, 2, 1))


def _swa_fwd(q, kp, vp, sink, *, name):
    s = q.shape[0]
    npairs = SWA_HEADS // 2
    t, nt = _swa_tiles(s)
    scale = SWA_DIM ** -0.5

    def body(q_ref, k_ref, v_ref, sk_ref, o_ref, lse_ref):
        sts, vws = [], []
        for u in range(nt):
            start = pl.multiple_of((pl.program_id(1) * nt + u) * t, t)
            kw = k_ref[pl.ds(start, t + WINDOW), :]
            vws.append(v_ref[pl.ds(start, t + WINDOW), :])
            qb = q_ref[u * t:(u + 1) * t, :]
            valid = _swa_valid(start, t)
            for h in range(2):
                st = lax.dot_general(kw, _head_sel(qb, h, HALF), _NT, preferred_element_type=F32) * scale
                sts.append(jnp.where(valid, st, -jnp.inf))
        pts = []
        for u in range(nt):
            for h in range(2):
                st = sts[2 * u + h]
                snk = sk_ref[h][:, 0:1]
                m = jnp.maximum(jnp.max(st, axis=0, keepdims=True), snk)
                e = jnp.exp(st - m)
                l = jnp.sum(e, axis=0, keepdims=True) + jnp.exp(snk - m)
                pts.append((e * (1.0 / l)).astype(BF16))
                lse_ref[h, :, u * t:(u + 1) * t] = m + jnp.log(l)
        for u in range(nt):
            outs = [lax.dot_general(vws[u], pts[2 * u + h], _TN, preferred_element_type=F32) for h in range(2)]
            o_ref[:, u * t:(u + 1) * t] = _pair_rows(outs)

    kvspec = pl.BlockSpec((s + WINDOW, LANES), lambda p, i: (0, 0))
    return pl.pallas_call(
        body, name=name, grid=(npairs, s // (t * nt)),
        in_specs=[pl.BlockSpec((t * nt, LANES), lambda p, i: (i, p)), kvspec, kvspec, pl.BlockSpec((2, 1, LANES), lambda p, i: (p, 0, 0))],
        out_specs=[pl.BlockSpec((LANES, t * nt), lambda p, i: (p, i)), pl.BlockSpec((2, 1, t * nt), lambda p, i: (p, 0, i))],
        out_shape=[jax.ShapeDtypeStruct((npairs * LANES, s), F32), jax.ShapeDtypeStruct((SWA_HEADS, 1, s), F32)],
        compiler_params=_cparams((_PAR, _PAR)),
    )(q, kp, vp, sink)


def _swa_bwd(q, kp, vp, kpt, sink, ot, do, dot, lse, *, name):
    s = q.shape[0]
    npairs = SWA_HEADS // 2
    t, nt = _swa_tiles(s)
    scale = SWA_DIM ** -0.5

    def body(q_ref, k_ref, v_ref, kt_ref, sk_ref, ot_ref, do_ref, dot_ref, lse_ref, dq_ref, dk_ref, dv_ref, dsk_ref):
        pp, i = pl.program_id(0), pl.program_id(1)

        @pl.when((pp == 0) & (i == 0))
        def _():
            dk_ref[...] = jnp.zeros_like(dk_ref)
            dv_ref[...] = jnp.zeros_like(dv_ref)

        @pl.when(i == 0)
        def _():
            dsk_ref[...] = jnp.zeros_like(dsk_ref)

        fed = []
        for u in range(nt):
            cols = slice(u * t, (u + 1) * t)
            start = pl.multiple_of((i * nt + u) * t, t)
            win = pl.ds(start, t + WINDOW)
            qb, dob = q_ref[cols, :], do_ref[cols, :]
            kw, vw = k_ref[win, :], v_ref[win, :]
            valid = _swa_valid(start, t)
            for h in range(2):
                st = lax.dot_general(kw, _head_sel(qb, h, HALF), _NT, preferred_element_type=F32)
                dpt = lax.dot_general(vw, _head_sel(dob, h, HALF), _NT, preferred_element_type=F32)
                fed.append((cols, win, qb, dob, valid, st, dpt))
        mid = []
        for u in range(nt):
            cols = fed[2 * u][0]
            prod = dot_ref[:, cols].astype(F32) * ot_ref[:, cols]
            for h in range(2):
                valid, st, dpt = fed[2 * u + h][4:]
                lse_b = lse_ref[h, :, cols]
                pt = jnp.exp(jnp.where(valid, st * scale, -jnp.inf) - lse_b)
                delta = jnp.sum(jnp.where(_sub_mask(prod.shape, h * HALF, (h + 1) * HALF), prod, 0.0), axis=0, keepdims=True)
                dst = pt * (dpt - delta)
                psink = jnp.exp(sk_ref[h][:, 0:1] - lse_b)
                dsk_ref[h] += jnp.broadcast_to(-jnp.sum(psink * delta, axis=1, keepdims=True), (1, LANES))
                mid.append((pt.astype(BF16), (dst * scale).astype(BF16)))
        for u in range(nt):
            cols, win, qb, dob = fed[2 * u][:4]
            ktw = kt_ref[:, win]
            dq_parts = [jnp.dot(ktw, mid[2 * u + h][1], preferred_element_type=F32) for h in range(2)]
            dk_parts = [jnp.dot(mid[2 * u + h][1], qb, preferred_element_type=F32) for h in range(2)]
            dv_parts = [jnp.dot(mid[2 * u + h][0], dob, preferred_element_type=F32) for h in range(2)]
            dq_ref[:, cols] = _pair_rows(dq_parts)
            dk_ref[win, :] += _pair(dk_parts)
            dv_ref[win, :] += _pair(dv_parts)

    tile = pl.BlockSpec((t * nt, LANES), lambda p, i: (i, p))
    ttile = pl.BlockSpec((LANES, t * nt), lambda p, i: (p, i))
    kvspec = pl.BlockSpec((s + WINDOW, LANES), lambda p, i: (0, 0))
    ktspec = pl.BlockSpec((LANES, s + WINDOW), lambda p, i: (0, 0))
    skspec = pl.BlockSpec((2, 1, LANES), lambda p, i: (p, 0, 0))
    return pl.pallas_call(
        body, name=name, grid=(npairs, s // (t * nt)),
        in_specs=[tile, kvspec, kvspec, ktspec, skspec, ttile, tile, ttile, pl.BlockSpec((2, 1, t * nt), lambda p, i: (p, 0, i))],
        out_specs=[ttile, kvspec, kvspec, skspec],
        out_shape=[jax.ShapeDtypeStruct((npairs * LANES, s), F32), jax.ShapeDtypeStruct((s + WINDOW, LANES), F32),
                   jax.ShapeDtypeStruct((s + WINDOW, LANES), F32), jax.ShapeDtypeStruct((SWA_HEADS, 1, LANES), F32)],
        compiler_params=_cparams((_ARB, _ARB)),
    )(q, kp, vp, kpt, sink, ot, do, dot, lse)


def _odd_rope(h, cos, sin, *, name):
    s = h.shape[0]
    tb = _pick(s, (512, 256, 128))
    nq = SWA_HEADS * SWA_DIM

    def body(q_ref, kv_ref, cos_ref, sin_ref, qo_ref, ko_ref, vo_ref):
        cs, sn = cos_ref[...], sin_ref[...]
        for j in range(nq // LANES):
            sl = slice(j * LANES, (j + 1) * LANES)
            qo_ref[:, sl] = _rope_tile(q_ref[:, sl], cs, sn, SWA_DIM // 2).astype(BF16)
        ko_ref[...] = _rope_tile(kv_ref[:, :LANES], cs, sn, SWA_DIM // 2).astype(BF16)
        vo_ref[...] = kv_ref[:, LANES:].astype(BF16)

    rows = lambda w, c=0: pl.BlockSpec((tb, w), lambda i: (i, c))
    return pl.pallas_call(
        body, name=name, grid=(s // tb,),
        in_specs=[rows(nq, O_Q // nq), rows(2 * LANES, O_K // (2 * LANES)), rows(LANES), rows(LANES)],
        out_specs=[rows(nq), rows(LANES), rows(LANES)],
        out_shape=[jax.ShapeDtypeStruct((s, nq), BF16), jax.ShapeDtypeStruct((s, LANES), BF16), jax.ShapeDtypeStruct((s, LANES), BF16)],
        compiler_params=_cparams((_PAR,)),
    )(h, h, cos, sin)


def _odd_rope_bwd(dq, dk, cos, sin, *, name):
    s = dq.shape[0]
    tb = _pick(s, (512, 256, 128))
    nq = SWA_HEADS * SWA_DIM

    def body(dq_ref, dk_ref, cos_ref, sin_ref, qo_ref, ko_ref):
        cs, sn = cos_ref[...], -sin_ref[...]
        for j in range(nq // LANES):
            sl = slice(j * LANES, (j + 1) * LANES)
            qo_ref[:, sl] = _rope_tile(dq_ref[:, sl], cs, sn, SWA_DIM // 2).astype(BF16)
        ko_ref[...] = _rope_tile(dk_ref[...], cs, sn, SWA_DIM // 2).astype(BF16)

    rows = lambda w: pl.BlockSpec((tb, w), lambda i: (i, 0))
    return pl.pallas_call(
        body, name=name, grid=(s // tb,), in_specs=[rows(nq), rows(LANES), rows(LANES), rows(LANES)],
        out_specs=[rows(nq), rows(LANES)],
        out_shape=[jax.ShapeDtypeStruct((s, nq), BF16), jax.ShapeDtypeStruct((s, LANES), BF16)],
        compiler_params=_cparams((_PAR,)),
    )(dq, dk, cos, sin)


def _out_fwd(o_parts, h, x, w_out, g, b, *, name):
    s = x.shape[0]
    tb = _pick(s, (256, 128))
    n_o = len(o_parts)

    def body(*refs):
        o_refs = refs[:n_o]
        gate_ref, x_ref, w_ref, g_ref, b_ref, xn_ref, xb_ref, z_ref, xh_ref, rs_ref = refs[n_o:]
        gate = gate_ref[...]
        o = o_refs[0][...] if n_o == 1 else jnp.concatenate([r[...] for r in o_refs], axis=1)
        z = (o * (gate * _sigmoid(gate))).astype(BF16)
        z_ref[...] = z
        r = ALPHA * x_ref[...] + jnp.dot(z, w_ref[...], preferred_element_type=F32)
        mu = jnp.mean(r, axis=-1, keepdims=True)
        rc = r - mu
        rstd = lax.rsqrt(jnp.mean(rc * rc, axis=-1, keepdims=True) + LN_EPS)
        xh = rc * rstd
        xn = xh * g_ref[...] + b_ref[...]
        xh_ref[...] = xh
        rs_ref[...] = rstd
        xn_ref[...] = xn
        xb_ref[...] = xn.astype(BF16)

    rows = lambda w: pl.BlockSpec((tb, w), lambda i: (i, 0))
    full = lambda a: pl.BlockSpec(a.shape, lambda i: (0,) * a.ndim)
    return pl.pallas_call(
        body, name=name, grid=(s // tb,),
        in_specs=[rows(a.shape[1]) for a in o_parts] + [rows(D_MODEL), rows(D_MODEL), full(w_out), full(g), full(b)],
        out_specs=[rows(D_MODEL), rows(D_MODEL), rows(D_MODEL), rows(D_MODEL), rows(1)],
        out_shape=[jax.ShapeDtypeStruct((s, D_MODEL), F32), jax.ShapeDtypeStruct((s, D_MODEL), BF16),
                   jax.ShapeDtypeStruct((s, D_MODEL), BF16), jax.ShapeDtypeStruct((s, D_MODEL), F32),
                   jax.ShapeDtypeStruct((s, 1), F32)],
        compiler_params=_cparams((_PAR,)),
    )(*o_parts, h, x, w_out, g, b)


def _out_bwd(dxn, xh, rstd, g, w_out_t, o_parts, h, *, name):
    s = dxn.shape[0]
    tb = _pick(s, (256, 128))
    n_o = len(o_parts)

    def body(*refs):
        dxn_ref, xh_ref, rs_ref, g_ref, wt_ref = refs[:5]
        o_refs = refs[5:5 + n_o]
        gate_ref, dr_ref, dy_ref, do_ref, dgate_ref, dg_ref, db_ref = refs[5 + n_o:]

        @pl.when(pl.program_id(0) == 0)
        def _():
            dg_ref[...] = jnp.zeros_like(dg_ref)
            db_ref[...] = jnp.zeros_like(db_ref)

        dxn_b, xh_b = dxn_ref[...], xh_ref[...]
        dg_ref[...] += jnp.sum(dxn_b * xh_b, axis=0, keepdims=True)
        db_ref[...] += jnp.sum(dxn_b, axis=0, keepdims=True)
        dxh = dxn_b * g_ref[...]
        dr = rs_ref[...] * (dxh - jnp.mean(dxh, axis=-1, keepdims=True) - xh_b * jnp.mean(dxh * xh_b, axis=-1, keepdims=True))
        dr_ref[...] = dr
        dy = dr.astype(BF16)
        dy_ref[...] = dy
        dz = jnp.dot(dy, wt_ref[...], preferred_element_type=F32)
        gate = gate_ref[...]
        sg = _sigmoid(gate)
        o = o_refs[0][...] if n_o == 1 else jnp.concatenate([r[...] for r in o_refs], axis=1)
        do_ref[...] = (dz * (gate * sg)).astype(BF16)
        dgate_ref[...] = (dz * o * (sg * (1.0 + gate * (1.0 - sg)))).astype(BF16)

    rows = lambda w: pl.BlockSpec((tb, w), lambda i: (i, 0))
    full = lambda a: pl.BlockSpec(a.shape, lambda i: (0,) * a.ndim)
    return pl.pallas_call(
        body, name=name, grid=(s // tb,),
        in_specs=[rows(D_MODEL), rows(D_MODEL), rows(1), full(g), full(w_out_t)] + [rows(a.shape[1]) for a in o_parts] + [rows(D_MODEL)],
        out_specs=[rows(D_MODEL), rows(D_MODEL), rows(D_MODEL), rows(D_MODEL), full(g), full(g)],
        out_shape=[jax.ShapeDtypeStruct((s, D_MODEL), F32), jax.ShapeDtypeStruct((s, D_MODEL), BF16),
                   jax.ShapeDtypeStruct((s, D_MODEL), BF16), jax.ShapeDtypeStruct((s, D_MODEL), BF16),
                   jax.ShapeDtypeStruct(g.shape, F32), jax.ShapeDtypeStruct(g.shape, F32)],
        compiler_params=_cparams((_ARB,)),
    )(dxn, xh, rstd, g, w_out_t, *o_parts, h)


def _loss_grad(y, target, *, name):
    s, d = y.shape
    tb = _pick(s, (512, 256, 128))

    def body(y_ref, t_ref, dy_ref, l_ref):
        @pl.when(pl.program_id(0) == 0)
        def _():
            l_ref[...] = jnp.zeros_like(l_ref)

        err = y_ref[...] - t_ref[...]
        dy_ref[...] = err * (1.0 / d)
        per_tok = jnp.mean(err * err, axis=-1, keepdims=True)
        l_ref[...] += 0.5 * jnp.sum(per_tok, axis=0, keepdims=True)

    rows = pl.BlockSpec((tb, d), lambda i: (i, 0))
    return pl.pallas_call(
        body, name=name, grid=(s // tb,), in_specs=[rows, rows],
        out_specs=[rows, pl.BlockSpec((8, LANES), lambda i: (0, 0))],
        out_shape=[jax.ShapeDtypeStruct((s, d), F32), jax.ShapeDtypeStruct((8, LANES), F32)],
        compiler_params=_cparams((_ARB,)),
    )(y, target)


def _adamw(w, g, m, v, *, name):
    shape = w.shape
    w3, g3, m3, v3 = (a.reshape((1,) * (3 - a.ndim) + a.shape) for a in (w, g, m, v))
    a0, a1, a2 = w3.shape
    tb = _pick(a1, (256, 128)) if a1 % 8 == 0 else a1
    c1 = 1.0 - ADAM_B1 ** ADAM_STEP
    c2 = 1.0 - ADAM_B2 ** ADAM_STEP

    def body(w_ref, g_ref, m_ref, v_ref, d_ref, mo_ref, vo_ref):
        gg = g_ref[...]
        mn = ADAM_B1 * m_ref[...] + (1.0 - ADAM_B1) * gg
        vn = ADAM_B2 * v_ref[...] + (1.0 - ADAM_B2) * (gg * gg)
        mo_ref[...] = mn
        vo_ref[...] = vn
        d_ref[...] = -ADAM_LR * ((mn / c1) / (jnp.sqrt(vn / c2) + ADAM_EPS) + ADAM_WD * w_ref[...])

    spec = pl.BlockSpec((1, tb, a2), lambda i, j: (i, j, 0))
    outs = pl.pallas_call(
        body, name=name, grid=(a0, a1 // tb), in_specs=[spec] * 4, out_specs=[spec] * 3,
        out_shape=[jax.ShapeDtypeStruct(w3.shape, F32)] * 3, compiler_params=_cparams((_PAR, _PAR)),
    )(w3, g3, m3, v3)
    return tuple(a.reshape(shape) for a in outs)


def _place():
    x, y, c = lax.axis_index("x"), lax.axis_index("y"), lax.axis_index("c")
    return x, y, c, [(1 - x, y), (x, 1 - y), (1 - x, 1 - y)]


_ANY = pl.BlockSpec(memory_space=pl.ANY)


def _sem_pairs(n):
    return [pltpu.SemaphoreType.DMA((n,)), pltpu.SemaphoreType.DMA((n,))]


def _gather_chip_shards(tensors, *, name):
    nt = len(tensors)
    per = 7

    def body(*refs):
        srcs, outs = refs[:nt], refs[nt:2 * nt]
        send_sems, recv_sems = refs[2 * nt:]
        x, y, c, chips = _place()
        me = 2 * x + y
        sib = (x, y, 1 - c)

        def half(t, chip, hc):
            rh = tensors[t].shape[1] // 2
            return outs[t].at[chip, :, pl.ds(hc * rh, rh), :]

        def copy(t, kk, s_ref, d_ref, to):
            return pltpu.make_async_remote_copy(src_ref=s_ref, dst_ref=d_ref, send_sem=send_sems.at[per * t + kk],
                                                recv_sem=recv_sems.at[per * t + kk], device_id=to, device_id_type=MESH)

        sent = []
        for j, (cx, cy) in enumerate(chips):
            for t in range(nt):
                rh = tensors[t].shape[1] // 2
                cp = copy(t, j, srcs[t].at[:, pl.ds(c * rh, rh), :], half(t, me, c), (cx, cy, c))
                cp.start()
                sent.append(cp)
        own = [copy(t, 6, srcs[t], outs[t].at[me], sib) for t in range(nt)]
        for cp in own:
            cp.start()
        for j, (cx, cy) in enumerate(chips):
            for t in range(nt):
                landed = half(t, 2 * cx + cy, c)
                copy(t, j, landed, landed, (cx, cy, c)).wait_recv()
                fw = copy(t, 3 + j, landed, landed, sib)
                fw.start()
                sent.append(fw)
        for j, (cx, cy) in enumerate(chips):
            for t in range(nt):
                theirs = half(t, 2 * cx + cy, 1 - c)
                copy(t, 3 + j, theirs, theirs, sib).wait_recv()
        for cp in own:
            cp.wait()
        for cp in sent:
            cp.wait_send()

    return pl.pallas_call(
        body, name=name, in_specs=[_ANY] * nt, out_specs=[_ANY] * nt,
        out_shape=[jax.ShapeDtypeStruct((N_CHIPS,) + a.shape, a.dtype) for a in tensors],
        scratch_shapes=_sem_pairs(per * nt),
    )(*tensors)


def _swap_halves(gs, *, name):
    nt = len(gs)

    def body(*refs):
        srcs, outs = refs[:nt], refs[nt:2 * nt]
        send_sems, recv_sems = refs[2 * nt:]
        x, y, c, _ = _place()
        cps = [pltpu.make_async_remote_copy(src_ref=srcs[t].at[:, 1 - c], dst_ref=outs[t], send_sem=send_sems.at[t],
                                            recv_sem=recv_sems.at[t], device_id=(x, y, 1 - c), device_id_type=MESH)
               for t in range(nt)]
        for cp in cps:
            cp.start()
        for cp in cps:
            cp.wait()

    return pl.pallas_call(
        body, name=name, in_specs=[_ANY] * nt, out_specs=[_ANY] * nt,
        out_shape=[jax.ShapeDtypeStruct((N_CHIPS,) + g.shape[2:], g.dtype) for g in gs], scratch_shapes=_sem_pairs(nt),
    )(*gs)


def _scatter_to_chips(blocks, *, name):
    nt = len(blocks)

    def body(*refs):
        srcs, outs = refs[:nt], refs[nt:2 * nt]
        send_sems, recv_sems = refs[2 * nt:]
        x, y, c, chips = _place()
        cps = [pltpu.make_async_remote_copy(src_ref=srcs[t].at[2 * cx + cy], dst_ref=outs[t].at[j],
                                            send_sem=send_sems.at[3 * t + j], recv_sem=recv_sems.at[3 * t + j],
                                            device_id=(cx, cy, c), device_id_type=MESH)
               for j, (cx, cy) in enumerate(chips) for t in range(nt)]
        for cp in cps:
            cp.start()
        for cp in cps:
            cp.wait()

    return pl.pallas_call(
        body, name=name, in_specs=[_ANY] * nt, out_specs=[_ANY] * nt,
        out_shape=[jax.ShapeDtypeStruct((3,) + a.shape[1:], a.dtype) for a in blocks], scratch_shapes=_sem_pairs(3 * nt),
    )(*blocks)


def _join_halves(bufs, *, name):
    nt = len(bufs)

    def body(*refs):
        srcs, outs = refs[:nt], refs[nt:2 * nt]
        send_sems, recv_sems = refs[2 * nt:]
        x, y, c, _ = _place()
        cps = []
        for t in range(nt):
            rh = bufs[t].shape[1] // 2
            rows = pl.ds(c * rh, rh)
            cps.append(pltpu.make_async_remote_copy(src_ref=srcs[t].at[:, rows, :], dst_ref=outs[t].at[:, rows, :],
                                                    send_sem=send_sems.at[t], recv_sem=recv_sems.at[t],
                                                    device_id=(x, y, 1 - c), device_id_type=MESH))
        for cp in cps:
            cp.start()
        for cp in cps:
            cp.wait()

    return pl.pallas_call(
        body, name=name, in_specs=[_ANY] * nt, out_specs=[_ANY] * nt,
        out_shape=[jax.ShapeDtypeStruct(b.shape, b.dtype) for b in bufs], scratch_shapes=_sem_pairs(nt),
        input_output_aliases={t: t for t in range(nt)},
    )(*bufs)


def _add_sibling(g, recv, cidx, chip_idx, *, name):
    _, _, na, rh, cdim = g.shape
    tb = _pick(rh, (256, 128, 64, 32, 16))

    def body(c_ref, k_ref, g_ref, r_ref, s_ref, o_ref):
        tot = g_ref[0, 0] + r_ref[0]
        s_ref[0] = tot.astype(BF16)

        @pl.when(pl.program_id(2) == k_ref[0])
        def _():
            o_ref[...] = tot

    return pl.pallas_call(
        body, name=name,
        grid_spec=pltpu.PrefetchScalarGridSpec(
            num_scalar_prefetch=2, grid=(na, rh // tb, N_CHIPS),
            in_specs=[pl.BlockSpec((1, 1, 1, tb, cdim), lambda a, i, k, c_ref, k_ref: (k, c_ref[0], a, i, 0)),
                      pl.BlockSpec((1, 1, tb, cdim), lambda a, i, k, c_ref, k_ref: (k, a, i, 0))],
            out_specs=[pl.BlockSpec((1, 1, tb, cdim), lambda a, i, k, c_ref, k_ref: (k, a, i, 0)),
                       pl.BlockSpec((1, tb, cdim), lambda a, i, k, c_ref, k_ref: (a, i, 0))]),
        out_shape=[jax.ShapeDtypeStruct((N_CHIPS, na, rh, cdim), BF16), jax.ShapeDtypeStruct((na, rh, cdim), F32)],
        compiler_params=_cparams((_PAR, _PAR, _ARB)),
    )(cidx, chip_idx, g, recv)


def _add_chips(own, recv, cidx, *, name):
    na, rh, cdim = own.shape
    tb = _pick(rh, (256, 128, 64, 32, 16))
    nblk = rh // tb

    def body(c_ref, a_ref, r0_ref, r1_ref, r2_ref, o_ref):
        o_ref[...] = ((a_ref[...] + r0_ref[0].astype(F32)) + r1_ref[0].astype(F32)) + r2_ref[0].astype(F32)

    slot = lambda j: pl.BlockSpec((1, 1, tb, cdim), lambda a, i, c_ref: (j, a, i, 0))
    return pl.pallas_call(
        body, name=name,
        grid_spec=pltpu.PrefetchScalarGridSpec(
            num_scalar_prefetch=1, grid=(na, nblk),
            in_specs=[pl.BlockSpec((1, tb, cdim), lambda a, i, c_ref: (a, i, 0)), slot(0), slot(1), slot(2)],
            out_specs=pl.BlockSpec((1, tb, cdim), lambda a, i, c_ref: (a, c_ref[0] * nblk + i, 0))),
        out_shape=jax.ShapeDtypeStruct((na, 2 * rh, cdim), F32), compiler_params=_cparams((_PAR, _PAR)),
    )(cidx, own, recv, recv, recv)


def _all_reduce_small(v, *, name):
    r, cdim = v.shape

    def body(v_ref, o_ref, buf, send_sems, recv_sems):
        x, y, c, _ = _place()
        me = 4 * x + 2 * y + c
        buf[me] = v_ref[...]
        cps = []
        for p in range(1, N_DEV):
            to = (1 - x if p & 4 else x, 1 - y if p & 2 else y, 1 - c if p & 1 else c)
            cp = pltpu.make_async_remote_copy(src_ref=v_ref, dst_ref=buf.at[me], send_sem=send_sems.at[p - 1],
                                              recv_sem=recv_sems.at[p - 1], device_id=to, device_id_type=MESH)
            cp.start()
            cps.append(cp)
        for p in range(1, N_DEV):
            frm = (4 * x + 2 * y + c) ^ p
            pltpu.make_async_remote_copy(src_ref=v_ref, dst_ref=buf.at[frm], send_sem=send_sems.at[p - 1],
                                         recv_sem=recv_sems.at[p - 1], device_id=(x, y, c), device_id_type=MESH).wait_recv()
        for cp in cps:
            cp.wait_send()
        tot = buf[0]
        for i in range(1, N_DEV):
            tot = tot + buf[i]
        o_ref[...] = tot

    vm = pl.BlockSpec(memory_space=pltpu.VMEM)
    return pl.pallas_call(
        body, name=name, in_specs=[vm], out_specs=vm, out_shape=jax.ShapeDtypeStruct((r, cdim), F32),
        scratch_shapes=[pltpu.VMEM((N_DEV, r, cdim), F32), pltpu.SemaphoreType.DMA((N_DEV - 1,)), pltpu.SemaphoreType.DMA((N_DEV - 1,))],
    )(v)


_SHARDED = ("even_w_in", "even_w_uq", "even_w_ukv", "even_w_out", "odd_w_in", "odd_w_out")
_COL_SHARDED = ("even_w_in", "even_w_uq", "even_w_ukv", "odd_w_in")
PACK_COLS = 1024


def _unshard(name, stacked):
    n, a, b, cc = stacked.shape
    if name in _COL_SHARDED:
        return stacked.transpose(1, 2, 0, 3).reshape(a, b, n * cc)
    return stacked.transpose(1, 0, 2, 3).reshape(a, n * b, cc)


def _chip_halves(name, full):
    a, b, cc = full.shape
    if name in _COL_SHARDED:
        return full.reshape(a, 2, b // 2, N_CHIPS, cc // N_CHIPS).transpose(3, 1, 0, 2, 4)
    return full.reshape(a, N_CHIPS, 2, b // (2 * N_CHIPS), cc).transpose(1, 2, 0, 3, 4)


def _rope_tables(s):
    pos = jnp.arange(s, dtype=F32)

    def ang(d):
        inv = ROPE_THETA ** (-jnp.arange(0, d, 2, dtype=F32) / d)
        a = pos[:, None] * inv[None, :]
        return jnp.cos(a), jnp.sin(a)

    c16, s16 = ang(MLA_ROPE)
    one, zero = jnp.ones((s, KPE_LANE), F32), jnp.zeros((s, KPE_LANE), F32)
    cos_m = jnp.concatenate([one, c16, c16, one[:, :32]], axis=1)
    sin_m = jnp.concatenate([zero, -s16, s16, zero[:, :32]], axis=1)
    c32, s32 = ang(SWA_DIM)
    cos_s = jnp.concatenate([c32, c32, c32, c32], axis=1)
    sin_s = jnp.concatenate([-s32, s32, -s32, s32], axis=1)
    return cos_m, sin_m, cos_s, sin_s


def _even_weights(w_in, w_uq, w_ukv):
    zeros = lambda n: jnp.zeros((D_MODEL, n), w_in.dtype)
    wcq, wckv, wkpe = w_in[:, 0:256], w_in[:, 256:384], w_in[:, 384:416]
    wfq, wfk, wfv = w_in[:, 416:928], w_in[:, 928:1440], w_in[:, 1440:1952]
    wfl, wg = w_in[:, 1952:1960], w_in[:, 1960:2984]
    misc = jnp.concatenate([zeros(KPE_LANE), wkpe, wfl, zeros(LANES - FL_LANE - FOX_HEADS)], axis=1)
    w_in_p = jnp.concatenate([wg, wfq, wfk, wfv, wcq, wckv, misc], axis=1)
    uq = w_uq.reshape(MLA_Q_RANK, MLA_HEADS, MLA_NOPE + MLA_ROPE)
    uq_p = jnp.pad(uq, ((0, 0), (0, 0), (0, HEAD_PAD - MLA_NOPE - MLA_ROPE))).reshape(MLA_Q_RANK, MLA_HEADS * HEAD_PAD)
    ukv = w_ukv.reshape(MLA_KV_RANK, MLA_HEADS, MLA_NOPE + MLA_V)
    uk_p = jnp.pad(ukv[..., :MLA_NOPE], ((0, 0), (0, 0), (0, HEAD_PAD - MLA_NOPE))).reshape(MLA_KV_RANK, MLA_HEADS * HEAD_PAD)
    uv = ukv[..., MLA_NOPE:].reshape(MLA_KV_RANK, MLA_HEADS * MLA_V)
    ukv_p = jnp.concatenate([uk_p, uv], axis=1)
    return w_in_p, w_in_p.T, uq_p, uq_p.T, ukv_p, ukv_p.T


def _even_weight_grads(dw_in_p, duq_p, dukv_p):
    g = dw_in_p
    gate, fq, fk, fv = g[:, E_GATE:E_FQ], g[:, E_FQ:E_FK], g[:, E_FK:E_FV], g[:, E_FV:E_SMALL]
    cq, ckv, misc = g[:, E_SMALL:E_SMALL + 256], g[:, E_SMALL + 256:E_SMALL + 384], g[:, E_SMALL + 384:]
    dw_in = jnp.concatenate([cq, ckv, misc[:, KPE_LANE:KPE_LANE + MLA_ROPE], fq, fk, fv, misc[:, FL_LANE:FL_LANE + FOX_HEADS], gate], axis=1)
    duq = duq_p.reshape(MLA_Q_RANK, MLA_HEADS, HEAD_PAD)[..., :MLA_NOPE + MLA_ROPE].reshape(MLA_Q_RANK, -1)
    nq = MLA_HEADS * HEAD_PAD
    dk = dukv_p[:, :nq].reshape(MLA_KV_RANK, MLA_HEADS, HEAD_PAD)[..., :MLA_NOPE]
    dvv = dukv_p[:, nq:].reshape(MLA_KV_RANK, MLA_HEADS, MLA_V)
    dukv = jnp.concatenate([dk, dvv], axis=-1).reshape(MLA_KV_RANK, -1)
    return dw_in, duq, dukv


def _interleave(w, forward):
    a, b = (SWA_KV_HEADS, SWA_HEADS // SWA_KV_HEADS) if forward else (SWA_HEADS // SWA_KV_HEADS, SWA_KV_HEADS)
    return w.reshape(w.shape[0], a, b, -1).transpose(0, 2, 1, 3).reshape(w.shape[0], -1)


def _odd_weights(w_in, w_out):
    q, k, v, gate = w_in[:, 0:1024], w_in[:, 1024:1152], w_in[:, 1152:1280], w_in[:, 1280:2304]
    w_p = jnp.concatenate([_interleave(gate, True), _interleave(q, True), k, v], axis=1)
    w_out_p = _interleave(w_out.T, True).T
    return w_p, w_p.T, w_out_p, w_out_p.T


def _odd_weight_grads(g, dw_out_p):
    dw_in = jnp.concatenate([_interleave(g[:, O_Q:O_K], False), g[:, O_K:O_V], g[:, O_V:], _interleave(g[:, O_GATE:O_Q], False)], axis=1)
    return dw_in, _interleave(dw_out_p.T, False).T


def kernel(x, even_w_in, even_q_norm, even_w_uq, even_kv_norm, even_w_ukv, even_b_f, even_w_out, even_ln_g, even_ln_b, odd_w_in, odd_sinks, odd_w_out, odd_ln_g, odd_ln_b, loss_target, m_even_w_in, m_even_q_norm, m_even_w_uq, m_even_kv_norm, m_even_w_ukv, m_even_b_f, m_even_w_out, m_even_ln_g, m_even_ln_b, m_odd_w_in, m_odd_sinks, m_odd_w_out, m_odd_ln_g, m_odd_ln_b, v_even_w_in, v_even_q_norm, v_even_w_uq, v_even_kv_norm, v_even_w_ukv, v_even_b_f, v_even_w_out, v_even_ln_g, v_even_ln_b, v_odd_w_in, v_odd_sinks, v_odd_w_out, v_odd_ln_g, v_odd_ln_b):
    weights = dict(even_w_in=even_w_in, even_q_norm=even_q_norm, even_w_uq=even_w_uq, even_kv_norm=even_kv_norm,
                   even_w_ukv=even_w_ukv, even_b_f=even_b_f, even_w_out=even_w_out, even_ln_g=even_ln_g, even_ln_b=even_ln_b,
                   odd_w_in=odd_w_in, odd_sinks=odd_sinks, odd_w_out=odd_w_out, odd_ln_g=odd_ln_g, odd_ln_b=odd_ln_b)
    mom_m = dict(even_w_in=m_even_w_in, even_q_norm=m_even_q_norm, even_w_uq=m_even_w_uq, even_kv_norm=m_even_kv_norm,
                 even_w_ukv=m_even_w_ukv, even_b_f=m_even_b_f, even_w_out=m_even_w_out, even_ln_g=m_even_ln_g, even_ln_b=m_even_ln_b,
                 odd_w_in=m_odd_w_in, odd_sinks=m_odd_sinks, odd_w_out=m_odd_w_out, odd_ln_g=m_odd_ln_g, odd_ln_b=m_odd_ln_b)
    mom_v = dict(even_w_in=v_even_w_in, even_q_norm=v_even_q_norm, even_w_uq=v_even_w_uq, even_kv_norm=v_even_kv_norm,
                 even_w_ukv=v_even_w_ukv, even_b_f=v_even_b_f, even_w_out=v_even_w_out, even_ln_g=v_even_ln_g, even_ln_b=v_even_ln_b,
                 odd_w_in=v_odd_w_in, odd_sinks=v_odd_sinks, odd_w_out=v_odd_w_out, odd_ln_g=v_odd_ln_g, odd_ln_b=v_odd_ln_b)
    names = list(weights)
    xl = x[0]
    tgt = loss_target[0]
    s = xl.shape[0]
    ax, ay, ac = lax.axis_index("x"), lax.axis_index("y"), lax.axis_index("c")
    chip = 2 * ax + ay
    c_idx = ac.astype(jnp.int32).reshape(1)
    chip_idx = chip.astype(jnp.int32).reshape(1)

    ln_odd = jnp.pad(jnp.concatenate([odd_ln_g, odd_ln_b]), ((0, 12), (0, 0)))[None]
    gathered = _gather_chip_shards([weights[n].astype(BF16) for n in _SHARDED] + [ln_odd], name="gather_weights")
    full = {n: _unshard(n, gathered[i]) for i, n in enumerate(_SHARDED)}
    ln_all = gathered[-1][:, 0]
    odd_g_full = ln_all[:, 0:2].transpose(1, 0, 2).reshape(2, D_MODEL)
    odd_b_full = ln_all[:, 2:4].transpose(1, 0, 2).reshape(2, D_MODEL)

    cos_m, sin_m, cos_s, sin_s = _rope_tables(s)
    bf_tiles = jnp.pad(even_b_f, ((0, 0), (FL_LANE, LANES - FL_LANE - FOX_HEADS)))
    sink_tiles = jnp.broadcast_to(_interleave(odd_sinks, True)[:, :, None, None], (2, SWA_HEADS, 1, LANES))
    mla_scale = (MLA_NOPE + MLA_ROPE) ** -0.5
    fox_scale = FOX_DIM ** -0.5
    mla_pairs, fox_pairs = MLA_HEADS // 2, FOX_HEADS // 2

    saved = []
    x_f, x_b = xl, xl.astype(BF16)
    for layer in range(DEPTH):
        j = layer // 2
        ln = f"L{layer}"
        if layer % 2 == 0:
            w_in_p, w_in_t, uq_p, uq_t, ukv_p, ukv_t = _even_weights(full["even_w_in"][j], full["even_w_uq"][j], full["even_w_ukv"][j])
            w_out, w_out_t = full["even_w_out"][j], full["even_w_out"][j].T
            qg, kg, bft = even_q_norm[j][None], even_kv_norm[j][None], bf_tiles[j][None]
            h, hb = _mm(x_b, w_in_p, out_dtype=F32, name=ln + "_in", also_bf16=True)
            q, k, v, qn, kvn, logf = _even_mid_fwd(h, qg, kg, bft, uq_p, ukv_p, cos_m, sin_m, name=ln + "_mid")
            cum = _cumsum(logf, reverse=False, name=ln + "_cum")[:, FL_LANE:FL_LANE + FOX_HEADS].T
            ccol, crow = cum[:, :, None], cum[:, None, :]
            o_mla, lse_mla = _attn_fwd((q, 0), (k, 0), (v, 0), None, None, dk=HEAD_PAD, npairs=mla_pairs, scale=mla_scale, name=ln + "_mla")
            o_fox, lse_fox = _attn_fwd((hb, E_FQ), (hb, E_FK), (hb, E_FV), ccol, crow, dk=FOX_DIM, npairs=fox_pairs,
                                       scale=fox_scale, name=ln + "_fox")
            o_parts = [o_mla, o_fox]
            g_ln, b_ln = even_ln_g[j][None], even_ln_b[j][None]
            x_n, x_nb, z, xh, rstd = _out_fwd(o_parts, h, x_f, w_out, g_ln, b_ln, name=ln + "_out")
            saved.append(dict(h=h, hb=hb, x_b=x_b, qn=qn, kvn=kvn, q=q, k=k, v=v, ccol=ccol, crow=crow,
                              o_mla=o_mla, o_fox=o_fox, lse_mla=lse_mla, lse_fox=lse_fox, o_parts=o_parts, z=z, xh=xh, rstd=rstd,
                              w_in_t=w_in_t, uq_t=uq_t, ukv_t=ukv_t, w_out_t=w_out_t, qg=qg, kg=kg, bft=bft, g_ln=g_ln))
        else:
            w_in_p, w_in_t, w_out, w_out_t = _odd_weights(full["odd_w_in"][j], full["odd_w_out"][j])
            h = _mm(x_b, w_in_p, out_dtype=F32, name=ln + "_in")
            q, k, v = _odd_rope(h, cos_s, sin_s, name=ln + "_rope")
            kp = jnp.pad(k, ((WINDOW, 0), (0, 0)))
            vp = jnp.pad(v, ((WINDOW, 0), (0, 0)))
            o_t, lse = _swa_fwd(q, kp, vp, sink_tiles[j], name=ln + "_swa")
            o = o_t.T
            g_ln, b_ln = odd_g_full[j][None], odd_b_full[j][None]
            x_n, x_nb, z, xh, rstd = _out_fwd([o], h, x_f, w_out, g_ln, b_ln, name=ln + "_out")
            saved.append(dict(h=h, x_b=x_b, q=q, kp=kp, vp=vp, lse=lse, o_t=o_t, o_parts=[o], z=z, xh=xh, rstd=rstd,
                              w_in_t=w_in_t, w_out_t=w_out_t, g_ln=g_ln))
        x_f, x_b = x_n, x_nb

    dxn, loss_tile = _loss_grad(x_f, tgt, name="loss")

    grads = {n: [None, None] for n in names}
    for layer in reversed(range(DEPTH)):
        j = layer // 2
        ln = f"L{layer}"
        sv = saved[layer]
        dr, dy, do, dgate, dg_ln, db_ln = _out_bwd(dxn, sv["xh"], sv["rstd"], sv["g_ln"], sv["w_out_t"], sv["o_parts"], sv["h"], name=ln + "_outb")
        dw_out = _mm(sv["z"].T, dy, out_dtype=F32, name=ln + "_dwout")
        if layer % 2 == 0:
            hb = sv["hb"]
            dq, dk, dv = _attn_bwd((sv["q"], 0), (sv["k"], 0), (sv["v"], 0), (sv["o_mla"], 0), (do, 0), sv["lse_mla"], None, None,
                                   dk=HEAD_PAD, npairs=mla_pairs, scale=mla_scale, dq_dtype=F32, name=ln + "_mlab")
            dfq, dfk, dfv, dcrow, dccol = _attn_bwd((hb, E_FQ), (hb, E_FK), (hb, E_FV), (sv["o_fox"], 0), (do, MLA_HEADS * MLA_V),
                                                    sv["lse_fox"], sv["ccol"], sv["crow"], dk=FOX_DIM, npairs=fox_pairs,
                                                    scale=fox_scale, dq_dtype=BF16, name=ln + "_foxb")
            dcum = jnp.pad((dcrow[:, 0, :] + dccol[:, :, 0]).T, ((0, 0), (FL_LANE, LANES - FL_LANE - FOX_HEADS)))
            dlogf = _cumsum(dcum, reverse=True, name=ln + "_cumb")
            dh_small, dq_pre, dqg, dkg, dbf = _even_mid_bwd(
                sv["h"], dq, dk, dv, dlogf, sv["qg"], sv["kg"], sv["bft"], sv["uq_t"], sv["ukv_t"], cos_m, sin_m, name=ln + "_midb")
            duq_p = _mm(sv["qn"].T, dq_pre, out_dtype=F32, name=ln + "_dwuq")
            dkv_cat = jnp.concatenate([dk.astype(BF16), dv], axis=1)
            dukv_p = _mm(sv["kvn"].T, dkv_cat, out_dtype=F32, name=ln + "_dwukv")
            dh = jnp.concatenate([dgate, dfq, dfk, dfv, dh_small], axis=1)
            dw_in_p = _mm(sv["x_b"].T, dh, out_dtype=F32, name=ln + "_dwin")
            dw_in, duq, dukv = _even_weight_grads(dw_in_p, duq_p, dukv_p)
            for n, val in (("even_w_in", dw_in), ("even_w_uq", duq), ("even_w_ukv", dukv), ("even_w_out", dw_out),
                           ("even_q_norm", dqg[0]), ("even_kv_norm", dkg[0]), ("even_b_f", dbf[0, FL_LANE:FL_LANE + FOX_HEADS]),
                           ("even_ln_g", dg_ln[0]), ("even_ln_b", db_ln[0])):
                grads[n][j] = val
        else:
            dq_t, dkp, dvp, dsink = _swa_bwd(sv["q"], sv["kp"], sv["vp"], sv["kp"].T, sink_tiles[j], sv["o_t"], do, do.T, sv["lse"],
                                             name=ln + "_swab")
            dq_r, dk_r = _odd_rope_bwd(dq_t.T, dkp[WINDOW:], cos_s, sin_s, name=ln + "_ropeb")
            dh = jnp.concatenate([dgate, dq_r, dk_r, dvp[WINDOW:].astype(BF16)], axis=1)
            dw_in_p = _mm(sv["x_b"].T, dh, out_dtype=F32, name=ln + "_dwin")
            dw_in, dw_out = _odd_weight_grads(dw_in_p, dw_out)
            for n, val in (("odd_w_in", dw_in), ("odd_w_out", dw_out), ("odd_sinks", _interleave(dsink[None, :, 0, 0], False)[0]),
                           ("odd_ln_g", dg_ln[0]), ("odd_ln_b", db_ln[0])):
                grads[n][j] = val
        dxn = _mm(dh, sv["w_in_t"], out_dtype=F32, name=ln + "_dx", res=dr, res_scale=ALPHA)
    grad_x = dxn[None]
    grads = {n: jnp.stack(v) for n, v in grads.items()}

    gp = [_chip_halves(n, grads[n]) for n in _SHARDED]
    from_sib = _swap_halves(gp, name="grad_swap")
    pair = [_add_sibling(g, r, c_idx, chip_idx, name="grad_add_sibling_" + n) for n, g, r in zip(_SHARDED, gp, from_sib)]
    from_chips = _scatter_to_chips([p[0] for p in pair], name="grad_scatter")
    mine = [_add_chips(p[1], r, c_idx, name="grad_add_chips_" + n) for n, p, r in zip(_SHARDED, pair, from_chips)]
    gshard = dict(zip(_SHARDED, _join_halves(mine, name="grad_join")))

    small = [n for n in names if n not in _SHARDED]
    sv_flat = jnp.concatenate([grads[n].reshape(-1) for n in small] + [loss_tile[0, :1]])
    sv_real = sv_flat.shape[0]
    sv_rows = -(-sv_real // (PACK_COLS * 8)) * 8
    sv_sum = _all_reduce_small(jnp.pad(sv_flat, (0, sv_rows * PACK_COLS - sv_real)).reshape(sv_rows, PACK_COLS), name="small_all_reduce").reshape(-1)
    off = 0
    for n in small:
        size = int(np.prod(grads[n].shape))
        gfull = sv_sum[off:off + size].reshape(grads[n].shape)
        off += size
        if n in ("odd_ln_g", "odd_ln_b"):
            gfull = lax.dynamic_slice_in_dim(gfull, chip * (D_MODEL // N_CHIPS), D_MODEL // N_CHIPS, axis=1)
        gshard[n] = gfull
    loss = sv_sum[off]

    deltas, new_m, new_v = {}, {}, {}
    for n in names:
        deltas[n], new_m[n], new_v[n] = _adamw(weights[n], gshard[n], mom_m[n], mom_v[n], name="adamw_" + n)
    return (loss, grad_x, *[gshard[n] for n in names], *[deltas[n] for n in names],
            *[new_m[n] for n in names], *[new_v[n] for n in names])
```

```python
import jax
import jax.numpy as jnp
import numpy as np
from jax import lax
from jax.experimental import pallas as pl
from jax.experimental.pallas import tpu as pltpu

F32 = jnp.float32
BF16 = jnp.bfloat16
MESH = pl.DeviceIdType.MESH

D_MODEL = 1024
DEPTH = 4
ROPE_THETA = 10000.0
MLA_HEADS, MLA_NOPE, MLA_ROPE, MLA_V = 8, 64, 32, 64
MLA_Q_RANK, MLA_KV_RANK = 256, 128
FOX_HEADS, FOX_DIM = 8, 64
SWA_HEADS, SWA_KV_HEADS, SWA_DIM, WINDOW = 16, 2, 64, 128
RMS_EPS = 1e-6
LN_EPS = 1e-5
ALPHA = (2 * DEPTH) ** 0.25
EVEN_IN = 2984
ODD_IN = 2304
ADAM_LR, ADAM_B1, ADAM_B2, ADAM_EPS, ADAM_WD, ADAM_STEP = 0.001, 0.9, 0.999, 1e-08, 0.01, 10

LANES = 128
HALF = LANES // 2
HEAD_PAD = 128
N_CHIPS = 4
N_DEV = 8
E_GATE, E_FQ, E_FK, E_FV, E_SMALL = 0, 1024, 1536, 2048, 2560
E_PAD_IN = 3072
KPE_LANE = 64
FL_LANE = 96
O_GATE, O_Q, O_K, O_V = 0, 1024, 2048, 2176

_ARB = "arbitrary"
_PAR = "parallel"


def _cparams(sem):
    return pltpu.CompilerParams(dimension_semantics=sem)


def _pick(n, cands):
    for c in cands:
        if n % c == 0:
            return c
    return n


RESIDENT_B_BYTES = 8 << 20


def _mm_resident(a, b, *, out_dtype, name, res, res_scale, also_bf16):
    m, k = a.shape
    _, n = b.shape
    tm = _pick(m, (256, 128))
    cn = _pick(n, (512, 384, 256, 128))
    dtypes = [out_dtype, BF16] if also_bf16 else [out_dtype]

    def body(*refs):
        a_ref, b_ref = refs[:2]
        r_ref = refs[2] if res is not None else None
        outs = refs[3 if res is not None else 2:]
        av = a_ref[...].astype(BF16)
        for c0 in range(0, n, cn):
            r = jnp.dot(av, b_ref[:, c0:c0 + cn].astype(BF16), preferred_element_type=F32)
            if res is not None:
                r = r + res_scale * r_ref[:, c0:c0 + cn]
            for o_ref in outs:
                o_ref[:, c0:c0 + cn] = r.astype(o_ref.dtype)

    rows = lambda w: pl.BlockSpec((tm, w), lambda i: (i, 0))
    in_specs = [rows(k), pl.BlockSpec((k, n), lambda i: (0, 0))] + ([rows(n)] if res is not None else [])
    out = pl.pallas_call(
        body, name=name, grid=(m // tm,), in_specs=in_specs, out_specs=[rows(n)] * len(dtypes),
        out_shape=[jax.ShapeDtypeStruct((m, n), d) for d in dtypes], compiler_params=_cparams((_PAR,)),
    )(*([a, b] + ([res] if res is not None else [])))
    return out if also_bf16 else out[0]


def _mm(a, b, *, out_dtype, name, res=None, res_scale=1.0, also_bf16=False):
    m, k = a.shape
    _, n = b.shape
    if b.size * b.dtype.itemsize <= RESIDENT_B_BYTES:
        return _mm_resident(a, b, out_dtype=out_dtype, name=name, res=res, res_scale=res_scale, also_bf16=also_bf16)
    tm = _pick(m, (512, 256, 128))
    tn = _pick(n, (1024, 768, 512, 384, 256, 128))
    tk = _pick(k, (1024, 768, 512, 256, 128))
    nk = k // tk

    def body(*refs):
        refs = list(refs)
        a_ref, b_ref = refs[:2]
        r_ref = refs[2] if res is not None else None
        acc_ref = refs[-1]
        outs = refs[3 if res is not None else 2:-1]
        kk = pl.program_id(2)

        @pl.when(kk == 0)
        def _():
            acc_ref[...] = jnp.zeros_like(acc_ref)

        acc_ref[...] += jnp.dot(a_ref[...].astype(BF16), b_ref[...].astype(BF16), preferred_element_type=F32)

        @pl.when(kk == nk - 1)
        def _():
            r = acc_ref[...]
            if res is not None:
                r = r + res_scale * r_ref[...]
            for o_ref in outs:
                o_ref[...] = r.astype(o_ref.dtype)

    in_specs = [pl.BlockSpec((tm, tk), lambda i, j, kk: (i, kk)), pl.BlockSpec((tk, tn), lambda i, j, kk: (kk, j))]
    args = [a, b]
    if res is not None:
        in_specs.append(pl.BlockSpec((tm, tn), lambda i, j, kk: (i, j)))
        args.append(res)
    ospec = pl.BlockSpec((tm, tn), lambda i, j, kk: (i, j))
    dtypes = [out_dtype, BF16] if also_bf16 else [out_dtype]
    out = pl.pallas_call(
        body, name=name, grid=(m // tm, n // tn, nk), in_specs=in_specs,
        out_specs=[ospec] * len(dtypes), out_shape=[jax.ShapeDtypeStruct((m, n), d) for d in dtypes],
        scratch_shapes=[pltpu.VMEM((tm, tn), F32)],
        compiler_params=_cparams((_PAR, _PAR, _ARB)),
    )(*args)
    return out if also_bf16 else out[0]


def _rope_tile(t, cos, sin, half):
    w = t.shape[-1]
    lane = lax.broadcasted_iota(jnp.int32, t.shape, 1)
    first = (lane % (2 * half)) < half
    sw = jnp.where(first, pltpu.roll(t, w - half, 1), pltpu.roll(t, half, 1))
    return t * cos + sw * sin


def _sigmoid(x):
    return 1.0 / (1.0 + jnp.exp(-x))


def _lane_mask(shape, lo, hi):
    lane = lax.broadcasted_iota(jnp.int32, shape, 1)
    return (lane >= lo) & (lane < hi)


def _rms(x, g):
    r = lax.rsqrt(jnp.mean(x * x, axis=-1, keepdims=True) + RMS_EPS)
    u = x * r
    return u, r, u * g


def _rms_bwd(dy, u, r, g):
    dyg = dy * g
    dx = r * (dyg - u * jnp.mean(dyg * u, axis=-1, keepdims=True))
    return dx, jnp.sum(dy * u, axis=0, keepdims=True)


def _even_mid_fwd(h, qg, kg, bf_tile, w_uq, w_ukv, cos, sin, *, name):
    s = h.shape[0]
    tb = _pick(s, (512, 256, 128))
    nq = MLA_HEADS * HEAD_PAD

    def body(h_ref, qg_ref, kg_ref, bf_ref, wuq_ref, wukv_ref, cos_ref, sin_ref,
             q_ref, k_ref, v_ref, qn_ref, kvn_ref, lf_ref):
        hb = h_ref[...]
        cq, ckv, misc = hb[:, :MLA_Q_RANK], hb[:, MLA_Q_RANK:MLA_Q_RANK + MLA_KV_RANK], hb[:, MLA_Q_RANK + MLA_KV_RANK:]
        cs, sn = cos_ref[...], sin_ref[...]
        _, _, qn = _rms(cq, qg_ref[...])
        qn = qn.astype(BF16)
        qn_ref[...] = qn
        q = jnp.dot(qn, wuq_ref[...], preferred_element_type=F32)
        _, _, kvn = _rms(ckv, kg_ref[...])
        kvn = kvn.astype(BF16)
        kvn_ref[...] = kvn
        kv = jnp.dot(kvn, wukv_ref[...], preferred_element_type=F32)
        kpe = jnp.where(_lane_mask(misc.shape, KPE_LANE, KPE_LANE + MLA_ROPE), _rope_tile(misc, cs, sn, MLA_ROPE // 2), 0.0)
        for hd in range(MLA_HEADS):
            sl = slice(hd * HEAD_PAD, (hd + 1) * HEAD_PAD)
            q_ref[:, sl] = _rope_tile(q[:, sl], cs, sn, MLA_ROPE // 2).astype(BF16)
            k_ref[:, sl] = (kv[:, sl] + kpe).astype(BF16)
        v_ref[...] = kv[:, nq:].astype(BF16)
        xf = misc + bf_ref[...]
        logf = jnp.minimum(xf, 0.0) - jnp.log(1.0 + jnp.exp(-jnp.abs(xf)))
        lf_ref[...] = jnp.where(_lane_mask(misc.shape, FL_LANE, FL_LANE + FOX_HEADS), logf, 0.0)

    full = lambda a: pl.BlockSpec(a.shape, lambda i: (0,) * a.ndim)
    rows = lambda w, c=0: pl.BlockSpec((tb, w), lambda i: (i, c))
    return pl.pallas_call(
        body, name=name, grid=(s // tb,),
        in_specs=[rows(512, E_SMALL // 512), full(qg), full(kg), full(bf_tile), full(w_uq), full(w_ukv), rows(LANES), rows(LANES)],
        out_specs=[rows(nq), rows(nq), rows(MLA_HEADS * MLA_V), rows(MLA_Q_RANK), rows(MLA_KV_RANK), rows(LANES)],
        out_shape=[jax.ShapeDtypeStruct((s, nq), BF16), jax.ShapeDtypeStruct((s, nq), BF16),
                   jax.ShapeDtypeStruct((s, MLA_HEADS * MLA_V), BF16), jax.ShapeDtypeStruct((s, MLA_Q_RANK), BF16),
                   jax.ShapeDtypeStruct((s, MLA_KV_RANK), BF16), jax.ShapeDtypeStruct((s, LANES), F32)],
        compiler_params=_cparams((_PAR,)),
    )(h, qg, kg, bf_tile, w_uq, w_ukv, cos, sin)


def _even_mid_bwd(h, dq, dk, dv, dlogf, qg, kg, bf_tile, w_uq_t, w_ukv_t, cos, sin, *, name):
    s = h.shape[0]
    tb = _pick(s, (256, 128))
    nq = MLA_HEADS * HEAD_PAD

    def body(h_ref, dq_ref, dk_ref, dv_ref, dlf_ref, qg_ref, kg_ref, bf_ref, wuqt_ref, wukvt_ref, cos_ref, sin_ref,
             dh_ref, dqp_ref, dqg_ref, dkg_ref, dbf_ref):
        @pl.when(pl.program_id(0) == 0)
        def _():
            dqg_ref[...] = jnp.zeros_like(dqg_ref)
            dkg_ref[...] = jnp.zeros_like(dkg_ref)
            dbf_ref[...] = jnp.zeros_like(dbf_ref)

        hb = h_ref[...]
        cq, ckv, misc = hb[:, :MLA_Q_RANK], hb[:, MLA_Q_RANK:MLA_Q_RANK + MLA_KV_RANK], hb[:, MLA_Q_RANK + MLA_KV_RANK:]
        cs, sn = cos_ref[...], -sin_ref[...]
        dkpe = jnp.zeros(misc.shape, F32)
        for hd in range(MLA_HEADS):
            sl = slice(hd * HEAD_PAD, (hd + 1) * HEAD_PAD)
            dqp_ref[:, sl] = _rope_tile(dq_ref[:, sl], cs, sn, MLA_ROPE // 2).astype(BF16)
            dkpe = dkpe + dk_ref[:, sl]
        dqn = jnp.dot(dqp_ref[...], wuqt_ref[...], preferred_element_type=F32)
        uq, rq, _ = _rms(cq, qg_ref[...])
        dcq, dqg = _rms_bwd(dqn, uq, rq, qg_ref[...])
        dqg_ref[...] += dqg
        dkv = jnp.concatenate([dk_ref[...].astype(BF16), dv_ref[...]], axis=1)
        dkvn = jnp.dot(dkv, wukvt_ref[...], preferred_element_type=F32)
        uk, rk, _ = _rms(ckv, kg_ref[...])
        dckv, dkg = _rms_bwd(dkvn, uk, rk, kg_ref[...])
        dkg_ref[...] += dkg
        dmisc = jnp.where(_lane_mask(misc.shape, KPE_LANE, KPE_LANE + MLA_ROPE), _rope_tile(dkpe, cs, sn, MLA_ROPE // 2), 0.0)
        dfl = jnp.where(_lane_mask(misc.shape, FL_LANE, FL_LANE + FOX_HEADS), dlf_ref[...] * _sigmoid(-(misc + bf_ref[...])), 0.0)
        dbf_ref[...] += jnp.sum(dfl, axis=0, keepdims=True)
        dh_ref[:, :MLA_Q_RANK] = dcq.astype(BF16)
        dh_ref[:, MLA_Q_RANK:MLA_Q_RANK + MLA_KV_RANK] = dckv.astype(BF16)
        dh_ref[:, MLA_Q_RANK + MLA_KV_RANK:] = (dmisc + dfl).astype(BF16)

    full = lambda a: pl.BlockSpec(a.shape, lambda i: (0,) * a.ndim)
    rows = lambda w, c=0: pl.BlockSpec((tb, w), lambda i: (i, c))
    return pl.pallas_call(
        body, name=name, grid=(s // tb,),
        in_specs=[rows(512, E_SMALL // 512), rows(nq), rows(nq), rows(MLA_HEADS * MLA_V), rows(LANES), full(qg), full(kg),
                  full(bf_tile), full(w_uq_t), full(w_ukv_t), rows(LANES), rows(LANES)],
        out_specs=[rows(512), rows(nq), full(qg), full(kg), full(bf_tile)],
        out_shape=[jax.ShapeDtypeStruct((s, 512), BF16), jax.ShapeDtypeStruct((s, nq), BF16),
                   jax.ShapeDtypeStruct(qg.shape, F32), jax.ShapeDtypeStruct(kg.shape, F32),
                   jax.ShapeDtypeStruct(bf_tile.shape, F32)],
        compiler_params=_cparams((_ARB,)),
    )(h, dq, dk, dv, dlogf, qg, kg, bf_tile, w_uq_t, w_ukv_t, cos, sin)


def _cumsum(x, *, reverse, name):
    s = x.shape[0]
    tb = _pick(s, (512, 256, 128))
    nb = s // tb

    def body(x_ref, o_ref, carry_ref):
        @pl.when(pl.program_id(0) == 0)
        def _():
            carry_ref[...] = jnp.zeros_like(carry_ref)

        xv = x_ref[...]
        r = lax.broadcasted_iota(jnp.int32, (tb, tb), 0)
        c = lax.broadcasted_iota(jnp.int32, (tb, tb), 1)
        tri = jnp.where((c >= r) if reverse else (c <= r), 1.0, 0.0).astype(BF16)
        hi = xv.astype(BF16)
        r1 = xv - hi.astype(F32)
        mid = r1.astype(BF16)
        lo = (r1 - mid.astype(F32)).astype(BF16)
        cs = (jnp.dot(tri, hi, preferred_element_type=F32) + jnp.dot(tri, mid, preferred_element_type=F32)
              + jnp.dot(tri, lo, preferred_element_type=F32)) + carry_ref[...]
        o_ref[...] = cs
        carry_ref[...] = cs[0:1, :] if reverse else cs[tb - 1:tb, :]

    imap = (lambda i: (nb - 1 - i, 0)) if reverse else (lambda i: (i, 0))
    return pl.pallas_call(
        body, name=name, grid=(nb,), in_specs=[pl.BlockSpec((tb, LANES), imap)],
        out_specs=pl.BlockSpec((tb, LANES), imap), out_shape=jax.ShapeDtypeStruct(x.shape, F32),
        scratch_shapes=[pltpu.VMEM((1, LANES), F32)], compiler_params=_cparams((_ARB,)),
    )(x)


_NT = (((1,), (1,)), ((), ()))
_TN = (((0,), (0,)), ((), ()))


def _head_sel(x, h, dk):
    if dk == LANES:
        return x[:, h * LANES:(h + 1) * LANES]
    return jnp.where(_lane_mask(x.shape, h * HALF, (h + 1) * HALF), x, jnp.zeros_like(x))


def _head_other(x, h, dk):
    return x[:, h * LANES:(h + 1) * LANES] if dk == LANES else x


def _pair(parts, dk=HALF):
    if dk == LANES:
        return jnp.concatenate(parts, axis=1)
    return jnp.where(_lane_mask(parts[0].shape, 0, HALF), parts[0], parts[1])


def _attn_fwd(q, k, v, crow, *, dk, npairs, scale, name):
    (qa, qo), (ka, ko), (va, vo) = q, k, v
    s = qa.shape[0]
    wq = 2 * dk
    t = _pick(s, (512, 256, 128))
    nb = s // t
    bias = crow is not None

    def body(*refs):
        if bias:
            q_ref, k_ref, v_ref, cr_ref, o_ref, lse_ref = refs
        else:
            q_ref, k_ref, v_ref, o_ref, lse_ref = refs
        qi = pl.program_id(1)
        qb = q_ref[...]
        qs = [_head_sel(qb, h, dk) for h in range(2)]

        def scores(j):
            rows = pl.ds(pl.multiple_of(j * t, t), t)
            kb = k_ref[rows, :]
            out = []
            for h in range(2):
                sc = lax.dot_general(qs[h], _head_other(kb, h, dk), _NT, preferred_element_type=F32) * scale
                out.append(sc - cr_ref[h, j] if bias else sc)
            return tuple(out)

        def update(scs, j, state):
            rows = pl.ds(pl.multiple_of(j * t, t), t)
            vb = v_ref[rows, :]
            soft = []
            for h in range(2):
                m, l, _ = state[3 * h:3 * h + 3]
                sc = scs[h]
                m_new = jnp.maximum(m, jnp.max(sc, axis=1, keepdims=True))
                a = jnp.exp(m - m_new)
                p = jnp.exp(sc - m_new)
                soft.append((m_new, a * l + jnp.sum(p, axis=1, keepdims=True), a, p.astype(BF16)))
            new = []
            for h in range(2):
                m_new, l, a, p = soft[h]
                new += [m_new, l, a * state[3 * h + 2] + jnp.dot(p, vb, preferred_element_type=F32)]
            return tuple(new)

        one = (jnp.full((t, 1), -jnp.inf, F32), jnp.zeros((t, 1), F32), jnp.zeros((t, LANES), F32))
        state = lax.fori_loop(0, qi, lambda j, st: update(scores(j), j, st), one + one)
        row = lax.broadcasted_iota(jnp.int32, (t, t), 0)
        col = lax.broadcasted_iota(jnp.int32, (t, t), 1)
        diag = tuple(jnp.where(col <= row, sc, -jnp.inf) for sc in scores(qi))
        state = update(diag, qi, state)
        outs = []
        for h in range(2):
            m, l, acc = state[3 * h:3 * h + 3]
            outs.append(acc / l)
            lse_ref[h] = m + jnp.log(l)
        o_ref[...] = _pair(outs)

    in_specs = [pl.BlockSpec((t, wq), lambda p, i: (i, qo // wq + p)), pl.BlockSpec((s, wq), lambda p, i: (0, ko // wq + p)),
                pl.BlockSpec((s, LANES), lambda p, i: (0, vo // LANES + p))]
    args = [qa, ka, va]
    if bias:
        in_specs.append(pl.BlockSpec((2, nb, 1, t), lambda p, i: (p, 0, 0, 0)))
        args.append(crow.reshape(2 * npairs, nb, 1, t))
    return pl.pallas_call(
        body, name=name, grid=(npairs, nb), in_specs=in_specs,
        out_specs=[pl.BlockSpec((t, LANES), lambda p, i: (i, p)), pl.BlockSpec((2, t, 1), lambda p, i: (p, i, 0))],
        out_shape=[jax.ShapeDtypeStruct((s, npairs * LANES), F32), jax.ShapeDtypeStruct((2 * npairs, s, 1), F32)],
        compiler_params=_cparams((_PAR, _PAR)),
    )(*args)


def _attn_bwd(q, k, v, o, do, lse, crow, *, dk, npairs, scale, dq_dtype, name):
    (qa, qo), (ka, ko), (va, vo), (oa, oo), (da, do_o) = q, k, v, o, do
    s = qa.shape[0]
    wq = 2 * dk
    t = _pick(s, (512, 256, 128))
    nb = s // t
    bias = crow is not None

    def body(*refs):
        if bias:
            (q_ref, k_ref, v_ref, o_ref, do_ref, lse_ref, cr_ref,
             dq_ref, dk_ref, dv_ref, dc_ref, dcc_ref, dq_s, dk_s, dv_s, dc_s) = refs
        else:
            q_ref, k_ref, v_ref, o_ref, do_ref, lse_ref, dq_ref, dk_ref, dv_ref, dq_s, dk_s, dv_s = refs
        ki, qi = pl.program_id(1), pl.program_id(2)

        @pl.when((ki == 0) & (qi == 0))
        def _():
            dq_s[...] = jnp.zeros_like(dq_s)
            if bias:
                dcc_ref[...] = jnp.zeros_like(dcc_ref)

        @pl.when(qi == ki)
        def _():
            dk_s[...] = jnp.zeros_like(dk_s)
            dv_s[...] = jnp.zeros_like(dv_s)
            if bias:
                dc_s[...] = jnp.zeros_like(dc_s)

        def block(on_diagonal):
            qb, kb, vb, dob, ob = q_ref[...], k_ref[...], v_ref[...], do_ref[...], o_ref[...]
            rows = pl.ds(pl.multiple_of(qi * t, t), t)
            row = lax.broadcasted_iota(jnp.int32, (t, t), 0) + qi * t
            col = lax.broadcasted_iota(jnp.int32, (t, t), 1) + ki * t
            fed = []
            for h in range(2):
                kh, doh = _head_other(kb, h, dk), _head_sel(dob, h, HALF)
                sc = lax.dot_general(_head_sel(qb, h, dk), kh, _NT, preferred_element_type=F32)
                dp = lax.dot_general(doh, vb, _NT, preferred_element_type=F32)
                fed.append((kh, doh, sc, dp))
            mid = []
            for h in range(2):
                kh, doh, sc, dp = fed[h]
                sc = sc * scale
                if bias:
                    sc = sc - cr_ref[h]
                if on_diagonal:
                    sc = jnp.where(col <= row, sc, -jnp.inf)
                p = jnp.exp(sc - lse_ref[h])
                delta = jnp.sum(doh.astype(F32) * ob, axis=1, keepdims=True)
                ds = p * (dp - delta)
                if bias:
                    dc_s[h] -= jnp.sum(ds, axis=0, keepdims=True)
                    dcc_ref[h, rows, :] += jnp.sum(ds, axis=1, keepdims=True)
                mid.append((kh, p.astype(BF16), (ds * scale).astype(BF16)))
            dq_parts, dk_parts, dv_parts = [], [], []
            for h in range(2):
                kh, pb, dsb = mid[h]
                dv_parts.append(lax.dot_general(pb, dob, _TN, preferred_element_type=F32))
                dk_parts.append(lax.dot_general(dsb, _head_other(qb, h, dk), _TN, preferred_element_type=F32))
                dq_parts.append(jnp.dot(dsb, kh, preferred_element_type=F32))
            dv_s[...] += _pair(dv_parts)
            dk_s[...] += _pair(dk_parts, dk)
            dq_s[rows, :] += _pair(dq_parts, dk)

        pl.when(qi > ki)(lambda: block(False))
        pl.when(qi == ki)(lambda: block(True))

        @pl.when(qi == nb - 1)
        def _():
            dk_ref[...] = dk_s[...].astype(dk_ref.dtype)
            dv_ref[...] = dv_s[...].astype(dv_ref.dtype)
            if bias:
                dc_ref[...] = dc_s[...]

        @pl.when((ki == nb - 1) & (qi == nb - 1))
        def _():
            dq_ref[...] = dq_s[...].astype(dq_ref.dtype)

    qrow = lambda p, j, i: jnp.maximum(i, j)
    in_specs = [pl.BlockSpec((t, wq), lambda p, j, i: (qrow(p, j, i), qo // wq + p)),
                pl.BlockSpec((t, wq), lambda p, j, i: (j, ko // wq + p)),
                pl.BlockSpec((t, LANES), lambda p, j, i: (j, vo // LANES + p)),
                pl.BlockSpec((t, LANES), lambda p, j, i: (qrow(p, j, i), oo // LANES + p)),
                pl.BlockSpec((t, LANES), lambda p, j, i: (qrow(p, j, i), do_o // LANES + p)),
                pl.BlockSpec((2, t, 1), lambda p, j, i: (p, qrow(p, j, i), 0))]
    args = [qa, ka, va, oa, da, lse]
    out_specs = [pl.BlockSpec((s, wq), lambda p, j, i: (0, p)), pl.BlockSpec((t, wq), lambda p, j, i: (j, p)),
                 pl.BlockSpec((t, LANES), lambda p, j, i: (j, p))]
    out_shape = [jax.ShapeDtypeStruct((s, npairs * wq), dq_dtype), jax.ShapeDtypeStruct((s, npairs * wq), dq_dtype),
                 jax.ShapeDtypeStruct((s, npairs * LANES), BF16)]
    scratch = [pltpu.VMEM((s, wq), F32), pltpu.VMEM((t, wq), F32), pltpu.VMEM((t, LANES), F32)]
    if bias:
        in_specs.append(pl.BlockSpec((2, 1, t), lambda p, j, i: (p, 0, j)))
        args.append(crow)
        out_specs += [pl.BlockSpec((2, 1, t), lambda p, j, i: (p, 0, j)), pl.BlockSpec((2, s, 1), lambda p, j, i: (p, 0, 0))]
        out_shape += [jax.ShapeDtypeStruct((2 * npairs, 1, s), F32), jax.ShapeDtypeStruct((2 * npairs, s, 1), F32)]
        scratch.append(pltpu.VMEM((2, 1, t), F32))
    return pl.pallas_call(
        body, name=name, grid=(npairs, nb, nb), in_specs=in_specs, out_specs=out_specs, out_shape=out_shape,
        scratch_shapes=scratch, compiler_params=_cparams((_PAR, _ARB, _ARB)),
    )(*args)


def _sub_mask(shape, lo, hi):
    sub = lax.broadcasted_iota(jnp.int32, shape, 0)
    return (sub >= lo) & (sub < hi)


def _pair_rows(parts):
    return jnp.where(_sub_mask(parts[0].shape, 0, HALF), parts[0], parts[1])


def _swa_valid(start, t):
    krow = lax.broadcasted_iota(jnp.int32, (t + WINDOW, t), 0)
    qcol = lax.broadcasted_iota(jnp.int32, (t + WINDOW, t), 1)
    diff = qcol - krow + WINDOW
    return (diff >= 0) & (diff < WINDOW) & (krow + start >= WINDOW)


def _swa_tiles(s):
    t = _pick(s, (256, 128))
    return t, _pick(s // t, (4, 2, 1))


def _swa_fwd(q, kp, vp, sink, *, name):
    s = q.shape[0]
    npairs = SWA_HEADS // 2
    t, nt = _swa_tiles(s)
    scale = SWA_DIM ** -0.5

    def body(q_ref, k_ref, v_ref, sk_ref, o_ref, lse_ref):
        sts, vws = [], []
        for u in range(nt):
            start = pl.multiple_of((pl.program_id(1) * nt + u) * t, t)
            kw = k_ref[pl.ds(start, t + WINDOW), :]
            vws.append(v_ref[pl.ds(start, t + WINDOW), :])
            qb = q_ref[u * t:(u + 1) * t, :]
            valid = _swa_valid(start, t)
            for h in range(2):
                st = lax.dot_general(kw, _head_sel(qb, h, HALF), _NT, preferred_element_type=F32) * scale
                sts.append(jnp.where(valid, st, -jnp.inf))
        pts = []
        for u in range(nt):
            for h in range(2):
                st = sts[2 * u + h]
                snk = sk_ref[h][:, 0:1]
                m = jnp.maximum(jnp.max(st, axis=0, keepdims=True), snk)
                e = jnp.exp(st - m)
                l = jnp.sum(e, axis=0, keepdims=True) + jnp.exp(snk - m)
                pts.append((e * (1.0 / l)).astype(BF16))
                lse_ref[h, :, u * t:(u + 1) * t] = m + jnp.log(l)
        for u in range(nt):
            outs = [lax.dot_general(vws[u], pts[2 * u + h], _TN, preferred_element_type=F32) for h in range(2)]
            o_ref[:, u * t:(u + 1) * t] = _pair_rows(outs)

    kvspec = pl.BlockSpec((s + WINDOW, LANES), lambda p, i: (0, 0))
    return pl.pallas_call(
        body, name=name, grid=(npairs, s // (t * nt)),
        in_specs=[pl.BlockSpec((t * nt, LANES), lambda p, i: (i, p)), kvspec, kvspec, pl.BlockSpec((2, 1, LANES), lambda p, i: (p, 0, 0))],
        out_specs=[pl.BlockSpec((LANES, t * nt), lambda p, i: (p, i)), pl.BlockSpec((2, 1, t * nt), lambda p, i: (p, 0, i))],
        out_shape=[jax.ShapeDtypeStruct((npairs * LANES, s), F32), jax.ShapeDtypeStruct((SWA_HEADS, 1, s), F32)],
        compiler_params=_cparams((_PAR, _PAR)),
    )(q, kp, vp, sink)


def _swa_bwd(q, kp, vp, kpt, sink, ot, do, dot, lse, *, name):
    s = q.shape[0]
    npairs = SWA_HEADS // 2
    t, nt = _swa_tiles(s)
    scale = SWA_DIM ** -0.5

    def body(q_ref, k_ref, v_ref, kt_ref, sk_ref, ot_ref, do_ref, dot_ref, lse_ref, dq_ref, dk_ref, dv_ref, dsk_ref):
        pp, i = pl.program_id(0), pl.program_id(1)

        @pl.when((pp == 0) & (i == 0))
        def _():
            dk_ref[...] = jnp.zeros_like(dk_ref)
            dv_ref[...] = jnp.zeros_like(dv_ref)

        @pl.when(i == 0)
        def _():
            dsk_ref[...] = jnp.zeros_like(dsk_ref)

        fed = []
        for u in range(nt):
            cols = slice(u * t, (u + 1) * t)
            start = pl.multiple_of((i * nt + u) * t, t)
            win = pl.ds(start, t + WINDOW)
            qb, dob = q_ref[cols, :], do_ref[cols, :]
            kw, vw = k_ref[win, :], v_ref[win, :]
            valid = _swa_valid(start, t)
            for h in range(2):
                st = lax.dot_general(kw, _head_sel(qb, h, HALF), _NT, preferred_element_type=F32)
                dpt = lax.dot_general(vw, _head_sel(dob, h, HALF), _NT, preferred_element_type=F32)
                fed.append((cols, win, qb, dob, valid, st, dpt))
        mid = []
        for u in range(nt):
            cols = fed[2 * u][0]
            prod = dot_ref[:, cols].astype(F32) * ot_ref[:, cols]
            for h in range(2):
                valid, st, dpt = fed[2 * u + h][4:]
                lse_b = lse_ref[h, :, cols]
                pt = jnp.exp(jnp.where(valid, st * scale, -jnp.inf) - lse_b)
                delta = jnp.sum(jnp.where(_sub_mask(prod.shape, h * HALF, (h + 1) * HALF), prod, 0.0), axis=0, keepdims=True)
                dst = pt * (dpt - delta)
                psink = jnp.exp(sk_ref[h][:, 0:1] - lse_b)
                dsk_ref[h] += jnp.broadcast_to(-jnp.sum(psink * delta, axis=1, keepdims=True), (1, LANES))
                mid.append((pt.astype(BF16), (dst * scale).astype(BF16)))
        for u in range(nt):
            cols, win, qb, dob = fed[2 * u][:4]
            ktw = kt_ref[:, win]
            dq_parts = [jnp.dot(ktw, mid[2 * u + h][1], preferred_element_type=F32) for h in range(2)]
            dk_parts = [jnp.dot(mid[2 * u + h][1], qb, preferred_element_type=F32) for h in range(2)]
            dv_parts = [jnp.dot(mid[2 * u + h][0], dob, preferred_element_type=F32) for h in range(2)]
            dq_ref[:, cols] = _pair_rows(dq_parts)
            dk_ref[win, :] += _pair(dk_parts)
            dv_ref[win, :] += _pair(dv_parts)

    tile = pl.BlockSpec((t * nt, LANES), lambda p, i: (i, p))
    ttile = pl.BlockSpec((LANES, t * nt), lambda p, i: (p, i))
    kvspec = pl.BlockSpec((s + WINDOW, LANES), lambda p, i: (0, 0))
    ktspec = pl.BlockSpec((LANES, s + WINDOW), lambda p, i: (0, 0))
    skspec = pl.BlockSpec((2, 1, LANES), lambda p, i: (p, 0, 0))
    return pl.pallas_call(
        body, name=name, grid=(npairs, s // (t * nt)),
        in_specs=[tile, kvspec, kvspec, ktspec, skspec, ttile, tile, ttile, pl.BlockSpec((2, 1, t * nt), lambda p, i: (p, 0, i))],
        out_specs=[ttile, kvspec, kvspec, skspec],
        out_shape=[jax.ShapeDtypeStruct((npairs * LANES, s), F32), jax.ShapeDtypeStruct((s + WINDOW, LANES), F32),
                   jax.ShapeDtypeStruct((s + WINDOW, LANES), F32), jax.ShapeDtypeStruct((SWA_HEADS, 1, LANES), F32)],
        compiler_params=_cparams((_ARB, _ARB)),
    )(q, kp, vp, kpt, sink, ot, do, dot, lse)


def _odd_rope(h, cos, sin, *, name):
    s = h.shape[0]
    tb = _pick(s, (512, 256, 128))
    nq = SWA_HEADS * SWA_DIM

    def body(q_ref, kv_ref, cos_ref, sin_ref, qo_ref, ko_ref, vo_ref):
        cs, sn = cos_ref[...], sin_ref[...]
        for j in range(nq // LANES):
            sl = slice(j * LANES, (j + 1) * LANES)
            qo_ref[:, sl] = _rope_tile(q_ref[:, sl], cs, sn, SWA_DIM // 2).astype(BF16)
        ko_ref[...] = _rope_tile(kv_ref[:, :LANES], cs, sn, SWA_DIM // 2).astype(BF16)
        vo_ref[...] = kv_ref[:, LANES:].astype(BF16)

    rows = lambda w, c=0: pl.BlockSpec((tb, w), lambda i: (i, c))
    return pl.pallas_call(
        body, name=name, grid=(s // tb,),
        in_specs=[rows(nq, O_Q // nq), rows(2 * LANES, O_K // (2 * LANES)), rows(LANES), rows(LANES)],
        out_specs=[rows(nq), rows(LANES), rows(LANES)],
        out_shape=[jax.ShapeDtypeStruct((s, nq), BF16), jax.ShapeDtypeStruct((s, LANES), BF16), jax.ShapeDtypeStruct((s, LANES), BF16)],
        compiler_params=_cparams((_PAR,)),
    )(h, h, cos, sin)


def _odd_rope_bwd(dq, dk, cos, sin, *, name):
    s = dq.shape[0]
    tb = _pick(s, (512, 256, 128))
    nq = SWA_HEADS * SWA_DIM

    def body(dq_ref, dk_ref, cos_ref, sin_ref, qo_ref, ko_ref):
        cs, sn = cos_ref[...], -sin_ref[...]
        for j in range(nq // LANES):
            sl = slice(j * LANES, (j + 1) * LANES)
            qo_ref[:, sl] = _rope_tile(dq_ref[:, sl], cs, sn, SWA_DIM // 2).astype(BF16)
        ko_ref[...] = _rope_tile(dk_ref[...], cs, sn, SWA_DIM // 2).astype(BF16)

    rows = lambda w: pl.BlockSpec((tb, w), lambda i: (i, 0))
    return pl.pallas_call(
        body, name=name, grid=(s // tb,), in_specs=[rows(nq), rows(LANES), rows(LANES), rows(LANES)],
        out_specs=[rows(nq), rows(LANES)],
        out_shape=[jax.ShapeDtypeStruct((s, nq), BF16), jax.ShapeDtypeStruct((s, LANES), BF16)],
        compiler_params=_cparams((_PAR,)),
    )(dq, dk, cos, sin)


def _out_fwd(o_parts, h, x, w_out, g, b, *, name):
    s = x.shape[0]
    tb = _pick(s, (256, 128))
    n_o = len(o_parts)

    def body(*refs):
        o_refs = refs[:n_o]
        gate_ref, x_ref, w_ref, g_ref, b_ref, xn_ref, xb_ref, z_ref, xh_ref, rs_ref = refs[n_o:]
        gate = gate_ref[...]
        o = o_refs[0][...] if n_o == 1 else jnp.concatenate([r[...] for r in o_refs], axis=1)
        z = (o * (gate * _sigmoid(gate))).astype(BF16)
        z_ref[...] = z
        r = ALPHA * x_ref[...] + jnp.dot(z, w_ref[...], preferred_element_type=F32)
        mu = jnp.mean(r, axis=-1, keepdims=True)
        rc = r - mu
        rstd = lax.rsqrt(jnp.mean(rc * rc, axis=-1, keepdims=True) + LN_EPS)
        xh = rc * rstd
        xn = xh * g_ref[...] + b_ref[...]
        xh_ref[...] = xh
        rs_ref[...] = rstd
        xn_ref[...] = xn
        xb_ref[...] = xn.astype(BF16)

    rows = lambda w: pl.BlockSpec((tb, w), lambda i: (i, 0))
    full = lambda a: pl.BlockSpec(a.shape, lambda i: (0,) * a.ndim)
    return pl.pallas_call(
        body, name=name, grid=(s // tb,),
        in_specs=[rows(a.shape[1]) for a in o_parts] + [rows(D_MODEL), rows(D_MODEL), full(w_out), full(g), full(b)],
        out_specs=[rows(D_MODEL), rows(D_MODEL), rows(D_MODEL), rows(D_MODEL), rows(1)],
        out_shape=[jax.ShapeDtypeStruct((s, D_MODEL), F32), jax.ShapeDtypeStruct((s, D_MODEL), BF16),
                   jax.ShapeDtypeStruct((s, D_MODEL), BF16), jax.ShapeDtypeStruct((s, D_MODEL), F32),
                   jax.ShapeDtypeStruct((s, 1), F32)],
        compiler_params=_cparams((_PAR,)),
    )(*o_parts, h, x, w_out, g, b)


def _out_bwd(dxn, xh, rstd, g, w_out_t, o_parts, h, *, name):
    s = dxn.shape[0]
    tb = _pick(s, (256, 128))
    n_o = len(o_parts)

    def body(*refs):
        dxn_ref, xh_ref, rs_ref, g_ref, wt_ref = refs[:5]
        o_refs = refs[5:5 + n_o]
        gate_ref, dr_ref, dy_ref, do_ref, dgate_ref, dg_ref, db_ref = refs[5 + n_o:]

        @pl.when(pl.program_id(0) == 0)
        def _():
            dg_ref[...] = jnp.zeros_like(dg_ref)
            db_ref[...] = jnp.zeros_like(db_ref)

        dxn_b, xh_b = dxn_ref[...], xh_ref[...]
        dg_ref[...] += jnp.sum(dxn_b * xh_b, axis=0, keepdims=True)
        db_ref[...] += jnp.sum(dxn_b, axis=0, keepdims=True)
        dxh = dxn_b * g_ref[...]
        dr = rs_ref[...] * (dxh - jnp.mean(dxh, axis=-1, keepdims=True) - xh_b * jnp.mean(dxh * xh_b, axis=-1, keepdims=True))
        dr_ref[...] = dr
        dy = dr.astype(BF16)
        dy_ref[...] = dy
        dz = jnp.dot(dy, wt_ref[...], preferred_element_type=F32)
        gate = gate_ref[...]
        sg = _sigmoid(gate)
        o = o_refs[0][...] if n_o == 1 else jnp.concatenate([r[...] for r in o_refs], axis=1)
        do_ref[...] = (dz * (gate * sg)).astype(BF16)
        dgate_ref[...] = (dz * o * (sg * (1.0 + gate * (1.0 - sg)))).astype(BF16)

    rows = lambda w: pl.BlockSpec((tb, w), lambda i: (i, 0))
    full = lambda a: pl.BlockSpec(a.shape, lambda i: (0,) * a.ndim)
    return pl.pallas_call(
        body, name=name, grid=(s // tb,),
        in_specs=[rows(D_MODEL), rows(D_MODEL), rows(1), full(g), full(w_out_t)] + [rows(a.shape[1]) for a in o_parts] + [rows(D_MODEL)],
        out_specs=[rows(D_MODEL), rows(D_MODEL), rows(D_MODEL), rows(D_MODEL), full(g), full(g)],
        out_shape=[jax.ShapeDtypeStruct((s, D_MODEL), F32), jax.ShapeDtypeStruct((s, D_MODEL), BF16),
                   jax.ShapeDtypeStruct((s, D_MODEL), BF16), jax.ShapeDtypeStruct((s, D_MODEL), BF16),
                   jax.ShapeDtypeStruct(g.shape, F32), jax.ShapeDtypeStruct(g.shape, F32)],
        compiler_params=_cparams((_ARB,)),
    )(dxn, xh, rstd, g, w_out_t, *o_parts, h)


def _loss_grad(y, target, *, name):
    s, d = y.shape
    tb = _pick(s, (512, 256, 128))

    def body(y_ref, t_ref, dy_ref, l_ref):
        @pl.when(pl.program_id(0) == 0)
        def _():
            l_ref[...] = jnp.zeros_like(l_ref)

        err = y_ref[...] - t_ref[...]
        dy_ref[...] = err * (1.0 / d)
        per_tok = jnp.mean(err * err, axis=-1, keepdims=True)
        l_ref[...] += 0.5 * jnp.sum(per_tok, axis=0, keepdims=True)

    rows = pl.BlockSpec((tb, d), lambda i: (i, 0))
    return pl.pallas_call(
        body, name=name, grid=(s // tb,), in_specs=[rows, rows],
        out_specs=[rows, pl.BlockSpec((8, LANES), lambda i: (0, 0))],
        out_shape=[jax.ShapeDtypeStruct((s, d), F32), jax.ShapeDtypeStruct((8, LANES), F32)],
        compiler_params=_cparams((_ARB,)),
    )(y, target)


def _adamw(w, g, m, v, *, name):
    shape = w.shape
    w3, g3, m3, v3 = (a.reshape((1,) * (3 - a.ndim) + a.shape) for a in (w, g, m, v))
    a0, a1, a2 = w3.shape
    tb = _pick(a1, (256, 128)) if a1 % 8 == 0 else a1
    c1 = 1.0 - ADAM_B1 ** ADAM_STEP
    c2 = 1.0 - ADAM_B2 ** ADAM_STEP

    def body(w_ref, g_ref, m_ref, v_ref, d_ref, mo_ref, vo_ref):
        gg = g_ref[...]
        mn = ADAM_B1 * m_ref[...] + (1.0 - ADAM_B1) * gg
        vn = ADAM_B2 * v_ref[...] + (1.0 - ADAM_B2) * (gg * gg)
        mo_ref[...] = mn
        vo_ref[...] = vn
        d_ref[...] = -ADAM_LR * ((mn / c1) / (jnp.sqrt(vn / c2) + ADAM_EPS) + ADAM_WD * w_ref[...])

    spec = pl.BlockSpec((1, tb, a2), lambda i, j: (i, j, 0))
    outs = pl.pallas_call(
        body, name=name, grid=(a0, a1 // tb), in_specs=[spec] * 4, out_specs=[spec] * 3,
        out_shape=[jax.ShapeDtypeStruct(w3.shape, F32)] * 3, compiler_params=_cparams((_PAR, _PAR)),
    )(w3, g3, m3, v3)
    return tuple(a.reshape(shape) for a in outs)


def _place():
    x, y, c = lax.axis_index("x"), lax.axis_index("y"), lax.axis_index("c")
    return x, y, c, [(1 - x, y), (x, 1 - y), (1 - x, 1 - y)]


_ANY = pl.BlockSpec(memory_space=pl.ANY)


def _sem_pairs(n):
    return [pltpu.SemaphoreType.DMA((n,)), pltpu.SemaphoreType.DMA((n,))]


def _gather_chip_shards(tensors, *, name):
    nt = len(tensors)
    per = 7

    def body(*refs):
        srcs, outs = refs[:nt], refs[nt:2 * nt]
        send_sems, recv_sems = refs[2 * nt:]
        x, y, c, chips = _place()
        me = 2 * x + y
        sib = (x, y, 1 - c)

        def half(t, chip, hc):
            rh = tensors[t].shape[1] // 2
            return outs[t].at[chip, :, pl.ds(hc * rh, rh), :]

        def copy(t, kk, s_ref, d_ref, to):
            return pltpu.make_async_remote_copy(src_ref=s_ref, dst_ref=d_ref, send_sem=send_sems.at[per * t + kk],
                                                recv_sem=recv_sems.at[per * t + kk], device_id=to, device_id_type=MESH)

        sent = []
        for j, (cx, cy) in enumerate(chips):
            for t in range(nt):
                rh = tensors[t].shape[1] // 2
                cp = copy(t, j, srcs[t].at[:, pl.ds(c * rh, rh), :], half(t, me, c), (cx, cy, c))
                cp.start()
                sent.append(cp)
        own = [copy(t, 6, srcs[t], outs[t].at[me], sib) for t in range(nt)]
        for cp in own:
            cp.start()
        for j, (cx, cy) in enumerate(chips):
            for t in range(nt):
                landed = half(t, 2 * cx + cy, c)
                copy(t, j, landed, landed, (cx, cy, c)).wait_recv()
                fw = copy(t, 3 + j, landed, landed, sib)
                fw.start()
                sent.append(fw)
        for j, (cx, cy) in enumerate(chips):
            for t in range(nt):
                theirs = half(t, 2 * cx + cy, 1 - c)
                copy(t, 3 + j, theirs, theirs, sib).wait_recv()
        for cp in own:
            cp.wait()
        for cp in sent:
            cp.wait_send()

    return pl.pallas_call(
        body, name=name, in_specs=[_ANY] * nt, out_specs=[_ANY] * nt,
        out_shape=[jax.ShapeDtypeStruct((N_CHIPS,) + a.shape, a.dtype) for a in tensors],
        scratch_shapes=_sem_pairs(per * nt),
    )(*tensors)


def _swap_halves(gs, *, name):
    nt = len(gs)

    def body(*refs):
        srcs, outs = refs[:nt], refs[nt:2 * nt]
        send_sems, recv_sems = refs[2 * nt:]
        x, y, c, _ = _place()
        cps = [pltpu.make_async_remote_copy(src_ref=srcs[t].at[:, 1 - c], dst_ref=outs[t], send_sem=send_sems.at[t],
                                            recv_sem=recv_sems.at[t], device_id=(x, y, 1 - c), device_id_type=MESH)
               for t in range(nt)]
        for cp in cps:
            cp.start()
        for cp in cps:
            cp.wait()

    return pl.pallas_call(
        body, name=name, in_specs=[_ANY] * nt, out_specs=[_ANY] * nt,
        out_shape=[jax.ShapeDtypeStruct((N_CHIPS,) + g.shape[2:], g.dtype) for g in gs], scratch_shapes=_sem_pairs(nt),
    )(*gs)


def _scatter_to_chips(blocks, *, name):
    nt = len(blocks)

    def body(*refs):
        srcs, outs = refs[:nt], refs[nt:2 * nt]
        send_sems, recv_sems = refs[2 * nt:]
        x, y, c, chips = _place()
        cps = [pltpu.make_async_remote_copy(src_ref=srcs[t].at[2 * cx + cy], dst_ref=outs[t].at[j],
                                            send_sem=send_sems.at[3 * t + j], recv_sem=recv_sems.at[3 * t + j],
                                            device_id=(cx, cy, c), device_id_type=MESH)
               for j, (cx, cy) in enumerate(chips) for t in range(nt)]
        for cp in cps:
            cp.start()
        for cp in cps:
            cp.wait()

    return pl.pallas_call(
        body, name=name, in_specs=[_ANY] * nt, out_specs=[_ANY] * nt,
        out_shape=[jax.ShapeDtypeStruct((3,) + a.shape[1:], a.dtype) for a in blocks], scratch_shapes=_sem_pairs(3 * nt),
    )(*blocks)


def _join_halves(bufs, *, name):
    nt = len(bufs)

    def body(*refs):
        srcs, outs = refs[:nt], refs[nt:2 * nt]
        send_sems, recv_sems = refs[2 * nt:]
        x, y, c, _ = _place()
        cps = []
        for t in range(nt):
            rh = bufs[t].shape[1] // 2
            rows = pl.ds(c * rh, rh)
            cps.append(pltpu.make_async_remote_copy(src_ref=srcs[t].at[:, rows, :], dst_ref=outs[t].at[:, rows, :],
                                                    send_sem=send_sems.at[t], recv_sem=recv_sems.at[t],
                                                    device_id=(x, y, 1 - c), device_id_type=MESH))
        for cp in cps:
            cp.start()
        for cp in cps:
            cp.wait()

    return pl.pallas_call(
        body, name=name, in_specs=[_ANY] * nt, out_specs=[_ANY] * nt,
        out_shape=[jax.ShapeDtypeStruct(b.shape, b.dtype) for b in bufs], scratch_shapes=_sem_pairs(nt),
        input_output_aliases={t: t for t in range(nt)},
    )(*bufs)


def _add_sibling(g, recv, cidx, chip_idx, *, name):
    _, _, na, rh, cdim = g.shape
    tb = _pick(rh, (256, 128, 64, 32, 16))

    def body(c_ref, k_ref, g_ref, r_ref, s_ref, o_ref):
        tot = g_ref[0, 0] + r_ref[0]
        s_ref[0] = tot.astype(BF16)

        @pl.when(pl.program_id(2) == k_ref[0])
        def _():
            o_ref[...] = tot

    return pl.pallas_call(
        body, name=name,
        grid_spec=pltpu.PrefetchScalarGridSpec(
            num_scalar_prefetch=2, grid=(na, rh // tb, N_CHIPS),
            in_specs=[pl.BlockSpec((1, 1, 1, tb, cdim), lambda a, i, k, c_ref, k_ref: (k, c_ref[0], a, i, 0)),
                      pl.BlockSpec((1, 1, tb, cdim), lambda a, i, k, c_ref, k_ref: (k, a, i, 0))],
            out_specs=[pl.BlockSpec((1, 1, tb, cdim), lambda a, i, k, c_ref, k_ref: (k, a, i, 0)),
                       pl.BlockSpec((1, tb, cdim), lambda a, i, k, c_ref, k_ref: (a, i, 0))]),
        out_shape=[jax.ShapeDtypeStruct((N_CHIPS, na, rh, cdim), BF16), jax.ShapeDtypeStruct((na, rh, cdim), F32)],
        compiler_params=_cparams((_PAR, _PAR, _ARB)),
    )(cidx, chip_idx, g, recv)


def _add_chips(own, recv, cidx, *, name):
    na, rh, cdim = own.shape
    tb = _pick(rh, (256, 128, 64, 32, 16))
    nblk = rh // tb

    def body(c_ref, a_ref, r0_ref, r1_ref, r2_ref, o_ref):
        o_ref[...] = ((a_ref[...] + r0_ref[0].astype(F32)) + r1_ref[0].astype(F32)) + r2_ref[0].astype(F32)

    slot = lambda j: pl.BlockSpec((1, 1, tb, cdim), lambda a, i, c_ref: (j, a, i, 0))
    return pl.pallas_call(
        body, name=name,
        grid_spec=pltpu.PrefetchScalarGridSpec(
            num_scalar_prefetch=1, grid=(na, nblk),
            in_specs=[pl.BlockSpec((1, tb, cdim), lambda a, i, c_ref: (a, i, 0)), slot(0), slot(1), slot(2)],
            out_specs=pl.BlockSpec((1, tb, cdim), lambda a, i, c_ref: (a, c_ref[0] * nblk + i, 0))),
        out_shape=jax.ShapeDtypeStruct((na, 2 * rh, cdim), F32), compiler_params=_cparams((_PAR, _PAR)),
    )(cidx, own, recv, recv, recv)


def _all_reduce_small(v, *, name):
    r, cdim = v.shape

    def body(v_ref, o_ref, buf, send_sems, recv_sems):
        x, y, c, _ = _place()
        me = 4 * x + 2 * y + c
        buf[me] = v_ref[...]
        cps = []
        for p in range(1, N_DEV):
            to = (1 - x if p & 4 else x, 1 - y if p & 2 else y, 1 - c if p & 1 else c)
            cp = pltpu.make_async_remote_copy(src_ref=v_ref, dst_ref=buf.at[me], send_sem=send_sems.at[p - 1],
                                              recv_sem=recv_sems.at[p - 1], device_id=to, device_id_type=MESH)
            cp.start()
            cps.append(cp)
        for p in range(1, N_DEV):
            frm = (4 * x + 2 * y + c) ^ p
            pltpu.make_async_remote_copy(src_ref=v_ref, dst_ref=buf.at[frm], send_sem=send_sems.at[p - 1],
                                         recv_sem=recv_sems.at[p - 1], device_id=(x, y, c), device_id_type=MESH).wait_recv()
        for cp in cps:
            cp.wait_send()
        tot = buf[0]
        for i in range(1, N_DEV):
            tot = tot + buf[i]
        o_ref[...] = tot

    vm = pl.BlockSpec(memory_space=pltpu.VMEM)
    return pl.pallas_call(
        body, name=name, in_specs=[vm], out_specs=vm, out_shape=jax.ShapeDtypeStruct((r, cdim), F32),
        scratch_shapes=[pltpu.VMEM((N_DEV, r, cdim), F32), pltpu.SemaphoreType.DMA((N_DEV - 1,)), pltpu.SemaphoreType.DMA((N_DEV - 1,))],
    )(v)


_SHARDED = ("even_w_in", "even_w_uq", "even_w_ukv", "even_w_out", "odd_w_in", "odd_w_out")
_COL_SHARDED = ("even_w_in", "even_w_uq", "even_w_ukv", "odd_w_in")
PACK_COLS = 1024


def _unshard(name, stacked):
    n, a, b, cc = stacked.shape
    if name in _COL_SHARDED:
        return stacked.transpose(1, 2, 0, 3).reshape(a, b, n * cc)
    return stacked.transpose(1, 0, 2, 3).reshape(a, n * b, cc)


def _chip_halves(name, full):
    a, b, cc = full.shape
    if name in _COL_SHARDED:
        return full.reshape(a, 2, b // 2, N_CHIPS, cc // N_CHIPS).transpose(3, 1, 0, 2, 4)
    return full.reshape(a, N_CHIPS, 2, b // (2 * N_CHIPS), cc).transpose(1, 2, 0, 3, 4)


def _rope_tables(s):
    pos = jnp.arange(s, dtype=F32)

    def ang(d):
        inv = ROPE_THETA ** (-jnp.arange(0, d, 2, dtype=F32) / d)
        a = pos[:, None] * inv[None, :]
        return jnp.cos(a), jnp.sin(a)

    c16, s16 = ang(MLA_ROPE)
    one, zero = jnp.ones((s, KPE_LANE), F32), jnp.zeros((s, KPE_LANE), F32)
    cos_m = jnp.concatenate([one, c16, c16, one[:, :32]], axis=1)
    sin_m = jnp.concatenate([zero, -s16, s16, zero[:, :32]], axis=1)
    c32, s32 = ang(SWA_DIM)
    cos_s = jnp.concatenate([c32, c32, c32, c32], axis=1)
    sin_s = jnp.concatenate([-s32, s32, -s32, s32], axis=1)
    return cos_m, sin_m, cos_s, sin_s


def _even_weights(w_in, w_uq, w_ukv):
    zeros = lambda n: jnp.zeros((D_MODEL, n), w_in.dtype)
    wcq, wckv, wkpe = w_in[:, 0:256], w_in[:, 256:384], w_in[:, 384:416]
    wfq, wfk, wfv = w_in[:, 416:928], w_in[:, 928:1440], w_in[:, 1440:1952]
    wfl, wg = w_in[:, 1952:1960], w_in[:, 1960:2984]
    misc = jnp.concatenate([zeros(KPE_LANE), wkpe, wfl, zeros(LANES - FL_LANE - FOX_HEADS)], axis=1)
    w_in_p = jnp.concatenate([wg, wfq, wfk, wfv, wcq, wckv, misc], axis=1)
    uq = w_uq.reshape(MLA_Q_RANK, MLA_HEADS, MLA_NOPE + MLA_ROPE)
    uq_p = jnp.pad(uq, ((0, 0), (0, 0), (0, HEAD_PAD - MLA_NOPE - MLA_ROPE))).reshape(MLA_Q_RANK, MLA_HEADS * HEAD_PAD)
    ukv = w_ukv.reshape(MLA_KV_RANK, MLA_HEADS, MLA_NOPE + MLA_V)
    uk_p = jnp.pad(ukv[..., :MLA_NOPE], ((0, 0), (0, 0), (0, HEAD_PAD - MLA_NOPE))).reshape(MLA_KV_RANK, MLA_HEADS * HEAD_PAD)
    uv = ukv[..., MLA_NOPE:].reshape(MLA_KV_RANK, MLA_HEADS * MLA_V)
    ukv_p = jnp.concatenate([uk_p, uv], axis=1)
    return w_in_p, w_in_p.T, uq_p, uq_p.T, ukv_p, ukv_p.T


def _even_weight_grads(dw_in_p, duq_p, dukv_p):
    g = dw_in_p
    gate, fq, fk, fv = g[:, E_GATE:E_FQ], g[:, E_FQ:E_FK], g[:, E_FK:E_FV], g[:, E_FV:E_SMALL]
    cq, ckv, misc = g[:, E_SMALL:E_SMALL + 256], g[:, E_SMALL + 256:E_SMALL + 384], g[:, E_SMALL + 384:]
    dw_in = jnp.concatenate([cq, ckv, misc[:, KPE_LANE:KPE_LANE + MLA_ROPE], fq, fk, fv, misc[:, FL_LANE:FL_LANE + FOX_HEADS], gate], axis=1)
    duq = duq_p.reshape(MLA_Q_RANK, MLA_HEADS, HEAD_PAD)[..., :MLA_NOPE + MLA_ROPE].reshape(MLA_Q_RANK, -1)
    nq = MLA_HEADS * HEAD_PAD
    dk = dukv_p[:, :nq].reshape(MLA_KV_RANK, MLA_HEADS, HEAD_PAD)[..., :MLA_NOPE]
    dvv = dukv_p[:, nq:].reshape(MLA_KV_RANK, MLA_HEADS, MLA_V)
    dukv = jnp.concatenate([dk, dvv], axis=-1).reshape(MLA_KV_RANK, -1)
    return dw_in, duq, dukv


def _interleave(w, forward):
    a, b = (SWA_KV_HEADS, SWA_HEADS // SWA_KV_HEADS) if forward else (SWA_HEADS // SWA_KV_HEADS, SWA_KV_HEADS)
    return w.reshape(w.shape[0], a, b, -1).transpose(0, 2, 1, 3).reshape(w.shape[0], -1)


def _odd_weights(w_in, w_out):
    q, k, v, gate = w_in[:, 0:1024], w_in[:, 1024:1152], w_in[:, 1152:1280], w_in[:, 1280:2304]
    w_p = jnp.concatenate([_interleave(gate, True), _interleave(q, True), k, v], axis=1)
    w_out_p = _interleave(w_out.T, True).T
    return w_p, w_p.T, w_out_p, w_out_p.T


def _odd_weight_grads(g, dw_out_p):
    dw_in = jnp.concatenate([_interleave(g[:, O_Q:O_K], False), g[:, O_K:O_V], g[:, O_V:], _interleave(g[:, O_GATE:O_Q], False)], axis=1)
    return dw_in, _interleave(dw_out_p.T, False).T


def kernel(x, even_w_in, even_q_norm, even_w_uq, even_kv_norm, even_w_ukv, even_b_f, even_w_out, even_ln_g, even_ln_b, odd_w_in, odd_sinks, odd_w_out, odd_ln_g, odd_ln_b, loss_target, m_even_w_in, m_even_q_norm, m_even_w_uq, m_even_kv_norm, m_even_w_ukv, m_even_b_f, m_even_w_out, m_even_ln_g, m_even_ln_b, m_odd_w_in, m_odd_sinks, m_odd_w_out, m_odd_ln_g, m_odd_ln_b, v_even_w_in, v_even_q_norm, v_even_w_uq, v_even_kv_norm, v_even_w_ukv, v_even_b_f, v_even_w_out, v_even_ln_g, v_even_ln_b, v_odd_w_in, v_odd_sinks, v_odd_w_out, v_odd_ln_g, v_odd_ln_b):
    weights = dict(even_w_in=even_w_in, even_q_norm=even_q_norm, even_w_uq=even_w_uq, even_kv_norm=even_kv_norm,
                   even_w_ukv=even_w_ukv, even_b_f=even_b_f, even_w_out=even_w_out, even_ln_g=even_ln_g, even_ln_b=even_ln_b,
                   odd_w_in=odd_w_in, odd_sinks=odd_sinks, odd_w_out=odd_w_out, odd_ln_g=odd_ln_g, odd_ln_b=odd_ln_b)
    mom_m = dict(even_w_in=m_even_w_in, even_q_norm=m_even_q_norm, even_w_uq=m_even_w_uq, even_kv_norm=m_even_kv_norm,
                 even_w_ukv=m_even_w_ukv, even_b_f=m_even_b_f, even_w_out=m_even_w_out, even_ln_g=m_even_ln_g, even_ln_b=m_even_ln_b,
                 odd_w_in=m_odd_w_in, odd_sinks=m_odd_sinks, odd_w_out=m_odd_w_out, odd_ln_g=m_odd_ln_g, odd_ln_b=m_odd_ln_b)
    mom_v = dict(even_w_in=v_even_w_in, even_q_norm=v_even_q_norm, even_w_uq=v_even_w_uq, even_kv_norm=v_even_kv_norm,
                 even_w_ukv=v_even_w_ukv, even_b_f=v_even_b_f, even_w_out=v_even_w_out, even_ln_g=v_even_ln_g, even_ln_b=v_even_ln_b,
                 odd_w_in=v_odd_w_in, odd_sinks=v_odd_sinks, odd_w_out=v_odd_w_out, odd_ln_g=v_odd_ln_g, odd_ln_b=v_odd_ln_b)
    names = list(weights)
    xl = x[0]
    tgt = loss_target[0]
    s = xl.shape[0]
    ax, ay, ac = lax.axis_index("x"), lax.axis_index("y"), lax.axis_index("c")
    chip = 2 * ax + ay
    c_idx = ac.astype(jnp.int32).reshape(1)
    chip_idx = chip.astype(jnp.int32).reshape(1)

    ln_odd = jnp.pad(jnp.concatenate([odd_ln_g, odd_ln_b]), ((0, 12), (0, 0)))[None]
    gathered = _gather_chip_shards([weights[n].astype(BF16) for n in _SHARDED] + [ln_odd], name="gather_weights")
    full = {n: _unshard(n, gathered[i]) for i, n in enumerate(_SHARDED)}
    ln_all = gathered[-1][:, 0]
    odd_g_full = ln_all[:, 0:2].transpose(1, 0, 2).reshape(2, D_MODEL)
    odd_b_full = ln_all[:, 2:4].transpose(1, 0, 2).reshape(2, D_MODEL)

    cos_m, sin_m, cos_s, sin_s = _rope_tables(s)
    bf_tiles = jnp.pad(even_b_f, ((0, 0), (FL_LANE, LANES - FL_LANE - FOX_HEADS)))
    sink_tiles = jnp.broadcast_to(_interleave(odd_sinks, True)[:, :, None, None], (2, SWA_HEADS, 1, LANES))
    mla_scale = (MLA_NOPE + MLA_ROPE) ** -0.5
    fox_scale = FOX_DIM ** -0.5
    mla_pairs, fox_pairs = MLA_HEADS // 2, FOX_HEADS // 2

    saved = []
    x_f, x_b = xl, xl.astype(BF16)
    for layer in range(DEPTH):
        j = layer // 2
        ln = f"L{layer}"
        if layer % 2 == 0:
            w_in_p, w_in_t, uq_p, uq_t, ukv_p, ukv_t = _even_weights(full["even_w_in"][j], full["even_w_uq"][j], full["even_w_ukv"][j])
            w_out, w_out_t = full["even_w_out"][j], full["even_w_out"][j].T
            qg, kg, bft = even_q_norm[j][None], even_kv_norm[j][None], bf_tiles[j][None]
            h, hb = _mm(x_b, w_in_p, out_dtype=F32, name=ln + "_in", also_bf16=True)
            q, k, v, qn, kvn, logf = _even_mid_fwd(h, qg, kg, bft, uq_p, ukv_p, cos_m, sin_m, name=ln + "_mid")
            cum = _cumsum(logf, reverse=False, name=ln + "_cum")[:, FL_LANE:FL_LANE + FOX_HEADS].T
            crow = cum[:, None, :]
            o_mla, lse_mla = _attn_fwd((q, 0), (k, 0), (v, 0), None, dk=HEAD_PAD, npairs=mla_pairs, scale=mla_scale, name=ln + "_mla")
            o_fox, lse_fox = _attn_fwd((hb, E_FQ), (hb, E_FK), (hb, E_FV), crow, dk=FOX_DIM, npairs=fox_pairs,
                                       scale=fox_scale, name=ln + "_fox")
            o_parts = [o_mla, o_fox]
            g_ln, b_ln = even_ln_g[j][None], even_ln_b[j][None]
            x_n, x_nb, z, xh, rstd = _out_fwd(o_parts, h, x_f, w_out, g_ln, b_ln, name=ln + "_out")
            saved.append(dict(h=h, hb=hb, x_b=x_b, qn=qn, kvn=kvn, q=q, k=k, v=v, crow=crow,
                              o_mla=o_mla, o_fox=o_fox, lse_mla=lse_mla, lse_fox=lse_fox, o_parts=o_parts, z=z, xh=xh, rstd=rstd,
                              w_in_t=w_in_t, uq_t=uq_t, ukv_t=ukv_t, w_out_t=w_out_t, qg=qg, kg=kg, bft=bft, g_ln=g_ln))
        else:
            w_in_p, w_in_t, w_out, w_out_t = _odd_weights(full["odd_w_in"][j], full["odd_w_out"][j])
            h = _mm(x_b, w_in_p, out_dtype=F32, name=ln + "_in")
            q, k, v = _odd_rope(h, cos_s, sin_s, name=ln + "_rope")
            kp = jnp.pad(k, ((WINDOW, 0), (0, 0)))
            vp = jnp.pad(v, ((WINDOW, 0), (0, 0)))
            o_t, lse = _swa_fwd(q, kp, vp, sink_tiles[j], name=ln + "_swa")
            o = o_t.T
            g_ln, b_ln = odd_g_full[j][None], odd_b_full[j][None]
            x_n, x_nb, z, xh, rstd = _out_fwd([o], h, x_f, w_out, g_ln, b_ln, name=ln + "_out")
            saved.append(dict(h=h, x_b=x_b, q=q, kp=kp, vp=vp, lse=lse, o_t=o_t, o_parts=[o], z=z, xh=xh, rstd=rstd,
                              w_in_t=w_in_t, w_out_t=w_out_t, g_ln=g_ln))
        x_f, x_b = x_n, x_nb

    dxn, loss_tile = _loss_grad(x_f, tgt, name="loss")

    grads = {n: [None, None] for n in names}
    for layer in reversed(range(DEPTH)):
        j = layer // 2
        ln = f"L{layer}"
        sv = saved[layer]
        dr, dy, do, dgate, dg_ln, db_ln = _out_bwd(dxn, sv["xh"], sv["rstd"], sv["g_ln"], sv["w_out_t"], sv["o_parts"], sv["h"], name=ln + "_outb")
        dw_out = _mm(sv["z"].T, dy, out_dtype=F32, name=ln + "_dwout")
        if layer % 2 == 0:
            hb = sv["hb"]
            dq, dk, dv = _attn_bwd((sv["q"], 0), (sv["k"], 0), (sv["v"], 0), (sv["o_mla"], 0), (do, 0), sv["lse_mla"], None,
                                   dk=HEAD_PAD, npairs=mla_pairs, scale=mla_scale, dq_dtype=F32, name=ln + "_mlab")
            dfq, dfk, dfv, dcrow, dccol = _attn_bwd((hb, E_FQ), (hb, E_FK), (hb, E_FV), (sv["o_fox"], 0), (do, MLA_HEADS * MLA_V),
                                                    sv["lse_fox"], sv["crow"], dk=FOX_DIM, npairs=fox_pairs,
                                                    scale=fox_scale, dq_dtype=BF16, name=ln + "_foxb")
            dcum = jnp.pad((dcrow[:, 0, :] + dccol[:, :, 0]).T, ((0, 0), (FL_LANE, LANES - FL_LANE - FOX_HEADS)))
            dlogf = _cumsum(dcum, reverse=True, name=ln + "_cumb")
            dh_small, dq_pre, dqg, dkg, dbf = _even_mid_bwd(
                sv["h"], dq, dk, dv, dlogf, sv["qg"], sv["kg"], sv["bft"], sv["uq_t"], sv["ukv_t"], cos_m, sin_m, name=ln + "_midb")
            duq_p = _mm(sv["qn"].T, dq_pre, out_dtype=F32, name=ln + "_dwuq")
            dkv_cat = jnp.concatenate([dk.astype(BF16), dv], axis=1)
            dukv_p = _mm(sv["kvn"].T, dkv_cat, out_dtype=F32, name=ln + "_dwukv")
            dh = jnp.concatenate([dgate, dfq, dfk, dfv, dh_small], axis=1)
            dw_in_p = _mm(sv["x_b"].T, dh, out_dtype=F32, name=ln + "_dwin")
            dw_in, duq, dukv = _even_weight_grads(dw_in_p, duq_p, dukv_p)
            for n, val in (("even_w_in", dw_in), ("even_w_uq", duq), ("even_w_ukv", dukv), ("even_w_out", dw_out),
                           ("even_q_norm", dqg[0]), ("even_kv_norm", dkg[0]), ("even_b_f", dbf[0, FL_LANE:FL_LANE + FOX_HEADS]),
                           ("even_ln_g", dg_ln[0]), ("even_ln_b", db_ln[0])):
                grads[n][j] = val
        else:
            dq_t, dkp, dvp, dsink = _swa_bwd(sv["q"], sv["kp"], sv["vp"], sv["kp"].T, sink_tiles[j], sv["o_t"], do, do.T, sv["lse"],
                                             name=ln + "_swab")
            dq_r, dk_r = _odd_rope_bwd(dq_t.T, dkp[WINDOW:], cos_s, sin_s, name=ln + "_ropeb")
            dh = jnp.concatenate([dgate, dq_r, dk_r, dvp[WINDOW:].astype(BF16)], axis=1)
            dw_in_p = _mm(sv["x_b"].T, dh, out_dtype=F32, name=ln + "_dwin")
            dw_in, dw_out = _odd_weight_grads(dw_in_p, dw_out)
            for n, val in (("odd_w_in", dw_in), ("odd_w_out", dw_out), ("odd_sinks", _interleave(dsink[None, :, 0, 0], False)[0]),
                           ("odd_ln_g", dg_ln[0]), ("odd_ln_b", db_ln[0])):
                grads[n][j] = val
        dxn = _mm(dh, sv["w_in_t"], out_dtype=F32, name=ln + "_dx", res=dr, res_scale=ALPHA)
    grad_x = dxn[None]
    grads = {n: jnp.stack(v) for n, v in grads.items()}

    gp = [_chip_halves(n, grads[n]) for n in _SHARDED]
    from_sib = _swap_halves(gp, name="grad_swap")
    pair = [_add_sibling(g, r, c_idx, chip_idx, name="grad_add_sibling_" + n) for n, g, r in zip(_SHARDED, gp, from_sib)]
    from_chips = _scatter_to_chips([p[0] for p in pair], name="grad_scatter")
    mine = [_add_chips(p[1], r, c_idx, name="grad_add_chips_" + n) for n, p, r in zip(_SHARDED, pair, from_chips)]
    gshard = dict(zip(_SHARDED, _join_halves(mine, name="grad_join")))

    small = [n for n in names if n not in _SHARDED]
    sv_flat = jnp.concatenate([grads[n].reshape(-1) for n in small] + [loss_tile[0, :1]])
    sv_real = sv_flat.shape[0]
    sv_rows = -(-sv_real // (PACK_COLS * 8)) * 8
    sv_sum = _all_reduce_small(jnp.pad(sv_flat, (0, sv_rows * PACK_COLS - sv_real)).reshape(sv_rows, PACK_COLS), name="small_all_reduce").reshape(-1)
    off = 0
    for n in small:
        size = int(np.prod(grads[n].shape))
        gfull = sv_sum[off:off + size].reshape(grads[n].shape)
        off += size
        if n in ("odd_ln_g", "odd_ln_b"):
            gfull = lax.dynamic_slice_in_dim(gfull, chip * (D_MODEL // N_CHIPS), D_MODEL // N_CHIPS, axis=1)
        gshard[n] = gfull
    loss = sv_sum[off]

    deltas, new_m, new_v = {}, {}, {}
    for n in names:
        deltas[n], new_m[n], new_v[n] = _adamw(weights[n], gshard[n], mom_m[n], mom_v[n], name="adamw_" + n)
    return (loss, grad_x, *[gshard[n] for n in names], *[deltas[n] for n in names],
            *[new_m[n] for n in names], *[new_v[n] for n in names])
```

```python
import jax
import jax.numpy as jnp
import numpy as np
from jax import lax
from jax.experimental import pallas as pl
from jax.experimental.pallas import tpu as pltpu

F32 = jnp.float32
BF16 = jnp.bfloat16
MESH = pl.DeviceIdType.MESH

D_MODEL = 1024
DEPTH = 4
ROPE_THETA = 10000.0
MLA_HEADS, MLA_NOPE, MLA_ROPE, MLA_V = 8, 64, 32, 64
MLA_Q_RANK, MLA_KV_RANK = 256, 128
FOX_HEADS, FOX_DIM = 8, 64
SWA_HEADS, SWA_KV_HEADS, SWA_DIM, WINDOW = 16, 2, 64, 128
RMS_EPS = 1e-6
LN_EPS = 1e-5
ALPHA = (2 * DEPTH) ** 0.25
EVEN_IN = 2984
ODD_IN = 2304
ADAM_LR, ADAM_B1, ADAM_B2, ADAM_EPS, ADAM_WD, ADAM_STEP = 0.001, 0.9, 0.999, 1e-08, 0.01, 10

LANES = 128
HALF = LANES // 2
HEAD_PAD = 128
N_CHIPS = 4
N_DEV = 8
E_GATE, E_FQ, E_FK, E_FV, E_SMALL = 0, 1024, 1536, 2048, 2560
E_PAD_IN = 3072
KPE_LANE = 64
FL_LANE = 96
O_GATE, O_Q, O_K, O_V = 0, 1024, 2048, 2176

_ARB = "arbitrary"
_PAR = "parallel"


def _cparams(sem):
    return pltpu.CompilerParams(dimension_semantics=sem)


def _pick(n, cands):
    for c in cands:
        if n % c == 0:
            return c
    return n


RESIDENT_B_BYTES = 8 << 20


def _mm_resident(a, b, *, out_dtype, name, res, res_scale, also_bf16):
    m, k = a.shape
    _, n = b.shape
    tm = _pick(m, (256, 128))
    cn = _pick(n, (512, 384, 256, 128))
    dtypes = [out_dtype, BF16] if also_bf16 else [out_dtype]

    def body(*refs):
        a_ref, b_ref = refs[:2]
        r_ref = refs[2] if res is not None else None
        outs = refs[3 if res is not None else 2:]
        av = a_ref[...].astype(BF16)
        for c0 in range(0, n, cn):
            r = jnp.dot(av, b_ref[:, c0:c0 + cn].astype(BF16), preferred_element_type=F32)
            if res is not None:
                r = r + res_scale * r_ref[:, c0:c0 + cn]
            for o_ref in outs:
                o_ref[:, c0:c0 + cn] = r.astype(o_ref.dtype)

    rows = lambda w: pl.BlockSpec((tm, w), lambda i: (i, 0))
    in_specs = [rows(k), pl.BlockSpec((k, n), lambda i: (0, 0))] + ([rows(n)] if res is not None else [])
    out = pl.pallas_call(
        body, name=name, grid=(m // tm,), in_specs=in_specs, out_specs=[rows(n)] * len(dtypes),
        out_shape=[jax.ShapeDtypeStruct((m, n), d) for d in dtypes], compiler_params=_cparams((_PAR,)),
    )(*([a, b] + ([res] if res is not None else [])))
    return out if also_bf16 else out[0]


def _mm(a, b, *, out_dtype, name, res=None, res_scale=1.0, also_bf16=False):
    m, k = a.shape
    _, n = b.shape
    if b.size * b.dtype.itemsize <= RESIDENT_B_BYTES:
        return _mm_resident(a, b, out_dtype=out_dtype, name=name, res=res, res_scale=res_scale, also_bf16=also_bf16)
    tm = _pick(m, (512, 256, 128))
    tn = _pick(n, (1024, 768, 512, 384, 256, 128))
    tk = _pick(k, (1024, 768, 512, 256, 128))
    nk = k // tk

    def body(*refs):
        refs = list(refs)
        a_ref, b_ref = refs[:2]
        r_ref = refs[2] if res is not None else None
        acc_ref = refs[-1]
        outs = refs[3 if res is not None else 2:-1]
        kk = pl.program_id(2)

        @pl.when(kk == 0)
        def _():
            acc_ref[...] = jnp.zeros_like(acc_ref)

        acc_ref[...] += jnp.dot(a_ref[...].astype(BF16), b_ref[...].astype(BF16), preferred_element_type=F32)

        @pl.when(kk == nk - 1)
        def _():
            r = acc_ref[...]
            if res is not None:
                r = r + res_scale * r_ref[...]
            for o_ref in outs:
                o_ref[...] = r.astype(o_ref.dtype)

    in_specs = [pl.BlockSpec((tm, tk), lambda i, j, kk: (i, kk)), pl.BlockSpec((tk, tn), lambda i, j, kk: (kk, j))]
    args = [a, b]
    if res is not None:
        in_specs.append(pl.BlockSpec((tm, tn), lambda i, j, kk: (i, j)))
        args.append(res)
    ospec = pl.BlockSpec((tm, tn), lambda i, j, kk: (i, j))
    dtypes = [out_dtype, BF16] if also_bf16 else [out_dtype]
    out = pl.pallas_call(
        body, name=name, grid=(m // tm, n // tn, nk), in_specs=in_specs,
        out_specs=[ospec] * len(dtypes), out_shape=[jax.ShapeDtypeStruct((m, n), d) for d in dtypes],
        scratch_shapes=[pltpu.VMEM((tm, tn), F32)],
        compiler_params=_cparams((_PAR, _PAR, _ARB)),
    )(*args)
    return out if also_bf16 else out[0]


def _rope_tile(t, cos, sin, half):
    w = t.shape[-1]
    lane = lax.broadcasted_iota(jnp.int32, t.shape, 1)
    first = (lane % (2 * half)) < half
    sw = jnp.where(first, pltpu.roll(t, w - half, 1), pltpu.roll(t, half, 1))
    return t * cos + sw * sin


def _sigmoid(x):
    return 1.0 / (1.0 + jnp.exp(-x))


def _lane_mask(shape, lo, hi):
    lane = lax.broadcasted_iota(jnp.int32, shape, 1)
    return (lane >= lo) & (lane < hi)


def _rms(x, g):
    r = lax.rsqrt(jnp.mean(x * x, axis=-1, keepdims=True) + RMS_EPS)
    u = x * r
    return u, r, u * g


def _rms_bwd(dy, u, r, g):
    dyg = dy * g
    dx = r * (dyg - u * jnp.mean(dyg * u, axis=-1, keepdims=True))
    return dx, jnp.sum(dy * u, axis=0, keepdims=True)


def _even_mid_fwd(h, qg, kg, bf_tile, w_uq, w_ukv, cos, sin, *, name):
    s = h.shape[0]
    tb = _pick(s, (512, 256, 128))
    nq = MLA_HEADS * HEAD_PAD

    def body(h_ref, qg_ref, kg_ref, bf_ref, wuq_ref, wukv_ref, cos_ref, sin_ref,
             q_ref, k_ref, v_ref, qn_ref, kvn_ref, lf_ref):
        hb = h_ref[...]
        cq, ckv, misc = hb[:, :MLA_Q_RANK], hb[:, MLA_Q_RANK:MLA_Q_RANK + MLA_KV_RANK], hb[:, MLA_Q_RANK + MLA_KV_RANK:]
        cs, sn = cos_ref[...], sin_ref[...]
        _, _, qn = _rms(cq, qg_ref[...])
        qn = qn.astype(BF16)
        qn_ref[...] = qn
        q = jnp.dot(qn, wuq_ref[...], preferred_element_type=F32)
        _, _, kvn = _rms(ckv, kg_ref[...])
        kvn = kvn.astype(BF16)
        kvn_ref[...] = kvn
        kv = jnp.dot(kvn, wukv_ref[...], preferred_element_type=F32)
        kpe = jnp.where(_lane_mask(misc.shape, KPE_LANE, KPE_LANE + MLA_ROPE), _rope_tile(misc, cs, sn, MLA_ROPE // 2), 0.0)
        for hd in range(MLA_HEADS):
            sl = slice(hd * HEAD_PAD, (hd + 1) * HEAD_PAD)
            q_ref[:, sl] = _rope_tile(q[:, sl], cs, sn, MLA_ROPE // 2).astype(BF16)
            k_ref[:, sl] = (kv[:, sl] + kpe).astype(BF16)
        v_ref[...] = kv[:, nq:].astype(BF16)
        xf = misc + bf_ref[...]
        logf = jnp.minimum(xf, 0.0) - jnp.log(1.0 + jnp.exp(-jnp.abs(xf)))
        lf_ref[...] = jnp.where(_lane_mask(misc.shape, FL_LANE, FL_LANE + FOX_HEADS), logf, 0.0)

    full = lambda a: pl.BlockSpec(a.shape, lambda i: (0,) * a.ndim)
    rows = lambda w, c=0: pl.BlockSpec((tb, w), lambda i: (i, c))
    return pl.pallas_call(
        body, name=name, grid=(s // tb,),
        in_specs=[rows(512, E_SMALL // 512), full(qg), full(kg), full(bf_tile), full(w_uq), full(w_ukv), rows(LANES), rows(LANES)],
        out_specs=[rows(nq), rows(nq), rows(MLA_HEADS * MLA_V), rows(MLA_Q_RANK), rows(MLA_KV_RANK), rows(LANES)],
        out_shape=[jax.ShapeDtypeStruct((s, nq), BF16), jax.ShapeDtypeStruct((s, nq), BF16),
                   jax.ShapeDtypeStruct((s, MLA_HEADS * MLA_V), BF16), jax.ShapeDtypeStruct((s, MLA_Q_RANK), BF16),
                   jax.ShapeDtypeStruct((s, MLA_KV_RANK), BF16), jax.ShapeDtypeStruct((s, LANES), F32)],
        compiler_params=_cparams((_PAR,)),
    )(h, qg, kg, bf_tile, w_uq, w_ukv, cos, sin)


def _even_mid_bwd(h, dq, dk, dv, dlogf, qg, kg, bf_tile, w_uq_t, w_ukv_t, cos, sin, *, name):
    s = h.shape[0]
    tb = _pick(s, (256, 128))
    nq = MLA_HEADS * HEAD_PAD

    def body(h_ref, dq_ref, dk_ref, dv_ref, dlf_ref, qg_ref, kg_ref, bf_ref, wuqt_ref, wukvt_ref, cos_ref, sin_ref,
             dh_ref, dqp_ref, dqg_ref, dkg_ref, dbf_ref):
        @pl.when(pl.program_id(0) == 0)
        def _():
            dqg_ref[...] = jnp.zeros_like(dqg_ref)
            dkg_ref[...] = jnp.zeros_like(dkg_ref)
            dbf_ref[...] = jnp.zeros_like(dbf_ref)

        hb = h_ref[...]
        cq, ckv, misc = hb[:, :MLA_Q_RANK], hb[:, MLA_Q_RANK:MLA_Q_RANK + MLA_KV_RANK], hb[:, MLA_Q_RANK + MLA_KV_RANK:]
        cs, sn = cos_ref[...], -sin_ref[...]
        dkpe = jnp.zeros(misc.shape, F32)
        for hd in range(MLA_HEADS):
            sl = slice(hd * HEAD_PAD, (hd + 1) * HEAD_PAD)
            dqp_ref[:, sl] = _rope_tile(dq_ref[:, sl], cs, sn, MLA_ROPE // 2).astype(BF16)
            dkpe = dkpe + dk_ref[:, sl]
        dqn = jnp.dot(dqp_ref[...], wuqt_ref[...], preferred_element_type=F32)
        uq, rq, _ = _rms(cq, qg_ref[...])
        dcq, dqg = _rms_bwd(dqn, uq, rq, qg_ref[...])
        dqg_ref[...] += dqg
        dkv = jnp.concatenate([dk_ref[...].astype(BF16), dv_ref[...]], axis=1)
        dkvn = jnp.dot(dkv, wukvt_ref[...], preferred_element_type=F32)
        uk, rk, _ = _rms(ckv, kg_ref[...])
        dckv, dkg = _rms_bwd(dkvn, uk, rk, kg_ref[...])
        dkg_ref[...] += dkg
        dmisc = jnp.where(_lane_mask(misc.shape, KPE_LANE, KPE_LANE + MLA_ROPE), _rope_tile(dkpe, cs, sn, MLA_ROPE // 2), 0.0)
        dfl = jnp.where(_lane_mask(misc.shape, FL_LANE, FL_LANE + FOX_HEADS), dlf_ref[...] * _sigmoid(-(misc + bf_ref[...])), 0.0)
        dbf_ref[...] += jnp.sum(dfl, axis=0, keepdims=True)
        dh_ref[:, :MLA_Q_RANK] = dcq.astype(BF16)
        dh_ref[:, MLA_Q_RANK:MLA_Q_RANK + MLA_KV_RANK] = dckv.astype(BF16)
        dh_ref[:, MLA_Q_RANK + MLA_KV_RANK:] = (dmisc + dfl).astype(BF16)

    full = lambda a: pl.BlockSpec(a.shape, lambda i: (0,) * a.ndim)
    rows = lambda w, c=0: pl.BlockSpec((tb, w), lambda i: (i, c))
    return pl.pallas_call(
        body, name=name, grid=(s // tb,),
        in_specs=[rows(512, E_SMALL // 512), rows(nq), rows(nq), rows(MLA_HEADS * MLA_V), rows(LANES), full(qg), full(kg),
                  full(bf_tile), full(w_uq_t), full(w_ukv_t), rows(LANES), rows(LANES)],
        out_specs=[rows(512), rows(nq), full(qg), full(kg), full(bf_tile)],
        out_shape=[jax.ShapeDtypeStruct((s, 512), BF16), jax.ShapeDtypeStruct((s, nq), BF16),
                   jax.ShapeDtypeStruct(qg.shape, F32), jax.ShapeDtypeStruct(kg.shape, F32),
                   jax.ShapeDtypeStruct(bf_tile.shape, F32)],
        compiler_params=_cparams((_ARB,)),
    )(h, dq, dk, dv, dlogf, qg, kg, bf_tile, w_uq_t, w_ukv_t, cos, sin)


def _cumsum(x, *, reverse, name):
    s = x.shape[0]
    tb = _pick(s, (512, 256, 128))
    nb = s // tb

    def body(x_ref, o_ref, carry_ref):
        @pl.when(pl.program_id(0) == 0)
        def _():
            carry_ref[...] = jnp.zeros_like(carry_ref)

        xv = x_ref[...]
        r = lax.broadcasted_iota(jnp.int32, (tb, tb), 0)
        c = lax.broadcasted_iota(jnp.int32, (tb, tb), 1)
        tri = jnp.where((c >= r) if reverse else (c <= r), 1.0, 0.0).astype(BF16)
        hi = xv.astype(BF16)
        r1 = xv - hi.astype(F32)
        mid = r1.astype(BF16)
        lo = (r1 - mid.astype(F32)).astype(BF16)
        cs = (jnp.dot(tri, hi, preferred_element_type=F32) + jnp.dot(tri, mid, preferred_element_type=F32)
              + jnp.dot(tri, lo, preferred_element_type=F32)) + carry_ref[...]
        o_ref[...] = cs
        carry_ref[...] = cs[0:1, :] if reverse else cs[tb - 1:tb, :]

    imap = (lambda i: (nb - 1 - i, 0)) if reverse else (lambda i: (i, 0))
    return pl.pallas_call(
        body, name=name, grid=(nb,), in_specs=[pl.BlockSpec((tb, LANES), imap)],
        out_specs=pl.BlockSpec((tb, LANES), imap), out_shape=jax.ShapeDtypeStruct(x.shape, F32),
        scratch_shapes=[pltpu.VMEM((1, LANES), F32)], compiler_params=_cparams((_ARB,)),
    )(x)


_NT = (((1,), (1,)), ((), ()))
_TN = (((0,), (0,)), ((), ()))


def _head_sel(x, h, dk):
    if dk == LANES:
        return x[:, h * LANES:(h + 1) * LANES]
    return jnp.where(_lane_mask(x.shape, h * HALF, (h + 1) * HALF), x, jnp.zeros_like(x))


def _head_other(x, h, dk):
    return x[:, h * LANES:(h + 1) * LANES] if dk == LANES else x


def _pair(parts, dk=HALF):
    if dk == LANES:
        return jnp.concatenate(parts, axis=1)
    return jnp.where(_lane_mask(parts[0].shape, 0, HALF), parts[0], parts[1])


def _attn_fwd(q, k, v, crow, *, dk, npairs, scale, name, gather=()):
    (qa, qo), (ka, ko), (va, vo) = q, k, v
    s = qa.shape[0]
    wq = 2 * dk
    t = _pick(s, (512, 256, 128))
    nb = s // t
    bias = crow is not None
    ng = len(gather)

    def body(*refs):
        n_in = 3 + bias
        q_ref, k_ref, v_ref = refs[:3]
        cr_ref = refs[3] if bias else None
        o_ref, lse_ref = refs[n_in + ng:n_in + ng + 2]
        qi = pl.program_id(1)
        if ng:
            phases = _gather_phases([a.shape for a in gather], refs[n_in:n_in + ng], refs[n_in + ng + 2:n_in + 2 * ng + 2],
                                    *refs[n_in + 2 * ng + 2:])
            pair = pl.program_id(0)
            pl.when((pair == 0) & (qi == 0))(phases[0])
            pl.when((pair == npairs - 1) & (qi == 0))(phases[1])
        qb = q_ref[...]
        qs = [_head_sel(qb, h, dk) for h in range(2)]

        def scores(j):
            rows = pl.ds(pl.multiple_of(j * t, t), t)
            kb = k_ref[rows, :]
            out = []
            for h in range(2):
                sc = lax.dot_general(qs[h], _head_other(kb, h, dk), _NT, preferred_element_type=F32) * scale
                out.append(sc - cr_ref[h, j] if bias else sc)
            return tuple(out)

        def update(scs, j, state):
            rows = pl.ds(pl.multiple_of(j * t, t), t)
            vb = v_ref[rows, :]
            soft = []
            for h in range(2):
                m, l, _ = state[3 * h:3 * h + 3]
                sc = scs[h]
                m_new = jnp.maximum(m, jnp.max(sc, axis=1, keepdims=True))
                a = jnp.exp(m - m_new)
                p = jnp.exp(sc - m_new)
                soft.append((m_new, a * l + jnp.sum(p, axis=1, keepdims=True), a, p.astype(BF16)))
            new = []
            for h in range(2):
                m_new, l, a, p = soft[h]
                new += [m_new, l, a * state[3 * h + 2] + jnp.dot(p, vb, preferred_element_type=F32)]
            return tuple(new)

        one = (jnp.full((t, 1), -jnp.inf, F32), jnp.zeros((t, 1), F32), jnp.zeros((t, LANES), F32))
        state = lax.fori_loop(0, qi, lambda j, st: update(scores(j), j, st), one + one)
        row = lax.broadcasted_iota(jnp.int32, (t, t), 0)
        col = lax.broadcasted_iota(jnp.int32, (t, t), 1)
        diag = tuple(jnp.where(col <= row, sc, -jnp.inf) for sc in scores(qi))
        state = update(diag, qi, state)
        outs = []
        for h in range(2):
            m, l, acc = state[3 * h:3 * h + 3]
            outs.append(acc / l)
            lse_ref[h] = m + jnp.log(l)
        o_ref[...] = _pair(outs)
        if ng:
            pl.when((pair == npairs - 1) & (qi == nb - 1))(phases[2])

    in_specs = [pl.BlockSpec((t, wq), lambda p, i: (i, qo // wq + p)), pl.BlockSpec((s, wq), lambda p, i: (0, ko // wq + p)),
                pl.BlockSpec((s, LANES), lambda p, i: (0, vo // LANES + p))]
    args = [qa, ka, va]
    if bias:
        in_specs.append(pl.BlockSpec((2, nb, 1, t), lambda p, i: (p, 0, 0, 0)))
        args.append(crow.reshape(2 * npairs, nb, 1, t))
    out = pl.pallas_call(
        body, name=name, grid=(npairs, nb), in_specs=in_specs + [_ANY] * ng,
        out_specs=[pl.BlockSpec((t, LANES), lambda p, i: (i, p)), pl.BlockSpec((2, t, 1), lambda p, i: (p, i, 0))] + [_ANY] * ng,
        out_shape=[jax.ShapeDtypeStruct((s, npairs * LANES), F32), jax.ShapeDtypeStruct((2 * npairs, s, 1), F32)]
        + [jax.ShapeDtypeStruct((N_CHIPS,) + a.shape, a.dtype) for a in gather],
        scratch_shapes=_sem_pairs(GATHER_COPIES * ng) if ng else [],
        compiler_params=_cparams((_ARB, _ARB) if ng else (_PAR, _PAR)),
    )(*args, *gather)
    return (out[0], out[1], out[2:]) if ng else (out[0], out[1])


def _attn_bwd(q, k, v, o, do, lse, crow, *, dk, npairs, scale, dq_dtype, name):
    (qa, qo), (ka, ko), (va, vo), (oa, oo), (da, do_o) = q, k, v, o, do
    s = qa.shape[0]
    wq = 2 * dk
    t = _pick(s, (512, 256, 128))
    nb = s // t
    bias = crow is not None

    def body(*refs):
        if bias:
            (q_ref, k_ref, v_ref, o_ref, do_ref, lse_ref, cr_ref,
             dq_ref, dk_ref, dv_ref, dc_ref, dcc_ref, dq_s, dk_s, dv_s, dc_s) = refs
        else:
            q_ref, k_ref, v_ref, o_ref, do_ref, lse_ref, dq_ref, dk_ref, dv_ref, dq_s, dk_s, dv_s = refs
        ki, qi = pl.program_id(1), pl.program_id(2)

        @pl.when((ki == 0) & (qi == 0))
        def _():
            dq_s[...] = jnp.zeros_like(dq_s)
            if bias:
                dcc_ref[...] = jnp.zeros_like(dcc_ref)

        @pl.when(qi == ki)
        def _():
            dk_s[...] = jnp.zeros_like(dk_s)
            dv_s[...] = jnp.zeros_like(dv_s)
            if bias:
                dc_s[...] = jnp.zeros_like(dc_s)

        def block(on_diagonal):
            qb, kb, vb, dob, ob = q_ref[...], k_ref[...], v_ref[...], do_ref[...], o_ref[...]
            rows = pl.ds(pl.multiple_of(qi * t, t), t)
            row = lax.broadcasted_iota(jnp.int32, (t, t), 0) + qi * t
            col = lax.broadcasted_iota(jnp.int32, (t, t), 1) + ki * t
            fed = []
            for h in range(2):
                kh, doh = _head_other(kb, h, dk), _head_sel(dob, h, HALF)
                sc = lax.dot_general(_head_sel(qb, h, dk), kh, _NT, preferred_element_type=F32)
                dp = lax.dot_general(doh, vb, _NT, preferred_element_type=F32)
                fed.append((kh, doh, sc, dp))
            mid = []
            for h in range(2):
                kh, doh, sc, dp = fed[h]
                sc = sc * scale
                if bias:
                    sc = sc - cr_ref[h]
                if on_diagonal:
                    sc = jnp.where(col <= row, sc, -jnp.inf)
                p = jnp.exp(sc - lse_ref[h])
                delta = jnp.sum(doh.astype(F32) * ob, axis=1, keepdims=True)
                ds = p * (dp - delta)
                if bias:
                    dc_s[h] -= jnp.sum(ds, axis=0, keepdims=True)
                    dcc_ref[h, rows, :] += jnp.sum(ds, axis=1, keepdims=True)
                mid.append((kh, p.astype(BF16), (ds * scale).astype(BF16)))
            dq_parts, dk_parts, dv_parts = [], [], []
            for h in range(2):
                kh, pb, dsb = mid[h]
                dv_parts.append(lax.dot_general(pb, dob, _TN, preferred_element_type=F32))
                dk_parts.append(lax.dot_general(dsb, _head_other(qb, h, dk), _TN, preferred_element_type=F32))
                dq_parts.append(jnp.dot(dsb, kh, preferred_element_type=F32))
            dv_s[...] += _pair(dv_parts)
            dk_s[...] += _pair(dk_parts, dk)
            dq_s[rows, :] += _pair(dq_parts, dk)

        pl.when(qi > ki)(lambda: block(False))
        pl.when(qi == ki)(lambda: block(True))

        @pl.when(qi == nb - 1)
        def _():
            dk_ref[...] = dk_s[...].astype(dk_ref.dtype)
            dv_ref[...] = dv_s[...].astype(dv_ref.dtype)
            if bias:
                dc_ref[...] = dc_s[...]

        @pl.when((ki == nb - 1) & (qi == nb - 1))
        def _():
            dq_ref[...] = dq_s[...].astype(dq_ref.dtype)

    qrow = lambda p, j, i: jnp.maximum(i, j)
    in_specs = [pl.BlockSpec((t, wq), lambda p, j, i: (qrow(p, j, i), qo // wq + p)),
                pl.BlockSpec((t, wq), lambda p, j, i: (j, ko // wq + p)),
                pl.BlockSpec((t, LANES), lambda p, j, i: (j, vo // LANES + p)),
                pl.BlockSpec((t, LANES), lambda p, j, i: (qrow(p, j, i), oo // LANES + p)),
                pl.BlockSpec((t, LANES), lambda p, j, i: (qrow(p, j, i), do_o // LANES + p)),
                pl.BlockSpec((2, t, 1), lambda p, j, i: (p, qrow(p, j, i), 0))]
    args = [qa, ka, va, oa, da, lse]
    out_specs = [pl.BlockSpec((s, wq), lambda p, j, i: (0, p)), pl.BlockSpec((t, wq), lambda p, j, i: (j, p)),
                 pl.BlockSpec((t, LANES), lambda p, j, i: (j, p))]
    out_shape = [jax.ShapeDtypeStruct((s, npairs * wq), dq_dtype), jax.ShapeDtypeStruct((s, npairs * wq), dq_dtype),
                 jax.ShapeDtypeStruct((s, npairs * LANES), BF16)]
    scratch = [pltpu.VMEM((s, wq), F32), pltpu.VMEM((t, wq), F32), pltpu.VMEM((t, LANES), F32)]
    if bias:
        in_specs.append(pl.BlockSpec((2, 1, t), lambda p, j, i: (p, 0, j)))
        args.append(crow)
        out_specs += [pl.BlockSpec((2, 1, t), lambda p, j, i: (p, 0, j)), pl.BlockSpec((2, s, 1), lambda p, j, i: (p, 0, 0))]
        out_shape += [jax.ShapeDtypeStruct((2 * npairs, 1, s), F32), jax.ShapeDtypeStruct((2 * npairs, s, 1), F32)]
        scratch.append(pltpu.VMEM((2, 1, t), F32))
    return pl.pallas_call(
        body, name=name, grid=(npairs, nb, nb), in_specs=in_specs, out_specs=out_specs, out_shape=out_shape,
        scratch_shapes=scratch, compiler_params=_cparams((_PAR, _ARB, _ARB)),
    )(*args)


def _sub_mask(shape, lo, hi):
    sub = lax.broadcasted_iota(jnp.int32, shape, 0)
    return (sub >= lo) & (sub < hi)


def _pair_rows(parts):
    return jnp.where(_sub_mask(parts[0].shape, 0, HALF), parts[0], parts[1])


def _swa_valid(start, t):
    krow = lax.broadcasted_iota(jnp.int32, (t + WINDOW, t), 0)
    qcol = lax.broadcasted_iota(jnp.int32, (t + WINDOW, t), 1)
    diff = qcol - krow + WINDOW
    return (diff >= 0) & (diff < WINDOW) & (krow + start >= WINDOW)


def _swa_tiles(s):
    t = _pick(s, (256, 128))
    return t, _pick(s // t, (4, 2, 1))


def _swa_fwd(q, kp, vp, sink, *, name):
    s = q.shape[0]
    npairs = SWA_HEADS // 2
    t, nt = _swa_tiles(s)
    scale = SWA_DIM ** -0.5

    def body(q_ref, k_ref, v_ref, sk_ref, o_ref, lse_ref):
        sts, vws = [], []
        for u in range(nt):
            start = pl.multiple_of((pl.program_id(1) * nt + u) * t, t)
            kw = k_ref[pl.ds(start, t + WINDOW), :]
            vws.append(v_ref[pl.ds(start, t + WINDOW), :])
            qb = q_ref[u * t:(u + 1) * t, :]
            valid = _swa_valid(start, t)
            for h in range(2):
                st = lax.dot_general(kw, _head_sel(qb, h, HALF), _NT, preferred_element_type=F32) * scale
                sts.append(jnp.where(valid, st, -jnp.inf))
        pts = []
        for u in range(nt):
            for h in range(2):
                st = sts[2 * u + h]
                snk = sk_ref[h][:, 0:1]
                m = jnp.maximum(jnp.max(st, axis=0, keepdims=True), snk)
                e = jnp.exp(st - m)
                l = jnp.sum(e, axis=0, keepdims=True) + jnp.exp(snk - m)
                pts.append((e * (1.0 / l)).astype(BF16))
                lse_ref[h, :, u * t:(u + 1) * t] = m + jnp.log(l)
        for u in range(nt):
            outs = [lax.dot_general(vws[u], pts[2 * u + h], _TN, preferred_element_type=F32) for h in range(2)]
            o_ref[:, u * t:(u + 1) * t] = _pair_rows(outs)

    kvspec = pl.BlockSpec((s + WINDOW, LANES), lambda p, i: (0, 0))
    return pl.pallas_call(
        body, name=name, grid=(npairs, s // (t * nt)),
        in_specs=[pl.BlockSpec((t * nt, LANES), lambda p, i: (i, p)), kvspec, kvspec, pl.BlockSpec((2, 1, LANES), lambda p, i: (p, 0, 0))],
        out_specs=[pl.BlockSpec((LANES, t * nt), lambda p, i: (p, i)), pl.BlockSpec((2, 1, t * nt), lambda p, i: (p, 0, i))],
        out_shape=[jax.ShapeDtypeStruct((npairs * LANES, s), F32), jax.ShapeDtypeStruct((SWA_HEADS, 1, s), F32)],
        compiler_params=_cparams((_PAR, _PAR)),
    )(q, kp, vp, sink)


def _swa_bwd(q, kp, vp, kpt, sink, ot, do, dot, lse, *, name):
    s = q.shape[0]
    npairs = SWA_HEADS // 2
    t, nt = _swa_tiles(s)
    scale = SWA_DIM ** -0.5

    def body(q_ref, k_ref, v_ref, kt_ref, sk_ref, ot_ref, do_ref, dot_ref, lse_ref, dq_ref, dk_ref, dv_ref, dsk_ref):
        pp, i = pl.program_id(0), pl.program_id(1)

        @pl.when((pp == 0) & (i == 0))
        def _():
            dk_ref[...] = jnp.zeros_like(dk_ref)
            dv_ref[...] = jnp.zeros_like(dv_ref)

        @pl.when(i == 0)
        def _():
            dsk_ref[...] = jnp.zeros_like(dsk_ref)

        fed = []
        for u in range(nt):
            cols = slice(u * t, (u + 1) * t)
            start = pl.multiple_of((i * nt + u) * t, t)
            win = pl.ds(start, t + WINDOW)
            qb, dob = q_ref[cols, :], do_ref[cols, :]
            kw, vw = k_ref[win, :], v_ref[win, :]
            valid = _swa_valid(start, t)
            for h in range(2):
                st = lax.dot_general(kw, _head_sel(qb, h, HALF), _NT, preferred_element_type=F32)
                dpt = lax.dot_general(vw, _head_sel(dob, h, HALF), _NT, preferred_element_type=F32)
                fed.append((cols, win, qb, dob, valid, st, dpt))
        mid = []
        for u in range(nt):
            cols = fed[2 * u][0]
            prod = dot_ref[:, cols].astype(F32) * ot_ref[:, cols]
            for h in range(2):
                valid, st, dpt = fed[2 * u + h][4:]
                lse_b = lse_ref[h, :, cols]
                pt = jnp.exp(jnp.where(valid, st * scale, -jnp.inf) - lse_b)
                delta = jnp.sum(jnp.where(_sub_mask(prod.shape, h * HALF, (h + 1) * HALF), prod, 0.0), axis=0, keepdims=True)
                dst = pt * (dpt - delta)
                psink = jnp.exp(sk_ref[h][:, 0:1] - lse_b)
                dsk_ref[h] += jnp.broadcast_to(-jnp.sum(psink * delta, axis=1, keepdims=True), (1, LANES))
                mid.append((pt.astype(BF16), (dst * scale).astype(BF16)))
        for u in range(nt):
            cols, win, qb, dob = fed[2 * u][:4]
            ktw = kt_ref[:, win]
            dq_parts = [jnp.dot(ktw, mid[2 * u + h][1], preferred_element_type=F32) for h in range(2)]
            dk_parts = [jnp.dot(mid[2 * u + h][1], qb, preferred_element_type=F32) for h in range(2)]
            dv_parts = [jnp.dot(mid[2 * u + h][0], dob, preferred_element_type=F32) for h in range(2)]
            dq_ref[:, cols] = _pair_rows(dq_parts)
            dk_ref[win, :] += _pair(dk_parts)
            dv_ref[win, :] += _pair(dv_parts)

    tile = pl.BlockSpec((t * nt, LANES), lambda p, i: (i, p))
    ttile = pl.BlockSpec((LANES, t * nt), lambda p, i: (p, i))
    kvspec = pl.BlockSpec((s + WINDOW, LANES), lambda p, i: (0, 0))
    ktspec = pl.BlockSpec((LANES, s + WINDOW), lambda p, i: (0, 0))
    skspec = pl.BlockSpec((2, 1, LANES), lambda p, i: (p, 0, 0))
    return pl.pallas_call(
        body, name=name, grid=(npairs, s // (t * nt)),
        in_specs=[tile, kvspec, kvspec, ktspec, skspec, ttile, tile, ttile, pl.BlockSpec((2, 1, t * nt), lambda p, i: (p, 0, i))],
        out_specs=[ttile, kvspec, kvspec, skspec],
        out_shape=[jax.ShapeDtypeStruct((npairs * LANES, s), F32), jax.ShapeDtypeStruct((s + WINDOW, LANES), F32),
                   jax.ShapeDtypeStruct((s + WINDOW, LANES), F32), jax.ShapeDtypeStruct((SWA_HEADS, 1, LANES), F32)],
        compiler_params=_cparams((_ARB, _ARB)),
    )(q, kp, vp, kpt, sink, ot, do, dot, lse)


def _odd_rope(h, cos, sin, *, name):
    s = h.shape[0]
    tb = _pick(s, (512, 256, 128))
    nq = SWA_HEADS * SWA_DIM

    def body(q_ref, kv_ref, cos_ref, sin_ref, qo_ref, ko_ref, vo_ref):
        cs, sn = cos_ref[...], sin_ref[...]
        for j in range(nq // LANES):
            sl = slice(j * LANES, (j + 1) * LANES)
            qo_ref[:, sl] = _rope_tile(q_ref[:, sl], cs, sn, SWA_DIM // 2).astype(BF16)
        ko_ref[...] = _rope_tile(kv_ref[:, :LANES], cs, sn, SWA_DIM // 2).astype(BF16)
        vo_ref[...] = kv_ref[:, LANES:].astype(BF16)

    rows = lambda w, c=0: pl.BlockSpec((tb, w), lambda i: (i, c))
    return pl.pallas_call(
        body, name=name, grid=(s // tb,),
        in_specs=[rows(nq, O_Q // nq), rows(2 * LANES, O_K // (2 * LANES)), rows(LANES), rows(LANES)],
        out_specs=[rows(nq), rows(LANES), rows(LANES)],
        out_shape=[jax.ShapeDtypeStruct((s, nq), BF16), jax.ShapeDtypeStruct((s, LANES), BF16), jax.ShapeDtypeStruct((s, LANES), BF16)],
        compiler_params=_cparams((_PAR,)),
    )(h, h, cos, sin)


def _odd_rope_bwd(dq, dk, cos, sin, *, name):
    s = dq.shape[0]
    tb = _pick(s, (512, 256, 128))
    nq = SWA_HEADS * SWA_DIM

    def body(dq_ref, dk_ref, cos_ref, sin_ref, qo_ref, ko_ref):
        cs, sn = cos_ref[...], -sin_ref[...]
        for j in range(nq // LANES):
            sl = slice(j * LANES, (j + 1) * LANES)
            qo_ref[:, sl] = _rope_tile(dq_ref[:, sl], cs, sn, SWA_DIM // 2).astype(BF16)
        ko_ref[...] = _rope_tile(dk_ref[...], cs, sn, SWA_DIM // 2).astype(BF16)

    rows = lambda w: pl.BlockSpec((tb, w), lambda i: (i, 0))
    return pl.pallas_call(
        body, name=name, grid=(s // tb,), in_specs=[rows(nq), rows(LANES), rows(LANES), rows(LANES)],
        out_specs=[rows(nq), rows(LANES)],
        out_shape=[jax.ShapeDtypeStruct((s, nq), BF16), jax.ShapeDtypeStruct((s, LANES), BF16)],
        compiler_params=_cparams((_PAR,)),
    )(dq, dk, cos, sin)


def _out_fwd(o_parts, h, x, w_out, g, b, *, name):
    s = x.shape[0]
    tb = _pick(s, (256, 128))
    n_o = len(o_parts)

    def body(*refs):
        o_refs = refs[:n_o]
        gate_ref, x_ref, w_ref, g_ref, b_ref, xn_ref, xb_ref, z_ref, xh_ref, rs_ref = refs[n_o:]
        gate = gate_ref[...]
        o = o_refs[0][...] if n_o == 1 else jnp.concatenate([r[...] for r in o_refs], axis=1)
        z = (o * (gate * _sigmoid(gate))).astype(BF16)
        z_ref[...] = z
        r = ALPHA * x_ref[...] + jnp.dot(z, w_ref[...], preferred_element_type=F32)
        mu = jnp.mean(r, axis=-1, keepdims=True)
        rc = r - mu
        rstd = lax.rsqrt(jnp.mean(rc * rc, axis=-1, keepdims=True) + LN_EPS)
        xh = rc * rstd
        xn = xh * g_ref[...] + b_ref[...]
        xh_ref[...] = xh
        rs_ref[...] = rstd
        xn_ref[...] = xn
        xb_ref[...] = xn.astype(BF16)

    rows = lambda w: pl.BlockSpec((tb, w), lambda i: (i, 0))
    full = lambda a: pl.BlockSpec(a.shape, lambda i: (0,) * a.ndim)
    return pl.pallas_call(
        body, name=name, grid=(s // tb,),
        in_specs=[rows(a.shape[1]) for a in o_parts] + [rows(D_MODEL), rows(D_MODEL), full(w_out), full(g), full(b)],
        out_specs=[rows(D_MODEL), rows(D_MODEL), rows(D_MODEL), rows(D_MODEL), rows(1)],
        out_shape=[jax.ShapeDtypeStruct((s, D_MODEL), F32), jax.ShapeDtypeStruct((s, D_MODEL), BF16),
                   jax.ShapeDtypeStruct((s, D_MODEL), BF16), jax.ShapeDtypeStruct((s, D_MODEL), F32),
                   jax.ShapeDtypeStruct((s, 1), F32)],
        compiler_params=_cparams((_PAR,)),
    )(*o_parts, h, x, w_out, g, b)


def _out_bwd(dxn, xh, rstd, g, w_out_t, o_parts, h, *, name):
    s = dxn.shape[0]
    tb = _pick(s, (256, 128))
    n_o = len(o_parts)

    def body(*refs):
        dxn_ref, xh_ref, rs_ref, g_ref, wt_ref = refs[:5]
        o_refs = refs[5:5 + n_o]
        gate_ref, dr_ref, dy_ref, do_ref, dgate_ref, dg_ref, db_ref = refs[5 + n_o:]

        @pl.when(pl.program_id(0) == 0)
        def _():
            dg_ref[...] = jnp.zeros_like(dg_ref)
            db_ref[...] = jnp.zeros_like(db_ref)

        dxn_b, xh_b = dxn_ref[...], xh_ref[...]
        dg_ref[...] += jnp.sum(dxn_b * xh_b, axis=0, keepdims=True)
        db_ref[...] += jnp.sum(dxn_b, axis=0, keepdims=True)
        dxh = dxn_b * g_ref[...]
        dr = rs_ref[...] * (dxh - jnp.mean(dxh, axis=-1, keepdims=True) - xh_b * jnp.mean(dxh * xh_b, axis=-1, keepdims=True))
        dr_ref[...] = dr
        dy = dr.astype(BF16)
        dy_ref[...] = dy
        dz = jnp.dot(dy, wt_ref[...], preferred_element_type=F32)
        gate = gate_ref[...]
        sg = _sigmoid(gate)
        o = o_refs[0][...] if n_o == 1 else jnp.concatenate([r[...] for r in o_refs], axis=1)
        do_ref[...] = (dz * (gate * sg)).astype(BF16)
        dgate_ref[...] = (dz * o * (sg * (1.0 + gate * (1.0 - sg)))).astype(BF16)

    rows = lambda w: pl.BlockSpec((tb, w), lambda i: (i, 0))
    full = lambda a: pl.BlockSpec(a.shape, lambda i: (0,) * a.ndim)
    return pl.pallas_call(
        body, name=name, grid=(s // tb,),
        in_specs=[rows(D_MODEL), rows(D_MODEL), rows(1), full(g), full(w_out_t)] + [rows(a.shape[1]) for a in o_parts] + [rows(D_MODEL)],
        out_specs=[rows(D_MODEL), rows(D_MODEL), rows(D_MODEL), rows(D_MODEL), full(g), full(g)],
        out_shape=[jax.ShapeDtypeStruct((s, D_MODEL), F32), jax.ShapeDtypeStruct((s, D_MODEL), BF16),
                   jax.ShapeDtypeStruct((s, D_MODEL), BF16), jax.ShapeDtypeStruct((s, D_MODEL), BF16),
                   jax.ShapeDtypeStruct(g.shape, F32), jax.ShapeDtypeStruct(g.shape, F32)],
        compiler_params=_cparams((_ARB,)),
    )(dxn, xh, rstd, g, w_out_t, *o_parts, h)


def _loss_grad(y, target, *, name):
    s, d = y.shape
    tb = _pick(s, (512, 256, 128))

    def body(y_ref, t_ref, dy_ref, l_ref):
        @pl.when(pl.program_id(0) == 0)
        def _():
            l_ref[...] = jnp.zeros_like(l_ref)

        err = y_ref[...] - t_ref[...]
        dy_ref[...] = err * (1.0 / d)
        per_tok = jnp.mean(err * err, axis=-1, keepdims=True)
        l_ref[...] += 0.5 * jnp.sum(per_tok, axis=0, keepdims=True)

    rows = pl.BlockSpec((tb, d), lambda i: (i, 0))
    return pl.pallas_call(
        body, name=name, grid=(s // tb,), in_specs=[rows, rows],
        out_specs=[rows, pl.BlockSpec((8, LANES), lambda i: (0, 0))],
        out_shape=[jax.ShapeDtypeStruct((s, d), F32), jax.ShapeDtypeStruct((8, LANES), F32)],
        compiler_params=_cparams((_ARB,)),
    )(y, target)


def _adamw(w, g, m, v, *, name):
    shape = w.shape
    w3, g3, m3, v3 = (a.reshape((1,) * (3 - a.ndim) + a.shape) for a in (w, g, m, v))
    a0, a1, a2 = w3.shape
    tb = _pick(a1, (256, 128)) if a1 % 8 == 0 else a1
    c1 = 1.0 - ADAM_B1 ** ADAM_STEP
    c2 = 1.0 - ADAM_B2 ** ADAM_STEP

    def body(w_ref, g_ref, m_ref, v_ref, d_ref, mo_ref, vo_ref):
        gg = g_ref[...]
        mn = ADAM_B1 * m_ref[...] + (1.0 - ADAM_B1) * gg
        vn = ADAM_B2 * v_ref[...] + (1.0 - ADAM_B2) * (gg * gg)
        mo_ref[...] = mn
        vo_ref[...] = vn
        d_ref[...] = -ADAM_LR * ((mn / c1) / (jnp.sqrt(vn / c2) + ADAM_EPS) + ADAM_WD * w_ref[...])

    spec = pl.BlockSpec((1, tb, a2), lambda i, j: (i, j, 0))
    outs = pl.pallas_call(
        body, name=name, grid=(a0, a1 // tb), in_specs=[spec] * 4, out_specs=[spec] * 3,
        out_shape=[jax.ShapeDtypeStruct(w3.shape, F32)] * 3, compiler_params=_cparams((_PAR, _PAR)),
    )(w3, g3, m3, v3)
    return tuple(a.reshape(shape) for a in outs)


def _place():
    x, y, c = lax.axis_index("x"), lax.axis_index("y"), lax.axis_index("c")
    return x, y, c, [(1 - x, y), (x, 1 - y), (1 - x, 1 - y)]


_ANY = pl.BlockSpec(memory_space=pl.ANY)


def _sem_pairs(n):
    return [pltpu.SemaphoreType.DMA((n,)), pltpu.SemaphoreType.DMA((n,))]


GATHER_COPIES = 7


def _gather_phases(shapes, srcs, outs, send_sems, recv_sems):
    nt = len(shapes)
    x, y, c, chips = _place()
    me = 2 * x + y
    sib = (x, y, 1 - c)

    def half(t, chip, hc):
        rh = shapes[t][1] // 2
        return outs[t].at[chip, :, pl.ds(hc * rh, rh), :]

    def copy(t, kk, s_ref, d_ref, to):
        return pltpu.make_async_remote_copy(src_ref=s_ref, dst_ref=d_ref, send_sem=send_sems.at[GATHER_COPIES * t + kk],
                                            recv_sem=recv_sems.at[GATHER_COPIES * t + kk], device_id=to, device_id_type=MESH)

    def first_hop(t, j):
        rh = shapes[t][1] // 2
        cx, cy = chips[j]
        return copy(t, j, srcs[t].at[:, pl.ds(c * rh, rh), :], half(t, me, c), (cx, cy, c))

    def passed(t, j):
        cx, cy = chips[j]
        landed = half(t, 2 * cx + cy, c)
        return copy(t, 3 + j, landed, landed, sib)

    def own(t):
        return copy(t, 6, srcs[t], outs[t].at[me], sib)

    pairs = [(t, j) for j in range(3) for t in range(nt)]

    def start():
        for t, j in pairs:
            first_hop(t, j).start()
        for t in range(nt):
            own(t).start()

    def pass_on():
        for t, j in pairs:
            cx, cy = chips[j]
            landed = half(t, 2 * cx + cy, c)
            copy(t, j, landed, landed, (cx, cy, c)).wait_recv()
            passed(t, j).start()

    def finish():
        for t, j in pairs:
            cx, cy = chips[j]
            theirs = half(t, 2 * cx + cy, 1 - c)
            copy(t, 3 + j, theirs, theirs, sib).wait_recv()
        for t in range(nt):
            own(t).wait()
        for t, j in pairs:
            first_hop(t, j).wait_send()
            passed(t, j).wait_send()

    return start, pass_on, finish


def _gather_chip_shards(tensors, *, name):
    nt = len(tensors)

    def body(*refs):
        for phase in _gather_phases([a.shape for a in tensors], refs[:nt], refs[nt:2 * nt], *refs[2 * nt:]):
            phase()

    return pl.pallas_call(
        body, name=name, in_specs=[_ANY] * nt, out_specs=[_ANY] * nt,
        out_shape=[jax.ShapeDtypeStruct((N_CHIPS,) + a.shape, a.dtype) for a in tensors],
        scratch_shapes=_sem_pairs(GATHER_COPIES * nt),
    )(*tensors)


def _swap_halves(gs, *, name):
    nt = len(gs)

    def body(*refs):
        srcs, outs = refs[:nt], refs[nt:2 * nt]
        send_sems, recv_sems = refs[2 * nt:]
        x, y, c, _ = _place()
        cps = [pltpu.make_async_remote_copy(src_ref=srcs[t].at[:, 1 - c], dst_ref=outs[t], send_sem=send_sems.at[t],
                                            recv_sem=recv_sems.at[t], device_id=(x, y, 1 - c), device_id_type=MESH)
               for t in range(nt)]
        for cp in cps:
            cp.start()
        for cp in cps:
            cp.wait()

    return pl.pallas_call(
        body, name=name, in_specs=[_ANY] * nt, out_specs=[_ANY] * nt,
        out_shape=[jax.ShapeDtypeStruct((N_CHIPS,) + g.shape[2:], g.dtype) for g in gs], scratch_shapes=_sem_pairs(nt),
    )(*gs)


def _scatter_to_chips(blocks, *, name):
    nt = len(blocks)

    def body(*refs):
        srcs, outs = refs[:nt], refs[nt:2 * nt]
        send_sems, recv_sems = refs[2 * nt:]
        x, y, c, chips = _place()
        cps = [pltpu.make_async_remote_copy(src_ref=srcs[t].at[2 * cx + cy], dst_ref=outs[t].at[j],
                                            send_sem=send_sems.at[3 * t + j], recv_sem=recv_sems.at[3 * t + j],
                                            device_id=(cx, cy, c), device_id_type=MESH)
               for j, (cx, cy) in enumerate(chips) for t in range(nt)]
        for cp in cps:
            cp.start()
        for cp in cps:
            cp.wait()

    return pl.pallas_call(
        body, name=name, in_specs=[_ANY] * nt, out_specs=[_ANY] * nt,
        out_shape=[jax.ShapeDtypeStruct((3,) + a.shape[1:], a.dtype) for a in blocks], scratch_shapes=_sem_pairs(3 * nt),
    )(*blocks)


def _join_halves(bufs, *, name):
    nt = len(bufs)

    def body(*refs):
        srcs, outs = refs[:nt], refs[nt:2 * nt]
        send_sems, recv_sems = refs[2 * nt:]
        x, y, c, _ = _place()
        cps = []
        for t in range(nt):
            rh = bufs[t].shape[1] // 2
            rows = pl.ds(c * rh, rh)
            cps.append(pltpu.make_async_remote_copy(src_ref=srcs[t].at[:, rows, :], dst_ref=outs[t].at[:, rows, :],
                                                    send_sem=send_sems.at[t], recv_sem=recv_sems.at[t],
                                                    device_id=(x, y, 1 - c), device_id_type=MESH))
        for cp in cps:
            cp.start()
        for cp in cps:
            cp.wait()

    return pl.pallas_call(
        body, name=name, in_specs=[_ANY] * nt, out_specs=[_ANY] * nt,
        out_shape=[jax.ShapeDtypeStruct(b.shape, b.dtype) for b in bufs], scratch_shapes=_sem_pairs(nt),
        input_output_aliases={t: t for t in range(nt)},
    )(*bufs)


def _add_sibling(g, recv, cidx, chip_idx, *, name):
    _, _, na, rh, cdim = g.shape
    tb = _pick(rh, (256, 128, 64, 32, 16))

    def body(c_ref, k_ref, g_ref, r_ref, s_ref, o_ref):
        tot = g_ref[0, 0] + r_ref[0]
        s_ref[0] = tot.astype(BF16)

        @pl.when(pl.program_id(2) == k_ref[0])
        def _():
            o_ref[...] = tot

    return pl.pallas_call(
        body, name=name,
        grid_spec=pltpu.PrefetchScalarGridSpec(
            num_scalar_prefetch=2, grid=(na, rh // tb, N_CHIPS),
            in_specs=[pl.BlockSpec((1, 1, 1, tb, cdim), lambda a, i, k, c_ref, k_ref: (k, c_ref[0], a, i, 0)),
                      pl.BlockSpec((1, 1, tb, cdim), lambda a, i, k, c_ref, k_ref: (k, a, i, 0))],
            out_specs=[pl.BlockSpec((1, 1, tb, cdim), lambda a, i, k, c_ref, k_ref: (k, a, i, 0)),
                       pl.BlockSpec((1, tb, cdim), lambda a, i, k, c_ref, k_ref: (a, i, 0))]),
        out_shape=[jax.ShapeDtypeStruct((N_CHIPS, na, rh, cdim), BF16), jax.ShapeDtypeStruct((na, rh, cdim), F32)],
        compiler_params=_cparams((_PAR, _PAR, _ARB)),
    )(cidx, chip_idx, g, recv)


def _add_chips(own, recv, cidx, *, name):
    na, rh, cdim = own.shape
    tb = _pick(rh, (256, 128, 64, 32, 16))
    nblk = rh // tb

    def body(c_ref, a_ref, r0_ref, r1_ref, r2_ref, o_ref):
        o_ref[...] = ((a_ref[...] + r0_ref[0].astype(F32)) + r1_ref[0].astype(F32)) + r2_ref[0].astype(F32)

    slot = lambda j: pl.BlockSpec((1, 1, tb, cdim), lambda a, i, c_ref: (j, a, i, 0))
    return pl.pallas_call(
        body, name=name,
        grid_spec=pltpu.PrefetchScalarGridSpec(
            num_scalar_prefetch=1, grid=(na, nblk),
            in_specs=[pl.BlockSpec((1, tb, cdim), lambda a, i, c_ref: (a, i, 0)), slot(0), slot(1), slot(2)],
            out_specs=pl.BlockSpec((1, tb, cdim), lambda a, i, c_ref: (a, c_ref[0] * nblk + i, 0))),
        out_shape=jax.ShapeDtypeStruct((na, 2 * rh, cdim), F32), compiler_params=_cparams((_PAR, _PAR)),
    )(cidx, own, recv, recv, recv)


def _all_reduce_small(v, *, name):
    r, cdim = v.shape

    def body(v_ref, o_ref, buf, send_sems, recv_sems):
        x, y, c, _ = _place()
        me = 4 * x + 2 * y + c
        buf[me] = v_ref[...]
        cps = []
        for p in range(1, N_DEV):
            to = (1 - x if p & 4 else x, 1 - y if p & 2 else y, 1 - c if p & 1 else c)
            cp = pltpu.make_async_remote_copy(src_ref=v_ref, dst_ref=buf.at[me], send_sem=send_sems.at[p - 1],
                                              recv_sem=recv_sems.at[p - 1], device_id=to, device_id_type=MESH)
            cp.start()
            cps.append(cp)
        for p in range(1, N_DEV):
            frm = (4 * x + 2 * y + c) ^ p
            pltpu.make_async_remote_copy(src_ref=v_ref, dst_ref=buf.at[frm], send_sem=send_sems.at[p - 1],
                                         recv_sem=recv_sems.at[p - 1], device_id=(x, y, c), device_id_type=MESH).wait_recv()
        for cp in cps:
            cp.wait_send()
        tot = buf[0]
        for i in range(1, N_DEV):
            tot = tot + buf[i]
        o_ref[...] = tot

    vm = pl.BlockSpec(memory_space=pltpu.VMEM)
    return pl.pallas_call(
        body, name=name, in_specs=[vm], out_specs=vm, out_shape=jax.ShapeDtypeStruct((r, cdim), F32),
        scratch_shapes=[pltpu.VMEM((N_DEV, r, cdim), F32), pltpu.SemaphoreType.DMA((N_DEV - 1,)), pltpu.SemaphoreType.DMA((N_DEV - 1,))],
    )(v)


_SHARDED = ("even_w_in", "even_w_uq", "even_w_ukv", "even_w_out", "odd_w_in", "odd_w_out")
_COL_SHARDED = ("even_w_in", "even_w_uq", "even_w_ukv", "odd_w_in")
PACK_COLS = 1024


def _unshard(name, stacked):
    n, a, b, cc = stacked.shape
    if name in _COL_SHARDED:
        return stacked.transpose(1, 2, 0, 3).reshape(a, b, n * cc)
    return stacked.transpose(1, 0, 2, 3).reshape(a, n * b, cc)


def _chip_halves(name, full):
    a, b, cc = full.shape
    if name in _COL_SHARDED:
        return full.reshape(a, 2, b // 2, N_CHIPS, cc // N_CHIPS).transpose(3, 1, 0, 2, 4)
    return full.reshape(a, N_CHIPS, 2, b // (2 * N_CHIPS), cc).transpose(1, 2, 0, 3, 4)


def _rope_tables(s):
    pos = jnp.arange(s, dtype=F32)

    def ang(d):
        inv = ROPE_THETA ** (-jnp.arange(0, d, 2, dtype=F32) / d)
        a = pos[:, None] * inv[None, :]
        return jnp.cos(a), jnp.sin(a)

    c16, s16 = ang(MLA_ROPE)
    one, zero = jnp.ones((s, KPE_LANE), F32), jnp.zeros((s, KPE_LANE), F32)
    cos_m = jnp.concatenate([one, c16, c16, one[:, :32]], axis=1)
    sin_m = jnp.concatenate([zero, -s16, s16, zero[:, :32]], axis=1)
    c32, s32 = ang(SWA_DIM)
    cos_s = jnp.concatenate([c32, c32, c32, c32], axis=1)
    sin_s = jnp.concatenate([-s32, s32, -s32, s32], axis=1)
    return cos_m, sin_m, cos_s, sin_s


def _even_weights(w_in, w_uq, w_ukv):
    zeros = lambda n: jnp.zeros((D_MODEL, n), w_in.dtype)
    wcq, wckv, wkpe = w_in[:, 0:256], w_in[:, 256:384], w_in[:, 384:416]
    wfq, wfk, wfv = w_in[:, 416:928], w_in[:, 928:1440], w_in[:, 1440:1952]
    wfl, wg = w_in[:, 1952:1960], w_in[:, 1960:2984]
    misc = jnp.concatenate([zeros(KPE_LANE), wkpe, wfl, zeros(LANES - FL_LANE - FOX_HEADS)], axis=1)
    w_in_p = jnp.concatenate([wg, wfq, wfk, wfv, wcq, wckv, misc], axis=1)
    uq = w_uq.reshape(MLA_Q_RANK, MLA_HEADS, MLA_NOPE + MLA_ROPE)
    uq_p = jnp.pad(uq, ((0, 0), (0, 0), (0, HEAD_PAD - MLA_NOPE - MLA_ROPE))).reshape(MLA_Q_RANK, MLA_HEADS * HEAD_PAD)
    ukv = w_ukv.reshape(MLA_KV_RANK, MLA_HEADS, MLA_NOPE + MLA_V)
    uk_p = jnp.pad(ukv[..., :MLA_NOPE], ((0, 0), (0, 0), (0, HEAD_PAD - MLA_NOPE))).reshape(MLA_KV_RANK, MLA_HEADS * HEAD_PAD)
    uv = ukv[..., MLA_NOPE:].reshape(MLA_KV_RANK, MLA_HEADS * MLA_V)
    ukv_p = jnp.concatenate([uk_p, uv], axis=1)
    return w_in_p, w_in_p.T, uq_p, uq_p.T, ukv_p, ukv_p.T


def _even_weight_grads(dw_in_p, duq_p, dukv_p):
    g = dw_in_p
    gate, fq, fk, fv = g[:, E_GATE:E_FQ], g[:, E_FQ:E_FK], g[:, E_FK:E_FV], g[:, E_FV:E_SMALL]
    cq, ckv, misc = g[:, E_SMALL:E_SMALL + 256], g[:, E_SMALL + 256:E_SMALL + 384], g[:, E_SMALL + 384:]
    dw_in = jnp.concatenate([cq, ckv, misc[:, KPE_LANE:KPE_LANE + MLA_ROPE], fq, fk, fv, misc[:, FL_LANE:FL_LANE + FOX_HEADS], gate], axis=1)
    duq = duq_p.reshape(MLA_Q_RANK, MLA_HEADS, HEAD_PAD)[..., :MLA_NOPE + MLA_ROPE].reshape(MLA_Q_RANK, -1)
    nq = MLA_HEADS * HEAD_PAD
    dk = dukv_p[:, :nq].reshape(MLA_KV_RANK, MLA_HEADS, HEAD_PAD)[..., :MLA_NOPE]
    dvv = dukv_p[:, nq:].reshape(MLA_KV_RANK, MLA_HEADS, MLA_V)
    dukv = jnp.concatenate([dk, dvv], axis=-1).reshape(MLA_KV_RANK, -1)
    return dw_in, duq, dukv


def _interleave(w, forward):
    a, b = (SWA_KV_HEADS, SWA_HEADS // SWA_KV_HEADS) if forward else (SWA_HEADS // SWA_KV_HEADS, SWA_KV_HEADS)
    return w.reshape(w.shape[0], a, b, -1).transpose(0, 2, 1, 3).reshape(w.shape[0], -1)


def _odd_weights(w_in, w_out):
    q, k, v, gate = w_in[:, 0:1024], w_in[:, 1024:1152], w_in[:, 1152:1280], w_in[:, 1280:2304]
    w_p = jnp.concatenate([_interleave(gate, True), _interleave(q, True), k, v], axis=1)
    w_out_p = _interleave(w_out.T, True).T
    return w_p, w_p.T, w_out_p, w_out_p.T


def _odd_weight_grads(g, dw_out_p):
    dw_in = jnp.concatenate([_interleave(g[:, O_Q:O_K], False), g[:, O_K:O_V], g[:, O_V:], _interleave(g[:, O_GATE:O_Q], False)], axis=1)
    return dw_in, _interleave(dw_out_p.T, False).T


def kernel(x, even_w_in, even_q_norm, even_w_uq, even_kv_norm, even_w_ukv, even_b_f, even_w_out, even_ln_g, even_ln_b, odd_w_in, odd_sinks, odd_w_out, odd_ln_g, odd_ln_b, loss_target, m_even_w_in, m_even_q_norm, m_even_w_uq, m_even_kv_norm, m_even_w_ukv, m_even_b_f, m_even_w_out, m_even_ln_g, m_even_ln_b, m_odd_w_in, m_odd_sinks, m_odd_w_out, m_odd_ln_g, m_odd_ln_b, v_even_w_in, v_even_q_norm, v_even_w_uq, v_even_kv_norm, v_even_w_ukv, v_even_b_f, v_even_w_out, v_even_ln_g, v_even_ln_b, v_odd_w_in, v_odd_sinks, v_odd_w_out, v_odd_ln_g, v_odd_ln_b):
    weights = dict(even_w_in=even_w_in, even_q_norm=even_q_norm, even_w_uq=even_w_uq, even_kv_norm=even_kv_norm,
                   even_w_ukv=even_w_ukv, even_b_f=even_b_f, even_w_out=even_w_out, even_ln_g=even_ln_g, even_ln_b=even_ln_b,
                   odd_w_in=odd_w_in, odd_sinks=odd_sinks, odd_w_out=odd_w_out, odd_ln_g=odd_ln_g, odd_ln_b=odd_ln_b)
    mom_m = dict(even_w_in=m_even_w_in, even_q_norm=m_even_q_norm, even_w_uq=m_even_w_uq, even_kv_norm=m_even_kv_norm,
                 even_w_ukv=m_even_w_ukv, even_b_f=m_even_b_f, even_w_out=m_even_w_out, even_ln_g=m_even_ln_g, even_ln_b=m_even_ln_b,
                 odd_w_in=m_odd_w_in, odd_sinks=m_odd_sinks, odd_w_out=m_odd_w_out, odd_ln_g=m_odd_ln_g, odd_ln_b=m_odd_ln_b)
    mom_v = dict(even_w_in=v_even_w_in, even_q_norm=v_even_q_norm, even_w_uq=v_even_w_uq, even_kv_norm=v_even_kv_norm,
                 even_w_ukv=v_even_w_ukv, even_b_f=v_even_b_f, even_w_out=v_even_w_out, even_ln_g=v_even_ln_g, even_ln_b=v_even_ln_b,
                 odd_w_in=v_odd_w_in, odd_sinks=v_odd_sinks, odd_w_out=v_odd_w_out, odd_ln_g=v_odd_ln_g, odd_ln_b=v_odd_ln_b)
    names = list(weights)
    xl = x[0]
    tgt = loss_target[0]
    s = xl.shape[0]
    ax, ay, ac = lax.axis_index("x"), lax.axis_index("y"), lax.axis_index("c")
    chip = 2 * ax + ay
    c_idx = ac.astype(jnp.int32).reshape(1)
    chip_idx = chip.astype(jnp.int32).reshape(1)

    ln_odd = jnp.pad(jnp.concatenate([odd_ln_g, odd_ln_b]), ((0, 12), (0, 0)))[None]
    wb = {n: weights[n].astype(BF16) for n in _SHARDED}
    even_mats = [n for n in _SHARDED if n.startswith("even")]
    odd_mats = [n for n in _SHARDED if n.startswith("odd")]
    gathered = _gather_chip_shards([wb[n][0:1] for n in even_mats] + [ln_odd], name="gather_first")
    full = {n: [_unshard(n, gathered[i])[0], None] for i, n in enumerate(even_mats)}
    later = [wb[n][1:2] for n in even_mats] + [wb[n] for n in odd_mats]
    ln_all = gathered[-1][:, 0]
    odd_g_full = ln_all[:, 0:2].transpose(1, 0, 2).reshape(2, D_MODEL)
    odd_b_full = ln_all[:, 2:4].transpose(1, 0, 2).reshape(2, D_MODEL)

    cos_m, sin_m, cos_s, sin_s = _rope_tables(s)
    bf_tiles = jnp.pad(even_b_f, ((0, 0), (FL_LANE, LANES - FL_LANE - FOX_HEADS)))
    sink_tiles = jnp.broadcast_to(_interleave(odd_sinks, True)[:, :, None, None], (2, SWA_HEADS, 1, LANES))
    mla_scale = (MLA_NOPE + MLA_ROPE) ** -0.5
    fox_scale = FOX_DIM ** -0.5
    mla_pairs, fox_pairs = MLA_HEADS // 2, FOX_HEADS // 2

    saved = []
    x_f, x_b = xl, xl.astype(BF16)
    for layer in range(DEPTH):
        j = layer // 2
        ln = f"L{layer}"
        if layer % 2 == 0:
            w_in_p, w_in_t, uq_p, uq_t, ukv_p, ukv_t = _even_weights(full["even_w_in"][j], full["even_w_uq"][j], full["even_w_ukv"][j])
            w_out, w_out_t = full["even_w_out"][j], full["even_w_out"][j].T
            qg, kg, bft = even_q_norm[j][None], even_kv_norm[j][None], bf_tiles[j][None]
            h, hb = _mm(x_b, w_in_p, out_dtype=F32, name=ln + "_in", also_bf16=True)
            q, k, v, qn, kvn, logf = _even_mid_fwd(h, qg, kg, bft, uq_p, ukv_p, cos_m, sin_m, name=ln + "_mid")
            cum = _cumsum(logf, reverse=False, name=ln + "_cum")[:, FL_LANE:FL_LANE + FOX_HEADS].T
            crow = cum[:, None, :]
            if layer == 0:
                o_mla, lse_mla, rest = _attn_fwd((q, 0), (k, 0), (v, 0), None, dk=HEAD_PAD, npairs=mla_pairs, scale=mla_scale,
                                                 name=ln + "_mla", gather=later)
                for i, n in enumerate(even_mats):
                    full[n][1] = _unshard(n, rest[i])[0]
                for i, n in enumerate(odd_mats):
                    full[n] = _unshard(n, rest[len(even_mats) + i])
            else:
                o_mla, lse_mla = _attn_fwd((q, 0), (k, 0), (v, 0), None, dk=HEAD_PAD, npairs=mla_pairs, scale=mla_scale, name=ln + "_mla")
            o_fox, lse_fox = _attn_fwd((hb, E_FQ), (hb, E_FK), (hb, E_FV), crow, dk=FOX_DIM, npairs=fox_pairs,
                                       scale=fox_scale, name=ln + "_fox")
            o_parts = [o_mla, o_fox]
            g_ln, b_ln = even_ln_g[j][None], even_ln_b[j][None]
            x_n, x_nb, z, xh, rstd = _out_fwd(o_parts, h, x_f, w_out, g_ln, b_ln, name=ln + "_out")
            saved.append(dict(h=h, hb=hb, x_b=x_b, qn=qn, kvn=kvn, q=q, k=k, v=v, crow=crow,
                              o_mla=o_mla, o_fox=o_fox, lse_mla=lse_mla, lse_fox=lse_fox, o_parts=o_parts, z=z, xh=xh, rstd=rstd,
                              w_in_t=w_in_t, uq_t=uq_t, ukv_t=ukv_t, w_out_t=w_out_t, qg=qg, kg=kg, bft=bft, g_ln=g_ln))
        else:
            w_in_p, w_in_t, w_out, w_out_t = _odd_weights(full["odd_w_in"][j], full["odd_w_out"][j])
            h = _mm(x_b, w_in_p, out_dtype=F32, name=ln + "_in")
            q, k, v = _odd_rope(h, cos_s, sin_s, name=ln + "_rope")
            kp = jnp.pad(k, ((WINDOW, 0), (0, 0)))
            vp = jnp.pad(v, ((WINDOW, 0), (0, 0)))
            o_t, lse = _swa_fwd(q, kp, vp, sink_tiles[j], name=ln + "_swa")
            o = o_t.T
            g_ln, b_ln = odd_g_full[j][None], odd_b_full[j][None]
            x_n, x_nb, z, xh, rstd = _out_fwd([o], h, x_f, w_out, g_ln, b_ln, name=ln + "_out")
            saved.append(dict(h=h, x_b=x_b, q=q, kp=kp, vp=vp, lse=lse, o_t=o_t, o_parts=[o], z=z, xh=xh, rstd=rstd,
                              w_in_t=w_in_t, w_out_t=w_out_t, g_ln=g_ln))
        x_f, x_b = x_n, x_nb

    dxn, loss_tile = _loss_grad(x_f, tgt, name="loss")

    grads = {n: [None, None] for n in names}
    for layer in reversed(range(DEPTH)):
        j = layer // 2
        ln = f"L{layer}"
        sv = saved[layer]
        dr, dy, do, dgate, dg_ln, db_ln = _out_bwd(dxn, sv["xh"], sv["rstd"], sv["g_ln"], sv["w_out_t"], sv["o_parts"], sv["h"], name=ln + "_outb")
        dw_out = _mm(sv["z"].T, dy, out_dtype=F32, name=ln + "_dwout")
        if layer % 2 == 0:
            hb = sv["hb"]
            dq, dk, dv = _attn_bwd((sv["q"], 0), (sv["k"], 0), (sv["v"], 0), (sv["o_mla"], 0), (do, 0), sv["lse_mla"], None,
                                   dk=HEAD_PAD, npairs=mla_pairs, scale=mla_scale, dq_dtype=F32, name=ln + "_mlab")
            dfq, dfk, dfv, dcrow, dccol = _attn_bwd((hb, E_FQ), (hb, E_FK), (hb, E_FV), (sv["o_fox"], 0), (do, MLA_HEADS * MLA_V),
                                                    sv["lse_fox"], sv["crow"], dk=FOX_DIM, npairs=fox_pairs,
                                                    scale=fox_scale, dq_dtype=BF16, name=ln + "_foxb")
            dcum = jnp.pad((dcrow[:, 0, :] + dccol[:, :, 0]).T, ((0, 0), (FL_LANE, LANES - FL_LANE - FOX_HEADS)))
            dlogf = _cumsum(dcum, reverse=True, name=ln + "_cumb")
            dh_small, dq_pre, dqg, dkg, dbf = _even_mid_bwd(
                sv["h"], dq, dk, dv, dlogf, sv["qg"], sv["kg"], sv["bft"], sv["uq_t"], sv["ukv_t"], cos_m, sin_m, name=ln + "_midb")
            duq_p = _mm(sv["qn"].T, dq_pre, out_dtype=F32, name=ln + "_dwuq")
            dkv_cat = jnp.concatenate([dk.astype(BF16), dv], axis=1)
            dukv_p = _mm(sv["kvn"].T, dkv_cat, out_dtype=F32, name=ln + "_dwukv")
            dh = jnp.concatenate([dgate, dfq, dfk, dfv, dh_small], axis=1)
            dw_in_p = _mm(sv["x_b"].T, dh, out_dtype=F32, name=ln + "_dwin")
            dw_in, duq, dukv = _even_weight_grads(dw_in_p, duq_p, dukv_p)
            for n, val in (("even_w_in", dw_in), ("even_w_uq", duq), ("even_w_ukv", dukv), ("even_w_out", dw_out),
                           ("even_q_norm", dqg[0]), ("even_kv_norm", dkg[0]), ("even_b_f", dbf[0, FL_LANE:FL_LANE + FOX_HEADS]),
                           ("even_ln_g", dg_ln[0]), ("even_ln_b", db_ln[0])):
                grads[n][j] = val
        else:
            dq_t, dkp, dvp, dsink = _swa_bwd(sv["q"], sv["kp"], sv["vp"], sv["kp"].T, sink_tiles[j], sv["o_t"], do, do.T, sv["lse"],
                                             name=ln + "_swab")
            dq_r, dk_r = _odd_rope_bwd(dq_t.T, dkp[WINDOW:], cos_s, sin_s, name=ln + "_ropeb")
            dh = jnp.concatenate([dgate, dq_r, dk_r, dvp[WINDOW:].astype(BF16)], axis=1)
            dw_in_p = _mm(sv["x_b"].T, dh, out_dtype=F32, name=ln + "_dwin")
            dw_in, dw_out = _odd_weight_grads(dw_in_p, dw_out)
            for n, val in (("odd_w_in", dw_in), ("odd_w_out", dw_out), ("odd_sinks", _interleave(dsink[None, :, 0, 0], False)[0]),
                           ("odd_ln_g", dg_ln[0]), ("odd_ln_b", db_ln[0])):
                grads[n][j] = val
        dxn = _mm(dh, sv["w_in_t"], out_dtype=F32, name=ln + "_dx", res=dr, res_scale=ALPHA)
    grad_x = dxn[None]
    grads = {n: jnp.stack(v) for n, v in grads.items()}

    gp = [_chip_halves(n, grads[n]) for n in _SHARDED]
    from_sib = _swap_halves(gp, name="grad_swap")
    pair = [_add_sibling(g, r, c_idx, chip_idx, name="grad_add_sibling_" + n) for n, g, r in zip(_SHARDED, gp, from_sib)]
    from_chips = _scatter_to_chips([p[0] for p in pair], name="grad_scatter")
    mine = [_add_chips(p[1], r, c_idx, name="grad_add_chips_" + n) for n, p, r in zip(_SHARDED, pair, from_chips)]
    gshard = dict(zip(_SHARDED, _join_halves(mine, name="grad_join")))

    small = [n for n in names if n not in _SHARDED]
    sv_flat = jnp.concatenate([grads[n].reshape(-1) for n in small] + [loss_tile[0, :1]])
    sv_real = sv_flat.shape[0]
    sv_rows = -(-sv_real // (PACK_COLS * 8)) * 8
    sv_sum = _all_reduce_small(jnp.pad(sv_flat, (0, sv_rows * PACK_COLS - sv_real)).reshape(sv_rows, PACK_COLS), name="small_all_reduce").reshape(-1)
    off = 0
    for n in small:
        size = int(np.prod(grads[n].shape))
        gfull = sv_sum[off:off + size].reshape(grads[n].shape)
        off += size
        if n in ("odd_ln_g", "odd_ln_b"):
            gfull = lax.dynamic_slice_in_dim(gfull, chip * (D_MODEL // N_CHIPS), D_MODEL // N_CHIPS, axis=1)
        gshard[n] = gfull
    loss = sv_sum[off]

    deltas, new_m, new_v = {}, {}, {}
    for n in names:
        deltas[n], new_m[n], new_v[n] = _adamw(weights[n], gshard[n], mom_m[n], mom_v[n], name="adamw_" + n)
    return (loss, grad_x, *[gshard[n] for n in names], *[deltas[n] for n in names],
            *[new_m[n] for n in names], *[new_v[n] for n in names])
```

```python
import jax
import jax.numpy as jnp
import numpy as np
from jax import lax
from jax.experimental import pallas as pl
from jax.experimental.pallas import tpu as pltpu

F32 = jnp.float32
BF16 = jnp.bfloat16
MESH = pl.DeviceIdType.MESH

D_MODEL = 1024
DEPTH = 4
ROPE_THETA = 10000.0
MLA_HEADS, MLA_NOPE, MLA_ROPE, MLA_V = 8, 64, 32, 64
MLA_Q_RANK, MLA_KV_RANK = 256, 128
FOX_HEADS, FOX_DIM = 8, 64
SWA_HEADS, SWA_KV_HEADS, SWA_DIM, WINDOW = 16, 2, 64, 128
RMS_EPS = 1e-6
LN_EPS = 1e-5
ALPHA = (2 * DEPTH) ** 0.25
EVEN_IN = 2984
ODD_IN = 2304
ADAM_LR, ADAM_B1, ADAM_B2, ADAM_EPS, ADAM_WD, ADAM_STEP = 0.001, 0.9, 0.999, 1e-08, 0.01, 10

LANES = 128
HALF = LANES // 2
HEAD_PAD = 128
N_CHIPS = 4
N_DEV = 8
E_GATE, E_FQ, E_FK, E_FV, E_SMALL = 0, 1024, 1536, 2048, 2560
E_PAD_IN = 3072
KPE_LANE = 64
FL_LANE = 96
O_GATE, O_Q, O_K, O_V = 0, 1024, 2048, 2176

_ARB = "arbitrary"
_PAR = "parallel"


def _cparams(sem):
    return pltpu.CompilerParams(dimension_semantics=sem)


def _pick(n, cands):
    for c in cands:
        if n % c == 0:
            return c
    return n


RESIDENT_B_BYTES = 8 << 20


def _mm_resident(a, b, *, out_dtype, name, res, res_scale, also_bf16):
    m, k = a.shape
    _, n = b.shape
    tm = _pick(m, (256, 128))
    cn = _pick(n, (512, 384, 256, 128))
    dtypes = [out_dtype, BF16] if also_bf16 else [out_dtype]

    def body(*refs):
        a_ref, b_ref = refs[:2]
        r_ref = refs[2] if res is not None else None
        outs = refs[3 if res is not None else 2:]
        av = a_ref[...].astype(BF16)
        for c0 in range(0, n, cn):
            r = jnp.dot(av, b_ref[:, c0:c0 + cn].astype(BF16), preferred_element_type=F32)
            if res is not None:
                r = r + res_scale * r_ref[:, c0:c0 + cn]
            for o_ref in outs:
                o_ref[:, c0:c0 + cn] = r.astype(o_ref.dtype)

    rows = lambda w: pl.BlockSpec((tm, w), lambda i: (i, 0))
    in_specs = [rows(k), pl.BlockSpec((k, n), lambda i: (0, 0))] + ([rows(n)] if res is not None else [])
    out = pl.pallas_call(
        body, name=name, grid=(m // tm,), in_specs=in_specs, out_specs=[rows(n)] * len(dtypes),
        out_shape=[jax.ShapeDtypeStruct((m, n), d) for d in dtypes], compiler_params=_cparams((_PAR,)),
    )(*([a, b] + ([res] if res is not None else [])))
    return out if also_bf16 else out[0]


def _mm(a, b, *, out_dtype, name, res=None, res_scale=1.0, also_bf16=False):
    m, k = a.shape
    _, n = b.shape
    if b.size * b.dtype.itemsize <= RESIDENT_B_BYTES:
        return _mm_resident(a, b, out_dtype=out_dtype, name=name, res=res, res_scale=res_scale, also_bf16=also_bf16)
    tm = _pick(m, (512, 256, 128))
    tn = _pick(n, (1024, 768, 512, 384, 256, 128))
    tk = _pick(k, (1024, 768, 512, 256, 128))
    nk = k // tk

    def body(*refs):
        refs = list(refs)
        a_ref, b_ref = refs[:2]
        r_ref = refs[2] if res is not None else None
        acc_ref = refs[-1]
        outs = refs[3 if res is not None else 2:-1]
        kk = pl.program_id(2)

        @pl.when(kk == 0)
        def _():
            acc_ref[...] = jnp.zeros_like(acc_ref)

        acc_ref[...] += jnp.dot(a_ref[...].astype(BF16), b_ref[...].astype(BF16), preferred_element_type=F32)

        @pl.when(kk == nk - 1)
        def _():
            r = acc_ref[...]
            if res is not None:
                r = r + res_scale * r_ref[...]
            for o_ref in outs:
                o_ref[...] = r.astype(o_ref.dtype)

    in_specs = [pl.BlockSpec((tm, tk), lambda i, j, kk: (i, kk)), pl.BlockSpec((tk, tn), lambda i, j, kk: (kk, j))]
    args = [a, b]
    if res is not None:
        in_specs.append(pl.BlockSpec((tm, tn), lambda i, j, kk: (i, j)))
        args.append(res)
    ospec = pl.BlockSpec((tm, tn), lambda i, j, kk: (i, j))
    dtypes = [out_dtype, BF16] if also_bf16 else [out_dtype]
    out = pl.pallas_call(
        body, name=name, grid=(m // tm, n // tn, nk), in_specs=in_specs,
        out_specs=[ospec] * len(dtypes), out_shape=[jax.ShapeDtypeStruct((m, n), d) for d in dtypes],
        scratch_shapes=[pltpu.VMEM((tm, tn), F32)],
        compiler_params=_cparams((_PAR, _PAR, _ARB)),
    )(*args)
    return out if also_bf16 else out[0]


def _rope_tile(t, cos, sin, half):
    w = t.shape[-1]
    lane = lax.broadcasted_iota(jnp.int32, t.shape, 1)
    first = (lane % (2 * half)) < half
    sw = jnp.where(first, pltpu.roll(t, w - half, 1), pltpu.roll(t, half, 1))
    return t * cos + sw * sin


def _sigmoid(x):
    return 1.0 / (1.0 + jnp.exp(-x))


def _lane_mask(shape, lo, hi):
    lane = lax.broadcasted_iota(jnp.int32, shape, 1)
    return (lane >= lo) & (lane < hi)


def _rms(x, g):
    r = lax.rsqrt(jnp.mean(x * x, axis=-1, keepdims=True) + RMS_EPS)
    u = x * r
    return u, r, u * g


def _rms_bwd(dy, u, r, g):
    dyg = dy * g
    dx = r * (dyg - u * jnp.mean(dyg * u, axis=-1, keepdims=True))
    return dx, jnp.sum(dy * u, axis=0, keepdims=True)


def _even_mid_fwd(h, qg, kg, bf_tile, w_uq, w_ukv, cos, sin, *, name):
    s = h.shape[0]
    tb = _pick(s, (512, 256, 128))
    nq = MLA_HEADS * HEAD_PAD

    def body(h_ref, qg_ref, kg_ref, bf_ref, wuq_ref, wukv_ref, cos_ref, sin_ref,
             q_ref, k_ref, v_ref, qn_ref, kvn_ref, lf_ref):
        hb = h_ref[...]
        cq, ckv, misc = hb[:, :MLA_Q_RANK], hb[:, MLA_Q_RANK:MLA_Q_RANK + MLA_KV_RANK], hb[:, MLA_Q_RANK + MLA_KV_RANK:]
        cs, sn = cos_ref[...], sin_ref[...]
        _, _, qn = _rms(cq, qg_ref[...])
        qn = qn.astype(BF16)
        qn_ref[...] = qn
        q = jnp.dot(qn, wuq_ref[...], preferred_element_type=F32)
        _, _, kvn = _rms(ckv, kg_ref[...])
        kvn = kvn.astype(BF16)
        kvn_ref[...] = kvn
        kv = jnp.dot(kvn, wukv_ref[...], preferred_element_type=F32)
        kpe = jnp.where(_lane_mask(misc.shape, KPE_LANE, KPE_LANE + MLA_ROPE), _rope_tile(misc, cs, sn, MLA_ROPE // 2), 0.0)
        for hd in range(MLA_HEADS):
            sl = slice(hd * HEAD_PAD, (hd + 1) * HEAD_PAD)
            q_ref[:, sl] = _rope_tile(q[:, sl], cs, sn, MLA_ROPE // 2).astype(BF16)
            k_ref[:, sl] = (kv[:, sl] + kpe).astype(BF16)
        v_ref[...] = kv[:, nq:].astype(BF16)
        xf = misc + bf_ref[...]
        logf = jnp.minimum(xf, 0.0) - jnp.log(1.0 + jnp.exp(-jnp.abs(xf)))
        lf_ref[...] = jnp.where(_lane_mask(misc.shape, FL_LANE, FL_LANE + FOX_HEADS), logf, 0.0)

    full = lambda a: pl.BlockSpec(a.shape, lambda i: (0,) * a.ndim)
    rows = lambda w, c=0: pl.BlockSpec((tb, w), lambda i: (i, c))
    return pl.pallas_call(
        body, name=name, grid=(s // tb,),
        in_specs=[rows(512, E_SMALL // 512), full(qg), full(kg), full(bf_tile), full(w_uq), full(w_ukv), rows(LANES), rows(LANES)],
        out_specs=[rows(nq), rows(nq), rows(MLA_HEADS * MLA_V), rows(MLA_Q_RANK), rows(MLA_KV_RANK), rows(LANES)],
        out_shape=[jax.ShapeDtypeStruct((s, nq), BF16), jax.ShapeDtypeStruct((s, nq), BF16),
                   jax.ShapeDtypeStruct((s, MLA_HEADS * MLA_V), BF16), jax.ShapeDtypeStruct((s, MLA_Q_RANK), BF16),
                   jax.ShapeDtypeStruct((s, MLA_KV_RANK), BF16), jax.ShapeDtypeStruct((s, LANES), F32)],
        compiler_params=_cparams((_PAR,)),
    )(h, qg, kg, bf_tile, w_uq, w_ukv, cos, sin)


def _even_mid_bwd(h, dq, dk, dv, dlogf, qg, kg, bf_tile, w_uq_t, w_ukv_t, cos, sin, *, name):
    s = h.shape[0]
    tb = _pick(s, (256, 128))
    nq = MLA_HEADS * HEAD_PAD

    def body(h_ref, dq_ref, dk_ref, dv_ref, dlf_ref, qg_ref, kg_ref, bf_ref, wuqt_ref, wukvt_ref, cos_ref, sin_ref,
             dh_ref, dqp_ref, dqg_ref, dkg_ref, dbf_ref):
        @pl.when(pl.program_id(0) == 0)
        def _():
            dqg_ref[...] = jnp.zeros_like(dqg_ref)
            dkg_ref[...] = jnp.zeros_like(dkg_ref)
            dbf_ref[...] = jnp.zeros_like(dbf_ref)

        hb = h_ref[...]
        cq, ckv, misc = hb[:, :MLA_Q_RANK], hb[:, MLA_Q_RANK:MLA_Q_RANK + MLA_KV_RANK], hb[:, MLA_Q_RANK + MLA_KV_RANK:]
        cs, sn = cos_ref[...], -sin_ref[...]
        dkpe = jnp.zeros(misc.shape, F32)
        for hd in range(MLA_HEADS):
            sl = slice(hd * HEAD_PAD, (hd + 1) * HEAD_PAD)
            dqp_ref[:, sl] = _rope_tile(dq_ref[:, sl], cs, sn, MLA_ROPE // 2).astype(BF16)
            dkpe = dkpe + dk_ref[:, sl]
        dqn = jnp.dot(dqp_ref[...], wuqt_ref[...], preferred_element_type=F32)
        uq, rq, _ = _rms(cq, qg_ref[...])
        dcq, dqg = _rms_bwd(dqn, uq, rq, qg_ref[...])
        dqg_ref[...] += dqg
        dkv = jnp.concatenate([dk_ref[...].astype(BF16), dv_ref[...]], axis=1)
        dkvn = jnp.dot(dkv, wukvt_ref[...], preferred_element_type=F32)
        uk, rk, _ = _rms(ckv, kg_ref[...])
        dckv, dkg = _rms_bwd(dkvn, uk, rk, kg_ref[...])
        dkg_ref[...] += dkg
        dmisc = jnp.where(_lane_mask(misc.shape, KPE_LANE, KPE_LANE + MLA_ROPE), _rope_tile(dkpe, cs, sn, MLA_ROPE // 2), 0.0)
        dfl = jnp.where(_lane_mask(misc.shape, FL_LANE, FL_LANE + FOX_HEADS), dlf_ref[...] * _sigmoid(-(misc + bf_ref[...])), 0.0)
        dbf_ref[...] += jnp.sum(dfl, axis=0, keepdims=True)
        dh_ref[:, :MLA_Q_RANK] = dcq.astype(BF16)
        dh_ref[:, MLA_Q_RANK:MLA_Q_RANK + MLA_KV_RANK] = dckv.astype(BF16)
        dh_ref[:, MLA_Q_RANK + MLA_KV_RANK:] = (dmisc + dfl).astype(BF16)

    full = lambda a: pl.BlockSpec(a.shape, lambda i: (0,) * a.ndim)
    rows = lambda w, c=0: pl.BlockSpec((tb, w), lambda i: (i, c))
    return pl.pallas_call(
        body, name=name, grid=(s // tb,),
        in_specs=[rows(512, E_SMALL // 512), rows(nq), rows(nq), rows(MLA_HEADS * MLA_V), rows(LANES), full(qg), full(kg),
                  full(bf_tile), full(w_uq_t), full(w_ukv_t), rows(LANES), rows(LANES)],
        out_specs=[rows(512), rows(nq), full(qg), full(kg), full(bf_tile)],
        out_shape=[jax.ShapeDtypeStruct((s, 512), BF16), jax.ShapeDtypeStruct((s, nq), BF16),
                   jax.ShapeDtypeStruct(qg.shape, F32), jax.ShapeDtypeStruct(kg.shape, F32),
                   jax.ShapeDtypeStruct(bf_tile.shape, F32)],
        compiler_params=_cparams((_ARB,)),
    )(h, dq, dk, dv, dlogf, qg, kg, bf_tile, w_uq_t, w_ukv_t, cos, sin)


def _cumsum(x, *, reverse, name):
    s = x.shape[0]
    tb = _pick(s, (512, 256, 128))
    nb = s // tb

    def body(x_ref, o_ref, carry_ref):
        @pl.when(pl.program_id(0) == 0)
        def _():
            carry_ref[...] = jnp.zeros_like(carry_ref)

        xv = x_ref[...]
        r = lax.broadcasted_iota(jnp.int32, (tb, tb), 0)
        c = lax.broadcasted_iota(jnp.int32, (tb, tb), 1)
        tri = jnp.where((c >= r) if reverse else (c <= r), 1.0, 0.0).astype(BF16)
        hi = xv.astype(BF16)
        r1 = xv - hi.astype(F32)
        mid = r1.astype(BF16)
        lo = (r1 - mid.astype(F32)).astype(BF16)
        cs = (jnp.dot(tri, hi, preferred_element_type=F32) + jnp.dot(tri, mid, preferred_element_type=F32)
              + jnp.dot(tri, lo, preferred_element_type=F32)) + carry_ref[...]
        o_ref[...] = cs
        carry_ref[...] = cs[0:1, :] if reverse else cs[tb - 1:tb, :]

    imap = (lambda i: (nb - 1 - i, 0)) if reverse else (lambda i: (i, 0))
    return pl.pallas_call(
        body, name=name, grid=(nb,), in_specs=[pl.BlockSpec((tb, LANES), imap)],
        out_specs=pl.BlockSpec((tb, LANES), imap), out_shape=jax.ShapeDtypeStruct(x.shape, F32),
        scratch_shapes=[pltpu.VMEM((1, LANES), F32)], compiler_params=_cparams((_ARB,)),
    )(x)


_NT = (((1,), (1,)), ((), ()))
_TN = (((0,), (0,)), ((), ()))


def _head_sel(x, h, dk):
    if dk == LANES:
        return x[:, h * LANES:(h + 1) * LANES]
    return jnp.where(_lane_mask(x.shape, h * HALF, (h + 1) * HALF), x, jnp.zeros_like(x))


def _head_other(x, h, dk):
    return x[:, h * LANES:(h + 1) * LANES] if dk == LANES else x


def _pair(parts, dk=HALF):
    if dk == LANES:
        return jnp.concatenate(parts, axis=1)
    return jnp.where(_lane_mask(parts[0].shape, 0, HALF), parts[0], parts[1])


def _attn_fwd(q, k, v, crow, *, dk, npairs, scale, name, gather=()):
    (qa, qo), (ka, ko), (va, vo) = q, k, v
    s = qa.shape[0]
    wq = 2 * dk
    t = _pick(s, (512, 256, 128))
    nb = s // t
    bias = crow is not None
    ng = len(gather)

    def body(*refs):
        n_in = 3 + bias
        q_ref, k_ref, v_ref = refs[:3]
        cr_ref = refs[3] if bias else None
        o_ref, lse_ref = refs[n_in + ng:n_in + ng + 2]
        qi = pl.program_id(1)
        if ng:
            phases = _gather_phases([a.shape for a in gather], refs[n_in:n_in + ng], refs[n_in + ng + 2:n_in + 2 * ng + 2],
                                    *refs[n_in + 2 * ng + 2:])
            pair = pl.program_id(0)
            pl.when((pair == 0) & (qi == 0))(phases[0])
            pl.when((pair == npairs - 1) & (qi == 0))(phases[1])
        qb = q_ref[...]
        qs = [_head_sel(qb, h, dk) for h in range(2)]

        def scores(j):
            rows = pl.ds(pl.multiple_of(j * t, t), t)
            kb = k_ref[rows, :]
            out = []
            for h in range(2):
                sc = lax.dot_general(qs[h], _head_other(kb, h, dk), _NT, preferred_element_type=F32) * scale
                out.append(sc - cr_ref[h, j] if bias else sc)
            return tuple(out)

        def update(scs, j, state):
            rows = pl.ds(pl.multiple_of(j * t, t), t)
            vb = v_ref[rows, :]
            soft = []
            for h in range(2):
                m, l, _ = state[3 * h:3 * h + 3]
                sc = scs[h]
                m_new = jnp.maximum(m, jnp.max(sc, axis=1, keepdims=True))
                a = jnp.exp(m - m_new)
                p = jnp.exp(sc - m_new)
                soft.append((m_new, a * l + jnp.sum(p, axis=1, keepdims=True), a, p.astype(BF16)))
            new = []
            for h in range(2):
                m_new, l, a, p = soft[h]
                new += [m_new, l, a * state[3 * h + 2] + jnp.dot(p, vb, preferred_element_type=F32)]
            return tuple(new)

        one = (jnp.full((t, 1), -jnp.inf, F32), jnp.zeros((t, 1), F32), jnp.zeros((t, LANES), F32))
        state = lax.fori_loop(0, qi, lambda j, st: update(scores(j), j, st), one + one)
        row = lax.broadcasted_iota(jnp.int32, (t, t), 0)
        col = lax.broadcasted_iota(jnp.int32, (t, t), 1)
        diag = tuple(jnp.where(col <= row, sc, -jnp.inf) for sc in scores(qi))
        state = update(diag, qi, state)
        outs = []
        for h in range(2):
            m, l, acc = state[3 * h:3 * h + 3]
            outs.append(acc / l)
            lse_ref[h] = m + jnp.log(l)
        o_ref[...] = _pair(outs)
        if ng:
            pl.when((pair == npairs - 1) & (qi == nb - 1))(phases[2])

    in_specs = [pl.BlockSpec((t, wq), lambda p, i: (i, qo // wq + p)), pl.BlockSpec((s, wq), lambda p, i: (0, ko // wq + p)),
                pl.BlockSpec((s, LANES), lambda p, i: (0, vo // LANES + p))]
    args = [qa, ka, va]
    if bias:
        in_specs.append(pl.BlockSpec((2, nb, 1, t), lambda p, i: (p, 0, 0, 0)))
        args.append(crow.reshape(2 * npairs, nb, 1, t))
    out = pl.pallas_call(
        body, name=name, grid=(npairs, nb), in_specs=in_specs + [_ANY] * ng,
        out_specs=[pl.BlockSpec((t, LANES), lambda p, i: (i, p)), pl.BlockSpec((2, t, 1), lambda p, i: (p, i, 0))] + [_ANY] * ng,
        out_shape=[jax.ShapeDtypeStruct((s, npairs * LANES), F32), jax.ShapeDtypeStruct((2 * npairs, s, 1), F32)]
        + [jax.ShapeDtypeStruct((N_CHIPS,) + a.shape, a.dtype) for a in gather],
        scratch_shapes=_sem_pairs(GATHER_COPIES * ng) if ng else [],
        compiler_params=_cparams((_ARB, _ARB) if ng else (_PAR, _PAR)),
    )(*args, *gather)
    return (out[0], out[1], out[2:]) if ng else (out[0], out[1])


def _attn_bwd(q, k, v, o, do, lse, crow, *, dk, npairs, scale, dq_dtype, name, scatter=()):
    (qa, qo), (ka, ko), (va, vo), (oa, oo), (da, do_o) = q, k, v, o, do
    s = qa.shape[0]
    wq = 2 * dk
    t = _pick(s, (512, 256, 128))
    nb = s // t
    bias = crow is not None
    ns = len(scatter)

    def body(*refs):
        n_in, n_out = 6 + bias, 3 + 2 * bias
        q_ref, k_ref, v_ref, o_ref, do_ref, lse_ref = refs[:6]
        outs = refs[n_in + ns:n_in + ns + n_out]
        scr = refs[n_in + 2 * ns + n_out:]
        dq_ref, dk_ref, dv_ref = outs[:3]
        dq_s, dk_s, dv_s = scr[:3]
        if bias:
            cr_ref, dc_ref, dcc_ref, dc_s = refs[6], outs[3], outs[4], scr[3]
        ki, qi = pl.program_id(1), pl.program_id(2)
        if ns:
            send_off, finish = _scatter_phases(ns, refs[n_in:n_in + ns], refs[n_in + ns + n_out:n_in + 2 * ns + n_out], *scr[3 + bias:])
            pair = pl.program_id(0)
            pl.when((pair == 0) & (ki == 0) & (qi == 0))(send_off)

        @pl.when((ki == 0) & (qi == 0))
        def _():
            dq_s[...] = jnp.zeros_like(dq_s)
            if bias:
                dcc_ref[...] = jnp.zeros_like(dcc_ref)

        @pl.when(qi == ki)
        def _():
            dk_s[...] = jnp.zeros_like(dk_s)
            dv_s[...] = jnp.zeros_like(dv_s)
            if bias:
                dc_s[...] = jnp.zeros_like(dc_s)

        def block(on_diagonal):
            qb, kb, vb, dob, ob = q_ref[...], k_ref[...], v_ref[...], do_ref[...], o_ref[...]
            rows = pl.ds(pl.multiple_of(qi * t, t), t)
            row = lax.broadcasted_iota(jnp.int32, (t, t), 0) + qi * t
            col = lax.broadcasted_iota(jnp.int32, (t, t), 1) + ki * t
            fed = []
            for h in range(2):
                kh, doh = _head_other(kb, h, dk), _head_sel(dob, h, HALF)
                sc = lax.dot_general(_head_sel(qb, h, dk), kh, _NT, preferred_element_type=F32)
                dp = lax.dot_general(doh, vb, _NT, preferred_element_type=F32)
                fed.append((kh, doh, sc, dp))
            mid = []
            for h in range(2):
                kh, doh, sc, dp = fed[h]
                sc = sc * scale
                if bias:
                    sc = sc - cr_ref[h]
                if on_diagonal:
                    sc = jnp.where(col <= row, sc, -jnp.inf)
                p = jnp.exp(sc - lse_ref[h])
                delta = jnp.sum(doh.astype(F32) * ob, axis=1, keepdims=True)
                ds = p * (dp - delta)
                if bias:
                    dc_s[h] -= jnp.sum(ds, axis=0, keepdims=True)
                    dcc_ref[h, rows, :] += jnp.sum(ds, axis=1, keepdims=True)
                mid.append((kh, p.astype(BF16), (ds * scale).astype(BF16)))
            dq_parts, dk_parts, dv_parts = [], [], []
            for h in range(2):
                kh, pb, dsb = mid[h]
                dv_parts.append(lax.dot_general(pb, dob, _TN, preferred_element_type=F32))
                dk_parts.append(lax.dot_general(dsb, _head_other(qb, h, dk), _TN, preferred_element_type=F32))
                dq_parts.append(jnp.dot(dsb, kh, preferred_element_type=F32))
            dv_s[...] += _pair(dv_parts)
            dk_s[...] += _pair(dk_parts, dk)
            dq_s[rows, :] += _pair(dq_parts, dk)

        pl.when(qi > ki)(lambda: block(False))
        pl.when(qi == ki)(lambda: block(True))

        @pl.when(qi == nb - 1)
        def _():
            dk_ref[...] = dk_s[...].astype(dk_ref.dtype)
            dv_ref[...] = dv_s[...].astype(dv_ref.dtype)
            if bias:
                dc_ref[...] = dc_s[...]

        @pl.when((ki == nb - 1) & (qi == nb - 1))
        def _():
            dq_ref[...] = dq_s[...].astype(dq_ref.dtype)

        if ns:
            pl.when((pair == npairs - 1) & (ki == nb - 1) & (qi == nb - 1))(finish)

    qrow = lambda p, j, i: jnp.maximum(i, j)
    in_specs = [pl.BlockSpec((t, wq), lambda p, j, i: (qrow(p, j, i), qo // wq + p)),
                pl.BlockSpec((t, wq), lambda p, j, i: (j, ko // wq + p)),
                pl.BlockSpec((t, LANES), lambda p, j, i: (j, vo // LANES + p)),
                pl.BlockSpec((t, LANES), lambda p, j, i: (qrow(p, j, i), oo // LANES + p)),
                pl.BlockSpec((t, LANES), lambda p, j, i: (qrow(p, j, i), do_o // LANES + p)),
                pl.BlockSpec((2, t, 1), lambda p, j, i: (p, qrow(p, j, i), 0))]
    args = [qa, ka, va, oa, da, lse]
    out_specs = [pl.BlockSpec((s, wq), lambda p, j, i: (0, p)), pl.BlockSpec((t, wq), lambda p, j, i: (j, p)),
                 pl.BlockSpec((t, LANES), lambda p, j, i: (j, p))]
    out_shape = [jax.ShapeDtypeStruct((s, npairs * wq), dq_dtype), jax.ShapeDtypeStruct((s, npairs * wq), dq_dtype),
                 jax.ShapeDtypeStruct((s, npairs * LANES), BF16)]
    scratch = [pltpu.VMEM((s, wq), F32), pltpu.VMEM((t, wq), F32), pltpu.VMEM((t, LANES), F32)]
    if bias:
        in_specs.append(pl.BlockSpec((2, 1, t), lambda p, j, i: (p, 0, j)))
        args.append(crow)
        out_specs += [pl.BlockSpec((2, 1, t), lambda p, j, i: (p, 0, j)), pl.BlockSpec((2, s, 1), lambda p, j, i: (p, 0, 0))]
        out_shape += [jax.ShapeDtypeStruct((2 * npairs, 1, s), F32), jax.ShapeDtypeStruct((2 * npairs, s, 1), F32)]
        scratch.append(pltpu.VMEM((2, 1, t), F32))
    n_out = len(out_shape)
    out = pl.pallas_call(
        body, name=name, grid=(npairs, nb, nb), in_specs=in_specs + [_ANY] * ns, out_specs=out_specs + [_ANY] * ns,
        out_shape=out_shape + [jax.ShapeDtypeStruct((3,) + a.shape[1:], a.dtype) for a in scatter],
        scratch_shapes=scratch + (_sem_pairs(3 * ns) if ns else []),
        compiler_params=_cparams((_ARB if ns else _PAR, _ARB, _ARB)),
    )(*args, *scatter)
    return (*out[:n_out], out[n_out:]) if ns else out


def _sub_mask(shape, lo, hi):
    sub = lax.broadcasted_iota(jnp.int32, shape, 0)
    return (sub >= lo) & (sub < hi)


def _pair_rows(parts):
    return jnp.where(_sub_mask(parts[0].shape, 0, HALF), parts[0], parts[1])


def _swa_valid(start, t):
    krow = lax.broadcasted_iota(jnp.int32, (t + WINDOW, t), 0)
    qcol = lax.broadcasted_iota(jnp.int32, (t + WINDOW, t), 1)
    diff = qcol - krow + WINDOW
    return (diff >= 0) & (diff < WINDOW) & (krow + start >= WINDOW)


def _swa_tiles(s):
    t = _pick(s, (256, 128))
    return t, _pick(s // t, (4, 2, 1))


def _swa_fwd(q, kp, vp, sink, *, name):
    s = q.shape[0]
    npairs = SWA_HEADS // 2
    t, nt = _swa_tiles(s)
    scale = SWA_DIM ** -0.5

    def body(q_ref, k_ref, v_ref, sk_ref, o_ref, lse_ref):
        sts, vws = [], []
        for u in range(nt):
            start = pl.multiple_of((pl.program_id(1) * nt + u) * t, t)
            kw = k_ref[pl.ds(start, t + WINDOW), :]
            vws.append(v_ref[pl.ds(start, t + WINDOW), :])
            qb = q_ref[u * t:(u + 1) * t, :]
            valid = _swa_valid(start, t)
            for h in range(2):
                st = lax.dot_general(kw, _head_sel(qb, h, HALF), _NT, preferred_element_type=F32) * scale
                sts.append(jnp.where(valid, st, -jnp.inf))
        pts = []
        for u in range(nt):
            for h in range(2):
                st = sts[2 * u + h]
                snk = sk_ref[h][:, 0:1]
                m = jnp.maximum(jnp.max(st, axis=0, keepdims=True), snk)
                e = jnp.exp(st - m)
                l = jnp.sum(e, axis=0, keepdims=True) + jnp.exp(snk - m)
                pts.append((e * (1.0 / l)).astype(BF16))
                lse_ref[h, :, u * t:(u + 1) * t] = m + jnp.log(l)
        for u in range(nt):
            outs = [lax.dot_general(vws[u], pts[2 * u + h], _TN, preferred_element_type=F32) for h in range(2)]
            o_ref[:, u * t:(u + 1) * t] = _pair_rows(outs)

    kvspec = pl.BlockSpec((s + WINDOW, LANES), lambda p, i: (0, 0))
    return pl.pallas_call(
        body, name=name, grid=(npairs, s // (t * nt)),
        in_specs=[pl.BlockSpec((t * nt, LANES), lambda p, i: (i, p)), kvspec, kvspec, pl.BlockSpec((2, 1, LANES), lambda p, i: (p, 0, 0))],
        out_specs=[pl.BlockSpec((LANES, t * nt), lambda p, i: (p, i)), pl.BlockSpec((2, 1, t * nt), lambda p, i: (p, 0, i))],
        out_shape=[jax.ShapeDtypeStruct((npairs * LANES, s), F32), jax.ShapeDtypeStruct((SWA_HEADS, 1, s), F32)],
        compiler_params=_cparams((_PAR, _PAR)),
    )(q, kp, vp, sink)


def _swa_bwd(q, kp, vp, kpt, sink, ot, do, dot, lse, *, name):
    s = q.shape[0]
    npairs = SWA_HEADS // 2
    t, nt = _swa_tiles(s)
    scale = SWA_DIM ** -0.5

    def body(q_ref, k_ref, v_ref, kt_ref, sk_ref, ot_ref, do_ref, dot_ref, lse_ref, dq_ref, dk_ref, dv_ref, dsk_ref):
        pp, i = pl.program_id(0), pl.program_id(1)

        @pl.when((pp == 0) & (i == 0))
        def _():
            dk_ref[...] = jnp.zeros_like(dk_ref)
            dv_ref[...] = jnp.zeros_like(dv_ref)

        @pl.when(i == 0)
        def _():
            dsk_ref[...] = jnp.zeros_like(dsk_ref)

        fed = []
        for u in range(nt):
            cols = slice(u * t, (u + 1) * t)
            start = pl.multiple_of((i * nt + u) * t, t)
            win = pl.ds(start, t + WINDOW)
            qb, dob = q_ref[cols, :], do_ref[cols, :]
            kw, vw = k_ref[win, :], v_ref[win, :]
            valid = _swa_valid(start, t)
            for h in range(2):
                st = lax.dot_general(kw, _head_sel(qb, h, HALF), _NT, preferred_element_type=F32)
                dpt = lax.dot_general(vw, _head_sel(dob, h, HALF), _NT, preferred_element_type=F32)
                fed.append((cols, win, qb, dob, valid, st, dpt))
        mid = []
        for u in range(nt):
            cols = fed[2 * u][0]
            prod = dot_ref[:, cols].astype(F32) * ot_ref[:, cols]
            for h in range(2):
                valid, st, dpt = fed[2 * u + h][4:]
                lse_b = lse_ref[h, :, cols]
                pt = jnp.exp(jnp.where(valid, st * scale, -jnp.inf) - lse_b)
                delta = jnp.sum(jnp.where(_sub_mask(prod.shape, h * HALF, (h + 1) * HALF), prod, 0.0), axis=0, keepdims=True)
                dst = pt * (dpt - delta)
                psink = jnp.exp(sk_ref[h][:, 0:1] - lse_b)
                dsk_ref[h] += jnp.broadcast_to(-jnp.sum(psink * delta, axis=1, keepdims=True), (1, LANES))
                mid.append((pt.astype(BF16), (dst * scale).astype(BF16)))
        for u in range(nt):
            cols, win, qb, dob = fed[2 * u][:4]
            ktw = kt_ref[:, win]
            dq_parts = [jnp.dot(ktw, mid[2 * u + h][1], preferred_element_type=F32) for h in range(2)]
            dk_parts = [jnp.dot(mid[2 * u + h][1], qb, preferred_element_type=F32) for h in range(2)]
            dv_parts = [jnp.dot(mid[2 * u + h][0], dob, preferred_element_type=F32) for h in range(2)]
            dq_ref[:, cols] = _pair_rows(dq_parts)
            dk_ref[win, :] += _pair(dk_parts)
            dv_ref[win, :] += _pair(dv_parts)

    tile = pl.BlockSpec((t * nt, LANES), lambda p, i: (i, p))
    ttile = pl.BlockSpec((LANES, t * nt), lambda p, i: (p, i))
    kvspec = pl.BlockSpec((s + WINDOW, LANES), lambda p, i: (0, 0))
    ktspec = pl.BlockSpec((LANES, s + WINDOW), lambda p, i: (0, 0))
    skspec = pl.BlockSpec((2, 1, LANES), lambda p, i: (p, 0, 0))
    return pl.pallas_call(
        body, name=name, grid=(npairs, s // (t * nt)),
        in_specs=[tile, kvspec, kvspec, ktspec, skspec, ttile, tile, ttile, pl.BlockSpec((2, 1, t * nt), lambda p, i: (p, 0, i))],
        out_specs=[ttile, kvspec, kvspec, skspec],
        out_shape=[jax.ShapeDtypeStruct((npairs * LANES, s), F32), jax.ShapeDtypeStruct((s + WINDOW, LANES), F32),
                   jax.ShapeDtypeStruct((s + WINDOW, LANES), F32), jax.ShapeDtypeStruct((SWA_HEADS, 1, LANES), F32)],
        compiler_params=_cparams((_ARB, _ARB)),
    )(q, kp, vp, kpt, sink, ot, do, dot, lse)


def _odd_rope(h, cos, sin, *, name):
    s = h.shape[0]
    tb = _pick(s, (512, 256, 128))
    nq = SWA_HEADS * SWA_DIM

    def body(q_ref, kv_ref, cos_ref, sin_ref, qo_ref, ko_ref, vo_ref):
        cs, sn = cos_ref[...], sin_ref[...]
        for j in range(nq // LANES):
            sl = slice(j * LANES, (j + 1) * LANES)
            qo_ref[:, sl] = _rope_tile(q_ref[:, sl], cs, sn, SWA_DIM // 2).astype(BF16)
        ko_ref[...] = _rope_tile(kv_ref[:, :LANES], cs, sn, SWA_DIM // 2).astype(BF16)
        vo_ref[...] = kv_ref[:, LANES:].astype(BF16)

    rows = lambda w, c=0: pl.BlockSpec((tb, w), lambda i: (i, c))
    return pl.pallas_call(
        body, name=name, grid=(s // tb,),
        in_specs=[rows(nq, O_Q // nq), rows(2 * LANES, O_K // (2 * LANES)), rows(LANES), rows(LANES)],
        out_specs=[rows(nq), rows(LANES), rows(LANES)],
        out_shape=[jax.ShapeDtypeStruct((s, nq), BF16), jax.ShapeDtypeStruct((s, LANES), BF16), jax.ShapeDtypeStruct((s, LANES), BF16)],
        compiler_params=_cparams((_PAR,)),
    )(h, h, cos, sin)


def _odd_rope_bwd(dq, dk, cos, sin, *, name):
    s = dq.shape[0]
    tb = _pick(s, (512, 256, 128))
    nq = SWA_HEADS * SWA_DIM

    def body(dq_ref, dk_ref, cos_ref, sin_ref, qo_ref, ko_ref):
        cs, sn = cos_ref[...], -sin_ref[...]
        for j in range(nq // LANES):
            sl = slice(j * LANES, (j + 1) * LANES)
            qo_ref[:, sl] = _rope_tile(dq_ref[:, sl], cs, sn, SWA_DIM // 2).astype(BF16)
        ko_ref[...] = _rope_tile(dk_ref[...], cs, sn, SWA_DIM // 2).astype(BF16)

    rows = lambda w: pl.BlockSpec((tb, w), lambda i: (i, 0))
    return pl.pallas_call(
        body, name=name, grid=(s // tb,), in_specs=[rows(nq), rows(LANES), rows(LANES), rows(LANES)],
        out_specs=[rows(nq), rows(LANES)],
        out_shape=[jax.ShapeDtypeStruct((s, nq), BF16), jax.ShapeDtypeStruct((s, LANES), BF16)],
        compiler_params=_cparams((_PAR,)),
    )(dq, dk, cos, sin)


def _out_fwd(o_parts, h, x, w_out, g, b, *, name):
    s = x.shape[0]
    tb = _pick(s, (256, 128))
    n_o = len(o_parts)

    def body(*refs):
        o_refs = refs[:n_o]
        gate_ref, x_ref, w_ref, g_ref, b_ref, xn_ref, xb_ref, z_ref, xh_ref, rs_ref = refs[n_o:]
        gate = gate_ref[...]
        o = o_refs[0][...] if n_o == 1 else jnp.concatenate([r[...] for r in o_refs], axis=1)
        z = (o * (gate * _sigmoid(gate))).astype(BF16)
        z_ref[...] = z
        r = ALPHA * x_ref[...] + jnp.dot(z, w_ref[...], preferred_element_type=F32)
        mu = jnp.mean(r, axis=-1, keepdims=True)
        rc = r - mu
        rstd = lax.rsqrt(jnp.mean(rc * rc, axis=-1, keepdims=True) + LN_EPS)
        xh = rc * rstd
        xn = xh * g_ref[...] + b_ref[...]
        xh_ref[...] = xh
        rs_ref[...] = rstd
        xn_ref[...] = xn
        xb_ref[...] = xn.astype(BF16)

    rows = lambda w: pl.BlockSpec((tb, w), lambda i: (i, 0))
    full = lambda a: pl.BlockSpec(a.shape, lambda i: (0,) * a.ndim)
    return pl.pallas_call(
        body, name=name, grid=(s // tb,),
        in_specs=[rows(a.shape[1]) for a in o_parts] + [rows(D_MODEL), rows(D_MODEL), full(w_out), full(g), full(b)],
        out_specs=[rows(D_MODEL), rows(D_MODEL), rows(D_MODEL), rows(D_MODEL), rows(1)],
        out_shape=[jax.ShapeDtypeStruct((s, D_MODEL), F32), jax.ShapeDtypeStruct((s, D_MODEL), BF16),
                   jax.ShapeDtypeStruct((s, D_MODEL), BF16), jax.ShapeDtypeStruct((s, D_MODEL), F32),
                   jax.ShapeDtypeStruct((s, 1), F32)],
        compiler_params=_cparams((_PAR,)),
    )(*o_parts, h, x, w_out, g, b)


def _out_bwd(dxn, xh, rstd, g, w_out_t, o_parts, h, *, name):
    s = dxn.shape[0]
    tb = _pick(s, (256, 128))
    n_o = len(o_parts)

    def body(*refs):
        dxn_ref, xh_ref, rs_ref, g_ref, wt_ref = refs[:5]
        o_refs = refs[5:5 + n_o]
        gate_ref, dr_ref, dy_ref, do_ref, dgate_ref, dg_ref, db_ref = refs[5 + n_o:]

        @pl.when(pl.program_id(0) == 0)
        def _():
            dg_ref[...] = jnp.zeros_like(dg_ref)
            db_ref[...] = jnp.zeros_like(db_ref)

        dxn_b, xh_b = dxn_ref[...], xh_ref[...]
        dg_ref[...] += jnp.sum(dxn_b * xh_b, axis=0, keepdims=True)
        db_ref[...] += jnp.sum(dxn_b, axis=0, keepdims=True)
        dxh = dxn_b * g_ref[...]
        dr = rs_ref[...] * (dxh - jnp.mean(dxh, axis=-1, keepdims=True) - xh_b * jnp.mean(dxh * xh_b, axis=-1, keepdims=True))
        dr_ref[...] = dr
        dy = dr.astype(BF16)
        dy_ref[...] = dy
        dz = jnp.dot(dy, wt_ref[...], preferred_element_type=F32)
        gate = gate_ref[...]
        sg = _sigmoid(gate)
        o = o_refs[0][...] if n_o == 1 else jnp.concatenate([r[...] for r in o_refs], axis=1)
        do_ref[...] = (dz * (gate * sg)).astype(BF16)
        dgate_ref[...] = (dz * o * (sg * (1.0 + gate * (1.0 - sg)))).astype(BF16)

    rows = lambda w: pl.BlockSpec((tb, w), lambda i: (i, 0))
    full = lambda a: pl.BlockSpec(a.shape, lambda i: (0,) * a.ndim)
    return pl.pallas_call(
        body, name=name, grid=(s // tb,),
        in_specs=[rows(D_MODEL), rows(D_MODEL), rows(1), full(g), full(w_out_t)] + [rows(a.shape[1]) for a in o_parts] + [rows(D_MODEL)],
        out_specs=[rows(D_MODEL), rows(D_MODEL), rows(D_MODEL), rows(D_MODEL), full(g), full(g)],
        out_shape=[jax.ShapeDtypeStruct((s, D_MODEL), F32), jax.ShapeDtypeStruct((s, D_MODEL), BF16),
                   jax.ShapeDtypeStruct((s, D_MODEL), BF16), jax.ShapeDtypeStruct((s, D_MODEL), BF16),
                   jax.ShapeDtypeStruct(g.shape, F32), jax.ShapeDtypeStruct(g.shape, F32)],
        compiler_params=_cparams((_ARB,)),
    )(dxn, xh, rstd, g, w_out_t, *o_parts, h)


def _loss_grad(y, target, *, name):
    s, d = y.shape
    tb = _pick(s, (512, 256, 128))

    def body(y_ref, t_ref, dy_ref, l_ref):
        @pl.when(pl.program_id(0) == 0)
        def _():
            l_ref[...] = jnp.zeros_like(l_ref)

        err = y_ref[...] - t_ref[...]
        dy_ref[...] = err * (1.0 / d)
        per_tok = jnp.mean(err * err, axis=-1, keepdims=True)
        l_ref[...] += 0.5 * jnp.sum(per_tok, axis=0, keepdims=True)

    rows = pl.BlockSpec((tb, d), lambda i: (i, 0))
    return pl.pallas_call(
        body, name=name, grid=(s // tb,), in_specs=[rows, rows],
        out_specs=[rows, pl.BlockSpec((8, LANES), lambda i: (0, 0))],
        out_shape=[jax.ShapeDtypeStruct((s, d), F32), jax.ShapeDtypeStruct((8, LANES), F32)],
        compiler_params=_cparams((_ARB,)),
    )(y, target)


def _adamw(w, g, m, v, *, name):
    shape = w.shape
    w3, g3, m3, v3 = (a.reshape((1,) * (3 - a.ndim) + a.shape) for a in (w, g, m, v))
    a0, a1, a2 = w3.shape
    tb = _pick(a1, (256, 128)) if a1 % 8 == 0 else a1
    c1 = 1.0 - ADAM_B1 ** ADAM_STEP
    c2 = 1.0 - ADAM_B2 ** ADAM_STEP

    def body(w_ref, g_ref, m_ref, v_ref, d_ref, mo_ref, vo_ref):
        gg = g_ref[...]
        mn = ADAM_B1 * m_ref[...] + (1.0 - ADAM_B1) * gg
        vn = ADAM_B2 * v_ref[...] + (1.0 - ADAM_B2) * (gg * gg)
        mo_ref[...] = mn
        vo_ref[...] = vn
        d_ref[...] = -ADAM_LR * ((mn / c1) / (jnp.sqrt(vn / c2) + ADAM_EPS) + ADAM_WD * w_ref[...])

    spec = pl.BlockSpec((1, tb, a2), lambda i, j: (i, j, 0))
    outs = pl.pallas_call(
        body, name=name, grid=(a0, a1 // tb), in_specs=[spec] * 4, out_specs=[spec] * 3,
        out_shape=[jax.ShapeDtypeStruct(w3.shape, F32)] * 3, compiler_params=_cparams((_PAR, _PAR)),
    )(w3, g3, m3, v3)
    return tuple(a.reshape(shape) for a in outs)


def _place():
    x, y, c = lax.axis_index("x"), lax.axis_index("y"), lax.axis_index("c")
    return x, y, c, [(1 - x, y), (x, 1 - y), (1 - x, 1 - y)]


_ANY = pl.BlockSpec(memory_space=pl.ANY)


def _sem_pairs(n):
    return [pltpu.SemaphoreType.DMA((n,)), pltpu.SemaphoreType.DMA((n,))]


GATHER_COPIES = 7


def _gather_phases(shapes, srcs, outs, send_sems, recv_sems):
    nt = len(shapes)
    x, y, c, chips = _place()
    me = 2 * x + y
    sib = (x, y, 1 - c)

    def half(t, chip, hc):
        rh = shapes[t][1] // 2
        return outs[t].at[chip, :, pl.ds(hc * rh, rh), :]

    def copy(t, kk, s_ref, d_ref, to):
        return pltpu.make_async_remote_copy(src_ref=s_ref, dst_ref=d_ref, send_sem=send_sems.at[GATHER_COPIES * t + kk],
                                            recv_sem=recv_sems.at[GATHER_COPIES * t + kk], device_id=to, device_id_type=MESH)

    def first_hop(t, j):
        rh = shapes[t][1] // 2
        cx, cy = chips[j]
        return copy(t, j, srcs[t].at[:, pl.ds(c * rh, rh), :], half(t, me, c), (cx, cy, c))

    def passed(t, j):
        cx, cy = chips[j]
        landed = half(t, 2 * cx + cy, c)
        return copy(t, 3 + j, landed, landed, sib)

    def own(t):
        return copy(t, 6, srcs[t], outs[t].at[me], sib)

    pairs = [(t, j) for j in range(3) for t in range(nt)]

    def start():
        for t, j in pairs:
            first_hop(t, j).start()
        for t in range(nt):
            own(t).start()

    def pass_on():
        for t, j in pairs:
            cx, cy = chips[j]
            landed = half(t, 2 * cx + cy, c)
            copy(t, j, landed, landed, (cx, cy, c)).wait_recv()
            passed(t, j).start()

    def finish():
        for t, j in pairs:
            cx, cy = chips[j]
            theirs = half(t, 2 * cx + cy, 1 - c)
            copy(t, 3 + j, theirs, theirs, sib).wait_recv()
        for t in range(nt):
            own(t).wait()
        for t, j in pairs:
            first_hop(t, j).wait_send()
            passed(t, j).wait_send()

    return start, pass_on, finish


def _gather_chip_shards(tensors, *, name):
    nt = len(tensors)

    def body(*refs):
        for phase in _gather_phases([a.shape for a in tensors], refs[:nt], refs[nt:2 * nt], *refs[2 * nt:]):
            phase()

    return pl.pallas_call(
        body, name=name, in_specs=[_ANY] * nt, out_specs=[_ANY] * nt,
        out_shape=[jax.ShapeDtypeStruct((N_CHIPS,) + a.shape, a.dtype) for a in tensors],
        scratch_shapes=_sem_pairs(GATHER_COPIES * nt),
    )(*tensors)


def _swap_halves(gs, *, name):
    nt = len(gs)

    def body(*refs):
        srcs, outs = refs[:nt], refs[nt:2 * nt]
        send_sems, recv_sems = refs[2 * nt:]
        x, y, c, _ = _place()
        cps = [pltpu.make_async_remote_copy(src_ref=srcs[t].at[:, 1 - c], dst_ref=outs[t], send_sem=send_sems.at[t],
                                            recv_sem=recv_sems.at[t], device_id=(x, y, 1 - c), device_id_type=MESH)
               for t in range(nt)]
        for cp in cps:
            cp.start()
        for cp in cps:
            cp.wait()

    return pl.pallas_call(
        body, name=name, in_specs=[_ANY] * nt, out_specs=[_ANY] * nt,
        out_shape=[jax.ShapeDtypeStruct((N_CHIPS,) + g.shape[2:], g.dtype) for g in gs], scratch_shapes=_sem_pairs(nt),
    )(*gs)


def _scatter_phases(nt, srcs, outs, send_sems, recv_sems):
    x, y, c, chips = _place()

    def copies():
        return [pltpu.make_async_remote_copy(src_ref=srcs[t].at[2 * cx + cy], dst_ref=outs[t].at[j],
                                             send_sem=send_sems.at[3 * t + j], recv_sem=recv_sems.at[3 * t + j],
                                             device_id=(cx, cy, c), device_id_type=MESH)
                for j, (cx, cy) in enumerate(chips) for t in range(nt)]

    def start():
        for cp in copies():
            cp.start()

    def finish():
        for cp in copies():
            cp.wait()

    return start, finish


def _scatter_to_chips(blocks, *, name):
    nt = len(blocks)

    def body(*refs):
        for phase in _scatter_phases(nt, refs[:nt], refs[nt:2 * nt], *refs[2 * nt:]):
            phase()

    return pl.pallas_call(
        body, name=name, in_specs=[_ANY] * nt, out_specs=[_ANY] * nt,
        out_shape=[jax.ShapeDtypeStruct((3,) + a.shape[1:], a.dtype) for a in blocks], scratch_shapes=_sem_pairs(3 * nt),
    )(*blocks)


def _join_halves(bufs, *, name):
    nt = len(bufs)

    def body(*refs):
        srcs, outs = refs[:nt], refs[nt:2 * nt]
        send_sems, recv_sems = refs[2 * nt:]
        x, y, c, _ = _place()
        cps = []
        for t in range(nt):
            rh = bufs[t].shape[1] // 2
            rows = pl.ds(c * rh, rh)
            cps.append(pltpu.make_async_remote_copy(src_ref=srcs[t].at[:, rows, :], dst_ref=outs[t].at[:, rows, :],
                                                    send_sem=send_sems.at[t], recv_sem=recv_sems.at[t],
                                                    device_id=(x, y, 1 - c), device_id_type=MESH))
        for cp in cps:
            cp.start()
        for cp in cps:
            cp.wait()

    return pl.pallas_call(
        body, name=name, in_specs=[_ANY] * nt, out_specs=[_ANY] * nt,
        out_shape=[jax.ShapeDtypeStruct(b.shape, b.dtype) for b in bufs], scratch_shapes=_sem_pairs(nt),
        input_output_aliases={t: t for t in range(nt)},
    )(*bufs)


def _add_sibling(g, recv, cidx, chip_idx, *, name):
    _, _, na, rh, cdim = g.shape
    tb = _pick(rh, (256, 128, 64, 32, 16))

    def body(c_ref, k_ref, g_ref, r_ref, s_ref, o_ref):
        tot = g_ref[0, 0] + r_ref[0]
        s_ref[0] = tot.astype(BF16)

        @pl.when(pl.program_id(2) == k_ref[0])
        def _():
            o_ref[...] = tot

    return pl.pallas_call(
        body, name=name,
        grid_spec=pltpu.PrefetchScalarGridSpec(
            num_scalar_prefetch=2, grid=(na, rh // tb, N_CHIPS),
            in_specs=[pl.BlockSpec((1, 1, 1, tb, cdim), lambda a, i, k, c_ref, k_ref: (k, c_ref[0], a, i, 0)),
                      pl.BlockSpec((1, 1, tb, cdim), lambda a, i, k, c_ref, k_ref: (k, a, i, 0))],
            out_specs=[pl.BlockSpec((1, 1, tb, cdim), lambda a, i, k, c_ref, k_ref: (k, a, i, 0)),
                       pl.BlockSpec((1, tb, cdim), lambda a, i, k, c_ref, k_ref: (a, i, 0))]),
        out_shape=[jax.ShapeDtypeStruct((N_CHIPS, na, rh, cdim), BF16), jax.ShapeDtypeStruct((na, rh, cdim), F32)],
        compiler_params=_cparams((_PAR, _PAR, _ARB)),
    )(cidx, chip_idx, g, recv)


def _add_chips(own, recv, cidx, *, name):
    na, rh, cdim = own.shape
    tb = _pick(rh, (256, 128, 64, 32, 16))
    nblk = rh // tb

    def body(c_ref, a_ref, r0_ref, r1_ref, r2_ref, o_ref):
        o_ref[...] = ((a_ref[...] + r0_ref[0].astype(F32)) + r1_ref[0].astype(F32)) + r2_ref[0].astype(F32)

    slot = lambda j: pl.BlockSpec((1, 1, tb, cdim), lambda a, i, c_ref: (j, a, i, 0))
    return pl.pallas_call(
        body, name=name,
        grid_spec=pltpu.PrefetchScalarGridSpec(
            num_scalar_prefetch=1, grid=(na, nblk),
            in_specs=[pl.BlockSpec((1, tb, cdim), lambda a, i, c_ref: (a, i, 0)), slot(0), slot(1), slot(2)],
            out_specs=pl.BlockSpec((1, tb, cdim), lambda a, i, c_ref: (a, c_ref[0] * nblk + i, 0))),
        out_shape=jax.ShapeDtypeStruct((na, 2 * rh, cdim), F32), compiler_params=_cparams((_PAR, _PAR)),
    )(cidx, own, recv, recv, recv)


def _all_reduce_small(v, *, name):
    r, cdim = v.shape

    def body(v_ref, o_ref, buf, send_sems, recv_sems):
        x, y, c, _ = _place()
        me = 4 * x + 2 * y + c
        buf[me] = v_ref[...]
        cps = []
        for p in range(1, N_DEV):
            to = (1 - x if p & 4 else x, 1 - y if p & 2 else y, 1 - c if p & 1 else c)
            cp = pltpu.make_async_remote_copy(src_ref=v_ref, dst_ref=buf.at[me], send_sem=send_sems.at[p - 1],
                                              recv_sem=recv_sems.at[p - 1], device_id=to, device_id_type=MESH)
            cp.start()
            cps.append(cp)
        for p in range(1, N_DEV):
            frm = (4 * x + 2 * y + c) ^ p
            pltpu.make_async_remote_copy(src_ref=v_ref, dst_ref=buf.at[frm], send_sem=send_sems.at[p - 1],
                                         recv_sem=recv_sems.at[p - 1], device_id=(x, y, c), device_id_type=MESH).wait_recv()
        for cp in cps:
            cp.wait_send()
        tot = buf[0]
        for i in range(1, N_DEV):
            tot = tot + buf[i]
        o_ref[...] = tot

    vm = pl.BlockSpec(memory_space=pltpu.VMEM)
    return pl.pallas_call(
        body, name=name, in_specs=[vm], out_specs=vm, out_shape=jax.ShapeDtypeStruct((r, cdim), F32),
        scratch_shapes=[pltpu.VMEM((N_DEV, r, cdim), F32), pltpu.SemaphoreType.DMA((N_DEV - 1,)), pltpu.SemaphoreType.DMA((N_DEV - 1,))],
    )(v)


_SHARDED = ("even_w_in", "even_w_uq", "even_w_ukv", "even_w_out", "odd_w_in", "odd_w_out")
_COL_SHARDED = ("even_w_in", "even_w_uq", "even_w_ukv", "odd_w_in")
PACK_COLS = 1024


def _unshard(name, stacked):
    n, a, b, cc = stacked.shape
    if name in _COL_SHARDED:
        return stacked.transpose(1, 2, 0, 3).reshape(a, b, n * cc)
    return stacked.transpose(1, 0, 2, 3).reshape(a, n * b, cc)


def _chip_halves(name, full):
    a, b, cc = full.shape
    if name in _COL_SHARDED:
        return full.reshape(a, 2, b // 2, N_CHIPS, cc // N_CHIPS).transpose(3, 1, 0, 2, 4)
    return full.reshape(a, N_CHIPS, 2, b // (2 * N_CHIPS), cc).transpose(1, 2, 0, 3, 4)


def _rope_tables(s):
    pos = jnp.arange(s, dtype=F32)

    def ang(d):
        inv = ROPE_THETA ** (-jnp.arange(0, d, 2, dtype=F32) / d)
        a = pos[:, None] * inv[None, :]
        return jnp.cos(a), jnp.sin(a)

    c16, s16 = ang(MLA_ROPE)
    one, zero = jnp.ones((s, KPE_LANE), F32), jnp.zeros((s, KPE_LANE), F32)
    cos_m = jnp.concatenate([one, c16, c16, one[:, :32]], axis=1)
    sin_m = jnp.concatenate([zero, -s16, s16, zero[:, :32]], axis=1)
    c32, s32 = ang(SWA_DIM)
    cos_s = jnp.concatenate([c32, c32, c32, c32], axis=1)
    sin_s = jnp.concatenate([-s32, s32, -s32, s32], axis=1)
    return cos_m, sin_m, cos_s, sin_s


def _even_weights(w_in, w_uq, w_ukv):
    zeros = lambda n: jnp.zeros((D_MODEL, n), w_in.dtype)
    wcq, wckv, wkpe = w_in[:, 0:256], w_in[:, 256:384], w_in[:, 384:416]
    wfq, wfk, wfv = w_in[:, 416:928], w_in[:, 928:1440], w_in[:, 1440:1952]
    wfl, wg = w_in[:, 1952:1960], w_in[:, 1960:2984]
    misc = jnp.concatenate([zeros(KPE_LANE), wkpe, wfl, zeros(LANES - FL_LANE - FOX_HEADS)], axis=1)
    w_in_p = jnp.concatenate([wg, wfq, wfk, wfv, wcq, wckv, misc], axis=1)
    uq = w_uq.reshape(MLA_Q_RANK, MLA_HEADS, MLA_NOPE + MLA_ROPE)
    uq_p = jnp.pad(uq, ((0, 0), (0, 0), (0, HEAD_PAD - MLA_NOPE - MLA_ROPE))).reshape(MLA_Q_RANK, MLA_HEADS * HEAD_PAD)
    ukv = w_ukv.reshape(MLA_KV_RANK, MLA_HEADS, MLA_NOPE + MLA_V)
    uk_p = jnp.pad(ukv[..., :MLA_NOPE], ((0, 0), (0, 0), (0, HEAD_PAD - MLA_NOPE))).reshape(MLA_KV_RANK, MLA_HEADS * HEAD_PAD)
    uv = ukv[..., MLA_NOPE:].reshape(MLA_KV_RANK, MLA_HEADS * MLA_V)
    ukv_p = jnp.concatenate([uk_p, uv], axis=1)
    return w_in_p, w_in_p.T, uq_p, uq_p.T, ukv_p, ukv_p.T


def _even_weight_grads(dw_in_p, duq_p, dukv_p):
    g = dw_in_p
    gate, fq, fk, fv = g[:, E_GATE:E_FQ], g[:, E_FQ:E_FK], g[:, E_FK:E_FV], g[:, E_FV:E_SMALL]
    cq, ckv, misc = g[:, E_SMALL:E_SMALL + 256], g[:, E_SMALL + 256:E_SMALL + 384], g[:, E_SMALL + 384:]
    dw_in = jnp.concatenate([cq, ckv, misc[:, KPE_LANE:KPE_LANE + MLA_ROPE], fq, fk, fv, misc[:, FL_LANE:FL_LANE + FOX_HEADS], gate], axis=1)
    duq = duq_p.reshape(MLA_Q_RANK, MLA_HEADS, HEAD_PAD)[..., :MLA_NOPE + MLA_ROPE].reshape(MLA_Q_RANK, -1)
    nq = MLA_HEADS * HEAD_PAD
    dk = dukv_p[:, :nq].reshape(MLA_KV_RANK, MLA_HEADS, HEAD_PAD)[..., :MLA_NOPE]
    dvv = dukv_p[:, nq:].reshape(MLA_KV_RANK, MLA_HEADS, MLA_V)
    dukv = jnp.concatenate([dk, dvv], axis=-1).reshape(MLA_KV_RANK, -1)
    return dw_in, duq, dukv


def _interleave(w, forward):
    a, b = (SWA_KV_HEADS, SWA_HEADS // SWA_KV_HEADS) if forward else (SWA_HEADS // SWA_KV_HEADS, SWA_KV_HEADS)
    return w.reshape(w.shape[0], a, b, -1).transpose(0, 2, 1, 3).reshape(w.shape[0], -1)


def _odd_weights(w_in, w_out):
    q, k, v, gate = w_in[:, 0:1024], w_in[:, 1024:1152], w_in[:, 1152:1280], w_in[:, 1280:2304]
    w_p = jnp.concatenate([_interleave(gate, True), _interleave(q, True), k, v], axis=1)
    w_out_p = _interleave(w_out.T, True).T
    return w_p, w_p.T, w_out_p, w_out_p.T


def _odd_weight_grads(g, dw_out_p):
    dw_in = jnp.concatenate([_interleave(g[:, O_Q:O_K], False), g[:, O_K:O_V], g[:, O_V:], _interleave(g[:, O_GATE:O_Q], False)], axis=1)
    return dw_in, _interleave(dw_out_p.T, False).T


def kernel(x, even_w_in, even_q_norm, even_w_uq, even_kv_norm, even_w_ukv, even_b_f, even_w_out, even_ln_g, even_ln_b, odd_w_in, odd_sinks, odd_w_out, odd_ln_g, odd_ln_b, loss_target, m_even_w_in, m_even_q_norm, m_even_w_uq, m_even_kv_norm, m_even_w_ukv, m_even_b_f, m_even_w_out, m_even_ln_g, m_even_ln_b, m_odd_w_in, m_odd_sinks, m_odd_w_out, m_odd_ln_g, m_odd_ln_b, v_even_w_in, v_even_q_norm, v_even_w_uq, v_even_kv_norm, v_even_w_ukv, v_even_b_f, v_even_w_out, v_even_ln_g, v_even_ln_b, v_odd_w_in, v_odd_sinks, v_odd_w_out, v_odd_ln_g, v_odd_ln_b):
    weights = dict(even_w_in=even_w_in, even_q_norm=even_q_norm, even_w_uq=even_w_uq, even_kv_norm=even_kv_norm,
                   even_w_ukv=even_w_ukv, even_b_f=even_b_f, even_w_out=even_w_out, even_ln_g=even_ln_g, even_ln_b=even_ln_b,
                   odd_w_in=odd_w_in, odd_sinks=odd_sinks, odd_w_out=odd_w_out, odd_ln_g=odd_ln_g, odd_ln_b=odd_ln_b)
    mom_m = dict(even_w_in=m_even_w_in, even_q_norm=m_even_q_norm, even_w_uq=m_even_w_uq, even_kv_norm=m_even_kv_norm,
                 even_w_ukv=m_even_w_ukv, even_b_f=m_even_b_f, even_w_out=m_even_w_out, even_ln_g=m_even_ln_g, even_ln_b=m_even_ln_b,
                 odd_w_in=m_odd_w_in, odd_sinks=m_odd_sinks, odd_w_out=m_odd_w_out, odd_ln_g=m_odd_ln_g, odd_ln_b=m_odd_ln_b)
    mom_v = dict(even_w_in=v_even_w_in, even_q_norm=v_even_q_norm, even_w_uq=v_even_w_uq, even_kv_norm=v_even_kv_norm,
                 even_w_ukv=v_even_w_ukv, even_b_f=v_even_b_f, even_w_out=v_even_w_out, even_ln_g=v_even_ln_g, even_ln_b=v_even_ln_b,
                 odd_w_in=v_odd_w_in, odd_sinks=v_odd_sinks, odd_w_out=v_odd_w_out, odd_ln_g=v_odd_ln_g, odd_ln_b=v_odd_ln_b)
    names = list(weights)
    xl = x[0]
    tgt = loss_target[0]
    s = xl.shape[0]
    ax, ay, ac = lax.axis_index("x"), lax.axis_index("y"), lax.axis_index("c")
    chip = 2 * ax + ay
    c_idx = ac.astype(jnp.int32).reshape(1)
    chip_idx = chip.astype(jnp.int32).reshape(1)

    ln_odd = jnp.pad(jnp.concatenate([odd_ln_g, odd_ln_b]), ((0, 12), (0, 0)))[None]
    wb = {n: weights[n].astype(BF16) for n in _SHARDED}
    even_mats = [n for n in _SHARDED if n.startswith("even")]
    odd_mats = [n for n in _SHARDED if n.startswith("odd")]
    gathered = _gather_chip_shards([wb[n][0:1] for n in even_mats] + [ln_odd], name="gather_first")
    full = {n: [_unshard(n, gathered[i])[0], None] for i, n in enumerate(even_mats)}
    later = [wb[n][1:2] for n in even_mats] + [wb[n] for n in odd_mats]
    ln_all = gathered[-1][:, 0]
    odd_g_full = ln_all[:, 0:2].transpose(1, 0, 2).reshape(2, D_MODEL)
    odd_b_full = ln_all[:, 2:4].transpose(1, 0, 2).reshape(2, D_MODEL)

    cos_m, sin_m, cos_s, sin_s = _rope_tables(s)
    bf_tiles = jnp.pad(even_b_f, ((0, 0), (FL_LANE, LANES - FL_LANE - FOX_HEADS)))
    sink_tiles = jnp.broadcast_to(_interleave(odd_sinks, True)[:, :, None, None], (2, SWA_HEADS, 1, LANES))
    mla_scale = (MLA_NOPE + MLA_ROPE) ** -0.5
    fox_scale = FOX_DIM ** -0.5
    mla_pairs, fox_pairs = MLA_HEADS // 2, FOX_HEADS // 2

    saved = []
    x_f, x_b = xl, xl.astype(BF16)
    for layer in range(DEPTH):
        j = layer // 2
        ln = f"L{layer}"
        if layer % 2 == 0:
            w_in_p, w_in_t, uq_p, uq_t, ukv_p, ukv_t = _even_weights(full["even_w_in"][j], full["even_w_uq"][j], full["even_w_ukv"][j])
            w_out, w_out_t = full["even_w_out"][j], full["even_w_out"][j].T
            qg, kg, bft = even_q_norm[j][None], even_kv_norm[j][None], bf_tiles[j][None]
            h, hb = _mm(x_b, w_in_p, out_dtype=F32, name=ln + "_in", also_bf16=True)
            q, k, v, qn, kvn, logf = _even_mid_fwd(h, qg, kg, bft, uq_p, ukv_p, cos_m, sin_m, name=ln + "_mid")
            cum = _cumsum(logf, reverse=False, name=ln + "_cum")[:, FL_LANE:FL_LANE + FOX_HEADS].T
            crow = cum[:, None, :]
            if layer == 0:
                o_mla, lse_mla, rest = _attn_fwd((q, 0), (k, 0), (v, 0), None, dk=HEAD_PAD, npairs=mla_pairs, scale=mla_scale,
                                                 name=ln + "_mla", gather=later)
                for i, n in enumerate(even_mats):
                    full[n][1] = _unshard(n, rest[i])[0]
                for i, n in enumerate(odd_mats):
                    full[n] = _unshard(n, rest[len(even_mats) + i])
            else:
                o_mla, lse_mla = _attn_fwd((q, 0), (k, 0), (v, 0), None, dk=HEAD_PAD, npairs=mla_pairs, scale=mla_scale, name=ln + "_mla")
            o_fox, lse_fox = _attn_fwd((hb, E_FQ), (hb, E_FK), (hb, E_FV), crow, dk=FOX_DIM, npairs=fox_pairs,
                                       scale=fox_scale, name=ln + "_fox")
            o_parts = [o_mla, o_fox]
            g_ln, b_ln = even_ln_g[j][None], even_ln_b[j][None]
            x_n, x_nb, z, xh, rstd = _out_fwd(o_parts, h, x_f, w_out, g_ln, b_ln, name=ln + "_out")
            saved.append(dict(h=h, hb=hb, x_b=x_b, qn=qn, kvn=kvn, q=q, k=k, v=v, crow=crow,
                              o_mla=o_mla, o_fox=o_fox, lse_mla=lse_mla, lse_fox=lse_fox, o_parts=o_parts, z=z, xh=xh, rstd=rstd,
                              w_in_t=w_in_t, uq_t=uq_t, ukv_t=ukv_t, w_out_t=w_out_t, qg=qg, kg=kg, bft=bft, g_ln=g_ln))
        else:
            w_in_p, w_in_t, w_out, w_out_t = _odd_weights(full["odd_w_in"][j], full["odd_w_out"][j])
            h = _mm(x_b, w_in_p, out_dtype=F32, name=ln + "_in")
            q, k, v = _odd_rope(h, cos_s, sin_s, name=ln + "_rope")
            kp = jnp.pad(k, ((WINDOW, 0), (0, 0)))
            vp = jnp.pad(v, ((WINDOW, 0), (0, 0)))
            o_t, lse = _swa_fwd(q, kp, vp, sink_tiles[j], name=ln + "_swa")
            o = o_t.T
            g_ln, b_ln = odd_g_full[j][None], odd_b_full[j][None]
            x_n, x_nb, z, xh, rstd = _out_fwd([o], h, x_f, w_out, g_ln, b_ln, name=ln + "_out")
            saved.append(dict(h=h, x_b=x_b, q=q, kp=kp, vp=vp, lse=lse, o_t=o_t, o_parts=[o], z=z, xh=xh, rstd=rstd,
                              w_in_t=w_in_t, w_out_t=w_out_t, g_ln=g_ln))
        x_f, x_b = x_n, x_nb

    dxn, loss_tile = _loss_grad(x_f, tgt, name="loss")

    def pair_sums(group, tag):
        gp = [_chip_halves(n, g) for n, g in group.items()]
        from_sib = _swap_halves(gp, name="grad_swap_" + tag)
        return [_add_sibling(g, r, c_idx, chip_idx, name=f"grad_add_sibling_{n}_{tag}") for n, g, r in zip(group, gp, from_sib)]

    def chip_sums(group, pair, from_chips, tag):
        mine = [_add_chips(p[1], r, c_idx, name=f"grad_add_chips_{n}_{tag}") for n, p, r in zip(group, pair, from_chips)]
        return dict(zip(group, _join_halves(mine, name="grad_join_" + tag)))

    grads = {n: [None, None] for n in names}
    for layer in reversed(range(DEPTH)):
        j = layer // 2
        ln = f"L{layer}"
        sv = saved[layer]
        dr, dy, do, dgate, dg_ln, db_ln = _out_bwd(dxn, sv["xh"], sv["rstd"], sv["g_ln"], sv["w_out_t"], sv["o_parts"], sv["h"], name=ln + "_outb")
        dw_out = _mm(sv["z"].T, dy, out_dtype=F32, name=ln + "_dwout")
        if layer % 2 == 0:
            hb = sv["hb"]
            if layer == 0:
                early = {n: grads[n][1][None] for n in even_mats}
                early.update({n: jnp.stack(grads[n]) for n in odd_mats})
                early_pair = pair_sums(early, "early")
                dq, dk, dv, early_chips = _attn_bwd((sv["q"], 0), (sv["k"], 0), (sv["v"], 0), (sv["o_mla"], 0), (do, 0), sv["lse_mla"], None,
                                                    dk=HEAD_PAD, npairs=mla_pairs, scale=mla_scale, dq_dtype=F32, name=ln + "_mlab",
                                                    scatter=[p[0] for p in early_pair])
            else:
                dq, dk, dv = _attn_bwd((sv["q"], 0), (sv["k"], 0), (sv["v"], 0), (sv["o_mla"], 0), (do, 0), sv["lse_mla"], None,
                                       dk=HEAD_PAD, npairs=mla_pairs, scale=mla_scale, dq_dtype=F32, name=ln + "_mlab")
            dfq, dfk, dfv, dcrow, dccol = _attn_bwd((hb, E_FQ), (hb, E_FK), (hb, E_FV), (sv["o_fox"], 0), (do, MLA_HEADS * MLA_V),
                                                    sv["lse_fox"], sv["crow"], dk=FOX_DIM, npairs=fox_pairs,
                                                    scale=fox_scale, dq_dtype=BF16, name=ln + "_foxb")
            dcum = jnp.pad((dcrow[:, 0, :] + dccol[:, :, 0]).T, ((0, 0), (FL_LANE, LANES - FL_LANE - FOX_HEADS)))
            dlogf = _cumsum(dcum, reverse=True, name=ln + "_cumb")
            dh_small, dq_pre, dqg, dkg, dbf = _even_mid_bwd(
                sv["h"], dq, dk, dv, dlogf, sv["qg"], sv["kg"], sv["bft"], sv["uq_t"], sv["ukv_t"], cos_m, sin_m, name=ln + "_midb")
            duq_p = _mm(sv["qn"].T, dq_pre, out_dtype=F32, name=ln + "_dwuq")
            dkv_cat = jnp.concatenate([dk.astype(BF16), dv], axis=1)
            dukv_p = _mm(sv["kvn"].T, dkv_cat, out_dtype=F32, name=ln + "_dwukv")
            dh = jnp.concatenate([dgate, dfq, dfk, dfv, dh_small], axis=1)
            dw_in_p = _mm(sv["x_b"].T, dh, out_dtype=F32, name=ln + "_dwin")
            dw_in, duq, dukv = _even_weight_grads(dw_in_p, duq_p, dukv_p)
            for n, val in (("even_w_in", dw_in), ("even_w_uq", duq), ("even_w_ukv", dukv), ("even_w_out", dw_out),
                           ("even_q_norm", dqg[0]), ("even_kv_norm", dkg[0]), ("even_b_f", dbf[0, FL_LANE:FL_LANE + FOX_HEADS]),
                           ("even_ln_g", dg_ln[0]), ("even_ln_b", db_ln[0])):
                grads[n][j] = val
        else:
            dq_t, dkp, dvp, dsink = _swa_bwd(sv["q"], sv["kp"], sv["vp"], sv["kp"].T, sink_tiles[j], sv["o_t"], do, do.T, sv["lse"],
                                             name=ln + "_swab")
            dq_r, dk_r = _odd_rope_bwd(dq_t.T, dkp[WINDOW:], cos_s, sin_s, name=ln + "_ropeb")
            dh = jnp.concatenate([dgate, dq_r, dk_r, dvp[WINDOW:].astype(BF16)], axis=1)
            dw_in_p = _mm(sv["x_b"].T, dh, out_dtype=F32, name=ln + "_dwin")
            dw_in, dw_out = _odd_weight_grads(dw_in_p, dw_out)
            for n, val in (("odd_w_in", dw_in), ("odd_w_out", dw_out), ("odd_sinks", _interleave(dsink[None, :, 0, 0], False)[0]),
                           ("odd_ln_g", dg_ln[0]), ("odd_ln_b", db_ln[0])):
                grads[n][j] = val
        dxn = _mm(dh, sv["w_in_t"], out_dtype=F32, name=ln + "_dx", res=dr, res_scale=ALPHA)
    grad_x = dxn[None]
    late = {n: grads[n][0][None] for n in even_mats}
    late_pair = pair_sums(late, "late")
    late_sum = chip_sums(late, late_pair, _scatter_to_chips([p[0] for p in late_pair], name="grad_scatter_late"), "late")
    early_sum = chip_sums(early, early_pair, early_chips, "early")
    gshard = {n: jnp.concatenate([late_sum[n], early_sum[n]]) for n in even_mats}
    gshard.update({n: early_sum[n] for n in odd_mats})

    small = [n for n in names if n not in _SHARDED]
    grads = {n: jnp.stack(grads[n]) for n in small}
    sv_flat = jnp.concatenate([grads[n].reshape(-1) for n in small] + [loss_tile[0, :1]])
    sv_real = sv_flat.shape[0]
    sv_rows = -(-sv_real // (PACK_COLS * 8)) * 8
    sv_sum = _all_reduce_small(jnp.pad(sv_flat, (0, sv_rows * PACK_COLS - sv_real)).reshape(sv_rows, PACK_COLS), name="small_all_reduce").reshape(-1)
    off = 0
    for n in small:
        size = int(np.prod(grads[n].shape))
        gfull = sv_sum[off:off + size].reshape(grads[n].shape)
        off += size
        if n in ("odd_ln_g", "odd_ln_b"):
            gfull = lax.dynamic_slice_in_dim(gfull, chip * (D_MODEL // N_CHIPS), D_MODEL // N_CHIPS, axis=1)
        gshard[n] = gfull
    loss = sv_sum[off]

    deltas, new_m, new_v = {}, {}, {}
    for n in names:
        deltas[n], new_m[n], new_v[n] = _adamw(weights[n], gshard[n], mom_m[n], mom_v[n], name="adamw_" + n)
    return (loss, grad_x, *[gshard[n] for n in names], *[deltas[n] for n in names],
            *[new_m[n] for n in names], *[new_v[n] for n in names])
```

```python
import jax
import jax.numpy as jnp
import numpy as np
from jax import lax
from jax.experimental import pallas as pl
from jax.experimental.pallas import tpu as pltpu

F32 = jnp.float32
BF16 = jnp.bfloat16
MESH = pl.DeviceIdType.MESH

D_MODEL = 1024
DEPTH = 4
ROPE_THETA = 10000.0
MLA_HEADS, MLA_NOPE, MLA_ROPE, MLA_V = 8, 64, 32, 64
MLA_Q_RANK, MLA_KV_RANK = 256, 128
FOX_HEADS, FOX_DIM = 8, 64
SWA_HEADS, SWA_KV_HEADS, SWA_DIM, WINDOW = 16, 2, 64, 128
RMS_EPS = 1e-6
LN_EPS = 1e-5
ALPHA = (2 * DEPTH) ** 0.25
EVEN_IN = 2984
ODD_IN = 2304
ADAM_LR, ADAM_B1, ADAM_B2, ADAM_EPS, ADAM_WD, ADAM_STEP = 0.001, 0.9, 0.999, 1e-08, 0.01, 10

LANES = 128
HALF = LANES // 2
HEAD_PAD = 128
N_CHIPS = 4
N_DEV = 8
E_GATE, E_FQ, E_FK, E_FV, E_SMALL = 0, 1024, 1536, 2048, 2560
E_PAD_IN = 3072
KPE_LANE = 64
FL_LANE = 96
O_GATE, O_Q, O_K, O_V = 0, 1024, 2048, 2176

_ARB = "arbitrary"
_PAR = "parallel"


def _cparams(sem):
    return pltpu.CompilerParams(dimension_semantics=sem)


def _pick(n, cands):
    for c in cands:
        if n % c == 0:
            return c
    return n


RESIDENT_BYTES = 8 << 20


def _mm_resident(a, b, *, out_dtype, name, res, res_scale, also_bf16):
    m, k = a.shape
    _, n = b.shape
    tm = _pick(m, (256, 128))
    cn = _pick(n, (512, 384, 256, 128))
    dtypes = [out_dtype, BF16] if also_bf16 else [out_dtype]

    def body(*refs):
        a_ref, b_ref = refs[:2]
        r_ref = refs[2] if res is not None else None
        outs = refs[3 if res is not None else 2:]
        av = a_ref[...].astype(BF16)
        for c0 in range(0, n, cn):
            r = jnp.dot(av, b_ref[:, c0:c0 + cn].astype(BF16), preferred_element_type=F32)
            if res is not None:
                r = r + res_scale * r_ref[:, c0:c0 + cn]
            for o_ref in outs:
                o_ref[:, c0:c0 + cn] = r.astype(o_ref.dtype)

    rows = lambda w: pl.BlockSpec((tm, w), lambda i: (i, 0))
    in_specs = [rows(k), pl.BlockSpec((k, n), lambda i: (0, 0))] + ([rows(n)] if res is not None else [])
    out = pl.pallas_call(
        body, name=name, grid=(m // tm,), in_specs=in_specs, out_specs=[rows(n)] * len(dtypes),
        out_shape=[jax.ShapeDtypeStruct((m, n), d) for d in dtypes], compiler_params=_cparams((_PAR,)),
    )(*([a, b] + ([res] if res is not None else [])))
    return out if also_bf16 else out[0]


def _mm_left_resident(a, b, *, out_dtype, name):
    m, k = a.shape
    _, n = b.shape
    tn = _pick(n, (256, 128))

    def body(a_ref, b_ref, o_ref):
        o_ref[...] = jnp.dot(a_ref[...].astype(BF16), b_ref[...].astype(BF16), preferred_element_type=F32).astype(o_ref.dtype)

    return pl.pallas_call(
        body, name=name, grid=(n // tn,),
        in_specs=[pl.BlockSpec((m, k), lambda j: (0, 0)), pl.BlockSpec((k, tn), lambda j: (0, j))],
        out_specs=pl.BlockSpec((m, tn), lambda j: (0, j)), out_shape=jax.ShapeDtypeStruct((m, n), out_dtype),
        compiler_params=_cparams((_PAR,)),
    )(a, b)


def _mm(a, b, *, out_dtype, name, res=None, res_scale=1.0, also_bf16=False):
    m, k = a.shape
    _, n = b.shape
    if b.size * b.dtype.itemsize <= RESIDENT_BYTES:
        return _mm_resident(a, b, out_dtype=out_dtype, name=name, res=res, res_scale=res_scale, also_bf16=also_bf16)
    if a.size * a.dtype.itemsize <= RESIDENT_BYTES and res is None and not also_bf16:
        return _mm_left_resident(a, b, out_dtype=out_dtype, name=name)
    tm = _pick(m, (512, 256, 128))
    tn = _pick(n, (1024, 768, 512, 384, 256, 128))
    tk = _pick(k, (1024, 768, 512, 256, 128))
    nk = k // tk

    def body(*refs):
        refs = list(refs)
        a_ref, b_ref = refs[:2]
        r_ref = refs[2] if res is not None else None
        acc_ref = refs[-1]
        outs = refs[3 if res is not None else 2:-1]
        kk = pl.program_id(2)

        @pl.when(kk == 0)
        def _():
            acc_ref[...] = jnp.zeros_like(acc_ref)

        acc_ref[...] += jnp.dot(a_ref[...].astype(BF16), b_ref[...].astype(BF16), preferred_element_type=F32)

        @pl.when(kk == nk - 1)
        def _():
            r = acc_ref[...]
            if res is not None:
                r = r + res_scale * r_ref[...]
            for o_ref in outs:
                o_ref[...] = r.astype(o_ref.dtype)

    in_specs = [pl.BlockSpec((tm, tk), lambda i, j, kk: (i, kk)), pl.BlockSpec((tk, tn), lambda i, j, kk: (kk, j))]
    args = [a, b]
    if res is not None:
        in_specs.append(pl.BlockSpec((tm, tn), lambda i, j, kk: (i, j)))
        args.append(res)
    ospec = pl.BlockSpec((tm, tn), lambda i, j, kk: (i, j))
    dtypes = [out_dtype, BF16] if also_bf16 else [out_dtype]
    out = pl.pallas_call(
        body, name=name, grid=(m // tm, n // tn, nk), in_specs=in_specs,
        out_specs=[ospec] * len(dtypes), out_shape=[jax.ShapeDtypeStruct((m, n), d) for d in dtypes],
        scratch_shapes=[pltpu.VMEM((tm, tn), F32)],
        compiler_params=_cparams((_PAR, _PAR, _ARB)),
    )(*args)
    return out if also_bf16 else out[0]


def _rope_tile(t, cos, sin, half):
    w = t.shape[-1]
    lane = lax.broadcasted_iota(jnp.int32, t.shape, 1)
    first = (lane % (2 * half)) < half
    sw = jnp.where(first, pltpu.roll(t, w - half, 1), pltpu.roll(t, half, 1))
    return t * cos + sw * sin


def _sigmoid(x):
    return 1.0 / (1.0 + jnp.exp(-x))


def _lane_mask(shape, lo, hi):
    lane = lax.broadcasted_iota(jnp.int32, shape, 1)
    return (lane >= lo) & (lane < hi)


def _rms(x, g):
    r = lax.rsqrt(jnp.mean(x * x, axis=-1, keepdims=True) + RMS_EPS)
    u = x * r
    return u, r, u * g


def _rms_bwd(dy, u, r, g):
    dyg = dy * g
    dx = r * (dyg - u * jnp.mean(dyg * u, axis=-1, keepdims=True))
    return dx, jnp.sum(dy * u, axis=0, keepdims=True)


def _even_mid_fwd(h, qg, kg, bf_tile, w_uq, w_ukv, cos, sin, *, name):
    s = h.shape[0]
    tb = _pick(s, (512, 256, 128))
    nq = MLA_HEADS * HEAD_PAD

    def body(h_ref, qg_ref, kg_ref, bf_ref, wuq_ref, wukv_ref, cos_ref, sin_ref,
             q_ref, k_ref, v_ref, qn_ref, kvn_ref, lf_ref):
        hb = h_ref[...]
        cq, ckv, misc = hb[:, :MLA_Q_RANK], hb[:, MLA_Q_RANK:MLA_Q_RANK + MLA_KV_RANK], hb[:, MLA_Q_RANK + MLA_KV_RANK:]
        cs, sn = cos_ref[...], sin_ref[...]
        _, _, qn = _rms(cq, qg_ref[...])
        qn = qn.astype(BF16)
        qn_ref[...] = qn
        q = jnp.dot(qn, wuq_ref[...], preferred_element_type=F32)
        _, _, kvn = _rms(ckv, kg_ref[...])
        kvn = kvn.astype(BF16)
        kvn_ref[...] = kvn
        kv = jnp.dot(kvn, wukv_ref[...], preferred_element_type=F32)
        kpe = jnp.where(_lane_mask(misc.shape, KPE_LANE, KPE_LANE + MLA_ROPE), _rope_tile(misc, cs, sn, MLA_ROPE // 2), 0.0)
        for hd in range(MLA_HEADS):
            sl = slice(hd * HEAD_PAD, (hd + 1) * HEAD_PAD)
            q_ref[:, sl] = _rope_tile(q[:, sl], cs, sn, MLA_ROPE // 2).astype(BF16)
            k_ref[:, sl] = (kv[:, sl] + kpe).astype(BF16)
        v_ref[...] = kv[:, nq:].astype(BF16)
        xf = misc + bf_ref[...]
        logf = jnp.minimum(xf, 0.0) - jnp.log(1.0 + jnp.exp(-jnp.abs(xf)))
        lf_ref[...] = jnp.where(_lane_mask(misc.shape, FL_LANE, FL_LANE + FOX_HEADS), logf, 0.0)

    full = lambda a: pl.BlockSpec(a.shape, lambda i: (0,) * a.ndim)
    rows = lambda w, c=0: pl.BlockSpec((tb, w), lambda i: (i, c))
    return pl.pallas_call(
        body, name=name, grid=(s // tb,),
        in_specs=[rows(512, E_SMALL // 512), full(qg), full(kg), full(bf_tile), full(w_uq), full(w_ukv), rows(LANES), rows(LANES)],
        out_specs=[rows(nq), rows(nq), rows(MLA_HEADS * MLA_V), rows(MLA_Q_RANK), rows(MLA_KV_RANK), rows(LANES)],
        out_shape=[jax.ShapeDtypeStruct((s, nq), BF16), jax.ShapeDtypeStruct((s, nq), BF16),
                   jax.ShapeDtypeStruct((s, MLA_HEADS * MLA_V), BF16), jax.ShapeDtypeStruct((s, MLA_Q_RANK), BF16),
                   jax.ShapeDtypeStruct((s, MLA_KV_RANK), BF16), jax.ShapeDtypeStruct((s, LANES), F32)],
        compiler_params=_cparams((_PAR,)),
    )(h, qg, kg, bf_tile, w_uq, w_ukv, cos, sin)


def _even_mid_bwd(h, dq, dk, dv, dlogf, qg, kg, bf_tile, w_uq_t, w_ukv_t, cos, sin, *, name):
    s = h.shape[0]
    tb = _pick(s, (256, 128))
    nq = MLA_HEADS * HEAD_PAD

    def body(h_ref, dq_ref, dk_ref, dv_ref, dlf_ref, qg_ref, kg_ref, bf_ref, wuqt_ref, wukvt_ref, cos_ref, sin_ref,
             dh_ref, dqp_ref, dqg_ref, dkg_ref, dbf_ref):
        @pl.when(pl.program_id(0) == 0)
        def _():
            dqg_ref[...] = jnp.zeros_like(dqg_ref)
            dkg_ref[...] = jnp.zeros_like(dkg_ref)
            dbf_ref[...] = jnp.zeros_like(dbf_ref)

        hb = h_ref[...]
        cq, ckv, misc = hb[:, :MLA_Q_RANK], hb[:, MLA_Q_RANK:MLA_Q_RANK + MLA_KV_RANK], hb[:, MLA_Q_RANK + MLA_KV_RANK:]
        cs, sn = cos_ref[...], -sin_ref[...]
        dkpe = jnp.zeros(misc.shape, F32)
        for hd in range(MLA_HEADS):
            sl = slice(hd * HEAD_PAD, (hd + 1) * HEAD_PAD)
            dqp_ref[:, sl] = _rope_tile(dq_ref[:, sl], cs, sn, MLA_ROPE // 2).astype(BF16)
            dkpe = dkpe + dk_ref[:, sl]
        dqn = jnp.dot(dqp_ref[...], wuqt_ref[...], preferred_element_type=F32)
        uq, rq, _ = _rms(cq, qg_ref[...])
        dcq, dqg = _rms_bwd(dqn, uq, rq, qg_ref[...])
        dqg_ref[...] += dqg
        dkv = jnp.concatenate([dk_ref[...].astype(BF16), dv_ref[...]], axis=1)
        dkvn = jnp.dot(dkv, wukvt_ref[...], preferred_element_type=F32)
        uk, rk, _ = _rms(ckv, kg_ref[...])
        dckv, dkg = _rms_bwd(dkvn, uk, rk, kg_ref[...])
        dkg_ref[...] += dkg
        dmisc = jnp.where(_lane_mask(misc.shape, KPE_LANE, KPE_LANE + MLA_ROPE), _rope_tile(dkpe, cs, sn, MLA_ROPE // 2), 0.0)
        dfl = jnp.where(_lane_mask(misc.shape, FL_LANE, FL_LANE + FOX_HEADS), dlf_ref[...] * _sigmoid(-(misc + bf_ref[...])), 0.0)
        dbf_ref[...] += jnp.sum(dfl, axis=0, keepdims=True)
        dh_ref[:, :MLA_Q_RANK] = dcq.astype(BF16)
        dh_ref[:, MLA_Q_RANK:MLA_Q_RANK + MLA_KV_RANK] = dckv.astype(BF16)
        dh_ref[:, MLA_Q_RANK + MLA_KV_RANK:] = (dmisc + dfl).astype(BF16)

    full = lambda a: pl.BlockSpec(a.shape, lambda i: (0,) * a.ndim)
    rows = lambda w, c=0: pl.BlockSpec((tb, w), lambda i: (i, c))
    return pl.pallas_call(
        body, name=name, grid=(s // tb,),
        in_specs=[rows(512, E_SMALL // 512), rows(nq), rows(nq), rows(MLA_HEADS * MLA_V), rows(LANES), full(qg), full(kg),
                  full(bf_tile), full(w_uq_t), full(w_ukv_t), rows(LANES), rows(LANES)],
        out_specs=[rows(512), rows(nq), full(qg), full(kg), full(bf_tile)],
        out_shape=[jax.ShapeDtypeStruct((s, 512), BF16), jax.ShapeDtypeStruct((s, nq), BF16),
                   jax.ShapeDtypeStruct(qg.shape, F32), jax.ShapeDtypeStruct(kg.shape, F32),
                   jax.ShapeDtypeStruct(bf_tile.shape, F32)],
        compiler_params=_cparams((_ARB,)),
    )(h, dq, dk, dv, dlogf, qg, kg, bf_tile, w_uq_t, w_ukv_t, cos, sin)


def _cumsum(x, *, reverse, name):
    s = x.shape[0]
    tb = _pick(s, (512, 256, 128))
    nb = s // tb

    def body(x_ref, o_ref, carry_ref):
        @pl.when(pl.program_id(0) == 0)
        def _():
            carry_ref[...] = jnp.zeros_like(carry_ref)

        xv = x_ref[...]
        r = lax.broadcasted_iota(jnp.int32, (tb, tb), 0)
        c = lax.broadcasted_iota(jnp.int32, (tb, tb), 1)
        tri = jnp.where((c >= r) if reverse else (c <= r), 1.0, 0.0).astype(BF16)
        hi = xv.astype(BF16)
        r1 = xv - hi.astype(F32)
        mid = r1.astype(BF16)
        lo = (r1 - mid.astype(F32)).astype(BF16)
        cs = (jnp.dot(tri, hi, preferred_element_type=F32) + jnp.dot(tri, mid, preferred_element_type=F32)
              + jnp.dot(tri, lo, preferred_element_type=F32)) + carry_ref[...]
        o_ref[...] = cs
        carry_ref[...] = cs[0:1, :] if reverse else cs[tb - 1:tb, :]

    imap = (lambda i: (nb - 1 - i, 0)) if reverse else (lambda i: (i, 0))
    return pl.pallas_call(
        body, name=name, grid=(nb,), in_specs=[pl.BlockSpec((tb, LANES), imap)],
        out_specs=pl.BlockSpec((tb, LANES), imap), out_shape=jax.ShapeDtypeStruct(x.shape, F32),
        scratch_shapes=[pltpu.VMEM((1, LANES), F32)], compiler_params=_cparams((_ARB,)),
    )(x)


_NT = (((1,), (1,)), ((), ()))
_TN = (((0,), (0,)), ((), ()))


def _head_sel(x, h, dk):
    if dk == LANES:
        return x[:, h * LANES:(h + 1) * LANES]
    return jnp.where(_lane_mask(x.shape, h * HALF, (h + 1) * HALF), x, jnp.zeros_like(x))


def _head_other(x, h, dk):
    return x[:, h * LANES:(h + 1) * LANES] if dk == LANES else x


def _pair(parts, dk=HALF):
    if dk == LANES:
        return jnp.concatenate(parts, axis=1)
    return jnp.where(_lane_mask(parts[0].shape, 0, HALF), parts[0], parts[1])


def _attn_fwd(q, k, v, crow, *, dk, npairs, scale, name, gather=()):
    (qa, qo), (ka, ko), (va, vo) = q, k, v
    s = qa.shape[0]
    wq = 2 * dk
    t = _pick(s, (512, 256, 128))
    nb = s // t
    bias = crow is not None
    ng = len(gather)

    def body(*refs):
        n_in = 3 + bias
        q_ref, k_ref, v_ref = refs[:3]
        cr_ref = refs[3] if bias else None
        o_ref, lse_ref = refs[n_in + ng:n_in + ng + 2]
        qi = pl.program_id(1)
        if ng:
            phases = _gather_phases([a.shape for a in gather], refs[n_in:n_in + ng], refs[n_in + ng + 2:n_in + 2 * ng + 2],
                                    *refs[n_in + 2 * ng + 2:])
            pair = pl.program_id(0)
            pl.when((pair == 0) & (qi == 0))(phases[0])
            pl.when((pair == npairs - 1) & (qi == 0))(phases[1])
        qb = q_ref[...]
        qs = [_head_sel(qb, h, dk) for h in range(2)]

        def scores(j):
            rows = pl.ds(pl.multiple_of(j * t, t), t)
            kb = k_ref[rows, :]
            out = []
            for h in range(2):
                sc = lax.dot_general(qs[h], _head_other(kb, h, dk), _NT, preferred_element_type=F32) * scale
                out.append(sc - cr_ref[h, j] if bias else sc)
            return tuple(out)

        def update(scs, j, state):
            rows = pl.ds(pl.multiple_of(j * t, t), t)
            vb = v_ref[rows, :]
            soft = []
            for h in range(2):
                m, l, _ = state[3 * h:3 * h + 3]
                sc = scs[h]
                m_new = jnp.maximum(m, jnp.max(sc, axis=1, keepdims=True))
                a = jnp.exp(m - m_new)
                p = jnp.exp(sc - m_new)
                soft.append((m_new, a * l + jnp.sum(p, axis=1, keepdims=True), a, p.astype(BF16)))
            new = []
            for h in range(2):
                m_new, l, a, p = soft[h]
                new += [m_new, l, a * state[3 * h + 2] + jnp.dot(p, vb, preferred_element_type=F32)]
            return tuple(new)

        one = (jnp.full((t, 1), -jnp.inf, F32), jnp.zeros((t, 1), F32), jnp.zeros((t, LANES), F32))
        state = lax.fori_loop(0, qi, lambda j, st: update(scores(j), j, st), one + one)
        row = lax.broadcasted_iota(jnp.int32, (t, t), 0)
        col = lax.broadcasted_iota(jnp.int32, (t, t), 1)
        diag = tuple(jnp.where(col <= row, sc, -jnp.inf) for sc in scores(qi))
        state = update(diag, qi, state)
        outs = []
        for h in range(2):
            m, l, acc = state[3 * h:3 * h + 3]
            outs.append(acc / l)
            lse_ref[h] = m + jnp.log(l)
        o_ref[...] = _pair(outs)
        if ng:
            pl.when((pair == npairs - 1) & (qi == nb - 1))(phases[2])

    in_specs = [pl.BlockSpec((t, wq), lambda p, i: (i, qo // wq + p)), pl.BlockSpec((s, wq), lambda p, i: (0, ko // wq + p)),
                pl.BlockSpec((s, LANES), lambda p, i: (0, vo // LANES + p))]
    args = [qa, ka, va]
    if bias:
        in_specs.append(pl.BlockSpec((2, nb, 1, t), lambda p, i: (p, 0, 0, 0)))
        args.append(crow.reshape(2 * npairs, nb, 1, t))
    out = pl.pallas_call(
        body, name=name, grid=(npairs, nb), in_specs=in_specs + [_ANY] * ng,
        out_specs=[pl.BlockSpec((t, LANES), lambda p, i: (i, p)), pl.BlockSpec((2, t, 1), lambda p, i: (p, i, 0))] + [_ANY] * ng,
        out_shape=[jax.ShapeDtypeStruct((s, npairs * LANES), F32), jax.ShapeDtypeStruct((2 * npairs, s, 1), F32)]
        + [jax.ShapeDtypeStruct((N_CHIPS,) + a.shape, a.dtype) for a in gather],
        scratch_shapes=_sem_pairs(GATHER_COPIES * ng) if ng else [],
        compiler_params=_cparams((_ARB, _ARB) if ng else (_PAR, _PAR)),
    )(*args, *gather)
    return (out[0], out[1], out[2:]) if ng else (out[0], out[1])


def _attn_bwd(q, k, v, o, do, lse, crow, *, dk, npairs, scale, dq_dtype, name, scatter=()):
    (qa, qo), (ka, ko), (va, vo), (oa, oo), (da, do_o) = q, k, v, o, do
    s = qa.shape[0]
    wq = 2 * dk
    t = _pick(s, (512, 256, 128))
    nb = s // t
    bias = crow is not None
    ns = len(scatter)

    def body(*refs):
        n_in, n_out = 6 + bias, 3 + 2 * bias
        q_ref, k_ref, v_ref, o_ref, do_ref, lse_ref = refs[:6]
        outs = refs[n_in + ns:n_in + ns + n_out]
        scr = refs[n_in + 2 * ns + n_out:]
        dq_ref, dk_ref, dv_ref = outs[:3]
        dq_s, dk_s, dv_s = scr[:3]
        if bias:
            cr_ref, dc_ref, dcc_ref, dc_s = refs[6], outs[3], outs[4], scr[3]
        ki, qi = pl.program_id(1), pl.program_id(2)
        if ns:
            send_off, finish = _scatter_phases(ns, refs[n_in:n_in + ns], refs[n_in + ns + n_out:n_in + 2 * ns + n_out], *scr[3 + bias:])
            pair = pl.program_id(0)
            pl.when((pair == 0) & (ki == 0) & (qi == 0))(send_off)

        @pl.when((ki == 0) & (qi == 0))
        def _():
            dq_s[...] = jnp.zeros_like(dq_s)
            if bias:
                dcc_ref[...] = jnp.zeros_like(dcc_ref)

        @pl.when(qi == ki)
        def _():
            dk_s[...] = jnp.zeros_like(dk_s)
            dv_s[...] = jnp.zeros_like(dv_s)
            if bias:
                dc_s[...] = jnp.zeros_like(dc_s)

        def block(on_diagonal):
            qb, kb, vb, dob, ob = q_ref[...], k_ref[...], v_ref[...], do_ref[...], o_ref[...]
            rows = pl.ds(pl.multiple_of(qi * t, t), t)
            row = lax.broadcasted_iota(jnp.int32, (t, t), 0) + qi * t
            col = lax.broadcasted_iota(jnp.int32, (t, t), 1) + ki * t
            fed = []
            for h in range(2):
                kh, doh = _head_other(kb, h, dk), _head_sel(dob, h, HALF)
                sc = lax.dot_general(_head_sel(qb, h, dk), kh, _NT, preferred_element_type=F32)
                dp = lax.dot_general(doh, vb, _NT, preferred_element_type=F32)
                fed.append((kh, doh, sc, dp))
            mid = []
            for h in range(2):
                kh, doh, sc, dp = fed[h]
                sc = sc * scale
                if bias:
                    sc = sc - cr_ref[h]
                if on_diagonal:
                    sc = jnp.where(col <= row, sc, -jnp.inf)
                p = jnp.exp(sc - lse_ref[h])
                delta = jnp.sum(doh.astype(F32) * ob, axis=1, keepdims=True)
                ds = p * (dp - delta)
                if bias:
                    dc_s[h] -= jnp.sum(ds, axis=0, keepdims=True)
                    dcc_ref[h, rows, :] += jnp.sum(ds, axis=1, keepdims=True)
                mid.append((kh, p.astype(BF16), (ds * scale).astype(BF16)))
            dq_parts, dk_parts, dv_parts = [], [], []
            for h in range(2):
                kh, pb, dsb = mid[h]
                dv_parts.append(lax.dot_general(pb, dob, _TN, preferred_element_type=F32))
                dk_parts.append(lax.dot_general(dsb, _head_other(qb, h, dk), _TN, preferred_element_type=F32))
                dq_parts.append(jnp.dot(dsb, kh, preferred_element_type=F32))
            dv_s[...] += _pair(dv_parts)
            dk_s[...] += _pair(dk_parts, dk)
            dq_s[rows, :] += _pair(dq_parts, dk)

        pl.when(qi > ki)(lambda: block(False))
        pl.when(qi == ki)(lambda: block(True))

        @pl.when(qi == nb - 1)
        def _():
            dk_ref[...] = dk_s[...].astype(dk_ref.dtype)
            dv_ref[...] = dv_s[...].astype(dv_ref.dtype)
            if bias:
                dc_ref[...] = dc_s[...]

        @pl.when((ki == nb - 1) & (qi == nb - 1))
        def _():
            dq_ref[...] = dq_s[...].astype(dq_ref.dtype)

        if ns:
            pl.when((pair == npairs - 1) & (ki == nb - 1) & (qi == nb - 1))(finish)

    qrow = lambda p, j, i: jnp.maximum(i, j)
    in_specs = [pl.BlockSpec((t, wq), lambda p, j, i: (qrow(p, j, i), qo // wq + p)),
                pl.BlockSpec((t, wq), lambda p, j, i: (j, ko // wq + p)),
                pl.BlockSpec((t, LANES), lambda p, j, i: (j, vo // LANES + p)),
                pl.BlockSpec((t, LANES), lambda p, j, i: (qrow(p, j, i), oo // LANES + p)),
                pl.BlockSpec((t, LANES), lambda p, j, i: (qrow(p, j, i), do_o // LANES + p)),
                pl.BlockSpec((2, t, 1), lambda p, j, i: (p, qrow(p, j, i), 0))]
    args = [qa, ka, va, oa, da, lse]
    out_specs = [pl.BlockSpec((s, wq), lambda p, j, i: (0, p)), pl.BlockSpec((t, wq), lambda p, j, i: (j, p)),
                 pl.BlockSpec((t, LANES), lambda p, j, i: (j, p))]
    out_shape = [jax.ShapeDtypeStruct((s, npairs * wq), dq_dtype), jax.ShapeDtypeStruct((s, npairs * wq), dq_dtype),
                 jax.ShapeDtypeStruct((s, npairs * LANES), BF16)]
    scratch = [pltpu.VMEM((s, wq), F32), pltpu.VMEM((t, wq), F32), pltpu.VMEM((t, LANES), F32)]
    if bias:
        in_specs.append(pl.BlockSpec((2, 1, t), lambda p, j, i: (p, 0, j)))
        args.append(crow)
        out_specs += [pl.BlockSpec((2, 1, t), lambda p, j, i: (p, 0, j)), pl.BlockSpec((2, s, 1), lambda p, j, i: (p, 0, 0))]
        out_shape += [jax.ShapeDtypeStruct((2 * npairs, 1, s), F32), jax.ShapeDtypeStruct((2 * npairs, s, 1), F32)]
        scratch.append(pltpu.VMEM((2, 1, t), F32))
    n_out = len(out_shape)
    out = pl.pallas_call(
        body, name=name, grid=(npairs, nb, nb), in_specs=in_specs + [_ANY] * ns, out_specs=out_specs + [_ANY] * ns,
        out_shape=out_shape + [jax.ShapeDtypeStruct((3,) + a.shape[1:], a.dtype) for a in scatter],
        scratch_shapes=scratch + (_sem_pairs(3 * ns) if ns else []),
        compiler_params=_cparams((_ARB if ns else _PAR, _ARB, _ARB)),
    )(*args, *scatter)
    return (*out[:n_out], out[n_out:]) if ns else out


def _sub_mask(shape, lo, hi):
    sub = lax.broadcasted_iota(jnp.int32, shape, 0)
    return (sub >= lo) & (sub < hi)


def _pair_rows(parts):
    return jnp.where(_sub_mask(parts[0].shape, 0, HALF), parts[0], parts[1])


def _swa_valid(start, t):
    krow = lax.broadcasted_iota(jnp.int32, (t + WINDOW, t), 0)
    qcol = lax.broadcasted_iota(jnp.int32, (t + WINDOW, t), 1)
    diff = qcol - krow + WINDOW
    return (diff >= 0) & (diff < WINDOW) & (krow + start >= WINDOW)


def _swa_tiles(s):
    t = _pick(s, (256, 128))
    return t, _pick(s // t, (4, 2, 1))


def _swa_fwd(q, kp, vp, sink, *, name):
    s = q.shape[0]
    npairs = SWA_HEADS // 2
    t, nt = _swa_tiles(s)
    scale = SWA_DIM ** -0.5

    def body(q_ref, k_ref, v_ref, sk_ref, o_ref, lse_ref):
        sts, vws = [], []
        for u in range(nt):
            start = pl.multiple_of((pl.program_id(1) * nt + u) * t, t)
            kw = k_ref[pl.ds(start, t + WINDOW), :]
            vws.append(v_ref[pl.ds(start, t + WINDOW), :])
            qb = q_ref[u * t:(u + 1) * t, :]
            valid = _swa_valid(start, t)
            for h in range(2):
                st = lax.dot_general(kw, _head_sel(qb, h, HALF), _NT, preferred_element_type=F32) * scale
                sts.append(jnp.where(valid, st, -jnp.inf))
        pts = []
        for u in range(nt):
            for h in range(2):
                st = sts[2 * u + h]
                snk = sk_ref[h][:, 0:1]
                m = jnp.maximum(jnp.max(st, axis=0, keepdims=True), snk)
                e = jnp.exp(st - m)
                l = jnp.sum(e, axis=0, keepdims=True) + jnp.exp(snk - m)
                pts.append((e * (1.0 / l)).astype(BF16))
                lse_ref[h, :, u * t:(u + 1) * t] = m + jnp.log(l)
        for u in range(nt):
            outs = [lax.dot_general(vws[u], pts[2 * u + h], _TN, preferred_element_type=F32) for h in range(2)]
            o_ref[:, u * t:(u + 1) * t] = _pair_rows(outs)

    kvspec = pl.BlockSpec((s + WINDOW, LANES), lambda p, i: (0, 0))
    return pl.pallas_call(
        body, name=name, grid=(npairs, s // (t * nt)),
        in_specs=[pl.BlockSpec((t * nt, LANES), lambda p, i: (i, p)), kvspec, kvspec, pl.BlockSpec((2, 1, LANES), lambda p, i: (p, 0, 0))],
        out_specs=[pl.BlockSpec((LANES, t * nt), lambda p, i: (p, i)), pl.BlockSpec((2, 1, t * nt), lambda p, i: (p, 0, i))],
        out_shape=[jax.ShapeDtypeStruct((npairs * LANES, s), F32), jax.ShapeDtypeStruct((SWA_HEADS, 1, s), F32)],
        compiler_params=_cparams((_PAR, _PAR)),
    )(q, kp, vp, sink)


def _swa_bwd(q, kp, vp, kpt, sink, ot, do, dot, lse, *, name):
    s = q.shape[0]
    npairs = SWA_HEADS // 2
    t, nt = _swa_tiles(s)
    scale = SWA_DIM ** -0.5

    def body(q_ref, k_ref, v_ref, kt_ref, sk_ref, ot_ref, do_ref, dot_ref, lse_ref, dq_ref, dk_ref, dv_ref, dsk_ref):
        pp, i = pl.program_id(0), pl.program_id(1)

        @pl.when((pp == 0) & (i == 0))
        def _():
            dk_ref[...] = jnp.zeros_like(dk_ref)
            dv_ref[...] = jnp.zeros_like(dv_ref)

        @pl.when(i == 0)
        def _():
            dsk_ref[...] = jnp.zeros_like(dsk_ref)

        fed = []
        for u in range(nt):
            cols = slice(u * t, (u + 1) * t)
            start = pl.multiple_of((i * nt + u) * t, t)
            win = pl.ds(start, t + WINDOW)
            qb, dob = q_ref[cols, :], do_ref[cols, :]
            kw, vw = k_ref[win, :], v_ref[win, :]
            valid = _swa_valid(start, t)
            for h in range(2):
                st = lax.dot_general(kw, _head_sel(qb, h, HALF), _NT, preferred_element_type=F32)
                dpt = lax.dot_general(vw, _head_sel(dob, h, HALF), _NT, preferred_element_type=F32)
                fed.append((cols, win, qb, dob, valid, st, dpt))
        mid = []
        for u in range(nt):
            cols = fed[2 * u][0]
            prod = dot_ref[:, cols].astype(F32) * ot_ref[:, cols]
            for h in range(2):
                valid, st, dpt = fed[2 * u + h][4:]
                lse_b = lse_ref[h, :, cols]
                pt = jnp.exp(jnp.where(valid, st * scale, -jnp.inf) - lse_b)
                delta = jnp.sum(jnp.where(_sub_mask(prod.shape, h * HALF, (h + 1) * HALF), prod, 0.0), axis=0, keepdims=True)
                dst = pt * (dpt - delta)
                psink = jnp.exp(sk_ref[h][:, 0:1] - lse_b)
                dsk_ref[h] += jnp.broadcast_to(-jnp.sum(psink * delta, axis=1, keepdims=True), (1, LANES))
                mid.append((pt.astype(BF16), (dst * scale).astype(BF16)))
        for u in range(nt):
            cols, win, qb, dob = fed[2 * u][:4]
            ktw = kt_ref[:, win]
            dq_parts = [jnp.dot(ktw, mid[2 * u + h][1], preferred_element_type=F32) for h in range(2)]
            dk_parts = [jnp.dot(mid[2 * u + h][1], qb, preferred_element_type=F32) for h in range(2)]
            dv_parts = [jnp.dot(mid[2 * u + h][0], dob, preferred_element_type=F32) for h in range(2)]
            dq_ref[:, cols] = _pair_rows(dq_parts)
            dk_ref[win, :] += _pair(dk_parts)
            dv_ref[win, :] += _pair(dv_parts)

    tile = pl.BlockSpec((t * nt, LANES), lambda p, i: (i, p))
    ttile = pl.BlockSpec((LANES, t * nt), lambda p, i: (p, i))
    kvspec = pl.BlockSpec((s + WINDOW, LANES), lambda p, i: (0, 0))
    ktspec = pl.BlockSpec((LANES, s + WINDOW), lambda p, i: (0, 0))
    skspec = pl.BlockSpec((2, 1, LANES), lambda p, i: (p, 0, 0))
    return pl.pallas_call(
        body, name=name, grid=(npairs, s // (t * nt)),
        in_specs=[tile, kvspec, kvspec, ktspec, skspec, ttile, tile, ttile, pl.BlockSpec((2, 1, t * nt), lambda p, i: (p, 0, i))],
        out_specs=[ttile, kvspec, kvspec, skspec],
        out_shape=[jax.ShapeDtypeStruct((npairs * LANES, s), F32), jax.ShapeDtypeStruct((s + WINDOW, LANES), F32),
                   jax.ShapeDtypeStruct((s + WINDOW, LANES), F32), jax.ShapeDtypeStruct((SWA_HEADS, 1, LANES), F32)],
        compiler_params=_cparams((_ARB, _ARB)),
    )(q, kp, vp, kpt, sink, ot, do, dot, lse)


def _odd_rope(h, cos, sin, *, name):
    s = h.shape[0]
    tb = _pick(s, (512, 256, 128))
    nq = SWA_HEADS * SWA_DIM

    def body(q_ref, kv_ref, cos_ref, sin_ref, qo_ref, ko_ref, vo_ref):
        cs, sn = cos_ref[...], sin_ref[...]
        for j in range(nq // LANES):
            sl = slice(j * LANES, (j + 1) * LANES)
            qo_ref[:, sl] = _rope_tile(q_ref[:, sl], cs, sn, SWA_DIM // 2).astype(BF16)
        ko_ref[...] = _rope_tile(kv_ref[:, :LANES], cs, sn, SWA_DIM // 2).astype(BF16)
        vo_ref[...] = kv_ref[:, LANES:].astype(BF16)

    rows = lambda w, c=0: pl.BlockSpec((tb, w), lambda i: (i, c))
    return pl.pallas_call(
        body, name=name, grid=(s // tb,),
        in_specs=[rows(nq, O_Q // nq), rows(2 * LANES, O_K // (2 * LANES)), rows(LANES), rows(LANES)],
        out_specs=[rows(nq), rows(LANES), rows(LANES)],
        out_shape=[jax.ShapeDtypeStruct((s, nq), BF16), jax.ShapeDtypeStruct((s, LANES), BF16), jax.ShapeDtypeStruct((s, LANES), BF16)],
        compiler_params=_cparams((_PAR,)),
    )(h, h, cos, sin)


def _odd_rope_bwd(dq_t, dk, cos, sin, *, name):
    s = dk.shape[0]
    tb = _pick(s, (256, 128))
    nq = SWA_HEADS * SWA_DIM

    def body(dq_ref, dk_ref, cos_ref, sin_ref, qo_ref, ko_ref):
        cs, sn = cos_ref[...], -sin_ref[...]
        dq = dq_ref[...].T
        for j in range(nq // LANES):
            sl = slice(j * LANES, (j + 1) * LANES)
            qo_ref[:, sl] = _rope_tile(dq[:, sl], cs, sn, SWA_DIM // 2).astype(BF16)
        ko_ref[...] = _rope_tile(dk_ref[...], cs, sn, SWA_DIM // 2).astype(BF16)

    rows = lambda w: pl.BlockSpec((tb, w), lambda i: (i, 0))
    return pl.pallas_call(
        body, name=name, grid=(s // tb,),
        in_specs=[pl.BlockSpec((nq, tb), lambda i: (0, i)), rows(LANES), rows(LANES), rows(LANES)],
        out_specs=[rows(nq), rows(LANES)],
        out_shape=[jax.ShapeDtypeStruct((s, nq), BF16), jax.ShapeDtypeStruct((s, LANES), BF16)],
        compiler_params=_cparams((_PAR,)),
    )(dq_t, dk, cos, sin)


def _mixer_out(o_refs, o_t):
    if o_t:
        return o_refs[0][...].T
    return o_refs[0][...] if len(o_refs) == 1 else jnp.concatenate([r[...] for r in o_refs], axis=1)


def _mixer_specs(o_parts, o_t, tb):
    if o_t:
        return [pl.BlockSpec((D_MODEL, tb), lambda i: (0, i))]
    return [pl.BlockSpec((tb, a.shape[1]), lambda i: (i, 0)) for a in o_parts]


def _out_fwd(o_parts, h, x, w_out, g, b, *, name, o_t=False):
    s = x.shape[0]
    tb = _pick(s, (256, 128))
    n_o = len(o_parts)

    def body(*refs):
        o_refs = refs[:n_o]
        gate_ref, x_ref, w_ref, g_ref, b_ref, xn_ref, xb_ref, z_ref, xh_ref, rs_ref = refs[n_o:]
        gate = gate_ref[...]
        o = _mixer_out(o_refs, o_t)
        z = (o * (gate * _sigmoid(gate))).astype(BF16)
        z_ref[...] = z
        r = ALPHA * x_ref[...] + jnp.dot(z, w_ref[...], preferred_element_type=F32)
        mu = jnp.mean(r, axis=-1, keepdims=True)
        rc = r - mu
        rstd = lax.rsqrt(jnp.mean(rc * rc, axis=-1, keepdims=True) + LN_EPS)
        xh = rc * rstd
        xn = xh * g_ref[...] + b_ref[...]
        xh_ref[...] = xh
        rs_ref[...] = rstd
        xn_ref[...] = xn
        xb_ref[...] = xn.astype(BF16)

    rows = lambda w: pl.BlockSpec((tb, w), lambda i: (i, 0))
    full = lambda a: pl.BlockSpec(a.shape, lambda i: (0,) * a.ndim)
    return pl.pallas_call(
        body, name=name, grid=(s // tb,),
        in_specs=_mixer_specs(o_parts, o_t, tb) + [rows(D_MODEL), rows(D_MODEL), full(w_out), full(g), full(b)],
        out_specs=[rows(D_MODEL), rows(D_MODEL), rows(D_MODEL), rows(D_MODEL), rows(1)],
        out_shape=[jax.ShapeDtypeStruct((s, D_MODEL), F32), jax.ShapeDtypeStruct((s, D_MODEL), BF16),
                   jax.ShapeDtypeStruct((s, D_MODEL), BF16), jax.ShapeDtypeStruct((s, D_MODEL), F32),
                   jax.ShapeDtypeStruct((s, 1), F32)],
        compiler_params=_cparams((_PAR,)),
    )(*o_parts, h, x, w_out, g, b)


def _out_bwd(dxn, xh, rstd, g, w_out_t, o_parts, h, *, name, o_t=False):
    s = dxn.shape[0]
    tb = _pick(s, (256, 128))
    n_o = len(o_parts)

    def body(*refs):
        dxn_ref, xh_ref, rs_ref, g_ref, wt_ref = refs[:5]
        o_refs = refs[5:5 + n_o]
        gate_ref, dr_ref, dy_ref, do_ref, dgate_ref, dg_ref, db_ref = refs[5 + n_o:12 + n_o]

        @pl.when(pl.program_id(0) == 0)
        def _():
            dg_ref[...] = jnp.zeros_like(dg_ref)
            db_ref[...] = jnp.zeros_like(db_ref)

        dxn_b, xh_b = dxn_ref[...], xh_ref[...]
        dg_ref[...] += jnp.sum(dxn_b * xh_b, axis=0, keepdims=True)
        db_ref[...] += jnp.sum(dxn_b, axis=0, keepdims=True)
        dxh = dxn_b * g_ref[...]
        dr = rs_ref[...] * (dxh - jnp.mean(dxh, axis=-1, keepdims=True) - xh_b * jnp.mean(dxh * xh_b, axis=-1, keepdims=True))
        dr_ref[...] = dr
        dy = dr.astype(BF16)
        dy_ref[...] = dy
        dz = jnp.dot(dy, wt_ref[...], preferred_element_type=F32)
        gate = gate_ref[...]
        sg = _sigmoid(gate)
        o = _mixer_out(o_refs, o_t)
        do = dz * (gate * sg)
        do_ref[...] = do.astype(BF16)
        if o_t:
            refs[12 + n_o][...] = do.T.astype(BF16)
        dgate_ref[...] = (dz * o * (sg * (1.0 + gate * (1.0 - sg)))).astype(BF16)

    rows = lambda w: pl.BlockSpec((tb, w), lambda i: (i, 0))
    full = lambda a: pl.BlockSpec(a.shape, lambda i: (0,) * a.ndim)
    cols = pl.BlockSpec((D_MODEL, tb), lambda i: (0, i))
    return pl.pallas_call(
        body, name=name, grid=(s // tb,),
        in_specs=[rows(D_MODEL), rows(D_MODEL), rows(1), full(g), full(w_out_t)] + _mixer_specs(o_parts, o_t, tb) + [rows(D_MODEL)],
        out_specs=[rows(D_MODEL), rows(D_MODEL), rows(D_MODEL), rows(D_MODEL), full(g), full(g)] + [cols] * o_t,
        out_shape=[jax.ShapeDtypeStruct((s, D_MODEL), F32), jax.ShapeDtypeStruct((s, D_MODEL), BF16),
                   jax.ShapeDtypeStruct((s, D_MODEL), BF16), jax.ShapeDtypeStruct((s, D_MODEL), BF16),
                   jax.ShapeDtypeStruct(g.shape, F32), jax.ShapeDtypeStruct(g.shape, F32)]
        + [jax.ShapeDtypeStruct((D_MODEL, s), BF16)] * o_t,
        compiler_params=_cparams((_ARB,)),
    )(dxn, xh, rstd, g, w_out_t, *o_parts, h)


def _loss_grad(y, target, *, name):
    s, d = y.shape
    tb = _pick(s, (512, 256, 128))

    def body(y_ref, t_ref, dy_ref, l_ref):
        @pl.when(pl.program_id(0) == 0)
        def _():
            l_ref[...] = jnp.zeros_like(l_ref)

        err = y_ref[...] - t_ref[...]
        dy_ref[...] = err * (1.0 / d)
        per_tok = jnp.mean(err * err, axis=-1, keepdims=True)
        l_ref[...] += 0.5 * jnp.sum(per_tok, axis=0, keepdims=True)

    rows = pl.BlockSpec((tb, d), lambda i: (i, 0))
    return pl.pallas_call(
        body, name=name, grid=(s // tb,), in_specs=[rows, rows],
        out_specs=[rows, pl.BlockSpec((8, LANES), lambda i: (0, 0))],
        out_shape=[jax.ShapeDtypeStruct((s, d), F32), jax.ShapeDtypeStruct((8, LANES), F32)],
        compiler_params=_cparams((_ARB,)),
    )(y, target)


def _adamw(w, g, m, v, *, name):
    shape = w.shape
    w3, g3, m3, v3 = (a.reshape((1,) * (3 - a.ndim) + a.shape) for a in (w, g, m, v))
    a0, a1, a2 = w3.shape
    tb = _pick(a1, (256, 128)) if a1 % 8 == 0 else a1
    c1 = 1.0 - ADAM_B1 ** ADAM_STEP
    c2 = 1.0 - ADAM_B2 ** ADAM_STEP

    def body(w_ref, g_ref, m_ref, v_ref, d_ref, mo_ref, vo_ref):
        gg = g_ref[...]
        mn = ADAM_B1 * m_ref[...] + (1.0 - ADAM_B1) * gg
        vn = ADAM_B2 * v_ref[...] + (1.0 - ADAM_B2) * (gg * gg)
        mo_ref[...] = mn
        vo_ref[...] = vn
        d_ref[...] = -ADAM_LR * ((mn / c1) / (jnp.sqrt(vn / c2) + ADAM_EPS) + ADAM_WD * w_ref[...])

    spec = pl.BlockSpec((1, tb, a2), lambda i, j: (i, j, 0))
    outs = pl.pallas_call(
        body, name=name, grid=(a0, a1 // tb), in_specs=[spec] * 4, out_specs=[spec] * 3,
        out_shape=[jax.ShapeDtypeStruct(w3.shape, F32)] * 3, compiler_params=_cparams((_PAR, _PAR)),
    )(w3, g3, m3, v3)
    return tuple(a.reshape(shape) for a in outs)


def _place():
    x, y, c = lax.axis_index("x"), lax.axis_index("y"), lax.axis_index("c")
    return x, y, c, [(1 - x, y), (x, 1 - y), (1 - x, 1 - y)]


_ANY = pl.BlockSpec(memory_space=pl.ANY)


def _sem_pairs(n):
    return [pltpu.SemaphoreType.DMA((n,)), pltpu.SemaphoreType.DMA((n,))]


GATHER_COPIES = 7


def _gather_phases(shapes, srcs, outs, send_sems, recv_sems):
    nt = len(shapes)
    x, y, c, chips = _place()
    me = 2 * x + y
    sib = (x, y, 1 - c)

    def half(t, chip, hc):
        rh = shapes[t][1] // 2
        return outs[t].at[chip, :, pl.ds(hc * rh, rh), :]

    def copy(t, kk, s_ref, d_ref, to):
        return pltpu.make_async_remote_copy(src_ref=s_ref, dst_ref=d_ref, send_sem=send_sems.at[GATHER_COPIES * t + kk],
                                            recv_sem=recv_sems.at[GATHER_COPIES * t + kk], device_id=to, device_id_type=MESH)

    def first_hop(t, j):
        rh = shapes[t][1] // 2
        cx, cy = chips[j]
        return copy(t, j, srcs[t].at[:, pl.ds(c * rh, rh), :], half(t, me, c), (cx, cy, c))

    def passed(t, j):
        cx, cy = chips[j]
        landed = half(t, 2 * cx + cy, c)
        return copy(t, 3 + j, landed, landed, sib)

    def own(t):
        return copy(t, 6, srcs[t], outs[t].at[me], sib)

    pairs = [(t, j) for j in range(3) for t in range(nt)]

    def start():
        for t, j in pairs:
            first_hop(t, j).start()
        for t in range(nt):
            own(t).start()

    def pass_on():
        for t, j in pairs:
            cx, cy = chips[j]
            landed = half(t, 2 * cx + cy, c)
            copy(t, j, landed, landed, (cx, cy, c)).wait_recv()
            passed(t, j).start()

    def finish():
        for t, j in pairs:
            cx, cy = chips[j]
            theirs = half(t, 2 * cx + cy, 1 - c)
            copy(t, 3 + j, theirs, theirs, sib).wait_recv()
        for t in range(nt):
            own(t).wait()
        for t, j in pairs:
            first_hop(t, j).wait_send()
            passed(t, j).wait_send()

    return start, pass_on, finish


def _gather_chip_shards(tensors, *, name):
    nt = len(tensors)

    def body(*refs):
        for phase in _gather_phases([a.shape for a in tensors], refs[:nt], refs[nt:2 * nt], *refs[2 * nt:]):
            phase()

    return pl.pallas_call(
        body, name=name, in_specs=[_ANY] * nt, out_specs=[_ANY] * nt,
        out_shape=[jax.ShapeDtypeStruct((N_CHIPS,) + a.shape, a.dtype) for a in tensors],
        scratch_shapes=_sem_pairs(GATHER_COPIES * nt),
    )(*tensors)


def _swap_halves(gs, *, name):
    nt = len(gs)

    def body(*refs):
        srcs, outs = refs[:nt], refs[nt:2 * nt]
        send_sems, recv_sems = refs[2 * nt:]
        x, y, c, _ = _place()
        cps = [pltpu.make_async_remote_copy(src_ref=srcs[t].at[:, 1 - c], dst_ref=outs[t], send_sem=send_sems.at[t],
                                            recv_sem=recv_sems.at[t], device_id=(x, y, 1 - c), device_id_type=MESH)
               for t in range(nt)]
        for cp in cps:
            cp.start()
        for cp in cps:
            cp.wait()

    return pl.pallas_call(
        body, name=name, in_specs=[_ANY] * nt, out_specs=[_ANY] * nt,
        out_shape=[jax.ShapeDtypeStruct((N_CHIPS,) + g.shape[2:], g.dtype) for g in gs], scratch_shapes=_sem_pairs(nt),
    )(*gs)


def _scatter_phases(nt, srcs, outs, send_sems, recv_sems):
    x, y, c, chips = _place()

    def copies():
        return [pltpu.make_async_remote_copy(src_ref=srcs[t].at[2 * cx + cy], dst_ref=outs[t].at[j],
                                             send_sem=send_sems.at[3 * t + j], recv_sem=recv_sems.at[3 * t + j],
                                             device_id=(cx, cy, c), device_id_type=MESH)
                for j, (cx, cy) in enumerate(chips) for t in range(nt)]

    def start():
        for cp in copies():
            cp.start()

    def finish():
        for cp in copies():
            cp.wait()

    return start, finish


def _scatter_to_chips(blocks, *, name):
    nt = len(blocks)

    def body(*refs):
        for phase in _scatter_phases(nt, refs[:nt], refs[nt:2 * nt], *refs[2 * nt:]):
            phase()

    return pl.pallas_call(
        body, name=name, in_specs=[_ANY] * nt, out_specs=[_ANY] * nt,
        out_shape=[jax.ShapeDtypeStruct((3,) + a.shape[1:], a.dtype) for a in blocks], scratch_shapes=_sem_pairs(3 * nt),
    )(*blocks)


def _join_halves(bufs, *, name):
    nt = len(bufs)

    def body(*refs):
        srcs, outs = refs[:nt], refs[nt:2 * nt]
        send_sems, recv_sems = refs[2 * nt:]
        x, y, c, _ = _place()
        cps = []
        for t in range(nt):
            rh = bufs[t].shape[1] // 2
            rows = pl.ds(c * rh, rh)
            cps.append(pltpu.make_async_remote_copy(src_ref=srcs[t].at[:, rows, :], dst_ref=outs[t].at[:, rows, :],
                                                    send_sem=send_sems.at[t], recv_sem=recv_sems.at[t],
                                                    device_id=(x, y, 1 - c), device_id_type=MESH))
        for cp in cps:
            cp.start()
        for cp in cps:
            cp.wait()

    return pl.pallas_call(
        body, name=name, in_specs=[_ANY] * nt, out_specs=[_ANY] * nt,
        out_shape=[jax.ShapeDtypeStruct(b.shape, b.dtype) for b in bufs], scratch_shapes=_sem_pairs(nt),
        input_output_aliases={t: t for t in range(nt)},
    )(*bufs)


def _add_sibling(g, recv, cidx, chip_idx, *, name):
    _, _, na, rh, cdim = g.shape
    tb = _pick(rh, (256, 128, 64, 32, 16))

    def body(c_ref, k_ref, g_ref, r_ref, s_ref, o_ref):
        tot = g_ref[0, 0] + r_ref[0]
        s_ref[0] = tot.astype(BF16)

        @pl.when(pl.program_id(2) == k_ref[0])
        def _():
            o_ref[...] = tot

    return pl.pallas_call(
        body, name=name,
        grid_spec=pltpu.PrefetchScalarGridSpec(
            num_scalar_prefetch=2, grid=(na, rh // tb, N_CHIPS),
            in_specs=[pl.BlockSpec((1, 1, 1, tb, cdim), lambda a, i, k, c_ref, k_ref: (k, c_ref[0], a, i, 0)),
                      pl.BlockSpec((1, 1, tb, cdim), lambda a, i, k, c_ref, k_ref: (k, a, i, 0))],
            out_specs=[pl.BlockSpec((1, 1, tb, cdim), lambda a, i, k, c_ref, k_ref: (k, a, i, 0)),
                       pl.BlockSpec((1, tb, cdim), lambda a, i, k, c_ref, k_ref: (a, i, 0))]),
        out_shape=[jax.ShapeDtypeStruct((N_CHIPS, na, rh, cdim), BF16), jax.ShapeDtypeStruct((na, rh, cdim), F32)],
        compiler_params=_cparams((_PAR, _PAR, _ARB)),
    )(cidx, chip_idx, g, recv)


def _add_chips(own, recv, cidx, *, name):
    na, rh, cdim = own.shape
    tb = _pick(rh, (256, 128, 64, 32, 16))
    nblk = rh // tb

    def body(c_ref, a_ref, r0_ref, r1_ref, r2_ref, o_ref):
        o_ref[...] = ((a_ref[...] + r0_ref[0].astype(F32)) + r1_ref[0].astype(F32)) + r2_ref[0].astype(F32)

    slot = lambda j: pl.BlockSpec((1, 1, tb, cdim), lambda a, i, c_ref: (j, a, i, 0))
    return pl.pallas_call(
        body, name=name,
        grid_spec=pltpu.PrefetchScalarGridSpec(
            num_scalar_prefetch=1, grid=(na, nblk),
            in_specs=[pl.BlockSpec((1, tb, cdim), lambda a, i, c_ref: (a, i, 0)), slot(0), slot(1), slot(2)],
            out_specs=pl.BlockSpec((1, tb, cdim), lambda a, i, c_ref: (a, c_ref[0] * nblk + i, 0))),
        out_shape=jax.ShapeDtypeStruct((na, 2 * rh, cdim), F32), compiler_params=_cparams((_PAR, _PAR)),
    )(cidx, own, recv, recv, recv)


def _all_reduce_small(v, *, name):
    r, cdim = v.shape

    def body(v_ref, o_ref, buf, send_sems, recv_sems):
        x, y, c, _ = _place()
        me = 4 * x + 2 * y + c
        buf[me] = v_ref[...]
        cps = []
        for p in range(1, N_DEV):
            to = (1 - x if p & 4 else x, 1 - y if p & 2 else y, 1 - c if p & 1 else c)
            cp = pltpu.make_async_remote_copy(src_ref=v_ref, dst_ref=buf.at[me], send_sem=send_sems.at[p - 1],
                                              recv_sem=recv_sems.at[p - 1], device_id=to, device_id_type=MESH)
            cp.start()
            cps.append(cp)
        for p in range(1, N_DEV):
            frm = (4 * x + 2 * y + c) ^ p
            pltpu.make_async_remote_copy(src_ref=v_ref, dst_ref=buf.at[frm], send_sem=send_sems.at[p - 1],
                                         recv_sem=recv_sems.at[p - 1], device_id=(x, y, c), device_id_type=MESH).wait_recv()
        for cp in cps:
            cp.wait_send()
        tot = buf[0]
        for i in range(1, N_DEV):
            tot = tot + buf[i]
        o_ref[...] = tot

    vm = pl.BlockSpec(memory_space=pltpu.VMEM)
    return pl.pallas_call(
        body, name=name, in_specs=[vm], out_specs=vm, out_shape=jax.ShapeDtypeStruct((r, cdim), F32),
        scratch_shapes=[pltpu.VMEM((N_DEV, r, cdim), F32), pltpu.SemaphoreType.DMA((N_DEV - 1,)), pltpu.SemaphoreType.DMA((N_DEV - 1,))],
    )(v)


_SHARDED = ("even_w_in", "even_w_uq", "even_w_ukv", "even_w_out", "odd_w_in", "odd_w_out")
_COL_SHARDED = ("even_w_in", "even_w_uq", "even_w_ukv", "odd_w_in")
PACK_COLS = 1024


def _unshard(name, stacked):
    n, a, b, cc = stacked.shape
    if name in _COL_SHARDED:
        return stacked.transpose(1, 2, 0, 3).reshape(a, b, n * cc)
    return stacked.transpose(1, 0, 2, 3).reshape(a, n * b, cc)


def _chip_halves(name, full):
    a, b, cc = full.shape
    if name in _COL_SHARDED:
        return full.reshape(a, 2, b // 2, N_CHIPS, cc // N_CHIPS).transpose(3, 1, 0, 2, 4)
    return full.reshape(a, N_CHIPS, 2, b // (2 * N_CHIPS), cc).transpose(1, 2, 0, 3, 4)


def _rope_tables(s):
    pos = jnp.arange(s, dtype=F32)

    def ang(d):
        inv = ROPE_THETA ** (-jnp.arange(0, d, 2, dtype=F32) / d)
        a = pos[:, None] * inv[None, :]
        return jnp.cos(a), jnp.sin(a)

    c16, s16 = ang(MLA_ROPE)
    one, zero = jnp.ones((s, KPE_LANE), F32), jnp.zeros((s, KPE_LANE), F32)
    cos_m = jnp.concatenate([one, c16, c16, one[:, :32]], axis=1)
    sin_m = jnp.concatenate([zero, -s16, s16, zero[:, :32]], axis=1)
    c32, s32 = ang(SWA_DIM)
    cos_s = jnp.concatenate([c32, c32, c32, c32], axis=1)
    sin_s = jnp.concatenate([-s32, s32, -s32, s32], axis=1)
    return cos_m, sin_m, cos_s, sin_s


def _even_weights(w_in, w_uq, w_ukv):
    zeros = lambda n: jnp.zeros((D_MODEL, n), w_in.dtype)
    wcq, wckv, wkpe = w_in[:, 0:256], w_in[:, 256:384], w_in[:, 384:416]
    wfq, wfk, wfv = w_in[:, 416:928], w_in[:, 928:1440], w_in[:, 1440:1952]
    wfl, wg = w_in[:, 1952:1960], w_in[:, 1960:2984]
    misc = jnp.concatenate([zeros(KPE_LANE), wkpe, wfl, zeros(LANES - FL_LANE - FOX_HEADS)], axis=1)
    w_in_p = jnp.concatenate([wg, wfq, wfk, wfv, wcq, wckv, misc], axis=1)
    uq = w_uq.reshape(MLA_Q_RANK, MLA_HEADS, MLA_NOPE + MLA_ROPE)
    uq_p = jnp.pad(uq, ((0, 0), (0, 0), (0, HEAD_PAD - MLA_NOPE - MLA_ROPE))).reshape(MLA_Q_RANK, MLA_HEADS * HEAD_PAD)
    ukv = w_ukv.reshape(MLA_KV_RANK, MLA_HEADS, MLA_NOPE + MLA_V)
    uk_p = jnp.pad(ukv[..., :MLA_NOPE], ((0, 0), (0, 0), (0, HEAD_PAD - MLA_NOPE))).reshape(MLA_KV_RANK, MLA_HEADS * HEAD_PAD)
    uv = ukv[..., MLA_NOPE:].reshape(MLA_KV_RANK, MLA_HEADS * MLA_V)
    ukv_p = jnp.concatenate([uk_p, uv], axis=1)
    return w_in_p, w_in_p.T, uq_p, uq_p.T, ukv_p, ukv_p.T


def _even_weight_grads(dw_in_p, duq_p, dukv_p):
    g = dw_in_p
    gate, fq, fk, fv = g[:, E_GATE:E_FQ], g[:, E_FQ:E_FK], g[:, E_FK:E_FV], g[:, E_FV:E_SMALL]
    cq, ckv, misc = g[:, E_SMALL:E_SMALL + 256], g[:, E_SMALL + 256:E_SMALL + 384], g[:, E_SMALL + 384:]
    dw_in = jnp.concatenate([cq, ckv, misc[:, KPE_LANE:KPE_LANE + MLA_ROPE], fq, fk, fv, misc[:, FL_LANE:FL_LANE + FOX_HEADS], gate], axis=1)
    duq = duq_p.reshape(MLA_Q_RANK, MLA_HEADS, HEAD_PAD)[..., :MLA_NOPE + MLA_ROPE].reshape(MLA_Q_RANK, -1)
    nq = MLA_HEADS * HEAD_PAD
    dk = dukv_p[:, :nq].reshape(MLA_KV_RANK, MLA_HEADS, HEAD_PAD)[..., :MLA_NOPE]
    dvv = dukv_p[:, nq:].reshape(MLA_KV_RANK, MLA_HEADS, MLA_V)
    dukv = jnp.concatenate([dk, dvv], axis=-1).reshape(MLA_KV_RANK, -1)
    return dw_in, duq, dukv


def _interleave(w, forward):
    a, b = (SWA_KV_HEADS, SWA_HEADS // SWA_KV_HEADS) if forward else (SWA_HEADS // SWA_KV_HEADS, SWA_KV_HEADS)
    return w.reshape(w.shape[0], a, b, -1).transpose(0, 2, 1, 3).reshape(w.shape[0], -1)


def _odd_weights(w_in, w_out):
    q, k, v, gate = w_in[:, 0:1024], w_in[:, 1024:1152], w_in[:, 1152:1280], w_in[:, 1280:2304]
    w_p = jnp.concatenate([_interleave(gate, True), _interleave(q, True), k, v], axis=1)
    w_out_p = _interleave(w_out.T, True).T
    return w_p, w_p.T, w_out_p, w_out_p.T


def _odd_weight_grads(g, dw_out_p):
    dw_in = jnp.concatenate([_interleave(g[:, O_Q:O_K], False), g[:, O_K:O_V], g[:, O_V:], _interleave(g[:, O_GATE:O_Q], False)], axis=1)
    return dw_in, _interleave(dw_out_p.T, False).T


def kernel(x, even_w_in, even_q_norm, even_w_uq, even_kv_norm, even_w_ukv, even_b_f, even_w_out, even_ln_g, even_ln_b, odd_w_in, odd_sinks, odd_w_out, odd_ln_g, odd_ln_b, loss_target, m_even_w_in, m_even_q_norm, m_even_w_uq, m_even_kv_norm, m_even_w_ukv, m_even_b_f, m_even_w_out, m_even_ln_g, m_even_ln_b, m_odd_w_in, m_odd_sinks, m_odd_w_out, m_odd_ln_g, m_odd_ln_b, v_even_w_in, v_even_q_norm, v_even_w_uq, v_even_kv_norm, v_even_w_ukv, v_even_b_f, v_even_w_out, v_even_ln_g, v_even_ln_b, v_odd_w_in, v_odd_sinks, v_odd_w_out, v_odd_ln_g, v_odd_ln_b):
    weights = dict(even_w_in=even_w_in, even_q_norm=even_q_norm, even_w_uq=even_w_uq, even_kv_norm=even_kv_norm,
                   even_w_ukv=even_w_ukv, even_b_f=even_b_f, even_w_out=even_w_out, even_ln_g=even_ln_g, even_ln_b=even_ln_b,
                   odd_w_in=odd_w_in, odd_sinks=odd_sinks, odd_w_out=odd_w_out, odd_ln_g=odd_ln_g, odd_ln_b=odd_ln_b)
    mom_m = dict(even_w_in=m_even_w_in, even_q_norm=m_even_q_norm, even_w_uq=m_even_w_uq, even_kv_norm=m_even_kv_norm,
                 even_w_ukv=m_even_w_ukv, even_b_f=m_even_b_f, even_w_out=m_even_w_out, even_ln_g=m_even_ln_g, even_ln_b=m_even_ln_b,
                 odd_w_in=m_odd_w_in, odd_sinks=m_odd_sinks, odd_w_out=m_odd_w_out, odd_ln_g=m_odd_ln_g, odd_ln_b=m_odd_ln_b)
    mom_v = dict(even_w_in=v_even_w_in, even_q_norm=v_even_q_norm, even_w_uq=v_even_w_uq, even_kv_norm=v_even_kv_norm,
                 even_w_ukv=v_even_w_ukv, even_b_f=v_even_b_f, even_w_out=v_even_w_out, even_ln_g=v_even_ln_g, even_ln_b=v_even_ln_b,
                 odd_w_in=v_odd_w_in, odd_sinks=v_odd_sinks, odd_w_out=v_odd_w_out, odd_ln_g=v_odd_ln_g, odd_ln_b=v_odd_ln_b)
    names = list(weights)
    xl = x[0]
    tgt = loss_target[0]
    s = xl.shape[0]
    ax, ay, ac = lax.axis_index("x"), lax.axis_index("y"), lax.axis_index("c")
    chip = 2 * ax + ay
    c_idx = ac.astype(jnp.int32).reshape(1)
    chip_idx = chip.astype(jnp.int32).reshape(1)

    ln_odd = jnp.pad(jnp.concatenate([odd_ln_g, odd_ln_b]), ((0, 12), (0, 0)))[None]
    wb = {n: weights[n].astype(BF16) for n in _SHARDED}
    even_mats = [n for n in _SHARDED if n.startswith("even")]
    odd_mats = [n for n in _SHARDED if n.startswith("odd")]
    gathered = _gather_chip_shards([wb[n][0:1] for n in even_mats] + [ln_odd], name="gather_first")
    full = {n: [_unshard(n, gathered[i])[0], None] for i, n in enumerate(even_mats)}
    later = [wb[n][1:2] for n in even_mats] + [wb[n] for n in odd_mats]
    ln_all = gathered[-1][:, 0]
    odd_g_full = ln_all[:, 0:2].transpose(1, 0, 2).reshape(2, D_MODEL)
    odd_b_full = ln_all[:, 2:4].transpose(1, 0, 2).reshape(2, D_MODEL)

    cos_m, sin_m, cos_s, sin_s = _rope_tables(s)
    bf_tiles = jnp.pad(even_b_f, ((0, 0), (FL_LANE, LANES - FL_LANE - FOX_HEADS)))
    sink_tiles = jnp.broadcast_to(_interleave(odd_sinks, True)[:, :, None, None], (2, SWA_HEADS, 1, LANES))
    mla_scale = (MLA_NOPE + MLA_ROPE) ** -0.5
    fox_scale = FOX_DIM ** -0.5
    mla_pairs, fox_pairs = MLA_HEADS // 2, FOX_HEADS // 2

    saved = []
    x_f, x_b = xl, xl.astype(BF16)
    for layer in range(DEPTH):
        j = layer // 2
        ln = f"L{layer}"
        if layer % 2 == 0:
            w_in_p, w_in_t, uq_p, uq_t, ukv_p, ukv_t = _even_weights(full["even_w_in"][j], full["even_w_uq"][j], full["even_w_ukv"][j])
            w_out, w_out_t = full["even_w_out"][j], full["even_w_out"][j].T
            qg, kg, bft = even_q_norm[j][None], even_kv_norm[j][None], bf_tiles[j][None]
            h, hb = _mm(x_b, w_in_p, out_dtype=F32, name=ln + "_in", also_bf16=True)
            q, k, v, qn, kvn, logf = _even_mid_fwd(h, qg, kg, bft, uq_p, ukv_p, cos_m, sin_m, name=ln + "_mid")
            cum = _cumsum(logf, reverse=False, name=ln + "_cum")[:, FL_LANE:FL_LANE + FOX_HEADS].T
            crow = cum[:, None, :]
            if layer == 0:
                o_mla, lse_mla, rest = _attn_fwd((q, 0), (k, 0), (v, 0), None, dk=HEAD_PAD, npairs=mla_pairs, scale=mla_scale,
                                                 name=ln + "_mla", gather=later)
                for i, n in enumerate(even_mats):
                    full[n][1] = _unshard(n, rest[i])[0]
                for i, n in enumerate(odd_mats):
                    full[n] = _unshard(n, rest[len(even_mats) + i])
            else:
                o_mla, lse_mla = _attn_fwd((q, 0), (k, 0), (v, 0), None, dk=HEAD_PAD, npairs=mla_pairs, scale=mla_scale, name=ln + "_mla")
            o_fox, lse_fox = _attn_fwd((hb, E_FQ), (hb, E_FK), (hb, E_FV), crow, dk=FOX_DIM, npairs=fox_pairs,
                                       scale=fox_scale, name=ln + "_fox")
            o_parts = [o_mla, o_fox]
            g_ln, b_ln = even_ln_g[j][None], even_ln_b[j][None]
            x_n, x_nb, z, xh, rstd = _out_fwd(o_parts, h, x_f, w_out, g_ln, b_ln, name=ln + "_out")
            saved.append(dict(h=h, hb=hb, x_b=x_b, qn=qn, kvn=kvn, q=q, k=k, v=v, crow=crow,
                              o_mla=o_mla, o_fox=o_fox, lse_mla=lse_mla, lse_fox=lse_fox, o_parts=o_parts, z=z, xh=xh, rstd=rstd,
                              w_in_t=w_in_t, uq_t=uq_t, ukv_t=ukv_t, w_out_t=w_out_t, qg=qg, kg=kg, bft=bft, g_ln=g_ln))
        else:
            w_in_p, w_in_t, w_out, w_out_t = _odd_weights(full["odd_w_in"][j], full["odd_w_out"][j])
            h = _mm(x_b, w_in_p, out_dtype=F32, name=ln + "_in")
            q, k, v = _odd_rope(h, cos_s, sin_s, name=ln + "_rope")
            kp = jnp.pad(k, ((WINDOW, 0), (0, 0)))
            vp = jnp.pad(v, ((WINDOW, 0), (0, 0)))
            o_t, lse = _swa_fwd(q, kp, vp, sink_tiles[j], name=ln + "_swa")
            g_ln, b_ln = odd_g_full[j][None], odd_b_full[j][None]
            x_n, x_nb, z, xh, rstd = _out_fwd([o_t], h, x_f, w_out, g_ln, b_ln, name=ln + "_out", o_t=True)
            saved.append(dict(h=h, x_b=x_b, q=q, kp=kp, vp=vp, lse=lse, o_t=o_t, o_parts=[o_t], z=z, xh=xh, rstd=rstd,
                              w_in_t=w_in_t, w_out_t=w_out_t, g_ln=g_ln))
        x_f, x_b = x_n, x_nb

    dxn, loss_tile = _loss_grad(x_f, tgt, name="loss")

    def pair_sums(group, tag):
        gp = [_chip_halves(n, g) for n, g in group.items()]
        from_sib = _swap_halves(gp, name="grad_swap_" + tag)
        return [_add_sibling(g, r, c_idx, chip_idx, name=f"grad_add_sibling_{n}_{tag}") for n, g, r in zip(group, gp, from_sib)]

    def chip_sums(group, pair, from_chips, tag):
        mine = [_add_chips(p[1], r, c_idx, name=f"grad_add_chips_{n}_{tag}") for n, p, r in zip(group, pair, from_chips)]
        return dict(zip(group, _join_halves(mine, name="grad_join_" + tag)))

    grads = {n: [None, None] for n in names}
    for layer in reversed(range(DEPTH)):
        j = layer // 2
        ln = f"L{layer}"
        sv = saved[layer]
        dr, dy, do, dgate, dg_ln, db_ln, *do_t = _out_bwd(dxn, sv["xh"], sv["rstd"], sv["g_ln"], sv["w_out_t"], sv["o_parts"], sv["h"],
                                                          name=ln + "_outb", o_t=layer % 2 == 1)
        dw_out = _mm(sv["z"].T, dy, out_dtype=F32, name=ln + "_dwout")
        if layer % 2 == 0:
            hb = sv["hb"]
            if layer == 0:
                early = {n: grads[n][1][None] for n in even_mats}
                early.update({n: jnp.stack(grads[n]) for n in odd_mats})
                early_pair = pair_sums(early, "early")
                dq, dk, dv, early_chips = _attn_bwd((sv["q"], 0), (sv["k"], 0), (sv["v"], 0), (sv["o_mla"], 0), (do, 0), sv["lse_mla"], None,
                                                    dk=HEAD_PAD, npairs=mla_pairs, scale=mla_scale, dq_dtype=F32, name=ln + "_mlab",
                                                    scatter=[p[0] for p in early_pair])
            else:
                dq, dk, dv = _attn_bwd((sv["q"], 0), (sv["k"], 0), (sv["v"], 0), (sv["o_mla"], 0), (do, 0), sv["lse_mla"], None,
                                       dk=HEAD_PAD, npairs=mla_pairs, scale=mla_scale, dq_dtype=F32, name=ln + "_mlab")
            dfq, dfk, dfv, dcrow, dccol = _attn_bwd((hb, E_FQ), (hb, E_FK), (hb, E_FV), (sv["o_fox"], 0), (do, MLA_HEADS * MLA_V),
                                                    sv["lse_fox"], sv["crow"], dk=FOX_DIM, npairs=fox_pairs,
                                                    scale=fox_scale, dq_dtype=BF16, name=ln + "_foxb")
            dcum = jnp.pad((dcrow[:, 0, :] + dccol[:, :, 0]).T, ((0, 0), (FL_LANE, LANES - FL_LANE - FOX_HEADS)))
            dlogf = _cumsum(dcum, reverse=True, name=ln + "_cumb")
            dh_small, dq_pre, dqg, dkg, dbf = _even_mid_bwd(
                sv["h"], dq, dk, dv, dlogf, sv["qg"], sv["kg"], sv["bft"], sv["uq_t"], sv["ukv_t"], cos_m, sin_m, name=ln + "_midb")
            duq_p = _mm(sv["qn"].T, dq_pre, out_dtype=F32, name=ln + "_dwuq")
            dkv_cat = jnp.concatenate([dk.astype(BF16), dv], axis=1)
            dukv_p = _mm(sv["kvn"].T, dkv_cat, out_dtype=F32, name=ln + "_dwukv")
            dh = jnp.concatenate([dgate, dfq, dfk, dfv, dh_small], axis=1)
            dw_in_p = _mm(sv["x_b"].T, dh, out_dtype=F32, name=ln + "_dwin")
            dw_in, duq, dukv = _even_weight_grads(dw_in_p, duq_p, dukv_p)
            for n, val in (("even_w_in", dw_in), ("even_w_uq", duq), ("even_w_ukv", dukv), ("even_w_out", dw_out),
                           ("even_q_norm", dqg[0]), ("even_kv_norm", dkg[0]), ("even_b_f", dbf[0, FL_LANE:FL_LANE + FOX_HEADS]),
                           ("even_ln_g", dg_ln[0]), ("even_ln_b", db_ln[0])):
                grads[n][j] = val
        else:
            dq_t, dkp, dvp, dsink = _swa_bwd(sv["q"], sv["kp"], sv["vp"], sv["kp"].T, sink_tiles[j], sv["o_t"], do, do_t[0], sv["lse"],
                                             name=ln + "_swab")
            dq_r, dk_r = _odd_rope_bwd(dq_t, dkp[WINDOW:], cos_s, sin_s, name=ln + "_ropeb")
            dh = jnp.concatenate([dgate, dq_r, dk_r, dvp[WINDOW:].astype(BF16)], axis=1)
            dw_in_p = _mm(sv["x_b"].T, dh, out_dtype=F32, name=ln + "_dwin")
            dw_in, dw_out = _odd_weight_grads(dw_in_p, dw_out)
            for n, val in (("odd_w_in", dw_in), ("odd_w_out", dw_out), ("odd_sinks", _interleave(dsink[None, :, 0, 0], False)[0]),
                           ("odd_ln_g", dg_ln[0]), ("odd_ln_b", db_ln[0])):
                grads[n][j] = val
        dxn = _mm(dh, sv["w_in_t"], out_dtype=F32, name=ln + "_dx", res=dr, res_scale=ALPHA)
    grad_x = dxn[None]
    late = {n: grads[n][0][None] for n in even_mats}
    late_pair = pair_sums(late, "late")
    late_sum = chip_sums(late, late_pair, _scatter_to_chips([p[0] for p in late_pair], name="grad_scatter_late"), "late")
    early_sum = chip_sums(early, early_pair, early_chips, "early")
    gshard = {n: jnp.concatenate([late_sum[n], early_sum[n]]) for n in even_mats}
    gshard.update({n: early_sum[n] for n in odd_mats})

    small = [n for n in names if n not in _SHARDED]
    grads = {n: jnp.stack(grads[n]) for n in small}
    sv_flat = jnp.concatenate([grads[n].reshape(-1) for n in small] + [loss_tile[0, :1]])
    sv_real = sv_flat.shape[0]
    sv_rows = -(-sv_real // (PACK_COLS * 8)) * 8
    sv_sum = _all_reduce_small(jnp.pad(sv_flat, (0, sv_rows * PACK_COLS - sv_real)).reshape(sv_rows, PACK_COLS), name="small_all_reduce").reshape(-1)
    off = 0
    for n in small:
        size = int(np.prod(grads[n].shape))
        gfull = sv_sum[off:off + size].reshape(grads[n].shape)
        off += size
        if n in ("odd_ln_g", "odd_ln_b"):
            gfull = lax.dynamic_slice_in_dim(gfull, chip * (D_MODEL // N_CHIPS), D_MODEL // N_CHIPS, axis=1)
        gshard[n] = gfull
    loss = sv_sum[off]

    deltas, new_m, new_v = {}, {}, {}
    for n in names:
        deltas[n], new_m[n], new_v[n] = _adamw(weights[n], gshard[n], mom_m[n], mom_v[n], name="adamw_" + n)
    return (loss, grad_x, *[gshard[n] for n in names], *[deltas[n] for n in names],
            *[new_m[n] for n in names], *[new_v[n] for n in names])
```

```python
import jax
import jax.numpy as jnp
import numpy as np
from jax import lax
from jax.experimental import pallas as pl
from jax.experimental.pallas import tpu as pltpu

F32 = jnp.float32
BF16 = jnp.bfloat16
MESH = pl.DeviceIdType.MESH

D_MODEL = 1024
DEPTH = 4
ROPE_THETA = 10000.0
MLA_HEADS, MLA_NOPE, MLA_ROPE, MLA_V = 8, 64, 32, 64
MLA_Q_RANK, MLA_KV_RANK = 256, 128
FOX_HEADS, FOX_DIM = 8, 64
SWA_HEADS, SWA_KV_HEADS, SWA_DIM, WINDOW = 16, 2, 64, 128
RMS_EPS = 1e-6
LN_EPS = 1e-5
ALPHA = (2 * DEPTH) ** 0.25
EVEN_IN = 2984
ODD_IN = 2304
ADAM_LR, ADAM_B1, ADAM_B2, ADAM_EPS, ADAM_WD, ADAM_STEP = 0.001, 0.9, 0.999, 1e-08, 0.01, 10

LANES = 128
HALF = LANES // 2
HEAD_PAD = 128
N_CHIPS = 4
N_DEV = 8
E_GATE, E_FQ, E_FK, E_FV, E_SMALL = 0, 1024, 1536, 2048, 2560
E_PAD_IN = 3072
KPE_LANE = 64
FL_LANE = 96
O_GATE, O_Q, O_K, O_V = 0, 1024, 2048, 2176

_ARB = "arbitrary"
_PAR = "parallel"


def _cparams(sem):
    return pltpu.CompilerParams(dimension_semantics=sem)


def _pick(n, cands):
    for c in cands:
        if n % c == 0:
            return c
    return n


RESIDENT_BYTES = 8 << 20


def _mm_resident(a, b, *, out_dtype, name, res, res_scale, also_bf16):
    m, k = a.shape
    _, n = b.shape
    tm = _pick(m, (256, 128))
    cn = _pick(n, (512, 384, 256, 128))
    dtypes = [out_dtype, BF16] if also_bf16 else [out_dtype]

    def body(*refs):
        a_ref, b_ref = refs[:2]
        r_ref = refs[2] if res is not None else None
        outs = refs[3 if res is not None else 2:]
        av = a_ref[...].astype(BF16)
        for c0 in range(0, n, cn):
            r = jnp.dot(av, b_ref[:, c0:c0 + cn].astype(BF16), preferred_element_type=F32)
            if res is not None:
                r = r + res_scale * r_ref[:, c0:c0 + cn]
            for o_ref in outs:
                o_ref[:, c0:c0 + cn] = r.astype(o_ref.dtype)

    rows = lambda w: pl.BlockSpec((tm, w), lambda i: (i, 0))
    in_specs = [rows(k), pl.BlockSpec((k, n), lambda i: (0, 0))] + ([rows(n)] if res is not None else [])
    out = pl.pallas_call(
        body, name=name, grid=(m // tm,), in_specs=in_specs, out_specs=[rows(n)] * len(dtypes),
        out_shape=[jax.ShapeDtypeStruct((m, n), d) for d in dtypes], compiler_params=_cparams((_PAR,)),
    )(*([a, b] + ([res] if res is not None else [])))
    return out if also_bf16 else out[0]


def _mm_left_resident(a, b, *, out_dtype, name):
    m, k = a.shape
    _, n = b.shape
    tn = _pick(n, (256, 128))

    def body(a_ref, b_ref, o_ref):
        o_ref[...] = jnp.dot(a_ref[...].astype(BF16), b_ref[...].astype(BF16), preferred_element_type=F32).astype(o_ref.dtype)

    return pl.pallas_call(
        body, name=name, grid=(n // tn,),
        in_specs=[pl.BlockSpec((m, k), lambda j: (0, 0)), pl.BlockSpec((k, tn), lambda j: (0, j))],
        out_specs=pl.BlockSpec((m, tn), lambda j: (0, j)), out_shape=jax.ShapeDtypeStruct((m, n), out_dtype),
        compiler_params=_cparams((_PAR,)),
    )(a, b)


def _mm(a, b, *, out_dtype, name, res=None, res_scale=1.0, also_bf16=False):
    m, k = a.shape
    _, n = b.shape
    if b.size * b.dtype.itemsize <= RESIDENT_BYTES:
        return _mm_resident(a, b, out_dtype=out_dtype, name=name, res=res, res_scale=res_scale, also_bf16=also_bf16)
    if a.size * a.dtype.itemsize <= RESIDENT_BYTES and res is None and not also_bf16:
        return _mm_left_resident(a, b, out_dtype=out_dtype, name=name)
    tm = _pick(m, (512, 256, 128))
    tn = _pick(n, (1024, 768, 512, 384, 256, 128))
    tk = _pick(k, (1024, 768, 512, 256, 128))
    nk = k // tk

    def body(*refs):
        refs = list(refs)
        a_ref, b_ref = refs[:2]
        r_ref = refs[2] if res is not None else None
        acc_ref = refs[-1]
        outs = refs[3 if res is not None else 2:-1]
        kk = pl.program_id(2)

        @pl.when(kk == 0)
        def _():
            acc_ref[...] = jnp.zeros_like(acc_ref)

        acc_ref[...] += jnp.dot(a_ref[...].astype(BF16), b_ref[...].astype(BF16), preferred_element_type=F32)

        @pl.when(kk == nk - 1)
        def _():
            r = acc_ref[...]
            if res is not None:
                r = r + res_scale * r_ref[...]
            for o_ref in outs:
                o_ref[...] = r.astype(o_ref.dtype)

    in_specs = [pl.BlockSpec((tm, tk), lambda i, j, kk: (i, kk)), pl.BlockSpec((tk, tn), lambda i, j, kk: (kk, j))]
    args = [a, b]
    if res is not None:
        in_specs.append(pl.BlockSpec((tm, tn), lambda i, j, kk: (i, j)))
        args.append(res)
    ospec = pl.BlockSpec((tm, tn), lambda i, j, kk: (i, j))
    dtypes = [out_dtype, BF16] if also_bf16 else [out_dtype]
    out = pl.pallas_call(
        body, name=name, grid=(m // tm, n // tn, nk), in_specs=in_specs,
        out_specs=[ospec] * len(dtypes), out_shape=[jax.ShapeDtypeStruct((m, n), d) for d in dtypes],
        scratch_shapes=[pltpu.VMEM((tm, tn), F32)],
        compiler_params=_cparams((_PAR, _PAR, _ARB)),
    )(*args)
    return out if also_bf16 else out[0]


def _rope_tile(t, cos, sin, half):
    w = t.shape[-1]
    lane = lax.broadcasted_iota(jnp.int32, t.shape, 1)
    first = (lane % (2 * half)) < half
    sw = jnp.where(first, pltpu.roll(t, w - half, 1), pltpu.roll(t, half, 1))
    return t * cos + sw * sin


def _sigmoid(x):
    return 1.0 / (1.0 + jnp.exp(-x))


def _lane_mask(shape, lo, hi):
    lane = lax.broadcasted_iota(jnp.int32, shape, 1)
    return (lane >= lo) & (lane < hi)


def _rms(x, g):
    r = lax.rsqrt(jnp.mean(x * x, axis=-1, keepdims=True) + RMS_EPS)
    u = x * r
    return u, r, u * g


def _rms_bwd(dy, u, r, g):
    dyg = dy * g
    dx = r * (dyg - u * jnp.mean(dyg * u, axis=-1, keepdims=True))
    return dx, jnp.sum(dy * u, axis=0, keepdims=True)


def _even_mid_fwd(h, qg, kg, bf_tile, w_uq, w_ukv, cos, sin, *, name):
    s = h.shape[0]
    tb = _pick(s, (512, 256, 128))
    nq = MLA_HEADS * HEAD_PAD

    def body(h_ref, qg_ref, kg_ref, bf_ref, wuq_ref, wukv_ref, cos_ref, sin_ref,
             q_ref, k_ref, v_ref, qn_ref, kvn_ref, lf_ref):
        hb = h_ref[...]
        cq, ckv, misc = hb[:, :MLA_Q_RANK], hb[:, MLA_Q_RANK:MLA_Q_RANK + MLA_KV_RANK], hb[:, MLA_Q_RANK + MLA_KV_RANK:]
        cs, sn = cos_ref[...], sin_ref[...]
        _, _, qn = _rms(cq, qg_ref[...])
        qn = qn.astype(BF16)
        qn_ref[...] = qn
        q = jnp.dot(qn, wuq_ref[...], preferred_element_type=F32)
        _, _, kvn = _rms(ckv, kg_ref[...])
        kvn = kvn.astype(BF16)
        kvn_ref[...] = kvn
        kv = jnp.dot(kvn, wukv_ref[...], preferred_element_type=F32)
        kpe = jnp.where(_lane_mask(misc.shape, KPE_LANE, KPE_LANE + MLA_ROPE), _rope_tile(misc, cs, sn, MLA_ROPE // 2), 0.0)
        for hd in range(MLA_HEADS):
            sl = slice(hd * HEAD_PAD, (hd + 1) * HEAD_PAD)
            q_ref[:, sl] = _rope_tile(q[:, sl], cs, sn, MLA_ROPE // 2).astype(BF16)
            k_ref[:, sl] = (kv[:, sl] + kpe).astype(BF16)
        v_ref[...] = kv[:, nq:].astype(BF16)
        xf = misc + bf_ref[...]
        logf = jnp.minimum(xf, 0.0) - jnp.log(1.0 + jnp.exp(-jnp.abs(xf)))
        lf_ref[...] = jnp.where(_lane_mask(misc.shape, FL_LANE, FL_LANE + FOX_HEADS), logf, 0.0)

    full = lambda a: pl.BlockSpec(a.shape, lambda i: (0,) * a.ndim)
    rows = lambda w, c=0: pl.BlockSpec((tb, w), lambda i: (i, c))
    return pl.pallas_call(
        body, name=name, grid=(s // tb,),
        in_specs=[rows(512, E_SMALL // 512), full(qg), full(kg), full(bf_tile), full(w_uq), full(w_ukv), rows(LANES), rows(LANES)],
        out_specs=[rows(nq), rows(nq), rows(MLA_HEADS * MLA_V), rows(MLA_Q_RANK), rows(MLA_KV_RANK), rows(LANES)],
        out_shape=[jax.ShapeDtypeStruct((s, nq), BF16), jax.ShapeDtypeStruct((s, nq), BF16),
                   jax.ShapeDtypeStruct((s, MLA_HEADS * MLA_V), BF16), jax.ShapeDtypeStruct((s, MLA_Q_RANK), BF16),
                   jax.ShapeDtypeStruct((s, MLA_KV_RANK), BF16), jax.ShapeDtypeStruct((s, LANES), F32)],
        compiler_params=_cparams((_PAR,)),
    )(h, qg, kg, bf_tile, w_uq, w_ukv, cos, sin)


def _even_mid_bwd(h, dq, dk, dv, dlogf, qg, kg, bf_tile, w_uq_t, w_ukv_t, cos, sin, *, name):
    s = h.shape[0]
    tb = _pick(s, (256, 128))
    nq = MLA_HEADS * HEAD_PAD

    def body(h_ref, dq_ref, dk_ref, dv_ref, dlf_ref, qg_ref, kg_ref, bf_ref, wuqt_ref, wukvt_ref, cos_ref, sin_ref,
             dh_ref, dqp_ref, dqg_ref, dkg_ref, dbf_ref):
        @pl.when(pl.program_id(0) == 0)
        def _():
            dqg_ref[...] = jnp.zeros_like(dqg_ref)
            dkg_ref[...] = jnp.zeros_like(dkg_ref)
            dbf_ref[...] = jnp.zeros_like(dbf_ref)

        hb = h_ref[...]
        cq, ckv, misc = hb[:, :MLA_Q_RANK], hb[:, MLA_Q_RANK:MLA_Q_RANK + MLA_KV_RANK], hb[:, MLA_Q_RANK + MLA_KV_RANK:]
        cs, sn = cos_ref[...], -sin_ref[...]
        dkpe = jnp.zeros(misc.shape, F32)
        for hd in range(MLA_HEADS):
            sl = slice(hd * HEAD_PAD, (hd + 1) * HEAD_PAD)
            dqp_ref[:, sl] = _rope_tile(dq_ref[:, sl], cs, sn, MLA_ROPE // 2).astype(BF16)
            dkpe = dkpe + dk_ref[:, sl]
        dqn = jnp.dot(dqp_ref[...], wuqt_ref[...], preferred_element_type=F32)
        uq, rq, _ = _rms(cq, qg_ref[...])
        dcq, dqg = _rms_bwd(dqn, uq, rq, qg_ref[...])
        dqg_ref[...] += dqg
        dkv = jnp.concatenate([dk_ref[...].astype(BF16), dv_ref[...]], axis=1)
        dkvn = jnp.dot(dkv, wukvt_ref[...], preferred_element_type=F32)
        uk, rk, _ = _rms(ckv, kg_ref[...])
        dckv, dkg = _rms_bwd(dkvn, uk, rk, kg_ref[...])
        dkg_ref[...] += dkg
        dmisc = jnp.where(_lane_mask(misc.shape, KPE_LANE, KPE_LANE + MLA_ROPE), _rope_tile(dkpe, cs, sn, MLA_ROPE // 2), 0.0)
        dfl = jnp.where(_lane_mask(misc.shape, FL_LANE, FL_LANE + FOX_HEADS), dlf_ref[...] * _sigmoid(-(misc + bf_ref[...])), 0.0)
        dbf_ref[...] += jnp.sum(dfl, axis=0, keepdims=True)
        dh_ref[:, :MLA_Q_RANK] = dcq.astype(BF16)
        dh_ref[:, MLA_Q_RANK:MLA_Q_RANK + MLA_KV_RANK] = dckv.astype(BF16)
        dh_ref[:, MLA_Q_RANK + MLA_KV_RANK:] = (dmisc + dfl).astype(BF16)

    full = lambda a: pl.BlockSpec(a.shape, lambda i: (0,) * a.ndim)
    rows = lambda w, c=0: pl.BlockSpec((tb, w), lambda i: (i, c))
    return pl.pallas_call(
        body, name=name, grid=(s // tb,),
        in_specs=[rows(512, E_SMALL // 512), rows(nq), rows(nq), rows(MLA_HEADS * MLA_V), rows(LANES), full(qg), full(kg),
                  full(bf_tile), full(w_uq_t), full(w_ukv_t), rows(LANES), rows(LANES)],
        out_specs=[rows(512), rows(nq), full(qg), full(kg), full(bf_tile)],
        out_shape=[jax.ShapeDtypeStruct((s, 512), BF16), jax.ShapeDtypeStruct((s, nq), BF16),
                   jax.ShapeDtypeStruct(qg.shape, F32), jax.ShapeDtypeStruct(kg.shape, F32),
                   jax.ShapeDtypeStruct(bf_tile.shape, F32)],
        compiler_params=_cparams((_ARB,)),
    )(h, dq, dk, dv, dlogf, qg, kg, bf_tile, w_uq_t, w_ukv_t, cos, sin)


def _cumsum(x, *, reverse, name):
    s = x.shape[0]
    tb = _pick(s, (512, 256, 128))
    nb = s // tb

    def body(x_ref, o_ref, carry_ref):
        @pl.when(pl.program_id(0) == 0)
        def _():
            carry_ref[...] = jnp.zeros_like(carry_ref)

        xv = x_ref[...]
        r = lax.broadcasted_iota(jnp.int32, (tb, tb), 0)
        c = lax.broadcasted_iota(jnp.int32, (tb, tb), 1)
        tri = jnp.where((c >= r) if reverse else (c <= r), 1.0, 0.0).astype(BF16)
        hi = xv.astype(BF16)
        r1 = xv - hi.astype(F32)
        mid = r1.astype(BF16)
        lo = (r1 - mid.astype(F32)).astype(BF16)
        cs = (jnp.dot(tri, hi, preferred_element_type=F32) + jnp.dot(tri, mid, preferred_element_type=F32)
              + jnp.dot(tri, lo, preferred_element_type=F32)) + carry_ref[...]
        o_ref[...] = cs
        carry_ref[...] = cs[0:1, :] if reverse else cs[tb - 1:tb, :]

    imap = (lambda i: (nb - 1 - i, 0)) if reverse else (lambda i: (i, 0))
    return pl.pallas_call(
        body, name=name, grid=(nb,), in_specs=[pl.BlockSpec((tb, LANES), imap)],
        out_specs=pl.BlockSpec((tb, LANES), imap), out_shape=jax.ShapeDtypeStruct(x.shape, F32),
        scratch_shapes=[pltpu.VMEM((1, LANES), F32)], compiler_params=_cparams((_ARB,)),
    )(x)


_NT = (((1,), (1,)), ((), ()))
_TN = (((0,), (0,)), ((), ()))


def _head_sel(x, h, dk):
    if dk == LANES:
        return x[:, h * LANES:(h + 1) * LANES]
    return jnp.where(_lane_mask(x.shape, h * HALF, (h + 1) * HALF), x, jnp.zeros_like(x))


def _head_other(x, h, dk):
    return x[:, h * LANES:(h + 1) * LANES] if dk == LANES else x


def _pair(parts, dk=HALF):
    if dk == LANES:
        return jnp.concatenate(parts, axis=1)
    return jnp.where(_lane_mask(parts[0].shape, 0, HALF), parts[0], parts[1])


def _attn_fwd(q, k, v, crow, *, dk, npairs, scale, name, gather=()):
    (qa, qo), (ka, ko), (va, vo) = q, k, v
    s = qa.shape[0]
    wq = 2 * dk
    t = _pick(s, (512, 256, 128))
    nb = s // t
    bias = crow is not None
    ng = len(gather)

    def body(*refs):
        n_in = 3 + bias
        q_ref, k_ref, v_ref = refs[:3]
        cr_ref = refs[3] if bias else None
        o_ref, lse_ref = refs[n_in + ng:n_in + ng + 2]
        qi = pl.program_id(1)
        if ng:
            phases = _gather_phases([a.shape for a in gather], refs[n_in:n_in + ng], refs[n_in + ng + 2:n_in + 2 * ng + 2],
                                    *refs[n_in + 2 * ng + 2:])
            pair = pl.program_id(0)
            pl.when((pair == 0) & (qi == 0))(phases[0])
            pl.when((pair == npairs - 1) & (qi == 0))(phases[1])
        qb = q_ref[...]
        qs = [_head_sel(qb, h, dk) for h in range(2)]

        def scores(j):
            rows = pl.ds(pl.multiple_of(j * t, t), t)
            kb = k_ref[rows, :]
            out = []
            for h in range(2):
                sc = lax.dot_general(qs[h], _head_other(kb, h, dk), _NT, preferred_element_type=F32) * scale
                out.append(sc - cr_ref[h, j] if bias else sc)
            return tuple(out)

        def update(scs, j, state):
            rows = pl.ds(pl.multiple_of(j * t, t), t)
            vb = v_ref[rows, :]
            soft = []
            for h in range(2):
                m, l, _ = state[3 * h:3 * h + 3]
                sc = scs[h]
                m_new = jnp.maximum(m, jnp.max(sc, axis=1, keepdims=True))
                a = jnp.exp(m - m_new)
                p = jnp.exp(sc - m_new)
                soft.append((m_new, a * l + jnp.sum(p, axis=1, keepdims=True), a, p.astype(BF16)))
            new = []
            for h in range(2):
                m_new, l, a, p = soft[h]
                new += [m_new, l, a * state[3 * h + 2] + jnp.dot(p, vb, preferred_element_type=F32)]
            return tuple(new)

        one = (jnp.full((t, 1), -jnp.inf, F32), jnp.zeros((t, 1), F32), jnp.zeros((t, LANES), F32))
        state = lax.fori_loop(0, qi, lambda j, st: update(scores(j), j, st), one + one)
        row = lax.broadcasted_iota(jnp.int32, (t, t), 0)
        col = lax.broadcasted_iota(jnp.int32, (t, t), 1)
        diag = tuple(jnp.where(col <= row, sc, -jnp.inf) for sc in scores(qi))
        state = update(diag, qi, state)
        outs = []
        for h in range(2):
            m, l, acc = state[3 * h:3 * h + 3]
            outs.append(acc / l)
            lse_ref[h] = m + jnp.log(l)
        o_ref[...] = _pair(outs)
        if ng:
            pl.when((pair == npairs - 1) & (qi == nb - 1))(phases[2])

    in_specs = [pl.BlockSpec((t, wq), lambda p, i: (i, qo // wq + p)), pl.BlockSpec((s, wq), lambda p, i: (0, ko // wq + p)),
                pl.BlockSpec((s, LANES), lambda p, i: (0, vo // LANES + p))]
    args = [qa, ka, va]
    if bias:
        in_specs.append(pl.BlockSpec((2, nb, 1, t), lambda p, i: (p, 0, 0, 0)))
        args.append(crow.reshape(2 * npairs, nb, 1, t))
    out = pl.pallas_call(
        body, name=name, grid=(npairs, nb), in_specs=in_specs + [_ANY] * ng,
        out_specs=[pl.BlockSpec((t, LANES), lambda p, i: (i, p)), pl.BlockSpec((2, t, 1), lambda p, i: (p, i, 0))] + [_ANY] * ng,
        out_shape=[jax.ShapeDtypeStruct((s, npairs * LANES), F32), jax.ShapeDtypeStruct((2 * npairs, s, 1), F32)]
        + [jax.ShapeDtypeStruct((N_CHIPS,) + a.shape, a.dtype) for a in gather],
        scratch_shapes=_sem_pairs(GATHER_COPIES * ng) if ng else [],
        compiler_params=_cparams((_ARB, _ARB) if ng else (_PAR, _PAR)),
    )(*args, *gather)
    return (out[0], out[1], out[2:]) if ng else (out[0], out[1])


def _attn_bwd(q, k, v, o, do, lse, crow, *, dk, npairs, scale, dq_dtype, name, scatter=()):
    (qa, qo), (ka, ko), (va, vo), (oa, oo), (da, do_o) = q, k, v, o, do
    s = qa.shape[0]
    wq = 2 * dk
    t = _pick(s, (512, 256, 128))
    nb = s // t
    bias = crow is not None
    ns = len(scatter)

    def body(*refs):
        n_in, n_out = 6 + bias, 3 + 2 * bias
        q_ref, k_ref, v_ref, o_ref, do_ref, lse_ref = refs[:6]
        outs = refs[n_in + ns:n_in + ns + n_out]
        scr = refs[n_in + 2 * ns + n_out:]
        dq_ref, dk_ref, dv_ref = outs[:3]
        dq_s, dk_s, dv_s = scr[:3]
        if bias:
            cr_ref, dc_ref, dcc_ref, dc_s = refs[6], outs[3], outs[4], scr[3]
        ki, qi = pl.program_id(1), pl.program_id(2)
        if ns:
            send_off, finish = _scatter_phases(ns, refs[n_in:n_in + ns], refs[n_in + ns + n_out:n_in + 2 * ns + n_out], *scr[3 + bias:])
            pair = pl.program_id(0)
            pl.when((pair == 0) & (ki == 0) & (qi == 0))(send_off)

        @pl.when((ki == 0) & (qi == 0))
        def _():
            dq_s[...] = jnp.zeros_like(dq_s)
            if bias:
                dcc_ref[...] = jnp.zeros_like(dcc_ref)

        @pl.when(qi == ki)
        def _():
            dk_s[...] = jnp.zeros_like(dk_s)
            dv_s[...] = jnp.zeros_like(dv_s)
            if bias:
                dc_s[...] = jnp.zeros_like(dc_s)

        def block(on_diagonal):
            qb, kb, vb, dob, ob = q_ref[...], k_ref[...], v_ref[...], do_ref[...], o_ref[...]
            rows = pl.ds(pl.multiple_of(qi * t, t), t)
            row = lax.broadcasted_iota(jnp.int32, (t, t), 0) + qi * t
            col = lax.broadcasted_iota(jnp.int32, (t, t), 1) + ki * t
            fed = []
            for h in range(2):
                kh, doh = _head_other(kb, h, dk), _head_sel(dob, h, HALF)
                sc = lax.dot_general(_head_sel(qb, h, dk), kh, _NT, preferred_element_type=F32)
                dp = lax.dot_general(doh, vb, _NT, preferred_element_type=F32)
                fed.append((kh, doh, sc, dp))
            mid = []
            for h in range(2):
                kh, doh, sc, dp = fed[h]
                sc = sc * scale
                if bias:
                    sc = sc - cr_ref[h]
                if on_diagonal:
                    sc = jnp.where(col <= row, sc, -jnp.inf)
                p = jnp.exp(sc - lse_ref[h])
                delta = jnp.sum(doh.astype(F32) * ob, axis=1, keepdims=True)
                ds = p * (dp - delta)
                mid.append((kh, p.astype(BF16), (ds * scale).astype(BF16)))
            dq_parts, dk_parts, dv_parts = [], [], []
            for h in range(2):
                kh, pb, dsb = mid[h]
                dv_parts.append(lax.dot_general(pb, dob, _TN, preferred_element_type=F32))
                dk_parts.append(lax.dot_general(dsb, _head_other(qb, h, dk), _TN, preferred_element_type=F32))
                dq_parts.append(jnp.dot(dsb, kh, preferred_element_type=F32))
                if bias:
                    ones = jnp.ones((8, t), BF16)
                    dc_s[h] -= jnp.dot(ones, dsb, preferred_element_type=F32)[0:1] * (1.0 / scale)
                    dcc_ref[h, :, rows] += lax.dot_general(ones, dsb, _NT, preferred_element_type=F32)[0:1] * (1.0 / scale)
            dv_s[...] += _pair(dv_parts)
            dk_s[...] += _pair(dk_parts, dk)
            dq_s[rows, :] += _pair(dq_parts, dk)

        pl.when(qi > ki)(lambda: block(False))
        pl.when(qi == ki)(lambda: block(True))

        @pl.when(qi == nb - 1)
        def _():
            dk_ref[...] = dk_s[...].astype(dk_ref.dtype)
            dv_ref[...] = dv_s[...].astype(dv_ref.dtype)
            if bias:
                dc_ref[...] = dc_s[...]

        @pl.when((ki == nb - 1) & (qi == nb - 1))
        def _():
            dq_ref[...] = dq_s[...].astype(dq_ref.dtype)

        if ns:
            pl.when((pair == npairs - 1) & (ki == nb - 1) & (qi == nb - 1))(finish)

    qrow = lambda p, j, i: jnp.maximum(i, j)
    in_specs = [pl.BlockSpec((t, wq), lambda p, j, i: (qrow(p, j, i), qo // wq + p)),
                pl.BlockSpec((t, wq), lambda p, j, i: (j, ko // wq + p)),
                pl.BlockSpec((t, LANES), lambda p, j, i: (j, vo // LANES + p)),
                pl.BlockSpec((t, LANES), lambda p, j, i: (qrow(p, j, i), oo // LANES + p)),
                pl.BlockSpec((t, LANES), lambda p, j, i: (qrow(p, j, i), do_o // LANES + p)),
                pl.BlockSpec((2, t, 1), lambda p, j, i: (p, qrow(p, j, i), 0))]
    args = [qa, ka, va, oa, da, lse]
    out_specs = [pl.BlockSpec((s, wq), lambda p, j, i: (0, p)), pl.BlockSpec((t, wq), lambda p, j, i: (j, p)),
                 pl.BlockSpec((t, LANES), lambda p, j, i: (j, p))]
    out_shape = [jax.ShapeDtypeStruct((s, npairs * wq), dq_dtype), jax.ShapeDtypeStruct((s, npairs * wq), dq_dtype),
                 jax.ShapeDtypeStruct((s, npairs * LANES), BF16)]
    scratch = [pltpu.VMEM((s, wq), F32), pltpu.VMEM((t, wq), F32), pltpu.VMEM((t, LANES), F32)]
    if bias:
        in_specs.append(pl.BlockSpec((2, 1, t), lambda p, j, i: (p, 0, j)))
        args.append(crow)
        out_specs += [pl.BlockSpec((2, 1, t), lambda p, j, i: (p, 0, j)), pl.BlockSpec((2, 1, s), lambda p, j, i: (p, 0, 0))]
        out_shape += [jax.ShapeDtypeStruct((2 * npairs, 1, s), F32), jax.ShapeDtypeStruct((2 * npairs, 1, s), F32)]
        scratch.append(pltpu.VMEM((2, 1, t), F32))
    n_out = len(out_shape)
    out = pl.pallas_call(
        body, name=name, grid=(npairs, nb, nb), in_specs=in_specs + [_ANY] * ns, out_specs=out_specs + [_ANY] * ns,
        out_shape=out_shape + [jax.ShapeDtypeStruct((3,) + a.shape[1:], a.dtype) for a in scatter],
        scratch_shapes=scratch + (_sem_pairs(3 * ns) if ns else []),
        compiler_params=_cparams((_ARB if ns else _PAR, _ARB, _ARB)),
    )(*args, *scatter)
    return (*out[:n_out], out[n_out:]) if ns else out


def _sub_mask(shape, lo, hi):
    sub = lax.broadcasted_iota(jnp.int32, shape, 0)
    return (sub >= lo) & (sub < hi)


def _pair_rows(parts):
    return jnp.where(_sub_mask(parts[0].shape, 0, HALF), parts[0], parts[1])


def _swa_valid(start, t):
    krow = lax.broadcasted_iota(jnp.int32, (t + WINDOW, t), 0)
    qcol = lax.broadcasted_iota(jnp.int32, (t + WINDOW, t), 1)
    diff = qcol - krow + WINDOW
    return (diff >= 0) & (diff < WINDOW) & (krow + start >= WINDOW)


def _swa_tiles(s):
    t = _pick(s, (256, 128))
    return t, _pick(s // t, (4, 2, 1))


def _swa_fwd(q, kp, vp, sink, *, name):
    s = q.shape[0]
    npairs = SWA_HEADS // 2
    t, nt = _swa_tiles(s)
    scale = SWA_DIM ** -0.5

    def body(q_ref, k_ref, v_ref, sk_ref, o_ref, lse_ref):
        sts, vws = [], []
        for u in range(nt):
            start = pl.multiple_of((pl.program_id(1) * nt + u) * t, t)
            kw = k_ref[pl.ds(start, t + WINDOW), :]
            vws.append(v_ref[pl.ds(start, t + WINDOW), :])
            qb = q_ref[u * t:(u + 1) * t, :]
            valid = _swa_valid(start, t)
            for h in range(2):
                st = lax.dot_general(kw, _head_sel(qb, h, HALF), _NT, preferred_element_type=F32) * scale
                sts.append(jnp.where(valid, st, -jnp.inf))
        pts = []
        for u in range(nt):
            for h in range(2):
                st = sts[2 * u + h]
                snk = sk_ref[h][:, 0:1]
                m = jnp.maximum(jnp.max(st, axis=0, keepdims=True), snk)
                e = jnp.exp(st - m)
                l = jnp.sum(e, axis=0, keepdims=True) + jnp.exp(snk - m)
                pts.append((e * (1.0 / l)).astype(BF16))
                lse_ref[h, :, u * t:(u + 1) * t] = m + jnp.log(l)
        for u in range(nt):
            outs = [lax.dot_general(vws[u], pts[2 * u + h], _TN, preferred_element_type=F32) for h in range(2)]
            o_ref[:, u * t:(u + 1) * t] = _pair_rows(outs)

    kvspec = pl.BlockSpec((s + WINDOW, LANES), lambda p, i: (0, 0))
    return pl.pallas_call(
        body, name=name, grid=(npairs, s // (t * nt)),
        in_specs=[pl.BlockSpec((t * nt, LANES), lambda p, i: (i, p)), kvspec, kvspec, pl.BlockSpec((2, 1, LANES), lambda p, i: (p, 0, 0))],
        out_specs=[pl.BlockSpec((LANES, t * nt), lambda p, i: (p, i)), pl.BlockSpec((2, 1, t * nt), lambda p, i: (p, 0, i))],
        out_shape=[jax.ShapeDtypeStruct((npairs * LANES, s), F32), jax.ShapeDtypeStruct((SWA_HEADS, 1, s), F32)],
        compiler_params=_cparams((_PAR, _PAR)),
    )(q, kp, vp, sink)


def _swa_bwd(q, kp, vp, kpt, sink, ot, do, dot, lse, *, name):
    s = q.shape[0]
    npairs = SWA_HEADS // 2
    t, nt = _swa_tiles(s)
    scale = SWA_DIM ** -0.5

    def body(q_ref, k_ref, v_ref, kt_ref, sk_ref, ot_ref, do_ref, dot_ref, lse_ref, dq_ref, dk_ref, dv_ref, dsk_ref):
        pp, i = pl.program_id(0), pl.program_id(1)

        @pl.when((pp == 0) & (i == 0))
        def _():
            dk_ref[...] = jnp.zeros_like(dk_ref)
            dv_ref[...] = jnp.zeros_like(dv_ref)

        @pl.when(i == 0)
        def _():
            dsk_ref[...] = jnp.zeros_like(dsk_ref)

        fed = []
        for u in range(nt):
            cols = slice(u * t, (u + 1) * t)
            start = pl.multiple_of((i * nt + u) * t, t)
            win = pl.ds(start, t + WINDOW)
            qb, dob = q_ref[cols, :], do_ref[cols, :]
            kw, vw = k_ref[win, :], v_ref[win, :]
            valid = _swa_valid(start, t)
            for h in range(2):
                st = lax.dot_general(kw, _head_sel(qb, h, HALF), _NT, preferred_element_type=F32)
                dpt = lax.dot_general(vw, _head_sel(dob, h, HALF), _NT, preferred_element_type=F32)
                fed.append((cols, win, qb, dob, valid, st, dpt))
        mid = []
        for u in range(nt):
            cols = fed[2 * u][0]
            prod = dot_ref[:, cols].astype(F32) * ot_ref[:, cols]
            for h in range(2):
                valid, st, dpt = fed[2 * u + h][4:]
                lse_b = lse_ref[h, :, cols]
                pt = jnp.exp(jnp.where(valid, st * scale, -jnp.inf) - lse_b)
                delta = jnp.sum(jnp.where(_sub_mask(prod.shape, h * HALF, (h + 1) * HALF), prod, 0.0), axis=0, keepdims=True)
                dst = pt * (dpt - delta)
                psink = jnp.exp(sk_ref[h][:, 0:1] - lse_b)
                dsk_ref[h] += jnp.broadcast_to(-jnp.sum(psink * delta, axis=1, keepdims=True), (1, LANES))
                mid.append((pt.astype(BF16), (dst * scale).astype(BF16)))
        for u in range(nt):
            cols, win, qb, dob = fed[2 * u][:4]
            ktw = kt_ref[:, win]
            dq_parts = [jnp.dot(ktw, mid[2 * u + h][1], preferred_element_type=F32) for h in range(2)]
            dk_parts = [jnp.dot(mid[2 * u + h][1], qb, preferred_element_type=F32) for h in range(2)]
            dv_parts = [jnp.dot(mid[2 * u + h][0], dob, preferred_element_type=F32) for h in range(2)]
            dq_ref[:, cols] = _pair_rows(dq_parts)
            dk_ref[win, :] += _pair(dk_parts)
            dv_ref[win, :] += _pair(dv_parts)

    tile = pl.BlockSpec((t * nt, LANES), lambda p, i: (i, p))
    ttile = pl.BlockSpec((LANES, t * nt), lambda p, i: (p, i))
    kvspec = pl.BlockSpec((s + WINDOW, LANES), lambda p, i: (0, 0))
    ktspec = pl.BlockSpec((LANES, s + WINDOW), lambda p, i: (0, 0))
    skspec = pl.BlockSpec((2, 1, LANES), lambda p, i: (p, 0, 0))
    return pl.pallas_call(
        body, name=name, grid=(npairs, s // (t * nt)),
        in_specs=[tile, kvspec, kvspec, ktspec, skspec, ttile, tile, ttile, pl.BlockSpec((2, 1, t * nt), lambda p, i: (p, 0, i))],
        out_specs=[ttile, kvspec, kvspec, skspec],
        out_shape=[jax.ShapeDtypeStruct((npairs * LANES, s), F32), jax.ShapeDtypeStruct((s + WINDOW, LANES), F32),
                   jax.ShapeDtypeStruct((s + WINDOW, LANES), F32), jax.ShapeDtypeStruct((SWA_HEADS, 1, LANES), F32)],
        compiler_params=_cparams((_ARB, _ARB)),
    )(q, kp, vp, kpt, sink, ot, do, dot, lse)


def _odd_rope(h, cos, sin, *, name):
    s = h.shape[0]
    tb = _pick(s, (512, 256, 128))
    nq = SWA_HEADS * SWA_DIM

    def body(q_ref, kv_ref, cos_ref, sin_ref, qo_ref, ko_ref, vo_ref):
        cs, sn = cos_ref[...], sin_ref[...]
        for j in range(nq // LANES):
            sl = slice(j * LANES, (j + 1) * LANES)
            qo_ref[:, sl] = _rope_tile(q_ref[:, sl], cs, sn, SWA_DIM // 2).astype(BF16)
        ko_ref[...] = _rope_tile(kv_ref[:, :LANES], cs, sn, SWA_DIM // 2).astype(BF16)
        vo_ref[...] = kv_ref[:, LANES:].astype(BF16)

    rows = lambda w, c=0: pl.BlockSpec((tb, w), lambda i: (i, c))
    return pl.pallas_call(
        body, name=name, grid=(s // tb,),
        in_specs=[rows(nq, O_Q // nq), rows(2 * LANES, O_K // (2 * LANES)), rows(LANES), rows(LANES)],
        out_specs=[rows(nq), rows(LANES), rows(LANES)],
        out_shape=[jax.ShapeDtypeStruct((s, nq), BF16), jax.ShapeDtypeStruct((s, LANES), BF16), jax.ShapeDtypeStruct((s, LANES), BF16)],
        compiler_params=_cparams((_PAR,)),
    )(h, h, cos, sin)


def _odd_rope_bwd(dq_t, dk, cos, sin, *, name):
    s = dk.shape[0]
    tb = _pick(s, (256, 128))
    nq = SWA_HEADS * SWA_DIM

    def body(dq_ref, dk_ref, cos_ref, sin_ref, qo_ref, ko_ref):
        cs, sn = cos_ref[...], -sin_ref[...]
        dq = dq_ref[...].T
        for j in range(nq // LANES):
            sl = slice(j * LANES, (j + 1) * LANES)
            qo_ref[:, sl] = _rope_tile(dq[:, sl], cs, sn, SWA_DIM // 2).astype(BF16)
        ko_ref[...] = _rope_tile(dk_ref[...], cs, sn, SWA_DIM // 2).astype(BF16)

    rows = lambda w: pl.BlockSpec((tb, w), lambda i: (i, 0))
    return pl.pallas_call(
        body, name=name, grid=(s // tb,),
        in_specs=[pl.BlockSpec((nq, tb), lambda i: (0, i)), rows(LANES), rows(LANES), rows(LANES)],
        out_specs=[rows(nq), rows(LANES)],
        out_shape=[jax.ShapeDtypeStruct((s, nq), BF16), jax.ShapeDtypeStruct((s, LANES), BF16)],
        compiler_params=_cparams((_PAR,)),
    )(dq_t, dk, cos, sin)


def _mixer_out(o_refs, o_t):
    if o_t:
        return o_refs[0][...].T
    return o_refs[0][...] if len(o_refs) == 1 else jnp.concatenate([r[...] for r in o_refs], axis=1)


def _mixer_specs(o_parts, o_t, tb):
    if o_t:
        return [pl.BlockSpec((D_MODEL, tb), lambda i: (0, i))]
    return [pl.BlockSpec((tb, a.shape[1]), lambda i: (i, 0)) for a in o_parts]


def _out_fwd(o_parts, h, x, w_out, g, b, *, name, o_t=False):
    s = x.shape[0]
    tb = _pick(s, (256, 128))
    n_o = len(o_parts)

    def body(*refs):
        o_refs = refs[:n_o]
        gate_ref, x_ref, w_ref, g_ref, b_ref, xn_ref, xb_ref, z_ref, xh_ref, rs_ref = refs[n_o:]
        gate = gate_ref[...]
        o = _mixer_out(o_refs, o_t)
        z = (o * (gate * _sigmoid(gate))).astype(BF16)
        z_ref[...] = z
        r = ALPHA * x_ref[...] + jnp.dot(z, w_ref[...], preferred_element_type=F32)
        mu = jnp.mean(r, axis=-1, keepdims=True)
        rc = r - mu
        rstd = lax.rsqrt(jnp.mean(rc * rc, axis=-1, keepdims=True) + LN_EPS)
        xh = rc * rstd
        xn = xh * g_ref[...] + b_ref[...]
        xh_ref[...] = xh
        rs_ref[...] = rstd
        xn_ref[...] = xn
        xb_ref[...] = xn.astype(BF16)

    rows = lambda w: pl.BlockSpec((tb, w), lambda i: (i, 0))
    full = lambda a: pl.BlockSpec(a.shape, lambda i: (0,) * a.ndim)
    return pl.pallas_call(
        body, name=name, grid=(s // tb,),
        in_specs=_mixer_specs(o_parts, o_t, tb) + [rows(D_MODEL), rows(D_MODEL), full(w_out), full(g), full(b)],
        out_specs=[rows(D_MODEL), rows(D_MODEL), rows(D_MODEL), rows(D_MODEL), rows(1)],
        out_shape=[jax.ShapeDtypeStruct((s, D_MODEL), F32), jax.ShapeDtypeStruct((s, D_MODEL), BF16),
                   jax.ShapeDtypeStruct((s, D_MODEL), BF16), jax.ShapeDtypeStruct((s, D_MODEL), F32),
                   jax.ShapeDtypeStruct((s, 1), F32)],
        compiler_params=_cparams((_PAR,)),
    )(*o_parts, h, x, w_out, g, b)


def _out_bwd(dxn, xh, rstd, g, w_out_t, o_parts, h, *, name, o_t=False):
    s = dxn.shape[0]
    tb = _pick(s, (256, 128))
    n_o = len(o_parts)

    def body(*refs):
        dxn_ref, xh_ref, rs_ref, g_ref, wt_ref = refs[:5]
        o_refs = refs[5:5 + n_o]
        gate_ref, dr_ref, dy_ref, do_ref, dgate_ref, dg_ref, db_ref = refs[5 + n_o:12 + n_o]

        @pl.when(pl.program_id(0) == 0)
        def _():
            dg_ref[...] = jnp.zeros_like(dg_ref)
            db_ref[...] = jnp.zeros_like(db_ref)

        dxn_b, xh_b = dxn_ref[...], xh_ref[...]
        dg_ref[...] += jnp.sum(dxn_b * xh_b, axis=0, keepdims=True)
        db_ref[...] += jnp.sum(dxn_b, axis=0, keepdims=True)
        dxh = dxn_b * g_ref[...]
        dr = rs_ref[...] * (dxh - jnp.mean(dxh, axis=-1, keepdims=True) - xh_b * jnp.mean(dxh * xh_b, axis=-1, keepdims=True))
        dr_ref[...] = dr
        dy = dr.astype(BF16)
        dy_ref[...] = dy
        dz = jnp.dot(dy, wt_ref[...], preferred_element_type=F32)
        gate = gate_ref[...]
        sg = _sigmoid(gate)
        o = _mixer_out(o_refs, o_t)
        do = dz * (gate * sg)
        do_ref[...] = do.astype(BF16)
        if o_t:
            refs[12 + n_o][...] = do.T.astype(BF16)
        dgate_ref[...] = (dz * o * (sg * (1.0 + gate * (1.0 - sg)))).astype(BF16)

    rows = lambda w: pl.BlockSpec((tb, w), lambda i: (i, 0))
    full = lambda a: pl.BlockSpec(a.shape, lambda i: (0,) * a.ndim)
    cols = pl.BlockSpec((D_MODEL, tb), lambda i: (0, i))
    return pl.pallas_call(
        body, name=name, grid=(s // tb,),
        in_specs=[rows(D_MODEL), rows(D_MODEL), rows(1), full(g), full(w_out_t)] + _mixer_specs(o_parts, o_t, tb) + [rows(D_MODEL)],
        out_specs=[rows(D_MODEL), rows(D_MODEL), rows(D_MODEL), rows(D_MODEL), full(g), full(g)] + [cols] * o_t,
        out_shape=[jax.ShapeDtypeStruct((s, D_MODEL), F32), jax.ShapeDtypeStruct((s, D_MODEL), BF16),
                   jax.ShapeDtypeStruct((s, D_MODEL), BF16), jax.ShapeDtypeStruct((s, D_MODEL), BF16),
                   jax.ShapeDtypeStruct(g.shape, F32), jax.ShapeDtypeStruct(g.shape, F32)]
        + [jax.ShapeDtypeStruct((D_MODEL, s), BF16)] * o_t,
        compiler_params=_cparams((_ARB,)),
    )(dxn, xh, rstd, g, w_out_t, *o_parts, h)


def _loss_grad(y, target, *, name):
    s, d = y.shape
    tb = _pick(s, (512, 256, 128))

    def body(y_ref, t_ref, dy_ref, l_ref):
        @pl.when(pl.program_id(0) == 0)
        def _():
            l_ref[...] = jnp.zeros_like(l_ref)

        err = y_ref[...] - t_ref[...]
        dy_ref[...] = err * (1.0 / d)
        per_tok = jnp.mean(err * err, axis=-1, keepdims=True)
        l_ref[...] += 0.5 * jnp.sum(per_tok, axis=0, keepdims=True)

    rows = pl.BlockSpec((tb, d), lambda i: (i, 0))
    return pl.pallas_call(
        body, name=name, grid=(s // tb,), in_specs=[rows, rows],
        out_specs=[rows, pl.BlockSpec((8, LANES), lambda i: (0, 0))],
        out_shape=[jax.ShapeDtypeStruct((s, d), F32), jax.ShapeDtypeStruct((8, LANES), F32)],
        compiler_params=_cparams((_ARB,)),
    )(y, target)


def _adamw(w, g, m, v, *, name):
    shape = w.shape
    w3, g3, m3, v3 = (a.reshape((1,) * (3 - a.ndim) + a.shape) for a in (w, g, m, v))
    a0, a1, a2 = w3.shape
    tb = _pick(a1, (256, 128)) if a1 % 8 == 0 else a1
    c1 = 1.0 - ADAM_B1 ** ADAM_STEP
    c2 = 1.0 - ADAM_B2 ** ADAM_STEP

    def body(w_ref, g_ref, m_ref, v_ref, d_ref, mo_ref, vo_ref):
        gg = g_ref[...]
        mn = ADAM_B1 * m_ref[...] + (1.0 - ADAM_B1) * gg
        vn = ADAM_B2 * v_ref[...] + (1.0 - ADAM_B2) * (gg * gg)
        mo_ref[...] = mn
        vo_ref[...] = vn
        d_ref[...] = -ADAM_LR * ((mn / c1) / (jnp.sqrt(vn / c2) + ADAM_EPS) + ADAM_WD * w_ref[...])

    spec = pl.BlockSpec((1, tb, a2), lambda i, j: (i, j, 0))
    outs = pl.pallas_call(
        body, name=name, grid=(a0, a1 // tb), in_specs=[spec] * 4, out_specs=[spec] * 3,
        out_shape=[jax.ShapeDtypeStruct(w3.shape, F32)] * 3, compiler_params=_cparams((_PAR, _PAR)),
    )(w3, g3, m3, v3)
    return tuple(a.reshape(shape) for a in outs)


def _place():
    x, y, c = lax.axis_index("x"), lax.axis_index("y"), lax.axis_index("c")
    return x, y, c, [(1 - x, y), (x, 1 - y), (1 - x, 1 - y)]


_ANY = pl.BlockSpec(memory_space=pl.ANY)


def _sem_pairs(n):
    return [pltpu.SemaphoreType.DMA((n,)), pltpu.SemaphoreType.DMA((n,))]


GATHER_COPIES = 7


def _gather_phases(shapes, srcs, outs, send_sems, recv_sems):
    nt = len(shapes)
    x, y, c, chips = _place()
    me = 2 * x + y
    sib = (x, y, 1 - c)

    def half(t, chip, hc):
        rh = shapes[t][1] // 2
        return outs[t].at[chip, :, pl.ds(hc * rh, rh), :]

    def copy(t, kk, s_ref, d_ref, to):
        return pltpu.make_async_remote_copy(src_ref=s_ref, dst_ref=d_ref, send_sem=send_sems.at[GATHER_COPIES * t + kk],
                                            recv_sem=recv_sems.at[GATHER_COPIES * t + kk], device_id=to, device_id_type=MESH)

    def first_hop(t, j):
        rh = shapes[t][1] // 2
        cx, cy = chips[j]
        return copy(t, j, srcs[t].at[:, pl.ds(c * rh, rh), :], half(t, me, c), (cx, cy, c))

    def passed(t, j):
        cx, cy = chips[j]
        landed = half(t, 2 * cx + cy, c)
        return copy(t, 3 + j, landed, landed, sib)

    def own(t):
        return copy(t, 6, srcs[t], outs[t].at[me], sib)

    pairs = [(t, j) for j in range(3) for t in range(nt)]

    def start():
        for t, j in pairs:
            first_hop(t, j).start()
        for t in range(nt):
            own(t).start()

    def pass_on():
        for t, j in pairs:
            cx, cy = chips[j]
            landed = half(t, 2 * cx + cy, c)
            copy(t, j, landed, landed, (cx, cy, c)).wait_recv()
            passed(t, j).start()

    def finish():
        for t, j in pairs:
            cx, cy = chips[j]
            theirs = half(t, 2 * cx + cy, 1 - c)
            copy(t, 3 + j, theirs, theirs, sib).wait_recv()
        for t in range(nt):
            own(t).wait()
        for t, j in pairs:
            first_hop(t, j).wait_send()
            passed(t, j).wait_send()

    return start, pass_on, finish


def _gather_chip_shards(tensors, *, name):
    nt = len(tensors)

    def body(*refs):
        for phase in _gather_phases([a.shape for a in tensors], refs[:nt], refs[nt:2 * nt], *refs[2 * nt:]):
            phase()

    return pl.pallas_call(
        body, name=name, in_specs=[_ANY] * nt, out_specs=[_ANY] * nt,
        out_shape=[jax.ShapeDtypeStruct((N_CHIPS,) + a.shape, a.dtype) for a in tensors],
        scratch_shapes=_sem_pairs(GATHER_COPIES * nt),
    )(*tensors)


def _swap_halves(gs, *, name):
    nt = len(gs)

    def body(*refs):
        srcs, outs = refs[:nt], refs[nt:2 * nt]
        send_sems, recv_sems = refs[2 * nt:]
        x, y, c, _ = _place()
        cps = [pltpu.make_async_remote_copy(src_ref=srcs[t].at[:, 1 - c], dst_ref=outs[t], send_sem=send_sems.at[t],
                                            recv_sem=recv_sems.at[t], device_id=(x, y, 1 - c), device_id_type=MESH)
               for t in range(nt)]
        for cp in cps:
            cp.start()
        for cp in cps:
            cp.wait()

    return pl.pallas_call(
        body, name=name, in_specs=[_ANY] * nt, out_specs=[_ANY] * nt,
        out_shape=[jax.ShapeDtypeStruct((N_CHIPS,) + g.shape[2:], g.dtype) for g in gs], scratch_shapes=_sem_pairs(nt),
    )(*gs)


def _scatter_phases(nt, srcs, outs, send_sems, recv_sems):
    x, y, c, chips = _place()

    def copies():
        return [pltpu.make_async_remote_copy(src_ref=srcs[t].at[2 * cx + cy], dst_ref=outs[t].at[j],
                                             send_sem=send_sems.at[3 * t + j], recv_sem=recv_sems.at[3 * t + j],
                                             device_id=(cx, cy, c), device_id_type=MESH)
                for j, (cx, cy) in enumerate(chips) for t in range(nt)]

    def start():
        for cp in copies():
            cp.start()

    def finish():
        for cp in copies():
            cp.wait()

    return start, finish


def _scatter_to_chips(blocks, *, name):
    nt = len(blocks)

    def body(*refs):
        for phase in _scatter_phases(nt, refs[:nt], refs[nt:2 * nt], *refs[2 * nt:]):
            phase()

    return pl.pallas_call(
        body, name=name, in_specs=[_ANY] * nt, out_specs=[_ANY] * nt,
        out_shape=[jax.ShapeDtypeStruct((3,) + a.shape[1:], a.dtype) for a in blocks], scratch_shapes=_sem_pairs(3 * nt),
    )(*blocks)


def _join_halves(bufs, *, name):
    nt = len(bufs)

    def body(*refs):
        srcs, outs = refs[:nt], refs[nt:2 * nt]
        send_sems, recv_sems = refs[2 * nt:]
        x, y, c, _ = _place()
        cps = []
        for t in range(nt):
            rh = bufs[t].shape[1] // 2
            rows = pl.ds(c * rh, rh)
            cps.append(pltpu.make_async_remote_copy(src_ref=srcs[t].at[:, rows, :], dst_ref=outs[t].at[:, rows, :],
                                                    send_sem=send_sems.at[t], recv_sem=recv_sems.at[t],
                                                    device_id=(x, y, 1 - c), device_id_type=MESH))
        for cp in cps:
            cp.start()
        for cp in cps:
            cp.wait()

    return pl.pallas_call(
        body, name=name, in_specs=[_ANY] * nt, out_specs=[_ANY] * nt,
        out_shape=[jax.ShapeDtypeStruct(b.shape, b.dtype) for b in bufs], scratch_shapes=_sem_pairs(nt),
        input_output_aliases={t: t for t in range(nt)},
    )(*bufs)


def _add_sibling(g, recv, cidx, chip_idx, *, name):
    _, _, na, rh, cdim = g.shape
    tb = _pick(rh, (256, 128, 64, 32, 16))

    def body(c_ref, k_ref, g_ref, r_ref, s_ref, o_ref):
        tot = g_ref[0, 0] + r_ref[0]
        s_ref[0] = tot.astype(BF16)

        @pl.when(pl.program_id(2) == k_ref[0])
        def _():
            o_ref[...] = tot

    return pl.pallas_call(
        body, name=name,
        grid_spec=pltpu.PrefetchScalarGridSpec(
            num_scalar_prefetch=2, grid=(na, rh // tb, N_CHIPS),
            in_specs=[pl.BlockSpec((1, 1, 1, tb, cdim), lambda a, i, k, c_ref, k_ref: (k, c_ref[0], a, i, 0)),
                      pl.BlockSpec((1, 1, tb, cdim), lambda a, i, k, c_ref, k_ref: (k, a, i, 0))],
            out_specs=[pl.BlockSpec((1, 1, tb, cdim), lambda a, i, k, c_ref, k_ref: (k, a, i, 0)),
                       pl.BlockSpec((1, tb, cdim), lambda a, i, k, c_ref, k_ref: (a, i, 0))]),
        out_shape=[jax.ShapeDtypeStruct((N_CHIPS, na, rh, cdim), BF16), jax.ShapeDtypeStruct((na, rh, cdim), F32)],
        compiler_params=_cparams((_PAR, _PAR, _ARB)),
    )(cidx, chip_idx, g, recv)


def _add_chips(own, recv, cidx, *, name):
    na, rh, cdim = own.shape
    tb = _pick(rh, (256, 128, 64, 32, 16))
    nblk = rh // tb

    def body(c_ref, a_ref, r0_ref, r1_ref, r2_ref, o_ref):
        o_ref[...] = ((a_ref[...] + r0_ref[0].astype(F32)) + r1_ref[0].astype(F32)) + r2_ref[0].astype(F32)

    slot = lambda j: pl.BlockSpec((1, 1, tb, cdim), lambda a, i, c_ref: (j, a, i, 0))
    return pl.pallas_call(
        body, name=name,
        grid_spec=pltpu.PrefetchScalarGridSpec(
            num_scalar_prefetch=1, grid=(na, nblk),
            in_specs=[pl.BlockSpec((1, tb, cdim), lambda a, i, c_ref: (a, i, 0)), slot(0), slot(1), slot(2)],
            out_specs=pl.BlockSpec((1, tb, cdim), lambda a, i, c_ref: (a, c_ref[0] * nblk + i, 0))),
        out_shape=jax.ShapeDtypeStruct((na, 2 * rh, cdim), F32), compiler_params=_cparams((_PAR, _PAR)),
    )(cidx, own, recv, recv, recv)


def _all_reduce_small(v, *, name):
    r, cdim = v.shape

    def body(v_ref, o_ref, buf, send_sems, recv_sems):
        x, y, c, _ = _place()
        me = 4 * x + 2 * y + c
        buf[me] = v_ref[...]
        cps = []
        for p in range(1, N_DEV):
            to = (1 - x if p & 4 else x, 1 - y if p & 2 else y, 1 - c if p & 1 else c)
            cp = pltpu.make_async_remote_copy(src_ref=v_ref, dst_ref=buf.at[me], send_sem=send_sems.at[p - 1],
                                              recv_sem=recv_sems.at[p - 1], device_id=to, device_id_type=MESH)
            cp.start()
            cps.append(cp)
        for p in range(1, N_DEV):
            frm = (4 * x + 2 * y + c) ^ p
            pltpu.make_async_remote_copy(src_ref=v_ref, dst_ref=buf.at[frm], send_sem=send_sems.at[p - 1],
                                         recv_sem=recv_sems.at[p - 1], device_id=(x, y, c), device_id_type=MESH).wait_recv()
        for cp in cps:
            cp.wait_send()
        tot = buf[0]
        for i in range(1, N_DEV):
            tot = tot + buf[i]
        o_ref[...] = tot

    vm = pl.BlockSpec(memory_space=pltpu.VMEM)
    return pl.pallas_call(
        body, name=name, in_specs=[vm], out_specs=vm, out_shape=jax.ShapeDtypeStruct((r, cdim), F32),
        scratch_shapes=[pltpu.VMEM((N_DEV, r, cdim), F32), pltpu.SemaphoreType.DMA((N_DEV - 1,)), pltpu.SemaphoreType.DMA((N_DEV - 1,))],
    )(v)


_SHARDED = ("even_w_in", "even_w_uq", "even_w_ukv", "even_w_out", "odd_w_in", "odd_w_out")
_COL_SHARDED = ("even_w_in", "even_w_uq", "even_w_ukv", "odd_w_in")
PACK_COLS = 1024


def _unshard(name, stacked):
    n, a, b, cc = stacked.shape
    if name in _COL_SHARDED:
        return stacked.transpose(1, 2, 0, 3).reshape(a, b, n * cc)
    return stacked.transpose(1, 0, 2, 3).reshape(a, n * b, cc)


def _chip_halves(name, full):
    a, b, cc = full.shape
    if name in _COL_SHARDED:
        return full.reshape(a, 2, b // 2, N_CHIPS, cc // N_CHIPS).transpose(3, 1, 0, 2, 4)
    return full.reshape(a, N_CHIPS, 2, b // (2 * N_CHIPS), cc).transpose(1, 2, 0, 3, 4)


def _rope_tables(s):
    pos = jnp.arange(s, dtype=F32)

    def ang(d):
        inv = ROPE_THETA ** (-jnp.arange(0, d, 2, dtype=F32) / d)
        a = pos[:, None] * inv[None, :]
        return jnp.cos(a), jnp.sin(a)

    c16, s16 = ang(MLA_ROPE)
    one, zero = jnp.ones((s, KPE_LANE), F32), jnp.zeros((s, KPE_LANE), F32)
    cos_m = jnp.concatenate([one, c16, c16, one[:, :32]], axis=1)
    sin_m = jnp.concatenate([zero, -s16, s16, zero[:, :32]], axis=1)
    c32, s32 = ang(SWA_DIM)
    cos_s = jnp.concatenate([c32, c32, c32, c32], axis=1)
    sin_s = jnp.concatenate([-s32, s32, -s32, s32], axis=1)
    return cos_m, sin_m, cos_s, sin_s


def _even_weights(w_in, w_uq, w_ukv):
    zeros = lambda n: jnp.zeros((D_MODEL, n), w_in.dtype)
    wcq, wckv, wkpe = w_in[:, 0:256], w_in[:, 256:384], w_in[:, 384:416]
    wfq, wfk, wfv = w_in[:, 416:928], w_in[:, 928:1440], w_in[:, 1440:1952]
    wfl, wg = w_in[:, 1952:1960], w_in[:, 1960:2984]
    misc = jnp.concatenate([zeros(KPE_LANE), wkpe, wfl, zeros(LANES - FL_LANE - FOX_HEADS)], axis=1)
    w_in_p = jnp.concatenate([wg, wfq, wfk, wfv, wcq, wckv, misc], axis=1)
    uq = w_uq.reshape(MLA_Q_RANK, MLA_HEADS, MLA_NOPE + MLA_ROPE)
    uq_p = jnp.pad(uq, ((0, 0), (0, 0), (0, HEAD_PAD - MLA_NOPE - MLA_ROPE))).reshape(MLA_Q_RANK, MLA_HEADS * HEAD_PAD)
    ukv = w_ukv.reshape(MLA_KV_RANK, MLA_HEADS, MLA_NOPE + MLA_V)
    uk_p = jnp.pad(ukv[..., :MLA_NOPE], ((0, 0), (0, 0), (0, HEAD_PAD - MLA_NOPE))).reshape(MLA_KV_RANK, MLA_HEADS * HEAD_PAD)
    uv = ukv[..., MLA_NOPE:].reshape(MLA_KV_RANK, MLA_HEADS * MLA_V)
    ukv_p = jnp.concatenate([uk_p, uv], axis=1)
    return w_in_p, w_in_p.T, uq_p, uq_p.T, ukv_p, ukv_p.T


def _even_weight_grads(dw_in_p, duq_p, dukv_p):
    g = dw_in_p
    gate, fq, fk, fv = g[:, E_GATE:E_FQ], g[:, E_FQ:E_FK], g[:, E_FK:E_FV], g[:, E_FV:E_SMALL]
    cq, ckv, misc = g[:, E_SMALL:E_SMALL + 256], g[:, E_SMALL + 256:E_SMALL + 384], g[:, E_SMALL + 384:]
    dw_in = jnp.concatenate([cq, ckv, misc[:, KPE_LANE:KPE_LANE + MLA_ROPE], fq, fk, fv, misc[:, FL_LANE:FL_LANE + FOX_HEADS], gate], axis=1)
    duq = duq_p.reshape(MLA_Q_RANK, MLA_HEADS, HEAD_PAD)[..., :MLA_NOPE + MLA_ROPE].reshape(MLA_Q_RANK, -1)
    nq = MLA_HEADS * HEAD_PAD
    dk = dukv_p[:, :nq].reshape(MLA_KV_RANK, MLA_HEADS, HEAD_PAD)[..., :MLA_NOPE]
    dvv = dukv_p[:, nq:].reshape(MLA_KV_RANK, MLA_HEADS, MLA_V)
    dukv = jnp.concatenate([dk, dvv], axis=-1).reshape(MLA_KV_RANK, -1)
    return dw_in, duq, dukv


def _interleave(w, forward):
    a, b = (SWA_KV_HEADS, SWA_HEADS // SWA_KV_HEADS) if forward else (SWA_HEADS // SWA_KV_HEADS, SWA_KV_HEADS)
    return w.reshape(w.shape[0], a, b, -1).transpose(0, 2, 1, 3).reshape(w.shape[0], -1)


def _odd_weights(w_in, w_out):
    q, k, v, gate = w_in[:, 0:1024], w_in[:, 1024:1152], w_in[:, 1152:1280], w_in[:, 1280:2304]
    w_p = jnp.concatenate([_interleave(gate, True), _interleave(q, True), k, v], axis=1)
    w_out_p = _interleave(w_out.T, True).T
    return w_p, w_p.T, w_out_p, w_out_p.T


def _odd_weight_grads(g, dw_out_p):
    dw_in = jnp.concatenate([_interleave(g[:, O_Q:O_K], False), g[:, O_K:O_V], g[:, O_V:], _interleave(g[:, O_GATE:O_Q], False)], axis=1)
    return dw_in, _interleave(dw_out_p.T, False).T


def kernel(x, even_w_in, even_q_norm, even_w_uq, even_kv_norm, even_w_ukv, even_b_f, even_w_out, even_ln_g, even_ln_b, odd_w_in, odd_sinks, odd_w_out, odd_ln_g, odd_ln_b, loss_target, m_even_w_in, m_even_q_norm, m_even_w_uq, m_even_kv_norm, m_even_w_ukv, m_even_b_f, m_even_w_out, m_even_ln_g, m_even_ln_b, m_odd_w_in, m_odd_sinks, m_odd_w_out, m_odd_ln_g, m_odd_ln_b, v_even_w_in, v_even_q_norm, v_even_w_uq, v_even_kv_norm, v_even_w_ukv, v_even_b_f, v_even_w_out, v_even_ln_g, v_even_ln_b, v_odd_w_in, v_odd_sinks, v_odd_w_out, v_odd_ln_g, v_odd_ln_b):
    weights = dict(even_w_in=even_w_in, even_q_norm=even_q_norm, even_w_uq=even_w_uq, even_kv_norm=even_kv_norm,
                   even_w_ukv=even_w_ukv, even_b_f=even_b_f, even_w_out=even_w_out, even_ln_g=even_ln_g, even_ln_b=even_ln_b,
                   odd_w_in=odd_w_in, odd_sinks=odd_sinks, odd_w_out=odd_w_out, odd_ln_g=odd_ln_g, odd_ln_b=odd_ln_b)
    mom_m = dict(even_w_in=m_even_w_in, even_q_norm=m_even_q_norm, even_w_uq=m_even_w_uq, even_kv_norm=m_even_kv_norm,
                 even_w_ukv=m_even_w_ukv, even_b_f=m_even_b_f, even_w_out=m_even_w_out, even_ln_g=m_even_ln_g, even_ln_b=m_even_ln_b,
                 odd_w_in=m_odd_w_in, odd_sinks=m_odd_sinks, odd_w_out=m_odd_w_out, odd_ln_g=m_odd_ln_g, odd_ln_b=m_odd_ln_b)
    mom_v = dict(even_w_in=v_even_w_in, even_q_norm=v_even_q_norm, even_w_uq=v_even_w_uq, even_kv_norm=v_even_kv_norm,
                 even_w_ukv=v_even_w_ukv, even_b_f=v_even_b_f, even_w_out=v_even_w_out, even_ln_g=v_even_ln_g, even_ln_b=v_even_ln_b,
                 odd_w_in=v_odd_w_in, odd_sinks=v_odd_sinks, odd_w_out=v_odd_w_out, odd_ln_g=v_odd_ln_g, odd_ln_b=v_odd_ln_b)
    names = list(weights)
    xl = x[0]
    tgt = loss_target[0]
    s = xl.shape[0]
    ax, ay, ac = lax.axis_index("x"), lax.axis_index("y"), lax.axis_index("c")
    chip = 2 * ax + ay
    c_idx = ac.astype(jnp.int32).reshape(1)
    chip_idx = chip.astype(jnp.int32).reshape(1)

    ln_odd = jnp.pad(jnp.concatenate([odd_ln_g, odd_ln_b]), ((0, 12), (0, 0)))[None]
    wb = {n: weights[n].astype(BF16) for n in _SHARDED}
    even_mats = [n for n in _SHARDED if n.startswith("even")]
    odd_mats = [n for n in _SHARDED if n.startswith("odd")]
    gathered = _gather_chip_shards([wb[n][0:1] for n in even_mats] + [ln_odd], name="gather_first")
    full = {n: [_unshard(n, gathered[i])[0], None] for i, n in enumerate(even_mats)}
    later = [wb[n][1:2] for n in even_mats] + [wb[n] for n in odd_mats]
    ln_all = gathered[-1][:, 0]
    odd_g_full = ln_all[:, 0:2].transpose(1, 0, 2).reshape(2, D_MODEL)
    odd_b_full = ln_all[:, 2:4].transpose(1, 0, 2).reshape(2, D_MODEL)

    cos_m, sin_m, cos_s, sin_s = _rope_tables(s)
    bf_tiles = jnp.pad(even_b_f, ((0, 0), (FL_LANE, LANES - FL_LANE - FOX_HEADS)))
    sink_tiles = jnp.broadcast_to(_interleave(odd_sinks, True)[:, :, None, None], (2, SWA_HEADS, 1, LANES))
    mla_scale = (MLA_NOPE + MLA_ROPE) ** -0.5
    fox_scale = FOX_DIM ** -0.5
    mla_pairs, fox_pairs = MLA_HEADS // 2, FOX_HEADS // 2

    saved = []
    x_f, x_b = xl, xl.astype(BF16)
    for layer in range(DEPTH):
        j = layer // 2
        ln = f"L{layer}"
        if layer % 2 == 0:
            w_in_p, w_in_t, uq_p, uq_t, ukv_p, ukv_t = _even_weights(full["even_w_in"][j], full["even_w_uq"][j], full["even_w_ukv"][j])
            w_out, w_out_t = full["even_w_out"][j], full["even_w_out"][j].T
            qg, kg, bft = even_q_norm[j][None], even_kv_norm[j][None], bf_tiles[j][None]
            h, hb = _mm(x_b, w_in_p, out_dtype=F32, name=ln + "_in", also_bf16=True)
            q, k, v, qn, kvn, logf = _even_mid_fwd(h, qg, kg, bft, uq_p, ukv_p, cos_m, sin_m, name=ln + "_mid")
            cum = _cumsum(logf, reverse=False, name=ln + "_cum")[:, FL_LANE:FL_LANE + FOX_HEADS].T
            crow = cum[:, None, :]
            if layer == 0:
                o_mla, lse_mla, rest = _attn_fwd((q, 0), (k, 0), (v, 0), None, dk=HEAD_PAD, npairs=mla_pairs, scale=mla_scale,
                                                 name=ln + "_mla", gather=later)
                for i, n in enumerate(even_mats):
                    full[n][1] = _unshard(n, rest[i])[0]
                for i, n in enumerate(odd_mats):
                    full[n] = _unshard(n, rest[len(even_mats) + i])
            else:
                o_mla, lse_mla = _attn_fwd((q, 0), (k, 0), (v, 0), None, dk=HEAD_PAD, npairs=mla_pairs, scale=mla_scale, name=ln + "_mla")
            o_fox, lse_fox = _attn_fwd((hb, E_FQ), (hb, E_FK), (hb, E_FV), crow, dk=FOX_DIM, npairs=fox_pairs,
                                       scale=fox_scale, name=ln + "_fox")
            o_parts = [o_mla, o_fox]
            g_ln, b_ln = even_ln_g[j][None], even_ln_b[j][None]
            x_n, x_nb, z, xh, rstd = _out_fwd(o_parts, h, x_f, w_out, g_ln, b_ln, name=ln + "_out")
            saved.append(dict(h=h, hb=hb, x_b=x_b, qn=qn, kvn=kvn, q=q, k=k, v=v, crow=crow,
                              o_mla=o_mla, o_fox=o_fox, lse_mla=lse_mla, lse_fox=lse_fox, o_parts=o_parts, z=z, xh=xh, rstd=rstd,
                              w_in_t=w_in_t, uq_t=uq_t, ukv_t=ukv_t, w_out_t=w_out_t, qg=qg, kg=kg, bft=bft, g_ln=g_ln))
        else:
            w_in_p, w_in_t, w_out, w_out_t = _odd_weights(full["odd_w_in"][j], full["odd_w_out"][j])
            h = _mm(x_b, w_in_p, out_dtype=F32, name=ln + "_in")
            q, k, v = _odd_rope(h, cos_s, sin_s, name=ln + "_rope")
            kp = jnp.pad(k, ((WINDOW, 0), (0, 0)))
            vp = jnp.pad(v, ((WINDOW, 0), (0, 0)))
            o_t, lse = _swa_fwd(q, kp, vp, sink_tiles[j], name=ln + "_swa")
            g_ln, b_ln = odd_g_full[j][None], odd_b_full[j][None]
            x_n, x_nb, z, xh, rstd = _out_fwd([o_t], h, x_f, w_out, g_ln, b_ln, name=ln + "_out", o_t=True)
            saved.append(dict(h=h, x_b=x_b, q=q, kp=kp, vp=vp, lse=lse, o_t=o_t, o_parts=[o_t], z=z, xh=xh, rstd=rstd,
                              w_in_t=w_in_t, w_out_t=w_out_t, g_ln=g_ln))
        x_f, x_b = x_n, x_nb

    dxn, loss_tile = _loss_grad(x_f, tgt, name="loss")

    def pair_sums(group, tag):
        gp = [_chip_halves(n, g) for n, g in group.items()]
        from_sib = _swap_halves(gp, name="grad_swap_" + tag)
        return [_add_sibling(g, r, c_idx, chip_idx, name=f"grad_add_sibling_{n}_{tag}") for n, g, r in zip(group, gp, from_sib)]

    def chip_sums(group, pair, from_chips, tag):
        mine = [_add_chips(p[1], r, c_idx, name=f"grad_add_chips_{n}_{tag}") for n, p, r in zip(group, pair, from_chips)]
        return dict(zip(group, _join_halves(mine, name="grad_join_" + tag)))

    grads = {n: [None, None] for n in names}
    for layer in reversed(range(DEPTH)):
        j = layer // 2
        ln = f"L{layer}"
        sv = saved[layer]
        dr, dy, do, dgate, dg_ln, db_ln, *do_t = _out_bwd(dxn, sv["xh"], sv["rstd"], sv["g_ln"], sv["w_out_t"], sv["o_parts"], sv["h"],
                                                          name=ln + "_outb", o_t=layer % 2 == 1)
        dw_out = _mm(sv["z"].T, dy, out_dtype=F32, name=ln + "_dwout")
        if layer % 2 == 0:
            hb = sv["hb"]
            if layer == 0:
                early = {n: grads[n][1][None] for n in even_mats}
                early.update({n: jnp.stack(grads[n]) for n in odd_mats})
                early_pair = pair_sums(early, "early")
                dq, dk, dv, early_chips = _attn_bwd((sv["q"], 0), (sv["k"], 0), (sv["v"], 0), (sv["o_mla"], 0), (do, 0), sv["lse_mla"], None,
                                                    dk=HEAD_PAD, npairs=mla_pairs, scale=mla_scale, dq_dtype=F32, name=ln + "_mlab",
                                                    scatter=[p[0] for p in early_pair])
            else:
                dq, dk, dv = _attn_bwd((sv["q"], 0), (sv["k"], 0), (sv["v"], 0), (sv["o_mla"], 0), (do, 0), sv["lse_mla"], None,
                                       dk=HEAD_PAD, npairs=mla_pairs, scale=mla_scale, dq_dtype=F32, name=ln + "_mlab")
            dfq, dfk, dfv, dcrow, dccol = _attn_bwd((hb, E_FQ), (hb, E_FK), (hb, E_FV), (sv["o_fox"], 0), (do, MLA_HEADS * MLA_V),
                                                    sv["lse_fox"], sv["crow"], dk=FOX_DIM, npairs=fox_pairs,
                                                    scale=fox_scale, dq_dtype=BF16, name=ln + "_foxb")
            dcum = jnp.pad((dcrow[:, 0, :] + dccol[:, 0, :]).T, ((0, 0), (FL_LANE, LANES - FL_LANE - FOX_HEADS)))
            dlogf = _cumsum(dcum, reverse=True, name=ln + "_cumb")
            dh_small, dq_pre, dqg, dkg, dbf = _even_mid_bwd(
                sv["h"], dq, dk, dv, dlogf, sv["qg"], sv["kg"], sv["bft"], sv["uq_t"], sv["ukv_t"], cos_m, sin_m, name=ln + "_midb")
            duq_p = _mm(sv["qn"].T, dq_pre, out_dtype=F32, name=ln + "_dwuq")
            dkv_cat = jnp.concatenate([dk.astype(BF16), dv], axis=1)
            dukv_p = _mm(sv["kvn"].T, dkv_cat, out_dtype=F32, name=ln + "_dwukv")
            dh = jnp.concatenate([dgate, dfq, dfk, dfv, dh_small], axis=1)
            dw_in_p = _mm(sv["x_b"].T, dh, out_dtype=F32, name=ln + "_dwin")
            dw_in, duq, dukv = _even_weight_grads(dw_in_p, duq_p, dukv_p)
            for n, val in (("even_w_in", dw_in), ("even_w_uq", duq), ("even_w_ukv", dukv), ("even_w_out", dw_out),
                           ("even_q_norm", dqg[0]), ("even_kv_norm", dkg[0]), ("even_b_f", dbf[0, FL_LANE:FL_LANE + FOX_HEADS]),
                           ("even_ln_g", dg_ln[0]), ("even_ln_b", db_ln[0])):
                grads[n][j] = val
        else:
            dq_t, dkp, dvp, dsink = _swa_bwd(sv["q"], sv["kp"], sv["vp"], sv["kp"].T, sink_tiles[j], sv["o_t"], do, do_t[0], sv["lse"],
                                             name=ln + "_swab")
            dq_r, dk_r = _odd_rope_bwd(dq_t, dkp[WINDOW:], cos_s, sin_s, name=ln + "_ropeb")
            dh = jnp.concatenate([dgate, dq_r, dk_r, dvp[WINDOW:].astype(BF16)], axis=1)
            dw_in_p = _mm(sv["x_b"].T, dh, out_dtype=F32, name=ln + "_dwin")
            dw_in, dw_out = _odd_weight_grads(dw_in_p, dw_out)
            for n, val in (("odd_w_in", dw_in), ("odd_w_out", dw_out), ("odd_sinks", _interleave(dsink[None, :, 0, 0], False)[0]),
                           ("odd_ln_g", dg_ln[0]), ("odd_ln_b", db_ln[0])):
                grads[n][j] = val
        dxn = _mm(dh, sv["w_in_t"], out_dtype=F32, name=ln + "_dx", res=dr, res_scale=ALPHA)
    grad_x = dxn[None]
    late = {n: grads[n][0][None] for n in even_mats}
    late_pair = pair_sums(late, "late")
    late_sum = chip_sums(late, late_pair, _scatter_to_chips([p[0] for p in late_pair], name="grad_scatter_late"), "late")
    early_sum = chip_sums(early, early_pair, early_chips, "early")
    gshard = {n: jnp.concatenate([late_sum[n], early_sum[n]]) for n in even_mats}
    gshard.update({n: early_sum[n] for n in odd_mats})

    small = [n for n in names if n not in _SHARDED]
    grads = {n: jnp.stack(grads[n]) for n in small}
    sv_flat = jnp.concatenate([grads[n].reshape(-1) for n in small] + [loss_tile[0, :1]])
    sv_real = sv_flat.shape[0]
    sv_rows = -(-sv_real // (PACK_COLS * 8)) * 8
    sv_sum = _all_reduce_small(jnp.pad(sv_flat, (0, sv_rows * PACK_COLS - sv_real)).reshape(sv_rows, PACK_COLS), name="small_all_reduce").reshape(-1)
    off = 0
    for n in small:
        size = int(np.prod(grads[n].shape))
        gfull = sv_sum[off:off + size].reshape(grads[n].shape)
        off += size
        if n in ("odd_ln_g", "odd_ln_b"):
            gfull = lax.dynamic_slice_in_dim(gfull, chip * (D_MODEL // N_CHIPS), D_MODEL // N_CHIPS, axis=1)
        gshard[n] = gfull
    loss = sv_sum[off]

    deltas, new_m, new_v = {}, {}, {}
    for n in names:
        deltas[n], new_m[n], new_v[n] = _adamw(weights[n], gshard[n], mom_m[n], mom_v[n], name="adamw_" + n)
    return (loss, grad_x, *[gshard[n] for n in names], *[deltas[n] for n in names],
            *[new_m[n] for n in names], *[new_v[n] for n in names])
```

```python
import jax
import jax.numpy as jnp
import numpy as np
from jax import lax
from jax.experimental import pallas as pl
from jax.experimental.pallas import tpu as pltpu

F32 = jnp.float32
BF16 = jnp.bfloat16
MESH = pl.DeviceIdType.MESH

D_MODEL = 1024
DEPTH = 4
ROPE_THETA = 10000.0
MLA_HEADS, MLA_NOPE, MLA_ROPE, MLA_V = 8, 64, 32, 64
MLA_Q_RANK, MLA_KV_RANK = 256, 128
FOX_HEADS, FOX_DIM = 8, 64
SWA_HEADS, SWA_KV_HEADS, SWA_DIM, WINDOW = 16, 2, 64, 128
RMS_EPS = 1e-6
LN_EPS = 1e-5
ALPHA = (2 * DEPTH) ** 0.25
EVEN_IN = 2984
ODD_IN = 2304
ADAM_LR, ADAM_B1, ADAM_B2, ADAM_EPS, ADAM_WD, ADAM_STEP = 0.001, 0.9, 0.999, 1e-08, 0.01, 10

LANES = 128
HALF = LANES // 2
HEAD_PAD = 128
N_CHIPS = 4
N_DEV = 8
E_GATE, E_FQ, E_FK, E_FV, E_SMALL = 0, 1024, 1536, 2048, 2560
E_PAD_IN = 3072
KPE_LANE = 64
FL_LANE = 96
O_GATE, O_Q, O_K, O_V = 0, 1024, 2048, 2176

_ARB = "arbitrary"
_PAR = "parallel"


def _cparams(sem):
    return pltpu.CompilerParams(dimension_semantics=sem)


def _pick(n, cands):
    for c in cands:
        if n % c == 0:
            return c
    return n


RESIDENT_BYTES = 8 << 20


def _mm_resident(a, b, *, out_dtype, name, res, res_scale, also_bf16):
    m, k = a.shape
    _, n = b.shape
    tm = _pick(m, (256, 128))
    cn = _pick(n, (512, 384, 256, 128))
    dtypes = [out_dtype, BF16] if also_bf16 else [out_dtype]

    def body(*refs):
        a_ref, b_ref = refs[:2]
        r_ref = refs[2] if res is not None else None
        outs = refs[3 if res is not None else 2:]
        av = a_ref[...].astype(BF16)
        for c0 in range(0, n, cn):
            r = jnp.dot(av, b_ref[:, c0:c0 + cn].astype(BF16), preferred_element_type=F32)
            if res is not None:
                r = r + res_scale * r_ref[:, c0:c0 + cn]
            for o_ref in outs:
                o_ref[:, c0:c0 + cn] = r.astype(o_ref.dtype)

    rows = lambda w: pl.BlockSpec((tm, w), lambda i: (i, 0))
    in_specs = [rows(k), pl.BlockSpec((k, n), lambda i: (0, 0))] + ([rows(n)] if res is not None else [])
    out = pl.pallas_call(
        body, name=name, grid=(m // tm,), in_specs=in_specs, out_specs=[rows(n)] * len(dtypes),
        out_shape=[jax.ShapeDtypeStruct((m, n), d) for d in dtypes], compiler_params=_cparams((_PAR,)),
    )(*([a, b] + ([res] if res is not None else [])))
    return out if also_bf16 else out[0]


def _mm_left_resident(a, b, *, out_dtype, name):
    m, k = a.shape
    _, n = b.shape
    tn = _pick(n, (256, 128))

    def body(a_ref, b_ref, o_ref):
        o_ref[...] = jnp.dot(a_ref[...].astype(BF16), b_ref[...].astype(BF16), preferred_element_type=F32).astype(o_ref.dtype)

    return pl.pallas_call(
        body, name=name, grid=(n // tn,),
        in_specs=[pl.BlockSpec((m, k), lambda j: (0, 0)), pl.BlockSpec((k, tn), lambda j: (0, j))],
        out_specs=pl.BlockSpec((m, tn), lambda j: (0, j)), out_shape=jax.ShapeDtypeStruct((m, n), out_dtype),
        compiler_params=_cparams((_PAR,)),
    )(a, b)


def _mm(a, b, *, out_dtype, name, res=None, res_scale=1.0, also_bf16=False):
    m, k = a.shape
    _, n = b.shape
    if b.size * b.dtype.itemsize <= RESIDENT_BYTES:
        return _mm_resident(a, b, out_dtype=out_dtype, name=name, res=res, res_scale=res_scale, also_bf16=also_bf16)
    if a.size * a.dtype.itemsize <= RESIDENT_BYTES and res is None and not also_bf16:
        return _mm_left_resident(a, b, out_dtype=out_dtype, name=name)
    tm = _pick(m, (512, 256, 128))
    tn = _pick(n, (1024, 768, 512, 384, 256, 128))
    tk = _pick(k, (1024, 768, 512, 256, 128))
    nk = k // tk

    def body(*refs):
        refs = list(refs)
        a_ref, b_ref = refs[:2]
        r_ref = refs[2] if res is not None else None
        acc_ref = refs[-1]
        outs = refs[3 if res is not None else 2:-1]
        kk = pl.program_id(2)

        @pl.when(kk == 0)
        def _():
            acc_ref[...] = jnp.zeros_like(acc_ref)

        acc_ref[...] += jnp.dot(a_ref[...].astype(BF16), b_ref[...].astype(BF16), preferred_element_type=F32)

        @pl.when(kk == nk - 1)
        def _():
            r = acc_ref[...]
            if res is not None:
                r = r + res_scale * r_ref[...]
            for o_ref in outs:
                o_ref[...] = r.astype(o_ref.dtype)

    in_specs = [pl.BlockSpec((tm, tk), lambda i, j, kk: (i, kk)), pl.BlockSpec((tk, tn), lambda i, j, kk: (kk, j))]
    args = [a, b]
    if res is not None:
        in_specs.append(pl.BlockSpec((tm, tn), lambda i, j, kk: (i, j)))
        args.append(res)
    ospec = pl.BlockSpec((tm, tn), lambda i, j, kk: (i, j))
    dtypes = [out_dtype, BF16] if also_bf16 else [out_dtype]
    out = pl.pallas_call(
        body, name=name, grid=(m // tm, n // tn, nk), in_specs=in_specs,
        out_specs=[ospec] * len(dtypes), out_shape=[jax.ShapeDtypeStruct((m, n), d) for d in dtypes],
        scratch_shapes=[pltpu.VMEM((tm, tn), F32)],
        compiler_params=_cparams((_PAR, _PAR, _ARB)),
    )(*args)
    return out if also_bf16 else out[0]


def _rope_tile(t, cos, sin, half):
    w = t.shape[-1]
    lane = lax.broadcasted_iota(jnp.int32, t.shape, 1)
    first = (lane % (2 * half)) < half
    sw = jnp.where(first, pltpu.roll(t, w - half, 1), pltpu.roll(t, half, 1))
    return t * cos + sw * sin


def _sigmoid(x):
    return 1.0 / (1.0 + jnp.exp(-x))


def _lane_mask(shape, lo, hi):
    lane = lax.broadcasted_iota(jnp.int32, shape, 1)
    return (lane >= lo) & (lane < hi)


def _rms(x, g):
    r = lax.rsqrt(jnp.mean(x * x, axis=-1, keepdims=True) + RMS_EPS)
    u = x * r
    return u, r, u * g


def _rms_bwd(dy, u, r, g):
    dyg = dy * g
    dx = r * (dyg - u * jnp.mean(dyg * u, axis=-1, keepdims=True))
    return dx, jnp.sum(dy * u, axis=0, keepdims=True)


def _even_mid_fwd(h, qg, kg, bf_tile, w_uq, w_ukv, cos, sin, *, name):
    s = h.shape[0]
    tb = _pick(s, (512, 256, 128))
    nq = MLA_HEADS * HEAD_PAD

    def body(h_ref, qg_ref, kg_ref, bf_ref, wuq_ref, wukv_ref, cos_ref, sin_ref,
             q_ref, k_ref, v_ref, qn_ref, kvn_ref, lf_ref):
        hb = h_ref[...]
        cq, ckv, misc = hb[:, :MLA_Q_RANK], hb[:, MLA_Q_RANK:MLA_Q_RANK + MLA_KV_RANK], hb[:, MLA_Q_RANK + MLA_KV_RANK:]
        cs, sn = cos_ref[...], sin_ref[...]
        _, _, qn = _rms(cq, qg_ref[...])
        qn = qn.astype(BF16)
        qn_ref[...] = qn
        q = jnp.dot(qn, wuq_ref[...], preferred_element_type=F32)
        _, _, kvn = _rms(ckv, kg_ref[...])
        kvn = kvn.astype(BF16)
        kvn_ref[...] = kvn
        kv = jnp.dot(kvn, wukv_ref[...], preferred_element_type=F32)
        kpe = jnp.where(_lane_mask(misc.shape, KPE_LANE, KPE_LANE + MLA_ROPE), _rope_tile(misc, cs, sn, MLA_ROPE // 2), 0.0)
        for hd in range(MLA_HEADS):
            sl = slice(hd * HEAD_PAD, (hd + 1) * HEAD_PAD)
            q_ref[:, sl] = _rope_tile(q[:, sl], cs, sn, MLA_ROPE // 2).astype(BF16)
            k_ref[:, sl] = (kv[:, sl] + kpe).astype(BF16)
        v_ref[...] = kv[:, nq:].astype(BF16)
        xf = misc + bf_ref[...]
        logf = jnp.minimum(xf, 0.0) - jnp.log(1.0 + jnp.exp(-jnp.abs(xf)))
        lf_ref[...] = jnp.where(_lane_mask(misc.shape, FL_LANE, FL_LANE + FOX_HEADS), logf, 0.0)

    full = lambda a: pl.BlockSpec(a.shape, lambda i: (0,) * a.ndim)
    rows = lambda w, c=0: pl.BlockSpec((tb, w), lambda i: (i, c))
    return pl.pallas_call(
        body, name=name, grid=(s // tb,),
        in_specs=[rows(512, E_SMALL // 512), full(qg), full(kg), full(bf_tile), full(w_uq), full(w_ukv), rows(LANES), rows(LANES)],
        out_specs=[rows(nq), rows(nq), rows(MLA_HEADS * MLA_V), rows(MLA_Q_RANK), rows(MLA_KV_RANK), rows(LANES)],
        out_shape=[jax.ShapeDtypeStruct((s, nq), BF16), jax.ShapeDtypeStruct((s, nq), BF16),
                   jax.ShapeDtypeStruct((s, MLA_HEADS * MLA_V), BF16), jax.ShapeDtypeStruct((s, MLA_Q_RANK), BF16),
                   jax.ShapeDtypeStruct((s, MLA_KV_RANK), BF16), jax.ShapeDtypeStruct((s, LANES), F32)],
        compiler_params=_cparams((_PAR,)),
    )(h, qg, kg, bf_tile, w_uq, w_ukv, cos, sin)


def _even_mid_bwd(h, dq, dk, dv, dlogf, qg, kg, bf_tile, w_uq_t, w_ukv_t, cos, sin, *, name):
    s = h.shape[0]
    tb = _pick(s, (256, 128))
    nq = MLA_HEADS * HEAD_PAD

    def body(h_ref, dq_ref, dk_ref, dv_ref, dlf_ref, qg_ref, kg_ref, bf_ref, wuqt_ref, wukvt_ref, cos_ref, sin_ref,
             dh_ref, dqp_ref, dqg_ref, dkg_ref, dbf_ref):
        @pl.when(pl.program_id(0) == 0)
        def _():
            dqg_ref[...] = jnp.zeros_like(dqg_ref)
            dkg_ref[...] = jnp.zeros_like(dkg_ref)
            dbf_ref[...] = jnp.zeros_like(dbf_ref)

        hb = h_ref[...]
        cq, ckv, misc = hb[:, :MLA_Q_RANK], hb[:, MLA_Q_RANK:MLA_Q_RANK + MLA_KV_RANK], hb[:, MLA_Q_RANK + MLA_KV_RANK:]
        cs, sn = cos_ref[...], -sin_ref[...]
        dkpe = jnp.zeros(misc.shape, F32)
        for hd in range(MLA_HEADS):
            sl = slice(hd * HEAD_PAD, (hd + 1) * HEAD_PAD)
            dqp_ref[:, sl] = _rope_tile(dq_ref[:, sl], cs, sn, MLA_ROPE // 2).astype(BF16)
            dkpe = dkpe + dk_ref[:, sl]
        dqn = jnp.dot(dqp_ref[...], wuqt_ref[...], preferred_element_type=F32)
        uq, rq, _ = _rms(cq, qg_ref[...])
        dcq, dqg = _rms_bwd(dqn, uq, rq, qg_ref[...])
        dqg_ref[...] += dqg
        dkv = jnp.concatenate([dk_ref[...].astype(BF16), dv_ref[...]], axis=1)
        dkvn = jnp.dot(dkv, wukvt_ref[...], preferred_element_type=F32)
        uk, rk, _ = _rms(ckv, kg_ref[...])
        dckv, dkg = _rms_bwd(dkvn, uk, rk, kg_ref[...])
        dkg_ref[...] += dkg
        dmisc = jnp.where(_lane_mask(misc.shape, KPE_LANE, KPE_LANE + MLA_ROPE), _rope_tile(dkpe, cs, sn, MLA_ROPE // 2), 0.0)
        dfl = jnp.where(_lane_mask(misc.shape, FL_LANE, FL_LANE + FOX_HEADS), dlf_ref[...] * _sigmoid(-(misc + bf_ref[...])), 0.0)
        dbf_ref[...] += jnp.sum(dfl, axis=0, keepdims=True)
        dh_ref[:, :MLA_Q_RANK] = dcq.astype(BF16)
        dh_ref[:, MLA_Q_RANK:MLA_Q_RANK + MLA_KV_RANK] = dckv.astype(BF16)
        dh_ref[:, MLA_Q_RANK + MLA_KV_RANK:] = (dmisc + dfl).astype(BF16)

    full = lambda a: pl.BlockSpec(a.shape, lambda i: (0,) * a.ndim)
    rows = lambda w, c=0: pl.BlockSpec((tb, w), lambda i: (i, c))
    return pl.pallas_call(
        body, name=name, grid=(s // tb,),
        in_specs=[rows(512, E_SMALL // 512), rows(nq), rows(nq), rows(MLA_HEADS * MLA_V), rows(LANES), full(qg), full(kg),
                  full(bf_tile), full(w_uq_t), full(w_ukv_t), rows(LANES), rows(LANES)],
        out_specs=[rows(512), rows(nq), full(qg), full(kg), full(bf_tile)],
        out_shape=[jax.ShapeDtypeStruct((s, 512), BF16), jax.ShapeDtypeStruct((s, nq), BF16),
                   jax.ShapeDtypeStruct(qg.shape, F32), jax.ShapeDtypeStruct(kg.shape, F32),
                   jax.ShapeDtypeStruct(bf_tile.shape, F32)],
        compiler_params=_cparams((_ARB,)),
    )(h, dq, dk, dv, dlogf, qg, kg, bf_tile, w_uq_t, w_ukv_t, cos, sin)


def _cumsum(x, *, reverse, name):
    s = x.shape[0]
    tb = _pick(s, (512, 256, 128))
    nb = s // tb

    def body(x_ref, o_ref, carry_ref):
        @pl.when(pl.program_id(0) == 0)
        def _():
            carry_ref[...] = jnp.zeros_like(carry_ref)

        xv = x_ref[...]
        r = lax.broadcasted_iota(jnp.int32, (tb, tb), 0)
        c = lax.broadcasted_iota(jnp.int32, (tb, tb), 1)
        tri = jnp.where((c >= r) if reverse else (c <= r), 1.0, 0.0).astype(BF16)
        hi = xv.astype(BF16)
        r1 = xv - hi.astype(F32)
        mid = r1.astype(BF16)
        lo = (r1 - mid.astype(F32)).astype(BF16)
        cs = (jnp.dot(tri, hi, preferred_element_type=F32) + jnp.dot(tri, mid, preferred_element_type=F32)
              + jnp.dot(tri, lo, preferred_element_type=F32)) + carry_ref[...]
        o_ref[...] = cs
        carry_ref[...] = cs[0:1, :] if reverse else cs[tb - 1:tb, :]

    imap = (lambda i: (nb - 1 - i, 0)) if reverse else (lambda i: (i, 0))
    return pl.pallas_call(
        body, name=name, grid=(nb,), in_specs=[pl.BlockSpec((tb, LANES), imap)],
        out_specs=pl.BlockSpec((tb, LANES), imap), out_shape=jax.ShapeDtypeStruct(x.shape, F32),
        scratch_shapes=[pltpu.VMEM((1, LANES), F32)], compiler_params=_cparams((_ARB,)),
    )(x)


_NT = (((1,), (1,)), ((), ()))
_TN = (((0,), (0,)), ((), ()))


def _head_sel(x, h, dk):
    if dk == LANES:
        return x[:, h * LANES:(h + 1) * LANES]
    return jnp.where(_lane_mask(x.shape, h * HALF, (h + 1) * HALF), x, jnp.zeros_like(x))


def _head_other(x, h, dk):
    return x[:, h * LANES:(h + 1) * LANES] if dk == LANES else x


def _pair(parts, dk=HALF):
    if dk == LANES:
        return jnp.concatenate(parts, axis=1)
    return jnp.where(_lane_mask(parts[0].shape, 0, HALF), parts[0], parts[1])


def _attn_fwd(q, k, v, crow, *, dk, npairs, scale, name, gather=()):
    (qa, qo), (ka, ko), (va, vo) = q, k, v
    s = qa.shape[0]
    wq = 2 * dk
    t = _pick(s, (512, 256, 128))
    nb = s // t
    bias = crow is not None
    ng = len(gather)

    def body(*refs):
        n_in = 3 + bias
        q_ref, k_ref, v_ref = refs[:3]
        cr_ref = refs[3] if bias else None
        o_ref, lse_ref = refs[n_in + ng:n_in + ng + 2]
        qi = pl.program_id(1)
        if ng:
            phases = _gather_phases([a.shape for a in gather], refs[n_in:n_in + ng], refs[n_in + ng + 2:n_in + 2 * ng + 2],
                                    *refs[n_in + 2 * ng + 2:])
            pair = pl.program_id(0)
            pl.when((pair == 0) & (qi == 0))(phases[0])
            pl.when((pair == npairs - 1) & (qi == 0))(phases[1])
        qb = q_ref[...]
        qs = [_head_sel(qb, h, dk) for h in range(2)]

        def scores(j):
            rows = pl.ds(pl.multiple_of(j * t, t), t)
            kb = k_ref[rows, :]
            out = []
            for h in range(2):
                sc = lax.dot_general(qs[h], _head_other(kb, h, dk), _NT, preferred_element_type=F32) * scale
                out.append(sc - cr_ref[h, j] if bias else sc)
            return tuple(out)

        def update(scs, j, state):
            rows = pl.ds(pl.multiple_of(j * t, t), t)
            vb = v_ref[rows, :]
            soft = []
            for h in range(2):
                m, l, _ = state[3 * h:3 * h + 3]
                sc = scs[h]
                m_new = jnp.maximum(m, jnp.max(sc, axis=1, keepdims=True))
                a = jnp.exp(m - m_new)
                p = jnp.exp(sc - m_new)
                soft.append((m_new, a * l + jnp.sum(p, axis=1, keepdims=True), a, p.astype(BF16)))
            new = []
            for h in range(2):
                m_new, l, a, p = soft[h]
                new += [m_new, l, a * state[3 * h + 2] + jnp.dot(p, vb, preferred_element_type=F32)]
            return tuple(new)

        one = (jnp.full((t, 1), -jnp.inf, F32), jnp.zeros((t, 1), F32), jnp.zeros((t, LANES), F32))
        state = lax.fori_loop(0, qi, lambda j, st: update(scores(j), j, st), one + one)
        row = lax.broadcasted_iota(jnp.int32, (t, t), 0)
        col = lax.broadcasted_iota(jnp.int32, (t, t), 1)
        diag = tuple(jnp.where(col <= row, sc, -jnp.inf) for sc in scores(qi))
        state = update(diag, qi, state)
        outs = []
        for h in range(2):
            m, l, acc = state[3 * h:3 * h + 3]
            outs.append(acc / l)
            lse_ref[h] = m + jnp.log(l)
        o_ref[...] = _pair(outs)
        if ng:
            pl.when((pair == npairs - 1) & (qi == nb - 1))(phases[2])

    in_specs = [pl.BlockSpec((t, wq), lambda p, i: (i, qo // wq + p)), pl.BlockSpec((s, wq), lambda p, i: (0, ko // wq + p)),
                pl.BlockSpec((s, LANES), lambda p, i: (0, vo // LANES + p))]
    args = [qa, ka, va]
    if bias:
        in_specs.append(pl.BlockSpec((2, nb, 1, t), lambda p, i: (p, 0, 0, 0)))
        args.append(crow.reshape(2 * npairs, nb, 1, t))
    out = pl.pallas_call(
        body, name=name, grid=(npairs, nb), in_specs=in_specs + [_ANY] * ng,
        out_specs=[pl.BlockSpec((t, LANES), lambda p, i: (i, p)), pl.BlockSpec((2, t, 1), lambda p, i: (p, i, 0))] + [_ANY] * ng,
        out_shape=[jax.ShapeDtypeStruct((s, npairs * LANES), F32), jax.ShapeDtypeStruct((2 * npairs, s, 1), F32)]
        + [jax.ShapeDtypeStruct((N_CHIPS,) + a.shape, a.dtype) for a in gather],
        scratch_shapes=_sem_pairs(GATHER_COPIES * ng) if ng else [],
        compiler_params=_cparams((_ARB, _ARB) if ng else (_PAR, _PAR)),
    )(*args, *gather)
    return (out[0], out[1], out[2:]) if ng else (out[0], out[1])


def _attn_bwd(q, k, v, o, do, lse, crow, *, dk, npairs, scale, dq_dtype, name, scatter=()):
    (qa, qo), (ka, ko), (va, vo), (oa, oo), (da, do_o) = q, k, v, o, do
    s = qa.shape[0]
    wq = 2 * dk
    t = _pick(s, (512, 256, 128))
    nb = s // t
    bias = crow is not None
    ns = len(scatter)

    def body(*refs):
        n_in, n_out = 6 + bias, 3 + 2 * bias
        q_ref, k_ref, v_ref, o_ref, do_ref, lse_ref = refs[:6]
        outs = refs[n_in + ns:n_in + ns + n_out]
        scr = refs[n_in + 2 * ns + n_out:]
        dq_ref, dk_ref, dv_ref = outs[:3]
        dq_s, dk_s, dv_s = scr[:3]
        if bias:
            cr_ref, dc_ref, dcc_ref, dc_s = refs[6], outs[3], outs[4], scr[3]
        ki, qi = pl.program_id(1), pl.program_id(2)
        if ns:
            send_off, finish = _scatter_phases(ns, refs[n_in:n_in + ns], refs[n_in + ns + n_out:n_in + 2 * ns + n_out], *scr[3 + bias:])
            pair = pl.program_id(0)
            pl.when((pair == 0) & (ki == 0) & (qi == 0))(send_off)

        @pl.when((ki == 0) & (qi == 0))
        def _():
            dq_s[...] = jnp.zeros_like(dq_s)
            if bias:
                dcc_ref[...] = jnp.zeros_like(dcc_ref)

        @pl.when(qi == ki)
        def _():
            dk_s[...] = jnp.zeros_like(dk_s)
            dv_s[...] = jnp.zeros_like(dv_s)
            if bias:
                dc_s[...] = jnp.zeros_like(dc_s)

        def block(on_diagonal):
            qb, kb, vb, dob, ob = q_ref[...], k_ref[...], v_ref[...], do_ref[...], o_ref[...]
            rows = pl.ds(pl.multiple_of(qi * t, t), t)
            row = lax.broadcasted_iota(jnp.int32, (t, t), 0) + qi * t
            col = lax.broadcasted_iota(jnp.int32, (t, t), 1) + ki * t
            fed = []
            for h in range(2):
                kh, doh = _head_other(kb, h, dk), _head_sel(dob, h, HALF)
                sc = lax.dot_general(_head_sel(qb, h, dk), kh, _NT, preferred_element_type=F32)
                dp = lax.dot_general(doh, vb, _NT, preferred_element_type=F32)
                fed.append((kh, doh, sc, dp))
            mid = []
            for h in range(2):
                kh, doh, sc, dp = fed[h]
                sc = sc * scale
                if bias:
                    sc = sc - cr_ref[h]
                if on_diagonal:
                    sc = jnp.where(col <= row, sc, -jnp.inf)
                p = jnp.exp(sc - lse_ref[h])
                delta = jnp.sum(doh.astype(F32) * ob, axis=1, keepdims=True)
                ds = p * (dp - delta)
                mid.append((kh, p.astype(BF16), (ds * scale).astype(BF16)))
            dq_parts, dk_parts, dv_parts = [], [], []
            for h in range(2):
                kh, pb, dsb = mid[h]
                dv_parts.append(lax.dot_general(pb, dob, _TN, preferred_element_type=F32))
                dk_parts.append(lax.dot_general(dsb, _head_other(qb, h, dk), _TN, preferred_element_type=F32))
                dq_parts.append(jnp.dot(dsb, kh, preferred_element_type=F32))
                if bias:
                    dc_s[h] -= jnp.sum(dsb.astype(F32), axis=0, keepdims=True) * (1.0 / scale)
                    ones = jnp.ones((8, t), BF16)
                    dcc_ref[h, :, rows] += lax.dot_general(ones, dsb, _NT, preferred_element_type=F32)[0:1] * (1.0 / scale)
            dv_s[...] += _pair(dv_parts)
            dk_s[...] += _pair(dk_parts, dk)
            dq_s[rows, :] += _pair(dq_parts, dk)

        pl.when(qi > ki)(lambda: block(False))
        pl.when(qi == ki)(lambda: block(True))

        @pl.when(qi == nb - 1)
        def _():
            dk_ref[...] = dk_s[...].astype(dk_ref.dtype)
            dv_ref[...] = dv_s[...].astype(dv_ref.dtype)
            if bias:
                dc_ref[...] = dc_s[...]

        @pl.when((ki == nb - 1) & (qi == nb - 1))
        def _():
            dq_ref[...] = dq_s[...].astype(dq_ref.dtype)

        if ns:
            pl.when((pair == npairs - 1) & (ki == nb - 1) & (qi == nb - 1))(finish)

    qrow = lambda p, j, i: jnp.maximum(i, j)
    in_specs = [pl.BlockSpec((t, wq), lambda p, j, i: (qrow(p, j, i), qo // wq + p)),
                pl.BlockSpec((t, wq), lambda p, j, i: (j, ko // wq + p)),
                pl.BlockSpec((t, LANES), lambda p, j, i: (j, vo // LANES + p)),
                pl.BlockSpec((t, LANES), lambda p, j, i: (qrow(p, j, i), oo // LANES + p)),
                pl.BlockSpec((t, LANES), lambda p, j, i: (qrow(p, j, i), do_o // LANES + p)),
                pl.BlockSpec((2, t, 1), lambda p, j, i: (p, qrow(p, j, i), 0))]
    args = [qa, ka, va, oa, da, lse]
    out_specs = [pl.BlockSpec((s, wq), lambda p, j, i: (0, p)), pl.BlockSpec((t, wq), lambda p, j, i: (j, p)),
                 pl.BlockSpec((t, LANES), lambda p, j, i: (j, p))]
    out_shape = [jax.ShapeDtypeStruct((s, npairs * wq), dq_dtype), jax.ShapeDtypeStruct((s, npairs * wq), dq_dtype),
                 jax.ShapeDtypeStruct((s, npairs * LANES), BF16)]
    scratch = [pltpu.VMEM((s, wq), F32), pltpu.VMEM((t, wq), F32), pltpu.VMEM((t, LANES), F32)]
    if bias:
        in_specs.append(pl.BlockSpec((2, 1, t), lambda p, j, i: (p, 0, j)))
        args.append(crow)
        out_specs += [pl.BlockSpec((2, 1, t), lambda p, j, i: (p, 0, j)), pl.BlockSpec((2, 1, s), lambda p, j, i: (p, 0, 0))]
        out_shape += [jax.ShapeDtypeStruct((2 * npairs, 1, s), F32), jax.ShapeDtypeStruct((2 * npairs, 1, s), F32)]
        scratch.append(pltpu.VMEM((2, 1, t), F32))
    n_out = len(out_shape)
    out = pl.pallas_call(
        body, name=name, grid=(npairs, nb, nb), in_specs=in_specs + [_ANY] * ns, out_specs=out_specs + [_ANY] * ns,
        out_shape=out_shape + [jax.ShapeDtypeStruct((3,) + a.shape[1:], a.dtype) for a in scatter],
        scratch_shapes=scratch + (_sem_pairs(3 * ns) if ns else []),
        compiler_params=_cparams((_ARB if ns else _PAR, _ARB, _ARB)),
    )(*args, *scatter)
    return (*out[:n_out], out[n_out:]) if ns else out


def _sub_mask(shape, lo, hi):
    sub = lax.broadcasted_iota(jnp.int32, shape, 0)
    return (sub >= lo) & (sub < hi)


def _pair_rows(parts):
    return jnp.where(_sub_mask(parts[0].shape, 0, HALF), parts[0], parts[1])


def _swa_valid(start, t):
    krow = lax.broadcasted_iota(jnp.int32, (t + WINDOW, t), 0)
    qcol = lax.broadcasted_iota(jnp.int32, (t + WINDOW, t), 1)
    diff = qcol - krow + WINDOW
    return (diff >= 0) & (diff < WINDOW) & (krow + start >= WINDOW)


def _swa_tiles(s):
    t = _pick(s, (256, 128))
    return t, _pick(s // t, (4, 2, 1))


def _swa_fwd(q, kp, vp, sink, *, name):
    s = q.shape[0]
    npairs = SWA_HEADS // 2
    t, nt = _swa_tiles(s)
    scale = SWA_DIM ** -0.5

    def body(q_ref, k_ref, v_ref, sk_ref, o_ref, lse_ref):
        sts, vws = [], []
        for u in range(nt):
            start = pl.multiple_of((pl.program_id(1) * nt + u) * t, t)
            kw = k_ref[pl.ds(start, t + WINDOW), :]
            vws.append(v_ref[pl.ds(start, t + WINDOW), :])
            qb = q_ref[u * t:(u + 1) * t, :]
            valid = _swa_valid(start, t)
            for h in range(2):
                st = lax.dot_general(kw, _head_sel(qb, h, HALF), _NT, preferred_element_type=F32) * scale
                sts.append(jnp.where(valid, st, -jnp.inf))
        pts = []
        for u in range(nt):
            for h in range(2):
                st = sts[2 * u + h]
                snk = sk_ref[h][:, 0:1]
                m = jnp.maximum(jnp.max(st, axis=0, keepdims=True), snk)
                e = jnp.exp(st - m)
                l = jnp.sum(e, axis=0, keepdims=True) + jnp.exp(snk - m)
                pts.append((e * (1.0 / l)).astype(BF16))
                lse_ref[h, :, u * t:(u + 1) * t] = m + jnp.log(l)
        for u in range(nt):
            outs = [lax.dot_general(vws[u], pts[2 * u + h], _TN, preferred_element_type=F32) for h in range(2)]
            o_ref[:, u * t:(u + 1) * t] = _pair_rows(outs)

    kvspec = pl.BlockSpec((s + WINDOW, LANES), lambda p, i: (0, 0))
    return pl.pallas_call(
        body, name=name, grid=(npairs, s // (t * nt)),
        in_specs=[pl.BlockSpec((t * nt, LANES), lambda p, i: (i, p)), kvspec, kvspec, pl.BlockSpec((2, 1, LANES), lambda p, i: (p, 0, 0))],
        out_specs=[pl.BlockSpec((LANES, t * nt), lambda p, i: (p, i)), pl.BlockSpec((2, 1, t * nt), lambda p, i: (p, 0, i))],
        out_shape=[jax.ShapeDtypeStruct((npairs * LANES, s), F32), jax.ShapeDtypeStruct((SWA_HEADS, 1, s), F32)],
        compiler_params=_cparams((_PAR, _PAR)),
    )(q, kp, vp, sink)


def _swa_bwd(q, kp, vp, kpt, sink, ot, do, dot, lse, *, name):
    s = q.shape[0]
    npairs = SWA_HEADS // 2
    t, nt = _swa_tiles(s)
    scale = SWA_DIM ** -0.5

    def body(q_ref, k_ref, v_ref, kt_ref, sk_ref, ot_ref, do_ref, dot_ref, lse_ref, dq_ref, dk_ref, dv_ref, dsk_ref):
        pp, i = pl.program_id(0), pl.program_id(1)

        @pl.when((pp == 0) & (i == 0))
        def _():
            dk_ref[...] = jnp.zeros_like(dk_ref)
            dv_ref[...] = jnp.zeros_like(dv_ref)

        @pl.when(i == 0)
        def _():
            dsk_ref[...] = jnp.zeros_like(dsk_ref)

        fed = []
        for u in range(nt):
            cols = slice(u * t, (u + 1) * t)
            start = pl.multiple_of((i * nt + u) * t, t)
            win = pl.ds(start, t + WINDOW)
            qb, dob = q_ref[cols, :], do_ref[cols, :]
            kw, vw = k_ref[win, :], v_ref[win, :]
            valid = _swa_valid(start, t)
            for h in range(2):
                st = lax.dot_general(kw, _head_sel(qb, h, HALF), _NT, preferred_element_type=F32)
                dpt = lax.dot_general(vw, _head_sel(dob, h, HALF), _NT, preferred_element_type=F32)
                fed.append((cols, win, qb, dob, valid, st, dpt))
        mid = []
        for u in range(nt):
            cols = fed[2 * u][0]
            prod = dot_ref[:, cols].astype(F32) * ot_ref[:, cols]
            for h in range(2):
                valid, st, dpt = fed[2 * u + h][4:]
                lse_b = lse_ref[h, :, cols]
                pt = jnp.exp(jnp.where(valid, st * scale, -jnp.inf) - lse_b)
                delta = jnp.sum(jnp.where(_sub_mask(prod.shape, h * HALF, (h + 1) * HALF), prod, 0.0), axis=0, keepdims=True)
                dst = pt * (dpt - delta)
                psink = jnp.exp(sk_ref[h][:, 0:1] - lse_b)
                dsk_ref[h] += jnp.broadcast_to(-jnp.sum(psink * delta, axis=1, keepdims=True), (1, LANES))
                mid.append((pt.astype(BF16), (dst * scale).astype(BF16)))
        for u in range(nt):
            cols, win, qb, dob = fed[2 * u][:4]
            ktw = kt_ref[:, win]
            dq_parts = [jnp.dot(ktw, mid[2 * u + h][1], preferred_element_type=F32) for h in range(2)]
            dk_parts = [jnp.dot(mid[2 * u + h][1], qb, preferred_element_type=F32) for h in range(2)]
            dv_parts = [jnp.dot(mid[2 * u + h][0], dob, preferred_element_type=F32) for h in range(2)]
            dq_ref[:, cols] = _pair_rows(dq_parts)
            dk_ref[win, :] += _pair(dk_parts)
            dv_ref[win, :] += _pair(dv_parts)

    tile = pl.BlockSpec((t * nt, LANES), lambda p, i: (i, p))
    ttile = pl.BlockSpec((LANES, t * nt), lambda p, i: (p, i))
    kvspec = pl.BlockSpec((s + WINDOW, LANES), lambda p, i: (0, 0))
    ktspec = pl.BlockSpec((LANES, s + WINDOW), lambda p, i: (0, 0))
    skspec = pl.BlockSpec((2, 1, LANES), lambda p, i: (p, 0, 0))
    return pl.pallas_call(
        body, name=name, grid=(npairs, s // (t * nt)),
        in_specs=[tile, kvspec, kvspec, ktspec, skspec, ttile, tile, ttile, pl.BlockSpec((2, 1, t * nt), lambda p, i: (p, 0, i))],
        out_specs=[ttile, kvspec, kvspec, skspec],
        out_shape=[jax.ShapeDtypeStruct((npairs * LANES, s), F32), jax.ShapeDtypeStruct((s + WINDOW, LANES), F32),
                   jax.ShapeDtypeStruct((s + WINDOW, LANES), F32), jax.ShapeDtypeStruct((SWA_HEADS, 1, LANES), F32)],
        compiler_params=_cparams((_ARB, _ARB)),
    )(q, kp, vp, kpt, sink, ot, do, dot, lse)


def _odd_rope(h, cos, sin, *, name):
    s = h.shape[0]
    tb = _pick(s, (512, 256, 128))
    nq = SWA_HEADS * SWA_DIM

    def body(q_ref, kv_ref, cos_ref, sin_ref, qo_ref, ko_ref, vo_ref):
        cs, sn = cos_ref[...], sin_ref[...]
        for j in range(nq // LANES):
            sl = slice(j * LANES, (j + 1) * LANES)
            qo_ref[:, sl] = _rope_tile(q_ref[:, sl], cs, sn, SWA_DIM // 2).astype(BF16)
        ko_ref[...] = _rope_tile(kv_ref[:, :LANES], cs, sn, SWA_DIM // 2).astype(BF16)
        vo_ref[...] = kv_ref[:, LANES:].astype(BF16)

    rows = lambda w, c=0: pl.BlockSpec((tb, w), lambda i: (i, c))
    return pl.pallas_call(
        body, name=name, grid=(s // tb,),
        in_specs=[rows(nq, O_Q // nq), rows(2 * LANES, O_K // (2 * LANES)), rows(LANES), rows(LANES)],
        out_specs=[rows(nq), rows(LANES), rows(LANES)],
        out_shape=[jax.ShapeDtypeStruct((s, nq), BF16), jax.ShapeDtypeStruct((s, LANES), BF16), jax.ShapeDtypeStruct((s, LANES), BF16)],
        compiler_params=_cparams((_PAR,)),
    )(h, h, cos, sin)


def _odd_rope_bwd(dq_t, dk, cos, sin, *, name):
    s = dk.shape[0]
    tb = _pick(s, (256, 128))
    nq = SWA_HEADS * SWA_DIM

    def body(dq_ref, dk_ref, cos_ref, sin_ref, qo_ref, ko_ref):
        cs, sn = cos_ref[...], -sin_ref[...]
        dq = dq_ref[...].T
        for j in range(nq // LANES):
            sl = slice(j * LANES, (j + 1) * LANES)
            qo_ref[:, sl] = _rope_tile(dq[:, sl], cs, sn, SWA_DIM // 2).astype(BF16)
        ko_ref[...] = _rope_tile(dk_ref[...], cs, sn, SWA_DIM // 2).astype(BF16)

    rows = lambda w: pl.BlockSpec((tb, w), lambda i: (i, 0))
    return pl.pallas_call(
        body, name=name, grid=(s // tb,),
        in_specs=[pl.BlockSpec((nq, tb), lambda i: (0, i)), rows(LANES), rows(LANES), rows(LANES)],
        out_specs=[rows(nq), rows(LANES)],
        out_shape=[jax.ShapeDtypeStruct((s, nq), BF16), jax.ShapeDtypeStruct((s, LANES), BF16)],
        compiler_params=_cparams((_PAR,)),
    )(dq_t, dk, cos, sin)


def _mixer_out(o_refs, o_t):
    if o_t:
        return o_refs[0][...].T
    return o_refs[0][...] if len(o_refs) == 1 else jnp.concatenate([r[...] for r in o_refs], axis=1)


def _mixer_specs(o_parts, o_t, tb):
    if o_t:
        return [pl.BlockSpec((D_MODEL, tb), lambda i: (0, i))]
    return [pl.BlockSpec((tb, a.shape[1]), lambda i: (i, 0)) for a in o_parts]


def _out_fwd(o_parts, h, x, w_out, g, b, *, name, o_t=False):
    s = x.shape[0]
    tb = _pick(s, (256, 128))
    n_o = len(o_parts)

    def body(*refs):
        o_refs = refs[:n_o]
        gate_ref, x_ref, w_ref, g_ref, b_ref, xn_ref, xb_ref, z_ref, xh_ref, rs_ref = refs[n_o:]
        gate = gate_ref[...]
        o = _mixer_out(o_refs, o_t)
        z = (o * (gate * _sigmoid(gate))).astype(BF16)
        z_ref[...] = z
        r = ALPHA * x_ref[...] + jnp.dot(z, w_ref[...], preferred_element_type=F32)
        mu = jnp.mean(r, axis=-1, keepdims=True)
        rc = r - mu
        rstd = lax.rsqrt(jnp.mean(rc * rc, axis=-1, keepdims=True) + LN_EPS)
        xh = rc * rstd
        xn = xh * g_ref[...] + b_ref[...]
        xh_ref[...] = xh
        rs_ref[...] = rstd
        xn_ref[...] = xn
        xb_ref[...] = xn.astype(BF16)

    rows = lambda w: pl.BlockSpec((tb, w), lambda i: (i, 0))
    full = lambda a: pl.BlockSpec(a.shape, lambda i: (0,) * a.ndim)
    return pl.pallas_call(
        body, name=name, grid=(s // tb,),
        in_specs=_mixer_specs(o_parts, o_t, tb) + [rows(D_MODEL), rows(D_MODEL), full(w_out), full(g), full(b)],
        out_specs=[rows(D_MODEL), rows(D_MODEL), rows(D_MODEL), rows(D_MODEL), rows(1)],
        out_shape=[jax.ShapeDtypeStruct((s, D_MODEL), F32), jax.ShapeDtypeStruct((s, D_MODEL), BF16),
                   jax.ShapeDtypeStruct((s, D_MODEL), BF16), jax.ShapeDtypeStruct((s, D_MODEL), F32),
                   jax.ShapeDtypeStruct((s, 1), F32)],
        compiler_params=_cparams((_PAR,)),
    )(*o_parts, h, x, w_out, g, b)


def _out_bwd(dxn, xh, rstd, g, w_out_t, o_parts, h, *, name, o_t=False):
    s = dxn.shape[0]
    tb = _pick(s, (256, 128))
    n_o = len(o_parts)

    def body(*refs):
        dxn_ref, xh_ref, rs_ref, g_ref, wt_ref = refs[:5]
        o_refs = refs[5:5 + n_o]
        gate_ref, dr_ref, dy_ref, do_ref, dgate_ref, dg_ref, db_ref = refs[5 + n_o:12 + n_o]

        @pl.when(pl.program_id(0) == 0)
        def _():
            dg_ref[...] = jnp.zeros_like(dg_ref)
            db_ref[...] = jnp.zeros_like(db_ref)

        dxn_b, xh_b = dxn_ref[...], xh_ref[...]
        dg_ref[...] += jnp.sum(dxn_b * xh_b, axis=0, keepdims=True)
        db_ref[...] += jnp.sum(dxn_b, axis=0, keepdims=True)
        dxh = dxn_b * g_ref[...]
        dr = rs_ref[...] * (dxh - jnp.mean(dxh, axis=-1, keepdims=True) - xh_b * jnp.mean(dxh * xh_b, axis=-1, keepdims=True))
        dr_ref[...] = dr
        dy = dr.astype(BF16)
        dy_ref[...] = dy
        dz = jnp.dot(dy, wt_ref[...], preferred_element_type=F32)
        gate = gate_ref[...]
        sg = _sigmoid(gate)
        o = _mixer_out(o_refs, o_t)
        do = dz * (gate * sg)
        do_ref[...] = do.astype(BF16)
        if o_t:
            refs[12 + n_o][...] = do.T.astype(BF16)
        dgate_ref[...] = (dz * o * (sg * (1.0 + gate * (1.0 - sg)))).astype(BF16)

    rows = lambda w: pl.BlockSpec((tb, w), lambda i: (i, 0))
    full = lambda a: pl.BlockSpec(a.shape, lambda i: (0,) * a.ndim)
    cols = pl.BlockSpec((D_MODEL, tb), lambda i: (0, i))
    return pl.pallas_call(
        body, name=name, grid=(s // tb,),
        in_specs=[rows(D_MODEL), rows(D_MODEL), rows(1), full(g), full(w_out_t)] + _mixer_specs(o_parts, o_t, tb) + [rows(D_MODEL)],
        out_specs=[rows(D_MODEL), rows(D_MODEL), rows(D_MODEL), rows(D_MODEL), full(g), full(g)] + [cols] * o_t,
        out_shape=[jax.ShapeDtypeStruct((s, D_MODEL), F32), jax.ShapeDtypeStruct((s, D_MODEL), BF16),
                   jax.ShapeDtypeStruct((s, D_MODEL), BF16), jax.ShapeDtypeStruct((s, D_MODEL), BF16),
                   jax.ShapeDtypeStruct(g.shape, F32), jax.ShapeDtypeStruct(g.shape, F32)]
        + [jax.ShapeDtypeStruct((D_MODEL, s), BF16)] * o_t,
        compiler_params=_cparams((_ARB,)),
    )(dxn, xh, rstd, g, w_out_t, *o_parts, h)


def _loss_grad(y, target, *, name):
    s, d = y.shape
    tb = _pick(s, (512, 256, 128))

    def body(y_ref, t_ref, dy_ref, l_ref):
        @pl.when(pl.program_id(0) == 0)
        def _():
            l_ref[...] = jnp.zeros_like(l_ref)

        err = y_ref[...] - t_ref[...]
        dy_ref[...] = err * (1.0 / d)
        per_tok = jnp.mean(err * err, axis=-1, keepdims=True)
        l_ref[...] += 0.5 * jnp.sum(per_tok, axis=0, keepdims=True)

    rows = pl.BlockSpec((tb, d), lambda i: (i, 0))
    return pl.pallas_call(
        body, name=name, grid=(s // tb,), in_specs=[rows, rows],
        out_specs=[rows, pl.BlockSpec((8, LANES), lambda i: (0, 0))],
        out_shape=[jax.ShapeDtypeStruct((s, d), F32), jax.ShapeDtypeStruct((8, LANES), F32)],
        compiler_params=_cparams((_ARB,)),
    )(y, target)


def _adamw(w, g, m, v, *, name):
    shape = w.shape
    w3, g3, m3, v3 = (a.reshape((1,) * (3 - a.ndim) + a.shape) for a in (w, g, m, v))
    a0, a1, a2 = w3.shape
    tb = _pick(a1, (256, 128)) if a1 % 8 == 0 else a1
    c1 = 1.0 - ADAM_B1 ** ADAM_STEP
    c2 = 1.0 - ADAM_B2 ** ADAM_STEP

    def body(w_ref, g_ref, m_ref, v_ref, d_ref, mo_ref, vo_ref):
        gg = g_ref[...]
        mn = ADAM_B1 * m_ref[...] + (1.0 - ADAM_B1) * gg
        vn = ADAM_B2 * v_ref[...] + (1.0 - ADAM_B2) * (gg * gg)
        mo_ref[...] = mn
        vo_ref[...] = vn
        d_ref[...] = -ADAM_LR * ((mn / c1) / (jnp.sqrt(vn / c2) + ADAM_EPS) + ADAM_WD * w_ref[...])

    spec = pl.BlockSpec((1, tb, a2), lambda i, j: (i, j, 0))
    outs = pl.pallas_call(
        body, name=name, grid=(a0, a1 // tb), in_specs=[spec] * 4, out_specs=[spec] * 3,
        out_shape=[jax.ShapeDtypeStruct(w3.shape, F32)] * 3, compiler_params=_cparams((_PAR, _PAR)),
    )(w3, g3, m3, v3)
    return tuple(a.reshape(shape) for a in outs)


def _place():
    x, y, c = lax.axis_index("x"), lax.axis_index("y"), lax.axis_index("c")
    return x, y, c, [(1 - x, y), (x, 1 - y), (1 - x, 1 - y)]


_ANY = pl.BlockSpec(memory_space=pl.ANY)


def _sem_pairs(n):
    return [pltpu.SemaphoreType.DMA((n,)), pltpu.SemaphoreType.DMA((n,))]


GATHER_COPIES = 7


def _gather_phases(shapes, srcs, outs, send_sems, recv_sems):
    nt = len(shapes)
    x, y, c, chips = _place()
    me = 2 * x + y
    sib = (x, y, 1 - c)

    def half(t, chip, hc):
        rh = shapes[t][1] // 2
        return outs[t].at[chip, :, pl.ds(hc * rh, rh), :]

    def copy(t, kk, s_ref, d_ref, to):
        return pltpu.make_async_remote_copy(src_ref=s_ref, dst_ref=d_ref, send_sem=send_sems.at[GATHER_COPIES * t + kk],
                                            recv_sem=recv_sems.at[GATHER_COPIES * t + kk], device_id=to, device_id_type=MESH)

    def first_hop(t, j):
        rh = shapes[t][1] // 2
        cx, cy = chips[j]
        return copy(t, j, srcs[t].at[:, pl.ds(c * rh, rh), :], half(t, me, c), (cx, cy, c))

    def passed(t, j):
        cx, cy = chips[j]
        landed = half(t, 2 * cx + cy, c)
        return copy(t, 3 + j, landed, landed, sib)

    def own(t):
        return copy(t, 6, srcs[t], outs[t].at[me], sib)

    pairs = [(t, j) for j in range(3) for t in range(nt)]

    def start():
        for t, j in pairs:
            first_hop(t, j).start()
        for t in range(nt):
            own(t).start()

    def pass_on():
        for t, j in pairs:
            cx, cy = chips[j]
            landed = half(t, 2 * cx + cy, c)
            copy(t, j, landed, landed, (cx, cy, c)).wait_recv()
            passed(t, j).start()

    def finish():
        for t, j in pairs:
            cx, cy = chips[j]
            theirs = half(t, 2 * cx + cy, 1 - c)
            copy(t, 3 + j, theirs, theirs, sib).wait_recv()
        for t in range(nt):
            own(t).wait()
        for t, j in pairs:
            first_hop(t, j).wait_send()
            passed(t, j).wait_send()

    return start, pass_on, finish


def _gather_chip_shards(tensors, *, name):
    nt = len(tensors)

    def body(*refs):
        for phase in _gather_phases([a.shape for a in tensors], refs[:nt], refs[nt:2 * nt], *refs[2 * nt:]):
            phase()

    return pl.pallas_call(
        body, name=name, in_specs=[_ANY] * nt, out_specs=[_ANY] * nt,
        out_shape=[jax.ShapeDtypeStruct((N_CHIPS,) + a.shape, a.dtype) for a in tensors],
        scratch_shapes=_sem_pairs(GATHER_COPIES * nt),
    )(*tensors)


def _swap_halves(gs, *, name):
    nt = len(gs)

    def body(*refs):
        srcs, outs = refs[:nt], refs[nt:2 * nt]
        send_sems, recv_sems = refs[2 * nt:]
        x, y, c, _ = _place()
        cps = [pltpu.make_async_remote_copy(src_ref=srcs[t].at[:, 1 - c], dst_ref=outs[t], send_sem=send_sems.at[t],
                                            recv_sem=recv_sems.at[t], device_id=(x, y, 1 - c), device_id_type=MESH)
               for t in range(nt)]
        for cp in cps:
            cp.start()
        for cp in cps:
            cp.wait()

    return pl.pallas_call(
        body, name=name, in_specs=[_ANY] * nt, out_specs=[_ANY] * nt,
        out_shape=[jax.ShapeDtypeStruct((N_CHIPS,) + g.shape[2:], g.dtype) for g in gs], scratch_shapes=_sem_pairs(nt),
    )(*gs)


def _scatter_phases(nt, srcs, outs, send_sems, recv_sems):
    x, y, c, chips = _place()

    def copies():
        return [pltpu.make_async_remote_copy(src_ref=srcs[t].at[2 * cx + cy], dst_ref=outs[t].at[j],
                                             send_sem=send_sems.at[3 * t + j], recv_sem=recv_sems.at[3 * t + j],
                                             device_id=(cx, cy, c), device_id_type=MESH)
                for j, (cx, cy) in enumerate(chips) for t in range(nt)]

    def start():
        for cp in copies():
            cp.start()

    def finish():
        for cp in copies():
            cp.wait()

    return start, finish


def _scatter_to_chips(blocks, *, name):
    nt = len(blocks)

    def body(*refs):
        for phase in _scatter_phases(nt, refs[:nt], refs[nt:2 * nt], *refs[2 * nt:]):
            phase()

    return pl.pallas_call(
        body, name=name, in_specs=[_ANY] * nt, out_specs=[_ANY] * nt,
        out_shape=[jax.ShapeDtypeStruct((3,) + a.shape[1:], a.dtype) for a in blocks], scratch_shapes=_sem_pairs(3 * nt),
    )(*blocks)


def _join_halves(bufs, *, name):
    nt = len(bufs)

    def body(*refs):
        srcs, outs = refs[:nt], refs[nt:2 * nt]
        send_sems, recv_sems = refs[2 * nt:]
        x, y, c, _ = _place()
        cps = []
        for t in range(nt):
            rh = bufs[t].shape[1] // 2
            rows = pl.ds(c * rh, rh)
            cps.append(pltpu.make_async_remote_copy(src_ref=srcs[t].at[:, rows, :], dst_ref=outs[t].at[:, rows, :],
                                                    send_sem=send_sems.at[t], recv_sem=recv_sems.at[t],
                                                    device_id=(x, y, 1 - c), device_id_type=MESH))
        for cp in cps:
            cp.start()
        for cp in cps:
            cp.wait()

    return pl.pallas_call(
        body, name=name, in_specs=[_ANY] * nt, out_specs=[_ANY] * nt,
        out_shape=[jax.ShapeDtypeStruct(b.shape, b.dtype) for b in bufs], scratch_shapes=_sem_pairs(nt),
        input_output_aliases={t: t for t in range(nt)},
    )(*bufs)


def _add_sibling(g, recv, cidx, chip_idx, *, name):
    _, _, na, rh, cdim = g.shape
    tb = _pick(rh, (256, 128, 64, 32, 16))

    def body(c_ref, k_ref, g_ref, r_ref, s_ref, o_ref):
        tot = g_ref[0, 0] + r_ref[0]
        s_ref[0] = tot.astype(BF16)

        @pl.when(pl.program_id(2) == k_ref[0])
        def _():
            o_ref[...] = tot

    return pl.pallas_call(
        body, name=name,
        grid_spec=pltpu.PrefetchScalarGridSpec(
            num_scalar_prefetch=2, grid=(na, rh // tb, N_CHIPS),
            in_specs=[pl.BlockSpec((1, 1, 1, tb, cdim), lambda a, i, k, c_ref, k_ref: (k, c_ref[0], a, i, 0)),
                      pl.BlockSpec((1, 1, tb, cdim), lambda a, i, k, c_ref, k_ref: (k, a, i, 0))],
            out_specs=[pl.BlockSpec((1, 1, tb, cdim), lambda a, i, k, c_ref, k_ref: (k, a, i, 0)),
                       pl.BlockSpec((1, tb, cdim), lambda a, i, k, c_ref, k_ref: (a, i, 0))]),
        out_shape=[jax.ShapeDtypeStruct((N_CHIPS, na, rh, cdim), BF16), jax.ShapeDtypeStruct((na, rh, cdim), F32)],
        compiler_params=_cparams((_PAR, _PAR, _ARB)),
    )(cidx, chip_idx, g, recv)


def _add_chips(own, recv, cidx, *, name):
    na, rh, cdim = own.shape
    tb = _pick(rh, (256, 128, 64, 32, 16))
    nblk = rh // tb

    def body(c_ref, a_ref, r0_ref, r1_ref, r2_ref, o_ref):
        o_ref[...] = ((a_ref[...] + r0_ref[0].astype(F32)) + r1_ref[0].astype(F32)) + r2_ref[0].astype(F32)

    slot = lambda j: pl.BlockSpec((1, 1, tb, cdim), lambda a, i, c_ref: (j, a, i, 0))
    return pl.pallas_call(
        body, name=name,
        grid_spec=pltpu.PrefetchScalarGridSpec(
            num_scalar_prefetch=1, grid=(na, nblk),
            in_specs=[pl.BlockSpec((1, tb, cdim), lambda a, i, c_ref: (a, i, 0)), slot(0), slot(1), slot(2)],
            out_specs=pl.BlockSpec((1, tb, cdim), lambda a, i, c_ref: (a, c_ref[0] * nblk + i, 0))),
        out_shape=jax.ShapeDtypeStruct((na, 2 * rh, cdim), F32), compiler_params=_cparams((_PAR, _PAR)),
    )(cidx, own, recv, recv, recv)


def _all_reduce_small(v, *, name):
    r, cdim = v.shape

    def body(v_ref, o_ref, buf, send_sems, recv_sems):
        x, y, c, _ = _place()
        me = 4 * x + 2 * y + c
        buf[me] = v_ref[...]
        cps = []
        for p in range(1, N_DEV):
            to = (1 - x if p & 4 else x, 1 - y if p & 2 else y, 1 - c if p & 1 else c)
            cp = pltpu.make_async_remote_copy(src_ref=v_ref, dst_ref=buf.at[me], send_sem=send_sems.at[p - 1],
                                              recv_sem=recv_sems.at[p - 1], device_id=to, device_id_type=MESH)
            cp.start()
            cps.append(cp)
        for p in range(1, N_DEV):
            frm = (4 * x + 2 * y + c) ^ p
            pltpu.make_async_remote_copy(src_ref=v_ref, dst_ref=buf.at[frm], send_sem=send_sems.at[p - 1],
                                         recv_sem=recv_sems.at[p - 1], device_id=(x, y, c), device_id_type=MESH).wait_recv()
        for cp in cps:
            cp.wait_send()
        tot = buf[0]
        for i in range(1, N_DEV):
            tot = tot + buf[i]
        o_ref[...] = tot

    vm = pl.BlockSpec(memory_space=pltpu.VMEM)
    return pl.pallas_call(
        body, name=name, in_specs=[vm], out_specs=vm, out_shape=jax.ShapeDtypeStruct((r, cdim), F32),
        scratch_shapes=[pltpu.VMEM((N_DEV, r, cdim), F32), pltpu.SemaphoreType.DMA((N_DEV - 1,)), pltpu.SemaphoreType.DMA((N_DEV - 1,))],
    )(v)


_SHARDED = ("even_w_in", "even_w_uq", "even_w_ukv", "even_w_out", "odd_w_in", "odd_w_out")
_COL_SHARDED = ("even_w_in", "even_w_uq", "even_w_ukv", "odd_w_in")
PACK_COLS = 1024


def _unshard(name, stacked):
    n, a, b, cc = stacked.shape
    if name in _COL_SHARDED:
        return stacked.transpose(1, 2, 0, 3).reshape(a, b, n * cc)
    return stacked.transpose(1, 0, 2, 3).reshape(a, n * b, cc)


def _chip_halves(name, full):
    a, b, cc = full.shape
    if name in _COL_SHARDED:
        return full.reshape(a, 2, b // 2, N_CHIPS, cc // N_CHIPS).transpose(3, 1, 0, 2, 4)
    return full.reshape(a, N_CHIPS, 2, b // (2 * N_CHIPS), cc).transpose(1, 2, 0, 3, 4)


def _rope_tables(s):
    pos = jnp.arange(s, dtype=F32)

    def ang(d):
        inv = ROPE_THETA ** (-jnp.arange(0, d, 2, dtype=F32) / d)
        a = pos[:, None] * inv[None, :]
        return jnp.cos(a), jnp.sin(a)

    c16, s16 = ang(MLA_ROPE)
    one, zero = jnp.ones((s, KPE_LANE), F32), jnp.zeros((s, KPE_LANE), F32)
    cos_m = jnp.concatenate([one, c16, c16, one[:, :32]], axis=1)
    sin_m = jnp.concatenate([zero, -s16, s16, zero[:, :32]], axis=1)
    c32, s32 = ang(SWA_DIM)
    cos_s = jnp.concatenate([c32, c32, c32, c32], axis=1)
    sin_s = jnp.concatenate([-s32, s32, -s32, s32], axis=1)
    return cos_m, sin_m, cos_s, sin_s


def _even_weights(w_in, w_uq, w_ukv):
    zeros = lambda n: jnp.zeros((D_MODEL, n), w_in.dtype)
    wcq, wckv, wkpe = w_in[:, 0:256], w_in[:, 256:384], w_in[:, 384:416]
    wfq, wfk, wfv = w_in[:, 416:928], w_in[:, 928:1440], w_in[:, 1440:1952]
    wfl, wg = w_in[:, 1952:1960], w_in[:, 1960:2984]
    misc = jnp.concatenate([zeros(KPE_LANE), wkpe, wfl, zeros(LANES - FL_LANE - FOX_HEADS)], axis=1)
    w_in_p = jnp.concatenate([wg, wfq, wfk, wfv, wcq, wckv, misc], axis=1)
    uq = w_uq.reshape(MLA_Q_RANK, MLA_HEADS, MLA_NOPE + MLA_ROPE)
    uq_p = jnp.pad(uq, ((0, 0), (0, 0), (0, HEAD_PAD - MLA_NOPE - MLA_ROPE))).reshape(MLA_Q_RANK, MLA_HEADS * HEAD_PAD)
    ukv = w_ukv.reshape(MLA_KV_RANK, MLA_HEADS, MLA_NOPE + MLA_V)
    uk_p = jnp.pad(ukv[..., :MLA_NOPE], ((0, 0), (0, 0), (0, HEAD_PAD - MLA_NOPE))).reshape(MLA_KV_RANK, MLA_HEADS * HEAD_PAD)
    uv = ukv[..., MLA_NOPE:].reshape(MLA_KV_RANK, MLA_HEADS * MLA_V)
    ukv_p = jnp.concatenate([uk_p, uv], axis=1)
    return w_in_p, w_in_p.T, uq_p, uq_p.T, ukv_p, ukv_p.T


def _even_weight_grads(dw_in_p, duq_p, dukv_p):
    g = dw_in_p
    gate, fq, fk, fv = g[:, E_GATE:E_FQ], g[:, E_FQ:E_FK], g[:, E_FK:E_FV], g[:, E_FV:E_SMALL]
    cq, ckv, misc = g[:, E_SMALL:E_SMALL + 256], g[:, E_SMALL + 256:E_SMALL + 384], g[:, E_SMALL + 384:]
    dw_in = jnp.concatenate([cq, ckv, misc[:, KPE_LANE:KPE_LANE + MLA_ROPE], fq, fk, fv, misc[:, FL_LANE:FL_LANE + FOX_HEADS], gate], axis=1)
    duq = duq_p.reshape(MLA_Q_RANK, MLA_HEADS, HEAD_PAD)[..., :MLA_NOPE + MLA_ROPE].reshape(MLA_Q_RANK, -1)
    nq = MLA_HEADS * HEAD_PAD
    dk = dukv_p[:, :nq].reshape(MLA_KV_RANK, MLA_HEADS, HEAD_PAD)[..., :MLA_NOPE]
    dvv = dukv_p[:, nq:].reshape(MLA_KV_RANK, MLA_HEADS, MLA_V)
    dukv = jnp.concatenate([dk, dvv], axis=-1).reshape(MLA_KV_RANK, -1)
    return dw_in, duq, dukv


def _interleave(w, forward):
    a, b = (SWA_KV_HEADS, SWA_HEADS // SWA_KV_HEADS) if forward else (SWA_HEADS // SWA_KV_HEADS, SWA_KV_HEADS)
    return w.reshape(w.shape[0], a, b, -1).transpose(0, 2, 1, 3).reshape(w.shape[0], -1)


def _odd_weights(w_in, w_out):
    q, k, v, gate = w_in[:, 0:1024], w_in[:, 1024:1152], w_in[:, 1152:1280], w_in[:, 1280:2304]
    w_p = jnp.concatenate([_interleave(gate, True), _interleave(q, True), k, v], axis=1)
    w_out_p = _interleave(w_out.T, True).T
    return w_p, w_p.T, w_out_p, w_out_p.T


def _odd_weight_grads(g, dw_out_p):
    dw_in = jnp.concatenate([_interleave(g[:, O_Q:O_K], False), g[:, O_K:O_V], g[:, O_V:], _interleave(g[:, O_GATE:O_Q], False)], axis=1)
    return dw_in, _interleave(dw_out_p.T, False).T


def kernel(x, even_w_in, even_q_norm, even_w_uq, even_kv_norm, even_w_ukv, even_b_f, even_w_out, even_ln_g, even_ln_b, odd_w_in, odd_sinks, odd_w_out, odd_ln_g, odd_ln_b, loss_target, m_even_w_in, m_even_q_norm, m_even_w_uq, m_even_kv_norm, m_even_w_ukv, m_even_b_f, m_even_w_out, m_even_ln_g, m_even_ln_b, m_odd_w_in, m_odd_sinks, m_odd_w_out, m_odd_ln_g, m_odd_ln_b, v_even_w_in, v_even_q_norm, v_even_w_uq, v_even_kv_norm, v_even_w_ukv, v_even_b_f, v_even_w_out, v_even_ln_g, v_even_ln_b, v_odd_w_in, v_odd_sinks, v_odd_w_out, v_odd_ln_g, v_odd_ln_b):
    weights = dict(even_w_in=even_w_in, even_q_norm=even_q_norm, even_w_uq=even_w_uq, even_kv_norm=even_kv_norm,
                   even_w_ukv=even_w_ukv, even_b_f=even_b_f, even_w_out=even_w_out, even_ln_g=even_ln_g, even_ln_b=even_ln_b,
                   odd_w_in=odd_w_in, odd_sinks=odd_sinks, odd_w_out=odd_w_out, odd_ln_g=odd_ln_g, odd_ln_b=odd_ln_b)
    mom_m = dict(even_w_in=m_even_w_in, even_q_norm=m_even_q_norm, even_w_uq=m_even_w_uq, even_kv_norm=m_even_kv_norm,
                 even_w_ukv=m_even_w_ukv, even_b_f=m_even_b_f, even_w_out=m_even_w_out, even_ln_g=m_even_ln_g, even_ln_b=m_even_ln_b,
                 odd_w_in=m_odd_w_in, odd_sinks=m_odd_sinks, odd_w_out=m_odd_w_out, odd_ln_g=m_odd_ln_g, odd_ln_b=m_odd_ln_b)
    mom_v = dict(even_w_in=v_even_w_in, even_q_norm=v_even_q_norm, even_w_uq=v_even_w_uq, even_kv_norm=v_even_kv_norm,
                 even_w_ukv=v_even_w_ukv, even_b_f=v_even_b_f, even_w_out=v_even_w_out, even_ln_g=v_even_ln_g, even_ln_b=v_even_ln_b,
                 odd_w_in=v_odd_w_in, odd_sinks=v_odd_sinks, odd_w_out=v_odd_w_out, odd_ln_g=v_odd_ln_g, odd_ln_b=v_odd_ln_b)
    names = list(weights)
    xl = x[0]
    tgt = loss_target[0]
    s = xl.shape[0]
    ax, ay, ac = lax.axis_index("x"), lax.axis_index("y"), lax.axis_index("c")
    chip = 2 * ax + ay
    c_idx = ac.astype(jnp.int32).reshape(1)
    chip_idx = chip.astype(jnp.int32).reshape(1)

    ln_odd = jnp.pad(jnp.concatenate([odd_ln_g, odd_ln_b]), ((0, 12), (0, 0)))[None]
    wb = {n: weights[n].astype(BF16) for n in _SHARDED}
    even_mats = [n for n in _SHARDED if n.startswith("even")]
    odd_mats = [n for n in _SHARDED if n.startswith("odd")]
    gathered = _gather_chip_shards([wb[n][0:1] for n in even_mats] + [ln_odd], name="gather_first")
    full = {n: [_unshard(n, gathered[i])[0], None] for i, n in enumerate(even_mats)}
    later = [wb[n][1:2] for n in even_mats] + [wb[n] for n in odd_mats]
    ln_all = gathered[-1][:, 0]
    odd_g_full = ln_all[:, 0:2].transpose(1, 0, 2).reshape(2, D_MODEL)
    odd_b_full = ln_all[:, 2:4].transpose(1, 0, 2).reshape(2, D_MODEL)

    cos_m, sin_m, cos_s, sin_s = _rope_tables(s)
    bf_tiles = jnp.pad(even_b_f, ((0, 0), (FL_LANE, LANES - FL_LANE - FOX_HEADS)))
    sink_tiles = jnp.broadcast_to(_interleave(odd_sinks, True)[:, :, None, None], (2, SWA_HEADS, 1, LANES))
    mla_scale = (MLA_NOPE + MLA_ROPE) ** -0.5
    fox_scale = FOX_DIM ** -0.5
    mla_pairs, fox_pairs = MLA_HEADS // 2, FOX_HEADS // 2

    saved = []
    x_f, x_b = xl, xl.astype(BF16)
    for layer in range(DEPTH):
        j = layer // 2
        ln = f"L{layer}"
        if layer % 2 == 0:
            w_in_p, w_in_t, uq_p, uq_t, ukv_p, ukv_t = _even_weights(full["even_w_in"][j], full["even_w_uq"][j], full["even_w_ukv"][j])
            w_out, w_out_t = full["even_w_out"][j], full["even_w_out"][j].T
            qg, kg, bft = even_q_norm[j][None], even_kv_norm[j][None], bf_tiles[j][None]
            h, hb = _mm(x_b, w_in_p, out_dtype=F32, name=ln + "_in", also_bf16=True)
            q, k, v, qn, kvn, logf = _even_mid_fwd(h, qg, kg, bft, uq_p, ukv_p, cos_m, sin_m, name=ln + "_mid")
            cum = _cumsum(logf, reverse=False, name=ln + "_cum")[:, FL_LANE:FL_LANE + FOX_HEADS].T
            crow = cum[:, None, :]
            if layer == 0:
                o_mla, lse_mla, rest = _attn_fwd((q, 0), (k, 0), (v, 0), None, dk=HEAD_PAD, npairs=mla_pairs, scale=mla_scale,
                                                 name=ln + "_mla", gather=later)
                for i, n in enumerate(even_mats):
                    full[n][1] = _unshard(n, rest[i])[0]
                for i, n in enumerate(odd_mats):
                    full[n] = _unshard(n, rest[len(even_mats) + i])
            else:
                o_mla, lse_mla = _attn_fwd((q, 0), (k, 0), (v, 0), None, dk=HEAD_PAD, npairs=mla_pairs, scale=mla_scale, name=ln + "_mla")
            o_fox, lse_fox = _attn_fwd((hb, E_FQ), (hb, E_FK), (hb, E_FV), crow, dk=FOX_DIM, npairs=fox_pairs,
                                       scale=fox_scale, name=ln + "_fox")
            o_parts = [o_mla, o_fox]
            g_ln, b_ln = even_ln_g[j][None], even_ln_b[j][None]
            x_n, x_nb, z, xh, rstd = _out_fwd(o_parts, h, x_f, w_out, g_ln, b_ln, name=ln + "_out")
            saved.append(dict(h=h, hb=hb, x_b=x_b, qn=qn, kvn=kvn, q=q, k=k, v=v, crow=crow,
                              o_mla=o_mla, o_fox=o_fox, lse_mla=lse_mla, lse_fox=lse_fox, o_parts=o_parts, z=z, xh=xh, rstd=rstd,
                              w_in_t=w_in_t, uq_t=uq_t, ukv_t=ukv_t, w_out_t=w_out_t, qg=qg, kg=kg, bft=bft, g_ln=g_ln))
        else:
            w_in_p, w_in_t, w_out, w_out_t = _odd_weights(full["odd_w_in"][j], full["odd_w_out"][j])
            h = _mm(x_b, w_in_p, out_dtype=F32, name=ln + "_in")
            q, k, v = _odd_rope(h, cos_s, sin_s, name=ln + "_rope")
            kp = jnp.pad(k, ((WINDOW, 0), (0, 0)))
            vp = jnp.pad(v, ((WINDOW, 0), (0, 0)))
            o_t, lse = _swa_fwd(q, kp, vp, sink_tiles[j], name=ln + "_swa")
            g_ln, b_ln = odd_g_full[j][None], odd_b_full[j][None]
            x_n, x_nb, z, xh, rstd = _out_fwd([o_t], h, x_f, w_out, g_ln, b_ln, name=ln + "_out", o_t=True)
            saved.append(dict(h=h, x_b=x_b, q=q, kp=kp, vp=vp, lse=lse, o_t=o_t, o_parts=[o_t], z=z, xh=xh, rstd=rstd,
                              w_in_t=w_in_t, w_out_t=w_out_t, g_ln=g_ln))
        x_f, x_b = x_n, x_nb

    dxn, loss_tile = _loss_grad(x_f, tgt, name="loss")

    def pair_sums(group, tag):
        gp = [_chip_halves(n, g) for n, g in group.items()]
        from_sib = _swap_halves(gp, name="grad_swap_" + tag)
        return [_add_sibling(g, r, c_idx, chip_idx, name=f"grad_add_sibling_{n}_{tag}") for n, g, r in zip(group, gp, from_sib)]

    def chip_sums(group, pair, from_chips, tag):
        mine = [_add_chips(p[1], r, c_idx, name=f"grad_add_chips_{n}_{tag}") for n, p, r in zip(group, pair, from_chips)]
        return dict(zip(group, _join_halves(mine, name="grad_join_" + tag)))

    grads = {n: [None, None] for n in names}
    for layer in reversed(range(DEPTH)):
        j = layer // 2
        ln = f"L{layer}"
        sv = saved[layer]
        dr, dy, do, dgate, dg_ln, db_ln, *do_t = _out_bwd(dxn, sv["xh"], sv["rstd"], sv["g_ln"], sv["w_out_t"], sv["o_parts"], sv["h"],
                                                          name=ln + "_outb", o_t=layer % 2 == 1)
        dw_out = _mm(sv["z"].T, dy, out_dtype=F32, name=ln + "_dwout")
        if layer % 2 == 0:
            hb = sv["hb"]
            if layer == 0:
                early = {n: grads[n][1][None] for n in even_mats}
                early.update({n: jnp.stack(grads[n]) for n in odd_mats})
                early_pair = pair_sums(early, "early")
                dq, dk, dv, early_chips = _attn_bwd((sv["q"], 0), (sv["k"], 0), (sv["v"], 0), (sv["o_mla"], 0), (do, 0), sv["lse_mla"], None,
                                                    dk=HEAD_PAD, npairs=mla_pairs, scale=mla_scale, dq_dtype=F32, name=ln + "_mlab",
                                                    scatter=[p[0] for p in early_pair])
            else:
                dq, dk, dv = _attn_bwd((sv["q"], 0), (sv["k"], 0), (sv["v"], 0), (sv["o_mla"], 0), (do, 0), sv["lse_mla"], None,
                                       dk=HEAD_PAD, npairs=mla_pairs, scale=mla_scale, dq_dtype=F32, name=ln + "_mlab")
            dfq, dfk, dfv, dcrow, dccol = _attn_bwd((hb, E_FQ), (hb, E_FK), (hb, E_FV), (sv["o_fox"], 0), (do, MLA_HEADS * MLA_V),
                                                    sv["lse_fox"], sv["crow"], dk=FOX_DIM, npairs=fox_pairs,
                                                    scale=fox_scale, dq_dtype=BF16, name=ln + "_foxb")
            dcum = jnp.pad((dcrow[:, 0, :] + dccol[:, 0, :]).T, ((0, 0), (FL_LANE, LANES - FL_LANE - FOX_HEADS)))
            dlogf = _cumsum(dcum, reverse=True, name=ln + "_cumb")
            dh_small, dq_pre, dqg, dkg, dbf = _even_mid_bwd(
                sv["h"], dq, dk, dv, dlogf, sv["qg"], sv["kg"], sv["bft"], sv["uq_t"], sv["ukv_t"], cos_m, sin_m, name=ln + "_midb")
            duq_p = _mm(sv["qn"].T, dq_pre, out_dtype=F32, name=ln + "_dwuq")
            dkv_cat = jnp.concatenate([dk.astype(BF16), dv], axis=1)
            dukv_p = _mm(sv["kvn"].T, dkv_cat, out_dtype=F32, name=ln + "_dwukv")
            dh = jnp.concatenate([dgate, dfq, dfk, dfv, dh_small], axis=1)
            dw_in_p = _mm(sv["x_b"].T, dh, out_dtype=F32, name=ln + "_dwin")
            dw_in, duq, dukv = _even_weight_grads(dw_in_p, duq_p, dukv_p)
            for n, val in (("even_w_in", dw_in), ("even_w_uq", duq), ("even_w_ukv", dukv), ("even_w_out", dw_out),
                           ("even_q_norm", dqg[0]), ("even_kv_norm", dkg[0]), ("even_b_f", dbf[0, FL_LANE:FL_LANE + FOX_HEADS]),
                           ("even_ln_g", dg_ln[0]), ("even_ln_b", db_ln[0])):
                grads[n][j] = val
        else:
            dq_t, dkp, dvp, dsink = _swa_bwd(sv["q"], sv["kp"], sv["vp"], sv["kp"].T, sink_tiles[j], sv["o_t"], do, do_t[0], sv["lse"],
                                             name=ln + "_swab")
            dq_r, dk_r = _odd_rope_bwd(dq_t, dkp[WINDOW:], cos_s, sin_s, name=ln + "_ropeb")
            dh = jnp.concatenate([dgate, dq_r, dk_r, dvp[WINDOW:].astype(BF16)], axis=1)
            dw_in_p = _mm(sv["x_b"].T, dh, out_dtype=F32, name=ln + "_dwin")
            dw_in, dw_out = _odd_weight_grads(dw_in_p, dw_out)
            for n, val in (("odd_w_in", dw_in), ("odd_w_out", dw_out), ("odd_sinks", _interleave(dsink[None, :, 0, 0], False)[0]),
                           ("odd_ln_g", dg_ln[0]), ("odd_ln_b", db_ln[0])):
                grads[n][j] = val
        dxn = _mm(dh, sv["w_in_t"], out_dtype=F32, name=ln + "_dx", res=dr, res_scale=ALPHA)
    grad_x = dxn[None]
    late = {n: grads[n][0][None] for n in even_mats}
    late_pair = pair_sums(late, "late")
    late_sum = chip_sums(late, late_pair, _scatter_to_chips([p[0] for p in late_pair], name="grad_scatter_late"), "late")
    early_sum = chip_sums(early, early_pair, early_chips, "early")
    gshard = {n: jnp.concatenate([late_sum[n], early_sum[n]]) for n in even_mats}
    gshard.update({n: early_sum[n] for n in odd_mats})

    small = [n for n in names if n not in _SHARDED]
    grads = {n: jnp.stack(grads[n]) for n in small}
    sv_flat = jnp.concatenate([grads[n].reshape(-1) for n in small] + [loss_tile[0, :1]])
    sv_real = sv_flat.shape[0]
    sv_rows = -(-sv_real // (PACK_COLS * 8)) * 8
    sv_sum = _all_reduce_small(jnp.pad(sv_flat, (0, sv_rows * PACK_COLS - sv_real)).reshape(sv_rows, PACK_COLS), name="small_all_reduce").reshape(-1)
    off = 0
    for n in small:
        size = int(np.prod(grads[n].shape))
        gfull = sv_sum[off:off + size].reshape(grads[n].shape)
        off += size
        if n in ("odd_ln_g", "odd_ln_b"):
            gfull = lax.dynamic_slice_in_dim(gfull, chip * (D_MODEL // N_CHIPS), D_MODEL // N_CHIPS, axis=1)
        gshard[n] = gfull
    loss = sv_sum[off]

    deltas, new_m, new_v = {}, {}, {}
    for n in names:
        deltas[n], new_m[n], new_v[n] = _adamw(weights[n], gshard[n], mom_m[n], mom_v[n], name="adamw_" + n)
    return (loss, grad_x, *[gshard[n] for n in names], *[deltas[n] for n in names],
            *[new_m[n] for n in names], *[new_v[n] for n in names])
```

```python
import jax
import jax.numpy as jnp
import numpy as np
from jax import lax
from jax.experimental import pallas as pl
from jax.experimental.pallas import tpu as pltpu

F32 = jnp.float32
BF16 = jnp.bfloat16
MESH = pl.DeviceIdType.MESH

D_MODEL = 1024
DEPTH = 4
ROPE_THETA = 10000.0
MLA_HEADS, MLA_NOPE, MLA_ROPE, MLA_V = 8, 64, 32, 64
MLA_Q_RANK, MLA_KV_RANK = 256, 128
FOX_HEADS, FOX_DIM = 8, 64
SWA_HEADS, SWA_KV_HEADS, SWA_DIM, WINDOW = 16, 2, 64, 128
RMS_EPS = 1e-6
LN_EPS = 1e-5
ALPHA = (2 * DEPTH) ** 0.25
EVEN_IN = 2984
ODD_IN = 2304
ADAM_LR, ADAM_B1, ADAM_B2, ADAM_EPS, ADAM_WD, ADAM_STEP = 0.001, 0.9, 0.999, 1e-08, 0.01, 10

LANES = 128
HALF = LANES // 2
HEAD_PAD = 128
N_CHIPS = 4
N_DEV = 8
E_GATE, E_FQ, E_FK, E_FV, E_SMALL = 0, 1024, 1536, 2048, 2560
E_PAD_IN = 3072
KPE_LANE = 64
FL_LANE = 96
O_GATE, O_Q, O_K, O_V = 0, 1024, 2048, 2176

_ARB = "arbitrary"
_PAR = "parallel"


def _cparams(sem):
    return pltpu.CompilerParams(dimension_semantics=sem)


def _pick(n, cands):
    for c in cands:
        if n % c == 0:
            return c
    return n


RESIDENT_BYTES = 8 << 20


def _mm_resident(a, b, *, out_dtype, name, res, res_scale, also_bf16):
    m, k = a.shape
    _, n = b.shape
    tm = _pick(m, (256, 128))
    cn = _pick(n, (512, 384, 256, 128))
    dtypes = [out_dtype, BF16] if also_bf16 else [out_dtype]

    def body(*refs):
        a_ref, b_ref = refs[:2]
        r_ref = refs[2] if res is not None else None
        outs = refs[3 if res is not None else 2:]
        av = a_ref[...].astype(BF16)
        for c0 in range(0, n, cn):
            r = jnp.dot(av, b_ref[:, c0:c0 + cn].astype(BF16), preferred_element_type=F32)
            if res is not None:
                r = r + res_scale * r_ref[:, c0:c0 + cn]
            for o_ref in outs:
                o_ref[:, c0:c0 + cn] = r.astype(o_ref.dtype)

    rows = lambda w: pl.BlockSpec((tm, w), lambda i: (i, 0))
    in_specs = [rows(k), pl.BlockSpec((k, n), lambda i: (0, 0))] + ([rows(n)] if res is not None else [])
    out = pl.pallas_call(
        body, name=name, grid=(m // tm,), in_specs=in_specs, out_specs=[rows(n)] * len(dtypes),
        out_shape=[jax.ShapeDtypeStruct((m, n), d) for d in dtypes], compiler_params=_cparams((_PAR,)),
    )(*([a, b] + ([res] if res is not None else [])))
    return out if also_bf16 else out[0]


def _mm_left_resident(a, b, *, out_dtype, name):
    m, k = a.shape
    _, n = b.shape
    tn = _pick(n, (256, 128))

    def body(a_ref, b_ref, o_ref):
        o_ref[...] = jnp.dot(a_ref[...].astype(BF16), b_ref[...].astype(BF16), preferred_element_type=F32).astype(o_ref.dtype)

    return pl.pallas_call(
        body, name=name, grid=(n // tn,),
        in_specs=[pl.BlockSpec((m, k), lambda j: (0, 0)), pl.BlockSpec((k, tn), lambda j: (0, j))],
        out_specs=pl.BlockSpec((m, tn), lambda j: (0, j)), out_shape=jax.ShapeDtypeStruct((m, n), out_dtype),
        compiler_params=_cparams((_PAR,)),
    )(a, b)


def _mm(a, b, *, out_dtype, name, res=None, res_scale=1.0, also_bf16=False):
    m, k = a.shape
    _, n = b.shape
    if b.size * b.dtype.itemsize <= RESIDENT_BYTES:
        return _mm_resident(a, b, out_dtype=out_dtype, name=name, res=res, res_scale=res_scale, also_bf16=also_bf16)
    if a.size * a.dtype.itemsize <= RESIDENT_BYTES and res is None and not also_bf16:
        return _mm_left_resident(a, b, out_dtype=out_dtype, name=name)
    tm = _pick(m, (512, 256, 128))
    tn = _pick(n, (1024, 768, 512, 384, 256, 128))
    tk = _pick(k, (1024, 768, 512, 256, 128))
    nk = k // tk

    def body(*refs):
        refs = list(refs)
        a_ref, b_ref = refs[:2]
        r_ref = refs[2] if res is not None else None
        acc_ref = refs[-1]
        outs = refs[3 if res is not None else 2:-1]
        kk = pl.program_id(2)

        @pl.when(kk == 0)
        def _():
            acc_ref[...] = jnp.zeros_like(acc_ref)

        acc_ref[...] += jnp.dot(a_ref[...].astype(BF16), b_ref[...].astype(BF16), preferred_element_type=F32)

        @pl.when(kk == nk - 1)
        def _():
            r = acc_ref[...]
            if res is not None:
                r = r + res_scale * r_ref[...]
            for o_ref in outs:
                o_ref[...] = r.astype(o_ref.dtype)

    in_specs = [pl.BlockSpec((tm, tk), lambda i, j, kk: (i, kk)), pl.BlockSpec((tk, tn), lambda i, j, kk: (kk, j))]
    args = [a, b]
    if res is not None:
        in_specs.append(pl.BlockSpec((tm, tn), lambda i, j, kk: (i, j)))
        args.append(res)
    ospec = pl.BlockSpec((tm, tn), lambda i, j, kk: (i, j))
    dtypes = [out_dtype, BF16] if also_bf16 else [out_dtype]
    out = pl.pallas_call(
        body, name=name, grid=(m // tm, n // tn, nk), in_specs=in_specs,
        out_specs=[ospec] * len(dtypes), out_shape=[jax.ShapeDtypeStruct((m, n), d) for d in dtypes],
        scratch_shapes=[pltpu.VMEM((tm, tn), F32)],
        compiler_params=_cparams((_PAR, _PAR, _ARB)),
    )(*args)
    return out if also_bf16 else out[0]


def _rope_tile(t, cos, sin, half):
    w = t.shape[-1]
    lane = lax.broadcasted_iota(jnp.int32, t.shape, 1)
    first = (lane % (2 * half)) < half
    sw = jnp.where(first, pltpu.roll(t, w - half, 1), pltpu.roll(t, half, 1))
    return t * cos + sw * sin


def _sigmoid(x):
    return 1.0 / (1.0 + jnp.exp(-x))


def _lane_mask(shape, lo, hi):
    lane = lax.broadcasted_iota(jnp.int32, shape, 1)
    return (lane >= lo) & (lane < hi)


def _rms(x, g):
    r = lax.rsqrt(jnp.mean(x * x, axis=-1, keepdims=True) + RMS_EPS)
    u = x * r
    return u, r, u * g


def _rms_bwd(dy, u, r, g):
    dyg = dy * g
    dx = r * (dyg - u * jnp.mean(dyg * u, axis=-1, keepdims=True))
    return dx, jnp.sum(dy * u, axis=0, keepdims=True)


def _even_mid_fwd(h, qg, kg, bf_tile, w_uq, w_ukv, cos, sin, *, name):
    s = h.shape[0]
    tb = _pick(s, (512, 256, 128))
    nq = MLA_HEADS * HEAD_PAD

    def body(h_ref, qg_ref, kg_ref, bf_ref, wuq_ref, wukv_ref, cos_ref, sin_ref,
             q_ref, k_ref, v_ref, qn_ref, kvn_ref, lf_ref):
        hb = h_ref[...]
        cq, ckv, misc = hb[:, :MLA_Q_RANK], hb[:, MLA_Q_RANK:MLA_Q_RANK + MLA_KV_RANK], hb[:, MLA_Q_RANK + MLA_KV_RANK:]
        cs, sn = cos_ref[...], sin_ref[...]
        _, _, qn = _rms(cq, qg_ref[...])
        qn = qn.astype(BF16)
        qn_ref[...] = qn
        q = jnp.dot(qn, wuq_ref[...], preferred_element_type=F32)
        _, _, kvn = _rms(ckv, kg_ref[...])
        kvn = kvn.astype(BF16)
        kvn_ref[...] = kvn
        kv = jnp.dot(kvn, wukv_ref[...], preferred_element_type=F32)
        kpe = jnp.where(_lane_mask(misc.shape, KPE_LANE, KPE_LANE + MLA_ROPE), _rope_tile(misc, cs, sn, MLA_ROPE // 2), 0.0)
        for hd in range(MLA_HEADS):
            sl = slice(hd * HEAD_PAD, (hd + 1) * HEAD_PAD)
            q_ref[:, sl] = _rope_tile(q[:, sl], cs, sn, MLA_ROPE // 2).astype(BF16)
            k_ref[:, sl] = (kv[:, sl] + kpe).astype(BF16)
        v_ref[...] = kv[:, nq:].astype(BF16)
        xf = misc + bf_ref[...]
        logf = jnp.minimum(xf, 0.0) - jnp.log(1.0 + jnp.exp(-jnp.abs(xf)))
        lf_ref[...] = jnp.where(_lane_mask(misc.shape, FL_LANE, FL_LANE + FOX_HEADS), logf, 0.0)

    full = lambda a: pl.BlockSpec(a.shape, lambda i: (0,) * a.ndim)
    rows = lambda w, c=0: pl.BlockSpec((tb, w), lambda i: (i, c))
    return pl.pallas_call(
        body, name=name, grid=(s // tb,),
        in_specs=[rows(512, E_SMALL // 512), full(qg), full(kg), full(bf_tile), full(w_uq), full(w_ukv), rows(LANES), rows(LANES)],
        out_specs=[rows(nq), rows(nq), rows(MLA_HEADS * MLA_V), rows(MLA_Q_RANK), rows(MLA_KV_RANK), rows(LANES)],
        out_shape=[jax.ShapeDtypeStruct((s, nq), BF16), jax.ShapeDtypeStruct((s, nq), BF16),
                   jax.ShapeDtypeStruct((s, MLA_HEADS * MLA_V), BF16), jax.ShapeDtypeStruct((s, MLA_Q_RANK), BF16),
                   jax.ShapeDtypeStruct((s, MLA_KV_RANK), BF16), jax.ShapeDtypeStruct((s, LANES), F32)],
        compiler_params=_cparams((_PAR,)),
    )(h, qg, kg, bf_tile, w_uq, w_ukv, cos, sin)


def _even_mid_bwd(h, dq, dk, dv, dlogf, qg, kg, bf_tile, w_uq_t, w_ukv_t, cos, sin, *, name):
    s = h.shape[0]
    tb = _pick(s, (256, 128))
    nq = MLA_HEADS * HEAD_PAD

    def body(h_ref, dq_ref, dk_ref, dv_ref, dlf_ref, qg_ref, kg_ref, bf_ref, wuqt_ref, wukvt_ref, cos_ref, sin_ref,
             dh_ref, dqp_ref, dqg_ref, dkg_ref, dbf_ref):
        @pl.when(pl.program_id(0) == 0)
        def _():
            dqg_ref[...] = jnp.zeros_like(dqg_ref)
            dkg_ref[...] = jnp.zeros_like(dkg_ref)
            dbf_ref[...] = jnp.zeros_like(dbf_ref)

        hb = h_ref[...]
        cq, ckv, misc = hb[:, :MLA_Q_RANK], hb[:, MLA_Q_RANK:MLA_Q_RANK + MLA_KV_RANK], hb[:, MLA_Q_RANK + MLA_KV_RANK:]
        cs, sn = cos_ref[...], -sin_ref[...]
        dkpe = jnp.zeros(misc.shape, F32)
        for hd in range(MLA_HEADS):
            sl = slice(hd * HEAD_PAD, (hd + 1) * HEAD_PAD)
            dqp_ref[:, sl] = _rope_tile(dq_ref[:, sl], cs, sn, MLA_ROPE // 2).astype(BF16)
            dkpe = dkpe + dk_ref[:, sl]
        dqn = jnp.dot(dqp_ref[...], wuqt_ref[...], preferred_element_type=F32)
        uq, rq, _ = _rms(cq, qg_ref[...])
        dcq, dqg = _rms_bwd(dqn, uq, rq, qg_ref[...])
        dqg_ref[...] += dqg
        dkv = jnp.concatenate([dk_ref[...].astype(BF16), dv_ref[...]], axis=1)
        dkvn = jnp.dot(dkv, wukvt_ref[...], preferred_element_type=F32)
        uk, rk, _ = _rms(ckv, kg_ref[...])
        dckv, dkg = _rms_bwd(dkvn, uk, rk, kg_ref[...])
        dkg_ref[...] += dkg
        dmisc = jnp.where(_lane_mask(misc.shape, KPE_LANE, KPE_LANE + MLA_ROPE), _rope_tile(dkpe, cs, sn, MLA_ROPE // 2), 0.0)
        dfl = jnp.where(_lane_mask(misc.shape, FL_LANE, FL_LANE + FOX_HEADS), dlf_ref[...] * _sigmoid(-(misc + bf_ref[...])), 0.0)
        dbf_ref[...] += jnp.sum(dfl, axis=0, keepdims=True)
        dh_ref[:, :MLA_Q_RANK] = dcq.astype(BF16)
        dh_ref[:, MLA_Q_RANK:MLA_Q_RANK + MLA_KV_RANK] = dckv.astype(BF16)
        dh_ref[:, MLA_Q_RANK + MLA_KV_RANK:] = (dmisc + dfl).astype(BF16)

    full = lambda a: pl.BlockSpec(a.shape, lambda i: (0,) * a.ndim)
    rows = lambda w, c=0: pl.BlockSpec((tb, w), lambda i: (i, c))
    return pl.pallas_call(
        body, name=name, grid=(s // tb,),
        in_specs=[rows(512, E_SMALL // 512), rows(nq), rows(nq), rows(MLA_HEADS * MLA_V), rows(LANES), full(qg), full(kg),
                  full(bf_tile), full(w_uq_t), full(w_ukv_t), rows(LANES), rows(LANES)],
        out_specs=[rows(512), rows(nq), full(qg), full(kg), full(bf_tile)],
        out_shape=[jax.ShapeDtypeStruct((s, 512), BF16), jax.ShapeDtypeStruct((s, nq), BF16),
                   jax.ShapeDtypeStruct(qg.shape, F32), jax.ShapeDtypeStruct(kg.shape, F32),
                   jax.ShapeDtypeStruct(bf_tile.shape, F32)],
        compiler_params=_cparams((_ARB,)),
    )(h, dq, dk, dv, dlogf, qg, kg, bf_tile, w_uq_t, w_ukv_t, cos, sin)


def _cumsum(x, *, reverse, name):
    s = x.shape[0]
    tb = _pick(s, (512, 256, 128))
    nb = s // tb

    def body(x_ref, o_ref, carry_ref):
        @pl.when(pl.program_id(0) == 0)
        def _():
            carry_ref[...] = jnp.zeros_like(carry_ref)

        xv = x_ref[...]
        r = lax.broadcasted_iota(jnp.int32, (tb, tb), 0)
        c = lax.broadcasted_iota(jnp.int32, (tb, tb), 1)
        tri = jnp.where((c >= r) if reverse else (c <= r), 1.0, 0.0).astype(BF16)
        hi = xv.astype(BF16)
        r1 = xv - hi.astype(F32)
        mid = r1.astype(BF16)
        lo = (r1 - mid.astype(F32)).astype(BF16)
        cs = (jnp.dot(tri, hi, preferred_element_type=F32) + jnp.dot(tri, mid, preferred_element_type=F32)
              + jnp.dot(tri, lo, preferred_element_type=F32)) + carry_ref[...]
        o_ref[...] = cs
        carry_ref[...] = cs[0:1, :] if reverse else cs[tb - 1:tb, :]

    imap = (lambda i: (nb - 1 - i, 0)) if reverse else (lambda i: (i, 0))
    return pl.pallas_call(
        body, name=name, grid=(nb,), in_specs=[pl.BlockSpec((tb, LANES), imap)],
        out_specs=pl.BlockSpec((tb, LANES), imap), out_shape=jax.ShapeDtypeStruct(x.shape, F32),
        scratch_shapes=[pltpu.VMEM((1, LANES), F32)], compiler_params=_cparams((_ARB,)),
    )(x)


_NT = (((1,), (1,)), ((), ()))
_TN = (((0,), (0,)), ((), ()))


def _head_sel(x, h, dk):
    if dk == LANES:
        return x[:, h * LANES:(h + 1) * LANES]
    return jnp.where(_lane_mask(x.shape, h * HALF, (h + 1) * HALF), x, jnp.zeros_like(x))


def _head_other(x, h, dk):
    return x[:, h * LANES:(h + 1) * LANES] if dk == LANES else x


def _pair(parts, dk=HALF):
    if dk == LANES:
        return jnp.concatenate(parts, axis=1)
    return jnp.where(_lane_mask(parts[0].shape, 0, HALF), parts[0], parts[1])


def _attn_fwd(q, k, v, crow, *, dk, npairs, scale, name, gather=()):
    (qa, qo), (ka, ko), (va, vo) = q, k, v
    s = qa.shape[0]
    wq = 2 * dk
    t = _pick(s, (512, 256, 128))
    nb = s // t
    bias = crow is not None
    ng = len(gather)

    def body(*refs):
        n_in = 3 + bias
        q_ref, k_ref, v_ref = refs[:3]
        cr_ref = refs[3] if bias else None
        o_ref, lse_ref = refs[n_in + ng:n_in + ng + 2]
        qi = pl.program_id(1)
        if ng:
            phases = _gather_phases([a.shape for a in gather], refs[n_in:n_in + ng], refs[n_in + ng + 2:n_in + 2 * ng + 2],
                                    *refs[n_in + 2 * ng + 2:])
            pair = pl.program_id(0)
            pl.when((pair == 0) & (qi == 0))(phases[0])
            pl.when((pair == npairs - 1) & (qi == 0))(phases[1])
        qb = q_ref[...]
        qs = [_head_sel(qb, h, dk) for h in range(2)]

        def scores(j):
            rows = pl.ds(pl.multiple_of(j * t, t), t)
            kb = k_ref[rows, :]
            out = []
            for h in range(2):
                sc = lax.dot_general(qs[h], _head_other(kb, h, dk), _NT, preferred_element_type=F32) * scale
                out.append(sc - cr_ref[h, j] if bias else sc)
            return tuple(out)

        def update(scs, j, state):
            rows = pl.ds(pl.multiple_of(j * t, t), t)
            vb = v_ref[rows, :]
            soft = []
            for h in range(2):
                m, l, _ = state[3 * h:3 * h + 3]
                sc = scs[h]
                m_new = jnp.maximum(m, jnp.max(sc, axis=1, keepdims=True))
                a = jnp.exp(m - m_new)
                p = jnp.exp(sc - m_new)
                soft.append((m_new, a * l + jnp.sum(p, axis=1, keepdims=True), a, p.astype(BF16)))
            new = []
            for h in range(2):
                m_new, l, a, p = soft[h]
                new += [m_new, l, a * state[3 * h + 2] + jnp.dot(p, vb, preferred_element_type=F32)]
            return tuple(new)

        one = (jnp.full((t, 1), -jnp.inf, F32), jnp.zeros((t, 1), F32), jnp.zeros((t, LANES), F32))
        state = lax.fori_loop(0, qi, lambda j, st: update(scores(j), j, st), one + one)
        row = lax.broadcasted_iota(jnp.int32, (t, t), 0)
        col = lax.broadcasted_iota(jnp.int32, (t, t), 1)
        diag = tuple(jnp.where(col <= row, sc, -jnp.inf) for sc in scores(qi))
        state = update(diag, qi, state)
        outs = []
        for h in range(2):
            m, l, acc = state[3 * h:3 * h + 3]
            outs.append(acc / l)
            lse_ref[h] = m + jnp.log(l)
        o_ref[...] = _pair(outs)
        if ng:
            pl.when((pair == npairs - 1) & (qi == nb - 1))(phases[2])

    in_specs = [pl.BlockSpec((t, wq), lambda p, i: (i, qo // wq + p)), pl.BlockSpec((s, wq), lambda p, i: (0, ko // wq + p)),
                pl.BlockSpec((s, LANES), lambda p, i: (0, vo // LANES + p))]
    args = [qa, ka, va]
    if bias:
        in_specs.append(pl.BlockSpec((2, nb, 1, t), lambda p, i: (p, 0, 0, 0)))
        args.append(crow.reshape(2 * npairs, nb, 1, t))
    out = pl.pallas_call(
        body, name=name, grid=(npairs, nb), in_specs=in_specs + [_ANY] * ng,
        out_specs=[pl.BlockSpec((t, LANES), lambda p, i: (i, p)), pl.BlockSpec((2, t, 1), lambda p, i: (p, i, 0))] + [_ANY] * ng,
        out_shape=[jax.ShapeDtypeStruct((s, npairs * LANES), F32), jax.ShapeDtypeStruct((2 * npairs, s, 1), F32)]
        + [jax.ShapeDtypeStruct((N_CHIPS,) + a.shape, a.dtype) for a in gather],
        scratch_shapes=_sem_pairs(GATHER_COPIES * ng) if ng else [],
        compiler_params=_cparams((_ARB, _ARB) if ng else (_PAR, _PAR)),
    )(*args, *gather)
    return (out[0], out[1], out[2:]) if ng else (out[0], out[1])


def _attn_bwd(q, k, v, o, do, lse, crow, *, dk, npairs, scale, dq_dtype, name, scatter=()):
    (qa, qo), (ka, ko), (va, vo), (oa, oo), (da, do_o) = q, k, v, o, do
    s = qa.shape[0]
    wq = 2 * dk
    t = _pick(s, (512, 256, 128))
    nb = s // t
    bias = crow is not None
    ns = len(scatter)

    def body(*refs):
        n_in, n_out = 6 + bias, 3 + 2 * bias
        q_ref, k_ref, v_ref, o_ref, do_ref, lse_ref = refs[:6]
        outs = refs[n_in + ns:n_in + ns + n_out]
        scr = refs[n_in + 2 * ns + n_out:]
        dq_ref, dk_ref, dv_ref = outs[:3]
        dq_s, dk_s, dv_s = scr[:3]
        if bias:
            cr_ref, dc_ref, dcc_ref, dc_s, rs_s = refs[6], outs[3], outs[4], scr[3], scr[4]
        ki, qi = pl.program_id(1), pl.program_id(2)
        if ns:
            send_off, finish = _scatter_phases(ns, refs[n_in:n_in + ns], refs[n_in + ns + n_out:n_in + 2 * ns + n_out], *scr[3 + 2 * bias:])
            pair = pl.program_id(0)
            pl.when((pair == 0) & (ki == 0) & (qi == 0))(send_off)

        @pl.when((ki == 0) & (qi == 0))
        def _():
            dq_s[...] = jnp.zeros_like(dq_s)
            if bias:
                rs_s[...] = jnp.zeros_like(rs_s)

        @pl.when(qi == ki)
        def _():
            dk_s[...] = jnp.zeros_like(dk_s)
            dv_s[...] = jnp.zeros_like(dv_s)
            if bias:
                dc_s[...] = jnp.zeros_like(dc_s)

        def block(on_diagonal):
            qb, kb, vb, dob, ob = q_ref[...], k_ref[...], v_ref[...], do_ref[...], o_ref[...]
            rows = pl.ds(pl.multiple_of(qi * t, t), t)
            row = lax.broadcasted_iota(jnp.int32, (t, t), 0) + qi * t
            col = lax.broadcasted_iota(jnp.int32, (t, t), 1) + ki * t
            fed = []
            for h in range(2):
                kh, doh = _head_other(kb, h, dk), _head_sel(dob, h, HALF)
                sc = lax.dot_general(_head_sel(qb, h, dk), kh, _NT, preferred_element_type=F32)
                dp = lax.dot_general(doh, vb, _NT, preferred_element_type=F32)
                fed.append((kh, doh, sc, dp))
            mid = []
            for h in range(2):
                kh, doh, sc, dp = fed[h]
                sc = sc * scale
                if bias:
                    sc = sc - cr_ref[h]
                if on_diagonal:
                    sc = jnp.where(col <= row, sc, -jnp.inf)
                p = jnp.exp(sc - lse_ref[h])
                delta = jnp.sum(doh.astype(F32) * ob, axis=1, keepdims=True)
                ds = p * (dp - delta)
                mid.append((kh, p.astype(BF16), (ds * scale).astype(BF16)))
            dq_parts, dk_parts, dv_parts = [], [], []
            for h in range(2):
                kh, pb, dsb = mid[h]
                dv_parts.append(lax.dot_general(pb, dob, _TN, preferred_element_type=F32))
                dk_parts.append(lax.dot_general(dsb, _head_other(qb, h, dk), _TN, preferred_element_type=F32))
                if bias:
                    dc_s[h] -= jnp.sum(dsb.astype(F32), axis=0, keepdims=True) * (1.0 / scale)
                    kh = jnp.where(_lane_mask(kh.shape, h * HALF, (h + 1) * HALF), kh, jnp.ones_like(kh))
                dq_parts.append(jnp.dot(dsb, kh, preferred_element_type=F32))
            dv_s[...] += _pair(dv_parts)
            dk_s[...] += _pair(dk_parts, dk)
            dq_s[rows, :] += _pair(dq_parts, dk)
            if bias:
                rs_s[rows, :] += _pair(dq_parts[::-1])

        pl.when(qi > ki)(lambda: block(False))
        pl.when(qi == ki)(lambda: block(True))

        @pl.when(qi == nb - 1)
        def _():
            dk_ref[...] = dk_s[...].astype(dk_ref.dtype)
            dv_ref[...] = dv_s[...].astype(dv_ref.dtype)
            if bias:
                dc_ref[...] = dc_s[...]

        @pl.when((ki == nb - 1) & (qi == nb - 1))
        def _():
            dq_ref[...] = dq_s[...].astype(dq_ref.dtype)
            if bias:
                dcc_ref[...] = rs_s[...] * (1.0 / scale)

        if ns:
            pl.when((pair == npairs - 1) & (ki == nb - 1) & (qi == nb - 1))(finish)

    qrow = lambda p, j, i: jnp.maximum(i, j)
    in_specs = [pl.BlockSpec((t, wq), lambda p, j, i: (qrow(p, j, i), qo // wq + p)),
                pl.BlockSpec((t, wq), lambda p, j, i: (j, ko // wq + p)),
                pl.BlockSpec((t, LANES), lambda p, j, i: (j, vo // LANES + p)),
                pl.BlockSpec((t, LANES), lambda p, j, i: (qrow(p, j, i), oo // LANES + p)),
                pl.BlockSpec((t, LANES), lambda p, j, i: (qrow(p, j, i), do_o // LANES + p)),
                pl.BlockSpec((2, t, 1), lambda p, j, i: (p, qrow(p, j, i), 0))]
    args = [qa, ka, va, oa, da, lse]
    out_specs = [pl.BlockSpec((s, wq), lambda p, j, i: (0, p)), pl.BlockSpec((t, wq), lambda p, j, i: (j, p)),
                 pl.BlockSpec((t, LANES), lambda p, j, i: (j, p))]
    out_shape = [jax.ShapeDtypeStruct((s, npairs * wq), dq_dtype), jax.ShapeDtypeStruct((s, npairs * wq), dq_dtype),
                 jax.ShapeDtypeStruct((s, npairs * LANES), BF16)]
    scratch = [pltpu.VMEM((s, wq), F32), pltpu.VMEM((t, wq), F32), pltpu.VMEM((t, LANES), F32)]
    if bias:
        in_specs.append(pl.BlockSpec((2, 1, t), lambda p, j, i: (p, 0, j)))
        args.append(crow)
        assert dk == HALF, "the sums over the keys ride in the unused half of the dq product of a 64-wide head pair"
        out_specs += [pl.BlockSpec((2, 1, t), lambda p, j, i: (p, 0, j)), pl.BlockSpec((s, LANES), lambda p, j, i: (0, p))]
        out_shape += [jax.ShapeDtypeStruct((2 * npairs, 1, s), F32), jax.ShapeDtypeStruct((s, npairs * LANES), F32)]
        scratch += [pltpu.VMEM((2, 1, t), F32), pltpu.VMEM((s, LANES), F32)]
    n_out = len(out_shape)
    out = pl.pallas_call(
        body, name=name, grid=(npairs, nb, nb), in_specs=in_specs + [_ANY] * ns, out_specs=out_specs + [_ANY] * ns,
        out_shape=out_shape + [jax.ShapeDtypeStruct((3,) + a.shape[1:], a.dtype) for a in scatter],
        scratch_shapes=scratch + (_sem_pairs(3 * ns) if ns else []),
        compiler_params=_cparams((_ARB if ns else _PAR, _ARB, _ARB)),
    )(*args, *scatter)
    return (*out[:n_out], out[n_out:]) if ns else out


def _sub_mask(shape, lo, hi):
    sub = lax.broadcasted_iota(jnp.int32, shape, 0)
    return (sub >= lo) & (sub < hi)


def _pair_rows(parts):
    return jnp.where(_sub_mask(parts[0].shape, 0, HALF), parts[0], parts[1])


def _swa_valid(start, t):
    krow = lax.broadcasted_iota(jnp.int32, (t + WINDOW, t), 0)
    qcol = lax.broadcasted_iota(jnp.int32, (t + WINDOW, t), 1)
    diff = qcol - krow + WINDOW
    return (diff >= 0) & (diff < WINDOW) & (krow + start >= WINDOW)


def _swa_tiles(s):
    t = _pick(s, (256, 128))
    return t, _pick(s // t, (4, 2, 1))


def _swa_fwd(q, kp, vp, sink, *, name):
    s = q.shape[0]
    npairs = SWA_HEADS // 2
    t, nt = _swa_tiles(s)
    scale = SWA_DIM ** -0.5

    def body(q_ref, k_ref, v_ref, sk_ref, o_ref, lse_ref):
        sts, vws = [], []
        for u in range(nt):
            start = pl.multiple_of((pl.program_id(1) * nt + u) * t, t)
            kw = k_ref[pl.ds(start, t + WINDOW), :]
            vws.append(v_ref[pl.ds(start, t + WINDOW), :])
            qb = q_ref[u * t:(u + 1) * t, :]
            valid = _swa_valid(start, t)
            for h in range(2):
                st = lax.dot_general(kw, _head_sel(qb, h, HALF), _NT, preferred_element_type=F32) * scale
                sts.append(jnp.where(valid, st, -jnp.inf))
        pts = []
        for u in range(nt):
            for h in range(2):
                st = sts[2 * u + h]
                snk = sk_ref[h][:, 0:1]
                m = jnp.maximum(jnp.max(st, axis=0, keepdims=True), snk)
                e = jnp.exp(st - m)
                l = jnp.sum(e, axis=0, keepdims=True) + jnp.exp(snk - m)
                pts.append((e * (1.0 / l)).astype(BF16))
                lse_ref[h, :, u * t:(u + 1) * t] = m + jnp.log(l)
        for u in range(nt):
            outs = [lax.dot_general(vws[u], pts[2 * u + h], _TN, preferred_element_type=F32) for h in range(2)]
            o_ref[:, u * t:(u + 1) * t] = _pair_rows(outs)

    kvspec = pl.BlockSpec((s + WINDOW, LANES), lambda p, i: (0, 0))
    return pl.pallas_call(
        body, name=name, grid=(npairs, s // (t * nt)),
        in_specs=[pl.BlockSpec((t * nt, LANES), lambda p, i: (i, p)), kvspec, kvspec, pl.BlockSpec((2, 1, LANES), lambda p, i: (p, 0, 0))],
        out_specs=[pl.BlockSpec((LANES, t * nt), lambda p, i: (p, i)), pl.BlockSpec((2, 1, t * nt), lambda p, i: (p, 0, i))],
        out_shape=[jax.ShapeDtypeStruct((npairs * LANES, s), F32), jax.ShapeDtypeStruct((SWA_HEADS, 1, s), F32)],
        compiler_params=_cparams((_PAR, _PAR)),
    )(q, kp, vp, sink)


def _swa_bwd(q, kp, vp, kpt, sink, ot, do, dot, lse, *, name):
    s = q.shape[0]
    npairs = SWA_HEADS // 2
    t, nt = _swa_tiles(s)
    scale = SWA_DIM ** -0.5

    def body(q_ref, k_ref, v_ref, kt_ref, sk_ref, ot_ref, do_ref, dot_ref, lse_ref, dq_ref, dk_ref, dv_ref, dsk_ref):
        pp, i = pl.program_id(0), pl.program_id(1)

        @pl.when((pp == 0) & (i == 0))
        def _():
            dk_ref[...] = jnp.zeros_like(dk_ref)
            dv_ref[...] = jnp.zeros_like(dv_ref)

        @pl.when(i == 0)
        def _():
            dsk_ref[...] = jnp.zeros_like(dsk_ref)

        fed = []
        for u in range(nt):
            cols = slice(u * t, (u + 1) * t)
            start = pl.multiple_of((i * nt + u) * t, t)
            win = pl.ds(start, t + WINDOW)
            qb, dob = q_ref[cols, :], do_ref[cols, :]
            kw, vw = k_ref[win, :], v_ref[win, :]
            valid = _swa_valid(start, t)
            for h in range(2):
                st = lax.dot_general(kw, _head_sel(qb, h, HALF), _NT, preferred_element_type=F32)
                dpt = lax.dot_general(vw, _head_sel(dob, h, HALF), _NT, preferred_element_type=F32)
                fed.append((cols, win, qb, dob, valid, st, dpt))
        mid = []
        for u in range(nt):
            cols = fed[2 * u][0]
            prod = dot_ref[:, cols].astype(F32) * ot_ref[:, cols]
            for h in range(2):
                valid, st, dpt = fed[2 * u + h][4:]
                lse_b = lse_ref[h, :, cols]
                pt = jnp.exp(jnp.where(valid, st * scale, -jnp.inf) - lse_b)
                delta = jnp.sum(jnp.where(_sub_mask(prod.shape, h * HALF, (h + 1) * HALF), prod, 0.0), axis=0, keepdims=True)
                dst = pt * (dpt - delta)
                psink = jnp.exp(sk_ref[h][:, 0:1] - lse_b)
                dsk_ref[h] += jnp.broadcast_to(-jnp.sum(psink * delta, axis=1, keepdims=True), (1, LANES))
                mid.append((pt.astype(BF16), (dst * scale).astype(BF16)))
        for u in range(nt):
            cols, win, qb, dob = fed[2 * u][:4]
            ktw = kt_ref[:, win]
            dq_parts = [jnp.dot(ktw, mid[2 * u + h][1], preferred_element_type=F32) for h in range(2)]
            dk_parts = [jnp.dot(mid[2 * u + h][1], qb, preferred_element_type=F32) for h in range(2)]
            dv_parts = [jnp.dot(mid[2 * u + h][0], dob, preferred_element_type=F32) for h in range(2)]
            dq_ref[:, cols] = _pair_rows(dq_parts)
            dk_ref[win, :] += _pair(dk_parts)
            dv_ref[win, :] += _pair(dv_parts)

    tile = pl.BlockSpec((t * nt, LANES), lambda p, i: (i, p))
    ttile = pl.BlockSpec((LANES, t * nt), lambda p, i: (p, i))
    kvspec = pl.BlockSpec((s + WINDOW, LANES), lambda p, i: (0, 0))
    ktspec = pl.BlockSpec((LANES, s + WINDOW), lambda p, i: (0, 0))
    skspec = pl.BlockSpec((2, 1, LANES), lambda p, i: (p, 0, 0))
    return pl.pallas_call(
        body, name=name, grid=(npairs, s // (t * nt)),
        in_specs=[tile, kvspec, kvspec, ktspec, skspec, ttile, tile, ttile, pl.BlockSpec((2, 1, t * nt), lambda p, i: (p, 0, i))],
        out_specs=[ttile, kvspec, kvspec, skspec],
        out_shape=[jax.ShapeDtypeStruct((npairs * LANES, s), F32), jax.ShapeDtypeStruct((s + WINDOW, LANES), F32),
                   jax.ShapeDtypeStruct((s + WINDOW, LANES), F32), jax.ShapeDtypeStruct((SWA_HEADS, 1, LANES), F32)],
        compiler_params=_cparams((_ARB, _ARB)),
    )(q, kp, vp, kpt, sink, ot, do, dot, lse)


def _odd_rope(h, cos, sin, *, name):
    s = h.shape[0]
    tb = _pick(s, (512, 256, 128))
    nq = SWA_HEADS * SWA_DIM

    def body(q_ref, kv_ref, cos_ref, sin_ref, qo_ref, ko_ref, vo_ref):
        cs, sn = cos_ref[...], sin_ref[...]
        for j in range(nq // LANES):
            sl = slice(j * LANES, (j + 1) * LANES)
            qo_ref[:, sl] = _rope_tile(q_ref[:, sl], cs, sn, SWA_DIM // 2).astype(BF16)
        ko_ref[...] = _rope_tile(kv_ref[:, :LANES], cs, sn, SWA_DIM // 2).astype(BF16)
        vo_ref[...] = kv_ref[:, LANES:].astype(BF16)

    rows = lambda w, c=0: pl.BlockSpec((tb, w), lambda i: (i, c))
    return pl.pallas_call(
        body, name=name, grid=(s // tb,),
        in_specs=[rows(nq, O_Q // nq), rows(2 * LANES, O_K // (2 * LANES)), rows(LANES), rows(LANES)],
        out_specs=[rows(nq), rows(LANES), rows(LANES)],
        out_shape=[jax.ShapeDtypeStruct((s, nq), BF16), jax.ShapeDtypeStruct((s, LANES), BF16), jax.ShapeDtypeStruct((s, LANES), BF16)],
        compiler_params=_cparams((_PAR,)),
    )(h, h, cos, sin)


def _odd_rope_bwd(dq_t, dk, cos, sin, *, name):
    s = dk.shape[0]
    tb = _pick(s, (256, 128))
    nq = SWA_HEADS * SWA_DIM

    def body(dq_ref, dk_ref, cos_ref, sin_ref, qo_ref, ko_ref):
        cs, sn = cos_ref[...], -sin_ref[...]
        dq = dq_ref[...].T
        for j in range(nq // LANES):
            sl = slice(j * LANES, (j + 1) * LANES)
            qo_ref[:, sl] = _rope_tile(dq[:, sl], cs, sn, SWA_DIM // 2).astype(BF16)
        ko_ref[...] = _rope_tile(dk_ref[...], cs, sn, SWA_DIM // 2).astype(BF16)

    rows = lambda w: pl.BlockSpec((tb, w), lambda i: (i, 0))
    return pl.pallas_call(
        body, name=name, grid=(s // tb,),
        in_specs=[pl.BlockSpec((nq, tb), lambda i: (0, i)), rows(LANES), rows(LANES), rows(LANES)],
        out_specs=[rows(nq), rows(LANES)],
        out_shape=[jax.ShapeDtypeStruct((s, nq), BF16), jax.ShapeDtypeStruct((s, LANES), BF16)],
        compiler_params=_cparams((_PAR,)),
    )(dq_t, dk, cos, sin)


def _mixer_out(o_refs, o_t):
    if o_t:
        return o_refs[0][...].T
    return o_refs[0][...] if len(o_refs) == 1 else jnp.concatenate([r[...] for r in o_refs], axis=1)


def _mixer_specs(o_parts, o_t, tb):
    if o_t:
        return [pl.BlockSpec((D_MODEL, tb), lambda i: (0, i))]
    return [pl.BlockSpec((tb, a.shape[1]), lambda i: (i, 0)) for a in o_parts]


def _out_fwd(o_parts, h, x, w_out, g, b, *, name, o_t=False):
    s = x.shape[0]
    tb = _pick(s, (256, 128))
    n_o = len(o_parts)

    def body(*refs):
        o_refs = refs[:n_o]
        gate_ref, x_ref, w_ref, g_ref, b_ref, xn_ref, xb_ref, z_ref, xh_ref, rs_ref = refs[n_o:]
        gate = gate_ref[...]
        o = _mixer_out(o_refs, o_t)
        z = (o * (gate * _sigmoid(gate))).astype(BF16)
        z_ref[...] = z
        r = ALPHA * x_ref[...] + jnp.dot(z, w_ref[...], preferred_element_type=F32)
        mu = jnp.mean(r, axis=-1, keepdims=True)
        rc = r - mu
        rstd = lax.rsqrt(jnp.mean(rc * rc, axis=-1, keepdims=True) + LN_EPS)
        xh = rc * rstd
        xn = xh * g_ref[...] + b_ref[...]
        xh_ref[...] = xh
        rs_ref[...] = rstd
        xn_ref[...] = xn
        xb_ref[...] = xn.astype(BF16)

    rows = lambda w: pl.BlockSpec((tb, w), lambda i: (i, 0))
    full = lambda a: pl.BlockSpec(a.shape, lambda i: (0,) * a.ndim)
    return pl.pallas_call(
        body, name=name, grid=(s // tb,),
        in_specs=_mixer_specs(o_parts, o_t, tb) + [rows(D_MODEL), rows(D_MODEL), full(w_out), full(g), full(b)],
        out_specs=[rows(D_MODEL), rows(D_MODEL), rows(D_MODEL), rows(D_MODEL), rows(1)],
        out_shape=[jax.ShapeDtypeStruct((s, D_MODEL), F32), jax.ShapeDtypeStruct((s, D_MODEL), BF16),
                   jax.ShapeDtypeStruct((s, D_MODEL), BF16), jax.ShapeDtypeStruct((s, D_MODEL), F32),
                   jax.ShapeDtypeStruct((s, 1), F32)],
        compiler_params=_cparams((_PAR,)),
    )(*o_parts, h, x, w_out, g, b)


def _out_bwd(dxn, xh, rstd, g, w_out_t, o_parts, h, *, name, o_t=False):
    s = dxn.shape[0]
    tb = _pick(s, (256, 128))
    n_o = len(o_parts)

    def body(*refs):
        dxn_ref, xh_ref, rs_ref, g_ref, wt_ref = refs[:5]
        o_refs = refs[5:5 + n_o]
        gate_ref, dr_ref, dy_ref, do_ref, dgate_ref, dg_ref, db_ref = refs[5 + n_o:12 + n_o]

        @pl.when(pl.program_id(0) == 0)
        def _():
            dg_ref[...] = jnp.zeros_like(dg_ref)
            db_ref[...] = jnp.zeros_like(db_ref)

        dxn_b, xh_b = dxn_ref[...], xh_ref[...]
        dg_ref[...] += jnp.sum(dxn_b * xh_b, axis=0, keepdims=True)
        db_ref[...] += jnp.sum(dxn_b, axis=0, keepdims=True)
        dxh = dxn_b * g_ref[...]
        dr = rs_ref[...] * (dxh - jnp.mean(dxh, axis=-1, keepdims=True) - xh_b * jnp.mean(dxh * xh_b, axis=-1, keepdims=True))
        dr_ref[...] = dr
        dy = dr.astype(BF16)
        dy_ref[...] = dy
        dz = jnp.dot(dy, wt_ref[...], preferred_element_type=F32)
        gate = gate_ref[...]
        sg = _sigmoid(gate)
        o = _mixer_out(o_refs, o_t)
        do = dz * (gate * sg)
        do_ref[...] = do.astype(BF16)
        if o_t:
            refs[12 + n_o][...] = do.T.astype(BF16)
        dgate_ref[...] = (dz * o * (sg * (1.0 + gate * (1.0 - sg)))).astype(BF16)

    rows = lambda w: pl.BlockSpec((tb, w), lambda i: (i, 0))
    full = lambda a: pl.BlockSpec(a.shape, lambda i: (0,) * a.ndim)
    cols = pl.BlockSpec((D_MODEL, tb), lambda i: (0, i))
    return pl.pallas_call(
        body, name=name, grid=(s // tb,),
        in_specs=[rows(D_MODEL), rows(D_MODEL), rows(1), full(g), full(w_out_t)] + _mixer_specs(o_parts, o_t, tb) + [rows(D_MODEL)],
        out_specs=[rows(D_MODEL), rows(D_MODEL), rows(D_MODEL), rows(D_MODEL), full(g), full(g)] + [cols] * o_t,
        out_shape=[jax.ShapeDtypeStruct((s, D_MODEL), F32), jax.ShapeDtypeStruct((s, D_MODEL), BF16),
                   jax.ShapeDtypeStruct((s, D_MODEL), BF16), jax.ShapeDtypeStruct((s, D_MODEL), BF16),
                   jax.ShapeDtypeStruct(g.shape, F32), jax.ShapeDtypeStruct(g.shape, F32)]
        + [jax.ShapeDtypeStruct((D_MODEL, s), BF16)] * o_t,
        compiler_params=_cparams((_ARB,)),
    )(dxn, xh, rstd, g, w_out_t, *o_parts, h)


def _loss_grad(y, target, *, name):
    s, d = y.shape
    tb = _pick(s, (512, 256, 128))

    def body(y_ref, t_ref, dy_ref, l_ref):
        @pl.when(pl.program_id(0) == 0)
        def _():
            l_ref[...] = jnp.zeros_like(l_ref)

        err = y_ref[...] - t_ref[...]
        dy_ref[...] = err * (1.0 / d)
        per_tok = jnp.mean(err * err, axis=-1, keepdims=True)
        l_ref[...] += 0.5 * jnp.sum(per_tok, axis=0, keepdims=True)

    rows = pl.BlockSpec((tb, d), lambda i: (i, 0))
    return pl.pallas_call(
        body, name=name, grid=(s // tb,), in_specs=[rows, rows],
        out_specs=[rows, pl.BlockSpec((8, LANES), lambda i: (0, 0))],
        out_shape=[jax.ShapeDtypeStruct((s, d), F32), jax.ShapeDtypeStruct((8, LANES), F32)],
        compiler_params=_cparams((_ARB,)),
    )(y, target)


def _adamw(w, g, m, v, *, name):
    shape = w.shape
    w3, g3, m3, v3 = (a.reshape((1,) * (3 - a.ndim) + a.shape) for a in (w, g, m, v))
    a0, a1, a2 = w3.shape
    tb = _pick(a1, (256, 128)) if a1 % 8 == 0 else a1
    c1 = 1.0 - ADAM_B1 ** ADAM_STEP
    c2 = 1.0 - ADAM_B2 ** ADAM_STEP

    def body(w_ref, g_ref, m_ref, v_ref, d_ref, mo_ref, vo_ref):
        gg = g_ref[...]
        mn = ADAM_B1 * m_ref[...] + (1.0 - ADAM_B1) * gg
        vn = ADAM_B2 * v_ref[...] + (1.0 - ADAM_B2) * (gg * gg)
        mo_ref[...] = mn
        vo_ref[...] = vn
        d_ref[...] = -ADAM_LR * ((mn / c1) / (jnp.sqrt(vn / c2) + ADAM_EPS) + ADAM_WD * w_ref[...])

    spec = pl.BlockSpec((1, tb, a2), lambda i, j: (i, j, 0))
    outs = pl.pallas_call(
        body, name=name, grid=(a0, a1 // tb), in_specs=[spec] * 4, out_specs=[spec] * 3,
        out_shape=[jax.ShapeDtypeStruct(w3.shape, F32)] * 3, compiler_params=_cparams((_PAR, _PAR)),
    )(w3, g3, m3, v3)
    return tuple(a.reshape(shape) for a in outs)


def _place():
    x, y, c = lax.axis_index("x"), lax.axis_index("y"), lax.axis_index("c")
    return x, y, c, [(1 - x, y), (x, 1 - y), (1 - x, 1 - y)]


_ANY = pl.BlockSpec(memory_space=pl.ANY)


def _sem_pairs(n):
    return [pltpu.SemaphoreType.DMA((n,)), pltpu.SemaphoreType.DMA((n,))]


GATHER_COPIES = 7


def _gather_phases(shapes, srcs, outs, send_sems, recv_sems):
    nt = len(shapes)
    x, y, c, chips = _place()
    me = 2 * x + y
    sib = (x, y, 1 - c)

    def half(t, chip, hc):
        rh = shapes[t][1] // 2
        return outs[t].at[chip, :, pl.ds(hc * rh, rh), :]

    def copy(t, kk, s_ref, d_ref, to):
        return pltpu.make_async_remote_copy(src_ref=s_ref, dst_ref=d_ref, send_sem=send_sems.at[GATHER_COPIES * t + kk],
                                            recv_sem=recv_sems.at[GATHER_COPIES * t + kk], device_id=to, device_id_type=MESH)

    def first_hop(t, j):
        rh = shapes[t][1] // 2
        cx, cy = chips[j]
        return copy(t, j, srcs[t].at[:, pl.ds(c * rh, rh), :], half(t, me, c), (cx, cy, c))

    def passed(t, j):
        cx, cy = chips[j]
        landed = half(t, 2 * cx + cy, c)
        return copy(t, 3 + j, landed, landed, sib)

    def own(t):
        return copy(t, 6, srcs[t], outs[t].at[me], sib)

    pairs = [(t, j) for j in range(3) for t in range(nt)]

    def start():
        for t, j in pairs:
            first_hop(t, j).start()
        for t in range(nt):
            own(t).start()

    def pass_on():
        for t, j in pairs:
            cx, cy = chips[j]
            landed = half(t, 2 * cx + cy, c)
            copy(t, j, landed, landed, (cx, cy, c)).wait_recv()
            passed(t, j).start()

    def finish():
        for t, j in pairs:
            cx, cy = chips[j]
            theirs = half(t, 2 * cx + cy, 1 - c)
            copy(t, 3 + j, theirs, theirs, sib).wait_recv()
        for t in range(nt):
            own(t).wait()
        for t, j in pairs:
            first_hop(t, j).wait_send()
            passed(t, j).wait_send()

    return start, pass_on, finish


def _gather_chip_shards(tensors, *, name):
    nt = len(tensors)

    def body(*refs):
        for phase in _gather_phases([a.shape for a in tensors], refs[:nt], refs[nt:2 * nt], *refs[2 * nt:]):
            phase()

    return pl.pallas_call(
        body, name=name, in_specs=[_ANY] * nt, out_specs=[_ANY] * nt,
        out_shape=[jax.ShapeDtypeStruct((N_CHIPS,) + a.shape, a.dtype) for a in tensors],
        scratch_shapes=_sem_pairs(GATHER_COPIES * nt),
    )(*tensors)


def _swap_halves(gs, *, name):
    nt = len(gs)

    def body(*refs):
        srcs, outs = refs[:nt], refs[nt:2 * nt]
        send_sems, recv_sems = refs[2 * nt:]
        x, y, c, _ = _place()
        cps = [pltpu.make_async_remote_copy(src_ref=srcs[t].at[:, 1 - c], dst_ref=outs[t], send_sem=send_sems.at[t],
                                            recv_sem=recv_sems.at[t], device_id=(x, y, 1 - c), device_id_type=MESH)
               for t in range(nt)]
        for cp in cps:
            cp.start()
        for cp in cps:
            cp.wait()

    return pl.pallas_call(
        body, name=name, in_specs=[_ANY] * nt, out_specs=[_ANY] * nt,
        out_shape=[jax.ShapeDtypeStruct((N_CHIPS,) + g.shape[2:], g.dtype) for g in gs], scratch_shapes=_sem_pairs(nt),
    )(*gs)


def _scatter_phases(nt, srcs, outs, send_sems, recv_sems):
    x, y, c, chips = _place()

    def copies():
        return [pltpu.make_async_remote_copy(src_ref=srcs[t].at[2 * cx + cy], dst_ref=outs[t].at[j],
                                             send_sem=send_sems.at[3 * t + j], recv_sem=recv_sems.at[3 * t + j],
                                             device_id=(cx, cy, c), device_id_type=MESH)
                for j, (cx, cy) in enumerate(chips) for t in range(nt)]

    def start():
        for cp in copies():
            cp.start()

    def finish():
        for cp in copies():
            cp.wait()

    return start, finish


def _scatter_to_chips(blocks, *, name):
    nt = len(blocks)

    def body(*refs):
        for phase in _scatter_phases(nt, refs[:nt], refs[nt:2 * nt], *refs[2 * nt:]):
            phase()

    return pl.pallas_call(
        body, name=name, in_specs=[_ANY] * nt, out_specs=[_ANY] * nt,
        out_shape=[jax.ShapeDtypeStruct((3,) + a.shape[1:], a.dtype) for a in blocks], scratch_shapes=_sem_pairs(3 * nt),
    )(*blocks)


def _join_halves(bufs, *, name):
    nt = len(bufs)

    def body(*refs):
        srcs, outs = refs[:nt], refs[nt:2 * nt]
        send_sems, recv_sems = refs[2 * nt:]
        x, y, c, _ = _place()
        cps = []
        for t in range(nt):
            rh = bufs[t].shape[1] // 2
            rows = pl.ds(c * rh, rh)
            cps.append(pltpu.make_async_remote_copy(src_ref=srcs[t].at[:, rows, :], dst_ref=outs[t].at[:, rows, :],
                                                    send_sem=send_sems.at[t], recv_sem=recv_sems.at[t],
                                                    device_id=(x, y, 1 - c), device_id_type=MESH))
        for cp in cps:
            cp.start()
        for cp in cps:
            cp.wait()

    return pl.pallas_call(
        body, name=name, in_specs=[_ANY] * nt, out_specs=[_ANY] * nt,
        out_shape=[jax.ShapeDtypeStruct(b.shape, b.dtype) for b in bufs], scratch_shapes=_sem_pairs(nt),
        input_output_aliases={t: t for t in range(nt)},
    )(*bufs)


def _add_sibling(g, recv, cidx, chip_idx, *, name):
    _, _, na, rh, cdim = g.shape
    tb = _pick(rh, (256, 128, 64, 32, 16))

    def body(c_ref, k_ref, g_ref, r_ref, s_ref, o_ref):
        tot = g_ref[0, 0] + r_ref[0]
        s_ref[0] = tot.astype(BF16)

        @pl.when(pl.program_id(2) == k_ref[0])
        def _():
            o_ref[...] = tot

    return pl.pallas_call(
        body, name=name,
        grid_spec=pltpu.PrefetchScalarGridSpec(
            num_scalar_prefetch=2, grid=(na, rh // tb, N_CHIPS),
            in_specs=[pl.BlockSpec((1, 1, 1, tb, cdim), lambda a, i, k, c_ref, k_ref: (k, c_ref[0], a, i, 0)),
                      pl.BlockSpec((1, 1, tb, cdim), lambda a, i, k, c_ref, k_ref: (k, a, i, 0))],
            out_specs=[pl.BlockSpec((1, 1, tb, cdim), lambda a, i, k, c_ref, k_ref: (k, a, i, 0)),
                       pl.BlockSpec((1, tb, cdim), lambda a, i, k, c_ref, k_ref: (a, i, 0))]),
        out_shape=[jax.ShapeDtypeStruct((N_CHIPS, na, rh, cdim), BF16), jax.ShapeDtypeStruct((na, rh, cdim), F32)],
        compiler_params=_cparams((_PAR, _PAR, _ARB)),
    )(cidx, chip_idx, g, recv)


def _add_chips(own, recv, cidx, *, name):
    na, rh, cdim = own.shape
    tb = _pick(rh, (256, 128, 64, 32, 16))
    nblk = rh // tb

    def body(c_ref, a_ref, r0_ref, r1_ref, r2_ref, o_ref):
        o_ref[...] = ((a_ref[...] + r0_ref[0].astype(F32)) + r1_ref[0].astype(F32)) + r2_ref[0].astype(F32)

    slot = lambda j: pl.BlockSpec((1, 1, tb, cdim), lambda a, i, c_ref: (j, a, i, 0))
    return pl.pallas_call(
        body, name=name,
        grid_spec=pltpu.PrefetchScalarGridSpec(
            num_scalar_prefetch=1, grid=(na, nblk),
            in_specs=[pl.BlockSpec((1, tb, cdim), lambda a, i, c_ref: (a, i, 0)), slot(0), slot(1), slot(2)],
            out_specs=pl.BlockSpec((1, tb, cdim), lambda a, i, c_ref: (a, c_ref[0] * nblk + i, 0))),
        out_shape=jax.ShapeDtypeStruct((na, 2 * rh, cdim), F32), compiler_params=_cparams((_PAR, _PAR)),
    )(cidx, own, recv, recv, recv)


def _all_reduce_small(v, *, name):
    r, cdim = v.shape

    def body(v_ref, o_ref, buf, send_sems, recv_sems):
        x, y, c, _ = _place()
        me = 4 * x + 2 * y + c
        buf[me] = v_ref[...]
        cps = []
        for p in range(1, N_DEV):
            to = (1 - x if p & 4 else x, 1 - y if p & 2 else y, 1 - c if p & 1 else c)
            cp = pltpu.make_async_remote_copy(src_ref=v_ref, dst_ref=buf.at[me], send_sem=send_sems.at[p - 1],
                                              recv_sem=recv_sems.at[p - 1], device_id=to, device_id_type=MESH)
            cp.start()
            cps.append(cp)
        for p in range(1, N_DEV):
            frm = (4 * x + 2 * y + c) ^ p
            pltpu.make_async_remote_copy(src_ref=v_ref, dst_ref=buf.at[frm], send_sem=send_sems.at[p - 1],
                                         recv_sem=recv_sems.at[p - 1], device_id=(x, y, c), device_id_type=MESH).wait_recv()
        for cp in cps:
            cp.wait_send()
        tot = buf[0]
        for i in range(1, N_DEV):
            tot = tot + buf[i]
        o_ref[...] = tot

    vm = pl.BlockSpec(memory_space=pltpu.VMEM)
    return pl.pallas_call(
        body, name=name, in_specs=[vm], out_specs=vm, out_shape=jax.ShapeDtypeStruct((r, cdim), F32),
        scratch_shapes=[pltpu.VMEM((N_DEV, r, cdim), F32), pltpu.SemaphoreType.DMA((N_DEV - 1,)), pltpu.SemaphoreType.DMA((N_DEV - 1,))],
    )(v)


_SHARDED = ("even_w_in", "even_w_uq", "even_w_ukv", "even_w_out", "odd_w_in", "odd_w_out")
_COL_SHARDED = ("even_w_in", "even_w_uq", "even_w_ukv", "odd_w_in")
PACK_COLS = 1024


def _unshard(name, stacked):
    n, a, b, cc = stacked.shape
    if name in _COL_SHARDED:
        return stacked.transpose(1, 2, 0, 3).reshape(a, b, n * cc)
    return stacked.transpose(1, 0, 2, 3).reshape(a, n * b, cc)


def _chip_halves(name, full):
    a, b, cc = full.shape
    if name in _COL_SHARDED:
        return full.reshape(a, 2, b // 2, N_CHIPS, cc // N_CHIPS).transpose(3, 1, 0, 2, 4)
    return full.reshape(a, N_CHIPS, 2, b // (2 * N_CHIPS), cc).transpose(1, 2, 0, 3, 4)


def _rope_tables(s):
    pos = jnp.arange(s, dtype=F32)

    def ang(d):
        inv = ROPE_THETA ** (-jnp.arange(0, d, 2, dtype=F32) / d)
        a = pos[:, None] * inv[None, :]
        return jnp.cos(a), jnp.sin(a)

    c16, s16 = ang(MLA_ROPE)
    one, zero = jnp.ones((s, KPE_LANE), F32), jnp.zeros((s, KPE_LANE), F32)
    cos_m = jnp.concatenate([one, c16, c16, one[:, :32]], axis=1)
    sin_m = jnp.concatenate([zero, -s16, s16, zero[:, :32]], axis=1)
    c32, s32 = ang(SWA_DIM)
    cos_s = jnp.concatenate([c32, c32, c32, c32], axis=1)
    sin_s = jnp.concatenate([-s32, s32, -s32, s32], axis=1)
    return cos_m, sin_m, cos_s, sin_s


def _even_weights(w_in, w_uq, w_ukv):
    zeros = lambda n: jnp.zeros((D_MODEL, n), w_in.dtype)
    wcq, wckv, wkpe = w_in[:, 0:256], w_in[:, 256:384], w_in[:, 384:416]
    wfq, wfk, wfv = w_in[:, 416:928], w_in[:, 928:1440], w_in[:, 1440:1952]
    wfl, wg = w_in[:, 1952:1960], w_in[:, 1960:2984]
    misc = jnp.concatenate([zeros(KPE_LANE), wkpe, wfl, zeros(LANES - FL_LANE - FOX_HEADS)], axis=1)
    w_in_p = jnp.concatenate([wg, wfq, wfk, wfv, wcq, wckv, misc], axis=1)
    uq = w_uq.reshape(MLA_Q_RANK, MLA_HEADS, MLA_NOPE + MLA_ROPE)
    uq_p = jnp.pad(uq, ((0, 0), (0, 0), (0, HEAD_PAD - MLA_NOPE - MLA_ROPE))).reshape(MLA_Q_RANK, MLA_HEADS * HEAD_PAD)
    ukv = w_ukv.reshape(MLA_KV_RANK, MLA_HEADS, MLA_NOPE + MLA_V)
    uk_p = jnp.pad(ukv[..., :MLA_NOPE], ((0, 0), (0, 0), (0, HEAD_PAD - MLA_NOPE))).reshape(MLA_KV_RANK, MLA_HEADS * HEAD_PAD)
    uv = ukv[..., MLA_NOPE:].reshape(MLA_KV_RANK, MLA_HEADS * MLA_V)
    ukv_p = jnp.concatenate([uk_p, uv], axis=1)
    return w_in_p, w_in_p.T, uq_p, uq_p.T, ukv_p, ukv_p.T


def _even_weight_grads(dw_in_p, duq_p, dukv_p):
    g = dw_in_p
    gate, fq, fk, fv = g[:, E_GATE:E_FQ], g[:, E_FQ:E_FK], g[:, E_FK:E_FV], g[:, E_FV:E_SMALL]
    cq, ckv, misc = g[:, E_SMALL:E_SMALL + 256], g[:, E_SMALL + 256:E_SMALL + 384], g[:, E_SMALL + 384:]
    dw_in = jnp.concatenate([cq, ckv, misc[:, KPE_LANE:KPE_LANE + MLA_ROPE], fq, fk, fv, misc[:, FL_LANE:FL_LANE + FOX_HEADS], gate], axis=1)
    duq = duq_p.reshape(MLA_Q_RANK, MLA_HEADS, HEAD_PAD)[..., :MLA_NOPE + MLA_ROPE].reshape(MLA_Q_RANK, -1)
    nq = MLA_HEADS * HEAD_PAD
    dk = dukv_p[:, :nq].reshape(MLA_KV_RANK, MLA_HEADS, HEAD_PAD)[..., :MLA_NOPE]
    dvv = dukv_p[:, nq:].reshape(MLA_KV_RANK, MLA_HEADS, MLA_V)
    dukv = jnp.concatenate([dk, dvv], axis=-1).reshape(MLA_KV_RANK, -1)
    return dw_in, duq, dukv


def _interleave(w, forward):
    a, b = (SWA_KV_HEADS, SWA_HEADS // SWA_KV_HEADS) if forward else (SWA_HEADS // SWA_KV_HEADS, SWA_KV_HEADS)
    return w.reshape(w.shape[0], a, b, -1).transpose(0, 2, 1, 3).reshape(w.shape[0], -1)


def _odd_weights(w_in, w_out):
    q, k, v, gate = w_in[:, 0:1024], w_in[:, 1024:1152], w_in[:, 1152:1280], w_in[:, 1280:2304]
    w_p = jnp.concatenate([_interleave(gate, True), _interleave(q, True), k, v], axis=1)
    w_out_p = _interleave(w_out.T, True).T
    return w_p, w_p.T, w_out_p, w_out_p.T


def _odd_weight_grads(g, dw_out_p):
    dw_in = jnp.concatenate([_interleave(g[:, O_Q:O_K], False), g[:, O_K:O_V], g[:, O_V:], _interleave(g[:, O_GATE:O_Q], False)], axis=1)
    return dw_in, _interleave(dw_out_p.T, False).T


def kernel(x, even_w_in, even_q_norm, even_w_uq, even_kv_norm, even_w_ukv, even_b_f, even_w_out, even_ln_g, even_ln_b, odd_w_in, odd_sinks, odd_w_out, odd_ln_g, odd_ln_b, loss_target, m_even_w_in, m_even_q_norm, m_even_w_uq, m_even_kv_norm, m_even_w_ukv, m_even_b_f, m_even_w_out, m_even_ln_g, m_even_ln_b, m_odd_w_in, m_odd_sinks, m_odd_w_out, m_odd_ln_g, m_odd_ln_b, v_even_w_in, v_even_q_norm, v_even_w_uq, v_even_kv_norm, v_even_w_ukv, v_even_b_f, v_even_w_out, v_even_ln_g, v_even_ln_b, v_odd_w_in, v_odd_sinks, v_odd_w_out, v_odd_ln_g, v_odd_ln_b):
    weights = dict(even_w_in=even_w_in, even_q_norm=even_q_norm, even_w_uq=even_w_uq, even_kv_norm=even_kv_norm,
                   even_w_ukv=even_w_ukv, even_b_f=even_b_f, even_w_out=even_w_out, even_ln_g=even_ln_g, even_ln_b=even_ln_b,
                   odd_w_in=odd_w_in, odd_sinks=odd_sinks, odd_w_out=odd_w_out, odd_ln_g=odd_ln_g, odd_ln_b=odd_ln_b)
    mom_m = dict(even_w_in=m_even_w_in, even_q_norm=m_even_q_norm, even_w_uq=m_even_w_uq, even_kv_norm=m_even_kv_norm,
                 even_w_ukv=m_even_w_ukv, even_b_f=m_even_b_f, even_w_out=m_even_w_out, even_ln_g=m_even_ln_g, even_ln_b=m_even_ln_b,
                 odd_w_in=m_odd_w_in, odd_sinks=m_odd_sinks, odd_w_out=m_odd_w_out, odd_ln_g=m_odd_ln_g, odd_ln_b=m_odd_ln_b)
    mom_v = dict(even_w_in=v_even_w_in, even_q_norm=v_even_q_norm, even_w_uq=v_even_w_uq, even_kv_norm=v_even_kv_norm,
                 even_w_ukv=v_even_w_ukv, even_b_f=v_even_b_f, even_w_out=v_even_w_out, even_ln_g=v_even_ln_g, even_ln_b=v_even_ln_b,
                 odd_w_in=v_odd_w_in, odd_sinks=v_odd_sinks, odd_w_out=v_odd_w_out, odd_ln_g=v_odd_ln_g, odd_ln_b=v_odd_ln_b)
    names = list(weights)
    xl = x[0]
    tgt = loss_target[0]
    s = xl.shape[0]
    ax, ay, ac = lax.axis_index("x"), lax.axis_index("y"), lax.axis_index("c")
    chip = 2 * ax + ay
    c_idx = ac.astype(jnp.int32).reshape(1)
    chip_idx = chip.astype(jnp.int32).reshape(1)

    ln_odd = jnp.pad(jnp.concatenate([odd_ln_g, odd_ln_b]), ((0, 12), (0, 0)))[None]
    wb = {n: weights[n].astype(BF16) for n in _SHARDED}
    even_mats = [n for n in _SHARDED if n.startswith("even")]
    odd_mats = [n for n in _SHARDED if n.startswith("odd")]
    gathered = _gather_chip_shards([wb[n][0:1] for n in even_mats] + [ln_odd], name="gather_first")
    full = {n: [_unshard(n, gathered[i])[0], None] for i, n in enumerate(even_mats)}
    later = [wb[n][1:2] for n in even_mats] + [wb[n] for n in odd_mats]
    ln_all = gathered[-1][:, 0]
    odd_g_full = ln_all[:, 0:2].transpose(1, 0, 2).reshape(2, D_MODEL)
    odd_b_full = ln_all[:, 2:4].transpose(1, 0, 2).reshape(2, D_MODEL)

    cos_m, sin_m, cos_s, sin_s = _rope_tables(s)
    bf_tiles = jnp.pad(even_b_f, ((0, 0), (FL_LANE, LANES - FL_LANE - FOX_HEADS)))
    sink_tiles = jnp.broadcast_to(_interleave(odd_sinks, True)[:, :, None, None], (2, SWA_HEADS, 1, LANES))
    mla_scale = (MLA_NOPE + MLA_ROPE) ** -0.5
    fox_scale = FOX_DIM ** -0.5
    mla_pairs, fox_pairs = MLA_HEADS // 2, FOX_HEADS // 2

    saved = []
    x_f, x_b = xl, xl.astype(BF16)
    for layer in range(DEPTH):
        j = layer // 2
        ln = f"L{layer}"
        if layer % 2 == 0:
            w_in_p, w_in_t, uq_p, uq_t, ukv_p, ukv_t = _even_weights(full["even_w_in"][j], full["even_w_uq"][j], full["even_w_ukv"][j])
            w_out, w_out_t = full["even_w_out"][j], full["even_w_out"][j].T
            qg, kg, bft = even_q_norm[j][None], even_kv_norm[j][None], bf_tiles[j][None]
            h, hb = _mm(x_b, w_in_p, out_dtype=F32, name=ln + "_in", also_bf16=True)
            q, k, v, qn, kvn, logf = _even_mid_fwd(h, qg, kg, bft, uq_p, ukv_p, cos_m, sin_m, name=ln + "_mid")
            cum = _cumsum(logf, reverse=False, name=ln + "_cum")[:, FL_LANE:FL_LANE + FOX_HEADS].T
            crow = cum[:, None, :]
            if layer == 0:
                o_mla, lse_mla, rest = _attn_fwd((q, 0), (k, 0), (v, 0), None, dk=HEAD_PAD, npairs=mla_pairs, scale=mla_scale,
                                                 name=ln + "_mla", gather=later)
                for i, n in enumerate(even_mats):
                    full[n][1] = _unshard(n, rest[i])[0]
                for i, n in enumerate(odd_mats):
                    full[n] = _unshard(n, rest[len(even_mats) + i])
            else:
                o_mla, lse_mla = _attn_fwd((q, 0), (k, 0), (v, 0), None, dk=HEAD_PAD, npairs=mla_pairs, scale=mla_scale, name=ln + "_mla")
            o_fox, lse_fox = _attn_fwd((hb, E_FQ), (hb, E_FK), (hb, E_FV), crow, dk=FOX_DIM, npairs=fox_pairs,
                                       scale=fox_scale, name=ln + "_fox")
            o_parts = [o_mla, o_fox]
            g_ln, b_ln = even_ln_g[j][None], even_ln_b[j][None]
            x_n, x_nb, z, xh, rstd = _out_fwd(o_parts, h, x_f, w_out, g_ln, b_ln, name=ln + "_out")
            saved.append(dict(h=h, hb=hb, x_b=x_b, qn=qn, kvn=kvn, q=q, k=k, v=v, crow=crow,
                              o_mla=o_mla, o_fox=o_fox, lse_mla=lse_mla, lse_fox=lse_fox, o_parts=o_parts, z=z, xh=xh, rstd=rstd,
                              w_in_t=w_in_t, uq_t=uq_t, ukv_t=ukv_t, w_out_t=w_out_t, qg=qg, kg=kg, bft=bft, g_ln=g_ln))
        else:
            w_in_p, w_in_t, w_out, w_out_t = _odd_weights(full["odd_w_in"][j], full["odd_w_out"][j])
            h = _mm(x_b, w_in_p, out_dtype=F32, name=ln + "_in")
            q, k, v = _odd_rope(h, cos_s, sin_s, name=ln + "_rope")
            kp = jnp.pad(k, ((WINDOW, 0), (0, 0)))
            vp = jnp.pad(v, ((WINDOW, 0), (0, 0)))
            o_t, lse = _swa_fwd(q, kp, vp, sink_tiles[j], name=ln + "_swa")
            g_ln, b_ln = odd_g_full[j][None], odd_b_full[j][None]
            x_n, x_nb, z, xh, rstd = _out_fwd([o_t], h, x_f, w_out, g_ln, b_ln, name=ln + "_out", o_t=True)
            saved.append(dict(h=h, x_b=x_b, q=q, kp=kp, vp=vp, lse=lse, o_t=o_t, o_parts=[o_t], z=z, xh=xh, rstd=rstd,
                              w_in_t=w_in_t, w_out_t=w_out_t, g_ln=g_ln))
        x_f, x_b = x_n, x_nb

    dxn, loss_tile = _loss_grad(x_f, tgt, name="loss")

    def pair_sums(group, tag):
        gp = [_chip_halves(n, g) for n, g in group.items()]
        from_sib = _swap_halves(gp, name="grad_swap_" + tag)
        return [_add_sibling(g, r, c_idx, chip_idx, name=f"grad_add_sibling_{n}_{tag}") for n, g, r in zip(group, gp, from_sib)]

    def chip_sums(group, pair, from_chips, tag):
        mine = [_add_chips(p[1], r, c_idx, name=f"grad_add_chips_{n}_{tag}") for n, p, r in zip(group, pair, from_chips)]
        return dict(zip(group, _join_halves(mine, name="grad_join_" + tag)))

    grads = {n: [None, None] for n in names}
    for layer in reversed(range(DEPTH)):
        j = layer // 2
        ln = f"L{layer}"
        sv = saved[layer]
        dr, dy, do, dgate, dg_ln, db_ln, *do_t = _out_bwd(dxn, sv["xh"], sv["rstd"], sv["g_ln"], sv["w_out_t"], sv["o_parts"], sv["h"],
                                                          name=ln + "_outb", o_t=layer % 2 == 1)
        dw_out = _mm(sv["z"].T, dy, out_dtype=F32, name=ln + "_dwout")
        if layer % 2 == 0:
            hb = sv["hb"]
            if layer == 0:
                early = {n: grads[n][1][None] for n in even_mats}
                early.update({n: jnp.stack(grads[n]) for n in odd_mats})
                early_pair = pair_sums(early, "early")
                dq, dk, dv, early_chips = _attn_bwd((sv["q"], 0), (sv["k"], 0), (sv["v"], 0), (sv["o_mla"], 0), (do, 0), sv["lse_mla"], None,
                                                    dk=HEAD_PAD, npairs=mla_pairs, scale=mla_scale, dq_dtype=F32, name=ln + "_mlab",
                                                    scatter=[p[0] for p in early_pair])
            else:
                dq, dk, dv = _attn_bwd((sv["q"], 0), (sv["k"], 0), (sv["v"], 0), (sv["o_mla"], 0), (do, 0), sv["lse_mla"], None,
                                       dk=HEAD_PAD, npairs=mla_pairs, scale=mla_scale, dq_dtype=F32, name=ln + "_mlab")
            dfq, dfk, dfv, dcrow, dccol = _attn_bwd((hb, E_FQ), (hb, E_FK), (hb, E_FV), (sv["o_fox"], 0), (do, MLA_HEADS * MLA_V),
                                                    sv["lse_fox"], sv["crow"], dk=FOX_DIM, npairs=fox_pairs,
                                                    scale=fox_scale, dq_dtype=BF16, name=ln + "_foxb")
            dkeys = dccol.reshape(s, fox_pairs, 2, HALF)[:, :, ::-1, 0].reshape(s, FOX_HEADS)
            dcum = jnp.pad(dcrow[:, 0, :].T + dkeys, ((0, 0), (FL_LANE, LANES - FL_LANE - FOX_HEADS)))
            dlogf = _cumsum(dcum, reverse=True, name=ln + "_cumb")
            dh_small, dq_pre, dqg, dkg, dbf = _even_mid_bwd(
                sv["h"], dq, dk, dv, dlogf, sv["qg"], sv["kg"], sv["bft"], sv["uq_t"], sv["ukv_t"], cos_m, sin_m, name=ln + "_midb")
            duq_p = _mm(sv["qn"].T, dq_pre, out_dtype=F32, name=ln + "_dwuq")
            dkv_cat = jnp.concatenate([dk.astype(BF16), dv], axis=1)
            dukv_p = _mm(sv["kvn"].T, dkv_cat, out_dtype=F32, name=ln + "_dwukv")
            dh = jnp.concatenate([dgate, dfq, dfk, dfv, dh_small], axis=1)
            dw_in_p = _mm(sv["x_b"].T, dh, out_dtype=F32, name=ln + "_dwin")
            dw_in, duq, dukv = _even_weight_grads(dw_in_p, duq_p, dukv_p)
            for n, val in (("even_w_in", dw_in), ("even_w_uq", duq), ("even_w_ukv", dukv), ("even_w_out", dw_out),
                           ("even_q_norm", dqg[0]), ("even_kv_norm", dkg[0]), ("even_b_f", dbf[0, FL_LANE:FL_LANE + FOX_HEADS]),
                           ("even_ln_g", dg_ln[0]), ("even_ln_b", db_ln[0])):
                grads[n][j] = val
        else:
            dq_t, dkp, dvp, dsink = _swa_bwd(sv["q"], sv["kp"], sv["vp"], sv["kp"].T, sink_tiles[j], sv["o_t"], do, do_t[0], sv["lse"],
                                             name=ln + "_swab")
            dq_r, dk_r = _odd_rope_bwd(dq_t, dkp[WINDOW:], cos_s, sin_s, name=ln + "_ropeb")
            dh = jnp.concatenate([dgate, dq_r, dk_r, dvp[WINDOW:].astype(BF16)], axis=1)
            dw_in_p = _mm(sv["x_b"].T, dh, out_dtype=F32, name=ln + "_dwin")
            dw_in, dw_out = _odd_weight_grads(dw_in_p, dw_out)
            for n, val in (("odd_w_in", dw_in), ("odd_w_out", dw_out), ("odd_sinks", _interleave(dsink[None, :, 0, 0], False)[0]),
                           ("odd_ln_g", dg_ln[0]), ("odd_ln_b", db_ln[0])):
                grads[n][j] = val
        dxn = _mm(dh, sv["w_in_t"], out_dtype=F32, name=ln + "_dx", res=dr, res_scale=ALPHA)
    grad_x = dxn[None]
    late = {n: grads[n][0][None] for n in even_mats}
    late_pair = pair_sums(late, "late")
    late_sum = chip_sums(late, late_pair, _scatter_to_chips([p[0] for p in late_pair], name="grad_scatter_late"), "late")
    early_sum = chip_sums(early, early_pair, early_chips, "early")
    gshard = {n: jnp.concatenate([late_sum[n], early_sum[n]]) for n in even_mats}
    gshard.update({n: early_sum[n] for n in odd_mats})

    small = [n for n in names if n not in _SHARDED]
    grads = {n: jnp.stack(grads[n]) for n in small}
    sv_flat = jnp.concatenate([grads[n].reshape(-1) for n in small] + [loss_tile[0, :1]])
    sv_real = sv_flat.shape[0]
    sv_rows = -(-sv_real // (PACK_COLS * 8)) * 8
    sv_sum = _all_reduce_small(jnp.pad(sv_flat, (0, sv_rows * PACK_COLS - sv_real)).reshape(sv_rows, PACK_COLS), name="small_all_reduce").reshape(-1)
    off = 0
    for n in small:
        size = int(np.prod(grads[n].shape))
        gfull = sv_sum[off:off + size].reshape(grads[n].shape)
        off += size
        if n in ("odd_ln_g", "odd_ln_b"):
            gfull = lax.dynamic_slice_in_dim(gfull, chip * (D_MODEL // N_CHIPS), D_MODEL // N_CHIPS, axis=1)
        gshard[n] = gfull
    loss = sv_sum[off]

    deltas, new_m, new_v = {}, {}, {}
    for n in names:
        deltas[n], new_m[n], new_v[n] = _adamw(weights[n], gshard[n], mom_m[n], mom_v[n], name="adamw_" + n)
    return (loss, grad_x, *[gshard[n] for n in names], *[deltas[n] for n in names],
            *[new_m[n] for n in names], *[new_v[n] for n in names])
```

```python
import jax
import jax.numpy as jnp
import numpy as np
from jax import lax
from jax.experimental import pallas as pl
from jax.experimental.pallas import tpu as pltpu

F32 = jnp.float32
BF16 = jnp.bfloat16
MESH = pl.DeviceIdType.MESH

D_MODEL = 1024
DEPTH = 4
ROPE_THETA = 10000.0
MLA_HEADS, MLA_NOPE, MLA_ROPE, MLA_V = 8, 64, 32, 64
MLA_Q_RANK, MLA_KV_RANK = 256, 128
FOX_HEADS, FOX_DIM = 8, 64
SWA_HEADS, SWA_KV_HEADS, SWA_DIM, WINDOW = 16, 2, 64, 128
RMS_EPS = 1e-6
LN_EPS = 1e-5
ALPHA = (2 * DEPTH) ** 0.25
EVEN_IN = 2984
ODD_IN = 2304
ADAM_LR, ADAM_B1, ADAM_B2, ADAM_EPS, ADAM_WD, ADAM_STEP = 0.001, 0.9, 0.999, 1e-08, 0.01, 10

LANES = 128
HALF = LANES // 2
HEAD_PAD = 128
N_CHIPS = 4
N_DEV = 8
E_GATE, E_FQ, E_FK, E_FV, E_SMALL = 0, 1024, 1536, 2048, 2560
E_PAD_IN = 3072
KPE_LANE = 64
FL_LANE = 96
O_GATE, O_Q, O_K, O_V = 0, 1024, 2048, 2176

_ARB = "arbitrary"
_PAR = "parallel"


def _cparams(sem):
    return pltpu.CompilerParams(dimension_semantics=sem)


def _pick(n, cands):
    for c in cands:
        if n % c == 0:
            return c
    return n


RESIDENT_BYTES = 8 << 20


def _mm_resident(a, b, *, out_dtype, name, res, res_scale, also_bf16):
    m, k = a.shape
    _, n = b.shape
    tm = _pick(m, (256, 128))
    cn = _pick(n, (512, 384, 256, 128))
    dtypes = [out_dtype, BF16] if also_bf16 else [out_dtype]

    def body(*refs):
        a_ref, b_ref = refs[:2]
        r_ref = refs[2] if res is not None else None
        outs = refs[3 if res is not None else 2:]
        av = a_ref[...].astype(BF16)
        for c0 in range(0, n, cn):
            r = jnp.dot(av, b_ref[:, c0:c0 + cn].astype(BF16), preferred_element_type=F32)
            if res is not None:
                r = r + res_scale * r_ref[:, c0:c0 + cn]
            for o_ref in outs:
                o_ref[:, c0:c0 + cn] = r.astype(o_ref.dtype)

    rows = lambda w: pl.BlockSpec((tm, w), lambda i: (i, 0))
    in_specs = [rows(k), pl.BlockSpec((k, n), lambda i: (0, 0))] + ([rows(n)] if res is not None else [])
    out = pl.pallas_call(
        body, name=name, grid=(m // tm,), in_specs=in_specs, out_specs=[rows(n)] * len(dtypes),
        out_shape=[jax.ShapeDtypeStruct((m, n), d) for d in dtypes], compiler_params=_cparams((_PAR,)),
    )(*([a, b] + ([res] if res is not None else [])))
    return out if also_bf16 else out[0]


def _mm_left_resident(a, b, *, out_dtype, name):
    m, k = a.shape
    _, n = b.shape
    tn = _pick(n, (256, 128))

    def body(a_ref, b_ref, o_ref):
        o_ref[...] = jnp.dot(a_ref[...].astype(BF16), b_ref[...].astype(BF16), preferred_element_type=F32).astype(o_ref.dtype)

    return pl.pallas_call(
        body, name=name, grid=(n // tn,),
        in_specs=[pl.BlockSpec((m, k), lambda j: (0, 0)), pl.BlockSpec((k, tn), lambda j: (0, j))],
        out_specs=pl.BlockSpec((m, tn), lambda j: (0, j)), out_shape=jax.ShapeDtypeStruct((m, n), out_dtype),
        compiler_params=_cparams((_PAR,)),
    )(a, b)


def _mm(a, b, *, out_dtype, name, res=None, res_scale=1.0, also_bf16=False):
    m, k = a.shape
    _, n = b.shape
    if b.size * b.dtype.itemsize <= RESIDENT_BYTES:
        return _mm_resident(a, b, out_dtype=out_dtype, name=name, res=res, res_scale=res_scale, also_bf16=also_bf16)
    if a.size * a.dtype.itemsize <= RESIDENT_BYTES and res is None and not also_bf16:
        return _mm_left_resident(a, b, out_dtype=out_dtype, name=name)
    tm = _pick(m, (512, 256, 128))
    tn = _pick(n, (1024, 768, 512, 384, 256, 128))
    tk = _pick(k, (1024, 768, 512, 256, 128))
    nk = k // tk

    def body(*refs):
        refs = list(refs)
        a_ref, b_ref = refs[:2]
        r_ref = refs[2] if res is not None else None
        acc_ref = refs[-1]
        outs = refs[3 if res is not None else 2:-1]
        kk = pl.program_id(2)

        @pl.when(kk == 0)
        def _():
            acc_ref[...] = jnp.zeros_like(acc_ref)

        acc_ref[...] += jnp.dot(a_ref[...].astype(BF16), b_ref[...].astype(BF16), preferred_element_type=F32)

        @pl.when(kk == nk - 1)
        def _():
            r = acc_ref[...]
            if res is not None:
                r = r + res_scale * r_ref[...]
            for o_ref in outs:
                o_ref[...] = r.astype(o_ref.dtype)

    in_specs = [pl.BlockSpec((tm, tk), lambda i, j, kk: (i, kk)), pl.BlockSpec((tk, tn), lambda i, j, kk: (kk, j))]
    args = [a, b]
    if res is not None:
        in_specs.append(pl.BlockSpec((tm, tn), lambda i, j, kk: (i, j)))
        args.append(res)
    ospec = pl.BlockSpec((tm, tn), lambda i, j, kk: (i, j))
    dtypes = [out_dtype, BF16] if also_bf16 else [out_dtype]
    out = pl.pallas_call(
        body, name=name, grid=(m // tm, n // tn, nk), in_specs=in_specs,
        out_specs=[ospec] * len(dtypes), out_shape=[jax.ShapeDtypeStruct((m, n), d) for d in dtypes],
        scratch_shapes=[pltpu.VMEM((tm, tn), F32)],
        compiler_params=_cparams((_PAR, _PAR, _ARB)),
    )(*args)
    return out if also_bf16 else out[0]


def _rope_tile(t, cos, sin, half):
    w = t.shape[-1]
    lane = lax.broadcasted_iota(jnp.int32, t.shape, 1)
    first = (lane % (2 * half)) < half
    sw = jnp.where(first, pltpu.roll(t, w - half, 1), pltpu.roll(t, half, 1))
    return t * cos + sw * sin


def _sigmoid(x):
    return 1.0 / (1.0 + jnp.exp(-x))


def _lane_mask(shape, lo, hi):
    lane = lax.broadcasted_iota(jnp.int32, shape, 1)
    return (lane >= lo) & (lane < hi)


def _rms(x, g):
    r = lax.rsqrt(jnp.mean(x * x, axis=-1, keepdims=True) + RMS_EPS)
    u = x * r
    return u, r, u * g


def _rms_bwd(dy, u, r, g):
    dyg = dy * g
    dx = r * (dyg - u * jnp.mean(dyg * u, axis=-1, keepdims=True))
    return dx, jnp.sum(dy * u, axis=0, keepdims=True)


def _even_mid_fwd(h, qg, kg, bf_tile, w_uq, w_ukv, cos, sin, *, name):
    s = h.shape[0]
    tb = _pick(s, (512, 256, 128))
    nq = MLA_HEADS * HEAD_PAD

    def body(h_ref, qg_ref, kg_ref, bf_ref, wuq_ref, wukv_ref, cos_ref, sin_ref,
             q_ref, k_ref, v_ref, qn_ref, kvn_ref, lf_ref):
        hb = h_ref[...]
        cq, ckv, misc = hb[:, :MLA_Q_RANK], hb[:, MLA_Q_RANK:MLA_Q_RANK + MLA_KV_RANK], hb[:, MLA_Q_RANK + MLA_KV_RANK:]
        cs, sn = cos_ref[...], sin_ref[...]
        _, _, qn = _rms(cq, qg_ref[...])
        qn = qn.astype(BF16)
        qn_ref[...] = qn
        q = jnp.dot(qn, wuq_ref[...], preferred_element_type=F32)
        _, _, kvn = _rms(ckv, kg_ref[...])
        kvn = kvn.astype(BF16)
        kvn_ref[...] = kvn
        kv = jnp.dot(kvn, wukv_ref[...], preferred_element_type=F32)
        kpe = jnp.where(_lane_mask(misc.shape, KPE_LANE, KPE_LANE + MLA_ROPE), _rope_tile(misc, cs, sn, MLA_ROPE // 2), 0.0)
        for hd in range(MLA_HEADS):
            sl = slice(hd * HEAD_PAD, (hd + 1) * HEAD_PAD)
            q_ref[:, sl] = _rope_tile(q[:, sl], cs, sn, MLA_ROPE // 2).astype(BF16)
            k_ref[:, sl] = (kv[:, sl] + kpe).astype(BF16)
        v_ref[...] = kv[:, nq:].astype(BF16)
        xf = misc + bf_ref[...]
        logf = jnp.minimum(xf, 0.0) - jnp.log(1.0 + jnp.exp(-jnp.abs(xf)))
        lf_ref[...] = jnp.where(_lane_mask(misc.shape, FL_LANE, FL_LANE + FOX_HEADS), logf, 0.0)

    full = lambda a: pl.BlockSpec(a.shape, lambda i: (0,) * a.ndim)
    rows = lambda w, c=0: pl.BlockSpec((tb, w), lambda i: (i, c))
    return pl.pallas_call(
        body, name=name, grid=(s // tb,),
        in_specs=[rows(512, E_SMALL // 512), full(qg), full(kg), full(bf_tile), full(w_uq), full(w_ukv), rows(LANES), rows(LANES)],
        out_specs=[rows(nq), rows(nq), rows(MLA_HEADS * MLA_V), rows(MLA_Q_RANK), rows(MLA_KV_RANK), rows(LANES)],
        out_shape=[jax.ShapeDtypeStruct((s, nq), BF16), jax.ShapeDtypeStruct((s, nq), BF16),
                   jax.ShapeDtypeStruct((s, MLA_HEADS * MLA_V), BF16), jax.ShapeDtypeStruct((s, MLA_Q_RANK), BF16),
                   jax.ShapeDtypeStruct((s, MLA_KV_RANK), BF16), jax.ShapeDtypeStruct((s, LANES), F32)],
        compiler_params=_cparams((_PAR,)),
    )(h, qg, kg, bf_tile, w_uq, w_ukv, cos, sin)


def _even_mid_bwd(h, dq, dk, dv, dlogf, qg, kg, bf_tile, w_uq_t, w_ukv_t, cos, sin, *, name):
    s = h.shape[0]
    tb = _pick(s, (256, 128))
    nq = MLA_HEADS * HEAD_PAD

    def body(h_ref, dq_ref, dk_ref, dv_ref, dlf_ref, qg_ref, kg_ref, bf_ref, wuqt_ref, wukvt_ref, cos_ref, sin_ref,
             dh_ref, dqp_ref, dqg_ref, dkg_ref, dbf_ref):
        @pl.when(pl.program_id(0) == 0)
        def _():
            dqg_ref[...] = jnp.zeros_like(dqg_ref)
            dkg_ref[...] = jnp.zeros_like(dkg_ref)
            dbf_ref[...] = jnp.zeros_like(dbf_ref)

        hb = h_ref[...]
        cq, ckv, misc = hb[:, :MLA_Q_RANK], hb[:, MLA_Q_RANK:MLA_Q_RANK + MLA_KV_RANK], hb[:, MLA_Q_RANK + MLA_KV_RANK:]
        cs, sn = cos_ref[...], -sin_ref[...]
        dkpe = jnp.zeros(misc.shape, F32)
        for hd in range(MLA_HEADS):
            sl = slice(hd * HEAD_PAD, (hd + 1) * HEAD_PAD)
            dqp_ref[:, sl] = _rope_tile(dq_ref[:, sl], cs, sn, MLA_ROPE // 2).astype(BF16)
            dkpe = dkpe + dk_ref[:, sl]
        dqn = jnp.dot(dqp_ref[...], wuqt_ref[...], preferred_element_type=F32)
        uq, rq, _ = _rms(cq, qg_ref[...])
        dcq, dqg = _rms_bwd(dqn, uq, rq, qg_ref[...])
        dqg_ref[...] += dqg
        dkv = jnp.concatenate([dk_ref[...].astype(BF16), dv_ref[...]], axis=1)
        dkvn = jnp.dot(dkv, wukvt_ref[...], preferred_element_type=F32)
        uk, rk, _ = _rms(ckv, kg_ref[...])
        dckv, dkg = _rms_bwd(dkvn, uk, rk, kg_ref[...])
        dkg_ref[...] += dkg
        dmisc = jnp.where(_lane_mask(misc.shape, KPE_LANE, KPE_LANE + MLA_ROPE), _rope_tile(dkpe, cs, sn, MLA_ROPE // 2), 0.0)
        dfl = jnp.where(_lane_mask(misc.shape, FL_LANE, FL_LANE + FOX_HEADS), dlf_ref[...] * _sigmoid(-(misc + bf_ref[...])), 0.0)
        dbf_ref[...] += jnp.sum(dfl, axis=0, keepdims=True)
        dh_ref[:, :MLA_Q_RANK] = dcq.astype(BF16)
        dh_ref[:, MLA_Q_RANK:MLA_Q_RANK + MLA_KV_RANK] = dckv.astype(BF16)
        dh_ref[:, MLA_Q_RANK + MLA_KV_RANK:] = (dmisc + dfl).astype(BF16)

    full = lambda a: pl.BlockSpec(a.shape, lambda i: (0,) * a.ndim)
    rows = lambda w, c=0: pl.BlockSpec((tb, w), lambda i: (i, c))
    return pl.pallas_call(
        body, name=name, grid=(s // tb,),
        in_specs=[rows(512, E_SMALL // 512), rows(nq), rows(nq), rows(MLA_HEADS * MLA_V), rows(LANES), full(qg), full(kg),
                  full(bf_tile), full(w_uq_t), full(w_ukv_t), rows(LANES), rows(LANES)],
        out_specs=[rows(512), rows(nq), full(qg), full(kg), full(bf_tile)],
        out_shape=[jax.ShapeDtypeStruct((s, 512), BF16), jax.ShapeDtypeStruct((s, nq), BF16),
                   jax.ShapeDtypeStruct(qg.shape, F32), jax.ShapeDtypeStruct(kg.shape, F32),
                   jax.ShapeDtypeStruct(bf_tile.shape, F32)],
        compiler_params=_cparams((_ARB,)),
    )(h, dq, dk, dv, dlogf, qg, kg, bf_tile, w_uq_t, w_ukv_t, cos, sin)


def _cumsum(x, *, reverse, name):
    s = x.shape[0]
    tb = _pick(s, (512, 256, 128))
    nb = s // tb

    def body(x_ref, o_ref, carry_ref):
        @pl.when(pl.program_id(0) == 0)
        def _():
            carry_ref[...] = jnp.zeros_like(carry_ref)

        xv = x_ref[...]
        r = lax.broadcasted_iota(jnp.int32, (tb, tb), 0)
        c = lax.broadcasted_iota(jnp.int32, (tb, tb), 1)
        tri = jnp.where((c >= r) if reverse else (c <= r), 1.0, 0.0).astype(BF16)
        hi = xv.astype(BF16)
        r1 = xv - hi.astype(F32)
        mid = r1.astype(BF16)
        lo = (r1 - mid.astype(F32)).astype(BF16)
        cs = (jnp.dot(tri, hi, preferred_element_type=F32) + jnp.dot(tri, mid, preferred_element_type=F32)
              + jnp.dot(tri, lo, preferred_element_type=F32)) + carry_ref[...]
        o_ref[...] = cs
        carry_ref[...] = cs[0:1, :] if reverse else cs[tb - 1:tb, :]

    imap = (lambda i: (nb - 1 - i, 0)) if reverse else (lambda i: (i, 0))
    return pl.pallas_call(
        body, name=name, grid=(nb,), in_specs=[pl.BlockSpec((tb, LANES), imap)],
        out_specs=pl.BlockSpec((tb, LANES), imap), out_shape=jax.ShapeDtypeStruct(x.shape, F32),
        scratch_shapes=[pltpu.VMEM((1, LANES), F32)], compiler_params=_cparams((_ARB,)),
    )(x)


_NT = (((1,), (1,)), ((), ()))
_TN = (((0,), (0,)), ((), ()))


def _head_sel(x, h, dk):
    if dk == LANES:
        return x[:, h * LANES:(h + 1) * LANES]
    return jnp.where(_lane_mask(x.shape, h * HALF, (h + 1) * HALF), x, jnp.zeros_like(x))


def _head_other(x, h, dk):
    return x[:, h * LANES:(h + 1) * LANES] if dk == LANES else x


def _pair(parts, dk=HALF):
    if dk == LANES:
        return jnp.concatenate(parts, axis=1)
    return jnp.where(_lane_mask(parts[0].shape, 0, HALF), parts[0], parts[1])


def _attn_fwd(q, k, v, crow, *, dk, npairs, scale, name, gather=()):
    (qa, qo), (ka, ko), (va, vo) = q, k, v
    s = qa.shape[0]
    wq = 2 * dk
    t = _pick(s, (512, 256, 128))
    nb = s // t
    bias = crow is not None
    ng = len(gather)

    def body(*refs):
        n_in = 3 + bias
        q_ref, k_ref, v_ref = refs[:3]
        cr_ref = refs[3] if bias else None
        o_ref, lse_ref = refs[n_in + ng:n_in + ng + 2]
        qi = pl.program_id(1)
        if ng:
            phases = _gather_phases([a.shape for a in gather], refs[n_in:n_in + ng], refs[n_in + ng + 2:n_in + 2 * ng + 2],
                                    *refs[n_in + 2 * ng + 2:])
            pair = pl.program_id(0)
            pl.when((pair == 0) & (qi == 0))(phases[0])
            pl.when((pair == npairs - 1) & (qi == 0))(phases[1])
        qb = q_ref[...]
        qs = [_head_sel(qb, h, dk) for h in range(2)]

        def scores(j):
            rows = pl.ds(pl.multiple_of(j * t, t), t)
            kb = k_ref[rows, :]
            out = []
            for h in range(2):
                sc = lax.dot_general(qs[h], _head_other(kb, h, dk), _NT, preferred_element_type=F32) * scale
                out.append(sc - cr_ref[h, j] if bias else sc)
            return tuple(out)

        def update(scs, j, state):
            rows = pl.ds(pl.multiple_of(j * t, t), t)
            vb = v_ref[rows, :]
            soft = []
            for h in range(2):
                m, l, _ = state[3 * h:3 * h + 3]
                sc = scs[h]
                m_new = jnp.maximum(m, jnp.max(sc, axis=1, keepdims=True))
                a = jnp.exp(m - m_new)
                p = jnp.exp(sc - m_new)
                soft.append((m_new, a * l + jnp.sum(p, axis=1, keepdims=True), a, p.astype(BF16)))
            new = []
            for h in range(2):
                m_new, l, a, p = soft[h]
                new += [m_new, l, a * state[3 * h + 2] + jnp.dot(p, vb, preferred_element_type=F32)]
            return tuple(new)

        one = (jnp.full((t, 1), -jnp.inf, F32), jnp.zeros((t, 1), F32), jnp.zeros((t, LANES), F32))
        state = lax.fori_loop(0, qi, lambda j, st: update(scores(j), j, st), one + one)
        row = lax.broadcasted_iota(jnp.int32, (t, t), 0)
        col = lax.broadcasted_iota(jnp.int32, (t, t), 1)
        diag = tuple(jnp.where(col <= row, sc, -jnp.inf) for sc in scores(qi))
        state = update(diag, qi, state)
        outs = []
        for h in range(2):
            m, l, acc = state[3 * h:3 * h + 3]
            outs.append(acc / l)
            lse_ref[h] = m + jnp.log(l)
        o_ref[...] = _pair(outs)
        if ng:
            pl.when((pair == npairs - 1) & (qi == nb - 1))(phases[2])

    in_specs = [pl.BlockSpec((t, wq), lambda p, i: (i, qo // wq + p)), pl.BlockSpec((s, wq), lambda p, i: (0, ko // wq + p)),
                pl.BlockSpec((s, LANES), lambda p, i: (0, vo // LANES + p))]
    args = [qa, ka, va]
    if bias:
        in_specs.append(pl.BlockSpec((2, nb, 1, t), lambda p, i: (p, 0, 0, 0)))
        args.append(crow.reshape(2 * npairs, nb, 1, t))
    out = pl.pallas_call(
        body, name=name, grid=(npairs, nb), in_specs=in_specs + [_ANY] * ng,
        out_specs=[pl.BlockSpec((t, LANES), lambda p, i: (i, p)), pl.BlockSpec((2, t, 1), lambda p, i: (p, i, 0))] + [_ANY] * ng,
        out_shape=[jax.ShapeDtypeStruct((s, npairs * LANES), F32), jax.ShapeDtypeStruct((2 * npairs, s, 1), F32)]
        + [jax.ShapeDtypeStruct((N_CHIPS,) + a.shape, a.dtype) for a in gather],
        scratch_shapes=_sem_pairs(GATHER_COPIES * ng) if ng else [],
        compiler_params=_cparams((_ARB, _ARB) if ng else (_PAR, _PAR)),
    )(*args, *gather)
    return (out[0], out[1], out[2:]) if ng else (out[0], out[1])


def _attn_bwd(q, k, v, o, do, lse, crow, *, dk, npairs, scale, dq_dtype, name, scatter=()):
    (qa, qo), (ka, ko), (va, vo), (oa, oo), (da, do_o) = q, k, v, o, do
    s = qa.shape[0]
    wq = 2 * dk
    t = _pick(s, (512, 256, 128))
    nb = s // t
    bias = crow is not None
    ns = len(scatter)

    def body(*refs):
        n_in, n_out = 6 + bias, 3 + 2 * bias
        q_ref, k_ref, v_ref, o_ref, do_ref, lse_ref = refs[:6]
        outs = refs[n_in + ns:n_in + ns + n_out]
        scr = refs[n_in + 2 * ns + n_out:]
        dq_ref, dk_ref, dv_ref = outs[:3]
        dq_s, dk_s, dv_s = scr[:3]
        if bias:
            cr_ref, dc_ref, dcc_ref, dc_s, rs_s = refs[6], outs[3], outs[4], scr[3], scr[4]
        ki, qi = pl.program_id(1), pl.program_id(2)
        if ns:
            send_off, finish = _scatter_phases(ns, refs[n_in:n_in + ns], refs[n_in + ns + n_out:n_in + 2 * ns + n_out], *scr[3 + 2 * bias:])
            pair = pl.program_id(0)
            pl.when((pair == 0) & (ki == 0) & (qi == 0))(send_off)

        @pl.when((ki == 0) & (qi == 0))
        def _():
            dq_s[...] = jnp.zeros_like(dq_s)
            if bias:
                rs_s[...] = jnp.zeros_like(rs_s)

        @pl.when(qi == ki)
        def _():
            dk_s[...] = jnp.zeros_like(dk_s)
            dv_s[...] = jnp.zeros_like(dv_s)
            if bias:
                dc_s[...] = jnp.zeros_like(dc_s)

        def block(on_diagonal):
            qb, kb, vb, dob, ob = q_ref[...], k_ref[...], v_ref[...], do_ref[...], o_ref[...]
            rows = pl.ds(pl.multiple_of(qi * t, t), t)
            row = lax.broadcasted_iota(jnp.int32, (t, t), 0) + qi * t
            col = lax.broadcasted_iota(jnp.int32, (t, t), 1) + ki * t
            fed = []
            for h in range(2):
                kh, doh = _head_other(kb, h, dk), _head_sel(dob, h, HALF)
                sc = lax.dot_general(_head_sel(qb, h, dk), kh, _NT, preferred_element_type=F32)
                dp = lax.dot_general(doh, vb, _NT, preferred_element_type=F32)
                fed.append((kh, doh, sc, dp))
            mid = []
            for h in range(2):
                kh, doh, sc, dp = fed[h]
                sc = sc * scale
                if bias:
                    sc = sc - cr_ref[h]
                if on_diagonal:
                    sc = jnp.where(col <= row, sc, -jnp.inf)
                p = jnp.exp(sc - lse_ref[h])
                delta = jnp.sum(doh.astype(F32) * ob, axis=1, keepdims=True)
                ds = p * (dp - delta)
                mid.append((kh, p.astype(BF16), (ds * scale).astype(BF16)))
            dq_parts, dk_parts, dv_parts = [], [], []
            for h in range(2):
                kh, pb, dsb = mid[h]
                dv_parts.append(lax.dot_general(pb, dob, _TN, preferred_element_type=F32))
                dk_parts.append(lax.dot_general(dsb, _head_other(qb, h, dk), _TN, preferred_element_type=F32))
                if bias:
                    dc_s[h] -= jnp.sum(dsb.astype(F32), axis=0, keepdims=True) * (1.0 / scale)
                    kh = jnp.where(_lane_mask(kh.shape, h * HALF, (h + 1) * HALF), kh, jnp.ones_like(kh))
                dq_parts.append(jnp.dot(dsb, kh, preferred_element_type=F32))
            dv_s[...] += _pair(dv_parts)
            dk_s[...] += _pair(dk_parts, dk)
            dq_s[rows, :] += _pair(dq_parts, dk)
            if bias:
                rs_s[rows, :] += _pair(dq_parts[::-1])

        pl.when(qi > ki)(lambda: block(False))
        pl.when(qi == ki)(lambda: block(True))

        @pl.when(qi == nb - 1)
        def _():
            dk_ref[...] = dk_s[...].astype(dk_ref.dtype)
            dv_ref[...] = dv_s[...].astype(dv_ref.dtype)
            if bias:
                dc_ref[...] = dc_s[...]

        @pl.when((ki == nb - 1) & (qi == nb - 1))
        def _():
            dq_ref[...] = dq_s[...].astype(dq_ref.dtype)
            if bias:
                dcc_ref[...] = rs_s[...] * (1.0 / scale)

        if ns:
            pl.when((pair == npairs - 1) & (ki == nb - 1) & (qi == nb - 1))(finish)

    qrow = lambda p, j, i: jnp.maximum(i, j)
    in_specs = [pl.BlockSpec((t, wq), lambda p, j, i: (qrow(p, j, i), qo // wq + p)),
                pl.BlockSpec((t, wq), lambda p, j, i: (j, ko // wq + p)),
                pl.BlockSpec((t, LANES), lambda p, j, i: (j, vo // LANES + p)),
                pl.BlockSpec((t, LANES), lambda p, j, i: (qrow(p, j, i), oo // LANES + p)),
                pl.BlockSpec((t, LANES), lambda p, j, i: (qrow(p, j, i), do_o // LANES + p)),
                pl.BlockSpec((2, t, 1), lambda p, j, i: (p, qrow(p, j, i), 0))]
    args = [qa, ka, va, oa, da, lse]
    out_specs = [pl.BlockSpec((s, wq), lambda p, j, i: (0, p)), pl.BlockSpec((t, wq), lambda p, j, i: (j, p)),
                 pl.BlockSpec((t, LANES), lambda p, j, i: (j, p))]
    out_shape = [jax.ShapeDtypeStruct((s, npairs * wq), dq_dtype), jax.ShapeDtypeStruct((s, npairs * wq), dq_dtype),
                 jax.ShapeDtypeStruct((s, npairs * LANES), BF16)]
    scratch = [pltpu.VMEM((s, wq), F32), pltpu.VMEM((t, wq), F32), pltpu.VMEM((t, LANES), F32)]
    if bias:
        in_specs.append(pl.BlockSpec((2, 1, t), lambda p, j, i: (p, 0, j)))
        args.append(crow)
        assert dk == HALF, "the sums over the keys ride in the unused half of the dq product of a 64-wide head pair"
        out_specs += [pl.BlockSpec((2, 1, t), lambda p, j, i: (p, 0, j)), pl.BlockSpec((s, LANES), lambda p, j, i: (0, p))]
        out_shape += [jax.ShapeDtypeStruct((2 * npairs, 1, s), F32), jax.ShapeDtypeStruct((s, npairs * LANES), F32)]
        scratch += [pltpu.VMEM((2, 1, t), F32), pltpu.VMEM((s, LANES), F32)]
    n_out = len(out_shape)
    out = pl.pallas_call(
        body, name=name, grid=(npairs, nb, nb), in_specs=in_specs + [_ANY] * ns, out_specs=out_specs + [_ANY] * ns,
        out_shape=out_shape + [jax.ShapeDtypeStruct((3,) + a.shape[1:], a.dtype) for a in scatter],
        scratch_shapes=scratch + (_sem_pairs(3 * ns) if ns else []),
        compiler_params=_cparams((_ARB if ns else _PAR, _ARB, _ARB)),
    )(*args, *scatter)
    return (*out[:n_out], out[n_out:]) if ns else out


def _sub_mask(shape, lo, hi):
    sub = lax.broadcasted_iota(jnp.int32, shape, 0)
    return (sub >= lo) & (sub < hi)


def _pair_rows(parts):
    return jnp.where(_sub_mask(parts[0].shape, 0, HALF), parts[0], parts[1])


def _swa_valid(start, t):
    krow = lax.broadcasted_iota(jnp.int32, (t + WINDOW, t), 0)
    qcol = lax.broadcasted_iota(jnp.int32, (t + WINDOW, t), 1)
    diff = qcol - krow + WINDOW
    return (diff >= 0) & (diff < WINDOW) & (krow + start >= WINDOW)


def _swa_tiles(s, most):
    t = _pick(s, (256, 128))
    return t, _pick(s // t, tuple(n for n in (8, 4, 2, 1) if n <= most))


def _swa_fwd(q, kp, vp, sink, *, name):
    s = q.shape[0]
    npairs = SWA_HEADS // 2
    t, nt = _swa_tiles(s, 8)
    scale = SWA_DIM ** -0.5

    def body(q_ref, k_ref, v_ref, sk_ref, o_ref, lse_ref):
        sts, vws = [], []
        for u in range(nt):
            start = pl.multiple_of((pl.program_id(1) * nt + u) * t, t)
            kw = k_ref[pl.ds(start, t + WINDOW), :]
            vws.append(v_ref[pl.ds(start, t + WINDOW), :])
            qb = q_ref[u * t:(u + 1) * t, :]
            valid = _swa_valid(start, t)
            for h in range(2):
                st = lax.dot_general(kw, _head_sel(qb, h, HALF), _NT, preferred_element_type=F32) * scale
                sts.append(jnp.where(valid, st, -jnp.inf))
        pts = []
        for u in range(nt):
            for h in range(2):
                st = sts[2 * u + h]
                snk = sk_ref[h][:, 0:1]
                m = jnp.maximum(jnp.max(st, axis=0, keepdims=True), snk)
                e = jnp.exp(st - m)
                l = jnp.sum(e, axis=0, keepdims=True) + jnp.exp(snk - m)
                pts.append((e * (1.0 / l)).astype(BF16))
                lse_ref[h, :, u * t:(u + 1) * t] = m + jnp.log(l)
        for u in range(nt):
            outs = [lax.dot_general(vws[u], pts[2 * u + h], _TN, preferred_element_type=F32) for h in range(2)]
            o_ref[:, u * t:(u + 1) * t] = _pair_rows(outs)

    kvspec = pl.BlockSpec((s + WINDOW, LANES), lambda p, i: (0, 0))
    return pl.pallas_call(
        body, name=name, grid=(npairs, s // (t * nt)),
        in_specs=[pl.BlockSpec((t * nt, LANES), lambda p, i: (i, p)), kvspec, kvspec, pl.BlockSpec((2, 1, LANES), lambda p, i: (p, 0, 0))],
        out_specs=[pl.BlockSpec((LANES, t * nt), lambda p, i: (p, i)), pl.BlockSpec((2, 1, t * nt), lambda p, i: (p, 0, i))],
        out_shape=[jax.ShapeDtypeStruct((npairs * LANES, s), F32), jax.ShapeDtypeStruct((SWA_HEADS, 1, s), F32)],
        compiler_params=_cparams((_PAR, _PAR)),
    )(q, kp, vp, sink)


def _swa_bwd(q, kp, vp, kpt, sink, ot, do, dot, lse, *, name):
    s = q.shape[0]
    npairs = SWA_HEADS // 2
    t, nt = _swa_tiles(s, 4)
    scale = SWA_DIM ** -0.5

    def body(q_ref, k_ref, v_ref, kt_ref, sk_ref, ot_ref, do_ref, dot_ref, lse_ref, dq_ref, dk_ref, dv_ref, dsk_ref):
        pp, i = pl.program_id(0), pl.program_id(1)

        @pl.when((pp == 0) & (i == 0))
        def _():
            dk_ref[...] = jnp.zeros_like(dk_ref)
            dv_ref[...] = jnp.zeros_like(dv_ref)

        @pl.when(i == 0)
        def _():
            dsk_ref[...] = jnp.zeros_like(dsk_ref)

        fed = []
        for u in range(nt):
            cols = slice(u * t, (u + 1) * t)
            start = pl.multiple_of((i * nt + u) * t, t)
            win = pl.ds(start, t + WINDOW)
            qb, dob = q_ref[cols, :], do_ref[cols, :]
            kw, vw = k_ref[win, :], v_ref[win, :]
            valid = _swa_valid(start, t)
            for h in range(2):
                st = lax.dot_general(kw, _head_sel(qb, h, HALF), _NT, preferred_element_type=F32)
                dpt = lax.dot_general(vw, _head_sel(dob, h, HALF), _NT, preferred_element_type=F32)
                fed.append((cols, win, qb, dob, valid, st, dpt))
        mid = []
        for u in range(nt):
            cols = fed[2 * u][0]
            prod = dot_ref[:, cols].astype(F32) * ot_ref[:, cols]
            for h in range(2):
                valid, st, dpt = fed[2 * u + h][4:]
                lse_b = lse_ref[h, :, cols]
                pt = jnp.exp(jnp.where(valid, st * scale, -jnp.inf) - lse_b)
                delta = jnp.sum(jnp.where(_sub_mask(prod.shape, h * HALF, (h + 1) * HALF), prod, 0.0), axis=0, keepdims=True)
                dst = pt * (dpt - delta)
                psink = jnp.exp(sk_ref[h][:, 0:1] - lse_b)
                dsk_ref[h] += jnp.broadcast_to(-jnp.sum(psink * delta, axis=1, keepdims=True), (1, LANES))
                mid.append((pt.astype(BF16), (dst * scale).astype(BF16)))
        for u in range(nt):
            cols, win, qb, dob = fed[2 * u][:4]
            ktw = kt_ref[:, win]
            dq_parts = [jnp.dot(ktw, mid[2 * u + h][1], preferred_element_type=F32) for h in range(2)]
            dk_parts = [jnp.dot(mid[2 * u + h][1], qb, preferred_element_type=F32) for h in range(2)]
            dv_parts = [jnp.dot(mid[2 * u + h][0], dob, preferred_element_type=F32) for h in range(2)]
            dq_ref[:, cols] = _pair_rows(dq_parts)
            dk_ref[win, :] += _pair(dk_parts)
            dv_ref[win, :] += _pair(dv_parts)

    tile = pl.BlockSpec((t * nt, LANES), lambda p, i: (i, p))
    ttile = pl.BlockSpec((LANES, t * nt), lambda p, i: (p, i))
    kvspec = pl.BlockSpec((s + WINDOW, LANES), lambda p, i: (0, 0))
    ktspec = pl.BlockSpec((LANES, s + WINDOW), lambda p, i: (0, 0))
    skspec = pl.BlockSpec((2, 1, LANES), lambda p, i: (p, 0, 0))
    return pl.pallas_call(
        body, name=name, grid=(npairs, s // (t * nt)),
        in_specs=[tile, kvspec, kvspec, ktspec, skspec, ttile, tile, ttile, pl.BlockSpec((2, 1, t * nt), lambda p, i: (p, 0, i))],
        out_specs=[ttile, kvspec, kvspec, skspec],
        out_shape=[jax.ShapeDtypeStruct((npairs * LANES, s), F32), jax.ShapeDtypeStruct((s + WINDOW, LANES), F32),
                   jax.ShapeDtypeStruct((s + WINDOW, LANES), F32), jax.ShapeDtypeStruct((SWA_HEADS, 1, LANES), F32)],
        compiler_params=_cparams((_ARB, _ARB)),
    )(q, kp, vp, kpt, sink, ot, do, dot, lse)


def _odd_rope(h, cos, sin, *, name):
    s = h.shape[0]
    tb = _pick(s, (512, 256, 128))
    nq = SWA_HEADS * SWA_DIM

    def body(q_ref, kv_ref, cos_ref, sin_ref, qo_ref, ko_ref, vo_ref):
        cs, sn = cos_ref[...], sin_ref[...]
        for j in range(nq // LANES):
            sl = slice(j * LANES, (j + 1) * LANES)
            qo_ref[:, sl] = _rope_tile(q_ref[:, sl], cs, sn, SWA_DIM // 2).astype(BF16)
        ko_ref[...] = _rope_tile(kv_ref[:, :LANES], cs, sn, SWA_DIM // 2).astype(BF16)
        vo_ref[...] = kv_ref[:, LANES:].astype(BF16)

    rows = lambda w, c=0: pl.BlockSpec((tb, w), lambda i: (i, c))
    return pl.pallas_call(
        body, name=name, grid=(s // tb,),
        in_specs=[rows(nq, O_Q // nq), rows(2 * LANES, O_K // (2 * LANES)), rows(LANES), rows(LANES)],
        out_specs=[rows(nq), rows(LANES), rows(LANES)],
        out_shape=[jax.ShapeDtypeStruct((s, nq), BF16), jax.ShapeDtypeStruct((s, LANES), BF16), jax.ShapeDtypeStruct((s, LANES), BF16)],
        compiler_params=_cparams((_PAR,)),
    )(h, h, cos, sin)


def _odd_rope_bwd(dq_t, dk, cos, sin, *, name):
    s = dk.shape[0]
    tb = _pick(s, (256, 128))
    nq = SWA_HEADS * SWA_DIM

    def body(dq_ref, dk_ref, cos_ref, sin_ref, qo_ref, ko_ref):
        cs, sn = cos_ref[...], -sin_ref[...]
        dq = dq_ref[...].T
        for j in range(nq // LANES):
            sl = slice(j * LANES, (j + 1) * LANES)
            qo_ref[:, sl] = _rope_tile(dq[:, sl], cs, sn, SWA_DIM // 2).astype(BF16)
        ko_ref[...] = _rope_tile(dk_ref[...], cs, sn, SWA_DIM // 2).astype(BF16)

    rows = lambda w: pl.BlockSpec((tb, w), lambda i: (i, 0))
    return pl.pallas_call(
        body, name=name, grid=(s // tb,),
        in_specs=[pl.BlockSpec((nq, tb), lambda i: (0, i)), rows(LANES), rows(LANES), rows(LANES)],
        out_specs=[rows(nq), rows(LANES)],
        out_shape=[jax.ShapeDtypeStruct((s, nq), BF16), jax.ShapeDtypeStruct((s, LANES), BF16)],
        compiler_params=_cparams((_PAR,)),
    )(dq_t, dk, cos, sin)


def _mixer_out(o_refs, o_t):
    if o_t:
        return o_refs[0][...].T
    return o_refs[0][...] if len(o_refs) == 1 else jnp.concatenate([r[...] for r in o_refs], axis=1)


def _mixer_specs(o_parts, o_t, tb):
    if o_t:
        return [pl.BlockSpec((D_MODEL, tb), lambda i: (0, i))]
    return [pl.BlockSpec((tb, a.shape[1]), lambda i: (i, 0)) for a in o_parts]


def _out_fwd(o_parts, h, x, w_out, g, b, *, name, o_t=False):
    s = x.shape[0]
    tb = _pick(s, (256, 128))
    n_o = len(o_parts)

    def body(*refs):
        o_refs = refs[:n_o]
        gate_ref, x_ref, w_ref, g_ref, b_ref, xn_ref, xb_ref, z_ref, xh_ref, rs_ref = refs[n_o:]
        gate = gate_ref[...]
        o = _mixer_out(o_refs, o_t)
        z = (o * (gate * _sigmoid(gate))).astype(BF16)
        z_ref[...] = z
        r = ALPHA * x_ref[...] + jnp.dot(z, w_ref[...], preferred_element_type=F32)
        mu = jnp.mean(r, axis=-1, keepdims=True)
        rc = r - mu
        rstd = lax.rsqrt(jnp.mean(rc * rc, axis=-1, keepdims=True) + LN_EPS)
        xh = rc * rstd
        xn = xh * g_ref[...] + b_ref[...]
        xh_ref[...] = xh
        rs_ref[...] = rstd
        xn_ref[...] = xn
        xb_ref[...] = xn.astype(BF16)

    rows = lambda w: pl.BlockSpec((tb, w), lambda i: (i, 0))
    full = lambda a: pl.BlockSpec(a.shape, lambda i: (0,) * a.ndim)
    return pl.pallas_call(
        body, name=name, grid=(s // tb,),
        in_specs=_mixer_specs(o_parts, o_t, tb) + [rows(D_MODEL), rows(D_MODEL), full(w_out), full(g), full(b)],
        out_specs=[rows(D_MODEL), rows(D_MODEL), rows(D_MODEL), rows(D_MODEL), rows(1)],
        out_shape=[jax.ShapeDtypeStruct((s, D_MODEL), F32), jax.ShapeDtypeStruct((s, D_MODEL), BF16),
                   jax.ShapeDtypeStruct((s, D_MODEL), BF16), jax.ShapeDtypeStruct((s, D_MODEL), F32),
                   jax.ShapeDtypeStruct((s, 1), F32)],
        compiler_params=_cparams((_PAR,)),
    )(*o_parts, h, x, w_out, g, b)


def _out_bwd(dxn, xh, rstd, g, w_out_t, o_parts, h, *, name, o_t=False):
    s = dxn.shape[0]
    tb = _pick(s, (256, 128))
    n_o = len(o_parts)

    def body(*refs):
        dxn_ref, xh_ref, rs_ref, g_ref, wt_ref = refs[:5]
        o_refs = refs[5:5 + n_o]
        gate_ref, dr_ref, dy_ref, do_ref, dgate_ref, dg_ref, db_ref = refs[5 + n_o:12 + n_o]

        @pl.when(pl.program_id(0) == 0)
        def _():
            dg_ref[...] = jnp.zeros_like(dg_ref)
            db_ref[...] = jnp.zeros_like(db_ref)

        dxn_b, xh_b = dxn_ref[...], xh_ref[...]
        dg_ref[...] += jnp.sum(dxn_b * xh_b, axis=0, keepdims=True)
        db_ref[...] += jnp.sum(dxn_b, axis=0, keepdims=True)
        dxh = dxn_b * g_ref[...]
        dr = rs_ref[...] * (dxh - jnp.mean(dxh, axis=-1, keepdims=True) - xh_b * jnp.mean(dxh * xh_b, axis=-1, keepdims=True))
        dr_ref[...] = dr
        dy = dr.astype(BF16)
        dy_ref[...] = dy
        dz = jnp.dot(dy, wt_ref[...], preferred_element_type=F32)
        gate = gate_ref[...]
        sg = _sigmoid(gate)
        o = _mixer_out(o_refs, o_t)
        do = dz * (gate * sg)
        do_ref[...] = do.astype(BF16)
        if o_t:
            refs[12 + n_o][...] = do.T.astype(BF16)
        dgate_ref[...] = (dz * o * (sg * (1.0 + gate * (1.0 - sg)))).astype(BF16)

    rows = lambda w: pl.BlockSpec((tb, w), lambda i: (i, 0))
    full = lambda a: pl.BlockSpec(a.shape, lambda i: (0,) * a.ndim)
    cols = pl.BlockSpec((D_MODEL, tb), lambda i: (0, i))
    return pl.pallas_call(
        body, name=name, grid=(s // tb,),
        in_specs=[rows(D_MODEL), rows(D_MODEL), rows(1), full(g), full(w_out_t)] + _mixer_specs(o_parts, o_t, tb) + [rows(D_MODEL)],
        out_specs=[rows(D_MODEL), rows(D_MODEL), rows(D_MODEL), rows(D_MODEL), full(g), full(g)] + [cols] * o_t,
        out_shape=[jax.ShapeDtypeStruct((s, D_MODEL), F32), jax.ShapeDtypeStruct((s, D_MODEL), BF16),
                   jax.ShapeDtypeStruct((s, D_MODEL), BF16), jax.ShapeDtypeStruct((s, D_MODEL), BF16),
                   jax.ShapeDtypeStruct(g.shape, F32), jax.ShapeDtypeStruct(g.shape, F32)]
        + [jax.ShapeDtypeStruct((D_MODEL, s), BF16)] * o_t,
        compiler_params=_cparams((_ARB,)),
    )(dxn, xh, rstd, g, w_out_t, *o_parts, h)


def _loss_grad(y, target, *, name):
    s, d = y.shape
    tb = _pick(s, (512, 256, 128))

    def body(y_ref, t_ref, dy_ref, l_ref):
        @pl.when(pl.program_id(0) == 0)
        def _():
            l_ref[...] = jnp.zeros_like(l_ref)

        err = y_ref[...] - t_ref[...]
        dy_ref[...] = err * (1.0 / d)
        per_tok = jnp.mean(err * err, axis=-1, keepdims=True)
        l_ref[...] += 0.5 * jnp.sum(per_tok, axis=0, keepdims=True)

    rows = pl.BlockSpec((tb, d), lambda i: (i, 0))
    return pl.pallas_call(
        body, name=name, grid=(s // tb,), in_specs=[rows, rows],
        out_specs=[rows, pl.BlockSpec((8, LANES), lambda i: (0, 0))],
        out_shape=[jax.ShapeDtypeStruct((s, d), F32), jax.ShapeDtypeStruct((8, LANES), F32)],
        compiler_params=_cparams((_ARB,)),
    )(y, target)


def _adamw(w, g, m, v, *, name):
    shape = w.shape
    w3, g3, m3, v3 = (a.reshape((1,) * (3 - a.ndim) + a.shape) for a in (w, g, m, v))
    a0, a1, a2 = w3.shape
    tb = _pick(a1, (256, 128)) if a1 % 8 == 0 else a1
    c1 = 1.0 - ADAM_B1 ** ADAM_STEP
    c2 = 1.0 - ADAM_B2 ** ADAM_STEP

    def body(w_ref, g_ref, m_ref, v_ref, d_ref, mo_ref, vo_ref):
        gg = g_ref[...]
        mn = ADAM_B1 * m_ref[...] + (1.0 - ADAM_B1) * gg
        vn = ADAM_B2 * v_ref[...] + (1.0 - ADAM_B2) * (gg * gg)
        mo_ref[...] = mn
        vo_ref[...] = vn
        d_ref[...] = -ADAM_LR * ((mn / c1) / (jnp.sqrt(vn / c2) + ADAM_EPS) + ADAM_WD * w_ref[...])

    spec = pl.BlockSpec((1, tb, a2), lambda i, j: (i, j, 0))
    outs = pl.pallas_call(
        body, name=name, grid=(a0, a1 // tb), in_specs=[spec] * 4, out_specs=[spec] * 3,
        out_shape=[jax.ShapeDtypeStruct(w3.shape, F32)] * 3, compiler_params=_cparams((_PAR, _PAR)),
    )(w3, g3, m3, v3)
    return tuple(a.reshape(shape) for a in outs)


def _place():
    x, y, c = lax.axis_index("x"), lax.axis_index("y"), lax.axis_index("c")
    return x, y, c, [(1 - x, y), (x, 1 - y), (1 - x, 1 - y)]


_ANY = pl.BlockSpec(memory_space=pl.ANY)


def _sem_pairs(n):
    return [pltpu.SemaphoreType.DMA((n,)), pltpu.SemaphoreType.DMA((n,))]


GATHER_COPIES = 7


def _gather_phases(shapes, srcs, outs, send_sems, recv_sems):
    nt = len(shapes)
    x, y, c, chips = _place()
    me = 2 * x + y
    sib = (x, y, 1 - c)

    def half(t, chip, hc):
        rh = shapes[t][1] // 2
        return outs[t].at[chip, :, pl.ds(hc * rh, rh), :]

    def copy(t, kk, s_ref, d_ref, to):
        return pltpu.make_async_remote_copy(src_ref=s_ref, dst_ref=d_ref, send_sem=send_sems.at[GATHER_COPIES * t + kk],
                                            recv_sem=recv_sems.at[GATHER_COPIES * t + kk], device_id=to, device_id_type=MESH)

    def first_hop(t, j):
        rh = shapes[t][1] // 2
        cx, cy = chips[j]
        return copy(t, j, srcs[t].at[:, pl.ds(c * rh, rh), :], half(t, me, c), (cx, cy, c))

    def passed(t, j):
        cx, cy = chips[j]
        landed = half(t, 2 * cx + cy, c)
        return copy(t, 3 + j, landed, landed, sib)

    def own(t):
        return copy(t, 6, srcs[t], outs[t].at[me], sib)

    pairs = [(t, j) for j in range(3) for t in range(nt)]

    def start():
        for t, j in pairs:
            first_hop(t, j).start()
        for t in range(nt):
            own(t).start()

    def pass_on():
        for t, j in pairs:
            cx, cy = chips[j]
            landed = half(t, 2 * cx + cy, c)
            copy(t, j, landed, landed, (cx, cy, c)).wait_recv()
            passed(t, j).start()

    def finish():
        for t, j in pairs:
            cx, cy = chips[j]
            theirs = half(t, 2 * cx + cy, 1 - c)
            copy(t, 3 + j, theirs, theirs, sib).wait_recv()
        for t in range(nt):
            own(t).wait()
        for t, j in pairs:
            first_hop(t, j).wait_send()
            passed(t, j).wait_send()

    return start, pass_on, finish


def _gather_chip_shards(tensors, *, name):
    nt = len(tensors)

    def body(*refs):
        for phase in _gather_phases([a.shape for a in tensors], refs[:nt], refs[nt:2 * nt], *refs[2 * nt:]):
            phase()

    return pl.pallas_call(
        body, name=name, in_specs=[_ANY] * nt, out_specs=[_ANY] * nt,
        out_shape=[jax.ShapeDtypeStruct((N_CHIPS,) + a.shape, a.dtype) for a in tensors],
        scratch_shapes=_sem_pairs(GATHER_COPIES * nt),
    )(*tensors)


def _swap_halves(gs, *, name):
    nt = len(gs)

    def body(*refs):
        srcs, outs = refs[:nt], refs[nt:2 * nt]
        send_sems, recv_sems = refs[2 * nt:]
        x, y, c, _ = _place()
        cps = [pltpu.make_async_remote_copy(src_ref=srcs[t].at[:, 1 - c], dst_ref=outs[t], send_sem=send_sems.at[t],
                                            recv_sem=recv_sems.at[t], device_id=(x, y, 1 - c), device_id_type=MESH)
               for t in range(nt)]
        for cp in cps:
            cp.start()
        for cp in cps:
            cp.wait()

    return pl.pallas_call(
        body, name=name, in_specs=[_ANY] * nt, out_specs=[_ANY] * nt,
        out_shape=[jax.ShapeDtypeStruct((N_CHIPS,) + g.shape[2:], g.dtype) for g in gs], scratch_shapes=_sem_pairs(nt),
    )(*gs)


def _scatter_phases(nt, srcs, outs, send_sems, recv_sems):
    x, y, c, chips = _place()

    def copies():
        return [pltpu.make_async_remote_copy(src_ref=srcs[t].at[2 * cx + cy], dst_ref=outs[t].at[j],
                                             send_sem=send_sems.at[3 * t + j], recv_sem=recv_sems.at[3 * t + j],
                                             device_id=(cx, cy, c), device_id_type=MESH)
                for j, (cx, cy) in enumerate(chips) for t in range(nt)]

    def start():
        for cp in copies():
            cp.start()

    def finish():
        for cp in copies():
            cp.wait()

    return start, finish


def _scatter_to_chips(blocks, *, name):
    nt = len(blocks)

    def body(*refs):
        for phase in _scatter_phases(nt, refs[:nt], refs[nt:2 * nt], *refs[2 * nt:]):
            phase()

    return pl.pallas_call(
        body, name=name, in_specs=[_ANY] * nt, out_specs=[_ANY] * nt,
        out_shape=[jax.ShapeDtypeStruct((3,) + a.shape[1:], a.dtype) for a in blocks], scratch_shapes=_sem_pairs(3 * nt),
    )(*blocks)


def _join_halves(bufs, *, name):
    nt = len(bufs)

    def body(*refs):
        srcs, outs = refs[:nt], refs[nt:2 * nt]
        send_sems, recv_sems = refs[2 * nt:]
        x, y, c, _ = _place()
        cps = []
        for t in range(nt):
            rh = bufs[t].shape[1] // 2
            rows = pl.ds(c * rh, rh)
            cps.append(pltpu.make_async_remote_copy(src_ref=srcs[t].at[:, rows, :], dst_ref=outs[t].at[:, rows, :],
                                                    send_sem=send_sems.at[t], recv_sem=recv_sems.at[t],
                                                    device_id=(x, y, 1 - c), device_id_type=MESH))
        for cp in cps:
            cp.start()
        for cp in cps:
            cp.wait()

    return pl.pallas_call(
        body, name=name, in_specs=[_ANY] * nt, out_specs=[_ANY] * nt,
        out_shape=[jax.ShapeDtypeStruct(b.shape, b.dtype) for b in bufs], scratch_shapes=_sem_pairs(nt),
        input_output_aliases={t: t for t in range(nt)},
    )(*bufs)


def _add_sibling(g, recv, cidx, chip_idx, *, name):
    _, _, na, rh, cdim = g.shape
    tb = _pick(rh, (256, 128, 64, 32, 16))

    def body(c_ref, k_ref, g_ref, r_ref, s_ref, o_ref):
        tot = g_ref[0, 0] + r_ref[0]
        s_ref[0] = tot.astype(BF16)

        @pl.when(pl.program_id(2) == k_ref[0])
        def _():
            o_ref[...] = tot

    return pl.pallas_call(
        body, name=name,
        grid_spec=pltpu.PrefetchScalarGridSpec(
            num_scalar_prefetch=2, grid=(na, rh // tb, N_CHIPS),
            in_specs=[pl.BlockSpec((1, 1, 1, tb, cdim), lambda a, i, k, c_ref, k_ref: (k, c_ref[0], a, i, 0)),
                      pl.BlockSpec((1, 1, tb, cdim), lambda a, i, k, c_ref, k_ref: (k, a, i, 0))],
            out_specs=[pl.BlockSpec((1, 1, tb, cdim), lambda a, i, k, c_ref, k_ref: (k, a, i, 0)),
                       pl.BlockSpec((1, tb, cdim), lambda a, i, k, c_ref, k_ref: (a, i, 0))]),
        out_shape=[jax.ShapeDtypeStruct((N_CHIPS, na, rh, cdim), BF16), jax.ShapeDtypeStruct((na, rh, cdim), F32)],
        compiler_params=_cparams((_PAR, _PAR, _ARB)),
    )(cidx, chip_idx, g, recv)


def _add_chips(own, recv, cidx, *, name):
    na, rh, cdim = own.shape
    tb = _pick(rh, (256, 128, 64, 32, 16))
    nblk = rh // tb

    def body(c_ref, a_ref, r0_ref, r1_ref, r2_ref, o_ref):
        o_ref[...] = ((a_ref[...] + r0_ref[0].astype(F32)) + r1_ref[0].astype(F32)) + r2_ref[0].astype(F32)

    slot = lambda j: pl.BlockSpec((1, 1, tb, cdim), lambda a, i, c_ref: (j, a, i, 0))
    return pl.pallas_call(
        body, name=name,
        grid_spec=pltpu.PrefetchScalarGridSpec(
            num_scalar_prefetch=1, grid=(na, nblk),
            in_specs=[pl.BlockSpec((1, tb, cdim), lambda a, i, c_ref: (a, i, 0)), slot(0), slot(1), slot(2)],
            out_specs=pl.BlockSpec((1, tb, cdim), lambda a, i, c_ref: (a, c_ref[0] * nblk + i, 0))),
        out_shape=jax.ShapeDtypeStruct((na, 2 * rh, cdim), F32), compiler_params=_cparams((_PAR, _PAR)),
    )(cidx, own, recv, recv, recv)


def _all_reduce_small(v, *, name):
    r, cdim = v.shape

    def body(v_ref, o_ref, buf, send_sems, recv_sems):
        x, y, c, _ = _place()
        me = 4 * x + 2 * y + c
        buf[me] = v_ref[...]
        cps = []
        for p in range(1, N_DEV):
            to = (1 - x if p & 4 else x, 1 - y if p & 2 else y, 1 - c if p & 1 else c)
            cp = pltpu.make_async_remote_copy(src_ref=v_ref, dst_ref=buf.at[me], send_sem=send_sems.at[p - 1],
                                              recv_sem=recv_sems.at[p - 1], device_id=to, device_id_type=MESH)
            cp.start()
            cps.append(cp)
        for p in range(1, N_DEV):
            frm = (4 * x + 2 * y + c) ^ p
            pltpu.make_async_remote_copy(src_ref=v_ref, dst_ref=buf.at[frm], send_sem=send_sems.at[p - 1],
                                         recv_sem=recv_sems.at[p - 1], device_id=(x, y, c), device_id_type=MESH).wait_recv()
        for cp in cps:
            cp.wait_send()
        tot = buf[0]
        for i in range(1, N_DEV):
            tot = tot + buf[i]
        o_ref[...] = tot

    vm = pl.BlockSpec(memory_space=pltpu.VMEM)
    return pl.pallas_call(
        body, name=name, in_specs=[vm], out_specs=vm, out_shape=jax.ShapeDtypeStruct((r, cdim), F32),
        scratch_shapes=[pltpu.VMEM((N_DEV, r, cdim), F32), pltpu.SemaphoreType.DMA((N_DEV - 1,)), pltpu.SemaphoreType.DMA((N_DEV - 1,))],
    )(v)


_SHARDED = ("even_w_in", "even_w_uq", "even_w_ukv", "even_w_out", "odd_w_in", "odd_w_out")
_COL_SHARDED = ("even_w_in", "even_w_uq", "even_w_ukv", "odd_w_in")
PACK_COLS = 1024


def _unshard(name, stacked):
    n, a, b, cc = stacked.shape
    if name in _COL_SHARDED:
        return stacked.transpose(1, 2, 0, 3).reshape(a, b, n * cc)
    return stacked.transpose(1, 0, 2, 3).reshape(a, n * b, cc)


def _chip_halves(name, full):
    a, b, cc = full.shape
    if name in _COL_SHARDED:
        return full.reshape(a, 2, b // 2, N_CHIPS, cc // N_CHIPS).transpose(3, 1, 0, 2, 4)
    return full.reshape(a, N_CHIPS, 2, b // (2 * N_CHIPS), cc).transpose(1, 2, 0, 3, 4)


def _rope_tables(s):
    pos = jnp.arange(s, dtype=F32)

    def ang(d):
        inv = ROPE_THETA ** (-jnp.arange(0, d, 2, dtype=F32) / d)
        a = pos[:, None] * inv[None, :]
        return jnp.cos(a), jnp.sin(a)

    c16, s16 = ang(MLA_ROPE)
    one, zero = jnp.ones((s, KPE_LANE), F32), jnp.zeros((s, KPE_LANE), F32)
    cos_m = jnp.concatenate([one, c16, c16, one[:, :32]], axis=1)
    sin_m = jnp.concatenate([zero, -s16, s16, zero[:, :32]], axis=1)
    c32, s32 = ang(SWA_DIM)
    cos_s = jnp.concatenate([c32, c32, c32, c32], axis=1)
    sin_s = jnp.concatenate([-s32, s32, -s32, s32], axis=1)
    return cos_m, sin_m, cos_s, sin_s


def _even_weights(w_in, w_uq, w_ukv):
    zeros = lambda n: jnp.zeros((D_MODEL, n), w_in.dtype)
    wcq, wckv, wkpe = w_in[:, 0:256], w_in[:, 256:384], w_in[:, 384:416]
    wfq, wfk, wfv = w_in[:, 416:928], w_in[:, 928:1440], w_in[:, 1440:1952]
    wfl, wg = w_in[:, 1952:1960], w_in[:, 1960:2984]
    misc = jnp.concatenate([zeros(KPE_LANE), wkpe, wfl, zeros(LANES - FL_LANE - FOX_HEADS)], axis=1)
    w_in_p = jnp.concatenate([wg, wfq, wfk, wfv, wcq, wckv, misc], axis=1)
    uq = w_uq.reshape(MLA_Q_RANK, MLA_HEADS, MLA_NOPE + MLA_ROPE)
    uq_p = jnp.pad(uq, ((0, 0), (0, 0), (0, HEAD_PAD - MLA_NOPE - MLA_ROPE))).reshape(MLA_Q_RANK, MLA_HEADS * HEAD_PAD)
    ukv = w_ukv.reshape(MLA_KV_RANK, MLA_HEADS, MLA_NOPE + MLA_V)
    uk_p = jnp.pad(ukv[..., :MLA_NOPE], ((0, 0), (0, 0), (0, HEAD_PAD - MLA_NOPE))).reshape(MLA_KV_RANK, MLA_HEADS * HEAD_PAD)
    uv = ukv[..., MLA_NOPE:].reshape(MLA_KV_RANK, MLA_HEADS * MLA_V)
    ukv_p = jnp.concatenate([uk_p, uv], axis=1)
    return w_in_p, w_in_p.T, uq_p, uq_p.T, ukv_p, ukv_p.T


def _even_weight_grads(dw_in_p, duq_p, dukv_p):
    g = dw_in_p
    gate, fq, fk, fv = g[:, E_GATE:E_FQ], g[:, E_FQ:E_FK], g[:, E_FK:E_FV], g[:, E_FV:E_SMALL]
    cq, ckv, misc = g[:, E_SMALL:E_SMALL + 256], g[:, E_SMALL + 256:E_SMALL + 384], g[:, E_SMALL + 384:]
    dw_in = jnp.concatenate([cq, ckv, misc[:, KPE_LANE:KPE_LANE + MLA_ROPE], fq, fk, fv, misc[:, FL_LANE:FL_LANE + FOX_HEADS], gate], axis=1)
    duq = duq_p.reshape(MLA_Q_RANK, MLA_HEADS, HEAD_PAD)[..., :MLA_NOPE + MLA_ROPE].reshape(MLA_Q_RANK, -1)
    nq = MLA_HEADS * HEAD_PAD
    dk = dukv_p[:, :nq].reshape(MLA_KV_RANK, MLA_HEADS, HEAD_PAD)[..., :MLA_NOPE]
    dvv = dukv_p[:, nq:].reshape(MLA_KV_RANK, MLA_HEADS, MLA_V)
    dukv = jnp.concatenate([dk, dvv], axis=-1).reshape(MLA_KV_RANK, -1)
    return dw_in, duq, dukv


def _interleave(w, forward):
    a, b = (SWA_KV_HEADS, SWA_HEADS // SWA_KV_HEADS) if forward else (SWA_HEADS // SWA_KV_HEADS, SWA_KV_HEADS)
    return w.reshape(w.shape[0], a, b, -1).transpose(0, 2, 1, 3).reshape(w.shape[0], -1)


def _odd_weights(w_in, w_out):
    q, k, v, gate = w_in[:, 0:1024], w_in[:, 1024:1152], w_in[:, 1152:1280], w_in[:, 1280:2304]
    w_p = jnp.concatenate([_interleave(gate, True), _interleave(q, True), k, v], axis=1)
    w_out_p = _interleave(w_out.T, True).T
    return w_p, w_p.T, w_out_p, w_out_p.T


def _odd_weight_grads(g, dw_out_p):
    dw_in = jnp.concatenate([_interleave(g[:, O_Q:O_K], False), g[:, O_K:O_V], g[:, O_V:], _interleave(g[:, O_GATE:O_Q], False)], axis=1)
    return dw_in, _interleave(dw_out_p.T, False).T


def kernel(x, even_w_in, even_q_norm, even_w_uq, even_kv_norm, even_w_ukv, even_b_f, even_w_out, even_ln_g, even_ln_b, odd_w_in, odd_sinks, odd_w_out, odd_ln_g, odd_ln_b, loss_target, m_even_w_in, m_even_q_norm, m_even_w_uq, m_even_kv_norm, m_even_w_ukv, m_even_b_f, m_even_w_out, m_even_ln_g, m_even_ln_b, m_odd_w_in, m_odd_sinks, m_odd_w_out, m_odd_ln_g, m_odd_ln_b, v_even_w_in, v_even_q_norm, v_even_w_uq, v_even_kv_norm, v_even_w_ukv, v_even_b_f, v_even_w_out, v_even_ln_g, v_even_ln_b, v_odd_w_in, v_odd_sinks, v_odd_w_out, v_odd_ln_g, v_odd_ln_b):
    weights = dict(even_w_in=even_w_in, even_q_norm=even_q_norm, even_w_uq=even_w_uq, even_kv_norm=even_kv_norm,
                   even_w_ukv=even_w_ukv, even_b_f=even_b_f, even_w_out=even_w_out, even_ln_g=even_ln_g, even_ln_b=even_ln_b,
                   odd_w_in=odd_w_in, odd_sinks=odd_sinks, odd_w_out=odd_w_out, odd_ln_g=odd_ln_g, odd_ln_b=odd_ln_b)
    mom_m = dict(even_w_in=m_even_w_in, even_q_norm=m_even_q_norm, even_w_uq=m_even_w_uq, even_kv_norm=m_even_kv_norm,
                 even_w_ukv=m_even_w_ukv, even_b_f=m_even_b_f, even_w_out=m_even_w_out, even_ln_g=m_even_ln_g, even_ln_b=m_even_ln_b,
                 odd_w_in=m_odd_w_in, odd_sinks=m_odd_sinks, odd_w_out=m_odd_w_out, odd_ln_g=m_odd_ln_g, odd_ln_b=m_odd_ln_b)
    mom_v = dict(even_w_in=v_even_w_in, even_q_norm=v_even_q_norm, even_w_uq=v_even_w_uq, even_kv_norm=v_even_kv_norm,
                 even_w_ukv=v_even_w_ukv, even_b_f=v_even_b_f, even_w_out=v_even_w_out, even_ln_g=v_even_ln_g, even_ln_b=v_even_ln_b,
                 odd_w_in=v_odd_w_in, odd_sinks=v_odd_sinks, odd_w_out=v_odd_w_out, odd_ln_g=v_odd_ln_g, odd_ln_b=v_odd_ln_b)
    names = list(weights)
    xl = x[0]
    tgt = loss_target[0]
    s = xl.shape[0]
    ax, ay, ac = lax.axis_index("x"), lax.axis_index("y"), lax.axis_index("c")
    chip = 2 * ax + ay
    c_idx = ac.astype(jnp.int32).reshape(1)
    chip_idx = chip.astype(jnp.int32).reshape(1)

    ln_odd = jnp.pad(jnp.concatenate([odd_ln_g, odd_ln_b]), ((0, 12), (0, 0)))[None]
    wb = {n: weights[n].astype(BF16) for n in _SHARDED}
    even_mats = [n for n in _SHARDED if n.startswith("even")]
    odd_mats = [n for n in _SHARDED if n.startswith("odd")]
    first_mats = [n for n in even_mats if n != "even_w_out"]
    gathered = _gather_chip_shards([wb[n][0:1] for n in first_mats], name="gather_first")
    full = {n: [None, None] for n in even_mats}
    for i, n in enumerate(first_mats):
        full[n][0] = _unshard(n, gathered[i])[0]
    later = [wb["even_w_out"][0:1]] + [wb[n][1:2] for n in even_mats] + [wb[n] for n in odd_mats] + [ln_odd]

    cos_m, sin_m, cos_s, sin_s = _rope_tables(s)
    bf_tiles = jnp.pad(even_b_f, ((0, 0), (FL_LANE, LANES - FL_LANE - FOX_HEADS)))
    sink_tiles = jnp.broadcast_to(_interleave(odd_sinks, True)[:, :, None, None], (2, SWA_HEADS, 1, LANES))
    mla_scale = (MLA_NOPE + MLA_ROPE) ** -0.5
    fox_scale = FOX_DIM ** -0.5
    mla_pairs, fox_pairs = MLA_HEADS // 2, FOX_HEADS // 2

    saved = []
    x_f, x_b = xl, xl.astype(BF16)
    for layer in range(DEPTH):
        j = layer // 2
        ln = f"L{layer}"
        if layer % 2 == 0:
            w_in_p, w_in_t, uq_p, uq_t, ukv_p, ukv_t = _even_weights(full["even_w_in"][j], full["even_w_uq"][j], full["even_w_ukv"][j])
            qg, kg, bft = even_q_norm[j][None], even_kv_norm[j][None], bf_tiles[j][None]
            h, hb = _mm(x_b, w_in_p, out_dtype=F32, name=ln + "_in", also_bf16=True)
            q, k, v, qn, kvn, logf = _even_mid_fwd(h, qg, kg, bft, uq_p, ukv_p, cos_m, sin_m, name=ln + "_mid")
            cum = _cumsum(logf, reverse=False, name=ln + "_cum")[:, FL_LANE:FL_LANE + FOX_HEADS].T
            crow = cum[:, None, :]
            if layer == 0:
                o_mla, lse_mla, rest = _attn_fwd((q, 0), (k, 0), (v, 0), None, dk=HEAD_PAD, npairs=mla_pairs, scale=mla_scale,
                                                 name=ln + "_mla", gather=later)
                full["even_w_out"][0] = _unshard("even_w_out", rest[0])[0]
                for i, n in enumerate(even_mats):
                    full[n][1] = _unshard(n, rest[1 + i])[0]
                for i, n in enumerate(odd_mats):
                    full[n] = _unshard(n, rest[1 + len(even_mats) + i])
                ln_all = rest[-1][:, 0]
                odd_g_full = ln_all[:, 0:2].transpose(1, 0, 2).reshape(2, D_MODEL)
                odd_b_full = ln_all[:, 2:4].transpose(1, 0, 2).reshape(2, D_MODEL)
            else:
                o_mla, lse_mla = _attn_fwd((q, 0), (k, 0), (v, 0), None, dk=HEAD_PAD, npairs=mla_pairs, scale=mla_scale, name=ln + "_mla")
            o_fox, lse_fox = _attn_fwd((hb, E_FQ), (hb, E_FK), (hb, E_FV), crow, dk=FOX_DIM, npairs=fox_pairs,
                                       scale=fox_scale, name=ln + "_fox")
            o_parts = [o_mla, o_fox]
            w_out, w_out_t = full["even_w_out"][j], full["even_w_out"][j].T
            g_ln, b_ln = even_ln_g[j][None], even_ln_b[j][None]
            x_n, x_nb, z, xh, rstd = _out_fwd(o_parts, h, x_f, w_out, g_ln, b_ln, name=ln + "_out")
            saved.append(dict(h=h, hb=hb, x_b=x_b, qn=qn, kvn=kvn, q=q, k=k, v=v, crow=crow,
                              o_mla=o_mla, o_fox=o_fox, lse_mla=lse_mla, lse_fox=lse_fox, o_parts=o_parts, z=z, xh=xh, rstd=rstd,
                              w_in_t=w_in_t, uq_t=uq_t, ukv_t=ukv_t, w_out_t=w_out_t, qg=qg, kg=kg, bft=bft, g_ln=g_ln))
        else:
            w_in_p, w_in_t, w_out, w_out_t = _odd_weights(full["odd_w_in"][j], full["odd_w_out"][j])
            h = _mm(x_b, w_in_p, out_dtype=F32, name=ln + "_in")
            q, k, v = _odd_rope(h, cos_s, sin_s, name=ln + "_rope")
            kp = jnp.pad(k, ((WINDOW, 0), (0, 0)))
            vp = jnp.pad(v, ((WINDOW, 0), (0, 0)))
            o_t, lse = _swa_fwd(q, kp, vp, sink_tiles[j], name=ln + "_swa")
            g_ln, b_ln = odd_g_full[j][None], odd_b_full[j][None]
            x_n, x_nb, z, xh, rstd = _out_fwd([o_t], h, x_f, w_out, g_ln, b_ln, name=ln + "_out", o_t=True)
            saved.append(dict(h=h, x_b=x_b, q=q, kp=kp, vp=vp, lse=lse, o_t=o_t, o_parts=[o_t], z=z, xh=xh, rstd=rstd,
                              w_in_t=w_in_t, w_out_t=w_out_t, g_ln=g_ln))
        x_f, x_b = x_n, x_nb

    dxn, loss_tile = _loss_grad(x_f, tgt, name="loss")

    def pair_sums(group, tag):
        gp = [_chip_halves(n, g) for n, g in group.items()]
        from_sib = _swap_halves(gp, name="grad_swap_" + tag)
        return [_add_sibling(g, r, c_idx, chip_idx, name=f"grad_add_sibling_{n}_{tag}") for n, g, r in zip(group, gp, from_sib)]

    def chip_sums(group, pair, from_chips, tag):
        mine = [_add_chips(p[1], r, c_idx, name=f"grad_add_chips_{n}_{tag}") for n, p, r in zip(group, pair, from_chips)]
        return dict(zip(group, _join_halves(mine, name="grad_join_" + tag)))

    grads = {n: [None, None] for n in names}
    for layer in reversed(range(DEPTH)):
        j = layer // 2
        ln = f"L{layer}"
        sv = saved[layer]
        dr, dy, do, dgate, dg_ln, db_ln, *do_t = _out_bwd(dxn, sv["xh"], sv["rstd"], sv["g_ln"], sv["w_out_t"], sv["o_parts"], sv["h"],
                                                          name=ln + "_outb", o_t=layer % 2 == 1)
        dw_out = _mm(sv["z"].T, dy, out_dtype=F32, name=ln + "_dwout")
        if layer % 2 == 0:
            hb = sv["hb"]
            if layer == 0:
                early = {n: grads[n][1][None] for n in even_mats}
                early.update({n: jnp.stack(grads[n]) for n in odd_mats})
                early_pair = pair_sums(early, "early")
                dq, dk, dv, early_chips = _attn_bwd((sv["q"], 0), (sv["k"], 0), (sv["v"], 0), (sv["o_mla"], 0), (do, 0), sv["lse_mla"], None,
                                                    dk=HEAD_PAD, npairs=mla_pairs, scale=mla_scale, dq_dtype=F32, name=ln + "_mlab",
                                                    scatter=[p[0] for p in early_pair])
            else:
                dq, dk, dv = _attn_bwd((sv["q"], 0), (sv["k"], 0), (sv["v"], 0), (sv["o_mla"], 0), (do, 0), sv["lse_mla"], None,
                                       dk=HEAD_PAD, npairs=mla_pairs, scale=mla_scale, dq_dtype=F32, name=ln + "_mlab")
            dfq, dfk, dfv, dcrow, dccol = _attn_bwd((hb, E_FQ), (hb, E_FK), (hb, E_FV), (sv["o_fox"], 0), (do, MLA_HEADS * MLA_V),
                                                    sv["lse_fox"], sv["crow"], dk=FOX_DIM, npairs=fox_pairs,
                                                    scale=fox_scale, dq_dtype=BF16, name=ln + "_foxb")
            dkeys = dccol.reshape(s, fox_pairs, 2, HALF)[:, :, ::-1, 0].reshape(s, FOX_HEADS)
            dcum = jnp.pad(dcrow[:, 0, :].T + dkeys, ((0, 0), (FL_LANE, LANES - FL_LANE - FOX_HEADS)))
            dlogf = _cumsum(dcum, reverse=True, name=ln + "_cumb")
            dh_small, dq_pre, dqg, dkg, dbf = _even_mid_bwd(
                sv["h"], dq, dk, dv, dlogf, sv["qg"], sv["kg"], sv["bft"], sv["uq_t"], sv["ukv_t"], cos_m, sin_m, name=ln + "_midb")
            duq_p = _mm(sv["qn"].T, dq_pre, out_dtype=F32, name=ln + "_dwuq")
            dkv_cat = jnp.concatenate([dk.astype(BF16), dv], axis=1)
            dukv_p = _mm(sv["kvn"].T, dkv_cat, out_dtype=F32, name=ln + "_dwukv")
            dh = jnp.concatenate([dgate, dfq, dfk, dfv, dh_small], axis=1)
            dw_in_p = _mm(sv["x_b"].T, dh, out_dtype=F32, name=ln + "_dwin")
            dw_in, duq, dukv = _even_weight_grads(dw_in_p, duq_p, dukv_p)
            for n, val in (("even_w_in", dw_in), ("even_w_uq", duq), ("even_w_ukv", dukv), ("even_w_out", dw_out),
                           ("even_q_norm", dqg[0]), ("even_kv_norm", dkg[0]), ("even_b_f", dbf[0, FL_LANE:FL_LANE + FOX_HEADS]),
                           ("even_ln_g", dg_ln[0]), ("even_ln_b", db_ln[0])):
                grads[n][j] = val
        else:
            dq_t, dkp, dvp, dsink = _swa_bwd(sv["q"], sv["kp"], sv["vp"], sv["kp"].T, sink_tiles[j], sv["o_t"], do, do_t[0], sv["lse"],
                                             name=ln + "_swab")
            dq_r, dk_r = _odd_rope_bwd(dq_t, dkp[WINDOW:], cos_s, sin_s, name=ln + "_ropeb")
            dh = jnp.concatenate([dgate, dq_r, dk_r, dvp[WINDOW:].astype(BF16)], axis=1)
            dw_in_p = _mm(sv["x_b"].T, dh, out_dtype=F32, name=ln + "_dwin")
            dw_in, dw_out = _odd_weight_grads(dw_in_p, dw_out)
            for n, val in (("odd_w_in", dw_in), ("odd_w_out", dw_out), ("odd_sinks", _interleave(dsink[None, :, 0, 0], False)[0]),
                           ("odd_ln_g", dg_ln[0]), ("odd_ln_b", db_ln[0])):
                grads[n][j] = val
        dxn = _mm(dh, sv["w_in_t"], out_dtype=F32, name=ln + "_dx", res=dr, res_scale=ALPHA)
    grad_x = dxn[None]
    late = {n: grads[n][0][None] for n in even_mats}
    late_pair = pair_sums(late, "late")
    late_sum = chip_sums(late, late_pair, _scatter_to_chips([p[0] for p in late_pair], name="grad_scatter_late"), "late")
    early_sum = chip_sums(early, early_pair, early_chips, "early")
    gshard = {n: jnp.concatenate([late_sum[n], early_sum[n]]) for n in even_mats}
    gshard.update({n: early_sum[n] for n in odd_mats})

    small = [n for n in names if n not in _SHARDED]
    grads = {n: jnp.stack(grads[n]) for n in small}
    sv_flat = jnp.concatenate([grads[n].reshape(-1) for n in small] + [loss_tile[0, :1]])
    sv_real = sv_flat.shape[0]
    sv_rows = -(-sv_real // (PACK_COLS * 8)) * 8
    sv_sum = _all_reduce_small(jnp.pad(sv_flat, (0, sv_rows * PACK_COLS - sv_real)).reshape(sv_rows, PACK_COLS), name="small_all_reduce").reshape(-1)
    off = 0
    for n in small:
        size = int(np.prod(grads[n].shape))
        gfull = sv_sum[off:off + size].reshape(grads[n].shape)
        off += size
        if n in ("odd_ln_g", "odd_ln_b"):
            gfull = lax.dynamic_slice_in_dim(gfull, chip * (D_MODEL // N_CHIPS), D_MODEL // N_CHIPS, axis=1)
        gshard[n] = gfull
    loss = sv_sum[off]

    deltas, new_m, new_v = {}, {}, {}
    for n in names:
        deltas[n], new_m[n], new_v[n] = _adamw(weights[n], gshard[n], mom_m[n], mom_v[n], name="adamw_" + n)
    return (loss, grad_x, *[gshard[n] for n in names], *[deltas[n] for n in names],
            *[new_m[n] for n in names], *[new_v[n] for n in names])
```

```python
import jax
import jax.numpy as jnp
import numpy as np
from jax import lax
from jax.experimental import pallas as pl
from jax.experimental.pallas import tpu as pltpu

F32 = jnp.float32
BF16 = jnp.bfloat16
MESH = pl.DeviceIdType.MESH

D_MODEL = 1024
DEPTH = 4
ROPE_THETA = 10000.0
MLA_HEADS, MLA_NOPE, MLA_ROPE, MLA_V = 8, 64, 32, 64
MLA_Q_RANK, MLA_KV_RANK = 256, 128
FOX_HEADS, FOX_DIM = 8, 64
SWA_HEADS, SWA_KV_HEADS, SWA_DIM, WINDOW = 16, 2, 64, 128
RMS_EPS = 1e-6
LN_EPS = 1e-5
ALPHA = (2 * DEPTH) ** 0.25
EVEN_IN = 2984
ODD_IN = 2304
ADAM_LR, ADAM_B1, ADAM_B2, ADAM_EPS, ADAM_WD, ADAM_STEP = 0.001, 0.9, 0.999, 1e-08, 0.01, 10

LANES = 128
HALF = LANES // 2
HEAD_PAD = 128
N_CHIPS = 4
N_DEV = 8
E_GATE, E_FQ, E_FK, E_FV, E_SMALL = 0, 1024, 1536, 2048, 2560
E_PAD_IN = 3072
KPE_LANE = 64
FL_LANE = 96
O_GATE, O_Q, O_K, O_V = 0, 1024, 2048, 2176

_ARB = "arbitrary"
_PAR = "parallel"


def _cparams(sem):
    return pltpu.CompilerParams(dimension_semantics=sem)


def _pick(n, cands):
    for c in cands:
        if n % c == 0:
            return c
    return n


RESIDENT_BYTES = 8 << 20


def _mm_resident(a, b, *, out_dtype, name, res, res_scale, also_bf16):
    m, k = a.shape
    _, n = b.shape
    tm = _pick(m, (256, 128))
    cn = _pick(n, (512, 384, 256, 128))
    dtypes = [out_dtype, BF16] if also_bf16 else [out_dtype]

    def body(*refs):
        a_ref, b_ref = refs[:2]
        r_ref = refs[2] if res is not None else None
        outs = refs[3 if res is not None else 2:]
        av = a_ref[...].astype(BF16)
        for c0 in range(0, n, cn):
            r = jnp.dot(av, b_ref[:, c0:c0 + cn].astype(BF16), preferred_element_type=F32)
            if res is not None:
                r = r + res_scale * r_ref[:, c0:c0 + cn]
            for o_ref in outs:
                o_ref[:, c0:c0 + cn] = r.astype(o_ref.dtype)

    rows = lambda w: pl.BlockSpec((tm, w), lambda i: (i, 0))
    in_specs = [rows(k), pl.BlockSpec((k, n), lambda i: (0, 0))] + ([rows(n)] if res is not None else [])
    out = pl.pallas_call(
        body, name=name, grid=(m // tm,), in_specs=in_specs, out_specs=[rows(n)] * len(dtypes),
        out_shape=[jax.ShapeDtypeStruct((m, n), d) for d in dtypes], compiler_params=_cparams((_PAR,)),
    )(*([a, b] + ([res] if res is not None else [])))
    return out if also_bf16 else out[0]


def _mm_left_resident(a, b, *, out_dtype, name):
    m, k = a.shape
    _, n = b.shape
    tn = _pick(n, (256, 128))

    def body(a_ref, b_ref, o_ref):
        o_ref[...] = jnp.dot(a_ref[...].astype(BF16), b_ref[...].astype(BF16), preferred_element_type=F32).astype(o_ref.dtype)

    return pl.pallas_call(
        body, name=name, grid=(n // tn,),
        in_specs=[pl.BlockSpec((m, k), lambda j: (0, 0)), pl.BlockSpec((k, tn), lambda j: (0, j))],
        out_specs=pl.BlockSpec((m, tn), lambda j: (0, j)), out_shape=jax.ShapeDtypeStruct((m, n), out_dtype),
        compiler_params=_cparams((_PAR,)),
    )(a, b)


def _mm(a, b, *, out_dtype, name, res=None, res_scale=1.0, also_bf16=False):
    m, k = a.shape
    _, n = b.shape
    if b.size * b.dtype.itemsize <= RESIDENT_BYTES:
        return _mm_resident(a, b, out_dtype=out_dtype, name=name, res=res, res_scale=res_scale, also_bf16=also_bf16)
    if a.size * a.dtype.itemsize <= RESIDENT_BYTES and res is None and not also_bf16:
        return _mm_left_resident(a, b, out_dtype=out_dtype, name=name)
    tm = _pick(m, (512, 256, 128))
    tn = _pick(n, (1024, 768, 512, 384, 256, 128))
    tk = _pick(k, (1024, 768, 512, 256, 128))
    nk = k // tk

    def body(*refs):
        refs = list(refs)
        a_ref, b_ref = refs[:2]
        r_ref = refs[2] if res is not None else None
        acc_ref = refs[-1]
        outs = refs[3 if res is not None else 2:-1]
        kk = pl.program_id(2)

        @pl.when(kk == 0)
        def _():
            acc_ref[...] = jnp.zeros_like(acc_ref)

        acc_ref[...] += jnp.dot(a_ref[...].astype(BF16), b_ref[...].astype(BF16), preferred_element_type=F32)

        @pl.when(kk == nk - 1)
        def _():
            r = acc_ref[...]
            if res is not None:
                r = r + res_scale * r_ref[...]
            for o_ref in outs:
                o_ref[...] = r.astype(o_ref.dtype)

    in_specs = [pl.BlockSpec((tm, tk), lambda i, j, kk: (i, kk)), pl.BlockSpec((tk, tn), lambda i, j, kk: (kk, j))]
    args = [a, b]
    if res is not None:
        in_specs.append(pl.BlockSpec((tm, tn), lambda i, j, kk: (i, j)))
        args.append(res)
    ospec = pl.BlockSpec((tm, tn), lambda i, j, kk: (i, j))
    dtypes = [out_dtype, BF16] if also_bf16 else [out_dtype]
    out = pl.pallas_call(
        body, name=name, grid=(m // tm, n // tn, nk), in_specs=in_specs,
        out_specs=[ospec] * len(dtypes), out_shape=[jax.ShapeDtypeStruct((m, n), d) for d in dtypes],
        scratch_shapes=[pltpu.VMEM((tm, tn), F32)],
        compiler_params=_cparams((_PAR, _PAR, _ARB)),
    )(*args)
    return out if also_bf16 else out[0]


def _rope_tile(t, cos, sin, half):
    w = t.shape[-1]
    lane = lax.broadcasted_iota(jnp.int32, t.shape, 1)
    first = (lane % (2 * half)) < half
    sw = jnp.where(first, pltpu.roll(t, w - half, 1), pltpu.roll(t, half, 1))
    return t * cos + sw * sin


def _sigmoid(x):
    return 1.0 / (1.0 + jnp.exp(-x))


def _lane_mask(shape, lo, hi):
    lane = lax.broadcasted_iota(jnp.int32, shape, 1)
    return (lane >= lo) & (lane < hi)


def _rms(x, g):
    r = lax.rsqrt(jnp.mean(x * x, axis=-1, keepdims=True) + RMS_EPS)
    u = x * r
    return u, r, u * g


def _rms_bwd(dy, u, r, g):
    dyg = dy * g
    dx = r * (dyg - u * jnp.mean(dyg * u, axis=-1, keepdims=True))
    return dx, jnp.sum(dy * u, axis=0, keepdims=True)


def _even_mid_fwd(h, qg, kg, bf_tile, w_uq, w_ukv, cos, sin, *, name):
    s = h.shape[0]
    tb = _pick(s, (512, 256, 128))
    nq = MLA_HEADS * HEAD_PAD

    def body(h_ref, qg_ref, kg_ref, bf_ref, wuq_ref, wukv_ref, cos_ref, sin_ref,
             q_ref, k_ref, v_ref, qn_ref, kvn_ref, lf_ref):
        hb = h_ref[...]
        cq, ckv, misc = hb[:, :MLA_Q_RANK], hb[:, MLA_Q_RANK:MLA_Q_RANK + MLA_KV_RANK], hb[:, MLA_Q_RANK + MLA_KV_RANK:]
        cs, sn = cos_ref[...], sin_ref[...]
        _, _, qn = _rms(cq, qg_ref[...])
        qn = qn.astype(BF16)
        qn_ref[...] = qn
        q = jnp.dot(qn, wuq_ref[...], preferred_element_type=F32)
        _, _, kvn = _rms(ckv, kg_ref[...])
        kvn = kvn.astype(BF16)
        kvn_ref[...] = kvn
        kv = jnp.dot(kvn, wukv_ref[...], preferred_element_type=F32)
        kpe = jnp.where(_lane_mask(misc.shape, KPE_LANE, KPE_LANE + MLA_ROPE), _rope_tile(misc, cs, sn, MLA_ROPE // 2), 0.0)
        for hd in range(MLA_HEADS):
            sl = slice(hd * HEAD_PAD, (hd + 1) * HEAD_PAD)
            q_ref[:, sl] = _rope_tile(q[:, sl], cs, sn, MLA_ROPE // 2).astype(BF16)
            k_ref[:, sl] = (kv[:, sl] + kpe).astype(BF16)
        v_ref[...] = kv[:, nq:].astype(BF16)
        xf = misc + bf_ref[...]
        logf = jnp.minimum(xf, 0.0) - jnp.log(1.0 + jnp.exp(-jnp.abs(xf)))
        lf_ref[...] = jnp.where(_lane_mask(misc.shape, FL_LANE, FL_LANE + FOX_HEADS), logf, 0.0)

    full = lambda a: pl.BlockSpec(a.shape, lambda i: (0,) * a.ndim)
    rows = lambda w, c=0: pl.BlockSpec((tb, w), lambda i: (i, c))
    return pl.pallas_call(
        body, name=name, grid=(s // tb,),
        in_specs=[rows(512, E_SMALL // 512), full(qg), full(kg), full(bf_tile), full(w_uq), full(w_ukv), rows(LANES), rows(LANES)],
        out_specs=[rows(nq), rows(nq), rows(MLA_HEADS * MLA_V), rows(MLA_Q_RANK), rows(MLA_KV_RANK), rows(LANES)],
        out_shape=[jax.ShapeDtypeStruct((s, nq), BF16), jax.ShapeDtypeStruct((s, nq), BF16),
                   jax.ShapeDtypeStruct((s, MLA_HEADS * MLA_V), BF16), jax.ShapeDtypeStruct((s, MLA_Q_RANK), BF16),
                   jax.ShapeDtypeStruct((s, MLA_KV_RANK), BF16), jax.ShapeDtypeStruct((s, LANES), F32)],
        compiler_params=_cparams((_PAR,)),
    )(h, qg, kg, bf_tile, w_uq, w_ukv, cos, sin)


def _even_mid_bwd(h, dq, dk, dv, dlogf, qg, kg, bf_tile, w_uq_t, w_ukv_t, cos, sin, *, name):
    s = h.shape[0]
    tb = _pick(s, (256, 128))
    nq = MLA_HEADS * HEAD_PAD

    def body(h_ref, dq_ref, dk_ref, dv_ref, dlf_ref, qg_ref, kg_ref, bf_ref, wuqt_ref, wukvt_ref, cos_ref, sin_ref,
             dh_ref, dqp_ref, dqg_ref, dkg_ref, dbf_ref):
        @pl.when(pl.program_id(0) == 0)
        def _():
            dqg_ref[...] = jnp.zeros_like(dqg_ref)
            dkg_ref[...] = jnp.zeros_like(dkg_ref)
            dbf_ref[...] = jnp.zeros_like(dbf_ref)

        hb = h_ref[...]
        cq, ckv, misc = hb[:, :MLA_Q_RANK], hb[:, MLA_Q_RANK:MLA_Q_RANK + MLA_KV_RANK], hb[:, MLA_Q_RANK + MLA_KV_RANK:]
        cs, sn = cos_ref[...], -sin_ref[...]
        dkpe = jnp.zeros(misc.shape, F32)
        for hd in range(MLA_HEADS):
            sl = slice(hd * HEAD_PAD, (hd + 1) * HEAD_PAD)
            dqp_ref[:, sl] = _rope_tile(dq_ref[:, sl], cs, sn, MLA_ROPE // 2).astype(BF16)
            dkpe = dkpe + dk_ref[:, sl]
        dqn = jnp.dot(dqp_ref[...], wuqt_ref[...], preferred_element_type=F32)
        uq, rq, _ = _rms(cq, qg_ref[...])
        dcq, dqg = _rms_bwd(dqn, uq, rq, qg_ref[...])
        dqg_ref[...] += dqg
        dkv = jnp.concatenate([dk_ref[...].astype(BF16), dv_ref[...]], axis=1)
        dkvn = jnp.dot(dkv, wukvt_ref[...], preferred_element_type=F32)
        uk, rk, _ = _rms(ckv, kg_ref[...])
        dckv, dkg = _rms_bwd(dkvn, uk, rk, kg_ref[...])
        dkg_ref[...] += dkg
        dmisc = jnp.where(_lane_mask(misc.shape, KPE_LANE, KPE_LANE + MLA_ROPE), _rope_tile(dkpe, cs, sn, MLA_ROPE // 2), 0.0)
        dfl = jnp.where(_lane_mask(misc.shape, FL_LANE, FL_LANE + FOX_HEADS), dlf_ref[...] * _sigmoid(-(misc + bf_ref[...])), 0.0)
        dbf_ref[...] += jnp.sum(dfl, axis=0, keepdims=True)
        dh_ref[:, :MLA_Q_RANK] = dcq.astype(BF16)
        dh_ref[:, MLA_Q_RANK:MLA_Q_RANK + MLA_KV_RANK] = dckv.astype(BF16)
        dh_ref[:, MLA_Q_RANK + MLA_KV_RANK:] = (dmisc + dfl).astype(BF16)

    full = lambda a: pl.BlockSpec(a.shape, lambda i: (0,) * a.ndim)
    rows = lambda w, c=0: pl.BlockSpec((tb, w), lambda i: (i, c))
    return pl.pallas_call(
        body, name=name, grid=(s // tb,),
        in_specs=[rows(512, E_SMALL // 512), rows(nq), rows(nq), rows(MLA_HEADS * MLA_V), rows(LANES), full(qg), full(kg),
                  full(bf_tile), full(w_uq_t), full(w_ukv_t), rows(LANES), rows(LANES)],
        out_specs=[rows(512), rows(nq), full(qg), full(kg), full(bf_tile)],
        out_shape=[jax.ShapeDtypeStruct((s, 512), BF16), jax.ShapeDtypeStruct((s, nq), BF16),
                   jax.ShapeDtypeStruct(qg.shape, F32), jax.ShapeDtypeStruct(kg.shape, F32),
                   jax.ShapeDtypeStruct(bf_tile.shape, F32)],
        compiler_params=_cparams((_ARB,)),
    )(h, dq, dk, dv, dlogf, qg, kg, bf_tile, w_uq_t, w_ukv_t, cos, sin)


def _cumsum(x, *, reverse, name):
    s = x.shape[0]
    tb = _pick(s, (512, 256, 128))
    nb = s // tb

    def body(x_ref, o_ref, carry_ref):
        @pl.when(pl.program_id(0) == 0)
        def _():
            carry_ref[...] = jnp.zeros_like(carry_ref)

        xv = x_ref[...]
        r = lax.broadcasted_iota(jnp.int32, (tb, tb), 0)
        c = lax.broadcasted_iota(jnp.int32, (tb, tb), 1)
        tri = jnp.where((c >= r) if reverse else (c <= r), 1.0, 0.0).astype(BF16)
        hi = xv.astype(BF16)
        r1 = xv - hi.astype(F32)
        mid = r1.astype(BF16)
        lo = (r1 - mid.astype(F32)).astype(BF16)
        cs = (jnp.dot(tri, hi, preferred_element_type=F32) + jnp.dot(tri, mid, preferred_element_type=F32)
              + jnp.dot(tri, lo, preferred_element_type=F32)) + carry_ref[...]
        o_ref[...] = cs
        carry_ref[...] = cs[0:1, :] if reverse else cs[tb - 1:tb, :]

    imap = (lambda i: (nb - 1 - i, 0)) if reverse else (lambda i: (i, 0))
    return pl.pallas_call(
        body, name=name, grid=(nb,), in_specs=[pl.BlockSpec((tb, LANES), imap)],
        out_specs=pl.BlockSpec((tb, LANES), imap), out_shape=jax.ShapeDtypeStruct(x.shape, F32),
        scratch_shapes=[pltpu.VMEM((1, LANES), F32)], compiler_params=_cparams((_ARB,)),
    )(x)


_NT = (((1,), (1,)), ((), ()))
_TN = (((0,), (0,)), ((), ()))


def _head_sel(x, h, dk):
    if dk == LANES:
        return x[:, h * LANES:(h + 1) * LANES]
    return jnp.where(_lane_mask(x.shape, h * HALF, (h + 1) * HALF), x, jnp.zeros_like(x))


def _head_other(x, h, dk):
    return x[:, h * LANES:(h + 1) * LANES] if dk == LANES else x


def _pair(parts, dk=HALF):
    if dk == LANES:
        return jnp.concatenate(parts, axis=1)
    return jnp.where(_lane_mask(parts[0].shape, 0, HALF), parts[0], parts[1])


def _attn_fwd(q, k, v, crow, *, dk, npairs, scale, name, gather=()):
    (qa, qo), (ka, ko), (va, vo) = q, k, v
    s = qa.shape[0]
    wq = 2 * dk
    t = _pick(s, (512, 256, 128))
    nb = s // t
    bias = crow is not None
    ng = len(gather)

    def body(*refs):
        n_in = 3 + bias
        q_ref, k_ref, v_ref = refs[:3]
        cr_ref = refs[3] if bias else None
        o_ref, lse_ref = refs[n_in + ng:n_in + ng + 2]
        qi = pl.program_id(1)
        if ng:
            phases = _gather_phases([a.shape for a in gather], refs[n_in:n_in + ng], refs[n_in + ng + 2:n_in + 2 * ng + 2],
                                    *refs[n_in + 2 * ng + 2:])
            pair = pl.program_id(0)
            pl.when((pair == 0) & (qi == 0))(phases[0])
            pl.when((pair == npairs - 1) & (qi == 0))(phases[1])
        qb = q_ref[...]
        qs = [_head_sel(qb, h, dk) for h in range(2)]

        def scores(j):
            rows = pl.ds(pl.multiple_of(j * t, t), t)
            kb = k_ref[rows, :]
            out = []
            for h in range(2):
                sc = lax.dot_general(qs[h], _head_other(kb, h, dk), _NT, preferred_element_type=F32) * scale
                out.append(sc - cr_ref[h, j] if bias else sc)
            return tuple(out)

        def update(scs, j, state):
            rows = pl.ds(pl.multiple_of(j * t, t), t)
            vb = v_ref[rows, :]
            soft = []
            for h in range(2):
                m, l, _ = state[3 * h:3 * h + 3]
                sc = scs[h]
                m_new = jnp.maximum(m, jnp.max(sc, axis=1, keepdims=True))
                a = jnp.exp(m - m_new)
                p = jnp.exp(sc - m_new)
                soft.append((m_new, a * l + jnp.sum(p, axis=1, keepdims=True), a, p.astype(BF16)))
            new = []
            for h in range(2):
                m_new, l, a, p = soft[h]
                new += [m_new, l, a * state[3 * h + 2] + jnp.dot(p, vb, preferred_element_type=F32)]
            return tuple(new)

        one = (jnp.full((t, 1), -jnp.inf, F32), jnp.zeros((t, 1), F32), jnp.zeros((t, LANES), F32))
        state = lax.fori_loop(0, qi, lambda j, st: update(scores(j), j, st), one + one)
        row = lax.broadcasted_iota(jnp.int32, (t, t), 0)
        col = lax.broadcasted_iota(jnp.int32, (t, t), 1)
        diag = tuple(jnp.where(col <= row, sc, -jnp.inf) for sc in scores(qi))
        state = update(diag, qi, state)
        outs = []
        for h in range(2):
            m, l, acc = state[3 * h:3 * h + 3]
            outs.append(acc / l)
            lse_ref[h] = m + jnp.log(l)
        o_ref[...] = _pair(outs)
        if ng:
            pl.when((pair == npairs - 1) & (qi == nb - 1))(phases[2])

    in_specs = [pl.BlockSpec((t, wq), lambda p, i: (i, qo // wq + p)), pl.BlockSpec((s, wq), lambda p, i: (0, ko // wq + p)),
                pl.BlockSpec((s, LANES), lambda p, i: (0, vo // LANES + p))]
    args = [qa, ka, va]
    if bias:
        in_specs.append(pl.BlockSpec((2, nb, 1, t), lambda p, i: (p, 0, 0, 0)))
        args.append(crow.reshape(2 * npairs, nb, 1, t))
    out = pl.pallas_call(
        body, name=name, grid=(npairs, nb), in_specs=in_specs + [_ANY] * ng,
        out_specs=[pl.BlockSpec((t, LANES), lambda p, i: (i, p)), pl.BlockSpec((2, t, 1), lambda p, i: (p, i, 0))] + [_ANY] * ng,
        out_shape=[jax.ShapeDtypeStruct((s, npairs * LANES), F32), jax.ShapeDtypeStruct((2 * npairs, s, 1), F32)]
        + [jax.ShapeDtypeStruct((N_CHIPS,) + a.shape, a.dtype) for a in gather],
        scratch_shapes=_sem_pairs(GATHER_COPIES * ng) if ng else [],
        compiler_params=_cparams((_ARB, _ARB) if ng else (_PAR, _PAR)),
    )(*args, *gather)
    return (out[0], out[1], out[2:]) if ng else (out[0], out[1])


def _attn_bwd(q, k, v, o, do, lse, crow, *, dk, npairs, scale, dq_dtype, name, scatter=()):
    (qa, qo), (ka, ko), (va, vo), (oa, oo), (da, do_o) = q, k, v, o, do
    s = qa.shape[0]
    wq = 2 * dk
    t = _pick(s, (512, 256, 128))
    nb = s // t
    bias = crow is not None
    ns = len(scatter)

    def body(*refs):
        n_in, n_out = 6 + bias, 3 + 2 * bias
        q_ref, k_ref, v_ref, o_ref, do_ref, lse_ref = refs[:6]
        outs = refs[n_in + ns:n_in + ns + n_out]
        scr = refs[n_in + 2 * ns + n_out:]
        dq_ref, dk_ref, dv_ref = outs[:3]
        dq_s, dk_s, dv_s = scr[:3]
        if bias:
            cr_ref, dc_ref, dcc_ref, dc_s, rs_s = refs[6], outs[3], outs[4], scr[3], scr[4]
        ki, qi = pl.program_id(1), pl.program_id(2)
        if ns:
            send_off, finish = _scatter_phases(ns, refs[n_in:n_in + ns], refs[n_in + ns + n_out:n_in + 2 * ns + n_out], *scr[3 + 2 * bias:])
            pair = pl.program_id(0)
            pl.when((pair == 0) & (ki == 0) & (qi == 0))(send_off)

        @pl.when((ki == 0) & (qi == 0))
        def _():
            dq_s[...] = jnp.zeros_like(dq_s)
            if bias:
                rs_s[...] = jnp.zeros_like(rs_s)

        @pl.when(qi == ki)
        def _():
            dk_s[...] = jnp.zeros_like(dk_s)
            dv_s[...] = jnp.zeros_like(dv_s)
            if bias:
                dc_s[...] = jnp.zeros_like(dc_s)

        def block(on_diagonal):
            qb, kb, vb, dob, ob = q_ref[...], k_ref[...], v_ref[...], do_ref[...], o_ref[...]
            rows = pl.ds(pl.multiple_of(qi * t, t), t)
            row = lax.broadcasted_iota(jnp.int32, (t, t), 0) + qi * t
            col = lax.broadcasted_iota(jnp.int32, (t, t), 1) + ki * t
            fed = []
            for h in range(2):
                kh, doh = _head_other(kb, h, dk), _head_sel(dob, h, HALF)
                sc = lax.dot_general(_head_sel(qb, h, dk), kh, _NT, preferred_element_type=F32)
                dp = lax.dot_general(doh, vb, _NT, preferred_element_type=F32)
                fed.append((kh, doh, sc, dp))
            mid = []
            for h in range(2):
                kh, doh, sc, dp = fed[h]
                sc = sc * scale
                if bias:
                    sc = sc - cr_ref[h]
                if on_diagonal:
                    sc = jnp.where(col <= row, sc, -jnp.inf)
                p = jnp.exp(sc - lse_ref[h])
                delta = jnp.sum(doh.astype(F32) * ob, axis=1, keepdims=True)
                ds = p * (dp - delta)
                mid.append((kh, p.astype(BF16), (ds * scale).astype(BF16)))
            dq_parts, dk_parts, dv_parts = [], [], []
            for h in range(2):
                kh, pb, dsb = mid[h]
                dv_parts.append(lax.dot_general(pb, dob, _TN, preferred_element_type=F32))
                dk_parts.append(lax.dot_general(dsb, _head_other(qb, h, dk), _TN, preferred_element_type=F32))
                if bias:
                    dc_s[h] -= jnp.sum(dsb.astype(F32), axis=0, keepdims=True) * (1.0 / scale)
                    kh = jnp.where(_lane_mask(kh.shape, h * HALF, (h + 1) * HALF), kh, jnp.ones_like(kh))
                dq_parts.append(jnp.dot(dsb, kh, preferred_element_type=F32))
            dv_s[...] += _pair(dv_parts)
            dk_s[...] += _pair(dk_parts, dk)
            dq_s[rows, :] += _pair(dq_parts, dk)
            if bias:
                rs_s[rows, :] += _pair(dq_parts[::-1])

        pl.when(qi > ki)(lambda: block(False))
        pl.when(qi == ki)(lambda: block(True))

        @pl.when(qi == nb - 1)
        def _():
            dk_ref[...] = dk_s[...].astype(dk_ref.dtype)
            dv_ref[...] = dv_s[...].astype(dv_ref.dtype)
            if bias:
                dc_ref[...] = dc_s[...]

        @pl.when((ki == nb - 1) & (qi == nb - 1))
        def _():
            dq_ref[...] = dq_s[...].astype(dq_ref.dtype)
            if bias:
                dcc_ref[...] = rs_s[...] * (1.0 / scale)

        if ns:
            pl.when((pair == npairs - 1) & (ki == nb - 1) & (qi == nb - 1))(finish)

    qrow = lambda p, j, i: jnp.maximum(i, j)
    in_specs = [pl.BlockSpec((t, wq), lambda p, j, i: (qrow(p, j, i), qo // wq + p)),
                pl.BlockSpec((t, wq), lambda p, j, i: (j, ko // wq + p)),
                pl.BlockSpec((t, LANES), lambda p, j, i: (j, vo // LANES + p)),
                pl.BlockSpec((t, LANES), lambda p, j, i: (qrow(p, j, i), oo // LANES + p)),
                pl.BlockSpec((t, LANES), lambda p, j, i: (qrow(p, j, i), do_o // LANES + p)),
                pl.BlockSpec((2, t, 1), lambda p, j, i: (p, qrow(p, j, i), 0))]
    args = [qa, ka, va, oa, da, lse]
    out_specs = [pl.BlockSpec((s, wq), lambda p, j, i: (0, p)), pl.BlockSpec((t, wq), lambda p, j, i: (j, p)),
                 pl.BlockSpec((t, LANES), lambda p, j, i: (j, p))]
    out_shape = [jax.ShapeDtypeStruct((s, npairs * wq), dq_dtype), jax.ShapeDtypeStruct((s, npairs * wq), dq_dtype),
                 jax.ShapeDtypeStruct((s, npairs * LANES), BF16)]
    scratch = [pltpu.VMEM((s, wq), F32), pltpu.VMEM((t, wq), F32), pltpu.VMEM((t, LANES), F32)]
    if bias:
        in_specs.append(pl.BlockSpec((2, 1, t), lambda p, j, i: (p, 0, j)))
        args.append(crow)
        assert dk == HALF, "the sums over the keys ride in the unused half of the dq product of a 64-wide head pair"
        out_specs += [pl.BlockSpec((2, 1, t), lambda p, j, i: (p, 0, j)), pl.BlockSpec((s, LANES), lambda p, j, i: (0, p))]
        out_shape += [jax.ShapeDtypeStruct((2 * npairs, 1, s), F32), jax.ShapeDtypeStruct((s, npairs * LANES), F32)]
        scratch += [pltpu.VMEM((2, 1, t), F32), pltpu.VMEM((s, LANES), F32)]
    n_out = len(out_shape)
    out = pl.pallas_call(
        body, name=name, grid=(npairs, nb, nb), in_specs=in_specs + [_ANY] * ns, out_specs=out_specs + [_ANY] * ns,
        out_shape=out_shape + [jax.ShapeDtypeStruct((3,) + a.shape[1:], a.dtype) for a in scatter],
        scratch_shapes=scratch + (_sem_pairs(3 * ns) if ns else []),
        compiler_params=_cparams((_ARB if ns else _PAR, _ARB, _ARB)),
    )(*args, *scatter)
    return (*out[:n_out], out[n_out:]) if ns else out


def _sub_mask(shape, lo, hi):
    sub = lax.broadcasted_iota(jnp.int32, shape, 0)
    return (sub >= lo) & (sub < hi)


def _pair_rows(parts):
    return jnp.where(_sub_mask(parts[0].shape, 0, HALF), parts[0], parts[1])


def _swa_valid(start, t):
    krow = lax.broadcasted_iota(jnp.int32, (t + WINDOW, t), 0)
    qcol = lax.broadcasted_iota(jnp.int32, (t + WINDOW, t), 1)
    diff = qcol - krow + WINDOW
    return (diff >= 0) & (diff < WINDOW) & (krow + start >= WINDOW)


def _swa_tiles(s, most):
    t = _pick(s, (256, 128))
    return t, _pick(s // t, tuple(n for n in (8, 4, 2, 1) if n <= most))


def _swa_fwd(q, kp, vp, sink, *, name):
    s = q.shape[0]
    npairs = SWA_HEADS // 2
    t, nt = _swa_tiles(s, 8)
    scale = SWA_DIM ** -0.5

    def body(q_ref, k_ref, v_ref, sk_ref, o_ref, lse_ref):
        sts, vws = [], []
        for u in range(nt):
            start = pl.multiple_of((pl.program_id(1) * nt + u) * t, t)
            kw = k_ref[pl.ds(start, t + WINDOW), :]
            vws.append(v_ref[pl.ds(start, t + WINDOW), :])
            qb = q_ref[u * t:(u + 1) * t, :]
            valid = _swa_valid(start, t)
            for h in range(2):
                st = lax.dot_general(kw, _head_sel(qb, h, HALF), _NT, preferred_element_type=F32) * scale
                sts.append(jnp.where(valid, st, -jnp.inf))
        pts = []
        for u in range(nt):
            for h in range(2):
                st = sts[2 * u + h]
                snk = sk_ref[h][:, 0:1]
                m = jnp.maximum(jnp.max(st, axis=0, keepdims=True), snk)
                e = jnp.exp(st - m)
                l = jnp.sum(e, axis=0, keepdims=True) + jnp.exp(snk - m)
                pts.append((e * (1.0 / l)).astype(BF16))
                lse_ref[h, :, u * t:(u + 1) * t] = m + jnp.log(l)
        for u in range(nt):
            outs = [lax.dot_general(vws[u], pts[2 * u + h], _TN, preferred_element_type=F32) for h in range(2)]
            o_ref[:, u * t:(u + 1) * t] = _pair_rows(outs)

    kvspec = pl.BlockSpec((s + WINDOW, LANES), lambda p, i: (0, 0))
    return pl.pallas_call(
        body, name=name, grid=(npairs, s // (t * nt)),
        in_specs=[pl.BlockSpec((t * nt, LANES), lambda p, i: (i, p)), kvspec, kvspec, pl.BlockSpec((2, 1, LANES), lambda p, i: (p, 0, 0))],
        out_specs=[pl.BlockSpec((LANES, t * nt), lambda p, i: (p, i)), pl.BlockSpec((2, 1, t * nt), lambda p, i: (p, 0, i))],
        out_shape=[jax.ShapeDtypeStruct((npairs * LANES, s), F32), jax.ShapeDtypeStruct((SWA_HEADS, 1, s), F32)],
        compiler_params=_cparams((_PAR, _PAR)),
    )(q, kp, vp, sink)


def _swa_bwd(q, kp, vp, kpt, sink, ot, do, dot, lse, *, name):
    s = q.shape[0]
    npairs = SWA_HEADS // 2
    t, nt = _swa_tiles(s, 4)
    scale = SWA_DIM ** -0.5

    def body(q_ref, k_ref, v_ref, kt_ref, sk_ref, ot_ref, do_ref, dot_ref, lse_ref, dq_ref, dk_ref, dv_ref, dsk_ref):
        pp, i = pl.program_id(0), pl.program_id(1)

        @pl.when((pp == 0) & (i == 0))
        def _():
            dk_ref[...] = jnp.zeros_like(dk_ref)
            dv_ref[...] = jnp.zeros_like(dv_ref)

        @pl.when(i == 0)
        def _():
            dsk_ref[...] = jnp.zeros_like(dsk_ref)

        fed = []
        for u in range(nt):
            cols = slice(u * t, (u + 1) * t)
            start = pl.multiple_of((i * nt + u) * t, t)
            win = pl.ds(start, t + WINDOW)
            qb, dob = q_ref[cols, :], do_ref[cols, :]
            kw, vw = k_ref[win, :], v_ref[win, :]
            valid = _swa_valid(start, t)
            for h in range(2):
                st = lax.dot_general(kw, _head_sel(qb, h, HALF), _NT, preferred_element_type=F32)
                dpt = lax.dot_general(vw, _head_sel(dob, h, HALF), _NT, preferred_element_type=F32)
                fed.append((cols, win, qb, dob, valid, st, dpt))
        mid = []
        for u in range(nt):
            cols = fed[2 * u][0]
            prod = dot_ref[:, cols].astype(F32) * ot_ref[:, cols]
            for h in range(2):
                valid, st, dpt = fed[2 * u + h][4:]
                lse_b = lse_ref[h, :, cols]
                pt = jnp.exp(jnp.where(valid, st * scale, -jnp.inf) - lse_b)
                delta = jnp.sum(jnp.where(_sub_mask(prod.shape, h * HALF, (h + 1) * HALF), prod, 0.0), axis=0, keepdims=True)
                dst = pt * (dpt - delta)
                psink = jnp.exp(sk_ref[h][:, 0:1] - lse_b)
                dsk_ref[h] += jnp.broadcast_to(-jnp.sum(psink * delta, axis=1, keepdims=True), (1, LANES))
                mid.append((pt.astype(BF16), (dst * scale).astype(BF16)))
        for u in range(nt):
            cols, win, qb, dob = fed[2 * u][:4]
            ktw = kt_ref[:, win]
            dq_parts = [jnp.dot(ktw, mid[2 * u + h][1], preferred_element_type=F32) for h in range(2)]
            dk_parts = [jnp.dot(mid[2 * u + h][1], qb, preferred_element_type=F32) for h in range(2)]
            dv_parts = [jnp.dot(mid[2 * u + h][0], dob, preferred_element_type=F32) for h in range(2)]
            dq_ref[:, cols] = _pair_rows(dq_parts)
            dk_ref[win, :] += _pair(dk_parts)
            dv_ref[win, :] += _pair(dv_parts)

    tile = pl.BlockSpec((t * nt, LANES), lambda p, i: (i, p))
    ttile = pl.BlockSpec((LANES, t * nt), lambda p, i: (p, i))
    kvspec = pl.BlockSpec((s + WINDOW, LANES), lambda p, i: (0, 0))
    ktspec = pl.BlockSpec((LANES, s + WINDOW), lambda p, i: (0, 0))
    skspec = pl.BlockSpec((2, 1, LANES), lambda p, i: (p, 0, 0))
    return pl.pallas_call(
        body, name=name, grid=(npairs, s // (t * nt)),
        in_specs=[tile, kvspec, kvspec, ktspec, skspec, ttile, tile, ttile, pl.BlockSpec((2, 1, t * nt), lambda p, i: (p, 0, i))],
        out_specs=[ttile, kvspec, kvspec, skspec],
        out_shape=[jax.ShapeDtypeStruct((npairs * LANES, s), F32), jax.ShapeDtypeStruct((s + WINDOW, LANES), F32),
                   jax.ShapeDtypeStruct((s + WINDOW, LANES), F32), jax.ShapeDtypeStruct((SWA_HEADS, 1, LANES), F32)],
        compiler_params=_cparams((_ARB, _ARB)),
    )(q, kp, vp, kpt, sink, ot, do, dot, lse)


def _odd_rope(h, cos, sin, *, name):
    s = h.shape[0]
    tb = _pick(s, (512, 256, 128))
    nq = SWA_HEADS * SWA_DIM

    def body(q_ref, kv_ref, cos_ref, sin_ref, qo_ref, ko_ref, vo_ref):
        cs, sn = cos_ref[...], sin_ref[...]
        for j in range(nq // LANES):
            sl = slice(j * LANES, (j + 1) * LANES)
            qo_ref[:, sl] = _rope_tile(q_ref[:, sl], cs, sn, SWA_DIM // 2).astype(BF16)
        ko_ref[...] = _rope_tile(kv_ref[:, :LANES], cs, sn, SWA_DIM // 2).astype(BF16)
        vo_ref[...] = kv_ref[:, LANES:].astype(BF16)

    rows = lambda w, c=0: pl.BlockSpec((tb, w), lambda i: (i, c))
    return pl.pallas_call(
        body, name=name, grid=(s // tb,),
        in_specs=[rows(nq, O_Q // nq), rows(2 * LANES, O_K // (2 * LANES)), rows(LANES), rows(LANES)],
        out_specs=[rows(nq), rows(LANES), rows(LANES)],
        out_shape=[jax.ShapeDtypeStruct((s, nq), BF16), jax.ShapeDtypeStruct((s, LANES), BF16), jax.ShapeDtypeStruct((s, LANES), BF16)],
        compiler_params=_cparams((_PAR,)),
    )(h, h, cos, sin)


def _odd_rope_bwd(dq_t, dk, cos, sin, *, name):
    s = dk.shape[0]
    tb = _pick(s, (256, 128))
    nq = SWA_HEADS * SWA_DIM

    def body(dq_ref, dk_ref, cos_ref, sin_ref, qo_ref, ko_ref):
        cs, sn = cos_ref[...], -sin_ref[...]
        dq = dq_ref[...].T
        for j in range(nq // LANES):
            sl = slice(j * LANES, (j + 1) * LANES)
            qo_ref[:, sl] = _rope_tile(dq[:, sl], cs, sn, SWA_DIM // 2).astype(BF16)
        ko_ref[...] = _rope_tile(dk_ref[...], cs, sn, SWA_DIM // 2).astype(BF16)

    rows = lambda w: pl.BlockSpec((tb, w), lambda i: (i, 0))
    return pl.pallas_call(
        body, name=name, grid=(s // tb,),
        in_specs=[pl.BlockSpec((nq, tb), lambda i: (0, i)), rows(LANES), rows(LANES), rows(LANES)],
        out_specs=[rows(nq), rows(LANES)],
        out_shape=[jax.ShapeDtypeStruct((s, nq), BF16), jax.ShapeDtypeStruct((s, LANES), BF16)],
        compiler_params=_cparams((_PAR,)),
    )(dq_t, dk, cos, sin)


def _mixer_out(o_refs, o_t):
    if o_t:
        return o_refs[0][...].T
    return o_refs[0][...] if len(o_refs) == 1 else jnp.concatenate([r[...] for r in o_refs], axis=1)


def _mixer_specs(o_parts, o_t, tb):
    if o_t:
        return [pl.BlockSpec((D_MODEL, tb), lambda i: (0, i))]
    return [pl.BlockSpec((tb, a.shape[1]), lambda i: (i, 0)) for a in o_parts]


def _out_fwd(o_parts, h, x, w_out, g, b, *, name, o_t=False):
    s = x.shape[0]
    tb = _pick(s, (256, 128))
    n_o = len(o_parts)

    def body(*refs):
        o_refs = refs[:n_o]
        gate_ref, x_ref, w_ref, g_ref, b_ref, xn_ref, xb_ref, z_ref, xh_ref, rs_ref = refs[n_o:]
        gate = gate_ref[...]
        o = _mixer_out(o_refs, o_t)
        z = (o * (gate * _sigmoid(gate))).astype(BF16)
        z_ref[...] = z
        r = ALPHA * x_ref[...] + jnp.dot(z, w_ref[...], preferred_element_type=F32)
        mu = jnp.mean(r, axis=-1, keepdims=True)
        rc = r - mu
        rstd = lax.rsqrt(jnp.mean(rc * rc, axis=-1, keepdims=True) + LN_EPS)
        xh = rc * rstd
        xn = xh * g_ref[...] + b_ref[...]
        xh_ref[...] = xh
        rs_ref[...] = rstd
        xn_ref[...] = xn
        xb_ref[...] = xn.astype(BF16)

    rows = lambda w: pl.BlockSpec((tb, w), lambda i: (i, 0))
    full = lambda a: pl.BlockSpec(a.shape, lambda i: (0,) * a.ndim)
    return pl.pallas_call(
        body, name=name, grid=(s // tb,),
        in_specs=_mixer_specs(o_parts, o_t, tb) + [rows(D_MODEL), rows(D_MODEL), full(w_out), full(g), full(b)],
        out_specs=[rows(D_MODEL), rows(D_MODEL), rows(D_MODEL), rows(D_MODEL), rows(1)],
        out_shape=[jax.ShapeDtypeStruct((s, D_MODEL), F32), jax.ShapeDtypeStruct((s, D_MODEL), BF16),
                   jax.ShapeDtypeStruct((s, D_MODEL), BF16), jax.ShapeDtypeStruct((s, D_MODEL), F32),
                   jax.ShapeDtypeStruct((s, 1), F32)],
        compiler_params=_cparams((_PAR,)),
    )(*o_parts, h, x, w_out, g, b)


def _out_bwd(dxn, xh, rstd, g, w_out_t, o_parts, h, *, name, o_t=False):
    s = dxn.shape[0]
    tb = _pick(s, (256, 128))
    n_o = len(o_parts)

    def body(*refs):
        dxn_ref, xh_ref, rs_ref, g_ref, wt_ref = refs[:5]
        o_refs = refs[5:5 + n_o]
        gate_ref, dr_ref, dy_ref, do_ref, dgate_ref, dg_ref, db_ref = refs[5 + n_o:12 + n_o]

        @pl.when(pl.program_id(0) == 0)
        def _():
            dg_ref[...] = jnp.zeros_like(dg_ref)
            db_ref[...] = jnp.zeros_like(db_ref)

        dxn_b, xh_b = dxn_ref[...], xh_ref[...]
        dg_ref[...] += jnp.sum(dxn_b * xh_b, axis=0, keepdims=True)
        db_ref[...] += jnp.sum(dxn_b, axis=0, keepdims=True)
        dxh = dxn_b * g_ref[...]
        dr = rs_ref[...] * (dxh - jnp.mean(dxh, axis=-1, keepdims=True) - xh_b * jnp.mean(dxh * xh_b, axis=-1, keepdims=True))
        dr_ref[...] = dr
        dy = dr.astype(BF16)
        dy_ref[...] = dy
        dz = jnp.dot(dy, wt_ref[...], preferred_element_type=F32)
        gate = gate_ref[...]
        sg = _sigmoid(gate)
        o = _mixer_out(o_refs, o_t)
        do = dz * (gate * sg)
        do_ref[...] = do.astype(BF16)
        if o_t:
            refs[12 + n_o][...] = do.T.astype(BF16)
        dgate_ref[...] = (dz * o * (sg * (1.0 + gate * (1.0 - sg)))).astype(BF16)

    rows = lambda w: pl.BlockSpec((tb, w), lambda i: (i, 0))
    full = lambda a: pl.BlockSpec(a.shape, lambda i: (0,) * a.ndim)
    cols = pl.BlockSpec((D_MODEL, tb), lambda i: (0, i))
    return pl.pallas_call(
        body, name=name, grid=(s // tb,),
        in_specs=[rows(D_MODEL), rows(D_MODEL), rows(1), full(g), full(w_out_t)] + _mixer_specs(o_parts, o_t, tb) + [rows(D_MODEL)],
        out_specs=[rows(D_MODEL), rows(D_MODEL), rows(D_MODEL), rows(D_MODEL), full(g), full(g)] + [cols] * o_t,
        out_shape=[jax.ShapeDtypeStruct((s, D_MODEL), F32), jax.ShapeDtypeStruct((s, D_MODEL), BF16),
                   jax.ShapeDtypeStruct((s, D_MODEL), BF16), jax.ShapeDtypeStruct((s, D_MODEL), BF16),
                   jax.ShapeDtypeStruct(g.shape, F32), jax.ShapeDtypeStruct(g.shape, F32)]
        + [jax.ShapeDtypeStruct((D_MODEL, s), BF16)] * o_t,
        compiler_params=_cparams((_ARB,)),
    )(dxn, xh, rstd, g, w_out_t, *o_parts, h)


def _loss_grad(y, target, *, name):
    s, d = y.shape
    tb = _pick(s, (512, 256, 128))

    def body(y_ref, t_ref, dy_ref, l_ref):
        @pl.when(pl.program_id(0) == 0)
        def _():
            l_ref[...] = jnp.zeros_like(l_ref)

        err = y_ref[...] - t_ref[...]
        dy_ref[...] = err * (1.0 / d)
        per_tok = jnp.mean(err * err, axis=-1, keepdims=True)
        l_ref[...] += 0.5 * jnp.sum(per_tok, axis=0, keepdims=True)

    rows = pl.BlockSpec((tb, d), lambda i: (i, 0))
    return pl.pallas_call(
        body, name=name, grid=(s // tb,), in_specs=[rows, rows],
        out_specs=[rows, pl.BlockSpec((8, LANES), lambda i: (0, 0))],
        out_shape=[jax.ShapeDtypeStruct((s, d), F32), jax.ShapeDtypeStruct((8, LANES), F32)],
        compiler_params=_cparams((_ARB,)),
    )(y, target)


def _adamw(w, g, m, v, *, name):
    shape = w.shape
    w3, g3, m3, v3 = (a.reshape((1,) * (3 - a.ndim) + a.shape) for a in (w, g, m, v))
    a0, a1, a2 = w3.shape
    tb = _pick(a1, (256, 128)) if a1 % 8 == 0 else a1
    c1 = 1.0 - ADAM_B1 ** ADAM_STEP
    c2 = 1.0 - ADAM_B2 ** ADAM_STEP

    def body(w_ref, g_ref, m_ref, v_ref, d_ref, mo_ref, vo_ref):
        gg = g_ref[...]
        mn = ADAM_B1 * m_ref[...] + (1.0 - ADAM_B1) * gg
        vn = ADAM_B2 * v_ref[...] + (1.0 - ADAM_B2) * (gg * gg)
        mo_ref[...] = mn
        vo_ref[...] = vn
        d_ref[...] = -ADAM_LR * ((mn / c1) / (jnp.sqrt(vn / c2) + ADAM_EPS) + ADAM_WD * w_ref[...])

    spec = pl.BlockSpec((1, tb, a2), lambda i, j: (i, j, 0))
    outs = pl.pallas_call(
        body, name=name, grid=(a0, a1 // tb), in_specs=[spec] * 4, out_specs=[spec] * 3,
        out_shape=[jax.ShapeDtypeStruct(w3.shape, F32)] * 3, compiler_params=_cparams((_PAR, _PAR)),
    )(w3, g3, m3, v3)
    return tuple(a.reshape(shape) for a in outs)


def _place():
    x, y, c = lax.axis_index("x"), lax.axis_index("y"), lax.axis_index("c")
    return x, y, c, [(1 - x, y), (x, 1 - y), (1 - x, 1 - y)]


_ANY = pl.BlockSpec(memory_space=pl.ANY)


def _sem_pairs(n):
    return [pltpu.SemaphoreType.DMA((n,)), pltpu.SemaphoreType.DMA((n,))]


GATHER_COPIES = 7


def _gather_phases(shapes, srcs, outs, send_sems, recv_sems):
    nt = len(shapes)
    x, y, c, chips = _place()
    me = 2 * x + y
    sib = (x, y, 1 - c)

    def half(t, chip, hc):
        rh = shapes[t][1] // 2
        return outs[t].at[chip, :, pl.ds(hc * rh, rh), :]

    def copy(t, kk, s_ref, d_ref, to):
        return pltpu.make_async_remote_copy(src_ref=s_ref, dst_ref=d_ref, send_sem=send_sems.at[GATHER_COPIES * t + kk],
                                            recv_sem=recv_sems.at[GATHER_COPIES * t + kk], device_id=to, device_id_type=MESH)

    def first_hop(t, j):
        rh = shapes[t][1] // 2
        cx, cy = chips[j]
        return copy(t, j, srcs[t].at[:, pl.ds(c * rh, rh), :], half(t, me, c), (cx, cy, c))

    def passed(t, j):
        cx, cy = chips[j]
        landed = half(t, 2 * cx + cy, c)
        return copy(t, 3 + j, landed, landed, sib)

    def own(t):
        return copy(t, 6, srcs[t], outs[t].at[me], sib)

    pairs = [(t, j) for j in range(3) for t in range(nt)]

    def start():
        for t, j in pairs:
            first_hop(t, j).start()
        for t in range(nt):
            own(t).start()

    def pass_on():
        for t, j in pairs:
            cx, cy = chips[j]
            landed = half(t, 2 * cx + cy, c)
            copy(t, j, landed, landed, (cx, cy, c)).wait_recv()
            passed(t, j).start()

    def finish():
        for t, j in pairs:
            cx, cy = chips[j]
            theirs = half(t, 2 * cx + cy, 1 - c)
            copy(t, 3 + j, theirs, theirs, sib).wait_recv()
        for t in range(nt):
            own(t).wait()
        for t, j in pairs:
            first_hop(t, j).wait_send()
            passed(t, j).wait_send()

    return start, pass_on, finish


def _gather_chip_shards(tensors, *, name):
    nt = len(tensors)

    def body(*refs):
        for phase in _gather_phases([a.shape for a in tensors], refs[:nt], refs[nt:2 * nt], *refs[2 * nt:]):
            phase()

    return pl.pallas_call(
        body, name=name, in_specs=[_ANY] * nt, out_specs=[_ANY] * nt,
        out_shape=[jax.ShapeDtypeStruct((N_CHIPS,) + a.shape, a.dtype) for a in tensors],
        scratch_shapes=_sem_pairs(GATHER_COPIES * nt),
    )(*tensors)


def _swap_halves(gs, *, name):
    nt = len(gs)

    def body(*refs):
        srcs, outs = refs[:nt], refs[nt:2 * nt]
        send_sems, recv_sems = refs[2 * nt:]
        x, y, c, _ = _place()
        cps = [pltpu.make_async_remote_copy(src_ref=srcs[t].at[:, 1 - c], dst_ref=outs[t], send_sem=send_sems.at[t],
                                            recv_sem=recv_sems.at[t], device_id=(x, y, 1 - c), device_id_type=MESH)
               for t in range(nt)]
        for cp in cps:
            cp.start()
        for cp in cps:
            cp.wait()

    return pl.pallas_call(
        body, name=name, in_specs=[_ANY] * nt, out_specs=[_ANY] * nt,
        out_shape=[jax.ShapeDtypeStruct((N_CHIPS,) + g.shape[2:], g.dtype) for g in gs], scratch_shapes=_sem_pairs(nt),
    )(*gs)


def _scatter_phases(nt, srcs, outs, send_sems, recv_sems):
    x, y, c, chips = _place()

    def copies():
        return [pltpu.make_async_remote_copy(src_ref=srcs[t].at[2 * cx + cy], dst_ref=outs[t].at[j],
                                             send_sem=send_sems.at[3 * t + j], recv_sem=recv_sems.at[3 * t + j],
                                             device_id=(cx, cy, c), device_id_type=MESH)
                for j, (cx, cy) in enumerate(chips) for t in range(nt)]

    def start():
        for cp in copies():
            cp.start()

    def finish():
        for cp in copies():
            cp.wait()

    return start, finish


def _scatter_to_chips(blocks, *, name):
    nt = len(blocks)

    def body(*refs):
        for phase in _scatter_phases(nt, refs[:nt], refs[nt:2 * nt], *refs[2 * nt:]):
            phase()

    return pl.pallas_call(
        body, name=name, in_specs=[_ANY] * nt, out_specs=[_ANY] * nt,
        out_shape=[jax.ShapeDtypeStruct((3,) + a.shape[1:], a.dtype) for a in blocks], scratch_shapes=_sem_pairs(3 * nt),
    )(*blocks)


def _join_halves(bufs, *, name):
    nt = len(bufs)

    def body(*refs):
        srcs, outs = refs[:nt], refs[nt:2 * nt]
        send_sems, recv_sems = refs[2 * nt:]
        x, y, c, _ = _place()
        cps = []
        for t in range(nt):
            rh = bufs[t].shape[1] // 2
            rows = pl.ds(c * rh, rh)
            cps.append(pltpu.make_async_remote_copy(src_ref=srcs[t].at[:, rows, :], dst_ref=outs[t].at[:, rows, :],
                                                    send_sem=send_sems.at[t], recv_sem=recv_sems.at[t],
                                                    device_id=(x, y, 1 - c), device_id_type=MESH))
        for cp in cps:
            cp.start()
        for cp in cps:
            cp.wait()

    return pl.pallas_call(
        body, name=name, in_specs=[_ANY] * nt, out_specs=[_ANY] * nt,
        out_shape=[jax.ShapeDtypeStruct(b.shape, b.dtype) for b in bufs], scratch_shapes=_sem_pairs(nt),
        input_output_aliases={t: t for t in range(nt)},
    )(*bufs)


def _add_sibling(g, recv, cidx, chip_idx, *, name):
    _, _, na, rh, cdim = g.shape
    tb = _pick(rh, (256, 128, 64, 32, 16))

    def body(c_ref, k_ref, g_ref, r_ref, s_ref, o_ref):
        tot = g_ref[0, 0] + r_ref[0]
        s_ref[0] = tot.astype(BF16)

        @pl.when(pl.program_id(2) == k_ref[0])
        def _():
            o_ref[...] = tot

    return pl.pallas_call(
        body, name=name,
        grid_spec=pltpu.PrefetchScalarGridSpec(
            num_scalar_prefetch=2, grid=(na, rh // tb, N_CHIPS),
            in_specs=[pl.BlockSpec((1, 1, 1, tb, cdim), lambda a, i, k, c_ref, k_ref: (k, c_ref[0], a, i, 0)),
                      pl.BlockSpec((1, 1, tb, cdim), lambda a, i, k, c_ref, k_ref: (k, a, i, 0))],
            out_specs=[pl.BlockSpec((1, 1, tb, cdim), lambda a, i, k, c_ref, k_ref: (k, a, i, 0)),
                       pl.BlockSpec((1, tb, cdim), lambda a, i, k, c_ref, k_ref: (a, i, 0))]),
        out_shape=[jax.ShapeDtypeStruct((N_CHIPS, na, rh, cdim), BF16), jax.ShapeDtypeStruct((na, rh, cdim), F32)],
        compiler_params=_cparams((_PAR, _PAR, _ARB)),
    )(cidx, chip_idx, g, recv)


def _add_chips(own, recv, cidx, *, name):
    na, rh, cdim = own.shape
    tb = _pick(rh, (256, 128, 64, 32, 16))
    nblk = rh // tb

    def body(c_ref, a_ref, r0_ref, r1_ref, r2_ref, o_ref):
        o_ref[...] = ((a_ref[...] + r0_ref[0].astype(F32)) + r1_ref[0].astype(F32)) + r2_ref[0].astype(F32)

    slot = lambda j: pl.BlockSpec((1, 1, tb, cdim), lambda a, i, c_ref: (j, a, i, 0))
    return pl.pallas_call(
        body, name=name,
        grid_spec=pltpu.PrefetchScalarGridSpec(
            num_scalar_prefetch=1, grid=(na, nblk),
            in_specs=[pl.BlockSpec((1, tb, cdim), lambda a, i, c_ref: (a, i, 0)), slot(0), slot(1), slot(2)],
            out_specs=pl.BlockSpec((1, tb, cdim), lambda a, i, c_ref: (a, c_ref[0] * nblk + i, 0))),
        out_shape=jax.ShapeDtypeStruct((na, 2 * rh, cdim), F32), compiler_params=_cparams((_PAR, _PAR)),
    )(cidx, own, recv, recv, recv)


def _all_reduce_small(v, *, name):
    r, cdim = v.shape

    def body(v_ref, o_ref, buf, send_sems, recv_sems):
        x, y, c, _ = _place()
        me = 4 * x + 2 * y + c
        buf[me] = v_ref[...]
        cps = []
        for p in range(1, N_DEV):
            to = (1 - x if p & 4 else x, 1 - y if p & 2 else y, 1 - c if p & 1 else c)
            cp = pltpu.make_async_remote_copy(src_ref=v_ref, dst_ref=buf.at[me], send_sem=send_sems.at[p - 1],
                                              recv_sem=recv_sems.at[p - 1], device_id=to, device_id_type=MESH)
            cp.start()
            cps.append(cp)
        for p in range(1, N_DEV):
            frm = (4 * x + 2 * y + c) ^ p
            pltpu.make_async_remote_copy(src_ref=v_ref, dst_ref=buf.at[frm], send_sem=send_sems.at[p - 1],
                                         recv_sem=recv_sems.at[p - 1], device_id=(x, y, c), device_id_type=MESH).wait_recv()
        for cp in cps:
            cp.wait_send()
        tot = buf[0]
        for i in range(1, N_DEV):
            tot = tot + buf[i]
        o_ref[...] = tot

    vm = pl.BlockSpec(memory_space=pltpu.VMEM)
    return pl.pallas_call(
        body, name=name, in_specs=[vm], out_specs=vm, out_shape=jax.ShapeDtypeStruct((r, cdim), F32),
        scratch_shapes=[pltpu.VMEM((N_DEV, r, cdim), F32), pltpu.SemaphoreType.DMA((N_DEV - 1,)), pltpu.SemaphoreType.DMA((N_DEV - 1,))],
    )(v)


_SHARDED = ("even_w_in", "even_w_uq", "even_w_ukv", "even_w_out", "odd_w_in", "odd_w_out")
_COL_SHARDED = ("even_w_in", "even_w_uq", "even_w_ukv", "odd_w_in")
PACK_COLS = 1024


def _unshard(name, stacked):
    n, a, b, cc = stacked.shape
    if name in _COL_SHARDED:
        return stacked.transpose(1, 2, 0, 3).reshape(a, b, n * cc)
    return stacked.transpose(1, 0, 2, 3).reshape(a, n * b, cc)


def _chip_halves(name, full):
    a, b, cc = full.shape
    if name in _COL_SHARDED:
        return full.reshape(a, 2, b // 2, N_CHIPS, cc // N_CHIPS).transpose(3, 1, 0, 2, 4)
    return full.reshape(a, N_CHIPS, 2, b // (2 * N_CHIPS), cc).transpose(1, 2, 0, 3, 4)


def _rope_tables(s):
    pos = jnp.arange(s, dtype=F32)

    def ang(d):
        inv = ROPE_THETA ** (-jnp.arange(0, d, 2, dtype=F32) / d)
        a = pos[:, None] * inv[None, :]
        return jnp.cos(a), jnp.sin(a)

    c16, s16 = ang(MLA_ROPE)
    one, zero = jnp.ones((s, KPE_LANE), F32), jnp.zeros((s, KPE_LANE), F32)
    cos_m = jnp.concatenate([one, c16, c16, one[:, :32]], axis=1)
    sin_m = jnp.concatenate([zero, -s16, s16, zero[:, :32]], axis=1)
    c32, s32 = ang(SWA_DIM)
    cos_s = jnp.concatenate([c32, c32, c32, c32], axis=1)
    sin_s = jnp.concatenate([-s32, s32, -s32, s32], axis=1)
    return cos_m, sin_m, cos_s, sin_s


def _even_weights(w_in, w_uq, w_ukv):
    zeros = lambda n: jnp.zeros((D_MODEL, n), w_in.dtype)
    wcq, wckv, wkpe = w_in[:, 0:256], w_in[:, 256:384], w_in[:, 384:416]
    wfq, wfk, wfv = w_in[:, 416:928], w_in[:, 928:1440], w_in[:, 1440:1952]
    wfl, wg = w_in[:, 1952:1960], w_in[:, 1960:2984]
    misc = jnp.concatenate([zeros(KPE_LANE), wkpe, wfl, zeros(LANES - FL_LANE - FOX_HEADS)], axis=1)
    w_in_p = jnp.concatenate([wg, wfq, wfk, wfv, wcq, wckv, misc], axis=1)
    uq = w_uq.reshape(MLA_Q_RANK, MLA_HEADS, MLA_NOPE + MLA_ROPE)
    uq_p = jnp.pad(uq, ((0, 0), (0, 0), (0, HEAD_PAD - MLA_NOPE - MLA_ROPE))).reshape(MLA_Q_RANK, MLA_HEADS * HEAD_PAD)
    ukv = w_ukv.reshape(MLA_KV_RANK, MLA_HEADS, MLA_NOPE + MLA_V)
    uk_p = jnp.pad(ukv[..., :MLA_NOPE], ((0, 0), (0, 0), (0, HEAD_PAD - MLA_NOPE))).reshape(MLA_KV_RANK, MLA_HEADS * HEAD_PAD)
    uv = ukv[..., MLA_NOPE:].reshape(MLA_KV_RANK, MLA_HEADS * MLA_V)
    ukv_p = jnp.concatenate([uk_p, uv], axis=1)
    return w_in_p, w_in_p.T, uq_p, uq_p.T, ukv_p, ukv_p.T


def _even_weight_grads(dw_in_p, duq_p, dukv_p):
    g = dw_in_p
    gate, fq, fk, fv = g[:, E_GATE:E_FQ], g[:, E_FQ:E_FK], g[:, E_FK:E_FV], g[:, E_FV:E_SMALL]
    cq, ckv, misc = g[:, E_SMALL:E_SMALL + 256], g[:, E_SMALL + 256:E_SMALL + 384], g[:, E_SMALL + 384:]
    dw_in = jnp.concatenate([cq, ckv, misc[:, KPE_LANE:KPE_LANE + MLA_ROPE], fq, fk, fv, misc[:, FL_LANE:FL_LANE + FOX_HEADS], gate], axis=1)
    duq = duq_p.reshape(MLA_Q_RANK, MLA_HEADS, HEAD_PAD)[..., :MLA_NOPE + MLA_ROPE].reshape(MLA_Q_RANK, -1)
    nq = MLA_HEADS * HEAD_PAD
    dk = dukv_p[:, :nq].reshape(MLA_KV_RANK, MLA_HEADS, HEAD_PAD)[..., :MLA_NOPE]
    dvv = dukv_p[:, nq:].reshape(MLA_KV_RANK, MLA_HEADS, MLA_V)
    dukv = jnp.concatenate([dk, dvv], axis=-1).reshape(MLA_KV_RANK, -1)
    return dw_in, duq, dukv


def _interleave(w, forward):
    a, b = (SWA_KV_HEADS, SWA_HEADS // SWA_KV_HEADS) if forward else (SWA_HEADS // SWA_KV_HEADS, SWA_KV_HEADS)
    return w.reshape(w.shape[0], a, b, -1).transpose(0, 2, 1, 3).reshape(w.shape[0], -1)


def _odd_weights(w_in, w_out):
    q, k, v, gate = w_in[:, 0:1024], w_in[:, 1024:1152], w_in[:, 1152:1280], w_in[:, 1280:2304]
    w_p = jnp.concatenate([_interleave(gate, True), _interleave(q, True), k, v], axis=1)
    w_out_p = _interleave(w_out.T, True).T
    return w_p, w_p.T, w_out_p, w_out_p.T


def _odd_weight_grads(g, dw_out_p):
    dw_in = jnp.concatenate([_interleave(g[:, O_Q:O_K], False), g[:, O_K:O_V], g[:, O_V:], _interleave(g[:, O_GATE:O_Q], False)], axis=1)
    return dw_in, _interleave(dw_out_p.T, False).T


def kernel(x, even_w_in, even_q_norm, even_w_uq, even_kv_norm, even_w_ukv, even_b_f, even_w_out, even_ln_g, even_ln_b, odd_w_in, odd_sinks, odd_w_out, odd_ln_g, odd_ln_b, loss_target, m_even_w_in, m_even_q_norm, m_even_w_uq, m_even_kv_norm, m_even_w_ukv, m_even_b_f, m_even_w_out, m_even_ln_g, m_even_ln_b, m_odd_w_in, m_odd_sinks, m_odd_w_out, m_odd_ln_g, m_odd_ln_b, v_even_w_in, v_even_q_norm, v_even_w_uq, v_even_kv_norm, v_even_w_ukv, v_even_b_f, v_even_w_out, v_even_ln_g, v_even_ln_b, v_odd_w_in, v_odd_sinks, v_odd_w_out, v_odd_ln_g, v_odd_ln_b):
    weights = dict(even_w_in=even_w_in, even_q_norm=even_q_norm, even_w_uq=even_w_uq, even_kv_norm=even_kv_norm,
                   even_w_ukv=even_w_ukv, even_b_f=even_b_f, even_w_out=even_w_out, even_ln_g=even_ln_g, even_ln_b=even_ln_b,
                   odd_w_in=odd_w_in, odd_sinks=odd_sinks, odd_w_out=odd_w_out, odd_ln_g=odd_ln_g, odd_ln_b=odd_ln_b)
    mom_m = dict(even_w_in=m_even_w_in, even_q_norm=m_even_q_norm, even_w_uq=m_even_w_uq, even_kv_norm=m_even_kv_norm,
                 even_w_ukv=m_even_w_ukv, even_b_f=m_even_b_f, even_w_out=m_even_w_out, even_ln_g=m_even_ln_g, even_ln_b=m_even_ln_b,
                 odd_w_in=m_odd_w_in, odd_sinks=m_odd_sinks, odd_w_out=m_odd_w_out, odd_ln_g=m_odd_ln_g, odd_ln_b=m_odd_ln_b)
    mom_v = dict(even_w_in=v_even_w_in, even_q_norm=v_even_q_norm, even_w_uq=v_even_w_uq, even_kv_norm=v_even_kv_norm,
                 even_w_ukv=v_even_w_ukv, even_b_f=v_even_b_f, even_w_out=v_even_w_out, even_ln_g=v_even_ln_g, even_ln_b=v_even_ln_b,
                 odd_w_in=v_odd_w_in, odd_sinks=v_odd_sinks, odd_w_out=v_odd_w_out, odd_ln_g=v_odd_ln_g, odd_ln_b=v_odd_ln_b)
    names = list(weights)
    xl = x[0]
    tgt = loss_target[0]
    s = xl.shape[0]
    ax, ay, ac = lax.axis_index("x"), lax.axis_index("y"), lax.axis_index("c")
    chip = 2 * ax + ay
    c_idx = ac.astype(jnp.int32).reshape(1)
    chip_idx = chip.astype(jnp.int32).reshape(1)

    ln_odd = jnp.pad(jnp.concatenate([odd_ln_g, odd_ln_b]), ((0, 12), (0, 0)))[None]
    wb = {n: weights[n].astype(BF16) for n in _SHARDED}
    even_mats = [n for n in _SHARDED if n.startswith("even")]
    odd_mats = [n for n in _SHARDED if n.startswith("odd")]
    first_mats = [n for n in even_mats if n != "even_w_out"]
    gathered = _gather_chip_shards([wb[n][0:1] for n in first_mats], name="gather_first")
    full = {n: [None, None] for n in even_mats}
    for i, n in enumerate(first_mats):
        full[n][0] = _unshard(n, gathered[i])[0]
    later = [wb["even_w_out"][0:1]] + [wb[n][1:2] for n in even_mats] + [wb[n] for n in odd_mats] + [ln_odd]

    cos_m, sin_m, cos_s, sin_s = _rope_tables(s)
    bf_tiles = jnp.pad(even_b_f, ((0, 0), (FL_LANE, LANES - FL_LANE - FOX_HEADS)))
    sink_tiles = jnp.broadcast_to(_interleave(odd_sinks, True)[:, :, None, None], (2, SWA_HEADS, 1, LANES))
    mla_scale = (MLA_NOPE + MLA_ROPE) ** -0.5
    fox_scale = FOX_DIM ** -0.5
    mla_pairs, fox_pairs = MLA_HEADS // 2, FOX_HEADS // 2

    saved = []
    x_f, x_b = xl, xl.astype(BF16)
    for layer in range(DEPTH):
        j = layer // 2
        ln = f"L{layer}"
        if layer % 2 == 0:
            w_in_p, w_in_t, uq_p, uq_t, ukv_p, ukv_t = _even_weights(full["even_w_in"][j], full["even_w_uq"][j], full["even_w_ukv"][j])
            qg, kg, bft = even_q_norm[j][None], even_kv_norm[j][None], bf_tiles[j][None]
            h, hb = _mm(x_b, w_in_p, out_dtype=F32, name=ln + "_in", also_bf16=True)
            q, k, v, qn, kvn, logf = _even_mid_fwd(h, qg, kg, bft, uq_p, ukv_p, cos_m, sin_m, name=ln + "_mid")
            cum = _cumsum(logf, reverse=False, name=ln + "_cum")[:, FL_LANE:FL_LANE + FOX_HEADS].T
            crow = cum[:, None, :]
            if layer == 0:
                o_mla, lse_mla, rest = _attn_fwd((q, 0), (k, 0), (v, 0), None, dk=HEAD_PAD, npairs=mla_pairs, scale=mla_scale,
                                                 name=ln + "_mla", gather=later)
                full["even_w_out"][0] = _unshard("even_w_out", rest[0])[0]
                for i, n in enumerate(even_mats):
                    full[n][1] = _unshard(n, rest[1 + i])[0]
                for i, n in enumerate(odd_mats):
                    full[n] = _unshard(n, rest[1 + len(even_mats) + i])
                ln_all = rest[-1][:, 0]
                odd_g_full = ln_all[:, 0:2].transpose(1, 0, 2).reshape(2, D_MODEL)
                odd_b_full = ln_all[:, 2:4].transpose(1, 0, 2).reshape(2, D_MODEL)
            else:
                o_mla, lse_mla = _attn_fwd((q, 0), (k, 0), (v, 0), None, dk=HEAD_PAD, npairs=mla_pairs, scale=mla_scale, name=ln + "_mla")
            o_fox, lse_fox = _attn_fwd((hb, E_FQ), (hb, E_FK), (hb, E_FV), crow, dk=FOX_DIM, npairs=fox_pairs,
                                       scale=fox_scale, name=ln + "_fox")
            o_parts = [o_mla, o_fox]
            w_out, w_out_t = full["even_w_out"][j], full["even_w_out"][j].T
            g_ln, b_ln = even_ln_g[j][None], even_ln_b[j][None]
            x_n, x_nb, z, xh, rstd = _out_fwd(o_parts, h, x_f, w_out, g_ln, b_ln, name=ln + "_out")
            saved.append(dict(h=h, hb=hb, x_b=x_b, qn=qn, kvn=kvn, q=q, k=k, v=v, crow=crow,
                              o_mla=o_mla, o_fox=o_fox, lse_mla=lse_mla, lse_fox=lse_fox, o_parts=o_parts, z=z, xh=xh, rstd=rstd,
                              w_in_t=w_in_t, uq_t=uq_t, ukv_t=ukv_t, w_out_t=w_out_t, qg=qg, kg=kg, bft=bft, g_ln=g_ln))
        else:
            w_in_p, w_in_t, w_out, w_out_t = _odd_weights(full["odd_w_in"][j], full["odd_w_out"][j])
            h = _mm(x_b, w_in_p, out_dtype=F32, name=ln + "_in")
            q, k, v = _odd_rope(h, cos_s, sin_s, name=ln + "_rope")
            kp = jnp.pad(k, ((WINDOW, 0), (0, 0)))
            vp = jnp.pad(v, ((WINDOW, 0), (0, 0)))
            o_t, lse = _swa_fwd(q, kp, vp, sink_tiles[j], name=ln + "_swa")
            g_ln, b_ln = odd_g_full[j][None], odd_b_full[j][None]
            x_n, x_nb, z, xh, rstd = _out_fwd([o_t], h, x_f, w_out, g_ln, b_ln, name=ln + "_out", o_t=True)
            saved.append(dict(h=h, x_b=x_b, q=q, kp=kp, vp=vp, lse=lse, o_t=o_t, o_parts=[o_t], z=z, xh=xh, rstd=rstd,
                              w_in_t=w_in_t, w_out_t=w_out_t, g_ln=g_ln))
        x_f, x_b = x_n, x_nb

    dxn, loss_tile = _loss_grad(x_f, tgt, name="loss")

    def pair_sums(group, tag):
        gp = [_chip_halves(n, g) for _, n, g in group]
        from_sib = _swap_halves(gp, name="grad_swap_" + tag)
        return [_add_sibling(g, r, c_idx, chip_idx, name=f"grad_add_sibling_{lab}_{tag}")
                for (lab, _, _), g, r in zip(group, gp, from_sib)]

    def chip_sums(group, pair, from_chips, tag):
        mine = [_add_chips(p[1], r, c_idx, name=f"grad_add_chips_{lab}_{tag}") for (lab, _, _), p, r in zip(group, pair, from_chips)]
        return dict(zip([lab for lab, _, _ in group], _join_halves(mine, name="grad_join_" + tag)))

    grads = {n: [None, None] for n in names}
    for layer in reversed(range(DEPTH)):
        j = layer // 2
        ln = f"L{layer}"
        sv = saved[layer]
        dr, dy, do, dgate, dg_ln, db_ln, *do_t = _out_bwd(dxn, sv["xh"], sv["rstd"], sv["g_ln"], sv["w_out_t"], sv["o_parts"], sv["h"],
                                                          name=ln + "_outb", o_t=layer % 2 == 1)
        dw_out = _mm(sv["z"].T, dy, out_dtype=F32, name=ln + "_dwout")
        if layer % 2 == 0:
            hb = sv["hb"]
            if layer == 0:
                early = [(n, n, grads[n][1][None]) for n in even_mats]
                early += [(f"{n}_{jj}", n, grads[n][jj][None]) for n in odd_mats for jj in range(2)]
                early_pair = pair_sums(early, "early")
                dq, dk, dv, early_chips = _attn_bwd((sv["q"], 0), (sv["k"], 0), (sv["v"], 0), (sv["o_mla"], 0), (do, 0), sv["lse_mla"], None,
                                                    dk=HEAD_PAD, npairs=mla_pairs, scale=mla_scale, dq_dtype=F32, name=ln + "_mlab",
                                                    scatter=[p[0] for p in early_pair])
            else:
                dq, dk, dv = _attn_bwd((sv["q"], 0), (sv["k"], 0), (sv["v"], 0), (sv["o_mla"], 0), (do, 0), sv["lse_mla"], None,
                                       dk=HEAD_PAD, npairs=mla_pairs, scale=mla_scale, dq_dtype=F32, name=ln + "_mlab")
            dfq, dfk, dfv, dcrow, dccol = _attn_bwd((hb, E_FQ), (hb, E_FK), (hb, E_FV), (sv["o_fox"], 0), (do, MLA_HEADS * MLA_V),
                                                    sv["lse_fox"], sv["crow"], dk=FOX_DIM, npairs=fox_pairs,
                                                    scale=fox_scale, dq_dtype=BF16, name=ln + "_foxb")
            dkeys = dccol.reshape(s, fox_pairs, 2, HALF)[:, :, ::-1, 0].reshape(s, FOX_HEADS)
            dcum = jnp.pad(dcrow[:, 0, :].T + dkeys, ((0, 0), (FL_LANE, LANES - FL_LANE - FOX_HEADS)))
            dlogf = _cumsum(dcum, reverse=True, name=ln + "_cumb")
            dh_small, dq_pre, dqg, dkg, dbf = _even_mid_bwd(
                sv["h"], dq, dk, dv, dlogf, sv["qg"], sv["kg"], sv["bft"], sv["uq_t"], sv["ukv_t"], cos_m, sin_m, name=ln + "_midb")
            duq_p = _mm(sv["qn"].T, dq_pre, out_dtype=F32, name=ln + "_dwuq")
            dkv_cat = jnp.concatenate([dk.astype(BF16), dv], axis=1)
            dukv_p = _mm(sv["kvn"].T, dkv_cat, out_dtype=F32, name=ln + "_dwukv")
            dh = jnp.concatenate([dgate, dfq, dfk, dfv, dh_small], axis=1)
            dw_in_p = _mm(sv["x_b"].T, dh, out_dtype=F32, name=ln + "_dwin")
            dw_in, duq, dukv = _even_weight_grads(dw_in_p, duq_p, dukv_p)
            for n, val in (("even_w_in", dw_in), ("even_w_uq", duq), ("even_w_ukv", dukv), ("even_w_out", dw_out),
                           ("even_q_norm", dqg[0]), ("even_kv_norm", dkg[0]), ("even_b_f", dbf[0, FL_LANE:FL_LANE + FOX_HEADS]),
                           ("even_ln_g", dg_ln[0]), ("even_ln_b", db_ln[0])):
                grads[n][j] = val
        else:
            dq_t, dkp, dvp, dsink = _swa_bwd(sv["q"], sv["kp"], sv["vp"], sv["kp"].T, sink_tiles[j], sv["o_t"], do, do_t[0], sv["lse"],
                                             name=ln + "_swab")
            dq_r, dk_r = _odd_rope_bwd(dq_t, dkp[WINDOW:], cos_s, sin_s, name=ln + "_ropeb")
            dh = jnp.concatenate([dgate, dq_r, dk_r, dvp[WINDOW:].astype(BF16)], axis=1)
            dw_in_p = _mm(sv["x_b"].T, dh, out_dtype=F32, name=ln + "_dwin")
            dw_in, dw_out = _odd_weight_grads(dw_in_p, dw_out)
            for n, val in (("odd_w_in", dw_in), ("odd_w_out", dw_out), ("odd_sinks", _interleave(dsink[None, :, 0, 0], False)[0]),
                           ("odd_ln_g", dg_ln[0]), ("odd_ln_b", db_ln[0])):
                grads[n][j] = val
        dxn = _mm(dh, sv["w_in_t"], out_dtype=F32, name=ln + "_dx", res=dr, res_scale=ALPHA)
    grad_x = dxn[None]
    late = [(n, n, grads[n][0][None]) for n in even_mats]
    late_pair = pair_sums(late, "late")
    late_sum = chip_sums(late, late_pair, _scatter_to_chips([p[0] for p in late_pair], name="grad_scatter_late"), "late")
    early_sum = chip_sums(early, early_pair, early_chips, "early")
    gshard = {n: jnp.concatenate([late_sum[n], early_sum[n]]) for n in even_mats}
    gshard.update({n: jnp.concatenate([early_sum[n + "_0"], early_sum[n + "_1"]]) for n in odd_mats})

    small = [n for n in names if n not in _SHARDED]
    grads = {n: jnp.stack(grads[n]) for n in small}
    sv_flat = jnp.concatenate([grads[n].reshape(-1) for n in small] + [loss_tile[0, :1]])
    sv_real = sv_flat.shape[0]
    sv_rows = -(-sv_real // (PACK_COLS * 8)) * 8
    sv_sum = _all_reduce_small(jnp.pad(sv_flat, (0, sv_rows * PACK_COLS - sv_real)).reshape(sv_rows, PACK_COLS), name="small_all_reduce").reshape(-1)
    off = 0
    for n in small:
        size = int(np.prod(grads[n].shape))
        gfull = sv_sum[off:off + size].reshape(grads[n].shape)
        off += size
        if n in ("odd_ln_g", "odd_ln_b"):
            gfull = lax.dynamic_slice_in_dim(gfull, chip * (D_MODEL // N_CHIPS), D_MODEL // N_CHIPS, axis=1)
        gshard[n] = gfull
    loss = sv_sum[off]

    deltas, new_m, new_v = {}, {}, {}
    for n in names:
        deltas[n], new_m[n], new_v[n] = _adamw(weights[n], gshard[n], mom_m[n], mom_v[n], name="adamw_" + n)
    return (loss, grad_x, *[gshard[n] for n in names], *[deltas[n] for n in names],
            *[new_m[n] for n in names], *[new_v[n] for n in names])
```

```python
import jax
import jax.numpy as jnp
import numpy as np
from jax import lax
from jax.experimental import pallas as pl
from jax.experimental.pallas import tpu as pltpu

F32 = jnp.float32
BF16 = jnp.bfloat16
MESH = pl.DeviceIdType.MESH

D_MODEL = 1024
DEPTH = 4
ROPE_THETA = 10000.0
MLA_HEADS, MLA_NOPE, MLA_ROPE, MLA_V = 8, 64, 32, 64
MLA_Q_RANK, MLA_KV_RANK = 256, 128
FOX_HEADS, FOX_DIM = 8, 64
SWA_HEADS, SWA_KV_HEADS, SWA_DIM, WINDOW = 16, 2, 64, 128
RMS_EPS = 1e-6
LN_EPS = 1e-5
ALPHA = (2 * DEPTH) ** 0.25
EVEN_IN = 2984
ODD_IN = 2304
ADAM_LR, ADAM_B1, ADAM_B2, ADAM_EPS, ADAM_WD, ADAM_STEP = 0.001, 0.9, 0.999, 1e-08, 0.01, 10

LANES = 128
HALF = LANES // 2
HEAD_PAD = 128
N_CHIPS = 4
N_DEV = 8
E_GATE, E_FQ, E_FK, E_FV, E_SMALL = 0, 1024, 1536, 2048, 2560
E_PAD_IN = 3072
KPE_LANE = 64
FL_LANE = 96
O_GATE, O_Q, O_K, O_V = 0, 1024, 2048, 2176

_ARB = "arbitrary"
_PAR = "parallel"


def _cparams(sem):
    return pltpu.CompilerParams(dimension_semantics=sem)


def _pick(n, cands):
    for c in cands:
        if n % c == 0:
            return c
    return n


RESIDENT_BYTES = 8 << 20


def _mm_resident(a, b, *, out_dtype, name, res, res_scale, also_bf16):
    m, k = a.shape
    _, n = b.shape
    tm = _pick(m, (256, 128))
    cn = _pick(n, (512, 384, 256, 128))
    dtypes = [out_dtype, BF16] if also_bf16 else [out_dtype]

    def body(*refs):
        a_ref, b_ref = refs[:2]
        r_ref = refs[2] if res is not None else None
        outs = refs[3 if res is not None else 2:]
        av = a_ref[...].astype(BF16)
        for c0 in range(0, n, cn):
            r = jnp.dot(av, b_ref[:, c0:c0 + cn].astype(BF16), preferred_element_type=F32)
            if res is not None:
                r = r + res_scale * r_ref[:, c0:c0 + cn]
            for o_ref in outs:
                o_ref[:, c0:c0 + cn] = r.astype(o_ref.dtype)

    rows = lambda w: pl.BlockSpec((tm, w), lambda i: (i, 0))
    in_specs = [rows(k), pl.BlockSpec((k, n), lambda i: (0, 0))] + ([rows(n)] if res is not None else [])
    out = pl.pallas_call(
        body, name=name, grid=(m // tm,), in_specs=in_specs, out_specs=[rows(n)] * len(dtypes),
        out_shape=[jax.ShapeDtypeStruct((m, n), d) for d in dtypes], compiler_params=_cparams((_PAR,)),
    )(*([a, b] + ([res] if res is not None else [])))
    return out if also_bf16 else out[0]


def _mm_left_resident(a, b, *, out_dtype, name):
    m, k = a.shape
    _, n = b.shape
    tn = _pick(n, (256, 128))

    def body(a_ref, b_ref, o_ref):
        o_ref[...] = jnp.dot(a_ref[...].astype(BF16), b_ref[...].astype(BF16), preferred_element_type=F32).astype(o_ref.dtype)

    return pl.pallas_call(
        body, name=name, grid=(n // tn,),
        in_specs=[pl.BlockSpec((m, k), lambda j: (0, 0)), pl.BlockSpec((k, tn), lambda j: (0, j))],
        out_specs=pl.BlockSpec((m, tn), lambda j: (0, j)), out_shape=jax.ShapeDtypeStruct((m, n), out_dtype),
        compiler_params=_cparams((_PAR,)),
    )(a, b)


def _mm(a, b, *, out_dtype, name, res=None, res_scale=1.0, also_bf16=False):
    m, k = a.shape
    _, n = b.shape
    if b.size * b.dtype.itemsize <= RESIDENT_BYTES:
        return _mm_resident(a, b, out_dtype=out_dtype, name=name, res=res, res_scale=res_scale, also_bf16=also_bf16)
    if a.size * a.dtype.itemsize <= RESIDENT_BYTES and res is None and not also_bf16:
        return _mm_left_resident(a, b, out_dtype=out_dtype, name=name)
    tm = _pick(m, (512, 256, 128))
    tn = _pick(n, (1024, 768, 512, 384, 256, 128))
    tk = _pick(k, (1024, 768, 512, 256, 128))
    nk = k // tk

    def body(*refs):
        refs = list(refs)
        a_ref, b_ref = refs[:2]
        r_ref = refs[2] if res is not None else None
        acc_ref = refs[-1]
        outs = refs[3 if res is not None else 2:-1]
        kk = pl.program_id(2)

        @pl.when(kk == 0)
        def _():
            acc_ref[...] = jnp.zeros_like(acc_ref)

        acc_ref[...] += jnp.dot(a_ref[...].astype(BF16), b_ref[...].astype(BF16), preferred_element_type=F32)

        @pl.when(kk == nk - 1)
        def _():
            r = acc_ref[...]
            if res is not None:
                r = r + res_scale * r_ref[...]
            for o_ref in outs:
                o_ref[...] = r.astype(o_ref.dtype)

    in_specs = [pl.BlockSpec((tm, tk), lambda i, j, kk: (i, kk)), pl.BlockSpec((tk, tn), lambda i, j, kk: (kk, j))]
    args = [a, b]
    if res is not None:
        in_specs.append(pl.BlockSpec((tm, tn), lambda i, j, kk: (i, j)))
        args.append(res)
    ospec = pl.BlockSpec((tm, tn), lambda i, j, kk: (i, j))
    dtypes = [out_dtype, BF16] if also_bf16 else [out_dtype]
    out = pl.pallas_call(
        body, name=name, grid=(m // tm, n // tn, nk), in_specs=in_specs,
        out_specs=[ospec] * len(dtypes), out_shape=[jax.ShapeDtypeStruct((m, n), d) for d in dtypes],
        scratch_shapes=[pltpu.VMEM((tm, tn), F32)],
        compiler_params=_cparams((_PAR, _PAR, _ARB)),
    )(*args)
    return out if also_bf16 else out[0]


def _rope_tile(t, cos, sin, half):
    w = t.shape[-1]
    lane = lax.broadcasted_iota(jnp.int32, t.shape, 1)
    first = (lane % (2 * half)) < half
    sw = jnp.where(first, pltpu.roll(t, w - half, 1), pltpu.roll(t, half, 1))
    return t * cos + sw * sin


def _sigmoid(x):
    return 1.0 / (1.0 + jnp.exp(-x))


def _lane_mask(shape, lo, hi):
    lane = lax.broadcasted_iota(jnp.int32, shape, 1)
    return (lane >= lo) & (lane < hi)


def _rms(x, g):
    r = lax.rsqrt(jnp.mean(x * x, axis=-1, keepdims=True) + RMS_EPS)
    u = x * r
    return u, r, u * g


def _rms_bwd(dy, u, r, g):
    dyg = dy * g
    dx = r * (dyg - u * jnp.mean(dyg * u, axis=-1, keepdims=True))
    return dx, jnp.sum(dy * u, axis=0, keepdims=True)


def _even_mid_fwd(h, qg, kg, bf_tile, w_uq, w_ukv, cos, sin, *, name):
    s = h.shape[0]
    tb = _pick(s, (512, 256, 128))
    nq = MLA_HEADS * HEAD_PAD

    def body(h_ref, qg_ref, kg_ref, bf_ref, wuq_ref, wukv_ref, cos_ref, sin_ref,
             q_ref, k_ref, v_ref, qn_ref, kvn_ref, lf_ref):
        hb = h_ref[...]
        cq, ckv, misc = hb[:, :MLA_Q_RANK], hb[:, MLA_Q_RANK:MLA_Q_RANK + MLA_KV_RANK], hb[:, MLA_Q_RANK + MLA_KV_RANK:]
        cs, sn = cos_ref[...], sin_ref[...]
        _, _, qn = _rms(cq, qg_ref[...])
        qn = qn.astype(BF16)
        qn_ref[...] = qn
        q = jnp.dot(qn, wuq_ref[...], preferred_element_type=F32)
        _, _, kvn = _rms(ckv, kg_ref[...])
        kvn = kvn.astype(BF16)
        kvn_ref[...] = kvn
        kv = jnp.dot(kvn, wukv_ref[...], preferred_element_type=F32)
        kpe = jnp.where(_lane_mask(misc.shape, KPE_LANE, KPE_LANE + MLA_ROPE), _rope_tile(misc, cs, sn, MLA_ROPE // 2), 0.0)
        for hd in range(MLA_HEADS):
            sl = slice(hd * HEAD_PAD, (hd + 1) * HEAD_PAD)
            q_ref[:, sl] = _rope_tile(q[:, sl], cs, sn, MLA_ROPE // 2).astype(BF16)
            k_ref[:, sl] = (kv[:, sl] + kpe).astype(BF16)
        v_ref[...] = kv[:, nq:].astype(BF16)
        xf = misc + bf_ref[...]
        logf = jnp.minimum(xf, 0.0) - jnp.log(1.0 + jnp.exp(-jnp.abs(xf)))
        lf_ref[...] = jnp.where(_lane_mask(misc.shape, FL_LANE, FL_LANE + FOX_HEADS), logf, 0.0)

    full = lambda a: pl.BlockSpec(a.shape, lambda i: (0,) * a.ndim)
    rows = lambda w, c=0: pl.BlockSpec((tb, w), lambda i: (i, c))
    return pl.pallas_call(
        body, name=name, grid=(s // tb,),
        in_specs=[rows(512, E_SMALL // 512), full(qg), full(kg), full(bf_tile), full(w_uq), full(w_ukv), rows(LANES), rows(LANES)],
        out_specs=[rows(nq), rows(nq), rows(MLA_HEADS * MLA_V), rows(MLA_Q_RANK), rows(MLA_KV_RANK), rows(LANES)],
        out_shape=[jax.ShapeDtypeStruct((s, nq), BF16), jax.ShapeDtypeStruct((s, nq), BF16),
                   jax.ShapeDtypeStruct((s, MLA_HEADS * MLA_V), BF16), jax.ShapeDtypeStruct((s, MLA_Q_RANK), BF16),
                   jax.ShapeDtypeStruct((s, MLA_KV_RANK), BF16), jax.ShapeDtypeStruct((s, LANES), F32)],
        compiler_params=_cparams((_PAR,)),
    )(h, qg, kg, bf_tile, w_uq, w_ukv, cos, sin)


def _even_mid_bwd(h, dq, dk, dv, dlogf, qg, kg, bf_tile, w_uq_t, w_ukv_t, cos, sin, *, name):
    s = h.shape[0]
    tb = _pick(s, (256, 128))
    nq = MLA_HEADS * HEAD_PAD

    def body(h_ref, dq_ref, dk_ref, dv_ref, dlf_ref, qg_ref, kg_ref, bf_ref, wuqt_ref, wukvt_ref, cos_ref, sin_ref,
             dh_ref, dqp_ref, dqg_ref, dkg_ref, dbf_ref):
        @pl.when(pl.program_id(0) == 0)
        def _():
            dqg_ref[...] = jnp.zeros_like(dqg_ref)
            dkg_ref[...] = jnp.zeros_like(dkg_ref)
            dbf_ref[...] = jnp.zeros_like(dbf_ref)

        hb = h_ref[...]
        cq, ckv, misc = hb[:, :MLA_Q_RANK], hb[:, MLA_Q_RANK:MLA_Q_RANK + MLA_KV_RANK], hb[:, MLA_Q_RANK + MLA_KV_RANK:]
        cs, sn = cos_ref[...], -sin_ref[...]
        dkpe = jnp.zeros(misc.shape, F32)
        for hd in range(MLA_HEADS):
            sl = slice(hd * HEAD_PAD, (hd + 1) * HEAD_PAD)
            dqp_ref[:, sl] = _rope_tile(dq_ref[:, sl], cs, sn, MLA_ROPE // 2).astype(BF16)
            dkpe = dkpe + dk_ref[:, sl]
        dqn = jnp.dot(dqp_ref[...], wuqt_ref[...], preferred_element_type=F32)
        uq, rq, _ = _rms(cq, qg_ref[...])
        dcq, dqg = _rms_bwd(dqn, uq, rq, qg_ref[...])
        dqg_ref[...] += dqg
        dkv = jnp.concatenate([dk_ref[...].astype(BF16), dv_ref[...]], axis=1)
        dkvn = jnp.dot(dkv, wukvt_ref[...], preferred_element_type=F32)
        uk, rk, _ = _rms(ckv, kg_ref[...])
        dckv, dkg = _rms_bwd(dkvn, uk, rk, kg_ref[...])
        dkg_ref[...] += dkg
        dmisc = jnp.where(_lane_mask(misc.shape, KPE_LANE, KPE_LANE + MLA_ROPE), _rope_tile(dkpe, cs, sn, MLA_ROPE // 2), 0.0)
        dfl = jnp.where(_lane_mask(misc.shape, FL_LANE, FL_LANE + FOX_HEADS), dlf_ref[...] * _sigmoid(-(misc + bf_ref[...])), 0.0)
        dbf_ref[...] += jnp.sum(dfl, axis=0, keepdims=True)
        dh_ref[:, :MLA_Q_RANK] = dcq.astype(BF16)
        dh_ref[:, MLA_Q_RANK:MLA_Q_RANK + MLA_KV_RANK] = dckv.astype(BF16)
        dh_ref[:, MLA_Q_RANK + MLA_KV_RANK:] = (dmisc + dfl).astype(BF16)

    full = lambda a: pl.BlockSpec(a.shape, lambda i: (0,) * a.ndim)
    rows = lambda w, c=0: pl.BlockSpec((tb, w), lambda i: (i, c))
    return pl.pallas_call(
        body, name=name, grid=(s // tb,),
        in_specs=[rows(512, E_SMALL // 512), rows(nq), rows(nq), rows(MLA_HEADS * MLA_V), rows(LANES), full(qg), full(kg),
                  full(bf_tile), full(w_uq_t), full(w_ukv_t), rows(LANES), rows(LANES)],
        out_specs=[rows(512), rows(nq), full(qg), full(kg), full(bf_tile)],
        out_shape=[jax.ShapeDtypeStruct((s, 512), BF16), jax.ShapeDtypeStruct((s, nq), BF16),
                   jax.ShapeDtypeStruct(qg.shape, F32), jax.ShapeDtypeStruct(kg.shape, F32),
                   jax.ShapeDtypeStruct(bf_tile.shape, F32)],
        compiler_params=_cparams((_ARB,)),
    )(h, dq, dk, dv, dlogf, qg, kg, bf_tile, w_uq_t, w_ukv_t, cos, sin)


def _cumsum(x, *, reverse, name):
    s = x.shape[0]
    tb = _pick(s, (512, 256, 128))
    nb = s // tb

    def body(x_ref, o_ref, carry_ref):
        @pl.when(pl.program_id(0) == 0)
        def _():
            carry_ref[...] = jnp.zeros_like(carry_ref)

        xv = x_ref[...]
        r = lax.broadcasted_iota(jnp.int32, (tb, tb), 0)
        c = lax.broadcasted_iota(jnp.int32, (tb, tb), 1)
        tri = jnp.where((c >= r) if reverse else (c <= r), 1.0, 0.0).astype(BF16)
        hi = xv.astype(BF16)
        r1 = xv - hi.astype(F32)
        mid = r1.astype(BF16)
        lo = (r1 - mid.astype(F32)).astype(BF16)
        cs = (jnp.dot(tri, hi, preferred_element_type=F32) + jnp.dot(tri, mid, preferred_element_type=F32)
              + jnp.dot(tri, lo, preferred_element_type=F32)) + carry_ref[...]
        o_ref[...] = cs
        carry_ref[...] = cs[0:1, :] if reverse else cs[tb - 1:tb, :]

    imap = (lambda i: (nb - 1 - i, 0)) if reverse else (lambda i: (i, 0))
    return pl.pallas_call(
        body, name=name, grid=(nb,), in_specs=[pl.BlockSpec((tb, LANES), imap)],
        out_specs=pl.BlockSpec((tb, LANES), imap), out_shape=jax.ShapeDtypeStruct(x.shape, F32),
        scratch_shapes=[pltpu.VMEM((1, LANES), F32)], compiler_params=_cparams((_ARB,)),
    )(x)


_NT = (((1,), (1,)), ((), ()))
_TN = (((0,), (0,)), ((), ()))


def _head_sel(x, h, dk):
    if dk == LANES:
        return x[:, h * LANES:(h + 1) * LANES]
    return jnp.where(_lane_mask(x.shape, h * HALF, (h + 1) * HALF), x, jnp.zeros_like(x))


def _head_other(x, h, dk):
    return x[:, h * LANES:(h + 1) * LANES] if dk == LANES else x


def _pair(parts, dk=HALF):
    if dk == LANES:
        return jnp.concatenate(parts, axis=1)
    return jnp.where(_lane_mask(parts[0].shape, 0, HALF), parts[0], parts[1])


def _attn_fwd(q, k, v, crow, *, dk, npairs, scale, name, gather=()):
    (qa, qo), (ka, ko), (va, vo) = q, k, v
    s = qa.shape[0]
    wq = 2 * dk
    t = _pick(s, (512, 256, 128))
    nb = s // t
    bias = crow is not None
    ng = len(gather)

    def body(*refs):
        n_in = 3 + bias
        q_ref, k_ref, v_ref = refs[:3]
        cr_ref = refs[3] if bias else None
        o_ref, lse_ref = refs[n_in + ng:n_in + ng + 2]
        qi = pl.program_id(1)
        if ng:
            phases = _gather_phases([a.shape for a in gather], refs[n_in:n_in + ng], refs[n_in + ng + 2:n_in + 2 * ng + 2],
                                    *refs[n_in + 2 * ng + 2:])
            pair = pl.program_id(0)
            pl.when((pair == 0) & (qi == 0))(phases[0])
            pl.when((pair == npairs - 1) & (qi == 0))(phases[1])
        qb = q_ref[...]
        qs = [_head_sel(qb, h, dk) for h in range(2)]

        def scores(j):
            rows = pl.ds(pl.multiple_of(j * t, t), t)
            kb = k_ref[rows, :]
            out = []
            for h in range(2):
                sc = lax.dot_general(qs[h], _head_other(kb, h, dk), _NT, preferred_element_type=F32) * scale
                out.append(sc - cr_ref[h, j] if bias else sc)
            return tuple(out)

        def update(scs, j, state):
            rows = pl.ds(pl.multiple_of(j * t, t), t)
            vb = v_ref[rows, :]
            soft = []
            for h in range(2):
                m, l, _ = state[3 * h:3 * h + 3]
                sc = scs[h]
                m_new = jnp.maximum(m, jnp.max(sc, axis=1, keepdims=True))
                a = jnp.exp(m - m_new)
                p = jnp.exp(sc - m_new)
                soft.append((m_new, a * l + jnp.sum(p, axis=1, keepdims=True), a, p.astype(BF16)))
            new = []
            for h in range(2):
                m_new, l, a, p = soft[h]
                new += [m_new, l, a * state[3 * h + 2] + jnp.dot(p, vb, preferred_element_type=F32)]
            return tuple(new)

        one = (jnp.full((t, 1), -jnp.inf, F32), jnp.zeros((t, 1), F32), jnp.zeros((t, LANES), F32))
        state = lax.fori_loop(0, qi, lambda j, st: update(scores(j), j, st), one + one)
        row = lax.broadcasted_iota(jnp.int32, (t, t), 0)
        col = lax.broadcasted_iota(jnp.int32, (t, t), 1)
        diag = tuple(jnp.where(col <= row, sc, -jnp.inf) for sc in scores(qi))
        state = update(diag, qi, state)
        outs = []
        for h in range(2):
            m, l, acc = state[3 * h:3 * h + 3]
            outs.append(acc / l)
            lse_ref[h] = m + jnp.log(l)
        o_ref[...] = _pair(outs)
        if ng:
            pl.when((pair == npairs - 1) & (qi == nb - 1))(phases[2])

    in_specs = [pl.BlockSpec((t, wq), lambda p, i: (i, qo // wq + p)), pl.BlockSpec((s, wq), lambda p, i: (0, ko // wq + p)),
                pl.BlockSpec((s, LANES), lambda p, i: (0, vo // LANES + p))]
    args = [qa, ka, va]
    if bias:
        in_specs.append(pl.BlockSpec((2, nb, 1, t), lambda p, i: (p, 0, 0, 0)))
        args.append(crow.reshape(2 * npairs, nb, 1, t))
    out = pl.pallas_call(
        body, name=name, grid=(npairs, nb), in_specs=in_specs + [_ANY] * ng,
        out_specs=[pl.BlockSpec((t, LANES), lambda p, i: (i, p)), pl.BlockSpec((2, t, 1), lambda p, i: (p, i, 0))] + [_ANY] * ng,
        out_shape=[jax.ShapeDtypeStruct((s, npairs * LANES), F32), jax.ShapeDtypeStruct((2 * npairs, s, 1), F32)]
        + [jax.ShapeDtypeStruct((N_CHIPS,) + a.shape, a.dtype) for a in gather],
        scratch_shapes=_sem_pairs(GATHER_COPIES * ng) if ng else [],
        compiler_params=_cparams((_ARB, _ARB) if ng else (_PAR, _PAR)),
    )(*args, *gather)
    return (out[0], out[1], out[2:]) if ng else (out[0], out[1])


def _attn_bwd(q, k, v, o, do, lse, crow, *, dk, npairs, scale, dq_dtype, name, scatter=()):
    (qa, qo), (ka, ko), (va, vo), (oa, oo), (da, do_o) = q, k, v, o, do
    s = qa.shape[0]
    wq = 2 * dk
    t = _pick(s, (512, 256, 128))
    nb = s // t
    bias = crow is not None
    ns = len(scatter)

    def body(*refs):
        n_in, n_out = 6 + bias, 3 + 2 * bias
        q_ref, k_ref, v_ref, o_ref, do_ref, lse_ref = refs[:6]
        outs = refs[n_in + ns:n_in + ns + n_out]
        scr = refs[n_in + 2 * ns + n_out:]
        dq_ref, dk_ref, dv_ref = outs[:3]
        dq_s, dk_s, dv_s = scr[:3]
        if bias:
            cr_ref, dc_ref, dcc_ref, dc_s, rs_s = refs[6], outs[3], outs[4], scr[3], scr[4]
        ki, qi = pl.program_id(1), pl.program_id(2)
        if ns:
            send_off, finish = _scatter_phases(ns, refs[n_in:n_in + ns], refs[n_in + ns + n_out:n_in + 2 * ns + n_out], *scr[3 + 2 * bias:])
            pair = pl.program_id(0)
            pl.when((pair == 0) & (ki == 0) & (qi == 0))(send_off)

        @pl.when((ki == 0) & (qi == 0))
        def _():
            dq_s[...] = jnp.zeros_like(dq_s)
            if bias:
                rs_s[...] = jnp.zeros_like(rs_s)

        @pl.when(qi == ki)
        def _():
            dk_s[...] = jnp.zeros_like(dk_s)
            dv_s[...] = jnp.zeros_like(dv_s)
            if bias:
                dc_s[...] = jnp.zeros_like(dc_s)

        def block(on_diagonal):
            qb, kb, vb, dob, ob = q_ref[...], k_ref[...], v_ref[...], do_ref[...], o_ref[...]
            rows = pl.ds(pl.multiple_of(qi * t, t), t)
            row = lax.broadcasted_iota(jnp.int32, (t, t), 0) + qi * t
            col = lax.broadcasted_iota(jnp.int32, (t, t), 1) + ki * t
            fed = []
            for h in range(2):
                kh, doh = _head_other(kb, h, dk), _head_sel(dob, h, HALF)
                sc = lax.dot_general(_head_sel(qb, h, dk), kh, _NT, preferred_element_type=F32)
                dp = lax.dot_general(doh, vb, _NT, preferred_element_type=F32)
                fed.append((kh, doh, sc, dp))
            mid = []
            for h in range(2):
                kh, doh, sc, dp = fed[h]
                sc = sc * scale
                if bias:
                    sc = sc - cr_ref[h]
                if on_diagonal:
                    sc = jnp.where(col <= row, sc, -jnp.inf)
                p = jnp.exp(sc - lse_ref[h])
                delta = jnp.sum(doh.astype(F32) * ob, axis=1, keepdims=True)
                ds = p * (dp - delta)
                mid.append((kh, p.astype(BF16), (ds * scale).astype(BF16)))
            dq_parts, dk_parts, dv_parts = [], [], []
            for h in range(2):
                kh, pb, dsb = mid[h]
                dv_parts.append(lax.dot_general(pb, dob, _TN, preferred_element_type=F32))
                dk_parts.append(lax.dot_general(dsb, _head_other(qb, h, dk), _TN, preferred_element_type=F32))
                if bias:
                    dc_s[h] -= jnp.sum(dsb.astype(F32), axis=0, keepdims=True) * (1.0 / scale)
                    kh = jnp.where(_lane_mask(kh.shape, h * HALF, (h + 1) * HALF), kh, jnp.ones_like(kh))
                dq_parts.append(jnp.dot(dsb, kh, preferred_element_type=F32))
            dv_s[...] += _pair(dv_parts)
            dk_s[...] += _pair(dk_parts, dk)
            dq_s[rows, :] += _pair(dq_parts, dk)
            if bias:
                rs_s[rows, :] += _pair(dq_parts[::-1])

        pl.when(qi > ki)(lambda: block(False))
        pl.when(qi == ki)(lambda: block(True))

        @pl.when(qi == nb - 1)
        def _():
            dk_ref[...] = dk_s[...].astype(dk_ref.dtype)
            dv_ref[...] = dv_s[...].astype(dv_ref.dtype)
            if bias:
                dc_ref[...] = dc_s[...]

        @pl.when((ki == nb - 1) & (qi == nb - 1))
        def _():
            dq_ref[...] = dq_s[...].astype(dq_ref.dtype)
            if bias:
                dcc_ref[...] = rs_s[...] * (1.0 / scale)

        if ns:
            pl.when((pair == npairs - 1) & (ki == nb - 1) & (qi == nb - 1))(finish)

    qrow = lambda p, j, i: jnp.maximum(i, j)
    in_specs = [pl.BlockSpec((t, wq), lambda p, j, i: (qrow(p, j, i), qo // wq + p)),
                pl.BlockSpec((t, wq), lambda p, j, i: (j, ko // wq + p)),
                pl.BlockSpec((t, LANES), lambda p, j, i: (j, vo // LANES + p)),
                pl.BlockSpec((t, LANES), lambda p, j, i: (qrow(p, j, i), oo // LANES + p)),
                pl.BlockSpec((t, LANES), lambda p, j, i: (qrow(p, j, i), do_o // LANES + p)),
                pl.BlockSpec((2, t, 1), lambda p, j, i: (p, qrow(p, j, i), 0))]
    args = [qa, ka, va, oa, da, lse]
    out_specs = [pl.BlockSpec((s, wq), lambda p, j, i: (0, p)), pl.BlockSpec((t, wq), lambda p, j, i: (j, p)),
                 pl.BlockSpec((t, LANES), lambda p, j, i: (j, p))]
    out_shape = [jax.ShapeDtypeStruct((s, npairs * wq), dq_dtype), jax.ShapeDtypeStruct((s, npairs * wq), dq_dtype),
                 jax.ShapeDtypeStruct((s, npairs * LANES), BF16)]
    scratch = [pltpu.VMEM((s, wq), F32), pltpu.VMEM((t, wq), F32), pltpu.VMEM((t, LANES), F32)]
    if bias:
        in_specs.append(pl.BlockSpec((2, 1, t), lambda p, j, i: (p, 0, j)))
        args.append(crow)
        assert dk == HALF, "the sums over the keys ride in the unused half of the dq product of a 64-wide head pair"
        out_specs += [pl.BlockSpec((2, 1, t), lambda p, j, i: (p, 0, j)), pl.BlockSpec((s, LANES), lambda p, j, i: (0, p))]
        out_shape += [jax.ShapeDtypeStruct((2 * npairs, 1, s), F32), jax.ShapeDtypeStruct((s, npairs * LANES), F32)]
        scratch += [pltpu.VMEM((2, 1, t), F32), pltpu.VMEM((s, LANES), F32)]
    n_out = len(out_shape)
    out = pl.pallas_call(
        body, name=name, grid=(npairs, nb, nb), in_specs=in_specs + [_ANY] * ns, out_specs=out_specs + [_ANY] * ns,
        out_shape=out_shape + [jax.ShapeDtypeStruct((3,) + a.shape[1:], a.dtype) for a in scatter],
        scratch_shapes=scratch + (_sem_pairs(3 * ns) if ns else []),
        compiler_params=_cparams((_ARB if ns else _PAR, _ARB, _ARB)),
    )(*args, *scatter)
    return (*out[:n_out], out[n_out:]) if ns else out


def _sub_mask(shape, lo, hi):
    sub = lax.broadcasted_iota(jnp.int32, shape, 0)
    return (sub >= lo) & (sub < hi)


def _pair_rows(parts):
    return jnp.where(_sub_mask(parts[0].shape, 0, HALF), parts[0], parts[1])


def _swa_valid(start, t):
    krow = lax.broadcasted_iota(jnp.int32, (t + WINDOW, t), 0)
    qcol = lax.broadcasted_iota(jnp.int32, (t + WINDOW, t), 1)
    diff = qcol - krow + WINDOW
    return (diff >= 0) & (diff < WINDOW) & (krow + start >= WINDOW)


def _swa_tiles(s, most):
    t = _pick(s, (256, 128))
    return t, _pick(s // t, tuple(n for n in (8, 4, 2, 1) if n <= most))


def _swa_fwd(q, kp, vp, sink, *, name):
    s = q.shape[0]
    npairs = SWA_HEADS // 2
    t, nt = _swa_tiles(s, 8)
    scale = SWA_DIM ** -0.5

    def body(q_ref, k_ref, v_ref, sk_ref, o_ref, lse_ref):
        sts, vws = [], []
        for u in range(nt):
            start = pl.multiple_of((pl.program_id(1) * nt + u) * t, t)
            kw = k_ref[pl.ds(start, t + WINDOW), :]
            vws.append(v_ref[pl.ds(start, t + WINDOW), :])
            qb = q_ref[u * t:(u + 1) * t, :]
            valid = _swa_valid(start, t)
            for h in range(2):
                st = lax.dot_general(kw, _head_sel(qb, h, HALF), _NT, preferred_element_type=F32) * scale
                sts.append(jnp.where(valid, st, -jnp.inf))
        pts = []
        for u in range(nt):
            for h in range(2):
                st = sts[2 * u + h]
                snk = sk_ref[h][:, 0:1]
                m = jnp.maximum(jnp.max(st, axis=0, keepdims=True), snk)
                e = jnp.exp(st - m)
                l = jnp.sum(e, axis=0, keepdims=True) + jnp.exp(snk - m)
                pts.append((e * (1.0 / l)).astype(BF16))
                lse_ref[h, :, u * t:(u + 1) * t] = m + jnp.log(l)
        for u in range(nt):
            outs = [lax.dot_general(vws[u], pts[2 * u + h], _TN, preferred_element_type=F32) for h in range(2)]
            o_ref[:, u * t:(u + 1) * t] = _pair_rows(outs)

    kvspec = pl.BlockSpec((s + WINDOW, LANES), lambda p, i: (0, 0))
    return pl.pallas_call(
        body, name=name, grid=(npairs, s // (t * nt)),
        in_specs=[pl.BlockSpec((t * nt, LANES), lambda p, i: (i, p)), kvspec, kvspec, pl.BlockSpec((2, 1, LANES), lambda p, i: (p, 0, 0))],
        out_specs=[pl.BlockSpec((LANES, t * nt), lambda p, i: (p, i)), pl.BlockSpec((2, 1, t * nt), lambda p, i: (p, 0, i))],
        out_shape=[jax.ShapeDtypeStruct((npairs * LANES, s), F32), jax.ShapeDtypeStruct((SWA_HEADS, 1, s), F32)],
        compiler_params=_cparams((_PAR, _PAR)),
    )(q, kp, vp, sink)


def _swa_bwd(q, kp, vp, kpt, sink, ot, do, dot, lse, *, name):
    s = q.shape[0]
    npairs = SWA_HEADS // 2
    t, nt = _swa_tiles(s, 4)
    scale = SWA_DIM ** -0.5

    def body(q_ref, k_ref, v_ref, kt_ref, sk_ref, ot_ref, do_ref, dot_ref, lse_ref, dq_ref, dk_ref, dv_ref, dsk_ref):
        pp, i = pl.program_id(0), pl.program_id(1)

        @pl.when((pp == 0) & (i == 0))
        def _():
            dk_ref[...] = jnp.zeros_like(dk_ref)
            dv_ref[...] = jnp.zeros_like(dv_ref)

        @pl.when(i == 0)
        def _():
            dsk_ref[...] = jnp.zeros_like(dsk_ref)

        fed = []
        for u in range(nt):
            cols = slice(u * t, (u + 1) * t)
            start = pl.multiple_of((i * nt + u) * t, t)
            win = pl.ds(start, t + WINDOW)
            qb, dob = q_ref[cols, :], do_ref[cols, :]
            kw, vw = k_ref[win, :], v_ref[win, :]
            valid = _swa_valid(start, t)
            for h in range(2):
                st = lax.dot_general(kw, _head_sel(qb, h, HALF), _NT, preferred_element_type=F32)
                dpt = lax.dot_general(vw, _head_sel(dob, h, HALF), _NT, preferred_element_type=F32)
                fed.append((cols, win, qb, dob, valid, st, dpt))
        mid = []
        for u in range(nt):
            cols = fed[2 * u][0]
            prod = dot_ref[:, cols].astype(F32) * ot_ref[:, cols]
            for h in range(2):
                valid, st, dpt = fed[2 * u + h][4:]
                lse_b = lse_ref[h, :, cols]
                pt = jnp.exp(jnp.where(valid, st * scale, -jnp.inf) - lse_b)
                delta = jnp.sum(jnp.where(_sub_mask(prod.shape, h * HALF, (h + 1) * HALF), prod, 0.0), axis=0, keepdims=True)
                dst = pt * (dpt - delta)
                psink = jnp.exp(sk_ref[h][:, 0:1] - lse_b)
                dsk_ref[h] += jnp.broadcast_to(-jnp.sum(psink * delta, axis=1, keepdims=True), (1, LANES))
                mid.append((pt.astype(BF16), (dst * scale).astype(BF16)))
        for u in range(nt):
            cols, win, qb, dob = fed[2 * u][:4]
            ktw = kt_ref[:, win]
            dq_parts = [jnp.dot(ktw, mid[2 * u + h][1], preferred_element_type=F32) for h in range(2)]
            dk_parts = [jnp.dot(mid[2 * u + h][1], qb, preferred_element_type=F32) for h in range(2)]
            dv_parts = [jnp.dot(mid[2 * u + h][0], dob, preferred_element_type=F32) for h in range(2)]
            dq_ref[:, cols] = _pair_rows(dq_parts)
            dk_ref[win, :] += _pair(dk_parts)
            dv_ref[win, :] += _pair(dv_parts)

    tile = pl.BlockSpec((t * nt, LANES), lambda p, i: (i, p))
    ttile = pl.BlockSpec((LANES, t * nt), lambda p, i: (p, i))
    kvspec = pl.BlockSpec((s + WINDOW, LANES), lambda p, i: (0, 0))
    ktspec = pl.BlockSpec((LANES, s + WINDOW), lambda p, i: (0, 0))
    skspec = pl.BlockSpec((2, 1, LANES), lambda p, i: (p, 0, 0))
    return pl.pallas_call(
        body, name=name, grid=(npairs, s // (t * nt)),
        in_specs=[tile, kvspec, kvspec, ktspec, skspec, ttile, tile, ttile, pl.BlockSpec((2, 1, t * nt), lambda p, i: (p, 0, i))],
        out_specs=[ttile, kvspec, kvspec, skspec],
        out_shape=[jax.ShapeDtypeStruct((npairs * LANES, s), F32), jax.ShapeDtypeStruct((s + WINDOW, LANES), F32),
                   jax.ShapeDtypeStruct((s + WINDOW, LANES), F32), jax.ShapeDtypeStruct((SWA_HEADS, 1, LANES), F32)],
        compiler_params=_cparams((_ARB, _ARB)),
    )(q, kp, vp, kpt, sink, ot, do, dot, lse)


def _odd_rope(h, cos, sin, *, name):
    s = h.shape[0]
    tb = _pick(s, (512, 256, 128))
    nq = SWA_HEADS * SWA_DIM

    def body(q_ref, kv_ref, cos_ref, sin_ref, qo_ref, ko_ref, vo_ref):
        cs, sn = cos_ref[...], sin_ref[...]
        for j in range(nq // LANES):
            sl = slice(j * LANES, (j + 1) * LANES)
            qo_ref[:, sl] = _rope_tile(q_ref[:, sl], cs, sn, SWA_DIM // 2).astype(BF16)
        ko_ref[...] = _rope_tile(kv_ref[:, :LANES], cs, sn, SWA_DIM // 2).astype(BF16)
        vo_ref[...] = kv_ref[:, LANES:].astype(BF16)

    rows = lambda w, c=0: pl.BlockSpec((tb, w), lambda i: (i, c))
    return pl.pallas_call(
        body, name=name, grid=(s // tb,),
        in_specs=[rows(nq, O_Q // nq), rows(2 * LANES, O_K // (2 * LANES)), rows(LANES), rows(LANES)],
        out_specs=[rows(nq), rows(LANES), rows(LANES)],
        out_shape=[jax.ShapeDtypeStruct((s, nq), BF16), jax.ShapeDtypeStruct((s, LANES), BF16), jax.ShapeDtypeStruct((s, LANES), BF16)],
        compiler_params=_cparams((_PAR,)),
    )(h, h, cos, sin)


def _odd_rope_bwd(dq_t, dk, cos, sin, *, name):
    s = dk.shape[0]
    tb = _pick(s, (256, 128))
    nq = SWA_HEADS * SWA_DIM

    def body(dq_ref, dk_ref, cos_ref, sin_ref, qo_ref, ko_ref):
        cs, sn = cos_ref[...], -sin_ref[...]
        dq = dq_ref[...].T
        for j in range(nq // LANES):
            sl = slice(j * LANES, (j + 1) * LANES)
            qo_ref[:, sl] = _rope_tile(dq[:, sl], cs, sn, SWA_DIM // 2).astype(BF16)
        ko_ref[...] = _rope_tile(dk_ref[...], cs, sn, SWA_DIM // 2).astype(BF16)

    rows = lambda w: pl.BlockSpec((tb, w), lambda i: (i, 0))
    return pl.pallas_call(
        body, name=name, grid=(s // tb,),
        in_specs=[pl.BlockSpec((nq, tb), lambda i: (0, i)), rows(LANES), rows(LANES), rows(LANES)],
        out_specs=[rows(nq), rows(LANES)],
        out_shape=[jax.ShapeDtypeStruct((s, nq), BF16), jax.ShapeDtypeStruct((s, LANES), BF16)],
        compiler_params=_cparams((_PAR,)),
    )(dq_t, dk, cos, sin)


def _mixer_out(o_refs, o_t):
    if o_t:
        return o_refs[0][...].T
    return o_refs[0][...] if len(o_refs) == 1 else jnp.concatenate([r[...] for r in o_refs], axis=1)


def _mixer_specs(o_parts, o_t, tb):
    if o_t:
        return [pl.BlockSpec((D_MODEL, tb), lambda i: (0, i))]
    return [pl.BlockSpec((tb, a.shape[1]), lambda i: (i, 0)) for a in o_parts]


def _out_fwd(o_parts, h, x, w_out, g, b, *, name, o_t=False):
    s = x.shape[0]
    tb = _pick(s, (512, 256, 128))
    n_o = len(o_parts)

    def body(*refs):
        o_refs = refs[:n_o]
        gate_ref, x_ref, w_ref, g_ref, b_ref, xn_ref, xb_ref, z_ref, xh_ref, rs_ref = refs[n_o:]
        gate = gate_ref[...]
        o = _mixer_out(o_refs, o_t)
        z = (o * (gate * _sigmoid(gate))).astype(BF16)
        z_ref[...] = z
        r = ALPHA * x_ref[...] + jnp.dot(z, w_ref[...], preferred_element_type=F32)
        mu = jnp.mean(r, axis=-1, keepdims=True)
        rc = r - mu
        rstd = lax.rsqrt(jnp.mean(rc * rc, axis=-1, keepdims=True) + LN_EPS)
        xh = rc * rstd
        xn = xh * g_ref[...] + b_ref[...]
        xh_ref[...] = xh
        rs_ref[...] = rstd
        xn_ref[...] = xn
        xb_ref[...] = xn.astype(BF16)

    rows = lambda w: pl.BlockSpec((tb, w), lambda i: (i, 0))
    full = lambda a: pl.BlockSpec(a.shape, lambda i: (0,) * a.ndim)
    return pl.pallas_call(
        body, name=name, grid=(s // tb,),
        in_specs=_mixer_specs(o_parts, o_t, tb) + [rows(D_MODEL), rows(D_MODEL), full(w_out), full(g), full(b)],
        out_specs=[rows(D_MODEL), rows(D_MODEL), rows(D_MODEL), rows(D_MODEL), rows(1)],
        out_shape=[jax.ShapeDtypeStruct((s, D_MODEL), F32), jax.ShapeDtypeStruct((s, D_MODEL), BF16),
                   jax.ShapeDtypeStruct((s, D_MODEL), BF16), jax.ShapeDtypeStruct((s, D_MODEL), F32),
                   jax.ShapeDtypeStruct((s, 1), F32)],
        compiler_params=_cparams((_PAR,)),
    )(*o_parts, h, x, w_out, g, b)


def _out_bwd(dxn, xh, rstd, g, w_out_t, o_parts, h, *, name, o_t=False):
    s = dxn.shape[0]
    tb = _pick(s, (512, 256, 128))
    n_o = len(o_parts)

    def body(*refs):
        dxn_ref, xh_ref, rs_ref, g_ref, wt_ref = refs[:5]
        o_refs = refs[5:5 + n_o]
        gate_ref, dr_ref, dy_ref, do_ref, dgate_ref, dg_ref, db_ref = refs[5 + n_o:12 + n_o]

        @pl.when(pl.program_id(0) == 0)
        def _():
            dg_ref[...] = jnp.zeros_like(dg_ref)
            db_ref[...] = jnp.zeros_like(db_ref)

        dxn_b, xh_b = dxn_ref[...], xh_ref[...]
        dg_ref[...] += jnp.sum(dxn_b * xh_b, axis=0, keepdims=True)
        db_ref[...] += jnp.sum(dxn_b, axis=0, keepdims=True)
        dxh = dxn_b * g_ref[...]
        dr = rs_ref[...] * (dxh - jnp.mean(dxh, axis=-1, keepdims=True) - xh_b * jnp.mean(dxh * xh_b, axis=-1, keepdims=True))
        dr_ref[...] = dr
        dy = dr.astype(BF16)
        dy_ref[...] = dy
        dz = jnp.dot(dy, wt_ref[...], preferred_element_type=F32)
        gate = gate_ref[...]
        sg = _sigmoid(gate)
        o = _mixer_out(o_refs, o_t)
        do = dz * (gate * sg)
        do_ref[...] = do.astype(BF16)
        if o_t:
            refs[12 + n_o][...] = do.T.astype(BF16)
        dgate_ref[...] = (dz * o * (sg * (1.0 + gate * (1.0 - sg)))).astype(BF16)

    rows = lambda w: pl.BlockSpec((tb, w), lambda i: (i, 0))
    full = lambda a: pl.BlockSpec(a.shape, lambda i: (0,) * a.ndim)
    cols = pl.BlockSpec((D_MODEL, tb), lambda i: (0, i))
    return pl.pallas_call(
        body, name=name, grid=(s // tb,),
        in_specs=[rows(D_MODEL), rows(D_MODEL), rows(1), full(g), full(w_out_t)] + _mixer_specs(o_parts, o_t, tb) + [rows(D_MODEL)],
        out_specs=[rows(D_MODEL), rows(D_MODEL), rows(D_MODEL), rows(D_MODEL), full(g), full(g)] + [cols] * o_t,
        out_shape=[jax.ShapeDtypeStruct((s, D_MODEL), F32), jax.ShapeDtypeStruct((s, D_MODEL), BF16),
                   jax.ShapeDtypeStruct((s, D_MODEL), BF16), jax.ShapeDtypeStruct((s, D_MODEL), BF16),
                   jax.ShapeDtypeStruct(g.shape, F32), jax.ShapeDtypeStruct(g.shape, F32)]
        + [jax.ShapeDtypeStruct((D_MODEL, s), BF16)] * o_t,
        compiler_params=_cparams((_ARB,)),
    )(dxn, xh, rstd, g, w_out_t, *o_parts, h)


def _loss_grad(y, target, *, name):
    s, d = y.shape
    tb = _pick(s, (512, 256, 128))

    def body(y_ref, t_ref, dy_ref, l_ref):
        @pl.when(pl.program_id(0) == 0)
        def _():
            l_ref[...] = jnp.zeros_like(l_ref)

        err = y_ref[...] - t_ref[...]
        dy_ref[...] = err * (1.0 / d)
        per_tok = jnp.mean(err * err, axis=-1, keepdims=True)
        l_ref[...] += 0.5 * jnp.sum(per_tok, axis=0, keepdims=True)

    rows = pl.BlockSpec((tb, d), lambda i: (i, 0))
    return pl.pallas_call(
        body, name=name, grid=(s // tb,), in_specs=[rows, rows],
        out_specs=[rows, pl.BlockSpec((8, LANES), lambda i: (0, 0))],
        out_shape=[jax.ShapeDtypeStruct((s, d), F32), jax.ShapeDtypeStruct((8, LANES), F32)],
        compiler_params=_cparams((_ARB,)),
    )(y, target)


def _adamw(w, g, m, v, *, name):
    shape = w.shape
    w3, g3, m3, v3 = (a.reshape((1,) * (3 - a.ndim) + a.shape) for a in (w, g, m, v))
    a0, a1, a2 = w3.shape
    tb = _pick(a1, (256, 128)) if a1 % 8 == 0 else a1
    c1 = 1.0 - ADAM_B1 ** ADAM_STEP
    c2 = 1.0 - ADAM_B2 ** ADAM_STEP

    def body(w_ref, g_ref, m_ref, v_ref, d_ref, mo_ref, vo_ref):
        gg = g_ref[...]
        mn = ADAM_B1 * m_ref[...] + (1.0 - ADAM_B1) * gg
        vn = ADAM_B2 * v_ref[...] + (1.0 - ADAM_B2) * (gg * gg)
        mo_ref[...] = mn
        vo_ref[...] = vn
        d_ref[...] = -ADAM_LR * ((mn / c1) / (jnp.sqrt(vn / c2) + ADAM_EPS) + ADAM_WD * w_ref[...])

    spec = pl.BlockSpec((1, tb, a2), lambda i, j: (i, j, 0))
    outs = pl.pallas_call(
        body, name=name, grid=(a0, a1 // tb), in_specs=[spec] * 4, out_specs=[spec] * 3,
        out_shape=[jax.ShapeDtypeStruct(w3.shape, F32)] * 3, compiler_params=_cparams((_PAR, _PAR)),
    )(w3, g3, m3, v3)
    return tuple(a.reshape(shape) for a in outs)


def _place():
    x, y, c = lax.axis_index("x"), lax.axis_index("y"), lax.axis_index("c")
    return x, y, c, [(1 - x, y), (x, 1 - y), (1 - x, 1 - y)]


_ANY = pl.BlockSpec(memory_space=pl.ANY)


def _sem_pairs(n):
    return [pltpu.SemaphoreType.DMA((n,)), pltpu.SemaphoreType.DMA((n,))]


GATHER_COPIES = 7


def _gather_phases(shapes, srcs, outs, send_sems, recv_sems):
    nt = len(shapes)
    x, y, c, chips = _place()
    me = 2 * x + y
    sib = (x, y, 1 - c)

    def half(t, chip, hc):
        rh = shapes[t][1] // 2
        return outs[t].at[chip, :, pl.ds(hc * rh, rh), :]

    def copy(t, kk, s_ref, d_ref, to):
        return pltpu.make_async_remote_copy(src_ref=s_ref, dst_ref=d_ref, send_sem=send_sems.at[GATHER_COPIES * t + kk],
                                            recv_sem=recv_sems.at[GATHER_COPIES * t + kk], device_id=to, device_id_type=MESH)

    def first_hop(t, j):
        rh = shapes[t][1] // 2
        cx, cy = chips[j]
        return copy(t, j, srcs[t].at[:, pl.ds(c * rh, rh), :], half(t, me, c), (cx, cy, c))

    def passed(t, j):
        cx, cy = chips[j]
        landed = half(t, 2 * cx + cy, c)
        return copy(t, 3 + j, landed, landed, sib)

    def own(t):
        return copy(t, 6, srcs[t], outs[t].at[me], sib)

    pairs = [(t, j) for j in range(3) for t in range(nt)]

    def start():
        for t, j in pairs:
            first_hop(t, j).start()
        for t in range(nt):
            own(t).start()

    def pass_on():
        for t, j in pairs:
            cx, cy = chips[j]
            landed = half(t, 2 * cx + cy, c)
            copy(t, j, landed, landed, (cx, cy, c)).wait_recv()
            passed(t, j).start()

    def finish():
        for t, j in pairs:
            cx, cy = chips[j]
            theirs = half(t, 2 * cx + cy, 1 - c)
            copy(t, 3 + j, theirs, theirs, sib).wait_recv()
        for t in range(nt):
            own(t).wait()
        for t, j in pairs:
            first_hop(t, j).wait_send()
            passed(t, j).wait_send()

    return start, pass_on, finish


def _gather_chip_shards(tensors, *, name):
    nt = len(tensors)

    def body(*refs):
        for phase in _gather_phases([a.shape for a in tensors], refs[:nt], refs[nt:2 * nt], *refs[2 * nt:]):
            phase()

    return pl.pallas_call(
        body, name=name, in_specs=[_ANY] * nt, out_specs=[_ANY] * nt,
        out_shape=[jax.ShapeDtypeStruct((N_CHIPS,) + a.shape, a.dtype) for a in tensors],
        scratch_shapes=_sem_pairs(GATHER_COPIES * nt),
    )(*tensors)


def _swap_halves(gs, *, name):
    nt = len(gs)

    def body(*refs):
        srcs, outs = refs[:nt], refs[nt:2 * nt]
        send_sems, recv_sems = refs[2 * nt:]
        x, y, c, _ = _place()
        cps = [pltpu.make_async_remote_copy(src_ref=srcs[t].at[:, 1 - c], dst_ref=outs[t], send_sem=send_sems.at[t],
                                            recv_sem=recv_sems.at[t], device_id=(x, y, 1 - c), device_id_type=MESH)
               for t in range(nt)]
        for cp in cps:
            cp.start()
        for cp in cps:
            cp.wait()

    return pl.pallas_call(
        body, name=name, in_specs=[_ANY] * nt, out_specs=[_ANY] * nt,
        out_shape=[jax.ShapeDtypeStruct((N_CHIPS,) + g.shape[2:], g.dtype) for g in gs], scratch_shapes=_sem_pairs(nt),
    )(*gs)


def _scatter_phases(nt, srcs, outs, send_sems, recv_sems):
    x, y, c, chips = _place()

    def copies():
        return [pltpu.make_async_remote_copy(src_ref=srcs[t].at[2 * cx + cy], dst_ref=outs[t].at[j],
                                             send_sem=send_sems.at[3 * t + j], recv_sem=recv_sems.at[3 * t + j],
                                             device_id=(cx, cy, c), device_id_type=MESH)
                for j, (cx, cy) in enumerate(chips) for t in range(nt)]

    def start():
        for cp in copies():
            cp.start()

    def finish():
        for cp in copies():
            cp.wait()

    return start, finish


def _scatter_to_chips(blocks, *, name):
    nt = len(blocks)

    def body(*refs):
        for phase in _scatter_phases(nt, refs[:nt], refs[nt:2 * nt], *refs[2 * nt:]):
            phase()

    return pl.pallas_call(
        body, name=name, in_specs=[_ANY] * nt, out_specs=[_ANY] * nt,
        out_shape=[jax.ShapeDtypeStruct((3,) + a.shape[1:], a.dtype) for a in blocks], scratch_shapes=_sem_pairs(3 * nt),
    )(*blocks)


def _join_halves(bufs, *, name):
    nt = len(bufs)

    def body(*refs):
        srcs, outs = refs[:nt], refs[nt:2 * nt]
        send_sems, recv_sems = refs[2 * nt:]
        x, y, c, _ = _place()
        cps = []
        for t in range(nt):
            rh = bufs[t].shape[1] // 2
            rows = pl.ds(c * rh, rh)
            cps.append(pltpu.make_async_remote_copy(src_ref=srcs[t].at[:, rows, :], dst_ref=outs[t].at[:, rows, :],
                                                    send_sem=send_sems.at[t], recv_sem=recv_sems.at[t],
                                                    device_id=(x, y, 1 - c), device_id_type=MESH))
        for cp in cps:
            cp.start()
        for cp in cps:
            cp.wait()

    return pl.pallas_call(
        body, name=name, in_specs=[_ANY] * nt, out_specs=[_ANY] * nt,
        out_shape=[jax.ShapeDtypeStruct(b.shape, b.dtype) for b in bufs], scratch_shapes=_sem_pairs(nt),
        input_output_aliases={t: t for t in range(nt)},
    )(*bufs)


def _add_sibling(g, recv, cidx, chip_idx, *, name):
    _, _, na, rh, cdim = g.shape
    tb = _pick(rh, (256, 128, 64, 32, 16))

    def body(c_ref, k_ref, g_ref, r_ref, s_ref, o_ref):
        tot = g_ref[0, 0] + r_ref[0]
        s_ref[0] = tot.astype(BF16)

        @pl.when(pl.program_id(2) == k_ref[0])
        def _():
            o_ref[...] = tot

    return pl.pallas_call(
        body, name=name,
        grid_spec=pltpu.PrefetchScalarGridSpec(
            num_scalar_prefetch=2, grid=(na, rh // tb, N_CHIPS),
            in_specs=[pl.BlockSpec((1, 1, 1, tb, cdim), lambda a, i, k, c_ref, k_ref: (k, c_ref[0], a, i, 0)),
                      pl.BlockSpec((1, 1, tb, cdim), lambda a, i, k, c_ref, k_ref: (k, a, i, 0))],
            out_specs=[pl.BlockSpec((1, 1, tb, cdim), lambda a, i, k, c_ref, k_ref: (k, a, i, 0)),
                       pl.BlockSpec((1, tb, cdim), lambda a, i, k, c_ref, k_ref: (a, i, 0))]),
        out_shape=[jax.ShapeDtypeStruct((N_CHIPS, na, rh, cdim), BF16), jax.ShapeDtypeStruct((na, rh, cdim), F32)],
        compiler_params=_cparams((_PAR, _PAR, _ARB)),
    )(cidx, chip_idx, g, recv)


def _add_chips(own, recv, cidx, *, name):
    na, rh, cdim = own.shape
    tb = _pick(rh, (256, 128, 64, 32, 16))
    nblk = rh // tb

    def body(c_ref, a_ref, r0_ref, r1_ref, r2_ref, o_ref):
        o_ref[...] = ((a_ref[...] + r0_ref[0].astype(F32)) + r1_ref[0].astype(F32)) + r2_ref[0].astype(F32)

    slot = lambda j: pl.BlockSpec((1, 1, tb, cdim), lambda a, i, c_ref: (j, a, i, 0))
    return pl.pallas_call(
        body, name=name,
        grid_spec=pltpu.PrefetchScalarGridSpec(
            num_scalar_prefetch=1, grid=(na, nblk),
            in_specs=[pl.BlockSpec((1, tb, cdim), lambda a, i, c_ref: (a, i, 0)), slot(0), slot(1), slot(2)],
            out_specs=pl.BlockSpec((1, tb, cdim), lambda a, i, c_ref: (a, c_ref[0] * nblk + i, 0))),
        out_shape=jax.ShapeDtypeStruct((na, 2 * rh, cdim), F32), compiler_params=_cparams((_PAR, _PAR)),
    )(cidx, own, recv, recv, recv)


def _all_reduce_small(v, *, name):
    r, cdim = v.shape

    def body(v_ref, o_ref, buf, send_sems, recv_sems):
        x, y, c, _ = _place()
        me = 4 * x + 2 * y + c
        buf[me] = v_ref[...]
        cps = []
        for p in range(1, N_DEV):
            to = (1 - x if p & 4 else x, 1 - y if p & 2 else y, 1 - c if p & 1 else c)
            cp = pltpu.make_async_remote_copy(src_ref=v_ref, dst_ref=buf.at[me], send_sem=send_sems.at[p - 1],
                                              recv_sem=recv_sems.at[p - 1], device_id=to, device_id_type=MESH)
            cp.start()
            cps.append(cp)
        for p in range(1, N_DEV):
            frm = (4 * x + 2 * y + c) ^ p
            pltpu.make_async_remote_copy(src_ref=v_ref, dst_ref=buf.at[frm], send_sem=send_sems.at[p - 1],
                                         recv_sem=recv_sems.at[p - 1], device_id=(x, y, c), device_id_type=MESH).wait_recv()
        for cp in cps:
            cp.wait_send()
        tot = buf[0]
        for i in range(1, N_DEV):
            tot = tot + buf[i]
        o_ref[...] = tot

    vm = pl.BlockSpec(memory_space=pltpu.VMEM)
    return pl.pallas_call(
        body, name=name, in_specs=[vm], out_specs=vm, out_shape=jax.ShapeDtypeStruct((r, cdim), F32),
        scratch_shapes=[pltpu.VMEM((N_DEV, r, cdim), F32), pltpu.SemaphoreType.DMA((N_DEV - 1,)), pltpu.SemaphoreType.DMA((N_DEV - 1,))],
    )(v)


_SHARDED = ("even_w_in", "even_w_uq", "even_w_ukv", "even_w_out", "odd_w_in", "odd_w_out")
_COL_SHARDED = ("even_w_in", "even_w_uq", "even_w_ukv", "odd_w_in")
PACK_COLS = 1024


def _unshard(name, stacked):
    n, a, b, cc = stacked.shape
    if name in _COL_SHARDED:
        return stacked.transpose(1, 2, 0, 3).reshape(a, b, n * cc)
    return stacked.transpose(1, 0, 2, 3).reshape(a, n * b, cc)


def _chip_halves(name, full):
    a, b, cc = full.shape
    if name in _COL_SHARDED:
        return full.reshape(a, 2, b // 2, N_CHIPS, cc // N_CHIPS).transpose(3, 1, 0, 2, 4)
    return full.reshape(a, N_CHIPS, 2, b // (2 * N_CHIPS), cc).transpose(1, 2, 0, 3, 4)


def _rope_tables(s):
    pos = jnp.arange(s, dtype=F32)

    def ang(d):
        inv = ROPE_THETA ** (-jnp.arange(0, d, 2, dtype=F32) / d)
        a = pos[:, None] * inv[None, :]
        return jnp.cos(a), jnp.sin(a)

    c16, s16 = ang(MLA_ROPE)
    one, zero = jnp.ones((s, KPE_LANE), F32), jnp.zeros((s, KPE_LANE), F32)
    cos_m = jnp.concatenate([one, c16, c16, one[:, :32]], axis=1)
    sin_m = jnp.concatenate([zero, -s16, s16, zero[:, :32]], axis=1)
    c32, s32 = ang(SWA_DIM)
    cos_s = jnp.concatenate([c32, c32, c32, c32], axis=1)
    sin_s = jnp.concatenate([-s32, s32, -s32, s32], axis=1)
    return cos_m, sin_m, cos_s, sin_s


def _even_weights(w_in, w_uq, w_ukv):
    zeros = lambda n: jnp.zeros((D_MODEL, n), w_in.dtype)
    wcq, wckv, wkpe = w_in[:, 0:256], w_in[:, 256:384], w_in[:, 384:416]
    wfq, wfk, wfv = w_in[:, 416:928], w_in[:, 928:1440], w_in[:, 1440:1952]
    wfl, wg = w_in[:, 1952:1960], w_in[:, 1960:2984]
    misc = jnp.concatenate([zeros(KPE_LANE), wkpe, wfl, zeros(LANES - FL_LANE - FOX_HEADS)], axis=1)
    w_in_p = jnp.concatenate([wg, wfq, wfk, wfv, wcq, wckv, misc], axis=1)
    uq = w_uq.reshape(MLA_Q_RANK, MLA_HEADS, MLA_NOPE + MLA_ROPE)
    uq_p = jnp.pad(uq, ((0, 0), (0, 0), (0, HEAD_PAD - MLA_NOPE - MLA_ROPE))).reshape(MLA_Q_RANK, MLA_HEADS * HEAD_PAD)
    ukv = w_ukv.reshape(MLA_KV_RANK, MLA_HEADS, MLA_NOPE + MLA_V)
    uk_p = jnp.pad(ukv[..., :MLA_NOPE], ((0, 0), (0, 0), (0, HEAD_PAD - MLA_NOPE))).reshape(MLA_KV_RANK, MLA_HEADS * HEAD_PAD)
    uv = ukv[..., MLA_NOPE:].reshape(MLA_KV_RANK, MLA_HEADS * MLA_V)
    ukv_p = jnp.concatenate([uk_p, uv], axis=1)
    return w_in_p, w_in_p.T, uq_p, uq_p.T, ukv_p, ukv_p.T


def _even_weight_grads(dw_in_p, duq_p, dukv_p):
    g = dw_in_p
    gate, fq, fk, fv = g[:, E_GATE:E_FQ], g[:, E_FQ:E_FK], g[:, E_FK:E_FV], g[:, E_FV:E_SMALL]
    cq, ckv, misc = g[:, E_SMALL:E_SMALL + 256], g[:, E_SMALL + 256:E_SMALL + 384], g[:, E_SMALL + 384:]
    dw_in = jnp.concatenate([cq, ckv, misc[:, KPE_LANE:KPE_LANE + MLA_ROPE], fq, fk, fv, misc[:, FL_LANE:FL_LANE + FOX_HEADS], gate], axis=1)
    duq = duq_p.reshape(MLA_Q_RANK, MLA_HEADS, HEAD_PAD)[..., :MLA_NOPE + MLA_ROPE].reshape(MLA_Q_RANK, -1)
    nq = MLA_HEADS * HEAD_PAD
    dk = dukv_p[:, :nq].reshape(MLA_KV_RANK, MLA_HEADS, HEAD_PAD)[..., :MLA_NOPE]
    dvv = dukv_p[:, nq:].reshape(MLA_KV_RANK, MLA_HEADS, MLA_V)
    dukv = jnp.concatenate([dk, dvv], axis=-1).reshape(MLA_KV_RANK, -1)
    return dw_in, duq, dukv


def _interleave(w, forward):
    a, b = (SWA_KV_HEADS, SWA_HEADS // SWA_KV_HEADS) if forward else (SWA_HEADS // SWA_KV_HEADS, SWA_KV_HEADS)
    return w.reshape(w.shape[0], a, b, -1).transpose(0, 2, 1, 3).reshape(w.shape[0], -1)


def _odd_weights(w_in, w_out):
    q, k, v, gate = w_in[:, 0:1024], w_in[:, 1024:1152], w_in[:, 1152:1280], w_in[:, 1280:2304]
    w_p = jnp.concatenate([_interleave(gate, True), _interleave(q, True), k, v], axis=1)
    w_out_p = _interleave(w_out.T, True).T
    return w_p, w_p.T, w_out_p, w_out_p.T


def _odd_weight_grads(g, dw_out_p):
    dw_in = jnp.concatenate([_interleave(g[:, O_Q:O_K], False), g[:, O_K:O_V], g[:, O_V:], _interleave(g[:, O_GATE:O_Q], False)], axis=1)
    return dw_in, _interleave(dw_out_p.T, False).T


def kernel(x, even_w_in, even_q_norm, even_w_uq, even_kv_norm, even_w_ukv, even_b_f, even_w_out, even_ln_g, even_ln_b, odd_w_in, odd_sinks, odd_w_out, odd_ln_g, odd_ln_b, loss_target, m_even_w_in, m_even_q_norm, m_even_w_uq, m_even_kv_norm, m_even_w_ukv, m_even_b_f, m_even_w_out, m_even_ln_g, m_even_ln_b, m_odd_w_in, m_odd_sinks, m_odd_w_out, m_odd_ln_g, m_odd_ln_b, v_even_w_in, v_even_q_norm, v_even_w_uq, v_even_kv_norm, v_even_w_ukv, v_even_b_f, v_even_w_out, v_even_ln_g, v_even_ln_b, v_odd_w_in, v_odd_sinks, v_odd_w_out, v_odd_ln_g, v_odd_ln_b):
    weights = dict(even_w_in=even_w_in, even_q_norm=even_q_norm, even_w_uq=even_w_uq, even_kv_norm=even_kv_norm,
                   even_w_ukv=even_w_ukv, even_b_f=even_b_f, even_w_out=even_w_out, even_ln_g=even_ln_g, even_ln_b=even_ln_b,
                   odd_w_in=odd_w_in, odd_sinks=odd_sinks, odd_w_out=odd_w_out, odd_ln_g=odd_ln_g, odd_ln_b=odd_ln_b)
    mom_m = dict(even_w_in=m_even_w_in, even_q_norm=m_even_q_norm, even_w_uq=m_even_w_uq, even_kv_norm=m_even_kv_norm,
                 even_w_ukv=m_even_w_ukv, even_b_f=m_even_b_f, even_w_out=m_even_w_out, even_ln_g=m_even_ln_g, even_ln_b=m_even_ln_b,
                 odd_w_in=m_odd_w_in, odd_sinks=m_odd_sinks, odd_w_out=m_odd_w_out, odd_ln_g=m_odd_ln_g, odd_ln_b=m_odd_ln_b)
    mom_v = dict(even_w_in=v_even_w_in, even_q_norm=v_even_q_norm, even_w_uq=v_even_w_uq, even_kv_norm=v_even_kv_norm,
                 even_w_ukv=v_even_w_ukv, even_b_f=v_even_b_f, even_w_out=v_even_w_out, even_ln_g=v_even_ln_g, even_ln_b=v_even_ln_b,
                 odd_w_in=v_odd_w_in, odd_sinks=v_odd_sinks, odd_w_out=v_odd_w_out, odd_ln_g=v_odd_ln_g, odd_ln_b=v_odd_ln_b)
    names = list(weights)
    xl = x[0]
    tgt = loss_target[0]
    s = xl.shape[0]
    ax, ay, ac = lax.axis_index("x"), lax.axis_index("y"), lax.axis_index("c")
    chip = 2 * ax + ay
    c_idx = ac.astype(jnp.int32).reshape(1)
    chip_idx = chip.astype(jnp.int32).reshape(1)

    ln_odd = jnp.pad(jnp.concatenate([odd_ln_g, odd_ln_b]), ((0, 12), (0, 0)))[None]
    wb = {n: weights[n].astype(BF16) for n in _SHARDED}
    even_mats = [n for n in _SHARDED if n.startswith("even")]
    odd_mats = [n for n in _SHARDED if n.startswith("odd")]
    first_mats = [n for n in even_mats if n != "even_w_out"]
    gathered = _gather_chip_shards([wb[n][0:1] for n in first_mats], name="gather_first")
    full = {n: [None, None] for n in even_mats}
    for i, n in enumerate(first_mats):
        full[n][0] = _unshard(n, gathered[i])[0]
    later = [wb["even_w_out"][0:1]] + [wb[n][1:2] for n in even_mats] + [wb[n] for n in odd_mats] + [ln_odd]

    cos_m, sin_m, cos_s, sin_s = _rope_tables(s)
    bf_tiles = jnp.pad(even_b_f, ((0, 0), (FL_LANE, LANES - FL_LANE - FOX_HEADS)))
    sink_tiles = jnp.broadcast_to(_interleave(odd_sinks, True)[:, :, None, None], (2, SWA_HEADS, 1, LANES))
    mla_scale = (MLA_NOPE + MLA_ROPE) ** -0.5
    fox_scale = FOX_DIM ** -0.5
    mla_pairs, fox_pairs = MLA_HEADS // 2, FOX_HEADS // 2

    saved = []
    x_f, x_b = xl, xl.astype(BF16)
    for layer in range(DEPTH):
        j = layer // 2
        ln = f"L{layer}"
        if layer % 2 == 0:
            w_in_p, w_in_t, uq_p, uq_t, ukv_p, ukv_t = _even_weights(full["even_w_in"][j], full["even_w_uq"][j], full["even_w_ukv"][j])
            qg, kg, bft = even_q_norm[j][None], even_kv_norm[j][None], bf_tiles[j][None]
            h, hb = _mm(x_b, w_in_p, out_dtype=F32, name=ln + "_in", also_bf16=True)
            q, k, v, qn, kvn, logf = _even_mid_fwd(h, qg, kg, bft, uq_p, ukv_p, cos_m, sin_m, name=ln + "_mid")
            cum = _cumsum(logf, reverse=False, name=ln + "_cum")[:, FL_LANE:FL_LANE + FOX_HEADS].T
            crow = cum[:, None, :]
            if layer == 0:
                o_mla, lse_mla, rest = _attn_fwd((q, 0), (k, 0), (v, 0), None, dk=HEAD_PAD, npairs=mla_pairs, scale=mla_scale,
                                                 name=ln + "_mla", gather=later)
                full["even_w_out"][0] = _unshard("even_w_out", rest[0])[0]
                for i, n in enumerate(even_mats):
                    full[n][1] = _unshard(n, rest[1 + i])[0]
                for i, n in enumerate(odd_mats):
                    full[n] = _unshard(n, rest[1 + len(even_mats) + i])
                ln_all = rest[-1][:, 0]
                odd_g_full = ln_all[:, 0:2].transpose(1, 0, 2).reshape(2, D_MODEL)
                odd_b_full = ln_all[:, 2:4].transpose(1, 0, 2).reshape(2, D_MODEL)
            else:
                o_mla, lse_mla = _attn_fwd((q, 0), (k, 0), (v, 0), None, dk=HEAD_PAD, npairs=mla_pairs, scale=mla_scale, name=ln + "_mla")
            o_fox, lse_fox = _attn_fwd((hb, E_FQ), (hb, E_FK), (hb, E_FV), crow, dk=FOX_DIM, npairs=fox_pairs,
                                       scale=fox_scale, name=ln + "_fox")
            o_parts = [o_mla, o_fox]
            w_out, w_out_t = full["even_w_out"][j], full["even_w_out"][j].T
            g_ln, b_ln = even_ln_g[j][None], even_ln_b[j][None]
            x_n, x_nb, z, xh, rstd = _out_fwd(o_parts, h, x_f, w_out, g_ln, b_ln, name=ln + "_out")
            saved.append(dict(h=h, hb=hb, x_b=x_b, qn=qn, kvn=kvn, q=q, k=k, v=v, crow=crow,
                              o_mla=o_mla, o_fox=o_fox, lse_mla=lse_mla, lse_fox=lse_fox, o_parts=o_parts, z=z, xh=xh, rstd=rstd,
                              w_in_t=w_in_t, uq_t=uq_t, ukv_t=ukv_t, w_out_t=w_out_t, qg=qg, kg=kg, bft=bft, g_ln=g_ln))
        else:
            w_in_p, w_in_t, w_out, w_out_t = _odd_weights(full["odd_w_in"][j], full["odd_w_out"][j])
            h = _mm(x_b, w_in_p, out_dtype=F32, name=ln + "_in")
            q, k, v = _odd_rope(h, cos_s, sin_s, name=ln + "_rope")
            kp = jnp.pad(k, ((WINDOW, 0), (0, 0)))
            vp = jnp.pad(v, ((WINDOW, 0), (0, 0)))
            o_t, lse = _swa_fwd(q, kp, vp, sink_tiles[j], name=ln + "_swa")
            g_ln, b_ln = odd_g_full[j][None], odd_b_full[j][None]
            x_n, x_nb, z, xh, rstd = _out_fwd([o_t], h, x_f, w_out, g_ln, b_ln, name=ln + "_out", o_t=True)
            saved.append(dict(h=h, x_b=x_b, q=q, kp=kp, vp=vp, lse=lse, o_t=o_t, o_parts=[o_t], z=z, xh=xh, rstd=rstd,
                              w_in_t=w_in_t, w_out_t=w_out_t, g_ln=g_ln))
        x_f, x_b = x_n, x_nb

    dxn, loss_tile = _loss_grad(x_f, tgt, name="loss")

    def pair_sums(group, tag):
        gp = [_chip_halves(n, g) for _, n, g in group]
        from_sib = _swap_halves(gp, name="grad_swap_" + tag)
        return [_add_sibling(g, r, c_idx, chip_idx, name=f"grad_add_sibling_{lab}_{tag}")
                for (lab, _, _), g, r in zip(group, gp, from_sib)]

    def chip_sums(group, pair, from_chips, tag):
        mine = [_add_chips(p[1], r, c_idx, name=f"grad_add_chips_{lab}_{tag}") for (lab, _, _), p, r in zip(group, pair, from_chips)]
        return dict(zip([lab for lab, _, _ in group], _join_halves(mine, name="grad_join_" + tag)))

    grads = {n: [None, None] for n in names}
    for layer in reversed(range(DEPTH)):
        j = layer // 2
        ln = f"L{layer}"
        sv = saved[layer]
        dr, dy, do, dgate, dg_ln, db_ln, *do_t = _out_bwd(dxn, sv["xh"], sv["rstd"], sv["g_ln"], sv["w_out_t"], sv["o_parts"], sv["h"],
                                                          name=ln + "_outb", o_t=layer % 2 == 1)
        dw_out = _mm(sv["z"].T, dy, out_dtype=F32, name=ln + "_dwout")
        if layer % 2 == 0:
            hb = sv["hb"]
            if layer == 0:
                early = [(n, n, grads[n][1][None]) for n in even_mats]
                early += [(f"{n}_{jj}", n, grads[n][jj][None]) for n in odd_mats for jj in range(2)]
                early_pair = pair_sums(early, "early")
                dq, dk, dv, early_chips = _attn_bwd((sv["q"], 0), (sv["k"], 0), (sv["v"], 0), (sv["o_mla"], 0), (do, 0), sv["lse_mla"], None,
                                                    dk=HEAD_PAD, npairs=mla_pairs, scale=mla_scale, dq_dtype=F32, name=ln + "_mlab",
                                                    scatter=[p[0] for p in early_pair])
            else:
                dq, dk, dv = _attn_bwd((sv["q"], 0), (sv["k"], 0), (sv["v"], 0), (sv["o_mla"], 0), (do, 0), sv["lse_mla"], None,
                                       dk=HEAD_PAD, npairs=mla_pairs, scale=mla_scale, dq_dtype=F32, name=ln + "_mlab")
            dfq, dfk, dfv, dcrow, dccol = _attn_bwd((hb, E_FQ), (hb, E_FK), (hb, E_FV), (sv["o_fox"], 0), (do, MLA_HEADS * MLA_V),
                                                    sv["lse_fox"], sv["crow"], dk=FOX_DIM, npairs=fox_pairs,
                                                    scale=fox_scale, dq_dtype=BF16, name=ln + "_foxb")
            dkeys = dccol.reshape(s, fox_pairs, 2, HALF)[:, :, ::-1, 0].reshape(s, FOX_HEADS)
            dcum = jnp.pad(dcrow[:, 0, :].T + dkeys, ((0, 0), (FL_LANE, LANES - FL_LANE - FOX_HEADS)))
            dlogf = _cumsum(dcum, reverse=True, name=ln + "_cumb")
            dh_small, dq_pre, dqg, dkg, dbf = _even_mid_bwd(
                sv["h"], dq, dk, dv, dlogf, sv["qg"], sv["kg"], sv["bft"], sv["uq_t"], sv["ukv_t"], cos_m, sin_m, name=ln + "_midb")
            duq_p = _mm(sv["qn"].T, dq_pre, out_dtype=F32, name=ln + "_dwuq")
            dkv_cat = jnp.concatenate([dk.astype(BF16), dv], axis=1)
            dukv_p = _mm(sv["kvn"].T, dkv_cat, out_dtype=F32, name=ln + "_dwukv")
            dh = jnp.concatenate([dgate, dfq, dfk, dfv, dh_small], axis=1)
            dw_in_p = _mm(sv["x_b"].T, dh, out_dtype=F32, name=ln + "_dwin")
            dw_in, duq, dukv = _even_weight_grads(dw_in_p, duq_p, dukv_p)
            for n, val in (("even_w_in", dw_in), ("even_w_uq", duq), ("even_w_ukv", dukv), ("even_w_out", dw_out),
                           ("even_q_norm", dqg[0]), ("even_kv_norm", dkg[0]), ("even_b_f", dbf[0, FL_LANE:FL_LANE + FOX_HEADS]),
                           ("even_ln_g", dg_ln[0]), ("even_ln_b", db_ln[0])):
                grads[n][j] = val
        else:
            dq_t, dkp, dvp, dsink = _swa_bwd(sv["q"], sv["kp"], sv["vp"], sv["kp"].T, sink_tiles[j], sv["o_t"], do, do_t[0], sv["lse"],
                                             name=ln + "_swab")
            dq_r, dk_r = _odd_rope_bwd(dq_t, dkp[WINDOW:], cos_s, sin_s, name=ln + "_ropeb")
            dh = jnp.concatenate([dgate, dq_r, dk_r, dvp[WINDOW:].astype(BF16)], axis=1)
            dw_in_p = _mm(sv["x_b"].T, dh, out_dtype=F32, name=ln + "_dwin")
            dw_in, dw_out = _odd_weight_grads(dw_in_p, dw_out)
            for n, val in (("odd_w_in", dw_in), ("odd_w_out", dw_out), ("odd_sinks", _interleave(dsink[None, :, 0, 0], False)[0]),
                           ("odd_ln_g", dg_ln[0]), ("odd_ln_b", db_ln[0])):
                grads[n][j] = val
        dxn = _mm(dh, sv["w_in_t"], out_dtype=F32, name=ln + "_dx", res=dr, res_scale=ALPHA)
    grad_x = dxn[None]
    late = [(n, n, grads[n][0][None]) for n in even_mats]
    late_pair = pair_sums(late, "late")
    late_sum = chip_sums(late, late_pair, _scatter_to_chips([p[0] for p in late_pair], name="grad_scatter_late"), "late")
    early_sum = chip_sums(early, early_pair, early_chips, "early")
    gshard = {n: jnp.concatenate([late_sum[n], early_sum[n]]) for n in even_mats}
    gshard.update({n: jnp.concatenate([early_sum[n + "_0"], early_sum[n + "_1"]]) for n in odd_mats})

    small = [n for n in names if n not in _SHARDED]
    grads = {n: jnp.stack(grads[n]) for n in small}
    sv_flat = jnp.concatenate([grads[n].reshape(-1) for n in small] + [loss_tile[0, :1]])
    sv_real = sv_flat.shape[0]
    sv_rows = -(-sv_real // (PACK_COLS * 8)) * 8
    sv_sum = _all_reduce_small(jnp.pad(sv_flat, (0, sv_rows * PACK_COLS - sv_real)).reshape(sv_rows, PACK_COLS), name="small_all_reduce").reshape(-1)
    off = 0
    for n in small:
        size = int(np.prod(grads[n].shape))
        gfull = sv_sum[off:off + size].reshape(grads[n].shape)
        off += size
        if n in ("odd_ln_g", "odd_ln_b"):
            gfull = lax.dynamic_slice_in_dim(gfull, chip * (D_MODEL // N_CHIPS), D_MODEL // N_CHIPS, axis=1)
        gshard[n] = gfull
    loss = sv_sum[off]

    deltas, new_m, new_v = {}, {}, {}
    for n in names:
        deltas[n], new_m[n], new_v[n] = _adamw(weights[n], gshard[n], mom_m[n], mom_v[n], name="adamw_" + n)
    return (loss, grad_x, *[gshard[n] for n in names], *[deltas[n] for n in names],
            *[new_m[n] for n in names], *[new_v[n] for n in names])
```
